```python
import math
import functools
import jax
import jax.numpy as jnp
from jax import lax
import numpy as np

D_MODEL = 1024
BATCH = 8
SEQ = 2048
DEPTH = 1
DEC_BATCH = 128
DEC_SEQ = 4
PAST_LEN = 16384
PAGE_SIZE = 128

ML_HEADS = 4
ML_DK = D_MODEL // 8
ML_DV = D_MODEL // 8
ML_WIDTH = ML_HEADS * ML_DV
ML_CHUNK = 64
GATE_SOFTCAP = 15.0
SWA_HEADS = 8
SWA_KV_HEADS = 2
SWA_HD = D_MODEL // 16
SWA_GROUP = SWA_HEADS // SWA_KV_HEADS
SWA_WIDTH = SWA_HEADS * SWA_HD
WINDOW = 128
REL_BUCKETS = 32
REL_MAX_DIST = 128
MIX_WIDTH = ML_WIDTH + SWA_WIDTH
D_FF = -(-8 * D_MODEL // (3 * 256)) * 256
ALPHA = (2.0 * DEPTH) ** 0.25
BETA = (8.0 * DEPTH) ** -0.25
LN_EPS = 1e-5
NORM_EPS = 1e-6
NEG_INF = -1e30
IN_SIZES = (ML_HEADS * ML_DK, ML_HEADS * ML_DK, ML_WIDTH, ML_WIDTH, ML_HEADS, ML_HEADS,
            SWA_WIDTH, SWA_KV_HEADS * SWA_HD, SWA_KV_HEADS * SWA_HD)
VALUE_COL_GROUPS = (2, 8)
IN_COLS = sum(IN_SIZES)

kernel_name = "hymba_mlstm_swa_sink_step"


def layer_norm(x, g, b):
    xf = x.astype(jnp.float32)
    mu = jnp.mean(xf, -1, keepdims=True)
    var = jnp.mean(jnp.square(xf - mu), -1, keepdims=True)
    return ((xf - mu) * lax.rsqrt(var + LN_EPS) * g + b).astype(x.dtype)


def softcap(a):
    return GATE_SOFTCAP * jnp.tanh(a / GATE_SOFTCAP)


def split_in_proj(proj):
    offs = np.cumsum(IN_SIZES)[:-1].tolist()
    return jnp.split(proj, offs, axis=-1)


def t5_bucket(dist):
    n = jnp.maximum(dist, 0)
    max_exact = REL_BUCKETS // 2
    nf = jnp.maximum(n, 1).astype(jnp.float32)
    large = max_exact + (jnp.log(nf / max_exact) / math.log(REL_MAX_DIST / max_exact)
                         * (REL_BUCKETS - max_exact)).astype(jnp.int32)
    large = jnp.minimum(large, REL_BUCKETS - 1)
    return jnp.where(n < max_exact, n, large)


def rel_pos_bias(dist, rel_bias):
    bias = rel_bias.astype(jnp.float32)[t5_bucket(dist)]
    return jnp.moveaxis(bias, -1, 0).reshape(SWA_KV_HEADS, SWA_GROUP, *dist.shape)


def sink_softmax(s, valid, sinks):
    snk = sinks.astype(jnp.float32).reshape(SWA_KV_HEADS, SWA_GROUP, 1, 1)
    s = jnp.where(valid, s, NEG_INF)
    mx = jnp.maximum(jnp.max(s, -1, keepdims=True), snk)
    p = jnp.exp(s - mx)
    return p / (jnp.sum(p, -1, keepdims=True) + jnp.exp(snk - mx))


def swa_prompt(q, k, v, sinks, rel_bias):
    B, S = q.shape[:2]
    nb = S // WINDOW
    f32 = jnp.float32
    qb = q.reshape(B, nb, WINDOW, SWA_KV_HEADS, SWA_GROUP, SWA_HD).astype(f32)

    def band(a):
        a = a.reshape(B, nb, WINDOW, SWA_KV_HEADS, SWA_HD).astype(f32)
        prev = jnp.concatenate([jnp.zeros_like(a[:, :1]), a[:, :-1]], axis=1)
        return jnp.concatenate([prev, a], axis=2)

    kk, vv = band(k), band(v)
    qi = jnp.arange(WINDOW)[:, None]
    kj = jnp.arange(2 * WINDOW)[None, :]
    dist = WINDOW + qi - kj
    kpos = jnp.arange(nb)[:, None] * WINDOW - WINDOW + jnp.arange(2 * WINDOW)[None, :]
    valid = ((dist >= 0) & (dist < WINDOW))[None] & (kpos >= 0)[:, None, :]
    s = jnp.einsum('bnqhgd,bnkhd->bnhgqk', qb, kk) * (SWA_HD ** -0.5) + rel_pos_bias(dist, rel_bias)
    p = sink_softmax(s, valid[None, :, None, None], sinks)
    o = jnp.einsum('bnhgqk,bnkhd->bnqhgd', p, vv).reshape(B, S, SWA_WIDTH).astype(q.dtype)
    wb = min(WINDOW, S)
    return o, k[:, S - wb:], v[:, S - wb:]


def swa_sample(q, k, v, sinks, rel_bias, kbuf, vbuf):
    B, T = q.shape[:2]
    wb = kbuf.shape[1]
    f32 = jnp.float32
    kk = jnp.concatenate([kbuf.astype(k.dtype), k], axis=1)
    vv = jnp.concatenate([vbuf.astype(v.dtype), v], axis=1)
    qpos = PAST_LEN + jnp.arange(T)
    kpos = PAST_LEN - wb + jnp.arange(wb + T)
    dist = qpos[:, None] - kpos[None, :]
    valid = (dist >= 0) & (dist < WINDOW)
    qh = q.reshape(B, T, SWA_KV_HEADS, SWA_GROUP, SWA_HD).astype(f32)
    s = jnp.einsum('bqhgd,bkhd->bhgqk', qh, kk.astype(f32)) * (SWA_HD ** -0.5) + rel_pos_bias(dist, rel_bias)
    p = sink_softmax(s, valid, sinks)
    o = jnp.einsum('bhgqk,bkhd->bqhgd', p, vv.astype(f32)).reshape(B, T, SWA_WIDTH).astype(q.dtype)
    return o, kk[:, T:], vv[:, T:]


def mlstm_chunkwise(q, k, v, i_pre, f_pre, C0, n0, m0):
    B, NH, L, _ = q.shape
    lc = ML_CHUNK if L % ML_CHUNK == 0 else L
    nc = L // lc

    def to_chunks(a):
        return jnp.moveaxis(a.reshape(B, NH, nc, lc, *a.shape[3:]), 2, 0)

    logf = jax.nn.log_sigmoid(f_pre)
    xs = (to_chunks(q), to_chunks(k), to_chunks(v), to_chunks(i_pre), to_chunks(logf))
    causal = jnp.tril(jnp.ones((lc, lc), bool))

    def step(carry, xc):
        C, n, m = carry
        qc, kc, vc, ic, fc = xc
        b = jnp.cumsum(fc, axis=-1)
        logD = jnp.where(causal, b[..., :, None] - b[..., None, :] + ic[..., None, :], -jnp.inf)
        inter = b + m[..., None]
        m_row = jnp.maximum(inter, jnp.max(logD, -1))
        sc = jnp.exp(inter - m_row)
        qk = jnp.einsum('bhtd,bhsd->bhts', qc, kc) * jnp.exp(logD - m_row[..., None])
        num = sc[..., None] * jnp.einsum('bhtd,bhde->bhte', qc, C) + jnp.einsum('bhts,bhse->bhte', qk, vc)
        den = sc * jnp.einsum('bhtd,bhd->bht', qc, n) + jnp.sum(qk, -1)
        h = num / jnp.maximum(jnp.abs(den), jnp.exp(-m_row))[..., None]
        bL = b[..., -1]
        w = bL[..., None] - b + ic
        m_new = jnp.maximum(bL + m, jnp.max(w, -1))
        a = jnp.exp(bL + m - m_new)
        kw = kc * jnp.exp(w - m_new[..., None])[..., None]
        C_new = a[..., None, None] * C + jnp.einsum('bhsd,bhse->bhde', kw, vc)
        n_new = a[..., None] * n + jnp.sum(kw, axis=2)
        return (C_new, n_new, m_new), h

    (C, n, m), hs = lax.scan(step, (C0, n0, m0), xs)
    h = jnp.moveaxis(hs, 0, 2).reshape(B, NH, L, -1)
    return h, C, n, m


def token_mixers(x, ml_state, swa_fn, w_in, b_ig, b_fg, ml_g, sinks, rel_bias):
    B, L, _ = x.shape
    f32 = jnp.float32
    mq, mk, mv, mo, mi, mf, sq, sk, sv = split_in_proj(x @ w_in)

    def heads(a):
        return a.reshape(B, L, ML_HEADS, -1).transpose(0, 2, 1, 3).astype(f32)

    q, k, v = heads(mq), heads(mk) * (ML_DK ** -0.5), heads(mv)
    i_pre = softcap(mi.astype(f32) + b_ig).transpose(0, 2, 1)
    f_pre = softcap(mf.astype(f32) + b_fg).transpose(0, 2, 1)
    h, C, n, m = mlstm_chunkwise(q, k, v, i_pre, f_pre, *ml_state)
    h = h.transpose(0, 2, 1, 3)
    h = h * lax.rsqrt(jnp.mean(h * h, -1, keepdims=True) + NORM_EPS)
    h_ml = (h.reshape(B, L, ML_WIDTH) * ml_g * jax.nn.sigmoid(mo.astype(f32))).astype(x.dtype)
    h_swa, kb, vb = swa_fn(sq.reshape(B, L, SWA_HEADS, SWA_HD),
                           sk.reshape(B, L, SWA_KV_HEADS, SWA_HD),
                           sv.reshape(B, L, SWA_KV_HEADS, SWA_HD), sinks, rel_bias)
    return jnp.concatenate([h_ml, h_swa], axis=-1), (C, n, m, kb, vb)


def decoder_layer(x, ml_state, swa_fn, w_in, b_ig, b_fg, ml_g, sinks, rel_bias,
                  w_out, ln1_g, ln1_b, w_ffn_in, w_ffn_out, ln2_g, ln2_b):
    mix, state = token_mixers(x, ml_state, swa_fn, w_in, b_ig, b_fg, ml_g, sinks, rel_bias)
    x = layer_norm(ALPHA * x + mix @ w_out, ln1_g, ln1_b)
    gate, up = jnp.split(x @ w_ffn_in, 2, axis=-1)
    x = layer_norm(ALPHA * x + (jax.nn.silu(gate) * up) @ w_ffn_out, ln2_g, ln2_b)
    return x, state


def setup_inputs(seed: int = 0) -> dict:
    key = jax.random.key(seed)
    ks = jax.random.split(key, 20)
    f32 = jnp.float32

    def nrm(k, shape, s):
        return jax.random.normal(k, shape, f32) * s

    wb = min(WINDOW, PAST_LEN)
    col_scale = jnp.concatenate([jnp.full((n,), BETA if i in VALUE_COL_GROUPS else 1.0, f32)
                                 for i, n in enumerate(IN_SIZES)])
    return {
        "x_prompt": nrm(ks[0], (BATCH, SEQ, D_MODEL), 1.0),
        "x_sample": nrm(ks[1], (DEC_BATCH, DEC_SEQ, D_MODEL), 1.0),
        "state_mlstm_C": nrm(ks[2], (DEPTH, DEC_BATCH, ML_HEADS, ML_DK, ML_DV), 0.05),
        "state_mlstm_n": nrm(ks[3], (DEPTH, DEC_BATCH, ML_HEADS, ML_DK), 0.1),
        "state_mlstm_m": nrm(ks[4], (DEPTH, DEC_BATCH, ML_HEADS), 1.0),
        "cache_swa_k": nrm(ks[5], (DEPTH, DEC_BATCH, wb, SWA_KV_HEADS, SWA_HD), 1.0),
        "cache_swa_v": nrm(ks[6], (DEPTH, DEC_BATCH, wb, SWA_KV_HEADS, SWA_HD), BETA),
        "w_in": nrm(ks[7], (DEPTH, D_MODEL, IN_COLS), D_MODEL ** -0.5) * col_scale,
        "b_igate": nrm(ks[8], (DEPTH, ML_HEADS), 0.1),
        "b_fgate": jnp.linspace(3.0, 6.0, ML_HEADS, dtype=f32)[None] + nrm(ks[9], (DEPTH, ML_HEADS), 0.1),
        "ml_norm_g": 1.0 + nrm(ks[10], (DEPTH, ML_WIDTH), 0.02),
        "swa_sinks": nrm(ks[11], (DEPTH, SWA_HEADS), 0.5),
        "rel_bias": nrm(ks[12], (REL_BUCKETS, SWA_HEADS), 0.1),
        "w_out": nrm(ks[13], (DEPTH, MIX_WIDTH, D_MODEL), BETA * MIX_WIDTH ** -0.5),
        "ln1_g": 1.0 + nrm(ks[14], (DEPTH, D_MODEL), 0.02),
        "ln1_b": nrm(ks[15], (DEPTH, D_MODEL), 0.02),
        "w_ffn_in": nrm(ks[16], (DEPTH, D_MODEL, 2 * D_FF), BETA * D_MODEL ** -0.5),
        "w_ffn_out": nrm(ks[17], (DEPTH, D_FF, D_MODEL), BETA * D_FF ** -0.5),
        "ln2_g": 1.0 + nrm(ks[18], (DEPTH, D_MODEL), 0.02),
        "ln2_b": nrm(ks[19], (DEPTH, D_MODEL), 0.02),
    }


def reference(x_prompt, x_sample, state_mlstm_C, state_mlstm_n, state_mlstm_m, cache_swa_k, cache_swa_v,
              w_in, b_igate, b_fgate, ml_norm_g, swa_sinks, rel_bias, w_out, ln1_g, ln1_b,
              w_ffn_in, w_ffn_out, ln2_g, ln2_b):
    f32 = jnp.float32
    bp = x_prompt.shape[0]
    yp, ys = x_prompt, x_sample
    new_p = ([], [], [], [], [])
    new_s = ([], [], [], [], [])
    for l in range(DEPTH):
        lw = (w_in[l], b_igate[l], b_fgate[l], ml_norm_g[l], swa_sinks[l], rel_bias,
              w_out[l], ln1_g[l], ln1_b[l], w_ffn_in[l], w_ffn_out[l], ln2_g[l], ln2_b[l])
        zero_state = (jnp.zeros((bp, ML_HEADS, ML_DK, ML_DV), f32),
                      jnp.zeros((bp, ML_HEADS, ML_DK), f32),
                      jnp.zeros((bp, ML_HEADS), f32))
        yp, st_p = decoder_layer(yp, zero_state, swa_prompt, *lw)
        past_state = (state_mlstm_C[l].astype(f32), state_mlstm_n[l].astype(f32), state_mlstm_m[l].astype(f32))
        swa_fn = functools.partial(swa_sample, kbuf=cache_swa_k[l], vbuf=cache_swa_v[l])
        ys, st_s = decoder_layer(ys, past_state, swa_fn, *lw)
        for lst, a in zip(new_p, st_p):
            lst.append(a)
        for lst, a in zip(new_s, st_s):
            lst.append(a)
    dts = (state_mlstm_C.dtype, state_mlstm_n.dtype, state_mlstm_m.dtype, cache_swa_k.dtype, cache_swa_v.dtype)
    p_C, p_n, p_m, p_k, p_v = [jnp.stack(lst).astype(dt) for lst, dt in zip(new_p, dts)]
    s_C, s_n, s_m, s_k, s_v = [jnp.stack(lst).astype(dt) for lst, dt in zip(new_s, dts)]
    return (yp, ys, p_C, p_n, p_m, p_k, p_v, s_C, s_n, s_m, s_k, s_v)
```

```python
import functools
import math

import jax
import jax.numpy as jnp
from jax import lax
from jax.experimental import pallas as pl
from jax.experimental.pallas import tpu as pltpu

F32 = jnp.float32
BF16 = jnp.bfloat16

D_MODEL = 1024
ML_HEADS = 4
ML_DK = 128
ML_DV = 128
ML_WIDTH = ML_HEADS * ML_DV
GATE_SOFTCAP = 15.0
SWA_HEADS = 8
SWA_KV_HEADS = 2
SWA_GROUP = SWA_HEADS // SWA_KV_HEADS
SWA_HD = 64
SWA_WIDTH = SWA_HEADS * SWA_HD
WINDOW = 128
REL_BUCKETS = 32
REL_MAX_DIST = 128
D_FF = 2816
DEPTH = 1
ALPHA = (2.0 * DEPTH) ** 0.25
LN_EPS = 1e-5
NORM_EPS = 1e-6
NEG_INF = -1e30
IN_SIZES = (512, 512, 512, 512, 4, 4, 512, 128, 128)
ML_SCALE = ML_DK ** -0.5
SWA_SCALE = SWA_HD ** -0.5

A_Q, A_K, A_V, A_SQ = 0, 512, 1024, 1536
A_COLS = 2048
B_MO, B_SK, B_SV, B_G = 0, 512, 640, 768
B_COLS = 896

BLK = 128
TM_PROMPT = 512
TM_DENSE = 512
FF_CHUNKS = 2
SAMPLE_GB = 16
SLAB = 16
VMEM_LIMIT = 56 * 1024 * 1024


def _softcap(a):
    return GATE_SOFTCAP * jnp.tanh(a / GATE_SOFTCAP)


def _log_sigmoid(x):
    return jnp.minimum(x, 0.0) - jnp.log1p(jnp.exp(-jnp.abs(x)))


def _layer_norm(z, g, b):
    mu = jnp.mean(z, axis=-1, keepdims=True)
    zc = z - mu
    var = jnp.mean(zc * zc, axis=-1, keepdims=True)
    return zc * lax.rsqrt(var + LN_EPS) * g + b


def _dot(a, b):
    return jnp.dot(a, b, preferred_element_type=F32)


def _dot_nt(a, b):
    return lax.dot_general(a, b, (((1,), (1,)), ((), ())), preferred_element_type=F32)


def _dot_tn(a, b):
    return lax.dot_general(a, b, (((0,), (0,)), ((), ())), preferred_element_type=F32)


def _split3(x):
    hi = x.astype(BF16)
    r1 = x - hi.astype(F32)
    mid = r1.astype(BF16)
    lo = (r1 - mid.astype(F32)).astype(BF16)
    return hi, mid, lo


def _t5_bucket(d):
    n = jnp.maximum(d, 0)
    max_exact = REL_BUCKETS // 2
    nf = jnp.maximum(n, 1).astype(F32)
    large = max_exact + (jnp.log(nf / max_exact) / math.log(REL_MAX_DIST / max_exact)
                         * (REL_BUCKETS - max_exact)).astype(jnp.int32)
    large = jnp.minimum(large, REL_BUCKETS - 1)
    return jnp.where(n < max_exact, n, large)


def _prompt_mixer_kernel(x_ref, wa_ref, wb_ref, gb_ref, mlg_ref, sinks_ref, rb_ref,
                         mix_ref, cn_out, m_out, ko_ref, vo_ref,
                         pbf, pf, kvb, cn, msc, tbl):
    b = pl.program_id(0)
    t = pl.program_id(1)
    nt = pl.num_programs(1)
    tm = x_ref.shape[1]

    @pl.when((b == 0) & (t == 0))
    def _build_bias_tables():
        qi = lax.broadcasted_iota(jnp.int32, (BLK, 2 * BLK), 0)
        kj = lax.broadcasted_iota(jnp.int32, (BLK, 2 * BLK), 1)
        d = WINDOW + qi - kj
        bucket = _t5_bucket(d)
        valid = (d >= 0) & (d < WINDOW)
        valid_first = valid & (kj >= WINDOW)
        for h in range(SWA_HEADS):
            acc = jnp.zeros((BLK, 2 * BLK), F32)
            for k in range(REL_BUCKETS):
                acc = jnp.where(bucket == k, rb_ref[k, h], acc)
            tbl[0, h] = jnp.where(valid_first, acc, NEG_INF)
            tbl[1, h] = jnp.where(valid, acc, NEG_INF)

    @pl.when(t == 0)
    def _reset_state():
        cn[...] = jnp.zeros_like(cn)
        msc[...] = jnp.zeros_like(msc)
        kvb[0:BLK, :] = jnp.zeros((BLK, kvb.shape[1]), kvb.dtype)

    xb = x_ref[0].astype(BF16)
    pbf[...] = _dot(xb, wa_ref[...]).astype(BF16)
    pf[...] = _dot(xb, wb_ref[...])
    kvb[BLK:BLK + tm, :] = pf[:, B_SK:B_SK + 2 * BLK].astype(BF16)

    row = lax.broadcasted_iota(jnp.int32, (BLK, BLK), 0)
    col = lax.broadcasted_iota(jnp.int32, (BLK, BLK), 1)
    causal = row >= col
    tri = causal.astype(BF16)
    ones_col = (col == 0).astype(BF16)
    gbias = gb_ref[...]
    mlg = mlg_ref[...]

    def block(j, carry):
        r0 = pl.multiple_of(j * BLK, BLK)
        rows = pl.ds(r0, BLK)

        pre = _softcap(pf[rows, B_G:B_G + BLK] + gbias)
        logf = _log_sigmoid(pre)
        hi, mid, lo = _split3(logf)
        bcum = _dot(tri, hi) + _dot(tri, mid) + _dot(tri, lo)
        bcum_t = bcum.T
        pre_t = pre.T

        for h in range(ML_HEADS):
            hs = slice(h * ML_DK, (h + 1) * ML_DK)
            bc = bcum[:, ML_HEADS + h:ML_HEADS + h + 1]
            br = bcum_t[ML_HEADS + h:ML_HEADS + h + 1, :]
            ir = pre_t[h:h + 1, :]
            ic = pre[:, h:h + 1]
            m_prev = msc[h:h + 1, 0:1]

            log_d = jnp.where(causal, bc - br + ir, -jnp.inf)
            inter = bc + m_prev
            m_row = jnp.maximum(inter, jnp.max(log_d, axis=1, keepdims=True))
            scl = jnp.exp(inter - m_row)
            dm = jnp.exp(log_d - m_row)

            q = pbf[rows, A_Q + h * ML_DK:A_Q + (h + 1) * ML_DK]
            k = pbf[rows, A_K + h * ML_DK:A_K + (h + 1) * ML_DK]
            v = pbf[rows, A_V + h * ML_DV:A_V + (h + 1) * ML_DV]
            vext = jnp.concatenate([v, ones_col], axis=1)

            s = _dot_nt(q, k) * ML_SCALE * dm
            c_old = cn[h]
            nd = scl * _dot(q, c_old.astype(BF16)) + _dot(s.astype(BF16), vext)
            num = nd[:, :ML_DV]
            den = nd[:, ML_DV:ML_DV + 1]
            hh = num / jnp.maximum(jnp.abs(den), jnp.exp(-m_row))
            hn = hh * lax.rsqrt(jnp.mean(hh * hh, axis=-1, keepdims=True) + NORM_EPS)
            og = jax.nn.sigmoid(pf[rows, B_MO + h * ML_DV:B_MO + (h + 1) * ML_DV])
            mix_ref[0, rows, hs] = (hn * mlg[:, hs] * og).astype(mix_ref.dtype)

            b_last = bc[BLK - 1:BLK, :]
            wc = b_last - bc + ic
            m_new = jnp.maximum(b_last + m_prev, jnp.max(wc, axis=0, keepdims=True))
            a = jnp.exp(b_last + m_prev - m_new)
            kw = (k.astype(F32) * (jnp.exp(wc - m_new) * ML_SCALE)).astype(BF16)
            cn[h] = a * c_old + _dot_tn(kw, vext)
            msc[h:h + 1, :] = jnp.broadcast_to(m_new, (1, msc.shape[1]))

        sel = jnp.where((t == 0) & (j == 0), 0, 1)
        band = pl.ds(r0, 2 * BLK)
        outs = []
        for g in range(SWA_KV_HEADS):
            kb = kvb[band, g * SWA_HD:(g + 1) * SWA_HD]
            vb = kvb[band, BLK + g * SWA_HD:BLK + (g + 1) * SWA_HD]
            for hl in range(SWA_GROUP):
                hq = g * SWA_GROUP + hl
                qh = pbf[rows, A_SQ + hq * SWA_HD:A_SQ + (hq + 1) * SWA_HD]
                sc = _dot_nt(qh, kb) * SWA_SCALE + tbl[sel, hq]
                snk = sinks_ref[hq]
                mx = jnp.maximum(jnp.max(sc, axis=-1, keepdims=True), snk)
                p = jnp.exp(sc - mx)
                den = jnp.sum(p, axis=-1, keepdims=True) + jnp.exp(snk - mx)
                outs.append(_dot(p.astype(BF16), vb) / den)
        mix_ref[0, rows, ML_WIDTH:ML_WIDTH + SWA_WIDTH] = (
            jnp.concatenate(outs, axis=1).astype(mix_ref.dtype))
        return carry

    lax.fori_loop(0, tm // BLK, block, 0)

    kvb[0:BLK, :] = kvb[tm:tm + BLK, :]

    @pl.when(t == nt - 1)
    def _write_state():
        cn_out[0] = cn[...]
        m_out[0] = msc[...]
        ko_ref[0] = pf[tm - BLK:tm, B_SK:B_SK + BLK]
        vo_ref[0] = pf[tm - BLK:tm, B_SV:B_SV + BLK]


def _prompt_mixer(x, wa, wb, gb, mlg, sinks, rb):
    bsz, seq, _ = x.shape
    tm = TM_PROMPT
    nt = seq // tm
    const2 = lambda b, t: (0, 0)
    smem = pl.BlockSpec(memory_space=pltpu.SMEM)
    return pl.pallas_call(
        _prompt_mixer_kernel,
        grid=(bsz, nt),
        in_specs=[
            pl.BlockSpec((1, tm, D_MODEL), lambda b, t: (b, t, 0)),
            pl.BlockSpec((D_MODEL, A_COLS), const2),
            pl.BlockSpec((D_MODEL, B_COLS), const2),
            pl.BlockSpec((1, BLK), const2),
            pl.BlockSpec((1, ML_WIDTH), const2),
            smem, smem,
        ],
        out_specs=[
            pl.BlockSpec((1, tm, D_MODEL), lambda b, t: (b, t, 0)),
            pl.BlockSpec((1, ML_HEADS, ML_DK, 2 * ML_DV), lambda b, t: (b, 0, 0, 0)),
            pl.BlockSpec((1, 8, BLK), lambda b, t: (b, 0, 0)),
            pl.BlockSpec((1, BLK, BLK), lambda b, t: (b, 0, 0)),
            pl.BlockSpec((1, BLK, BLK), lambda b, t: (b, 0, 0)),
        ],
        out_shape=[
            jax.ShapeDtypeStruct((bsz, seq, D_MODEL), BF16),
            jax.ShapeDtypeStruct((bsz, ML_HEADS, ML_DK, 2 * ML_DV), F32),
            jax.ShapeDtypeStruct((bsz, 8, BLK), F32),
            jax.ShapeDtypeStruct((bsz, BLK, BLK), F32),
            jax.ShapeDtypeStruct((bsz, BLK, BLK), F32),
        ],
        scratch_shapes=[
            pltpu.VMEM((tm, A_COLS), BF16),
            pltpu.VMEM((tm, B_COLS), F32),
            pltpu.VMEM((tm + BLK, 2 * BLK), BF16),
            pltpu.VMEM((ML_HEADS, ML_DK, 2 * ML_DV), F32),
            pltpu.VMEM((8, BLK), F32),
            pltpu.VMEM((2, SWA_HEADS, BLK, 2 * BLK), F32),
        ],
        compiler_params=pltpu.CompilerParams(
            dimension_semantics=("arbitrary", "arbitrary"),
            vmem_limit_bytes=VMEM_LIMIT),
        name="prompt_mixer",
    )(x, wa, wb, gb, mlg, sinks, rb)


def _roll_rows(x, shift):
    return pltpu.roll(x, shift % x.shape[0], 0)


def _sample_mixer_kernel(x_ref, wa_ref, wb_ref, gb_ref, mlg_ref, sinks_ref, rb_ref, mrep_ref,
                         c_ref, n_ref, kc_ref, vc_ref,
                         mix_ref, c_out, n_out, mo_ref, ko_ref, vo_ref,
                         pa, pf, tblc, tbln):
    step = pl.program_id(0)
    rows_total = x_ref.shape[0]
    tdec = 4
    nslab = rows_total // SLAB
    qrows = SWA_GROUP * SLAB

    @pl.when(step == 0)
    def _build_bias_tables():
        r = lax.broadcasted_iota(jnp.int32, (qrows, BLK), 0)
        c = lax.broadcasted_iota(jnp.int32, (qrows, BLK), 1)
        hl = r // SLAB
        bl = (r % SLAB) // tdec
        tq = r % tdec
        d_cache = WINDOW + tq - c
        d_new = tq - (c % tdec)
        bk_cache = _t5_bucket(d_cache)
        bk_new = _t5_bucket(d_new)
        ok_cache = c > tq
        ok_new = (c < SLAB) & ((c // tdec) == bl) & (d_new >= 0)
        for g in range(SWA_KV_HEADS):
            acc_c = jnp.zeros((qrows, BLK), F32)
            acc_n = jnp.zeros((qrows, BLK), F32)
            for k in range(REL_BUCKETS):
                for hh in range(SWA_GROUP):
                    val = rb_ref[k, g * SWA_GROUP + hh]
                    acc_c = jnp.where((bk_cache == k) & (hl == hh), val, acc_c)
                    acc_n = jnp.where((bk_new == k) & (hl == hh), val, acc_n)
            tblc[g] = jnp.where(ok_cache, acc_c, NEG_INF)
            tbln[g] = jnp.where(ok_new, acc_n, NEG_INF)

    xb = x_ref[...].astype(BF16)
    pa[...] = _dot(xb, wa_ref[...])
    pf[...] = _dot(xb, wb_ref[...])

    gbias = gb_ref[...]
    mlg = mlg_ref[...]
    r16 = lax.broadcasted_iota(jnp.int32, (SLAB, BLK), 0)
    rr = r16 % tdec
    bl16 = r16 // tdec
    r64 = lax.broadcasted_iota(jnp.int32, (qrows, BLK), 0)
    bl64 = (r64 % SLAB) // tdec
    zeros_pad = jnp.zeros((BLK - SLAB, SWA_HD), BF16)

    def seg_last(x):
        return jnp.where(rr == 3, x,
                         jnp.where(rr == 2, _roll_rows(x, -1),
                                   jnp.where(rr == 1, _roll_rows(x, -2), _roll_rows(x, -3))))

    def seg_max(x):
        m1 = jnp.maximum(x, jnp.where(rr % 2 == 0, _roll_rows(x, -1), _roll_rows(x, 1)))
        return jnp.maximum(m1, jnp.where(rr < 2, _roll_rows(m1, -2), _roll_rows(m1, 2)))

    def slab(si, carry):
        r0 = pl.multiple_of(si * SLAB, SLAB)
        rows = pl.ds(r0, SLAB)
        seq0 = si * (SLAB // tdec)

        pre = _softcap(pf[rows, B_G:B_G + BLK] + gbias)
        logf = _log_sigmoid(pre)
        y = logf + jnp.where(rr >= 1, _roll_rows(logf, 1), 0.0)
        bcum = y + jnp.where(rr >= 2, _roll_rows(y, 2), 0.0)
        ig = pltpu.roll(pre, ML_HEADS, 1)
        m_prev = mrep_ref[rows, :]
        b_minus_i = bcum - ig
        log_d = [jnp.where(rr >= dl, bcum - _roll_rows(b_minus_i, dl), -jnp.inf) for dl in range(tdec)]
        rowmax = functools.reduce(jnp.maximum, log_d)
        inter = bcum + m_prev
        m_row = jnp.maximum(inter, rowmax)
        scl = jnp.exp(inter - m_row)
        dm = [jnp.exp(ld - m_row) for ld in log_d]
        enm = jnp.exp(-m_row)
        b_last = seg_last(bcum)
        w = b_last - bcum + ig
        m_new = jnp.maximum(b_last + m_prev, seg_max(w))
        a = jnp.exp(b_last + m_prev - m_new)
        kws = jnp.exp(w - m_new) * ML_SCALE
        mo_ref[rows, :] = m_new

        for h in range(ML_HEADS):
            hs = slice(h * ML_DK, (h + 1) * ML_DK)
            lane = ML_HEADS + h
            q = pa[rows, A_Q + h * ML_DK:A_Q + (h + 1) * ML_DK]
            k = pa[rows, A_K + h * ML_DK:A_K + (h + 1) * ML_DK]
            v = pa[rows, A_V + h * ML_DV:A_V + (h + 1) * ML_DV]
            bm = jnp.zeros((SLAB, ML_DV), F32)
            sum_s = jnp.zeros((SLAB, 1), F32)
            for dl in range(tdec):
                s_dl = jnp.sum(q * _roll_rows(k, dl), axis=-1, keepdims=True)
                ws = s_dl * ML_SCALE * dm[dl][:, lane:lane + 1]
                bm = bm + ws * _roll_rows(v, dl)
                sum_s = sum_s + ws
            qc = jnp.zeros((SLAB, ML_DV), F32)
            nsel = jnp.zeros((SLAB, ML_DK), F32)
            for bl in range(SLAB // tdec):
                msk = bl16 == bl
                c_b = c_ref[seq0 + bl, h]
                n_b = n_ref[seq0 + bl][h:h + 1, :]
                qc = qc + _dot(jnp.where(msk, q, 0.0).astype(BF16), c_b.astype(BF16))
                nsel = jnp.where(msk, n_b, nsel)
                a_b = a[bl * tdec:bl * tdec + 1, lane:lane + 1]
                kw = jnp.where(msk, k * kws[:, lane:lane + 1], 0.0)
                c_out[seq0 + bl, h] = a_b * c_b + _dot_tn(kw.astype(BF16), v.astype(BF16))
                n_out[seq0 + bl, h:h + 1, :] = a_b * n_b + jnp.sum(kw, axis=0, keepdims=True)
            sclh = scl[:, lane:lane + 1]
            num = sclh * qc + bm
            den = sclh * jnp.sum(q * nsel, axis=-1, keepdims=True) + sum_s
            hh = num / jnp.maximum(jnp.abs(den), enm[:, lane:lane + 1])
            hn = hh * lax.rsqrt(jnp.mean(hh * hh, axis=-1, keepdims=True) + NORM_EPS)
            og = jax.nn.sigmoid(pf[rows, B_MO + h * ML_DV:B_MO + (h + 1) * ML_DV])
            mix_ref[rows, hs] = (hn * mlg[:, hs] * og).astype(mix_ref.dtype)

        knew = pf[rows, B_SK:B_SK + BLK]
        vnew = pf[rows, B_SV:B_SV + BLK]
        for g in range(SWA_KV_HEADS):
            gs = slice(g * SWA_HD, (g + 1) * SWA_HD)
            qg = jnp.concatenate(
                [pa[rows, A_SQ + (g * SWA_GROUP + hl) * SWA_HD:A_SQ + (g * SWA_GROUP + hl + 1) * SWA_HD]
                 for hl in range(SWA_GROUP)], axis=0).astype(BF16)
            kn = jnp.concatenate([knew[:, gs].astype(BF16), zeros_pad], axis=0)
            vn = jnp.concatenate([vnew[:, gs].astype(BF16), zeros_pad], axis=0)
            s_c = jnp.zeros((qrows, BLK), F32)
            for bl in range(SLAB // tdec):
                kc = kc_ref[seq0 + bl][:, gs].astype(BF16)
                s_c = jnp.where(bl64 == bl, _dot_nt(qg, kc), s_c)
            s_c = s_c * SWA_SCALE + tblc[g]
            s_n = _dot_nt(qg, kn) * SWA_SCALE + tbln[g]
            hl64 = r64[:, 0:1] // SLAB
            snk = jnp.zeros((qrows, 1), F32)
            for hl in range(SWA_GROUP):
                snk = jnp.where(hl64 == hl, sinks_ref[g * SWA_GROUP + hl], snk)
            mx = jnp.maximum(jnp.maximum(jnp.max(s_c, axis=-1, keepdims=True),
                                         jnp.max(s_n, axis=-1, keepdims=True)), snk)
            p_c = jnp.exp(s_c - mx)
            p_n = jnp.exp(s_n - mx)
            den = (jnp.sum(p_c, axis=-1, keepdims=True) + jnp.sum(p_n, axis=-1, keepdims=True)
                   + jnp.exp(snk - mx))
            o = _dot(p_n.astype(BF16), vn)
            for bl in range(SLAB // tdec):
                vc = vc_ref[seq0 + bl][:, gs].astype(BF16)
                o = o + _dot(jnp.where(bl64 == bl, p_c, 0.0).astype(BF16), vc)
            o = o / den
            og = jnp.concatenate([o[hl * SLAB:(hl + 1) * SLAB, :] for hl in range(SWA_GROUP)], axis=1)
            c0 = ML_WIDTH + g * SWA_GROUP * SWA_HD
            mix_ref[rows, c0:c0 + SWA_GROUP * SWA_HD] = og.astype(mix_ref.dtype)

        for bl in range(SLAB // tdec):
            ko_ref[seq0 + bl, 0:WINDOW - tdec, :] = kc_ref[seq0 + bl, tdec:WINDOW, :]
            vo_ref[seq0 + bl, 0:WINDOW - tdec, :] = vc_ref[seq0 + bl, tdec:WINDOW, :]
            ko_ref[seq0 + bl, WINDOW - tdec:WINDOW, :] = knew[bl * tdec:(bl + 1) * tdec, :]
            vo_ref[seq0 + bl, WINDOW - tdec:WINDOW, :] = vnew[bl * tdec:(bl + 1) * tdec, :]
        return carry

    lax.fori_loop(0, nslab, slab, 0)


def _sample_mixer(x2d, wa, wb, gb, mlg, sinks, rb, mrep, c0, n0, kc, vc):
    rows = x2d.shape[0]
    tdec = 4
    nseq = rows // tdec
    gbs = SAMPLE_GB
    steps = nseq // gbs
    rb_rows = gbs * tdec
    const2 = lambda i: (0, 0)
    smem = pl.BlockSpec(memory_space=pltpu.SMEM)
    return pl.pallas_call(
        _sample_mixer_kernel,
        grid=(steps,),
        in_specs=[
            pl.BlockSpec((rb_rows, D_MODEL), lambda i: (i, 0)),
            pl.BlockSpec((D_MODEL, A_COLS), const2),
            pl.BlockSpec((D_MODEL, B_COLS), const2),
            pl.BlockSpec((1, BLK), const2),
            pl.BlockSpec((1, ML_WIDTH), const2),
            smem, smem,
            pl.BlockSpec((rb_rows, BLK), lambda i: (i, 0)),
            pl.BlockSpec((gbs, ML_HEADS, ML_DK, ML_DV), lambda i: (i, 0, 0, 0)),
            pl.BlockSpec((gbs, ML_HEADS, ML_DK), lambda i: (i, 0, 0)),
            pl.BlockSpec((gbs, WINDOW, BLK), lambda i: (i, 0, 0)),
            pl.BlockSpec((gbs, WINDOW, BLK), lambda i: (i, 0, 0)),
        ],
        out_specs=[
            pl.BlockSpec((rb_rows, D_MODEL), lambda i: (i, 0)),
            pl.BlockSpec((gbs, ML_HEADS, ML_DK, ML_DV), lambda i: (i, 0, 0, 0)),
            pl.BlockSpec((gbs, ML_HEADS, ML_DK), lambda i: (i, 0, 0)),
            pl.BlockSpec((rb_rows, BLK), lambda i: (i, 0)),
            pl.BlockSpec((gbs, WINDOW, BLK), lambda i: (i, 0, 0)),
            pl.BlockSpec((gbs, WINDOW, BLK), lambda i: (i, 0, 0)),
        ],
        out_shape=[
            jax.ShapeDtypeStruct((rows, D_MODEL), BF16),
            jax.ShapeDtypeStruct((nseq, ML_HEADS, ML_DK, ML_DV), F32),
            jax.ShapeDtypeStruct((nseq, ML_HEADS, ML_DK), F32),
            jax.ShapeDtypeStruct((rows, BLK), F32),
            jax.ShapeDtypeStruct((nseq, WINDOW, BLK), F32),
            jax.ShapeDtypeStruct((nseq, WINDOW, BLK), F32),
        ],
        scratch_shapes=[
            pltpu.VMEM((rb_rows, A_COLS), F32),
            pltpu.VMEM((rb_rows, B_COLS), F32),
            pltpu.VMEM((SWA_KV_HEADS, SWA_GROUP * SLAB, BLK), F32),
            pltpu.VMEM((SWA_KV_HEADS, SWA_GROUP * SLAB, BLK), F32),
        ],
        compiler_params=pltpu.CompilerParams(
            dimension_semantics=("arbitrary",),
            vmem_limit_bytes=VMEM_LIMIT),
        name="sample_mixer",
    )(x2d, wa, wb, gb, mlg, sinks, rb, mrep, c0, n0, kc, vc)


def _dense_kernel(x_ref, mix_ref, wo_ref, g1_ref, b1_ref, wfi_ref, wfo_ref, g2_ref, b2_ref, y_ref):
    x = x_ref[...]
    h1 = _layer_norm(ALPHA * x + _dot(mix_ref[...], wo_ref[...]), g1_ref[...], b1_ref[...])
    h1b = h1.astype(BF16)
    fc = D_FF // FF_CHUNKS
    acc = None
    for c in range(FF_CHUNKS):
        gate = _dot(h1b, wfi_ref[:, c * fc:(c + 1) * fc])
        up = _dot(h1b, wfi_ref[:, D_FF + c * fc:D_FF + (c + 1) * fc])
        act = (gate * jax.nn.sigmoid(gate) * up).astype(BF16)
        part = _dot(act, wfo_ref[c * fc:(c + 1) * fc, :])
        acc = part if acc is None else acc + part
    y_ref[...] = _layer_norm(ALPHA * h1 + acc, g2_ref[...], b2_ref[...])


def _dense(x2d, mix2d, wo, g1, b1, wfi, wfo, g2, b2):
    rows = x2d.shape[0]
    tm = min(TM_DENSE, rows)
    const2 = lambda i: (0, 0)

    def wspec(shape):
        return pl.BlockSpec(shape, const2, pipeline_mode=pl.Buffered(1))

    return pl.pallas_call(
        _dense_kernel,
        grid=(rows // tm,),
        in_specs=[
            pl.BlockSpec((tm, D_MODEL), lambda i: (i, 0)),
            pl.BlockSpec((tm, D_MODEL), lambda i: (i, 0)),
            wspec((D_MODEL, D_MODEL)),
            wspec((1, D_MODEL)), wspec((1, D_MODEL)),
            wspec((D_MODEL, 2 * D_FF)),
            wspec((D_FF, D_MODEL)),
            wspec((1, D_MODEL)), wspec((1, D_MODEL)),
        ],
        out_specs=pl.BlockSpec((tm, D_MODEL), lambda i: (i, 0)),
        out_shape=jax.ShapeDtypeStruct((rows, D_MODEL), F32),
        compiler_params=pltpu.CompilerParams(
            dimension_semantics=("arbitrary",),
            vmem_limit_bytes=VMEM_LIMIT),
        name="dense",
    )(x2d, mix2d, wo, g1, b1, wfi, wfo, g2, b2)


def kernel(x_prompt, x_sample, state_mlstm_C, state_mlstm_n, state_mlstm_m, cache_swa_k, cache_swa_v,
           w_in, b_igate, b_fgate, ml_norm_g, swa_sinks, rel_bias, w_out, ln1_g, ln1_b,
           w_ffn_in, w_ffn_out, ln2_g, ln2_b):
    bp, seq, _ = x_prompt.shape
    bs, tdec, _ = x_sample.shape
    l = 0

    offs = [0]
    for n in IN_SIZES:
        offs.append(offs[-1] + n)
    w = w_in[l]
    mq, mk, mv, mo, mi, mf, sq, sk, sv = [w[:, offs[i]:offs[i + 1]] for i in range(len(IN_SIZES))]
    gpad = jnp.zeros((D_MODEL, BLK - 2 * ML_HEADS), w.dtype)
    wa = jnp.concatenate([mq, mk, mv, sq], axis=1).astype(BF16)
    wb = jnp.concatenate([mo, sk, sv, mi, mf, gpad], axis=1).astype(BF16)
    gb = jnp.concatenate([b_igate[l], b_fgate[l], jnp.zeros((BLK - 2 * ML_HEADS,), F32)])[None, :].astype(F32)
    mlg = ml_norm_g[l][None, :].astype(F32)
    sinks = swa_sinks[l].astype(F32)
    rb = rel_bias.astype(F32)
    wo = w_out[l].astype(BF16)
    wfi = w_ffn_in[l].astype(BF16)
    wfo = w_ffn_out[l].astype(BF16)
    g1, b1 = ln1_g[l][None, :], ln1_b[l][None, :]
    g2, b2 = ln2_g[l][None, :], ln2_b[l][None, :]

    mix_p, cn_p, m_p, k_p, v_p = _prompt_mixer(x_prompt, wa, wb, gb, mlg, sinks, rb)
    y_p = _dense(x_prompt.reshape(bp * seq, D_MODEL), mix_p.reshape(bp * seq, D_MODEL),
                 wo, g1, b1, wfi, wfo, g2, b2).reshape(bp, seq, D_MODEL)
    p_c = cn_p[..., :ML_DV][None]
    p_n = cn_p[..., ML_DV][None]
    p_m = m_p[:, :ML_HEADS, 0][None]
    p_k = k_p.reshape(1, bp, WINDOW, SWA_KV_HEADS, SWA_HD)
    p_v = v_p.reshape(1, bp, WINDOW, SWA_KV_HEADS, SWA_HD)

    xs = x_sample.reshape(bs * tdec, D_MODEL)
    m0 = state_mlstm_m[l].astype(F32)
    mrep = jnp.pad(jnp.repeat(m0, tdec, axis=0), ((0, 0), (ML_HEADS, BLK - 2 * ML_HEADS)))
    wlen = cache_swa_k.shape[2]
    kc = cache_swa_k[l].reshape(bs, wlen, SWA_KV_HEADS * SWA_HD)
    vc = cache_swa_v[l].reshape(bs, wlen, SWA_KV_HEADS * SWA_HD)
    mix_s, c_s, n_s, mo_s, k_s, v_s = _sample_mixer(
        xs, wa, wb, gb, mlg, sinks, rb, mrep,
        state_mlstm_C[l].astype(F32), state_mlstm_n[l].astype(F32), kc, vc)
    y_s = _dense(xs, mix_s, wo, g1, b1, wfi, wfo, g2, b2).reshape(bs, tdec, D_MODEL)
    s_m = mo_s.reshape(bs, tdec, BLK)[:, 0, ML_HEADS:2 * ML_HEADS][None]
    s_k = k_s.reshape(1, bs, wlen, SWA_KV_HEADS, SWA_HD)
    s_v = v_s.reshape(1, bs, wlen, SWA_KV_HEADS, SWA_HD)

    return (y_p, y_s, p_c, p_n, p_m, p_k, p_v,
            c_s[None], n_s[None], s_m, s_k, s_v)
```

```python
import functools
import math

import jax
import jax.numpy as jnp
from jax import lax
from jax.experimental import pallas as pl
from jax.experimental.pallas import tpu as pltpu

F32 = jnp.float32
BF16 = jnp.bfloat16

D_MODEL = 1024
ML_HEADS = 4
ML_DK = 128
ML_DV = 128
ML_WIDTH = ML_HEADS * ML_DV
GATE_SOFTCAP = 15.0
SWA_HEADS = 8
SWA_KV_HEADS = 2
SWA_GROUP = SWA_HEADS // SWA_KV_HEADS
SWA_HD = 64
SWA_WIDTH = SWA_HEADS * SWA_HD
WINDOW = 128
REL_BUCKETS = 32
REL_MAX_DIST = 128
D_FF = 2816
DEPTH = 1
ALPHA = (2.0 * DEPTH) ** 0.25
LN_EPS = 1e-5
NORM_EPS = 1e-6
NEG_INF = -1e30
IN_SIZES = (512, 512, 512, 512, 4, 4, 512, 128, 128)
ML_SCALE = ML_DK ** -0.5
SWA_SCALE = SWA_HD ** -0.5

A_Q, A_K, A_V, A_SQ = 0, 512, 1024, 1536
A_COLS = 2048
B_MO, B_SK, B_SV, B_G = 0, 512, 640, 768
B_COLS = 896

N_K = 0
N_B = 512
N_COLS = N_B + B_COLS
T_Q, T_V, T_SQ, T_SV, T_G = 0, 512, 1024, 1536, 1664
T_GROWS = 16
T_ROWS = T_G + T_GROWS
STATE_ROWS = ML_DV + 16

BLK = 128
TM_PROMPT = 512
TM_DENSE = 512
FF_SPLITS = (0, 1536, D_FF)
SAMPLE_GB = 16
SLAB = 16
VMEM_LIMIT = 56 * 1024 * 1024


def _softcap(a):
    return GATE_SOFTCAP * jnp.tanh(a / GATE_SOFTCAP)


def _log_sigmoid(x):
    return jnp.minimum(x, 0.0) - jnp.log1p(jnp.exp(-jnp.abs(x)))


def _layer_norm(z, g, b):
    mu = jnp.mean(z, axis=-1, keepdims=True)
    zc = z - mu
    var = jnp.mean(zc * zc, axis=-1, keepdims=True)
    return zc * lax.rsqrt(var + LN_EPS) * g + b


def _dot(a, b):
    return jnp.dot(a, b, preferred_element_type=F32)


def _dot_nt(a, b):
    return lax.dot_general(a, b, (((1,), (1,)), ((), ())), preferred_element_type=F32)


def _dot_tn(a, b):
    return lax.dot_general(a, b, (((0,), (0,)), ((), ())), preferred_element_type=F32)


def _split3(x):
    hi = x.astype(BF16)
    r1 = x - hi.astype(F32)
    mid = r1.astype(BF16)
    lo = (r1 - mid.astype(F32)).astype(BF16)
    return hi, mid, lo


def _t5_bucket(d):
    n = jnp.maximum(d, 0)
    max_exact = REL_BUCKETS // 2
    nf = jnp.maximum(n, 1).astype(F32)
    large = max_exact + (jnp.log(nf / max_exact) / math.log(REL_MAX_DIST / max_exact)
                         * (REL_BUCKETS - max_exact)).astype(jnp.int32)
    large = jnp.minimum(large, REL_BUCKETS - 1)
    return jnp.where(n < max_exact, n, large)


def _prompt_mixer_kernel(x_ref, wn_ref, wt_ref, gbr_ref, gbc_ref, mlg_ref, sinks_ref, rb_ref,
                         mix_ref, c_out, n_out, m_out, ko_ref, vo_ref,
                         kn, pf, kband, q_t, v_t, sq_t, sv_t, g_t, ct, msc, tbl):
    b = pl.program_id(0)
    t = pl.program_id(1)
    nt = pl.num_programs(1)
    tm = x_ref.shape[1]
    nblk = tm // BLK
    pairs = SWA_HEADS // 2

    @pl.when((b == 0) & (t == 0))
    def _build_bias_tables():
        r = lax.broadcasted_iota(jnp.int32, (BLK, 2 * BLK), 0)
        ln = lax.broadcasted_iota(jnp.int32, (BLK, 2 * BLK), 1)
        second = ln >= BLK
        qi = jnp.where(second, ln - BLK, ln)
        prev = r > qi
        d = jnp.where(prev, WINDOW + qi - r, qi - r)
        bucket = _t5_bucket(d)
        for p in range(pairs):
            acc = jnp.zeros((BLK, 2 * BLK), F32)
            for k in range(REL_BUCKETS):
                acc = jnp.where(bucket == k, jnp.where(second, rb_ref[k, p + pairs], rb_ref[k, p]), acc)
            tbl[0, p] = jnp.where(prev, NEG_INF, acc)
            tbl[1, p] = acc

    @pl.when(t == 0)
    def _reset_state():
        ct[...] = jnp.zeros_like(ct)
        msc[...] = jnp.zeros_like(msc)
        kband[0:BLK, :] = jnp.zeros((BLK, kband.shape[1]), kband.dtype)
        sv_t[0] = jnp.zeros(sv_t.shape[1:], sv_t.dtype)

    xb = x_ref[0].astype(BF16)
    kn[...] = _dot(xb, wn_ref[:, N_K:N_K + ML_HEADS * ML_DK]).astype(BF16)
    pf[...] = _dot(xb, wn_ref[:, N_B:N_B + B_COLS])
    kband[BLK:BLK + tm, :] = pf[:, B_SK:B_SK + BLK].astype(BF16)

    def put(dst, r0, nrows, off, scale=None):
        res = _dot_nt(wt_ref[r0:r0 + nrows, :], xb)
        if scale is not None:
            res = res * scale
        for jj in range(nblk):
            dst[jj + off] = res[:, jj * BLK:(jj + 1) * BLK].astype(dst.dtype)

    put(q_t, T_Q, ML_HEADS * ML_DK, 0)
    put(v_t, T_V, ML_WIDTH, 0)
    put(sq_t, T_SQ, SWA_WIDTH, 0, SWA_SCALE)
    put(sv_t, T_SV, SWA_KV_HEADS * SWA_HD, 1)
    put(g_t, T_G, T_GROWS, 0)

    row = lax.broadcasted_iota(jnp.int32, (BLK, BLK), 0)
    col = lax.broadcasted_iota(jnp.int32, (BLK, BLK), 1)
    tri = (row >= col).astype(BF16)
    causal_t = row <= col
    triu = causal_t.astype(BF16)
    r16 = lax.broadcasted_iota(jnp.int32, (16, BLK), 0)
    ones_rows = (r16 == 0).astype(BF16)
    gbr = gbr_ref[...]
    gbc = gbc_ref[...]
    mlg = mlg_ref[...]
    lane2 = lax.broadcasted_iota(jnp.int32, (1, 2 * BLK), 1)
    snk_rows = [jnp.where(lane2 >= BLK, sinks_ref[p + pairs], sinks_ref[p]) for p in range(pairs)]
    zq = jnp.zeros((SWA_HD, BLK), BF16)
    prev2 = jnp.concatenate([row > col, row > col], axis=1)

    def block(j, carry):
        r0 = pl.multiple_of(j * BLK, BLK)
        rows = pl.ds(r0, BLK)

        pre_t = _softcap(g_t[j] + gbc)
        hi, mid, lo = _split3(_log_sigmoid(pre_t))
        b_t = _dot(hi, triu) + _dot(mid, triu) + _dot(lo, triu)
        pre_c = _softcap(pf[rows, B_G:B_G + BLK] + gbr)
        hi, mid, lo = _split3(_log_sigmoid(pre_c))
        b_c = _dot(tri, hi) + _dot(tri, mid) + _dot(tri, lo)
        u_c = pltpu.roll(pre_c, ML_HEADS, 1) - b_c

        for h in range(ML_HEADS):
            hs = slice(h * ML_DK, (h + 1) * ML_DK)
            b_row = b_t[ML_HEADS + h:ML_HEADS + h + 1, :]
            i_row = pre_t[h:h + 1, :]
            u_col = u_c[:, ML_HEADS + h:ML_HEADS + h + 1]
            m_prev = msc[h:h + 1, 0:1]

            log_d = jnp.where(causal_t, b_row + u_col, -jnp.inf)
            inter = b_row + m_prev
            m_row = jnp.maximum(inter, jnp.max(log_d, axis=0, keepdims=True))
            scl = jnp.exp(inter - m_row)
            dm = jnp.exp(log_d - m_row)

            k_h = kn[rows, hs]
            qt_h = q_t[j, hs, :]
            vte = jnp.concatenate([v_t[j, hs, :], ones_rows], axis=0)
            s_t = (_dot(k_h, qt_h) * ML_SCALE * dm).astype(BF16)
            c_old = ct[h]
            nd = scl * _dot(c_old.astype(BF16), qt_h) + _dot(vte, s_t)
            num = nd[:ML_DV]
            den = nd[ML_DV:ML_DV + 1]
            hh = num / jnp.maximum(jnp.abs(den), jnp.exp(-m_row))
            hn = hh * lax.rsqrt(jnp.mean(hh * hh, axis=0, keepdims=True) + NORM_EPS)
            og = jax.nn.sigmoid(pf[rows, B_MO + h * ML_DV:B_MO + (h + 1) * ML_DV])
            mix_ref[0, rows, hs] = (hn.T * mlg[:, hs] * og).astype(mix_ref.dtype)

            b_last = b_row[:, BLK - 1:BLK]
            w_row = b_last - b_row + i_row
            m_new = jnp.maximum(b_last + m_prev, jnp.max(w_row, axis=1, keepdims=True))
            a = jnp.exp(b_last + m_prev - m_new)
            wexp = jnp.exp(w_row - m_new) * ML_SCALE
            vwte = (vte.astype(F32) * wexp).astype(BF16)
            ct[h] = a * c_old + _dot(vwte, k_h)
            msc[h:h + 1, :] = jnp.broadcast_to(m_new, (1, msc.shape[1]))

        sel = jnp.where((t == 0) & (j == 0), 0, 1)
        kb = kband[pl.ds(r0, 2 * BLK), :]
        vtb = jnp.concatenate([sv_t[j], sv_t[j + 1]], axis=1)
        o_parts = [None] * SWA_HEADS
        for p in range(pairs):
            qa = sq_t[j, p * SWA_HD:(p + 1) * SWA_HD, :]
            qb = sq_t[j, (p + pairs) * SWA_HD:(p + pairs + 1) * SWA_HD, :]
            bd = jnp.concatenate([jnp.concatenate([qa, zq], axis=1),
                                  jnp.concatenate([zq, qb], axis=1)], axis=0)
            sc2 = _dot(kb, bd)
            sc = jnp.where(prev2, sc2[:BLK], sc2[BLK:]) + tbl[sel, p]
            snk = snk_rows[p]
            mx = jnp.maximum(jnp.max(sc, axis=0, keepdims=True), snk)
            pe = jnp.exp(sc - mx)
            den = jnp.sum(pe, axis=0, keepdims=True) + jnp.exp(snk - mx)
            pw = jnp.concatenate([jnp.where(prev2, pe, 0.0), jnp.where(prev2, 0.0, pe)], axis=0)
            o2 = _dot(vtb, pw.astype(BF16)) / den
            o_parts[p] = o2[0:SWA_HD, 0:BLK]
            o_parts[p + pairs] = o2[SWA_HD:2 * SWA_HD, BLK:2 * BLK]
        o_t = jnp.concatenate(o_parts, axis=0)
        mix_ref[0, rows, ML_WIDTH:ML_WIDTH + SWA_WIDTH] = o_t.T.astype(mix_ref.dtype)
        return carry

    lax.fori_loop(0, nblk, block, 0)

    kband[0:BLK, :] = kband[tm:tm + BLK, :]
    sv_t[0] = sv_t[nblk]

    @pl.when(t == nt - 1)
    def _write_state():
        for h in range(ML_HEADS):
            c_out[0, h] = ct[h, 0:ML_DV, :].T
            n_out[0, h:h + 1, :] = ct[h, ML_DV:ML_DV + 1, :]
        m_out[0] = msc[...]
        ko_ref[0] = pf[tm - BLK:tm, B_SK:B_SK + BLK]
        vo_ref[0] = pf[tm - BLK:tm, B_SV:B_SV + BLK]


def _prompt_mixer(x, wn, wt, gbr, gbc, mlg, sinks, rb):
    bsz, seq, _ = x.shape
    tm = TM_PROMPT
    nt = seq // tm
    nblk = tm // BLK
    const2 = lambda b, t: (0, 0)
    smem = pl.BlockSpec(memory_space=pltpu.SMEM)
    return pl.pallas_call(
        _prompt_mixer_kernel,
        grid=(bsz, nt),
        in_specs=[
            pl.BlockSpec((1, tm, D_MODEL), lambda b, t: (b, t, 0)),
            pl.BlockSpec((D_MODEL, N_COLS), const2),
            pl.BlockSpec((T_ROWS, D_MODEL), const2),
            pl.BlockSpec((1, BLK), const2),
            pl.BlockSpec((T_GROWS, BLK), const2),
            pl.BlockSpec((1, ML_WIDTH), const2),
            smem, smem,
        ],
        out_specs=[
            pl.BlockSpec((1, tm, D_MODEL), lambda b, t: (b, t, 0)),
            pl.BlockSpec((1, ML_HEADS, ML_DK, ML_DV), lambda b, t: (b, 0, 0, 0)),
            pl.BlockSpec((1, ML_HEADS, ML_DK), lambda b, t: (b, 0, 0)),
            pl.BlockSpec((1, 8, BLK), lambda b, t: (b, 0, 0)),
            pl.BlockSpec((1, BLK, BLK), lambda b, t: (b, 0, 0)),
            pl.BlockSpec((1, BLK, BLK), lambda b, t: (b, 0, 0)),
        ],
        out_shape=[
            jax.ShapeDtypeStruct((bsz, seq, D_MODEL), BF16),
            jax.ShapeDtypeStruct((bsz, ML_HEADS, ML_DK, ML_DV), F32),
            jax.ShapeDtypeStruct((bsz, ML_HEADS, ML_DK), F32),
            jax.ShapeDtypeStruct((bsz, 8, BLK), F32),
            jax.ShapeDtypeStruct((bsz, BLK, BLK), F32),
            jax.ShapeDtypeStruct((bsz, BLK, BLK), F32),
        ],
        scratch_shapes=[
            pltpu.VMEM((tm, ML_HEADS * ML_DK), BF16),
            pltpu.VMEM((tm, B_COLS), F32),
            pltpu.VMEM((tm + BLK, BLK), BF16),
            pltpu.VMEM((nblk, ML_HEADS * ML_DK, BLK), BF16),
            pltpu.VMEM((nblk, ML_WIDTH, BLK), BF16),
            pltpu.VMEM((nblk, SWA_WIDTH, BLK), BF16),
            pltpu.VMEM((nblk + 1, SWA_KV_HEADS * SWA_HD, BLK), BF16),
            pltpu.VMEM((nblk, T_GROWS, BLK), F32),
            pltpu.VMEM((ML_HEADS, STATE_ROWS, ML_DK), F32),
            pltpu.VMEM((8, BLK), F32),
            pltpu.VMEM((2, SWA_HEADS // 2, BLK, 2 * BLK), F32),
        ],
        compiler_params=pltpu.CompilerParams(
            dimension_semantics=("arbitrary", "arbitrary"),
            vmem_limit_bytes=VMEM_LIMIT),
        name="prompt_mixer",
    )(x, wn, wt, gbr, gbc, mlg, sinks, rb)


def _roll_rows(x, shift):
    return pltpu.roll(x, shift % x.shape[0], 0)


def _sample_mixer_kernel(x_ref, wa_ref, wb_ref, gb_ref, mlg_ref, sinks_ref, rb_ref, mrep_ref,
                         c_ref, n_ref, kc_ref, vc_ref,
                         mix_ref, c_out, n_out, mo_ref, ko_ref, vo_ref,
                         pa, pf, tblc, tbln):
    step = pl.program_id(0)
    rows_total = x_ref.shape[0]
    tdec = 4
    nslab = rows_total // SLAB
    qrows = SWA_GROUP * SLAB

    @pl.when(step == 0)
    def _build_bias_tables():
        r = lax.broadcasted_iota(jnp.int32, (qrows, BLK), 0)
        c = lax.broadcasted_iota(jnp.int32, (qrows, BLK), 1)
        hl = r // SLAB
        bl = (r % SLAB) // tdec
        tq = r % tdec
        d_cache = WINDOW + tq - c
        d_new = tq - (c % tdec)
        bk_cache = _t5_bucket(d_cache)
        bk_new = _t5_bucket(d_new)
        ok_cache = c > tq
        ok_new = (c < SLAB) & ((c // tdec) == bl) & (d_new >= 0)
        for g in range(SWA_KV_HEADS):
            acc_c = jnp.zeros((qrows, BLK), F32)
            acc_n = jnp.zeros((qrows, BLK), F32)
            for k in range(REL_BUCKETS):
                for hh in range(SWA_GROUP):
                    val = rb_ref[k, g * SWA_GROUP + hh]
                    acc_c = jnp.where((bk_cache == k) & (hl == hh), val, acc_c)
                    acc_n = jnp.where((bk_new == k) & (hl == hh), val, acc_n)
            tblc[g] = jnp.where(ok_cache, acc_c, NEG_INF)
            tbln[g] = jnp.where(ok_new, acc_n, NEG_INF)

    xb = x_ref[...].astype(BF16)
    pa[...] = _dot(xb, wa_ref[...])
    pf[...] = _dot(xb, wb_ref[...])

    gbias = gb_ref[...]
    mlg = mlg_ref[...]
    r16 = lax.broadcasted_iota(jnp.int32, (SLAB, BLK), 0)
    rr = r16 % tdec
    bl16 = r16 // tdec
    r64 = lax.broadcasted_iota(jnp.int32, (qrows, BLK), 0)
    bl64 = (r64 % SLAB) // tdec
    zeros_pad = jnp.zeros((BLK - SLAB, SWA_HD), BF16)

    def seg_last(x):
        return jnp.where(rr == 3, x,
                         jnp.where(rr == 2, _roll_rows(x, -1),
                                   jnp.where(rr == 1, _roll_rows(x, -2), _roll_rows(x, -3))))

    def seg_max(x):
        m1 = jnp.maximum(x, jnp.where(rr % 2 == 0, _roll_rows(x, -1), _roll_rows(x, 1)))
        return jnp.maximum(m1, jnp.where(rr < 2, _roll_rows(m1, -2), _roll_rows(m1, 2)))

    def slab(si, carry):
        r0 = pl.multiple_of(si * SLAB, SLAB)
        rows = pl.ds(r0, SLAB)
        seq0 = si * (SLAB // tdec)

        pre = _softcap(pf[rows, B_G:B_G + BLK] + gbias)
        logf = _log_sigmoid(pre)
        y = logf + jnp.where(rr >= 1, _roll_rows(logf, 1), 0.0)
        bcum = y + jnp.where(rr >= 2, _roll_rows(y, 2), 0.0)
        ig = pltpu.roll(pre, ML_HEADS, 1)
        m_prev = mrep_ref[rows, :]
        b_minus_i = bcum - ig
        log_d = [jnp.where(rr >= dl, bcum - _roll_rows(b_minus_i, dl), -jnp.inf) for dl in range(tdec)]
        rowmax = functools.reduce(jnp.maximum, log_d)
        inter = bcum + m_prev
        m_row = jnp.maximum(inter, rowmax)
        scl = jnp.exp(inter - m_row)
        dm = [jnp.exp(ld - m_row) for ld in log_d]
        enm = jnp.exp(-m_row)
        b_last = seg_last(bcum)
        w = b_last - bcum + ig
        m_new = jnp.maximum(b_last + m_prev, seg_max(w))
        a = jnp.exp(b_last + m_prev - m_new)
        kws = jnp.exp(w - m_new) * ML_SCALE
        mo_ref[rows, :] = m_new

        for h in range(ML_HEADS):
            hs = slice(h * ML_DK, (h + 1) * ML_DK)
            lane = ML_HEADS + h
            q = pa[rows, A_Q + h * ML_DK:A_Q + (h + 1) * ML_DK]
            k = pa[rows, A_K + h * ML_DK:A_K + (h + 1) * ML_DK]
            v = pa[rows, A_V + h * ML_DV:A_V + (h + 1) * ML_DV]
            bm = jnp.zeros((SLAB, ML_DV), F32)
            sum_s = jnp.zeros((SLAB, 1), F32)
            for dl in range(tdec):
                s_dl = jnp.sum(q * _roll_rows(k, dl), axis=-1, keepdims=True)
                ws = s_dl * ML_SCALE * dm[dl][:, lane:lane + 1]
                bm = bm + ws * _roll_rows(v, dl)
                sum_s = sum_s + ws
            qc = jnp.zeros((SLAB, ML_DV), F32)
            nsel = jnp.zeros((SLAB, ML_DK), F32)
            for bl in range(SLAB // tdec):
                msk = bl16 == bl
                c_b = c_ref[seq0 + bl, h]
                n_b = n_ref[seq0 + bl][h:h + 1, :]
                qc = qc + _dot(jnp.where(msk, q, 0.0).astype(BF16), c_b.astype(BF16))
                nsel = jnp.where(msk, n_b, nsel)
                a_b = a[bl * tdec:bl * tdec + 1, lane:lane + 1]
                kw = jnp.where(msk, k * kws[:, lane:lane + 1], 0.0)
                c_out[seq0 + bl, h] = a_b * c_b + _dot_tn(kw.astype(BF16), v.astype(BF16))
                n_out[seq0 + bl, h:h + 1, :] = a_b * n_b + jnp.sum(kw, axis=0, keepdims=True)
            sclh = scl[:, lane:lane + 1]
            num = sclh * qc + bm
            den = sclh * jnp.sum(q * nsel, axis=-1, keepdims=True) + sum_s
            hh = num / jnp.maximum(jnp.abs(den), enm[:, lane:lane + 1])
            hn = hh * lax.rsqrt(jnp.mean(hh * hh, axis=-1, keepdims=True) + NORM_EPS)
            og = jax.nn.sigmoid(pf[rows, B_MO + h * ML_DV:B_MO + (h + 1) * ML_DV])
            mix_ref[rows, hs] = (hn * mlg[:, hs] * og).astype(mix_ref.dtype)

        knew = pf[rows, B_SK:B_SK + BLK]
        vnew = pf[rows, B_SV:B_SV + BLK]
        for g in range(SWA_KV_HEADS):
            gs = slice(g * SWA_HD, (g + 1) * SWA_HD)
            qg = jnp.concatenate(
                [pa[rows, A_SQ + (g * SWA_GROUP + hl) * SWA_HD:A_SQ + (g * SWA_GROUP + hl + 1) * SWA_HD]
                 for hl in range(SWA_GROUP)], axis=0).astype(BF16)
            kn = jnp.concatenate([knew[:, gs].astype(BF16), zeros_pad], axis=0)
            vn = jnp.concatenate([vnew[:, gs].astype(BF16), zeros_pad], axis=0)
            s_c = jnp.zeros((qrows, BLK), F32)
            for bl in range(SLAB // tdec):
                kc = kc_ref[seq0 + bl][:, gs].astype(BF16)
                s_c = jnp.where(bl64 == bl, _dot_nt(qg, kc), s_c)
            s_c = s_c * SWA_SCALE + tblc[g]
            s_n = _dot_nt(qg, kn) * SWA_SCALE + tbln[g]
            hl64 = r64[:, 0:1] // SLAB
            snk = jnp.zeros((qrows, 1), F32)
            for hl in range(SWA_GROUP):
                snk = jnp.where(hl64 == hl, sinks_ref[g * SWA_GROUP + hl], snk)
            mx = jnp.maximum(jnp.maximum(jnp.max(s_c, axis=-1, keepdims=True),
                                         jnp.max(s_n, axis=-1, keepdims=True)), snk)
            p_c = jnp.exp(s_c - mx)
            p_n = jnp.exp(s_n - mx)
            den = (jnp.sum(p_c, axis=-1, keepdims=True) + jnp.sum(p_n, axis=-1, keepdims=True)
                   + jnp.exp(snk - mx))
            o = _dot(p_n.astype(BF16), vn)
            for bl in range(SLAB // tdec):
                vc = vc_ref[seq0 + bl][:, gs].astype(BF16)
                o = o + _dot(jnp.where(bl64 == bl, p_c, 0.0).astype(BF16), vc)
            o = o / den
            og = jnp.concatenate([o[hl * SLAB:(hl + 1) * SLAB, :] for hl in range(SWA_GROUP)], axis=1)
            c0 = ML_WIDTH + g * SWA_GROUP * SWA_HD
            mix_ref[rows, c0:c0 + SWA_GROUP * SWA_HD] = og.astype(mix_ref.dtype)

        for bl in range(SLAB // tdec):
            ko_ref[seq0 + bl, 0:WINDOW - tdec, :] = kc_ref[seq0 + bl, tdec:WINDOW, :]
            vo_ref[seq0 + bl, 0:WINDOW - tdec, :] = vc_ref[seq0 + bl, tdec:WINDOW, :]
            ko_ref[seq0 + bl, WINDOW - tdec:WINDOW, :] = knew[bl * tdec:(bl + 1) * tdec, :]
            vo_ref[seq0 + bl, WINDOW - tdec:WINDOW, :] = vnew[bl * tdec:(bl + 1) * tdec, :]
        return carry

    lax.fori_loop(0, nslab, slab, 0)


def _sample_mixer(x2d, wa, wb, gb, mlg, sinks, rb, mrep, c0, n0, kc, vc):
    rows = x2d.shape[0]
    tdec = 4
    nseq = rows // tdec
    gbs = SAMPLE_GB
    steps = nseq // gbs
    rb_rows = gbs * tdec
    const2 = lambda i: (0, 0)
    smem = pl.BlockSpec(memory_space=pltpu.SMEM)
    return pl.pallas_call(
        _sample_mixer_kernel,
        grid=(steps,),
        in_specs=[
            pl.BlockSpec((rb_rows, D_MODEL), lambda i: (i, 0)),
            pl.BlockSpec((D_MODEL, A_COLS), const2),
            pl.BlockSpec((D_MODEL, B_COLS), const2),
            pl.BlockSpec((1, BLK), const2),
            pl.BlockSpec((1, ML_WIDTH), const2),
            smem, smem,
            pl.BlockSpec((rb_rows, BLK), lambda i: (i, 0)),
            pl.BlockSpec((gbs, ML_HEADS, ML_DK, ML_DV), lambda i: (i, 0, 0, 0)),
            pl.BlockSpec((gbs, ML_HEADS, ML_DK), lambda i: (i, 0, 0)),
            pl.BlockSpec((gbs, WINDOW, BLK), lambda i: (i, 0, 0)),
            pl.BlockSpec((gbs, WINDOW, BLK), lambda i: (i, 0, 0)),
        ],
        out_specs=[
            pl.BlockSpec((rb_rows, D_MODEL), lambda i: (i, 0)),
            pl.BlockSpec((gbs, ML_HEADS, ML_DK, ML_DV), lambda i: (i, 0, 0, 0)),
            pl.BlockSpec((gbs, ML_HEADS, ML_DK), lambda i: (i, 0, 0)),
            pl.BlockSpec((rb_rows, BLK), lambda i: (i, 0)),
            pl.BlockSpec((gbs, WINDOW, BLK), lambda i: (i, 0, 0)),
            pl.BlockSpec((gbs, WINDOW, BLK), lambda i: (i, 0, 0)),
        ],
        out_shape=[
            jax.ShapeDtypeStruct((rows, D_MODEL), BF16),
            jax.ShapeDtypeStruct((nseq, ML_HEADS, ML_DK, ML_DV), F32),
            jax.ShapeDtypeStruct((nseq, ML_HEADS, ML_DK), F32),
            jax.ShapeDtypeStruct((rows, BLK), F32),
            jax.ShapeDtypeStruct((nseq, WINDOW, BLK), F32),
            jax.ShapeDtypeStruct((nseq, WINDOW, BLK), F32),
        ],
        scratch_shapes=[
            pltpu.VMEM((rb_rows, A_COLS), F32),
            pltpu.VMEM((rb_rows, B_COLS), F32),
            pltpu.VMEM((SWA_KV_HEADS, SWA_GROUP * SLAB, BLK), F32),
            pltpu.VMEM((SWA_KV_HEADS, SWA_GROUP * SLAB, BLK), F32),
        ],
        compiler_params=pltpu.CompilerParams(
            dimension_semantics=("arbitrary",),
            vmem_limit_bytes=VMEM_LIMIT),
        name="sample_mixer",
    )(x2d, wa, wb, gb, mlg, sinks, rb, mrep, c0, n0, kc, vc)


def _dense_kernel(x_ref, mix_ref, wo_ref, g1_ref, b1_ref, wfi_ref, wfo_ref, g2_ref, b2_ref, y_ref):
    x = x_ref[...]
    h1 = _layer_norm(ALPHA * x + _dot(mix_ref[...], wo_ref[...]), g1_ref[...], b1_ref[...])
    h1b = h1.astype(BF16)
    acc = None
    for c0, c1 in zip(FF_SPLITS[:-1], FF_SPLITS[1:]):
        gate = _dot(h1b, wfi_ref[:, c0:c1])
        up = _dot(h1b, wfi_ref[:, D_FF + c0:D_FF + c1])
        act = (gate * jax.nn.sigmoid(gate) * up).astype(BF16)
        part = _dot(act, wfo_ref[c0:c1, :])
        acc = part if acc is None else acc + part
    y_ref[...] = _layer_norm(ALPHA * h1 + acc, g2_ref[...], b2_ref[...])


def _dense(x2d, mix2d, wo, g1, b1, wfi, wfo, g2, b2):
    rows = x2d.shape[0]
    tm = min(TM_DENSE, rows)
    const2 = lambda i: (0, 0)

    def wspec(shape):
        return pl.BlockSpec(shape, const2, pipeline_mode=pl.Buffered(1))

    return pl.pallas_call(
        _dense_kernel,
        grid=(rows // tm,),
        in_specs=[
            pl.BlockSpec((tm, D_MODEL), lambda i: (i, 0)),
            pl.BlockSpec((tm, D_MODEL), lambda i: (i, 0)),
            wspec((D_MODEL, D_MODEL)),
            wspec((1, D_MODEL)), wspec((1, D_MODEL)),
            wspec((D_MODEL, 2 * D_FF)),
            wspec((D_FF, D_MODEL)),
            wspec((1, D_MODEL)), wspec((1, D_MODEL)),
        ],
        out_specs=pl.BlockSpec((tm, D_MODEL), lambda i: (i, 0)),
        out_shape=jax.ShapeDtypeStruct((rows, D_MODEL), F32),
        compiler_params=pltpu.CompilerParams(
            dimension_semantics=("arbitrary",),
            vmem_limit_bytes=VMEM_LIMIT),
        name="dense",
    )(x2d, mix2d, wo, g1, b1, wfi, wfo, g2, b2)


def kernel(x_prompt, x_sample, state_mlstm_C, state_mlstm_n, state_mlstm_m, cache_swa_k, cache_swa_v,
           w_in, b_igate, b_fgate, ml_norm_g, swa_sinks, rel_bias, w_out, ln1_g, ln1_b,
           w_ffn_in, w_ffn_out, ln2_g, ln2_b):
    bp, seq, _ = x_prompt.shape
    bs, tdec, _ = x_sample.shape
    l = 0

    offs = [0]
    for n in IN_SIZES:
        offs.append(offs[-1] + n)
    w = w_in[l].astype(BF16)
    mq, mk, mv, mo, mi, mf, sq, sk, sv = [w[:, offs[i]:offs[i + 1]] for i in range(len(IN_SIZES))]
    gpad = jnp.zeros((D_MODEL, BLK - 2 * ML_HEADS), BF16)
    wb = jnp.concatenate([mo, sk, sv, mi, mf, gpad], axis=1)
    wa = jnp.concatenate([mq, mk, mv, sq], axis=1)
    wn = jnp.concatenate([mk, wb], axis=1)
    wt = jnp.concatenate([mq, mv, sq, sv, mi, mf, gpad[:, :T_GROWS - 2 * ML_HEADS]], axis=1).T
    gvec = jnp.concatenate([b_igate[l], b_fgate[l]]).astype(F32)
    gb = jnp.pad(gvec, (0, BLK - 2 * ML_HEADS))[None, :]
    gbc = jnp.broadcast_to(jnp.pad(gvec, (0, T_GROWS - 2 * ML_HEADS))[:, None], (T_GROWS, BLK))
    mlg = ml_norm_g[l][None, :].astype(F32)
    sinks = swa_sinks[l].astype(F32)
    rb = rel_bias.astype(F32)
    wo = w_out[l].astype(BF16)
    wfi = w_ffn_in[l].astype(BF16)
    wfo = w_ffn_out[l].astype(BF16)
    g1, b1 = ln1_g[l][None, :], ln1_b[l][None, :]
    g2, b2 = ln2_g[l][None, :], ln2_b[l][None, :]

    mix_p, c_p, n_p, m_p, k_p, v_p = _prompt_mixer(x_prompt, wn, wt, gb, gbc, mlg, sinks, rb)
    y_p = _dense(x_prompt.reshape(bp * seq, D_MODEL), mix_p.reshape(bp * seq, D_MODEL),
                 wo, g1, b1, wfi, wfo, g2, b2).reshape(bp, seq, D_MODEL)
    p_m = m_p[:, :ML_HEADS, 0][None]
    p_k = k_p.reshape(1, bp, WINDOW, SWA_KV_HEADS, SWA_HD)
    p_v = v_p.reshape(1, bp, WINDOW, SWA_KV_HEADS, SWA_HD)

    xs = x_sample.reshape(bs * tdec, D_MODEL)
    m0 = state_mlstm_m[l].astype(F32)
    mrep = jnp.pad(jnp.repeat(m0, tdec, axis=0), ((0, 0), (ML_HEADS, BLK - 2 * ML_HEADS)))
    wlen = cache_swa_k.shape[2]
    kc = cache_swa_k[l].reshape(bs, wlen, SWA_KV_HEADS * SWA_HD)
    vc = cache_swa_v[l].reshape(bs, wlen, SWA_KV_HEADS * SWA_HD)
    mix_s, c_s, n_s, mo_s, k_s, v_s = _sample_mixer(
        xs, wa, wb, gb, mlg, sinks, rb, mrep,
        state_mlstm_C[l].astype(F32), state_mlstm_n[l].astype(F32), kc, vc)
    y_s = _dense(xs, mix_s, wo, g1, b1, wfi, wfo, g2, b2).reshape(bs, tdec, D_MODEL)
    s_m = mo_s.reshape(bs, tdec, BLK)[:, 0, ML_HEADS:2 * ML_HEADS][None]
    s_k = k_s.reshape(1, bs, wlen, SWA_KV_HEADS, SWA_HD)
    s_v = v_s.reshape(1, bs, wlen, SWA_KV_HEADS, SWA_HD)

    return (y_p, y_s, c_p[None], n_p[None], p_m, p_k, p_v,
            c_s[None], n_s[None], s_m, s_k, s_v)
```

```python
import functools
import math

import jax
import jax.numpy as jnp
from jax import lax
from jax.experimental import pallas as pl
from jax.experimental.pallas import tpu as pltpu

F32 = jnp.float32
BF16 = jnp.bfloat16

D_MODEL = 1024
ML_HEADS = 4
ML_DK = 128
ML_DV = 128
ML_WIDTH = ML_HEADS * ML_DV
GATE_SOFTCAP = 15.0
SWA_HEADS = 8
SWA_KV_HEADS = 2
SWA_GROUP = SWA_HEADS // SWA_KV_HEADS
SWA_HD = 64
SWA_WIDTH = SWA_HEADS * SWA_HD
WINDOW = 128
REL_BUCKETS = 32
REL_MAX_DIST = 128
D_FF = 2816
DEPTH = 1
ALPHA = (2.0 * DEPTH) ** 0.25
LN_EPS = 1e-5
NORM_EPS = 1e-6
NEG_INF = -1e30
IN_SIZES = (512, 512, 512, 512, 4, 4, 512, 128, 128)
ML_SCALE = ML_DK ** -0.5
SWA_SCALE = SWA_HD ** -0.5

A_Q, A_K, A_V, A_SQ = 0, 512, 1024, 1536
A_COLS = 2048
B_MO, B_SK, B_SV, B_G = 0, 512, 640, 768
B_COLS = 896

N_K = 0
N_B = 512
N_COLS = N_B + B_COLS
T_Q, T_V, T_SQ, T_SV, T_G = 0, 512, 1024, 1536, 1664
T_GROWS = 16
T_ROWS = T_G + T_GROWS
STATE_ROWS = ML_DV + 16

BLK = 128
TM_PROMPT = 512
TM_DENSE = 512
FF_SPLITS = (0, 1536, D_FF)
SAMPLE_GB = 16
SLAB = 16
VMEM_LIMIT = 56 * 1024 * 1024


def _softcap(a):
    return GATE_SOFTCAP * jnp.tanh(a / GATE_SOFTCAP)


def _log_sigmoid(x):
    return jnp.minimum(x, 0.0) - jnp.log1p(jnp.exp(-jnp.abs(x)))


def _layer_norm(z, g, b):
    mu = jnp.mean(z, axis=-1, keepdims=True)
    zc = z - mu
    var = jnp.mean(zc * zc, axis=-1, keepdims=True)
    return zc * lax.rsqrt(var + LN_EPS) * g + b


def _dot(a, b):
    return jnp.dot(a, b, preferred_element_type=F32)


def _dot_nt(a, b):
    return lax.dot_general(a, b, (((1,), (1,)), ((), ())), preferred_element_type=F32)


def _dot_tn(a, b):
    return lax.dot_general(a, b, (((0,), (0,)), ((), ())), preferred_element_type=F32)


def _split3(x):
    hi = x.astype(BF16)
    r1 = x - hi.astype(F32)
    mid = r1.astype(BF16)
    lo = (r1 - mid.astype(F32)).astype(BF16)
    return hi, mid, lo


def _t5_bucket(d):
    n = jnp.maximum(d, 0)
    max_exact = REL_BUCKETS // 2
    nf = jnp.maximum(n, 1).astype(F32)
    large = max_exact + (jnp.log(nf / max_exact) / math.log(REL_MAX_DIST / max_exact)
                         * (REL_BUCKETS - max_exact)).astype(jnp.int32)
    large = jnp.minimum(large, REL_BUCKETS - 1)
    return jnp.where(n < max_exact, n, large)


def _prompt_mixer_kernel(x_ref, wn_ref, wt_ref, gbr_ref, gbc_ref, mlg_ref, sinks_ref, rb_ref,
                         mix_ref, c_out, n_out, m_out, ko_ref, vo_ref,
                         kn, pf, kband, q_t, v_t, sq_t, sv_t, g_t, ct, msc, tbl):
    b = pl.program_id(0)
    t = pl.program_id(1)
    nt = pl.num_programs(1)
    tm = x_ref.shape[1]
    nblk = tm // BLK
    pairs = SWA_HEADS // 2

    @pl.when((b == 0) & (t == 0))
    def _build_bias_tables():
        r = lax.broadcasted_iota(jnp.int32, (BLK, 2 * BLK), 0)
        ln = lax.broadcasted_iota(jnp.int32, (BLK, 2 * BLK), 1)
        second = ln >= BLK
        qi = jnp.where(second, ln - BLK, ln)
        prev = r > qi
        d = jnp.where(prev, WINDOW + qi - r, qi - r)
        bucket = _t5_bucket(d)
        for p in range(pairs):
            acc = jnp.zeros((BLK, 2 * BLK), F32)
            for k in range(REL_BUCKETS):
                acc = jnp.where(bucket == k, jnp.where(second, rb_ref[k, p + pairs], rb_ref[k, p]), acc)
            tbl[0, p] = jnp.where(prev, NEG_INF, acc)
            tbl[1, p] = acc

    @pl.when(t == 0)
    def _reset_state():
        ct[...] = jnp.zeros_like(ct)
        msc[...] = jnp.zeros_like(msc)
        kband[0:BLK, :] = jnp.zeros((BLK, kband.shape[1]), kband.dtype)
        sv_t[0] = jnp.zeros(sv_t.shape[1:], sv_t.dtype)

    xb = x_ref[0].astype(BF16)
    kn[...] = _dot(xb, wn_ref[:, N_K:N_K + ML_HEADS * ML_DK]).astype(BF16)
    pf[...] = _dot(xb, wn_ref[:, N_B:N_B + B_COLS])
    kband[BLK:BLK + tm, :] = pf[:, B_SK:B_SK + BLK].astype(BF16)

    def put(dst, r0, nrows, off, scale=None):
        res = _dot_nt(wt_ref[r0:r0 + nrows, :], xb)
        if scale is not None:
            res = res * scale
        for jj in range(nblk):
            dst[jj + off] = res[:, jj * BLK:(jj + 1) * BLK].astype(dst.dtype)

    put(q_t, T_Q, ML_HEADS * ML_DK, 0)
    put(v_t, T_V, ML_WIDTH, 0)
    put(sq_t, T_SQ, SWA_WIDTH, 0, SWA_SCALE)
    put(sv_t, T_SV, SWA_KV_HEADS * SWA_HD, 1)
    put(g_t, T_G, T_GROWS, 0)

    row = lax.broadcasted_iota(jnp.int32, (BLK, BLK), 0)
    col = lax.broadcasted_iota(jnp.int32, (BLK, BLK), 1)
    tri = (row >= col).astype(BF16)
    causal_t = row <= col
    triu = causal_t.astype(BF16)
    r16 = lax.broadcasted_iota(jnp.int32, (16, BLK), 0)
    ones_rows = (r16 == 0).astype(BF16)
    gbr = gbr_ref[...]
    gbc = gbc_ref[...]
    mlg = mlg_ref[...]
    lane2 = lax.broadcasted_iota(jnp.int32, (1, 2 * BLK), 1)
    snk_rows = [jnp.where(lane2 >= BLK, sinks_ref[p + pairs], sinks_ref[p]) for p in range(pairs)]
    zq = jnp.zeros((SWA_HD, BLK), BF16)
    prev2 = jnp.concatenate([row > col, row > col], axis=1)

    def block(j, carry):
        r0 = pl.multiple_of(j * BLK, BLK)
        rows = pl.ds(r0, BLK)
        heads = range(ML_HEADS)
        hsl = [slice(h * ML_DK, (h + 1) * ML_DK) for h in heads]

        pre_t = _softcap(g_t[j] + gbc)
        hi, mid, lo = _split3(_log_sigmoid(pre_t))
        b_t = _dot(hi, triu) + _dot(mid, triu) + _dot(lo, triu)
        pre_c = _softcap(pf[rows, B_G:B_G + BLK] + gbr)
        hi, mid, lo = _split3(_log_sigmoid(pre_c))
        b_c = _dot(tri, hi) + _dot(tri, mid) + _dot(tri, lo)
        u_c = pltpu.roll(pre_c, ML_HEADS, 1) - b_c

        k_h = [kn[rows, hsl[h]] for h in heads]
        qt_h = [q_t[j, hsl[h], :] for h in heads]
        vte = [jnp.concatenate([v_t[j, hsl[h], :], ones_rows], axis=0) for h in heads]
        kq = [_dot(k_h[h], qt_h[h]) for h in heads]

        sel = jnp.where((t == 0) & (j == 0), 0, 1)
        kb = kband[pl.ds(r0, 2 * BLK), :]
        vtb = jnp.concatenate([sv_t[j], sv_t[j + 1]], axis=1)
        sc2 = []
        for p in range(pairs):
            qa = sq_t[j, p * SWA_HD:(p + 1) * SWA_HD, :]
            qb = sq_t[j, (p + pairs) * SWA_HD:(p + pairs + 1) * SWA_HD, :]
            bd = jnp.concatenate([jnp.concatenate([qa, zq], axis=1),
                                  jnp.concatenate([zq, qb], axis=1)], axis=0)
            sc2.append(_dot(kb, bd))

        s_t, scl, m_row, m_new, a_h, vwte = [], [], [], [], [], []
        for h in heads:
            b_row = b_t[ML_HEADS + h:ML_HEADS + h + 1, :]
            i_row = pre_t[h:h + 1, :]
            u_col = u_c[:, ML_HEADS + h:ML_HEADS + h + 1]
            m_prev = msc[h:h + 1, 0:1]
            log_d = jnp.where(causal_t, b_row + u_col, -jnp.inf)
            inter = b_row + m_prev
            mr = jnp.maximum(inter, jnp.max(log_d, axis=0, keepdims=True))
            m_row.append(mr)
            scl.append(jnp.exp(inter - mr))
            s_t.append((kq[h] * ML_SCALE * jnp.exp(log_d - mr)).astype(BF16))
            b_last = b_row[:, BLK - 1:BLK]
            w_row = b_last - b_row + i_row
            mn = jnp.maximum(b_last + m_prev, jnp.max(w_row, axis=1, keepdims=True))
            m_new.append(mn)
            a_h.append(jnp.exp(b_last + m_prev - mn))
            wexp = jnp.exp(w_row - mn) * ML_SCALE
            vwte.append((vte[h].astype(F32) * wexp).astype(BF16))

        pw, den_p = [], []
        for p in range(pairs):
            sc = jnp.where(prev2, sc2[p][:BLK], sc2[p][BLK:]) + tbl[sel, p]
            snk = snk_rows[p]
            mx = jnp.maximum(jnp.max(sc, axis=0, keepdims=True), snk)
            pe = jnp.exp(sc - mx)
            den_p.append(jnp.sum(pe, axis=0, keepdims=True) + jnp.exp(snk - mx))
            pw.append(jnp.concatenate([jnp.where(prev2, pe, 0.0), jnp.where(prev2, 0.0, pe)],
                                      axis=0).astype(BF16))

        c_old = [ct[h] for h in heads]
        cq = [_dot(c_old[h].astype(BF16), qt_h[h]) for h in heads]
        vs = [_dot(vte[h], s_t[h]) for h in heads]
        upd = [_dot(vwte[h], k_h[h]) for h in heads]
        o2 = [_dot(vtb, pw[p]) for p in range(pairs)]

        for h in heads:
            nd = scl[h] * cq[h] + vs[h]
            num = nd[:ML_DV]
            den = nd[ML_DV:ML_DV + 1]
            hh = num / jnp.maximum(jnp.abs(den), jnp.exp(-m_row[h]))
            hn = hh * lax.rsqrt(jnp.mean(hh * hh, axis=0, keepdims=True) + NORM_EPS)
            og = jax.nn.sigmoid(pf[rows, B_MO + h * ML_DV:B_MO + (h + 1) * ML_DV])
            mix_ref[0, rows, hsl[h]] = (hn.T * mlg[:, hsl[h]] * og).astype(mix_ref.dtype)
            ct[h] = a_h[h] * c_old[h] + upd[h]
            msc[h:h + 1, :] = jnp.broadcast_to(m_new[h], (1, msc.shape[1]))

        o_parts = [None] * SWA_HEADS
        for p in range(pairs):
            on = o2[p] / den_p[p]
            o_parts[p] = on[0:SWA_HD, 0:BLK]
            o_parts[p + pairs] = on[SWA_HD:2 * SWA_HD, BLK:2 * BLK]
        o_t = jnp.concatenate(o_parts, axis=0)
        mix_ref[0, rows, ML_WIDTH:ML_WIDTH + SWA_WIDTH] = o_t.T.astype(mix_ref.dtype)
        return carry

    lax.fori_loop(0, nblk, block, 0)

    kband[0:BLK, :] = kband[tm:tm + BLK, :]
    sv_t[0] = sv_t[nblk]

    @pl.when(t == nt - 1)
    def _write_state():
        for h in range(ML_HEADS):
            c_out[0, h] = ct[h, 0:ML_DV, :].T
            n_out[0, h:h + 1, :] = ct[h, ML_DV:ML_DV + 1, :]
        m_out[0] = msc[...]
        ko_ref[0] = pf[tm - BLK:tm, B_SK:B_SK + BLK]
        vo_ref[0] = pf[tm - BLK:tm, B_SV:B_SV + BLK]


def _prompt_mixer(x, wn, wt, gbr, gbc, mlg, sinks, rb):
    bsz, seq, _ = x.shape
    tm = TM_PROMPT
    nt = seq // tm
    nblk = tm // BLK
    const2 = lambda b, t: (0, 0)
    smem = pl.BlockSpec(memory_space=pltpu.SMEM)
    return pl.pallas_call(
        _prompt_mixer_kernel,
        grid=(bsz, nt),
        in_specs=[
            pl.BlockSpec((1, tm, D_MODEL), lambda b, t: (b, t, 0)),
            pl.BlockSpec((D_MODEL, N_COLS), const2),
            pl.BlockSpec((T_ROWS, D_MODEL), const2),
            pl.BlockSpec((1, BLK), const2),
            pl.BlockSpec((T_GROWS, BLK), const2),
            pl.BlockSpec((1, ML_WIDTH), const2),
            smem, smem,
        ],
        out_specs=[
            pl.BlockSpec((1, tm, D_MODEL), lambda b, t: (b, t, 0)),
            pl.BlockSpec((1, ML_HEADS, ML_DK, ML_DV), lambda b, t: (b, 0, 0, 0)),
            pl.BlockSpec((1, ML_HEADS, ML_DK), lambda b, t: (b, 0, 0)),
            pl.BlockSpec((1, 8, BLK), lambda b, t: (b, 0, 0)),
            pl.BlockSpec((1, BLK, BLK), lambda b, t: (b, 0, 0)),
            pl.BlockSpec((1, BLK, BLK), lambda b, t: (b, 0, 0)),
        ],
        out_shape=[
            jax.ShapeDtypeStruct((bsz, seq, D_MODEL), BF16),
            jax.ShapeDtypeStruct((bsz, ML_HEADS, ML_DK, ML_DV), F32),
            jax.ShapeDtypeStruct((bsz, ML_HEADS, ML_DK), F32),
            jax.ShapeDtypeStruct((bsz, 8, BLK), F32),
            jax.ShapeDtypeStruct((bsz, BLK, BLK), F32),
            jax.ShapeDtypeStruct((bsz, BLK, BLK), F32),
        ],
        scratch_shapes=[
            pltpu.VMEM((tm, ML_HEADS * ML_DK), BF16),
            pltpu.VMEM((tm, B_COLS), F32),
            pltpu.VMEM((tm + BLK, BLK), BF16),
            pltpu.VMEM((nblk, ML_HEADS * ML_DK, BLK), BF16),
            pltpu.VMEM((nblk, ML_WIDTH, BLK), BF16),
            pltpu.VMEM((nblk, SWA_WIDTH, BLK), BF16),
            pltpu.VMEM((nblk + 1, SWA_KV_HEADS * SWA_HD, BLK), BF16),
            pltpu.VMEM((nblk, T_GROWS, BLK), F32),
            pltpu.VMEM((ML_HEADS, STATE_ROWS, ML_DK), F32),
            pltpu.VMEM((8, BLK), F32),
            pltpu.VMEM((2, SWA_HEADS // 2, BLK, 2 * BLK), F32),
        ],
        compiler_params=pltpu.CompilerParams(
            dimension_semantics=("arbitrary", "arbitrary"),
            vmem_limit_bytes=VMEM_LIMIT),
        name="prompt_mixer",
    )(x, wn, wt, gbr, gbc, mlg, sinks, rb)


def _roll_rows(x, shift):
    return pltpu.roll(x, shift % x.shape[0], 0)


def _sample_mixer_kernel(x_ref, wa_ref, wb_ref, gb_ref, mlg_ref, sinks_ref, rb_ref, mrep_ref,
                         c_ref, n_ref, kc_ref, vc_ref,
                         mix_ref, c_out, n_out, mo_ref, ko_ref, vo_ref,
                         pa, pf, tblc, tbln):
    step = pl.program_id(0)
    rows_total = x_ref.shape[0]
    tdec = 4
    nslab = rows_total // SLAB
    qrows = SWA_GROUP * SLAB

    @pl.when(step == 0)
    def _build_bias_tables():
        r = lax.broadcasted_iota(jnp.int32, (qrows, BLK), 0)
        c = lax.broadcasted_iota(jnp.int32, (qrows, BLK), 1)
        hl = r // SLAB
        bl = (r % SLAB) // tdec
        tq = r % tdec
        d_cache = WINDOW + tq - c
        d_new = tq - (c % tdec)
        bk_cache = _t5_bucket(d_cache)
        bk_new = _t5_bucket(d_new)
        ok_cache = c > tq
        ok_new = (c < SLAB) & ((c // tdec) == bl) & (d_new >= 0)
        for g in range(SWA_KV_HEADS):
            acc_c = jnp.zeros((qrows, BLK), F32)
            acc_n = jnp.zeros((qrows, BLK), F32)
            for k in range(REL_BUCKETS):
                for hh in range(SWA_GROUP):
                    val = rb_ref[k, g * SWA_GROUP + hh]
                    acc_c = jnp.where((bk_cache == k) & (hl == hh), val, acc_c)
                    acc_n = jnp.where((bk_new == k) & (hl == hh), val, acc_n)
            tblc[g] = jnp.where(ok_cache, acc_c, NEG_INF)
            tbln[g] = jnp.where(ok_new, acc_n, NEG_INF)

    xb = x_ref[...].astype(BF16)
    pa[...] = _dot(xb, wa_ref[...])
    pf[...] = _dot(xb, wb_ref[...])

    gbias = gb_ref[...]
    mlg = mlg_ref[...]
    r16 = lax.broadcasted_iota(jnp.int32, (SLAB, BLK), 0)
    rr = r16 % tdec
    bl16 = r16 // tdec
    r64 = lax.broadcasted_iota(jnp.int32, (qrows, BLK), 0)
    bl64 = (r64 % SLAB) // tdec
    zeros_pad = jnp.zeros((BLK - SLAB, SWA_HD), BF16)

    def seg_last(x):
        return jnp.where(rr == 3, x,
                         jnp.where(rr == 2, _roll_rows(x, -1),
                                   jnp.where(rr == 1, _roll_rows(x, -2), _roll_rows(x, -3))))

    def seg_max(x):
        m1 = jnp.maximum(x, jnp.where(rr % 2 == 0, _roll_rows(x, -1), _roll_rows(x, 1)))
        return jnp.maximum(m1, jnp.where(rr < 2, _roll_rows(m1, -2), _roll_rows(m1, 2)))

    def slab(si, carry):
        r0 = pl.multiple_of(si * SLAB, SLAB)
        rows = pl.ds(r0, SLAB)
        seq0 = si * (SLAB // tdec)

        pre = _softcap(pf[rows, B_G:B_G + BLK] + gbias)
        logf = _log_sigmoid(pre)
        y = logf + jnp.where(rr >= 1, _roll_rows(logf, 1), 0.0)
        bcum = y + jnp.where(rr >= 2, _roll_rows(y, 2), 0.0)
        ig = pltpu.roll(pre, ML_HEADS, 1)
        m_prev = mrep_ref[rows, :]
        b_minus_i = bcum - ig
        log_d = [jnp.where(rr >= dl, bcum - _roll_rows(b_minus_i, dl), -jnp.inf) for dl in range(tdec)]
        rowmax = functools.reduce(jnp.maximum, log_d)
        inter = bcum + m_prev
        m_row = jnp.maximum(inter, rowmax)
        scl = jnp.exp(inter - m_row)
        dm = [jnp.exp(ld - m_row) for ld in log_d]
        enm = jnp.exp(-m_row)
        b_last = seg_last(bcum)
        w = b_last - bcum + ig
        m_new = jnp.maximum(b_last + m_prev, seg_max(w))
        a = jnp.exp(b_last + m_prev - m_new)
        kws = jnp.exp(w - m_new) * ML_SCALE
        mo_ref[rows, :] = m_new

        for h in range(ML_HEADS):
            hs = slice(h * ML_DK, (h + 1) * ML_DK)
            lane = ML_HEADS + h
            q = pa[rows, A_Q + h * ML_DK:A_Q + (h + 1) * ML_DK]
            k = pa[rows, A_K + h * ML_DK:A_K + (h + 1) * ML_DK]
            v = pa[rows, A_V + h * ML_DV:A_V + (h + 1) * ML_DV]
            bm = jnp.zeros((SLAB, ML_DV), F32)
            sum_s = jnp.zeros((SLAB, 1), F32)
            for dl in range(tdec):
                s_dl = jnp.sum(q * _roll_rows(k, dl), axis=-1, keepdims=True)
                ws = s_dl * ML_SCALE * dm[dl][:, lane:lane + 1]
                bm = bm + ws * _roll_rows(v, dl)
                sum_s = sum_s + ws
            qc = jnp.zeros((SLAB, ML_DV), F32)
            nsel = jnp.zeros((SLAB, ML_DK), F32)
            for bl in range(SLAB // tdec):
                msk = bl16 == bl
                c_b = c_ref[seq0 + bl, h]
                n_b = n_ref[seq0 + bl][h:h + 1, :]
                qc = qc + _dot(jnp.where(msk, q, 0.0).astype(BF16), c_b.astype(BF16))
                nsel = jnp.where(msk, n_b, nsel)
                a_b = a[bl * tdec:bl * tdec + 1, lane:lane + 1]
                kw = jnp.where(msk, k * kws[:, lane:lane + 1], 0.0)
                c_out[seq0 + bl, h] = a_b * c_b + _dot_tn(kw.astype(BF16), v.astype(BF16))
                n_out[seq0 + bl, h:h + 1, :] = a_b * n_b + jnp.sum(kw, axis=0, keepdims=True)
            sclh = scl[:, lane:lane + 1]
            num = sclh * qc + bm
            den = sclh * jnp.sum(q * nsel, axis=-1, keepdims=True) + sum_s
            hh = num / jnp.maximum(jnp.abs(den), enm[:, lane:lane + 1])
            hn = hh * lax.rsqrt(jnp.mean(hh * hh, axis=-1, keepdims=True) + NORM_EPS)
            og = jax.nn.sigmoid(pf[rows, B_MO + h * ML_DV:B_MO + (h + 1) * ML_DV])
            mix_ref[rows, hs] = (hn * mlg[:, hs] * og).astype(mix_ref.dtype)

        knew = pf[rows, B_SK:B_SK + BLK]
        vnew = pf[rows, B_SV:B_SV + BLK]
        for g in range(SWA_KV_HEADS):
            gs = slice(g * SWA_HD, (g + 1) * SWA_HD)
            qg = jnp.concatenate(
                [pa[rows, A_SQ + (g * SWA_GROUP + hl) * SWA_HD:A_SQ + (g * SWA_GROUP + hl + 1) * SWA_HD]
                 for hl in range(SWA_GROUP)], axis=0).astype(BF16)
            kn = jnp.concatenate([knew[:, gs].astype(BF16), zeros_pad], axis=0)
            vn = jnp.concatenate([vnew[:, gs].astype(BF16), zeros_pad], axis=0)
            s_c = jnp.zeros((qrows, BLK), F32)
            for bl in range(SLAB // tdec):
                kc = kc_ref[seq0 + bl][:, gs].astype(BF16)
                s_c = jnp.where(bl64 == bl, _dot_nt(qg, kc), s_c)
            s_c = s_c * SWA_SCALE + tblc[g]
            s_n = _dot_nt(qg, kn) * SWA_SCALE + tbln[g]
            hl64 = r64[:, 0:1] // SLAB
            snk = jnp.zeros((qrows, 1), F32)
            for hl in range(SWA_GROUP):
                snk = jnp.where(hl64 == hl, sinks_ref[g * SWA_GROUP + hl], snk)
            mx = jnp.maximum(jnp.maximum(jnp.max(s_c, axis=-1, keepdims=True),
                                         jnp.max(s_n, axis=-1, keepdims=True)), snk)
            p_c = jnp.exp(s_c - mx)
            p_n = jnp.exp(s_n - mx)
            den = (jnp.sum(p_c, axis=-1, keepdims=True) + jnp.sum(p_n, axis=-1, keepdims=True)
                   + jnp.exp(snk - mx))
            o = _dot(p_n.astype(BF16), vn)
            for bl in range(SLAB // tdec):
                vc = vc_ref[seq0 + bl][:, gs].astype(BF16)
                o = o + _dot(jnp.where(bl64 == bl, p_c, 0.0).astype(BF16), vc)
            o = o / den
            og = jnp.concatenate([o[hl * SLAB:(hl + 1) * SLAB, :] for hl in range(SWA_GROUP)], axis=1)
            c0 = ML_WIDTH + g * SWA_GROUP * SWA_HD
            mix_ref[rows, c0:c0 + SWA_GROUP * SWA_HD] = og.astype(mix_ref.dtype)

        for bl in range(SLAB // tdec):
            ko_ref[seq0 + bl, 0:WINDOW - tdec, :] = kc_ref[seq0 + bl, tdec:WINDOW, :]
            vo_ref[seq0 + bl, 0:WINDOW - tdec, :] = vc_ref[seq0 + bl, tdec:WINDOW, :]
            ko_ref[seq0 + bl, WINDOW - tdec:WINDOW, :] = knew[bl * tdec:(bl + 1) * tdec, :]
            vo_ref[seq0 + bl, WINDOW - tdec:WINDOW, :] = vnew[bl * tdec:(bl + 1) * tdec, :]
        return carry

    lax.fori_loop(0, nslab, slab, 0)


def _sample_mixer(x2d, wa, wb, gb, mlg, sinks, rb, mrep, c0, n0, kc, vc):
    rows = x2d.shape[0]
    tdec = 4
    nseq = rows // tdec
    gbs = SAMPLE_GB
    steps = nseq // gbs
    rb_rows = gbs * tdec
    const2 = lambda i: (0, 0)
    smem = pl.BlockSpec(memory_space=pltpu.SMEM)
    return pl.pallas_call(
        _sample_mixer_kernel,
        grid=(steps,),
        in_specs=[
            pl.BlockSpec((rb_rows, D_MODEL), lambda i: (i, 0)),
            pl.BlockSpec((D_MODEL, A_COLS), const2),
            pl.BlockSpec((D_MODEL, B_COLS), const2),
            pl.BlockSpec((1, BLK), const2),
            pl.BlockSpec((1, ML_WIDTH), const2),
            smem, smem,
            pl.BlockSpec((rb_rows, BLK), lambda i: (i, 0)),
            pl.BlockSpec((gbs, ML_HEADS, ML_DK, ML_DV), lambda i: (i, 0, 0, 0)),
            pl.BlockSpec((gbs, ML_HEADS, ML_DK), lambda i: (i, 0, 0)),
            pl.BlockSpec((gbs, WINDOW, BLK), lambda i: (i, 0, 0)),
            pl.BlockSpec((gbs, WINDOW, BLK), lambda i: (i, 0, 0)),
        ],
        out_specs=[
            pl.BlockSpec((rb_rows, D_MODEL), lambda i: (i, 0)),
            pl.BlockSpec((gbs, ML_HEADS, ML_DK, ML_DV), lambda i: (i, 0, 0, 0)),
            pl.BlockSpec((gbs, ML_HEADS, ML_DK), lambda i: (i, 0, 0)),
            pl.BlockSpec((rb_rows, BLK), lambda i: (i, 0)),
            pl.BlockSpec((gbs, WINDOW, BLK), lambda i: (i, 0, 0)),
            pl.BlockSpec((gbs, WINDOW, BLK), lambda i: (i, 0, 0)),
        ],
        out_shape=[
            jax.ShapeDtypeStruct((rows, D_MODEL), BF16),
            jax.ShapeDtypeStruct((nseq, ML_HEADS, ML_DK, ML_DV), F32),
            jax.ShapeDtypeStruct((nseq, ML_HEADS, ML_DK), F32),
            jax.ShapeDtypeStruct((rows, BLK), F32),
            jax.ShapeDtypeStruct((nseq, WINDOW, BLK), F32),
            jax.ShapeDtypeStruct((nseq, WINDOW, BLK), F32),
        ],
        scratch_shapes=[
            pltpu.VMEM((rb_rows, A_COLS), F32),
            pltpu.VMEM((rb_rows, B_COLS), F32),
            pltpu.VMEM((SWA_KV_HEADS, SWA_GROUP * SLAB, BLK), F32),
            pltpu.VMEM((SWA_KV_HEADS, SWA_GROUP * SLAB, BLK), F32),
        ],
        compiler_params=pltpu.CompilerParams(
            dimension_semantics=("arbitrary",),
            vmem_limit_bytes=VMEM_LIMIT),
        name="sample_mixer",
    )(x2d, wa, wb, gb, mlg, sinks, rb, mrep, c0, n0, kc, vc)


def _dense_kernel(x_ref, mix_ref, wo_ref, g1_ref, b1_ref, wfi_ref, wfo_ref, g2_ref, b2_ref, y_ref):
    x = x_ref[...]
    h1 = _layer_norm(ALPHA * x + _dot(mix_ref[...], wo_ref[...]), g1_ref[...], b1_ref[...])
    h1b = h1.astype(BF16)
    acc = None
    for c0, c1 in zip(FF_SPLITS[:-1], FF_SPLITS[1:]):
        gate = _dot(h1b, wfi_ref[:, c0:c1])
        up = _dot(h1b, wfi_ref[:, D_FF + c0:D_FF + c1])
        act = (gate * jax.nn.sigmoid(gate) * up).astype(BF16)
        part = _dot(act, wfo_ref[c0:c1, :])
        acc = part if acc is None else acc + part
    y_ref[...] = _layer_norm(ALPHA * h1 + acc, g2_ref[...], b2_ref[...])


def _dense(x2d, mix2d, wo, g1, b1, wfi, wfo, g2, b2):
    rows = x2d.shape[0]
    tm = min(TM_DENSE, rows)
    const2 = lambda i: (0, 0)

    def wspec(shape):
        return pl.BlockSpec(shape, const2, pipeline_mode=pl.Buffered(1))

    return pl.pallas_call(
        _dense_kernel,
        grid=(rows // tm,),
        in_specs=[
            pl.BlockSpec((tm, D_MODEL), lambda i: (i, 0)),
            pl.BlockSpec((tm, D_MODEL), lambda i: (i, 0)),
            wspec((D_MODEL, D_MODEL)),
            wspec((1, D_MODEL)), wspec((1, D_MODEL)),
            wspec((D_MODEL, 2 * D_FF)),
            wspec((D_FF, D_MODEL)),
            wspec((1, D_MODEL)), wspec((1, D_MODEL)),
        ],
        out_specs=pl.BlockSpec((tm, D_MODEL), lambda i: (i, 0)),
        out_shape=jax.ShapeDtypeStruct((rows, D_MODEL), F32),
        compiler_params=pltpu.CompilerParams(
            dimension_semantics=("arbitrary",),
            vmem_limit_bytes=VMEM_LIMIT),
        name="dense",
    )(x2d, mix2d, wo, g1, b1, wfi, wfo, g2, b2)


def kernel(x_prompt, x_sample, state_mlstm_C, state_mlstm_n, state_mlstm_m, cache_swa_k, cache_swa_v,
           w_in, b_igate, b_fgate, ml_norm_g, swa_sinks, rel_bias, w_out, ln1_g, ln1_b,
           w_ffn_in, w_ffn_out, ln2_g, ln2_b):
    bp, seq, _ = x_prompt.shape
    bs, tdec, _ = x_sample.shape
    l = 0

    offs = [0]
    for n in IN_SIZES:
        offs.append(offs[-1] + n)
    w = w_in[l].astype(BF16)
    mq, mk, mv, mo, mi, mf, sq, sk, sv = [w[:, offs[i]:offs[i + 1]] for i in range(len(IN_SIZES))]
    gpad = jnp.zeros((D_MODEL, BLK - 2 * ML_HEADS), BF16)
    wb = jnp.concatenate([mo, sk, sv, mi, mf, gpad], axis=1)
    wa = jnp.concatenate([mq, mk, mv, sq], axis=1)
    wn = jnp.concatenate([mk, wb], axis=1)
    wt = jnp.concatenate([mq, mv, sq, sv, mi, mf, gpad[:, :T_GROWS - 2 * ML_HEADS]], axis=1).T
    gvec = jnp.concatenate([b_igate[l], b_fgate[l]]).astype(F32)
    gb = jnp.pad(gvec, (0, BLK - 2 * ML_HEADS))[None, :]
    gbc = jnp.broadcast_to(jnp.pad(gvec, (0, T_GROWS - 2 * ML_HEADS))[:, None], (T_GROWS, BLK))
    mlg = ml_norm_g[l][None, :].astype(F32)
    sinks = swa_sinks[l].astype(F32)
    rb = rel_bias.astype(F32)
    wo = w_out[l].astype(BF16)
    wfi = w_ffn_in[l].astype(BF16)
    wfo = w_ffn_out[l].astype(BF16)
    g1, b1 = ln1_g[l][None, :], ln1_b[l][None, :]
    g2, b2 = ln2_g[l][None, :], ln2_b[l][None, :]

    mix_p, c_p, n_p, m_p, k_p, v_p = _prompt_mixer(x_prompt, wn, wt, gb, gbc, mlg, sinks, rb)
    y_p = _dense(x_prompt.reshape(bp * seq, D_MODEL), mix_p.reshape(bp * seq, D_MODEL),
                 wo, g1, b1, wfi, wfo, g2, b2).reshape(bp, seq, D_MODEL)
    p_m = m_p[:, :ML_HEADS, 0][None]
    p_k = k_p.reshape(1, bp, WINDOW, SWA_KV_HEADS, SWA_HD)
    p_v = v_p.reshape(1, bp, WINDOW, SWA_KV_HEADS, SWA_HD)

    xs = x_sample.reshape(bs * tdec, D_MODEL)
    m0 = state_mlstm_m[l].astype(F32)
    mrep = jnp.pad(jnp.repeat(m0, tdec, axis=0), ((0, 0), (ML_HEADS, BLK - 2 * ML_HEADS)))
    wlen = cache_swa_k.shape[2]
    kc = cache_swa_k[l].reshape(bs, wlen, SWA_KV_HEADS * SWA_HD)
    vc = cache_swa_v[l].reshape(bs, wlen, SWA_KV_HEADS * SWA_HD)
    mix_s, c_s, n_s, mo_s, k_s, v_s = _sample_mixer(
        xs, wa, wb, gb, mlg, sinks, rb, mrep,
        state_mlstm_C[l].astype(F32), state_mlstm_n[l].astype(F32), kc, vc)
    y_s = _dense(xs, mix_s, wo, g1, b1, wfi, wfo, g2, b2).reshape(bs, tdec, D_MODEL)
    s_m = mo_s.reshape(bs, tdec, BLK)[:, 0, ML_HEADS:2 * ML_HEADS][None]
    s_k = k_s.reshape(1, bs, wlen, SWA_KV_HEADS, SWA_HD)
    s_v = v_s.reshape(1, bs, wlen, SWA_KV_HEADS, SWA_HD)

    return (y_p, y_s, c_p[None], n_p[None], p_m, p_k, p_v,
            c_s[None], n_s[None], s_m, s_k, s_v)
```

```python
import functools
import math

import jax
import jax.numpy as jnp
from jax import lax
from jax.experimental import pallas as pl
from jax.experimental.pallas import tpu as pltpu

F32 = jnp.float32
BF16 = jnp.bfloat16

D_MODEL = 1024
ML_HEADS = 4
ML_DK = 128
ML_DV = 128
ML_WIDTH = ML_HEADS * ML_DV
GATE_SOFTCAP = 15.0
SWA_HEADS = 8
SWA_KV_HEADS = 2
SWA_GROUP = SWA_HEADS // SWA_KV_HEADS
SWA_HD = 64
SWA_WIDTH = SWA_HEADS * SWA_HD
WINDOW = 128
REL_BUCKETS = 32
REL_MAX_DIST = 128
D_FF = 2816
DEPTH = 1
ALPHA = (2.0 * DEPTH) ** 0.25
LN_EPS = 1e-5
NORM_EPS = 1e-6
NEG_INF = -1e30
IN_SIZES = (512, 512, 512, 512, 4, 4, 512, 128, 128)
ML_SCALE = ML_DK ** -0.5
SWA_SCALE = SWA_HD ** -0.5

A_Q, A_K, A_V, A_SQ = 0, 512, 1024, 1536
A_COLS = 2048
B_MO, B_SK, B_SV, B_G = 0, 512, 640, 768
B_COLS = 896

N_K = 0
N_B = 512
N_COLS = N_B + B_COLS
T_Q, T_V, T_SQ, T_SV, T_G = 0, 512, 1024, 1536, 1664
T_GROWS = 16
T_ROWS = T_G + T_GROWS
STATE_ROWS = ML_DV + 16

BLK = 128
TM_PROMPT = 512
TM_DENSE = 1024
DENSE_SUB = 512
FF_SPLITS = (0, 768, 1536, 2304, D_FF)
SAMPLE_GB = 16
SLAB = 16
VMEM_LIMIT = 56 * 1024 * 1024


def _softcap(a):
    return GATE_SOFTCAP * jnp.tanh(a / GATE_SOFTCAP)


def _log_sigmoid(x):
    return jnp.minimum(x, 0.0) - jnp.log1p(jnp.exp(-jnp.abs(x)))


def _layer_norm(z, g, b):
    mu = jnp.mean(z, axis=-1, keepdims=True)
    zc = z - mu
    var = jnp.mean(zc * zc, axis=-1, keepdims=True)
    return zc * lax.rsqrt(var + LN_EPS) * g + b


def _dot(a, b):
    return jnp.dot(a, b, preferred_element_type=F32)


def _dot_nt(a, b):
    return lax.dot_general(a, b, (((1,), (1,)), ((), ())), preferred_element_type=F32)


def _dot_tn(a, b):
    return lax.dot_general(a, b, (((0,), (0,)), ((), ())), preferred_element_type=F32)


def _split3(x):
    hi = x.astype(BF16)
    r1 = x - hi.astype(F32)
    mid = r1.astype(BF16)
    lo = (r1 - mid.astype(F32)).astype(BF16)
    return hi, mid, lo


def _t5_bucket(d):
    n = jnp.maximum(d, 0)
    max_exact = REL_BUCKETS // 2
    nf = jnp.maximum(n, 1).astype(F32)
    large = max_exact + (jnp.log(nf / max_exact) / math.log(REL_MAX_DIST / max_exact)
                         * (REL_BUCKETS - max_exact)).astype(jnp.int32)
    large = jnp.minimum(large, REL_BUCKETS - 1)
    return jnp.where(n < max_exact, n, large)


def _prompt_mixer_kernel(x_ref, wn_ref, wt_ref, gbr_ref, gbc_ref, mlg_ref, sinks_ref, rb_ref,
                         mix_ref, c_out, n_out, m_out, ko_ref, vo_ref,
                         kn, pf, kband, q_t, v_t, sq_t, sv_t, g_t, ct, msc, tbl):
    b = pl.program_id(0)
    t = pl.program_id(1)
    nt = pl.num_programs(1)
    tm = x_ref.shape[1]
    nblk = tm // BLK
    pairs = SWA_HEADS // 2

    @pl.when((b == 0) & (t == 0))
    def _build_bias_tables():
        r = lax.broadcasted_iota(jnp.int32, (BLK, 2 * BLK), 0)
        ln = lax.broadcasted_iota(jnp.int32, (BLK, 2 * BLK), 1)
        second = ln >= BLK
        qi = jnp.where(second, ln - BLK, ln)
        prev = r > qi
        d = jnp.where(prev, WINDOW + qi - r, qi - r)
        bucket = _t5_bucket(d)
        for p in range(pairs):
            acc = jnp.zeros((BLK, 2 * BLK), F32)
            for k in range(REL_BUCKETS):
                acc = jnp.where(bucket == k, jnp.where(second, rb_ref[k, p + pairs], rb_ref[k, p]), acc)
            tbl[0, p] = jnp.where(prev, NEG_INF, acc)
            tbl[1, p] = acc

    @pl.when(t == 0)
    def _reset_state():
        ct[...] = jnp.zeros_like(ct)
        msc[...] = jnp.zeros_like(msc)
        kband[0:BLK, :] = jnp.zeros((BLK, kband.shape[1]), kband.dtype)
        sv_t[0] = jnp.zeros(sv_t.shape[1:], sv_t.dtype)

    xb = x_ref[0].astype(BF16)
    kn[...] = _dot(xb, wn_ref[:, N_K:N_K + ML_HEADS * ML_DK]).astype(BF16)
    pf[...] = _dot(xb, wn_ref[:, N_B:N_B + B_COLS])
    kband[BLK:BLK + tm, :] = pf[:, B_SK:B_SK + BLK].astype(BF16)

    def put(dst, r0, nrows, off, scale=None):
        res = _dot_nt(wt_ref[r0:r0 + nrows, :], xb)
        if scale is not None:
            res = res * scale
        for jj in range(nblk):
            dst[jj + off] = res[:, jj * BLK:(jj + 1) * BLK].astype(dst.dtype)

    put(q_t, T_Q, ML_HEADS * ML_DK, 0)
    put(v_t, T_V, ML_WIDTH, 0)
    put(sq_t, T_SQ, SWA_WIDTH, 0, SWA_SCALE)
    put(sv_t, T_SV, SWA_KV_HEADS * SWA_HD, 1)
    put(g_t, T_G, T_GROWS, 0)

    row = lax.broadcasted_iota(jnp.int32, (BLK, BLK), 0)
    col = lax.broadcasted_iota(jnp.int32, (BLK, BLK), 1)
    tri = (row >= col).astype(BF16)
    causal_t = row <= col
    triu = causal_t.astype(BF16)
    r16 = lax.broadcasted_iota(jnp.int32, (16, BLK), 0)
    ones_rows = (r16 == 0).astype(BF16)
    gbr = gbr_ref[...]
    gbc = gbc_ref[...]
    mlg = mlg_ref[...]
    lane2 = lax.broadcasted_iota(jnp.int32, (1, 2 * BLK), 1)
    snk_rows = [jnp.where(lane2 >= BLK, sinks_ref[p + pairs], sinks_ref[p]) for p in range(pairs)]
    zq = jnp.zeros((SWA_HD, BLK), BF16)
    prev2 = jnp.concatenate([row > col, row > col], axis=1)

    def block(j, carry):
        r0 = pl.multiple_of(j * BLK, BLK)
        rows = pl.ds(r0, BLK)
        heads = range(ML_HEADS)
        hsl = [slice(h * ML_DK, (h + 1) * ML_DK) for h in heads]

        pre_t = _softcap(g_t[j] + gbc)
        hi, mid, lo = _split3(_log_sigmoid(pre_t))
        b_t = _dot(hi, triu) + _dot(mid, triu) + _dot(lo, triu)
        pre_c = _softcap(pf[rows, B_G:B_G + BLK] + gbr)
        hi, mid, lo = _split3(_log_sigmoid(pre_c))
        b_c = _dot(tri, hi) + _dot(tri, mid) + _dot(tri, lo)
        u_c = pltpu.roll(pre_c, ML_HEADS, 1) - b_c

        k_h = [kn[rows, hsl[h]] for h in heads]
        qt_h = [q_t[j, hsl[h], :] for h in heads]
        vte = [jnp.concatenate([v_t[j, hsl[h], :], ones_rows], axis=0) for h in heads]
        kq = [_dot(k_h[h], qt_h[h]) for h in heads]

        sel = jnp.where((t == 0) & (j == 0), 0, 1)
        kb = kband[pl.ds(r0, 2 * BLK), :]
        vtb = jnp.concatenate([sv_t[j], sv_t[j + 1]], axis=1)
        sc2 = []
        for p in range(pairs):
            qa = sq_t[j, p * SWA_HD:(p + 1) * SWA_HD, :]
            qb = sq_t[j, (p + pairs) * SWA_HD:(p + pairs + 1) * SWA_HD, :]
            bd = jnp.concatenate([jnp.concatenate([qa, zq], axis=1),
                                  jnp.concatenate([zq, qb], axis=1)], axis=0)
            sc2.append(_dot(kb, bd))

        s_t, scl, m_row, m_new, a_h, vwte = [], [], [], [], [], []
        for h in heads:
            b_row = b_t[ML_HEADS + h:ML_HEADS + h + 1, :]
            i_row = pre_t[h:h + 1, :]
            u_col = u_c[:, ML_HEADS + h:ML_HEADS + h + 1]
            m_prev = msc[h:h + 1, 0:1]
            log_d = jnp.where(causal_t, b_row + u_col, -jnp.inf)
            inter = b_row + m_prev
            mr = jnp.maximum(inter, jnp.max(log_d, axis=0, keepdims=True))
            m_row.append(mr)
            scl.append(jnp.exp(inter - mr))
            s_t.append((kq[h] * ML_SCALE * jnp.exp(log_d - mr)).astype(BF16))
            b_last = b_row[:, BLK - 1:BLK]
            w_row = b_last - b_row + i_row
            mn = jnp.maximum(b_last + m_prev, jnp.max(w_row, axis=1, keepdims=True))
            m_new.append(mn)
            a_h.append(jnp.exp(b_last + m_prev - mn))
            wexp = jnp.exp(w_row - mn) * ML_SCALE
            vwte.append((vte[h].astype(F32) * wexp).astype(BF16))

        pw, den_p = [], []
        for p in range(pairs):
            sc = jnp.where(prev2, sc2[p][:BLK], sc2[p][BLK:]) + tbl[sel, p]
            snk = snk_rows[p]
            mx = jnp.maximum(jnp.max(sc, axis=0, keepdims=True), snk)
            pe = jnp.exp(sc - mx)
            den_p.append(jnp.sum(pe, axis=0, keepdims=True) + jnp.exp(snk - mx))
            pw.append(jnp.concatenate([jnp.where(prev2, pe, 0.0), jnp.where(prev2, 0.0, pe)],
                                      axis=0).astype(BF16))

        c_old = [ct[h] for h in heads]
        cq = [_dot(c_old[h].astype(BF16), qt_h[h]) for h in heads]
        vs = [_dot(vte[h], s_t[h]) for h in heads]
        upd = [_dot(vwte[h], k_h[h]) for h in heads]
        o2 = [_dot(vtb, pw[p]) for p in range(pairs)]

        for h in heads:
            nd = scl[h] * cq[h] + vs[h]
            num = nd[:ML_DV]
            den = nd[ML_DV:ML_DV + 1]
            hh = num / jnp.maximum(jnp.abs(den), jnp.exp(-m_row[h]))
            hn = hh * lax.rsqrt(jnp.mean(hh * hh, axis=0, keepdims=True) + NORM_EPS)
            og = jax.nn.sigmoid(pf[rows, B_MO + h * ML_DV:B_MO + (h + 1) * ML_DV])
            mix_ref[0, rows, hsl[h]] = (hn.T * mlg[:, hsl[h]] * og).astype(mix_ref.dtype)
            ct[h] = a_h[h] * c_old[h] + upd[h]
            msc[h:h + 1, :] = jnp.broadcast_to(m_new[h], (1, msc.shape[1]))

        o_parts = [None] * SWA_HEADS
        for p in range(pairs):
            on = o2[p] / den_p[p]
            o_parts[p] = on[0:SWA_HD, 0:BLK]
            o_parts[p + pairs] = on[SWA_HD:2 * SWA_HD, BLK:2 * BLK]
        o_t = jnp.concatenate(o_parts, axis=0)
        mix_ref[0, rows, ML_WIDTH:ML_WIDTH + SWA_WIDTH] = o_t.T.astype(mix_ref.dtype)
        return carry

    lax.fori_loop(0, nblk, block, 0, unroll=2)

    kband[0:BLK, :] = kband[tm:tm + BLK, :]
    sv_t[0] = sv_t[nblk]

    @pl.when(t == nt - 1)
    def _write_state():
        for h in range(ML_HEADS):
            c_out[0, h] = ct[h, 0:ML_DV, :].T
            n_out[0, h:h + 1, :] = ct[h, ML_DV:ML_DV + 1, :]
        m_out[0] = msc[...]
        ko_ref[0] = pf[tm - BLK:tm, B_SK:B_SK + BLK]
        vo_ref[0] = pf[tm - BLK:tm, B_SV:B_SV + BLK]


def _prompt_mixer(x, wn, wt, gbr, gbc, mlg, sinks, rb):
    bsz, seq, _ = x.shape
    tm = TM_PROMPT
    nt = seq // tm
    nblk = tm // BLK
    const2 = lambda b, t: (0, 0)
    smem = pl.BlockSpec(memory_space=pltpu.SMEM)
    return pl.pallas_call(
        _prompt_mixer_kernel,
        grid=(bsz, nt),
        in_specs=[
            pl.BlockSpec((1, tm, D_MODEL), lambda b, t: (b, t, 0)),
            pl.BlockSpec((D_MODEL, N_COLS), const2),
            pl.BlockSpec((T_ROWS, D_MODEL), const2),
            pl.BlockSpec((1, BLK), const2),
            pl.BlockSpec((T_GROWS, BLK), const2),
            pl.BlockSpec((1, ML_WIDTH), const2),
            smem, smem,
        ],
        out_specs=[
            pl.BlockSpec((1, tm, D_MODEL), lambda b, t: (b, t, 0)),
            pl.BlockSpec((1, ML_HEADS, ML_DK, ML_DV), lambda b, t: (b, 0, 0, 0)),
            pl.BlockSpec((1, ML_HEADS, ML_DK), lambda b, t: (b, 0, 0)),
            pl.BlockSpec((1, 8, BLK), lambda b, t: (b, 0, 0)),
            pl.BlockSpec((1, BLK, BLK), lambda b, t: (b, 0, 0)),
            pl.BlockSpec((1, BLK, BLK), lambda b, t: (b, 0, 0)),
        ],
        out_shape=[
            jax.ShapeDtypeStruct((bsz, seq, D_MODEL), BF16),
            jax.ShapeDtypeStruct((bsz, ML_HEADS, ML_DK, ML_DV), F32),
            jax.ShapeDtypeStruct((bsz, ML_HEADS, ML_DK), F32),
            jax.ShapeDtypeStruct((bsz, 8, BLK), F32),
            jax.ShapeDtypeStruct((bsz, BLK, BLK), F32),
            jax.ShapeDtypeStruct((bsz, BLK, BLK), F32),
        ],
        scratch_shapes=[
            pltpu.VMEM((tm, ML_HEADS * ML_DK), BF16),
            pltpu.VMEM((tm, B_COLS), F32),
            pltpu.VMEM((tm + BLK, BLK), BF16),
            pltpu.VMEM((nblk, ML_HEADS * ML_DK, BLK), BF16),
            pltpu.VMEM((nblk, ML_WIDTH, BLK), BF16),
            pltpu.VMEM((nblk, SWA_WIDTH, BLK), BF16),
            pltpu.VMEM((nblk + 1, SWA_KV_HEADS * SWA_HD, BLK), BF16),
            pltpu.VMEM((nblk, T_GROWS, BLK), F32),
            pltpu.VMEM((ML_HEADS, STATE_ROWS, ML_DK), F32),
            pltpu.VMEM((8, BLK), F32),
            pltpu.VMEM((2, SWA_HEADS // 2, BLK, 2 * BLK), F32),
        ],
        compiler_params=pltpu.CompilerParams(
            dimension_semantics=("arbitrary", "arbitrary"),
            vmem_limit_bytes=VMEM_LIMIT),
        name="prompt_mixer",
    )(x, wn, wt, gbr, gbc, mlg, sinks, rb)


def _roll_rows(x, shift):
    return pltpu.roll(x, shift % x.shape[0], 0)


def _sample_mixer_kernel(x_ref, wa_ref, wb_ref, gb_ref, mlg_ref, sinks_ref, rb_ref, mrep_ref,
                         c_ref, n_ref, kc_ref, vc_ref,
                         mix_ref, c_out, n_out, mo_ref, ko_ref, vo_ref,
                         pa, pf, tblc, tbln):
    step = pl.program_id(0)
    rows_total = x_ref.shape[0]
    tdec = 4
    nslab = rows_total // SLAB
    qrows = SWA_GROUP * SLAB

    @pl.when(step == 0)
    def _build_bias_tables():
        r = lax.broadcasted_iota(jnp.int32, (qrows, BLK), 0)
        c = lax.broadcasted_iota(jnp.int32, (qrows, BLK), 1)
        hl = r // SLAB
        bl = (r % SLAB) // tdec
        tq = r % tdec
        d_cache = WINDOW + tq - c
        d_new = tq - (c % tdec)
        bk_cache = _t5_bucket(d_cache)
        bk_new = _t5_bucket(d_new)
        ok_cache = c > tq
        ok_new = (c < SLAB) & ((c // tdec) == bl) & (d_new >= 0)
        for g in range(SWA_KV_HEADS):
            acc_c = jnp.zeros((qrows, BLK), F32)
            acc_n = jnp.zeros((qrows, BLK), F32)
            for k in range(REL_BUCKETS):
                for hh in range(SWA_GROUP):
                    val = rb_ref[k, g * SWA_GROUP + hh]
                    acc_c = jnp.where((bk_cache == k) & (hl == hh), val, acc_c)
                    acc_n = jnp.where((bk_new == k) & (hl == hh), val, acc_n)
            tblc[g] = jnp.where(ok_cache, acc_c, NEG_INF)
            tbln[g] = jnp.where(ok_new, acc_n, NEG_INF)

    xb = x_ref[...].astype(BF16)
    pa[...] = _dot(xb, wa_ref[...])
    pf[...] = _dot(xb, wb_ref[...])

    gbias = gb_ref[...]
    mlg = mlg_ref[...]
    r16 = lax.broadcasted_iota(jnp.int32, (SLAB, BLK), 0)
    rr = r16 % tdec
    bl16 = r16 // tdec
    r64 = lax.broadcasted_iota(jnp.int32, (qrows, BLK), 0)
    bl64 = (r64 % SLAB) // tdec
    zeros_pad = jnp.zeros((BLK - SLAB, SWA_HD), BF16)

    def seg_last(x):
        return jnp.where(rr == 3, x,
                         jnp.where(rr == 2, _roll_rows(x, -1),
                                   jnp.where(rr == 1, _roll_rows(x, -2), _roll_rows(x, -3))))

    def seg_max(x):
        m1 = jnp.maximum(x, jnp.where(rr % 2 == 0, _roll_rows(x, -1), _roll_rows(x, 1)))
        return jnp.maximum(m1, jnp.where(rr < 2, _roll_rows(m1, -2), _roll_rows(m1, 2)))

    def slab(si, carry):
        r0 = pl.multiple_of(si * SLAB, SLAB)
        rows = pl.ds(r0, SLAB)
        seq0 = si * (SLAB // tdec)

        pre = _softcap(pf[rows, B_G:B_G + BLK] + gbias)
        logf = _log_sigmoid(pre)
        y = logf + jnp.where(rr >= 1, _roll_rows(logf, 1), 0.0)
        bcum = y + jnp.where(rr >= 2, _roll_rows(y, 2), 0.0)
        ig = pltpu.roll(pre, ML_HEADS, 1)
        m_prev = mrep_ref[rows, :]
        b_minus_i = bcum - ig
        log_d = [jnp.where(rr >= dl, bcum - _roll_rows(b_minus_i, dl), -jnp.inf) for dl in range(tdec)]
        rowmax = functools.reduce(jnp.maximum, log_d)
        inter = bcum + m_prev
        m_row = jnp.maximum(inter, rowmax)
        scl = jnp.exp(inter - m_row)
        dm = [jnp.exp(ld - m_row) for ld in log_d]
        enm = jnp.exp(-m_row)
        b_last = seg_last(bcum)
        w = b_last - bcum + ig
        m_new = jnp.maximum(b_last + m_prev, seg_max(w))
        a = jnp.exp(b_last + m_prev - m_new)
        kws = jnp.exp(w - m_new) * ML_SCALE
        mo_ref[rows, :] = m_new

        for h in range(ML_HEADS):
            hs = slice(h * ML_DK, (h + 1) * ML_DK)
            lane = ML_HEADS + h
            q = pa[rows, A_Q + h * ML_DK:A_Q + (h + 1) * ML_DK]
            k = pa[rows, A_K + h * ML_DK:A_K + (h + 1) * ML_DK]
            v = pa[rows, A_V + h * ML_DV:A_V + (h + 1) * ML_DV]
            bm = jnp.zeros((SLAB, ML_DV), F32)
            sum_s = jnp.zeros((SLAB, 1), F32)
            for dl in range(tdec):
                s_dl = jnp.sum(q * _roll_rows(k, dl), axis=-1, keepdims=True)
                ws = s_dl * ML_SCALE * dm[dl][:, lane:lane + 1]
                bm = bm + ws * _roll_rows(v, dl)
                sum_s = sum_s + ws
            qc = jnp.zeros((SLAB, ML_DV), F32)
            nsel = jnp.zeros((SLAB, ML_DK), F32)
            for bl in range(SLAB // tdec):
                msk = bl16 == bl
                c_b = c_ref[seq0 + bl, h]
                n_b = n_ref[seq0 + bl][h:h + 1, :]
                qc = qc + _dot(jnp.where(msk, q, 0.0).astype(BF16), c_b.astype(BF16))
                nsel = jnp.where(msk, n_b, nsel)
                a_b = a[bl * tdec:bl * tdec + 1, lane:lane + 1]
                kw = jnp.where(msk, k * kws[:, lane:lane + 1], 0.0)
                c_out[seq0 + bl, h] = a_b * c_b + _dot_tn(kw.astype(BF16), v.astype(BF16))
                n_out[seq0 + bl, h:h + 1, :] = a_b * n_b + jnp.sum(kw, axis=0, keepdims=True)
            sclh = scl[:, lane:lane + 1]
            num = sclh * qc + bm
            den = sclh * jnp.sum(q * nsel, axis=-1, keepdims=True) + sum_s
            hh = num / jnp.maximum(jnp.abs(den), enm[:, lane:lane + 1])
            hn = hh * lax.rsqrt(jnp.mean(hh * hh, axis=-1, keepdims=True) + NORM_EPS)
            og = jax.nn.sigmoid(pf[rows, B_MO + h * ML_DV:B_MO + (h + 1) * ML_DV])
            mix_ref[rows, hs] = (hn * mlg[:, hs] * og).astype(mix_ref.dtype)

        knew = pf[rows, B_SK:B_SK + BLK]
        vnew = pf[rows, B_SV:B_SV + BLK]
        for g in range(SWA_KV_HEADS):
            gs = slice(g * SWA_HD, (g + 1) * SWA_HD)
            qg = jnp.concatenate(
                [pa[rows, A_SQ + (g * SWA_GROUP + hl) * SWA_HD:A_SQ + (g * SWA_GROUP + hl + 1) * SWA_HD]
                 for hl in range(SWA_GROUP)], axis=0).astype(BF16)
            kn = jnp.concatenate([knew[:, gs].astype(BF16), zeros_pad], axis=0)
            vn = jnp.concatenate([vnew[:, gs].astype(BF16), zeros_pad], axis=0)
            s_c = jnp.zeros((qrows, BLK), F32)
            for bl in range(SLAB // tdec):
                kc = kc_ref[seq0 + bl][:, gs].astype(BF16)
                s_c = jnp.where(bl64 == bl, _dot_nt(qg, kc), s_c)
            s_c = s_c * SWA_SCALE + tblc[g]
            s_n = _dot_nt(qg, kn) * SWA_SCALE + tbln[g]
            hl64 = r64[:, 0:1] // SLAB
            snk = jnp.zeros((qrows, 1), F32)
            for hl in range(SWA_GROUP):
                snk = jnp.where(hl64 == hl, sinks_ref[g * SWA_GROUP + hl], snk)
            mx = jnp.maximum(jnp.maximum(jnp.max(s_c, axis=-1, keepdims=True),
                                         jnp.max(s_n, axis=-1, keepdims=True)), snk)
            p_c = jnp.exp(s_c - mx)
            p_n = jnp.exp(s_n - mx)
            den = (jnp.sum(p_c, axis=-1, keepdims=True) + jnp.sum(p_n, axis=-1, keepdims=True)
                   + jnp.exp(snk - mx))
            o = _dot(p_n.astype(BF16), vn)
            for bl in range(SLAB // tdec):
                vc = vc_ref[seq0 + bl][:, gs].astype(BF16)
                o = o + _dot(jnp.where(bl64 == bl, p_c, 0.0).astype(BF16), vc)
            o = o / den
            og = jnp.concatenate([o[hl * SLAB:(hl + 1) * SLAB, :] for hl in range(SWA_GROUP)], axis=1)
            c0 = ML_WIDTH + g * SWA_GROUP * SWA_HD
            mix_ref[rows, c0:c0 + SWA_GROUP * SWA_HD] = og.astype(mix_ref.dtype)

        for bl in range(SLAB // tdec):
            ko_ref[seq0 + bl, 0:WINDOW - tdec, :] = kc_ref[seq0 + bl, tdec:WINDOW, :]
            vo_ref[seq0 + bl, 0:WINDOW - tdec, :] = vc_ref[seq0 + bl, tdec:WINDOW, :]
            ko_ref[seq0 + bl, WINDOW - tdec:WINDOW, :] = knew[bl * tdec:(bl + 1) * tdec, :]
            vo_ref[seq0 + bl, WINDOW - tdec:WINDOW, :] = vnew[bl * tdec:(bl + 1) * tdec, :]
        return carry

    lax.fori_loop(0, nslab, slab, 0)


def _sample_mixer(x2d, wa, wb, gb, mlg, sinks, rb, mrep, c0, n0, kc, vc):
    rows = x2d.shape[0]
    tdec = 4
    nseq = rows // tdec
    gbs = SAMPLE_GB
    steps = nseq // gbs
    rb_rows = gbs * tdec
    const2 = lambda i: (0, 0)
    smem = pl.BlockSpec(memory_space=pltpu.SMEM)
    return pl.pallas_call(
        _sample_mixer_kernel,
        grid=(steps,),
        in_specs=[
            pl.BlockSpec((rb_rows, D_MODEL), lambda i: (i, 0)),
            pl.BlockSpec((D_MODEL, A_COLS), const2),
            pl.BlockSpec((D_MODEL, B_COLS), const2),
            pl.BlockSpec((1, BLK), const2),
            pl.BlockSpec((1, ML_WIDTH), const2),
            smem, smem,
            pl.BlockSpec((rb_rows, BLK), lambda i: (i, 0)),
            pl.BlockSpec((gbs, ML_HEADS, ML_DK, ML_DV), lambda i: (i, 0, 0, 0)),
            pl.BlockSpec((gbs, ML_HEADS, ML_DK), lambda i: (i, 0, 0)),
            pl.BlockSpec((gbs, WINDOW, BLK), lambda i: (i, 0, 0)),
            pl.BlockSpec((gbs, WINDOW, BLK), lambda i: (i, 0, 0)),
        ],
        out_specs=[
            pl.BlockSpec((rb_rows, D_MODEL), lambda i: (i, 0)),
            pl.BlockSpec((gbs, ML_HEADS, ML_DK, ML_DV), lambda i: (i, 0, 0, 0)),
            pl.BlockSpec((gbs, ML_HEADS, ML_DK), lambda i: (i, 0, 0)),
            pl.BlockSpec((rb_rows, BLK), lambda i: (i, 0)),
            pl.BlockSpec((gbs, WINDOW, BLK), lambda i: (i, 0, 0)),
            pl.BlockSpec((gbs, WINDOW, BLK), lambda i: (i, 0, 0)),
        ],
        out_shape=[
            jax.ShapeDtypeStruct((rows, D_MODEL), BF16),
            jax.ShapeDtypeStruct((nseq, ML_HEADS, ML_DK, ML_DV), F32),
            jax.ShapeDtypeStruct((nseq, ML_HEADS, ML_DK), F32),
            jax.ShapeDtypeStruct((rows, BLK), F32),
            jax.ShapeDtypeStruct((nseq, WINDOW, BLK), F32),
            jax.ShapeDtypeStruct((nseq, WINDOW, BLK), F32),
        ],
        scratch_shapes=[
            pltpu.VMEM((rb_rows, A_COLS), F32),
            pltpu.VMEM((rb_rows, B_COLS), F32),
            pltpu.VMEM((SWA_KV_HEADS, SWA_GROUP * SLAB, BLK), F32),
            pltpu.VMEM((SWA_KV_HEADS, SWA_GROUP * SLAB, BLK), F32),
        ],
        compiler_params=pltpu.CompilerParams(
            dimension_semantics=("arbitrary",),
            vmem_limit_bytes=VMEM_LIMIT),
        name="sample_mixer",
    )(x2d, wa, wb, gb, mlg, sinks, rb, mrep, c0, n0, kc, vc)


def _dense_kernel(x_ref, mix_ref, wo_ref, g1_ref, b1_ref, wfi_ref, wfo_ref, g2_ref, b2_ref, y_ref):
    nsub = x_ref.shape[0] // DENSE_SUB
    subs = [slice(i * DENSE_SUB, (i + 1) * DENSE_SUB) for i in range(nsub)]
    chunks = list(zip(FF_SPLITS[:-1], FF_SPLITS[1:]))
    g1, b1, g2, b2 = g1_ref[...], b1_ref[...], g2_ref[...], b2_ref[...]

    proj = [_dot(mix_ref[s, :], wo_ref[...]) for s in subs]
    h1 = [_layer_norm(ALPHA * x_ref[s, :] + proj[i], g1, b1) for i, s in enumerate(subs)]
    h1b = [h.astype(BF16) for h in h1]
    acc = [None] * nsub
    for c0, c1 in chunks:
        gate = [_dot(h1b[i], wfi_ref[:, c0:c1]) for i in range(nsub)]
        up = [_dot(h1b[i], wfi_ref[:, D_FF + c0:D_FF + c1]) for i in range(nsub)]
        for i in range(nsub):
            act = (gate[i] * jax.nn.sigmoid(gate[i]) * up[i]).astype(BF16)
            part = _dot(act, wfo_ref[c0:c1, :])
            acc[i] = part if acc[i] is None else acc[i] + part
    for i, s in enumerate(subs):
        y_ref[s, :] = _layer_norm(ALPHA * h1[i] + acc[i], g2, b2)


def _dense(x2d, mix2d, wo, g1, b1, wfi, wfo, g2, b2):
    rows = x2d.shape[0]
    tm = min(TM_DENSE, rows)
    const2 = lambda i: (0, 0)

    def wspec(shape):
        return pl.BlockSpec(shape, const2, pipeline_mode=pl.Buffered(1))

    return pl.pallas_call(
        _dense_kernel,
        grid=(rows // tm,),
        in_specs=[
            pl.BlockSpec((tm, D_MODEL), lambda i: (i, 0)),
            pl.BlockSpec((tm, D_MODEL), lambda i: (i, 0)),
            wspec((D_MODEL, D_MODEL)),
            wspec((1, D_MODEL)), wspec((1, D_MODEL)),
            wspec((D_MODEL, 2 * D_FF)),
            wspec((D_FF, D_MODEL)),
            wspec((1, D_MODEL)), wspec((1, D_MODEL)),
        ],
        out_specs=pl.BlockSpec((tm, D_MODEL), lambda i: (i, 0)),
        out_shape=jax.ShapeDtypeStruct((rows, D_MODEL), F32),
        compiler_params=pltpu.CompilerParams(
            dimension_semantics=("arbitrary",),
            vmem_limit_bytes=VMEM_LIMIT),
        name="dense",
    )(x2d, mix2d, wo, g1, b1, wfi, wfo, g2, b2)


def kernel(x_prompt, x_sample, state_mlstm_C, state_mlstm_n, state_mlstm_m, cache_swa_k, cache_swa_v,
           w_in, b_igate, b_fgate, ml_norm_g, swa_sinks, rel_bias, w_out, ln1_g, ln1_b,
           w_ffn_in, w_ffn_out, ln2_g, ln2_b):
    bp, seq, _ = x_prompt.shape
    bs, tdec, _ = x_sample.shape
    l = 0

    offs = [0]
    for n in IN_SIZES:
        offs.append(offs[-1] + n)
    w = w_in[l].astype(BF16)
    mq, mk, mv, mo, mi, mf, sq, sk, sv = [w[:, offs[i]:offs[i + 1]] for i in range(len(IN_SIZES))]
    gpad = jnp.zeros((D_MODEL, BLK - 2 * ML_HEADS), BF16)
    wb = jnp.concatenate([mo, sk, sv, mi, mf, gpad], axis=1)
    wa = jnp.concatenate([mq, mk, mv, sq], axis=1)
    wn = jnp.concatenate([mk, wb], axis=1)
    wt = jnp.concatenate([mq, mv, sq, sv, mi, mf, gpad[:, :T_GROWS - 2 * ML_HEADS]], axis=1).T
    gvec = jnp.concatenate([b_igate[l], b_fgate[l]]).astype(F32)
    gb = jnp.pad(gvec, (0, BLK - 2 * ML_HEADS))[None, :]
    gbc = jnp.broadcast_to(jnp.pad(gvec, (0, T_GROWS - 2 * ML_HEADS))[:, None], (T_GROWS, BLK))
    mlg = ml_norm_g[l][None, :].astype(F32)
    sinks = swa_sinks[l].astype(F32)
    rb = rel_bias.astype(F32)
    wo = w_out[l].astype(BF16)
    wfi = w_ffn_in[l].astype(BF16)
    wfo = w_ffn_out[l].astype(BF16)
    g1, b1 = ln1_g[l][None, :], ln1_b[l][None, :]
    g2, b2 = ln2_g[l][None, :], ln2_b[l][None, :]

    mix_p, c_p, n_p, m_p, k_p, v_p = _prompt_mixer(x_prompt, wn, wt, gb, gbc, mlg, sinks, rb)
    y_p = _dense(x_prompt.reshape(bp * seq, D_MODEL), mix_p.reshape(bp * seq, D_MODEL),
                 wo, g1, b1, wfi, wfo, g2, b2).reshape(bp, seq, D_MODEL)
    p_m = m_p[:, :ML_HEADS, 0][None]
    p_k = k_p.reshape(1, bp, WINDOW, SWA_KV_HEADS, SWA_HD)
    p_v = v_p.reshape(1, bp, WINDOW, SWA_KV_HEADS, SWA_HD)

    xs = x_sample.reshape(bs * tdec, D_MODEL)
    m0 = state_mlstm_m[l].astype(F32)
    mrep = jnp.pad(jnp.repeat(m0, tdec, axis=0), ((0, 0), (ML_HEADS, BLK - 2 * ML_HEADS)))
    wlen = cache_swa_k.shape[2]
    kc = cache_swa_k[l].reshape(bs, wlen, SWA_KV_HEADS * SWA_HD)
    vc = cache_swa_v[l].reshape(bs, wlen, SWA_KV_HEADS * SWA_HD)
    mix_s, c_s, n_s, mo_s, k_s, v_s = _sample_mixer(
        xs, wa, wb, gb, mlg, sinks, rb, mrep,
        state_mlstm_C[l].astype(F32), state_mlstm_n[l].astype(F32), kc, vc)
    y_s = _dense(xs, mix_s, wo, g1, b1, wfi, wfo, g2, b2).reshape(bs, tdec, D_MODEL)
    s_m = mo_s.reshape(bs, tdec, BLK)[:, 0, ML_HEADS:2 * ML_HEADS][None]
    s_k = k_s.reshape(1, bs, wlen, SWA_KV_HEADS, SWA_HD)
    s_v = v_s.reshape(1, bs, wlen, SWA_KV_HEADS, SWA_HD)

    return (y_p, y_s, c_p[None], n_p[None], p_m, p_k, p_v,
            c_s[None], n_s[None], s_m, s_k, s_v)
```

```python
import functools
import math

import jax
import jax.numpy as jnp
from jax import lax
from jax.experimental import pallas as pl
from jax.experimental.pallas import tpu as pltpu

F32 = jnp.float32
BF16 = jnp.bfloat16

D_MODEL = 1024
ML_HEADS = 4
ML_DK = 128
ML_DV = 128
ML_WIDTH = ML_HEADS * ML_DV
GATE_SOFTCAP = 15.0
SWA_HEADS = 8
SWA_KV_HEADS = 2
SWA_GROUP = SWA_HEADS // SWA_KV_HEADS
SWA_HD = 64
SWA_WIDTH = SWA_HEADS * SWA_HD
WINDOW = 128
REL_BUCKETS = 32
REL_MAX_DIST = 128
D_FF = 2816
DEPTH = 1
ALPHA = (2.0 * DEPTH) ** 0.25
LN_EPS = 1e-5
NORM_EPS = 1e-6
NEG_INF = -1e30
IN_SIZES = (512, 512, 512, 512, 4, 4, 512, 128, 128)
ML_SCALE = ML_DK ** -0.5
SWA_SCALE = SWA_HD ** -0.5

BLK = 128

A_Q, A_K, A_V, A_SQ = 0, 512, 1024, 1536
A_COLS = 2048
B_MO, B_SK, B_SV, B_G = 0, 512, 640, 768
B_COLS = 896

N_K = 0
N_B = 512
N_COLS = N_B + B_COLS
T_Q, T_V, T_SQ, T_SV, T_G = 0, 512, 1024, 1536, 1664
T_GROWS = 16
T_COLS = T_G + BLK
W_COLS = N_COLS + T_COLS
STATE_ROWS = ML_DV + 16

TM_PROMPT = 512
TM_DENSE = 1024
DENSE_SUB = 512
FF_SPLITS = (0, 768, 1536, 2304, D_FF)
SAMPLE_GB = 16
SLAB = 16
VMEM_LIMIT = 56 * 1024 * 1024


def _softcap(a):
    return GATE_SOFTCAP * jnp.tanh(a / GATE_SOFTCAP)


def _log_sigmoid(x):
    return jnp.minimum(x, 0.0) - jnp.log1p(jnp.exp(-jnp.abs(x)))


def _layer_norm(z, g, b):
    mu = jnp.mean(z, axis=-1, keepdims=True)
    zc = z - mu
    var = jnp.mean(zc * zc, axis=-1, keepdims=True)
    return zc * lax.rsqrt(var + LN_EPS) * g + b


def _dot(a, b):
    return jnp.dot(a, b, preferred_element_type=F32)


def _dot_nt(a, b):
    return lax.dot_general(a, b, (((1,), (1,)), ((), ())), preferred_element_type=F32)


def _dot_tn(a, b):
    return lax.dot_general(a, b, (((0,), (0,)), ((), ())), preferred_element_type=F32)


def _split3(x):
    hi = x.astype(BF16)
    r1 = x - hi.astype(F32)
    mid = r1.astype(BF16)
    lo = (r1 - mid.astype(F32)).astype(BF16)
    return hi, mid, lo


def _t5_bucket(d):
    n = jnp.maximum(d, 0)
    max_exact = REL_BUCKETS // 2
    nlog = REL_BUCKETS - max_exact
    large = jnp.full(n.shape, max_exact, jnp.int32)
    for k in range(1, nlog):
        thr = math.ceil(max_exact * (REL_MAX_DIST / max_exact) ** (k / nlog))
        large = large + jnp.where(n >= thr, 1, 0)
    return jnp.where(n < max_exact, n, large)


def _prompt_mixer_kernel(x_ref, w_ref, gbr_ref, gbc_ref, mlg_ref, sinks_ref, rb_ref,
                         mix_ref, c_out, n_out, m_out, ko_ref, vo_ref,
                         wt, kn, pf, kband, q_t, v_t, sq_t, sv_t, g_t, ct, msc, tbl):
    b = pl.program_id(0)
    t = pl.program_id(1)
    nt = pl.num_programs(1)
    tm = x_ref.shape[1]
    nblk = tm // BLK
    pairs = SWA_HEADS // 2

    @pl.when((b == 0) & (t == 0))
    def _transpose_feature_major_weights():
        for i in range(T_COLS // BLK):
            wt[i * BLK:(i + 1) * BLK, :] = w_ref[:, N_COLS + i * BLK:N_COLS + (i + 1) * BLK].T

    @pl.when((b == 0) & (t == 0))
    def _build_bias_tables():
        r = lax.broadcasted_iota(jnp.int32, (BLK, 2 * BLK), 0)
        ln = lax.broadcasted_iota(jnp.int32, (BLK, 2 * BLK), 1)
        second = ln >= BLK
        qi = jnp.where(second, ln - BLK, ln)
        prev = r > qi
        d = jnp.where(prev, WINDOW + qi - r, qi - r)
        bucket = _t5_bucket(d)
        for p in range(pairs):
            acc = jnp.zeros((BLK, 2 * BLK), F32)
            for k in range(REL_BUCKETS):
                acc = jnp.where(bucket == k, jnp.where(second, rb_ref[k, p + pairs], rb_ref[k, p]), acc)
            tbl[0, p] = jnp.where(prev, NEG_INF, acc)
            tbl[1, p] = acc

    @pl.when(t == 0)
    def _reset_state():
        ct[...] = jnp.zeros_like(ct)
        msc[...] = jnp.zeros_like(msc)
        kband[0:BLK, :] = jnp.zeros((BLK, kband.shape[1]), kband.dtype)
        sv_t[0] = jnp.zeros(sv_t.shape[1:], sv_t.dtype)

    xb = x_ref[0].astype(BF16)
    kn[...] = _dot(xb, w_ref[:, N_K:N_K + ML_HEADS * ML_DK]).astype(BF16)
    pf[...] = _dot(xb, w_ref[:, N_B:N_B + B_COLS])
    kband[BLK:BLK + tm, :] = pf[:, B_SK:B_SK + BLK].astype(BF16)

    def put(dst, r0, nrows, off, scale=None):
        res = _dot_nt(wt[r0:r0 + nrows, :], xb)
        if scale is not None:
            res = res * scale
        for jj in range(nblk):
            dst[jj + off] = res[:, jj * BLK:(jj + 1) * BLK].astype(dst.dtype)

    put(q_t, T_Q, ML_HEADS * ML_DK, 0)
    put(v_t, T_V, ML_WIDTH, 0)
    put(sq_t, T_SQ, SWA_WIDTH, 0, SWA_SCALE)
    put(sv_t, T_SV, SWA_KV_HEADS * SWA_HD, 1)
    put(g_t, T_G, T_GROWS, 0)

    row = lax.broadcasted_iota(jnp.int32, (BLK, BLK), 0)
    col = lax.broadcasted_iota(jnp.int32, (BLK, BLK), 1)
    tri = (row >= col).astype(BF16)
    causal_t = row <= col
    triu = causal_t.astype(BF16)
    r16 = lax.broadcasted_iota(jnp.int32, (16, BLK), 0)
    ones_rows = (r16 == 0).astype(BF16)
    gbr = gbr_ref[...]
    gbc = gbc_ref[...]
    mlg = mlg_ref[...]
    lane2 = lax.broadcasted_iota(jnp.int32, (1, 2 * BLK), 1)
    snk_rows = [jnp.where(lane2 >= BLK, sinks_ref[p + pairs], sinks_ref[p]) for p in range(pairs)]
    zq = jnp.zeros((SWA_HD, BLK), BF16)
    prev2 = jnp.concatenate([row > col, row > col], axis=1)

    def block(j, carry):
        r0 = pl.multiple_of(j * BLK, BLK)
        rows = pl.ds(r0, BLK)
        heads = range(ML_HEADS)
        hsl = [slice(h * ML_DK, (h + 1) * ML_DK) for h in heads]

        pre_t = _softcap(g_t[j] + gbc)
        hi, mid, lo = _split3(_log_sigmoid(pre_t))
        b_t = _dot(hi, triu) + _dot(mid, triu) + _dot(lo, triu)
        pre_c = _softcap(pf[rows, B_G:B_G + BLK] + gbr)
        hi, mid, lo = _split3(_log_sigmoid(pre_c))
        b_c = _dot(tri, hi) + _dot(tri, mid) + _dot(tri, lo)
        u_c = pltpu.roll(pre_c, ML_HEADS, 1) - b_c

        k_h = [kn[rows, hsl[h]] for h in heads]
        qt_h = [q_t[j, hsl[h], :] for h in heads]
        vte = [jnp.concatenate([v_t[j, hsl[h], :], ones_rows], axis=0) for h in heads]
        kq = [_dot(k_h[h], qt_h[h]) for h in heads]

        sel = jnp.where((t == 0) & (j == 0), 0, 1)
        kb = kband[pl.ds(r0, 2 * BLK), :]
        vtb = jnp.concatenate([sv_t[j], sv_t[j + 1]], axis=1)
        sc2 = []
        for p in range(pairs):
            qa = sq_t[j, p * SWA_HD:(p + 1) * SWA_HD, :]
            qb = sq_t[j, (p + pairs) * SWA_HD:(p + pairs + 1) * SWA_HD, :]
            bd = jnp.concatenate([jnp.concatenate([qa, zq], axis=1),
                                  jnp.concatenate([zq, qb], axis=1)], axis=0)
            sc2.append(_dot(kb, bd))

        s_t, scl, m_row, m_new, a_h, vwte = [], [], [], [], [], []
        for h in heads:
            b_row = b_t[ML_HEADS + h:ML_HEADS + h + 1, :]
            i_row = pre_t[h:h + 1, :]
            u_col = u_c[:, ML_HEADS + h:ML_HEADS + h + 1]
            m_prev = msc[h:h + 1, 0:1]
            log_d = jnp.where(causal_t, b_row + u_col, -jnp.inf)
            inter = b_row + m_prev
            mr = jnp.maximum(inter, jnp.max(log_d, axis=0, keepdims=True))
            m_row.append(mr)
            scl.append(jnp.exp(inter - mr))
            s_t.append((kq[h] * ML_SCALE * jnp.exp(log_d - mr)).astype(BF16))
            b_last = b_row[:, BLK - 1:BLK]
            w_row = b_last - b_row + i_row
            mn = jnp.maximum(b_last + m_prev, jnp.max(w_row, axis=1, keepdims=True))
            m_new.append(mn)
            a_h.append(jnp.exp(b_last + m_prev - mn))
            wexp = jnp.exp(w_row - mn) * ML_SCALE
            vwte.append((vte[h].astype(F32) * wexp).astype(BF16))

        pw, den_p = [], []
        for p in range(pairs):
            sc = jnp.where(prev2, sc2[p][:BLK], sc2[p][BLK:]) + tbl[sel, p]
            snk = snk_rows[p]
            mx = jnp.maximum(jnp.max(sc, axis=0, keepdims=True), snk)
            pe = jnp.exp(sc - mx)
            den_p.append(jnp.sum(pe, axis=0, keepdims=True) + jnp.exp(snk - mx))
            pw.append(jnp.concatenate([jnp.where(prev2, pe, 0.0), jnp.where(prev2, 0.0, pe)],
                                      axis=0).astype(BF16))

        c_old = [ct[h] for h in heads]
        cq = [_dot(c_old[h].astype(BF16), qt_h[h]) for h in heads]
        vs = [_dot(vte[h], s_t[h]) for h in heads]
        upd = [_dot(vwte[h], k_h[h]) for h in heads]
        o2 = [_dot(vtb, pw[p]) for p in range(pairs)]

        for h in heads:
            nd = scl[h] * cq[h] + vs[h]
            num = nd[:ML_DV]
            den = nd[ML_DV:ML_DV + 1]
            hh = num / jnp.maximum(jnp.abs(den), jnp.exp(-m_row[h]))
            hn = hh * lax.rsqrt(jnp.mean(hh * hh, axis=0, keepdims=True) + NORM_EPS)
            og = jax.nn.sigmoid(pf[rows, B_MO + h * ML_DV:B_MO + (h + 1) * ML_DV])
            mix_ref[0, rows, hsl[h]] = (hn.T * mlg[:, hsl[h]] * og).astype(mix_ref.dtype)
            ct[h] = a_h[h] * c_old[h] + upd[h]
            msc[h:h + 1, :] = jnp.broadcast_to(m_new[h], (1, msc.shape[1]))

        o_parts = [None] * SWA_HEADS
        for p in range(pairs):
            on = o2[p] / den_p[p]
            o_parts[p] = on[0:SWA_HD, 0:BLK]
            o_parts[p + pairs] = on[SWA_HD:2 * SWA_HD, BLK:2 * BLK]
        o_t = jnp.concatenate(o_parts, axis=0)
        mix_ref[0, rows, ML_WIDTH:ML_WIDTH + SWA_WIDTH] = o_t.T.astype(mix_ref.dtype)
        return carry

    lax.fori_loop(0, nblk, block, 0, unroll=2)

    kband[0:BLK, :] = kband[tm:tm + BLK, :]
    sv_t[0] = sv_t[nblk]

    @pl.when(t == nt - 1)
    def _write_state():
        for h in range(ML_HEADS):
            c_out[0, h] = ct[h, 0:ML_DV, :].T
            n_out[0, h:h + 1, :] = ct[h, ML_DV:ML_DV + 1, :]
        m_out[0] = msc[...]
        ko_ref[0] = pf[tm - BLK:tm, B_SK:B_SK + BLK]
        vo_ref[0] = pf[tm - BLK:tm, B_SV:B_SV + BLK]


def _prompt_mixer(x, wall, gbr, gbc, mlg, sinks, rb):
    bsz, seq, _ = x.shape
    tm = TM_PROMPT
    nt = seq // tm
    nblk = tm // BLK
    const2 = lambda b, t: (0, 0)
    smem = pl.BlockSpec(memory_space=pltpu.SMEM)
    return pl.pallas_call(
        _prompt_mixer_kernel,
        grid=(bsz, nt),
        in_specs=[
            pl.BlockSpec((1, tm, D_MODEL), lambda b, t: (b, t, 0)),
            pl.BlockSpec((D_MODEL, W_COLS), const2, pipeline_mode=pl.Buffered(1)),
            pl.BlockSpec((1, BLK), const2),
            pl.BlockSpec((T_GROWS, BLK), const2),
            pl.BlockSpec((1, ML_WIDTH), const2),
            smem, smem,
        ],
        out_specs=[
            pl.BlockSpec((1, tm, D_MODEL), lambda b, t: (b, t, 0)),
            pl.BlockSpec((1, ML_HEADS, ML_DK, ML_DV), lambda b, t: (b, 0, 0, 0)),
            pl.BlockSpec((1, ML_HEADS, ML_DK), lambda b, t: (b, 0, 0)),
            pl.BlockSpec((1, 8, BLK), lambda b, t: (b, 0, 0)),
            pl.BlockSpec((1, BLK, BLK), lambda b, t: (b, 0, 0)),
            pl.BlockSpec((1, BLK, BLK), lambda b, t: (b, 0, 0)),
        ],
        out_shape=[
            jax.ShapeDtypeStruct((bsz, seq, D_MODEL), BF16),
            jax.ShapeDtypeStruct((bsz, ML_HEADS, ML_DK, ML_DV), F32),
            jax.ShapeDtypeStruct((bsz, ML_HEADS, ML_DK), F32),
            jax.ShapeDtypeStruct((bsz, 8, BLK), F32),
            jax.ShapeDtypeStruct((bsz, BLK, BLK), F32),
            jax.ShapeDtypeStruct((bsz, BLK, BLK), F32),
        ],
        scratch_shapes=[
            pltpu.VMEM((T_COLS, D_MODEL), BF16),
            pltpu.VMEM((tm, ML_HEADS * ML_DK), BF16),
            pltpu.VMEM((tm, B_COLS), F32),
            pltpu.VMEM((tm + BLK, BLK), BF16),
            pltpu.VMEM((nblk, ML_HEADS * ML_DK, BLK), BF16),
            pltpu.VMEM((nblk, ML_WIDTH, BLK), BF16),
            pltpu.VMEM((nblk, SWA_WIDTH, BLK), BF16),
            pltpu.VMEM((nblk + 1, SWA_KV_HEADS * SWA_HD, BLK), BF16),
            pltpu.VMEM((nblk, T_GROWS, BLK), F32),
            pltpu.VMEM((ML_HEADS, STATE_ROWS, ML_DK), F32),
            pltpu.VMEM((8, BLK), F32),
            pltpu.VMEM((2, SWA_HEADS // 2, BLK, 2 * BLK), F32),
        ],
        compiler_params=pltpu.CompilerParams(
            dimension_semantics=("arbitrary", "arbitrary"),
            vmem_limit_bytes=VMEM_LIMIT),
        name="prompt_mixer",
    )(x, wall, gbr, gbc, mlg, sinks, rb)


def _roll_rows(x, shift):
    return pltpu.roll(x, shift % x.shape[0], 0)


def _sample_mixer_kernel(x_ref, w_ref, gb_ref, mlg_ref, sinks_ref, rb_ref, mrep_ref,
                         c_ref, n_ref, kc_ref, vc_ref,
                         mix_ref, c_out, n_out, mo_ref, ko_ref, vo_ref,
                         pa, pf, tblc, tbln):
    step = pl.program_id(0)
    rows_total = x_ref.shape[0]
    tdec = 4
    nslab = rows_total // SLAB
    qrows = SWA_GROUP * SLAB

    @pl.when(step == 0)
    def _build_bias_tables():
        r = lax.broadcasted_iota(jnp.int32, (qrows, BLK), 0)
        c = lax.broadcasted_iota(jnp.int32, (qrows, BLK), 1)
        hl = r // SLAB
        bl = (r % SLAB) // tdec
        tq = r % tdec
        d_cache = WINDOW + tq - c
        d_new = tq - (c % tdec)
        bk_cache = _t5_bucket(d_cache)
        bk_new = _t5_bucket(d_new)
        ok_cache = c > tq
        ok_new = (c < SLAB) & ((c // tdec) == bl) & (d_new >= 0)
        for g in range(SWA_KV_HEADS):
            acc_c = jnp.zeros((qrows, BLK), F32)
            acc_n = jnp.zeros((qrows, BLK), F32)
            for k in range(REL_BUCKETS):
                for hh in range(SWA_GROUP):
                    val = rb_ref[k, g * SWA_GROUP + hh]
                    acc_c = jnp.where((bk_cache == k) & (hl == hh), val, acc_c)
                    acc_n = jnp.where((bk_new == k) & (hl == hh), val, acc_n)
            tblc[g] = jnp.where(ok_cache, acc_c, NEG_INF)
            tbln[g] = jnp.where(ok_new, acc_n, NEG_INF)

    xb = x_ref[...].astype(BF16)
    tq, tv = N_COLS + T_Q, N_COLS + T_V
    pa[:, A_Q:A_K] = _dot(xb, w_ref[:, tq:tq + ML_HEADS * ML_DK])
    pa[:, A_K:A_V] = _dot(xb, w_ref[:, N_K:N_K + ML_HEADS * ML_DK])
    pa[:, A_V:A_COLS] = _dot(xb, w_ref[:, tv:tv + ML_WIDTH + SWA_WIDTH])
    pf[...] = _dot(xb, w_ref[:, N_B:N_B + B_COLS])

    gbias = gb_ref[...]
    mlg = mlg_ref[...]
    r16 = lax.broadcasted_iota(jnp.int32, (SLAB, BLK), 0)
    rr = r16 % tdec
    bl16 = r16 // tdec
    r64 = lax.broadcasted_iota(jnp.int32, (qrows, BLK), 0)
    bl64 = (r64 % SLAB) // tdec
    zeros_pad = jnp.zeros((BLK - SLAB, SWA_HD), BF16)

    def seg_last(x):
        return jnp.where(rr == 3, x,
                         jnp.where(rr == 2, _roll_rows(x, -1),
                                   jnp.where(rr == 1, _roll_rows(x, -2), _roll_rows(x, -3))))

    def seg_max(x):
        m1 = jnp.maximum(x, jnp.where(rr % 2 == 0, _roll_rows(x, -1), _roll_rows(x, 1)))
        return jnp.maximum(m1, jnp.where(rr < 2, _roll_rows(m1, -2), _roll_rows(m1, 2)))

    def slab(si, carry):
        r0 = pl.multiple_of(si * SLAB, SLAB)
        rows = pl.ds(r0, SLAB)
        seq0 = si * (SLAB // tdec)

        pre = _softcap(pf[rows, B_G:B_G + BLK] + gbias)
        logf = _log_sigmoid(pre)
        y = logf + jnp.where(rr >= 1, _roll_rows(logf, 1), 0.0)
        bcum = y + jnp.where(rr >= 2, _roll_rows(y, 2), 0.0)
        ig = pltpu.roll(pre, ML_HEADS, 1)
        m_prev = mrep_ref[rows, :]
        b_minus_i = bcum - ig
        log_d = [jnp.where(rr >= dl, bcum - _roll_rows(b_minus_i, dl), -jnp.inf) for dl in range(tdec)]
        rowmax = functools.reduce(jnp.maximum, log_d)
        inter = bcum + m_prev
        m_row = jnp.maximum(inter, rowmax)
        scl = jnp.exp(inter - m_row)
        dm = [jnp.exp(ld - m_row) for ld in log_d]
        enm = jnp.exp(-m_row)
        b_last = seg_last(bcum)
        w = b_last - bcum + ig
        m_new = jnp.maximum(b_last + m_prev, seg_max(w))
        a = jnp.exp(b_last + m_prev - m_new)
        kws = jnp.exp(w - m_new) * ML_SCALE
        mo_ref[rows, :] = m_new

        for h in range(ML_HEADS):
            hs = slice(h * ML_DK, (h + 1) * ML_DK)
            lane = ML_HEADS + h
            q = pa[rows, A_Q + h * ML_DK:A_Q + (h + 1) * ML_DK]
            k = pa[rows, A_K + h * ML_DK:A_K + (h + 1) * ML_DK]
            v = pa[rows, A_V + h * ML_DV:A_V + (h + 1) * ML_DV]
            bm = jnp.zeros((SLAB, ML_DV), F32)
            sum_s = jnp.zeros((SLAB, 1), F32)
            for dl in range(tdec):
                s_dl = jnp.sum(q * _roll_rows(k, dl), axis=-1, keepdims=True)
                ws = s_dl * ML_SCALE * dm[dl][:, lane:lane + 1]
                bm = bm + ws * _roll_rows(v, dl)
                sum_s = sum_s + ws
            qc = jnp.zeros((SLAB, ML_DV), F32)
            nsel = jnp.zeros((SLAB, ML_DK), F32)
            for bl in range(SLAB // tdec):
                msk = bl16 == bl
                c_b = c_ref[seq0 + bl, h]
                n_b = n_ref[seq0 + bl][h:h + 1, :]
                qc = qc + _dot(jnp.where(msk, q, 0.0).astype(BF16), c_b.astype(BF16))
                nsel = jnp.where(msk, n_b, nsel)
                a_b = a[bl * tdec:bl * tdec + 1, lane:lane + 1]
                kw = jnp.where(msk, k * kws[:, lane:lane + 1], 0.0)
                c_out[seq0 + bl, h] = a_b * c_b + _dot_tn(kw.astype(BF16), v.astype(BF16))
                n_out[seq0 + bl, h:h + 1, :] = a_b * n_b + jnp.sum(kw, axis=0, keepdims=True)
            sclh = scl[:, lane:lane + 1]
            num = sclh * qc + bm
            den = sclh * jnp.sum(q * nsel, axis=-1, keepdims=True) + sum_s
            hh = num / jnp.maximum(jnp.abs(den), enm[:, lane:lane + 1])
            hn = hh * lax.rsqrt(jnp.mean(hh * hh, axis=-1, keepdims=True) + NORM_EPS)
            og = jax.nn.sigmoid(pf[rows, B_MO + h * ML_DV:B_MO + (h + 1) * ML_DV])
            mix_ref[rows, hs] = (hn * mlg[:, hs] * og).astype(mix_ref.dtype)

        knew = pf[rows, B_SK:B_SK + BLK]
        vnew = pf[rows, B_SV:B_SV + BLK]
        for g in range(SWA_KV_HEADS):
            gs = slice(g * SWA_HD, (g + 1) * SWA_HD)
            qg = jnp.concatenate(
                [pa[rows, A_SQ + (g * SWA_GROUP + hl) * SWA_HD:A_SQ + (g * SWA_GROUP + hl + 1) * SWA_HD]
                 for hl in range(SWA_GROUP)], axis=0).astype(BF16)
            kn = jnp.concatenate([knew[:, gs].astype(BF16), zeros_pad], axis=0)
            vn = jnp.concatenate([vnew[:, gs].astype(BF16), zeros_pad], axis=0)
            s_c = jnp.zeros((qrows, BLK), F32)
            for bl in range(SLAB // tdec):
                kc = kc_ref[seq0 + bl][:, gs].astype(BF16)
                s_c = jnp.where(bl64 == bl, _dot_nt(qg, kc), s_c)
            s_c = s_c * SWA_SCALE + tblc[g]
            s_n = _dot_nt(qg, kn) * SWA_SCALE + tbln[g]
            hl64 = r64[:, 0:1] // SLAB
            snk = jnp.zeros((qrows, 1), F32)
            for hl in range(SWA_GROUP):
                snk = jnp.where(hl64 == hl, sinks_ref[g * SWA_GROUP + hl], snk)
            mx = jnp.maximum(jnp.maximum(jnp.max(s_c, axis=-1, keepdims=True),
                                         jnp.max(s_n, axis=-1, keepdims=True)), snk)
            p_c = jnp.exp(s_c - mx)
            p_n = jnp.exp(s_n - mx)
            den = (jnp.sum(p_c, axis=-1, keepdims=True) + jnp.sum(p_n, axis=-1, keepdims=True)
                   + jnp.exp(snk - mx))
            o = _dot(p_n.astype(BF16), vn)
            for bl in range(SLAB // tdec):
                vc = vc_ref[seq0 + bl][:, gs].astype(BF16)
                o = o + _dot(jnp.where(bl64 == bl, p_c, 0.0).astype(BF16), vc)
            o = o / den
            og = jnp.concatenate([o[hl * SLAB:(hl + 1) * SLAB, :] for hl in range(SWA_GROUP)], axis=1)
            c0 = ML_WIDTH + g * SWA_GROUP * SWA_HD
            mix_ref[rows, c0:c0 + SWA_GROUP * SWA_HD] = og.astype(mix_ref.dtype)

        for bl in range(SLAB // tdec):
            ko_ref[seq0 + bl, 0:WINDOW - tdec, :] = kc_ref[seq0 + bl, tdec:WINDOW, :]
            vo_ref[seq0 + bl, 0:WINDOW - tdec, :] = vc_ref[seq0 + bl, tdec:WINDOW, :]
            ko_ref[seq0 + bl, WINDOW - tdec:WINDOW, :] = knew[bl * tdec:(bl + 1) * tdec, :]
            vo_ref[seq0 + bl, WINDOW - tdec:WINDOW, :] = vnew[bl * tdec:(bl + 1) * tdec, :]
        return carry

    lax.fori_loop(0, nslab, slab, 0)


def _sample_mixer(x2d, wall, gb, mlg, sinks, rb, mrep, c0, n0, kc, vc):
    rows = x2d.shape[0]
    tdec = 4
    nseq = rows // tdec
    gbs = SAMPLE_GB
    steps = nseq // gbs
    rb_rows = gbs * tdec
    const2 = lambda i: (0, 0)
    smem = pl.BlockSpec(memory_space=pltpu.SMEM)
    return pl.pallas_call(
        _sample_mixer_kernel,
        grid=(steps,),
        in_specs=[
            pl.BlockSpec((rb_rows, D_MODEL), lambda i: (i, 0)),
            pl.BlockSpec((D_MODEL, W_COLS), const2, pipeline_mode=pl.Buffered(1)),
            pl.BlockSpec((1, BLK), const2),
            pl.BlockSpec((1, ML_WIDTH), const2),
            smem, smem,
            pl.BlockSpec((rb_rows, BLK), lambda i: (i, 0)),
            pl.BlockSpec((gbs, ML_HEADS, ML_DK, ML_DV), lambda i: (i, 0, 0, 0)),
            pl.BlockSpec((gbs, ML_HEADS, ML_DK), lambda i: (i, 0, 0)),
            pl.BlockSpec((gbs, WINDOW, BLK), lambda i: (i, 0, 0)),
            pl.BlockSpec((gbs, WINDOW, BLK), lambda i: (i, 0, 0)),
        ],
        out_specs=[
            pl.BlockSpec((rb_rows, D_MODEL), lambda i: (i, 0)),
            pl.BlockSpec((gbs, ML_HEADS, ML_DK, ML_DV), lambda i: (i, 0, 0, 0)),
            pl.BlockSpec((gbs, ML_HEADS, ML_DK), lambda i: (i, 0, 0)),
            pl.BlockSpec((rb_rows, BLK), lambda i: (i, 0)),
            pl.BlockSpec((gbs, WINDOW, BLK), lambda i: (i, 0, 0)),
            pl.BlockSpec((gbs, WINDOW, BLK), lambda i: (i, 0, 0)),
        ],
        out_shape=[
            jax.ShapeDtypeStruct((rows, D_MODEL), BF16),
            jax.ShapeDtypeStruct((nseq, ML_HEADS, ML_DK, ML_DV), F32),
            jax.ShapeDtypeStruct((nseq, ML_HEADS, ML_DK), F32),
            jax.ShapeDtypeStruct((rows, BLK), F32),
            jax.ShapeDtypeStruct((nseq, WINDOW, BLK), F32),
            jax.ShapeDtypeStruct((nseq, WINDOW, BLK), F32),
        ],
        scratch_shapes=[
            pltpu.VMEM((rb_rows, A_COLS), F32),
            pltpu.VMEM((rb_rows, B_COLS), F32),
            pltpu.VMEM((SWA_KV_HEADS, SWA_GROUP * SLAB, BLK), F32),
            pltpu.VMEM((SWA_KV_HEADS, SWA_GROUP * SLAB, BLK), F32),
        ],
        compiler_params=pltpu.CompilerParams(
            dimension_semantics=("arbitrary",),
            vmem_limit_bytes=VMEM_LIMIT),
        name="sample_mixer",
    )(x2d, wall, gb, mlg, sinks, rb, mrep, c0, n0, kc, vc)


def _dense_kernel(x_ref, mix_ref, wo_ref, g1_ref, b1_ref, wfi_ref, wfo_ref, g2_ref, b2_ref, y_ref):
    nsub = x_ref.shape[0] // DENSE_SUB
    subs = [slice(i * DENSE_SUB, (i + 1) * DENSE_SUB) for i in range(nsub)]
    chunks = list(zip(FF_SPLITS[:-1], FF_SPLITS[1:]))
    g1, b1, g2, b2 = g1_ref[...], b1_ref[...], g2_ref[...], b2_ref[...]

    proj = [_dot(mix_ref[s, :], wo_ref[...]) for s in subs]
    h1 = [_layer_norm(ALPHA * x_ref[s, :] + proj[i], g1, b1) for i, s in enumerate(subs)]
    h1b = [h.astype(BF16) for h in h1]
    acc = [None] * nsub
    for c0, c1 in chunks:
        gate = [_dot(h1b[i], wfi_ref[:, c0:c1]) for i in range(nsub)]
        up = [_dot(h1b[i], wfi_ref[:, D_FF + c0:D_FF + c1]) for i in range(nsub)]
        for i in range(nsub):
            act = (gate[i] * jax.nn.sigmoid(gate[i]) * up[i]).astype(BF16)
            part = _dot(act, wfo_ref[c0:c1, :])
            acc[i] = part if acc[i] is None else acc[i] + part
    for i, s in enumerate(subs):
        y_ref[s, :] = _layer_norm(ALPHA * h1[i] + acc[i], g2, b2)


def _dense(x2d, mix2d, wo, g1, b1, wfi, wfo, g2, b2):
    rows = x2d.shape[0]
    tm = min(TM_DENSE, rows)
    const2 = lambda i: (0, 0)

    def wspec(shape):
        return pl.BlockSpec(shape, const2, pipeline_mode=pl.Buffered(1))

    return pl.pallas_call(
        _dense_kernel,
        grid=(rows // tm,),
        in_specs=[
            pl.BlockSpec((tm, D_MODEL), lambda i: (i, 0)),
            pl.BlockSpec((tm, D_MODEL), lambda i: (i, 0)),
            wspec((D_MODEL, D_MODEL)),
            wspec((1, D_MODEL)), wspec((1, D_MODEL)),
            wspec((D_MODEL, 2 * D_FF)),
            wspec((D_FF, D_MODEL)),
            wspec((1, D_MODEL)), wspec((1, D_MODEL)),
        ],
        out_specs=pl.BlockSpec((tm, D_MODEL), lambda i: (i, 0)),
        out_shape=jax.ShapeDtypeStruct((rows, D_MODEL), F32),
        compiler_params=pltpu.CompilerParams(
            dimension_semantics=("arbitrary",),
            vmem_limit_bytes=VMEM_LIMIT),
        name="dense",
    )(x2d, mix2d, wo, g1, b1, wfi, wfo, g2, b2)


def kernel(x_prompt, x_sample, state_mlstm_C, state_mlstm_n, state_mlstm_m, cache_swa_k, cache_swa_v,
           w_in, b_igate, b_fgate, ml_norm_g, swa_sinks, rel_bias, w_out, ln1_g, ln1_b,
           w_ffn_in, w_ffn_out, ln2_g, ln2_b):
    bp, seq, _ = x_prompt.shape
    bs, tdec, _ = x_sample.shape
    l = 0

    offs = [0]
    for n in IN_SIZES:
        offs.append(offs[-1] + n)
    w = w_in[l]
    mq, mk, mv, mo, mi, mf, sq, sk, sv = [w[:, offs[i]:offs[i + 1]] for i in range(len(IN_SIZES))]
    gpad = jnp.zeros((D_MODEL, BLK - 2 * ML_HEADS), w.dtype)
    wall = jnp.concatenate([mk, mo, sk, sv, mi, mf, gpad,
                            mq, mv, sq, sv, mi, mf, gpad],
                           axis=1).astype(BF16)
    gvec = jnp.concatenate([b_igate[l], b_fgate[l]]).astype(F32)
    gb = jnp.pad(gvec, (0, BLK - 2 * ML_HEADS))[None, :]
    gbc = jnp.broadcast_to(jnp.pad(gvec, (0, T_GROWS - 2 * ML_HEADS))[:, None], (T_GROWS, BLK))
    mlg = ml_norm_g[l][None, :].astype(F32)
    sinks = swa_sinks[l].astype(F32)
    rb = rel_bias.astype(F32)
    wo = w_out[l].astype(BF16)
    wfi = w_ffn_in[l].astype(BF16)
    wfo = w_ffn_out[l].astype(BF16)
    g1, b1 = ln1_g[l][None, :], ln1_b[l][None, :]
    g2, b2 = ln2_g[l][None, :], ln2_b[l][None, :]

    mix_p, c_p, n_p, m_p, k_p, v_p = _prompt_mixer(x_prompt, wall, gb, gbc, mlg, sinks, rb)
    y_p = _dense(x_prompt.reshape(bp * seq, D_MODEL), mix_p.reshape(bp * seq, D_MODEL),
                 wo, g1, b1, wfi, wfo, g2, b2).reshape(bp, seq, D_MODEL)
    p_m = m_p[:, :ML_HEADS, 0][None]
    p_k = k_p.reshape(1, bp, WINDOW, SWA_KV_HEADS, SWA_HD)
    p_v = v_p.reshape(1, bp, WINDOW, SWA_KV_HEADS, SWA_HD)

    xs = x_sample.reshape(bs * tdec, D_MODEL)
    m0 = state_mlstm_m[l].astype(F32)
    mrep = jnp.pad(jnp.repeat(m0, tdec, axis=0), ((0, 0), (ML_HEADS, BLK - 2 * ML_HEADS)))
    wlen = cache_swa_k.shape[2]
    kc = cache_swa_k[l].reshape(bs, wlen, SWA_KV_HEADS * SWA_HD)
    vc = cache_swa_v[l].reshape(bs, wlen, SWA_KV_HEADS * SWA_HD)
    mix_s, c_s, n_s, mo_s, k_s, v_s = _sample_mixer(
        xs, wall, gb, mlg, sinks, rb, mrep,
        state_mlstm_C[l].astype(F32), state_mlstm_n[l].astype(F32), kc, vc)
    y_s = _dense(xs, mix_s, wo, g1, b1, wfi, wfo, g2, b2).reshape(bs, tdec, D_MODEL)
    s_m = mo_s.reshape(bs, tdec, BLK)[:, 0, ML_HEADS:2 * ML_HEADS][None]
    s_k = k_s.reshape(1, bs, wlen, SWA_KV_HEADS, SWA_HD)
    s_v = v_s.reshape(1, bs, wlen, SWA_KV_HEADS, SWA_HD)

    return (y_p, y_s, c_p[None], n_p[None], p_m, p_k, p_v,
            c_s[None], n_s[None], s_m, s_k, s_v)
```

```python
import functools
import math

import jax
import jax.numpy as jnp
from jax import lax
from jax.experimental import pallas as pl
from jax.experimental.pallas import tpu as pltpu

F32 = jnp.float32
BF16 = jnp.bfloat16

D_MODEL = 1024
ML_HEADS = 4
ML_DK = 128
ML_DV = 128
ML_WIDTH = ML_HEADS * ML_DV
GATE_SOFTCAP = 15.0
SWA_HEADS = 8
SWA_KV_HEADS = 2
SWA_GROUP = SWA_HEADS // SWA_KV_HEADS
SWA_HD = 64
SWA_WIDTH = SWA_HEADS * SWA_HD
WINDOW = 128
REL_BUCKETS = 32
REL_MAX_DIST = 128
D_FF = 2816
DEPTH = 1
ALPHA = (2.0 * DEPTH) ** 0.25
LN_EPS = 1e-5
NORM_EPS = 1e-6
NEG_INF = -1e30
IN_SIZES = (512, 512, 512, 512, 4, 4, 512, 128, 128)
ML_SCALE = ML_DK ** -0.5
SWA_SCALE = SWA_HD ** -0.5

BLK = 128

A_Q, A_K, A_V, A_SQ = 0, 512, 1024, 1536
A_COLS = 2048
B_MO, B_SK, B_SV, B_G = 0, 512, 640, 768
B_COLS = 896

N_K = 0
N_B = 512
N_COLS = N_B + B_COLS
T_Q, T_V, T_SQ, T_SV, T_G = 0, 512, 1024, 1536, 1664
T_GROWS = 16
T_COLS = T_G + BLK
W_COLS = N_COLS + T_COLS
STATE_ROWS = ML_DV + 16

TM_PROMPT = 1024
TM_DENSE = 1024
DENSE_SUB = 512
FF_SPLITS = (0, 768, 1536, 2304, D_FF)
SAMPLE_GB = 16
SLAB = 16
VMEM_LIMIT = 56 * 1024 * 1024


def _softcap(a):
    return GATE_SOFTCAP * jnp.tanh(a / GATE_SOFTCAP)


def _log_sigmoid(x):
    return jnp.minimum(x, 0.0) - jnp.log1p(jnp.exp(-jnp.abs(x)))


def _layer_norm(z, g, b):
    mu = jnp.mean(z, axis=-1, keepdims=True)
    zc = z - mu
    var = jnp.mean(zc * zc, axis=-1, keepdims=True)
    return zc * lax.rsqrt(var + LN_EPS) * g + b


def _dot(a, b):
    return jnp.dot(a, b, preferred_element_type=F32)


def _dot_nt(a, b):
    return lax.dot_general(a, b, (((1,), (1,)), ((), ())), preferred_element_type=F32)


def _dot_tn(a, b):
    return lax.dot_general(a, b, (((0,), (0,)), ((), ())), preferred_element_type=F32)


def _split3(x):
    hi = x.astype(BF16)
    r1 = x - hi.astype(F32)
    mid = r1.astype(BF16)
    lo = (r1 - mid.astype(F32)).astype(BF16)
    return hi, mid, lo


def _t5_bucket(d):
    n = jnp.maximum(d, 0)
    max_exact = REL_BUCKETS // 2
    nlog = REL_BUCKETS - max_exact
    large = jnp.full(n.shape, max_exact, jnp.int32)
    for k in range(1, nlog):
        thr = math.ceil(max_exact * (REL_MAX_DIST / max_exact) ** (k / nlog))
        large = large + jnp.where(n >= thr, 1, 0)
    return jnp.where(n < max_exact, n, large)


def _prompt_mixer_kernel(x_ref, w_ref, gbr_ref, gbc_ref, mlg_ref, sinks_ref, rb_ref,
                         mix_ref, c_out, n_out, m_out, ko_ref, vo_ref,
                         wt, kn, pf, kband, q_t, v_t, sq_t, sv_t, g_t, ct, msc, tbl):
    b = pl.program_id(0)
    t = pl.program_id(1)
    nt = pl.num_programs(1)
    tm = x_ref.shape[1]
    nblk = tm // BLK
    pairs = SWA_HEADS // 2

    @pl.when((b == 0) & (t == 0))
    def _transpose_feature_major_weights():
        for i in range(T_COLS // BLK):
            wt[i * BLK:(i + 1) * BLK, :] = w_ref[:, N_COLS + i * BLK:N_COLS + (i + 1) * BLK].T

    @pl.when((b == 0) & (t == 0))
    def _build_bias_tables():
        r = lax.broadcasted_iota(jnp.int32, (BLK, 2 * BLK), 0)
        ln = lax.broadcasted_iota(jnp.int32, (BLK, 2 * BLK), 1)
        second = ln >= BLK
        qi = jnp.where(second, ln - BLK, ln)
        prev = r > qi
        d = jnp.where(prev, WINDOW + qi - r, qi - r)
        bucket = _t5_bucket(d)
        for p in range(pairs):
            acc = jnp.zeros((BLK, 2 * BLK), F32)
            for k in range(REL_BUCKETS):
                acc = jnp.where(bucket == k, jnp.where(second, rb_ref[k, p + pairs], rb_ref[k, p]), acc)
            tbl[0, p] = jnp.where(prev, NEG_INF, acc)
            tbl[1, p] = acc

    @pl.when(t == 0)
    def _reset_state():
        ct[...] = jnp.zeros_like(ct)
        msc[...] = jnp.zeros_like(msc)
        kband[0:BLK, :] = jnp.zeros((BLK, kband.shape[1]), kband.dtype)
        sv_t[0] = jnp.zeros(sv_t.shape[1:], sv_t.dtype)

    xb = x_ref[0].astype(BF16)
    kn[...] = _dot(xb, w_ref[:, N_K:N_K + ML_HEADS * ML_DK]).astype(BF16)
    pf[...] = _dot(xb, w_ref[:, N_B:N_B + B_COLS])
    kband[BLK:BLK + tm, :] = pf[:, B_SK:B_SK + BLK].astype(BF16)

    def put(dst, r0, nrows, off, scale=None):
        res = _dot_nt(wt[r0:r0 + nrows, :], xb)
        if scale is not None:
            res = res * scale
        for jj in range(nblk):
            dst[jj + off] = res[:, jj * BLK:(jj + 1) * BLK].astype(dst.dtype)

    put(q_t, T_Q, ML_HEADS * ML_DK, 0)
    put(v_t, T_V, ML_WIDTH, 0)
    put(sq_t, T_SQ, SWA_WIDTH, 0, SWA_SCALE)
    put(sv_t, T_SV, SWA_KV_HEADS * SWA_HD, 1)
    put(g_t, T_G, T_GROWS, 0)

    row = lax.broadcasted_iota(jnp.int32, (BLK, BLK), 0)
    col = lax.broadcasted_iota(jnp.int32, (BLK, BLK), 1)
    tri = (row >= col).astype(BF16)
    causal_t = row <= col
    triu = causal_t.astype(BF16)
    r16 = lax.broadcasted_iota(jnp.int32, (16, BLK), 0)
    ones_rows = (r16 == 0).astype(BF16)
    gbr = gbr_ref[...]
    gbc = gbc_ref[...]
    mlg = mlg_ref[...]
    lane2 = lax.broadcasted_iota(jnp.int32, (1, 2 * BLK), 1)
    snk_rows = [jnp.where(lane2 >= BLK, sinks_ref[p + pairs], sinks_ref[p]) for p in range(pairs)]
    zq = jnp.zeros((SWA_HD, BLK), BF16)
    prev2 = jnp.concatenate([row > col, row > col], axis=1)

    def block(j, carry):
        r0 = pl.multiple_of(j * BLK, BLK)
        rows = pl.ds(r0, BLK)
        heads = range(ML_HEADS)
        hsl = [slice(h * ML_DK, (h + 1) * ML_DK) for h in heads]

        pre_t = _softcap(g_t[j] + gbc)
        hi, mid, lo = _split3(_log_sigmoid(pre_t))
        b_t = _dot(hi, triu) + _dot(mid, triu) + _dot(lo, triu)
        pre_c = _softcap(pf[rows, B_G:B_G + BLK] + gbr)
        hi, mid, lo = _split3(_log_sigmoid(pre_c))
        b_c = _dot(tri, hi) + _dot(tri, mid) + _dot(tri, lo)
        u_c = pltpu.roll(pre_c, ML_HEADS, 1) - b_c

        k_h = [kn[rows, hsl[h]] for h in heads]
        qt_h = [q_t[j, hsl[h], :] for h in heads]
        vte = [jnp.concatenate([v_t[j, hsl[h], :], ones_rows], axis=0) for h in heads]
        kq = [_dot(k_h[h], qt_h[h]) for h in heads]

        sel = jnp.where((t == 0) & (j == 0), 0, 1)
        kb = kband[pl.ds(r0, 2 * BLK), :]
        vtb = jnp.concatenate([sv_t[j], sv_t[j + 1]], axis=1)
        sc2 = []
        for p in range(pairs):
            qa = sq_t[j, p * SWA_HD:(p + 1) * SWA_HD, :]
            qb = sq_t[j, (p + pairs) * SWA_HD:(p + pairs + 1) * SWA_HD, :]
            bd = jnp.concatenate([jnp.concatenate([qa, zq], axis=1),
                                  jnp.concatenate([zq, qb], axis=1)], axis=0)
            sc2.append(_dot(kb, bd))

        s_t, scl, m_row, m_new, a_h, vwte = [], [], [], [], [], []
        for h in heads:
            b_row = b_t[ML_HEADS + h:ML_HEADS + h + 1, :]
            i_row = pre_t[h:h + 1, :]
            u_col = u_c[:, ML_HEADS + h:ML_HEADS + h + 1]
            m_prev = msc[h:h + 1, 0:1]
            log_d = jnp.where(causal_t, b_row + u_col, -jnp.inf)
            inter = b_row + m_prev
            mr = jnp.maximum(inter, jnp.max(log_d, axis=0, keepdims=True))
            m_row.append(mr)
            scl.append(jnp.exp(inter - mr))
            s_t.append((kq[h] * ML_SCALE * jnp.exp(log_d - mr)).astype(BF16))
            b_last = b_row[:, BLK - 1:BLK]
            w_row = b_last - b_row + i_row
            mn = jnp.maximum(b_last + m_prev, jnp.max(w_row, axis=1, keepdims=True))
            m_new.append(mn)
            a_h.append(jnp.exp(b_last + m_prev - mn))
            wexp = jnp.exp(w_row - mn) * ML_SCALE
            vwte.append((vte[h].astype(F32) * wexp).astype(BF16))

        pw, den_p = [], []
        for p in range(pairs):
            sc = jnp.where(prev2, sc2[p][:BLK], sc2[p][BLK:]) + tbl[sel, p]
            snk = snk_rows[p]
            mx = jnp.maximum(jnp.max(sc, axis=0, keepdims=True), snk)
            pe = jnp.exp(sc - mx)
            den_p.append(jnp.sum(pe, axis=0, keepdims=True) + jnp.exp(snk - mx))
            pw.append(jnp.concatenate([jnp.where(prev2, pe, 0.0), jnp.where(prev2, 0.0, pe)],
                                      axis=0).astype(BF16))

        c_old = [ct[h] for h in heads]
        cq = [_dot(c_old[h].astype(BF16), qt_h[h]) for h in heads]
        vs = [_dot(vte[h], s_t[h]) for h in heads]
        upd = [_dot(vwte[h], k_h[h]) for h in heads]
        o2 = [_dot(vtb, pw[p]) for p in range(pairs)]

        for h in heads:
            nd = scl[h] * cq[h] + vs[h]
            num = nd[:ML_DV]
            den = nd[ML_DV:ML_DV + 1]
            hh = num / jnp.maximum(jnp.abs(den), jnp.exp(-m_row[h]))
            hn = hh * lax.rsqrt(jnp.mean(hh * hh, axis=0, keepdims=True) + NORM_EPS)
            og = jax.nn.sigmoid(pf[rows, B_MO + h * ML_DV:B_MO + (h + 1) * ML_DV])
            mix_ref[0, rows, hsl[h]] = (hn.T * mlg[:, hsl[h]] * og).astype(mix_ref.dtype)
            ct[h] = a_h[h] * c_old[h] + upd[h]
            msc[h:h + 1, :] = jnp.broadcast_to(m_new[h], (1, msc.shape[1]))

        o_parts = [None] * SWA_HEADS
        for p in range(pairs):
            on = o2[p] / den_p[p]
            o_parts[p] = on[0:SWA_HD, 0:BLK]
            o_parts[p + pairs] = on[SWA_HD:2 * SWA_HD, BLK:2 * BLK]
        o_t = jnp.concatenate(o_parts, axis=0)
        mix_ref[0, rows, ML_WIDTH:ML_WIDTH + SWA_WIDTH] = o_t.T.astype(mix_ref.dtype)
        return carry

    lax.fori_loop(0, nblk, block, 0, unroll=2)

    kband[0:BLK, :] = kband[tm:tm + BLK, :]
    sv_t[0] = sv_t[nblk]

    @pl.when(t == nt - 1)
    def _write_state():
        for h in range(ML_HEADS):
            c_out[0, h] = ct[h, 0:ML_DV, :].T
            n_out[0, h:h + 1, :] = ct[h, ML_DV:ML_DV + 1, :]
        m_out[0] = msc[...]
        ko_ref[0] = pf[tm - BLK:tm, B_SK:B_SK + BLK]
        vo_ref[0] = pf[tm - BLK:tm, B_SV:B_SV + BLK]


def _prompt_mixer(x, wall, gbr, gbc, mlg, sinks, rb):
    bsz, seq, _ = x.shape
    tm = TM_PROMPT
    nt = seq // tm
    nblk = tm // BLK
    const2 = lambda b, t: (0, 0)
    smem = pl.BlockSpec(memory_space=pltpu.SMEM)
    return pl.pallas_call(
        _prompt_mixer_kernel,
        grid=(bsz, nt),
        in_specs=[
            pl.BlockSpec((1, tm, D_MODEL), lambda b, t: (b, t, 0)),
            pl.BlockSpec((D_MODEL, W_COLS), const2, pipeline_mode=pl.Buffered(1)),
            pl.BlockSpec((1, BLK), const2),
            pl.BlockSpec((T_GROWS, BLK), const2),
            pl.BlockSpec((1, ML_WIDTH), const2),
            smem, smem,
        ],
        out_specs=[
            pl.BlockSpec((1, tm, D_MODEL), lambda b, t: (b, t, 0)),
            pl.BlockSpec((1, ML_HEADS, ML_DK, ML_DV), lambda b, t: (b, 0, 0, 0)),
            pl.BlockSpec((1, ML_HEADS, ML_DK), lambda b, t: (b, 0, 0)),
            pl.BlockSpec((1, 8, BLK), lambda b, t: (b, 0, 0)),
            pl.BlockSpec((1, BLK, BLK), lambda b, t: (b, 0, 0)),
            pl.BlockSpec((1, BLK, BLK), lambda b, t: (b, 0, 0)),
        ],
        out_shape=[
            jax.ShapeDtypeStruct((bsz, seq, D_MODEL), BF16),
            jax.ShapeDtypeStruct((bsz, ML_HEADS, ML_DK, ML_DV), F32),
            jax.ShapeDtypeStruct((bsz, ML_HEADS, ML_DK), F32),
            jax.ShapeDtypeStruct((bsz, 8, BLK), F32),
            jax.ShapeDtypeStruct((bsz, BLK, BLK), F32),
            jax.ShapeDtypeStruct((bsz, BLK, BLK), F32),
        ],
        scratch_shapes=[
            pltpu.VMEM((T_COLS, D_MODEL), BF16),
            pltpu.VMEM((tm, ML_HEADS * ML_DK), BF16),
            pltpu.VMEM((tm, B_COLS), F32),
            pltpu.VMEM((tm + BLK, BLK), BF16),
            pltpu.VMEM((nblk, ML_HEADS * ML_DK, BLK), BF16),
            pltpu.VMEM((nblk, ML_WIDTH, BLK), BF16),
            pltpu.VMEM((nblk, SWA_WIDTH, BLK), BF16),
            pltpu.VMEM((nblk + 1, SWA_KV_HEADS * SWA_HD, BLK), BF16),
            pltpu.VMEM((nblk, T_GROWS, BLK), F32),
            pltpu.VMEM((ML_HEADS, STATE_ROWS, ML_DK), F32),
            pltpu.VMEM((8, BLK), F32),
            pltpu.VMEM((2, SWA_HEADS // 2, BLK, 2 * BLK), F32),
        ],
        compiler_params=pltpu.CompilerParams(
            dimension_semantics=("arbitrary", "arbitrary"),
            vmem_limit_bytes=VMEM_LIMIT),
        name="prompt_mixer",
    )(x, wall, gbr, gbc, mlg, sinks, rb)


def _roll_rows(x, shift):
    return pltpu.roll(x, shift % x.shape[0], 0)


def _sample_mixer_kernel(x_ref, w_ref, gb_ref, mlg_ref, sinks_ref, rb_ref, mrep_ref,
                         c_ref, n_ref, kc_ref, vc_ref,
                         mix_ref, c_out, n_out, mo_ref, ko_ref, vo_ref,
                         pa, pf, tblc, tbln):
    step = pl.program_id(0)
    rows_total = x_ref.shape[0]
    tdec = 4
    nslab = rows_total // SLAB
    qrows = SWA_GROUP * SLAB

    @pl.when(step == 0)
    def _build_bias_tables():
        r = lax.broadcasted_iota(jnp.int32, (qrows, BLK), 0)
        c = lax.broadcasted_iota(jnp.int32, (qrows, BLK), 1)
        hl = r // SLAB
        bl = (r % SLAB) // tdec
        tq = r % tdec
        d_cache = WINDOW + tq - c
        d_new = tq - (c % tdec)
        bk_cache = _t5_bucket(d_cache)
        bk_new = _t5_bucket(d_new)
        ok_cache = c > tq
        ok_new = (c < SLAB) & ((c // tdec) == bl) & (d_new >= 0)
        for g in range(SWA_KV_HEADS):
            acc_c = jnp.zeros((qrows, BLK), F32)
            acc_n = jnp.zeros((qrows, BLK), F32)
            for k in range(REL_BUCKETS):
                for hh in range(SWA_GROUP):
                    val = rb_ref[k, g * SWA_GROUP + hh]
                    acc_c = jnp.where((bk_cache == k) & (hl == hh), val, acc_c)
                    acc_n = jnp.where((bk_new == k) & (hl == hh), val, acc_n)
            tblc[g] = jnp.where(ok_cache, acc_c, NEG_INF)
            tbln[g] = jnp.where(ok_new, acc_n, NEG_INF)

    xb = x_ref[...].astype(BF16)
    tq, tv = N_COLS + T_Q, N_COLS + T_V
    pa[:, A_Q:A_K] = _dot(xb, w_ref[:, tq:tq + ML_HEADS * ML_DK])
    pa[:, A_K:A_V] = _dot(xb, w_ref[:, N_K:N_K + ML_HEADS * ML_DK])
    pa[:, A_V:A_COLS] = _dot(xb, w_ref[:, tv:tv + ML_WIDTH + SWA_WIDTH])
    pf[...] = _dot(xb, w_ref[:, N_B:N_B + B_COLS])

    gbias = gb_ref[...]
    mlg = mlg_ref[...]
    r16 = lax.broadcasted_iota(jnp.int32, (SLAB, BLK), 0)
    rr = r16 % tdec
    bl16 = r16 // tdec
    r64 = lax.broadcasted_iota(jnp.int32, (qrows, BLK), 0)
    bl64 = (r64 % SLAB) // tdec
    zeros_pad = jnp.zeros((BLK - SLAB, SWA_HD), BF16)

    def seg_last(x):
        return jnp.where(rr == 3, x,
                         jnp.where(rr == 2, _roll_rows(x, -1),
                                   jnp.where(rr == 1, _roll_rows(x, -2), _roll_rows(x, -3))))

    def seg_max(x):
        m1 = jnp.maximum(x, jnp.where(rr % 2 == 0, _roll_rows(x, -1), _roll_rows(x, 1)))
        return jnp.maximum(m1, jnp.where(rr < 2, _roll_rows(m1, -2), _roll_rows(m1, 2)))

    def slab(si, carry):
        r0 = pl.multiple_of(si * SLAB, SLAB)
        rows = pl.ds(r0, SLAB)
        seq0 = si * (SLAB // tdec)

        pre = _softcap(pf[rows, B_G:B_G + BLK] + gbias)
        logf = _log_sigmoid(pre)
        y = logf + jnp.where(rr >= 1, _roll_rows(logf, 1), 0.0)
        bcum = y + jnp.where(rr >= 2, _roll_rows(y, 2), 0.0)
        ig = pltpu.roll(pre, ML_HEADS, 1)
        m_prev = mrep_ref[rows, :]
        b_minus_i = bcum - ig
        log_d = [jnp.where(rr >= dl, bcum - _roll_rows(b_minus_i, dl), -jnp.inf) for dl in range(tdec)]
        rowmax = functools.reduce(jnp.maximum, log_d)
        inter = bcum + m_prev
        m_row = jnp.maximum(inter, rowmax)
        scl = jnp.exp(inter - m_row)
        dm = [jnp.exp(ld - m_row) for ld in log_d]
        enm = jnp.exp(-m_row)
        b_last = seg_last(bcum)
        w = b_last - bcum + ig
        m_new = jnp.maximum(b_last + m_prev, seg_max(w))
        a = jnp.exp(b_last + m_prev - m_new)
        kws = jnp.exp(w - m_new) * ML_SCALE
        mo_ref[rows, :] = m_new

        for h in range(ML_HEADS):
            hs = slice(h * ML_DK, (h + 1) * ML_DK)
            lane = ML_HEADS + h
            q = pa[rows, A_Q + h * ML_DK:A_Q + (h + 1) * ML_DK]
            k = pa[rows, A_K + h * ML_DK:A_K + (h + 1) * ML_DK]
            v = pa[rows, A_V + h * ML_DV:A_V + (h + 1) * ML_DV]
            bm = jnp.zeros((SLAB, ML_DV), F32)
            sum_s = jnp.zeros((SLAB, 1), F32)
            for dl in range(tdec):
                s_dl = jnp.sum(q * _roll_rows(k, dl), axis=-1, keepdims=True)
                ws = s_dl * ML_SCALE * dm[dl][:, lane:lane + 1]
                bm = bm + ws * _roll_rows(v, dl)
                sum_s = sum_s + ws
            qc = jnp.zeros((SLAB, ML_DV), F32)
            nsel = jnp.zeros((SLAB, ML_DK), F32)
            for bl in range(SLAB // tdec):
                msk = bl16 == bl
                c_b = c_ref[seq0 + bl, h]
                n_b = n_ref[seq0 + bl][h:h + 1, :]
                qc = qc + _dot(jnp.where(msk, q, 0.0).astype(BF16), c_b.astype(BF16))
                nsel = jnp.where(msk, n_b, nsel)
                a_b = a[bl * tdec:bl * tdec + 1, lane:lane + 1]
                kw = jnp.where(msk, k * kws[:, lane:lane + 1], 0.0)
                c_out[seq0 + bl, h] = a_b * c_b + _dot_tn(kw.astype(BF16), v.astype(BF16))
                n_out[seq0 + bl, h:h + 1, :] = a_b * n_b + jnp.sum(kw, axis=0, keepdims=True)
            sclh = scl[:, lane:lane + 1]
            num = sclh * qc + bm
            den = sclh * jnp.sum(q * nsel, axis=-1, keepdims=True) + sum_s
            hh = num / jnp.maximum(jnp.abs(den), enm[:, lane:lane + 1])
            hn = hh * lax.rsqrt(jnp.mean(hh * hh, axis=-1, keepdims=True) + NORM_EPS)
            og = jax.nn.sigmoid(pf[rows, B_MO + h * ML_DV:B_MO + (h + 1) * ML_DV])
            mix_ref[rows, hs] = (hn * mlg[:, hs] * og).astype(mix_ref.dtype)

        knew = pf[rows, B_SK:B_SK + BLK]
        vnew = pf[rows, B_SV:B_SV + BLK]
        for g in range(SWA_KV_HEADS):
            gs = slice(g * SWA_HD, (g + 1) * SWA_HD)
            qg = jnp.concatenate(
                [pa[rows, A_SQ + (g * SWA_GROUP + hl) * SWA_HD:A_SQ + (g * SWA_GROUP + hl + 1) * SWA_HD]
                 for hl in range(SWA_GROUP)], axis=0).astype(BF16)
            kn = jnp.concatenate([knew[:, gs].astype(BF16), zeros_pad], axis=0)
            vn = jnp.concatenate([vnew[:, gs].astype(BF16), zeros_pad], axis=0)
            s_c = jnp.zeros((qrows, BLK), F32)
            for bl in range(SLAB // tdec):
                kc = kc_ref[seq0 + bl][:, gs].astype(BF16)
                s_c = jnp.where(bl64 == bl, _dot_nt(qg, kc), s_c)
            s_c = s_c * SWA_SCALE + tblc[g]
            s_n = _dot_nt(qg, kn) * SWA_SCALE + tbln[g]
            hl64 = r64[:, 0:1] // SLAB
            snk = jnp.zeros((qrows, 1), F32)
            for hl in range(SWA_GROUP):
                snk = jnp.where(hl64 == hl, sinks_ref[g * SWA_GROUP + hl], snk)
            mx = jnp.maximum(jnp.maximum(jnp.max(s_c, axis=-1, keepdims=True),
                                         jnp.max(s_n, axis=-1, keepdims=True)), snk)
            p_c = jnp.exp(s_c - mx)
            p_n = jnp.exp(s_n - mx)
            den = (jnp.sum(p_c, axis=-1, keepdims=True) + jnp.sum(p_n, axis=-1, keepdims=True)
                   + jnp.exp(snk - mx))
            o = _dot(p_n.astype(BF16), vn)
            for bl in range(SLAB // tdec):
                vc = vc_ref[seq0 + bl][:, gs].astype(BF16)
                o = o + _dot(jnp.where(bl64 == bl, p_c, 0.0).astype(BF16), vc)
            o = o / den
            og = jnp.concatenate([o[hl * SLAB:(hl + 1) * SLAB, :] for hl in range(SWA_GROUP)], axis=1)
            c0 = ML_WIDTH + g * SWA_GROUP * SWA_HD
            mix_ref[rows, c0:c0 + SWA_GROUP * SWA_HD] = og.astype(mix_ref.dtype)

        for bl in range(SLAB // tdec):
            ko_ref[seq0 + bl, 0:WINDOW - tdec, :] = kc_ref[seq0 + bl, tdec:WINDOW, :]
            vo_ref[seq0 + bl, 0:WINDOW - tdec, :] = vc_ref[seq0 + bl, tdec:WINDOW, :]
            ko_ref[seq0 + bl, WINDOW - tdec:WINDOW, :] = knew[bl * tdec:(bl + 1) * tdec, :]
            vo_ref[seq0 + bl, WINDOW - tdec:WINDOW, :] = vnew[bl * tdec:(bl + 1) * tdec, :]
        return carry

    lax.fori_loop(0, nslab, slab, 0)


def _sample_mixer(x2d, wall, gb, mlg, sinks, rb, mrep, c0, n0, kc, vc):
    rows = x2d.shape[0]
    tdec = 4
    nseq = rows // tdec
    gbs = SAMPLE_GB
    steps = nseq // gbs
    rb_rows = gbs * tdec
    const2 = lambda i: (0, 0)
    smem = pl.BlockSpec(memory_space=pltpu.SMEM)
    return pl.pallas_call(
        _sample_mixer_kernel,
        grid=(steps,),
        in_specs=[
            pl.BlockSpec((rb_rows, D_MODEL), lambda i: (i, 0)),
            pl.BlockSpec((D_MODEL, W_COLS), const2, pipeline_mode=pl.Buffered(1)),
            pl.BlockSpec((1, BLK), const2),
            pl.BlockSpec((1, ML_WIDTH), const2),
            smem, smem,
            pl.BlockSpec((rb_rows, BLK), lambda i: (i, 0)),
            pl.BlockSpec((gbs, ML_HEADS, ML_DK, ML_DV), lambda i: (i, 0, 0, 0)),
            pl.BlockSpec((gbs, ML_HEADS, ML_DK), lambda i: (i, 0, 0)),
            pl.BlockSpec((gbs, WINDOW, BLK), lambda i: (i, 0, 0)),
            pl.BlockSpec((gbs, WINDOW, BLK), lambda i: (i, 0, 0)),
        ],
        out_specs=[
            pl.BlockSpec((rb_rows, D_MODEL), lambda i: (i, 0)),
            pl.BlockSpec((gbs, ML_HEADS, ML_DK, ML_DV), lambda i: (i, 0, 0, 0)),
            pl.BlockSpec((gbs, ML_HEADS, ML_DK), lambda i: (i, 0, 0)),
            pl.BlockSpec((rb_rows, BLK), lambda i: (i, 0)),
            pl.BlockSpec((gbs, WINDOW, BLK), lambda i: (i, 0, 0)),
            pl.BlockSpec((gbs, WINDOW, BLK), lambda i: (i, 0, 0)),
        ],
        out_shape=[
            jax.ShapeDtypeStruct((rows, D_MODEL), BF16),
            jax.ShapeDtypeStruct((nseq, ML_HEADS, ML_DK, ML_DV), F32),
            jax.ShapeDtypeStruct((nseq, ML_HEADS, ML_DK), F32),
            jax.ShapeDtypeStruct((rows, BLK), F32),
            jax.ShapeDtypeStruct((nseq, WINDOW, BLK), F32),
            jax.ShapeDtypeStruct((nseq, WINDOW, BLK), F32),
        ],
        scratch_shapes=[
            pltpu.VMEM((rb_rows, A_COLS), F32),
            pltpu.VMEM((rb_rows, B_COLS), F32),
            pltpu.VMEM((SWA_KV_HEADS, SWA_GROUP * SLAB, BLK), F32),
            pltpu.VMEM((SWA_KV_HEADS, SWA_GROUP * SLAB, BLK), F32),
        ],
        compiler_params=pltpu.CompilerParams(
            dimension_semantics=("arbitrary",),
            vmem_limit_bytes=VMEM_LIMIT),
        name="sample_mixer",
    )(x2d, wall, gb, mlg, sinks, rb, mrep, c0, n0, kc, vc)


def _dense_kernel(x_ref, mix_ref, wo_ref, g1_ref, b1_ref, wfi_ref, wfo_ref, g2_ref, b2_ref, y_ref):
    nsub = x_ref.shape[0] // DENSE_SUB
    subs = [slice(i * DENSE_SUB, (i + 1) * DENSE_SUB) for i in range(nsub)]
    chunks = list(zip(FF_SPLITS[:-1], FF_SPLITS[1:]))
    g1, b1, g2, b2 = g1_ref[...], b1_ref[...], g2_ref[...], b2_ref[...]

    proj = [_dot(mix_ref[s, :], wo_ref[...]) for s in subs]
    h1 = [_layer_norm(ALPHA * x_ref[s, :] + proj[i], g1, b1) for i, s in enumerate(subs)]
    h1b = [h.astype(BF16) for h in h1]
    acc = [None] * nsub
    for c0, c1 in chunks:
        gate = [_dot(h1b[i], wfi_ref[:, c0:c1]) for i in range(nsub)]
        up = [_dot(h1b[i], wfi_ref[:, D_FF + c0:D_FF + c1]) for i in range(nsub)]
        for i in range(nsub):
            act = (gate[i] * jax.nn.sigmoid(gate[i]) * up[i]).astype(BF16)
            part = _dot(act, wfo_ref[c0:c1, :])
            acc[i] = part if acc[i] is None else acc[i] + part
    for i, s in enumerate(subs):
        y_ref[s, :] = _layer_norm(ALPHA * h1[i] + acc[i], g2, b2)


def _dense(x2d, mix2d, wo, g1, b1, wfi, wfo, g2, b2):
    rows = x2d.shape[0]
    tm = min(TM_DENSE, rows)
    const2 = lambda i: (0, 0)

    def wspec(shape):
        return pl.BlockSpec(shape, const2, pipeline_mode=pl.Buffered(1))

    return pl.pallas_call(
        _dense_kernel,
        grid=(rows // tm,),
        in_specs=[
            pl.BlockSpec((tm, D_MODEL), lambda i: (i, 0)),
            pl.BlockSpec((tm, D_MODEL), lambda i: (i, 0)),
            wspec((D_MODEL, D_MODEL)),
            wspec((1, D_MODEL)), wspec((1, D_MODEL)),
            wspec((D_MODEL, 2 * D_FF)),
            wspec((D_FF, D_MODEL)),
            wspec((1, D_MODEL)), wspec((1, D_MODEL)),
        ],
        out_specs=pl.BlockSpec((tm, D_MODEL), lambda i: (i, 0)),
        out_shape=jax.ShapeDtypeStruct((rows, D_MODEL), F32),
        compiler_params=pltpu.CompilerParams(
            dimension_semantics=("arbitrary",),
            vmem_limit_bytes=VMEM_LIMIT),
        name="dense",
    )(x2d, mix2d, wo, g1, b1, wfi, wfo, g2, b2)


def kernel(x_prompt, x_sample, state_mlstm_C, state_mlstm_n, state_mlstm_m, cache_swa_k, cache_swa_v,
           w_in, b_igate, b_fgate, ml_norm_g, swa_sinks, rel_bias, w_out, ln1_g, ln1_b,
           w_ffn_in, w_ffn_out, ln2_g, ln2_b):
    bp, seq, _ = x_prompt.shape
    bs, tdec, _ = x_sample.shape
    l = 0

    offs = [0]
    for n in IN_SIZES:
        offs.append(offs[-1] + n)
    w = w_in[l]
    mq, mk, mv, mo, mi, mf, sq, sk, sv = [w[:, offs[i]:offs[i + 1]] for i in range(len(IN_SIZES))]
    gpad = jnp.zeros((D_MODEL, BLK - 2 * ML_HEADS), w.dtype)
    wall = jnp.concatenate([mk, mo, sk, sv, mi, mf, gpad,
                            mq, mv, sq, sv, mi, mf, gpad],
                           axis=1).astype(BF16)
    gvec = jnp.concatenate([b_igate[l], b_fgate[l]]).astype(F32)
    gb = jnp.pad(gvec, (0, BLK - 2 * ML_HEADS))[None, :]
    gbc = jnp.broadcast_to(jnp.pad(gvec, (0, T_GROWS - 2 * ML_HEADS))[:, None], (T_GROWS, BLK))
    mlg = ml_norm_g[l][None, :].astype(F32)
    sinks = swa_sinks[l].astype(F32)
    rb = rel_bias.astype(F32)
    wo = w_out[l].astype(BF16)
    wfi = w_ffn_in[l].astype(BF16)
    wfo = w_ffn_out[l].astype(BF16)
    g1, b1 = ln1_g[l][None, :], ln1_b[l][None, :]
    g2, b2 = ln2_g[l][None, :], ln2_b[l][None, :]

    mix_p, c_p, n_p, m_p, k_p, v_p = _prompt_mixer(x_prompt, wall, gb, gbc, mlg, sinks, rb)
    y_p = _dense(x_prompt.reshape(bp * seq, D_MODEL), mix_p.reshape(bp * seq, D_MODEL),
                 wo, g1, b1, wfi, wfo, g2, b2).reshape(bp, seq, D_MODEL)
    p_m = m_p[:, :ML_HEADS, 0][None]
    p_k = k_p.reshape(1, bp, WINDOW, SWA_KV_HEADS, SWA_HD)
    p_v = v_p.reshape(1, bp, WINDOW, SWA_KV_HEADS, SWA_HD)

    xs = x_sample.reshape(bs * tdec, D_MODEL)
    m0 = state_mlstm_m[l].astype(F32)
    mrep = jnp.pad(jnp.repeat(m0, tdec, axis=0), ((0, 0), (ML_HEADS, BLK - 2 * ML_HEADS)))
    wlen = cache_swa_k.shape[2]
    kc = cache_swa_k[l].reshape(bs, wlen, SWA_KV_HEADS * SWA_HD)
    vc = cache_swa_v[l].reshape(bs, wlen, SWA_KV_HEADS * SWA_HD)
    mix_s, c_s, n_s, mo_s, k_s, v_s = _sample_mixer(
        xs, wall, gb, mlg, sinks, rb, mrep,
        state_mlstm_C[l].astype(F32), state_mlstm_n[l].astype(F32), kc, vc)
    y_s = _dense(xs, mix_s, wo, g1, b1, wfi, wfo, g2, b2).reshape(bs, tdec, D_MODEL)
    s_m = mo_s.reshape(bs, tdec, BLK)[:, 0, ML_HEADS:2 * ML_HEADS][None]
    s_k = k_s.reshape(1, bs, wlen, SWA_KV_HEADS, SWA_HD)
    s_v = v_s.reshape(1, bs, wlen, SWA_KV_HEADS, SWA_HD)

    return (y_p, y_s, c_p[None], n_p[None], p_m, p_k, p_v,
            c_s[None], n_s[None], s_m, s_k, s_v)
```

```python
import functools
import math

import jax
import jax.numpy as jnp
from jax import lax
from jax.experimental import pallas as pl
from jax.experimental.pallas import tpu as pltpu

F32 = jnp.float32
BF16 = jnp.bfloat16

D_MODEL = 1024
ML_HEADS = 4
ML_DK = 128
ML_DV = 128
ML_WIDTH = ML_HEADS * ML_DV
GATE_SOFTCAP = 15.0
SWA_HEADS = 8
SWA_KV_HEADS = 2
SWA_GROUP = SWA_HEADS // SWA_KV_HEADS
SWA_HD = 64
SWA_WIDTH = SWA_HEADS * SWA_HD
WINDOW = 128
REL_BUCKETS = 32
REL_MAX_DIST = 128
D_FF = 2816
DEPTH = 1
ALPHA = (2.0 * DEPTH) ** 0.25
LN_EPS = 1e-5
NORM_EPS = 1e-6
NEG_INF = -1e30
IN_SIZES = (512, 512, 512, 512, 4, 4, 512, 128, 128)
ML_SCALE = ML_DK ** -0.5
LOG_ML_SCALE = math.log(ML_SCALE)
SWA_SCALE = SWA_HD ** -0.5

BLK = 128

A_Q, A_K, A_V, A_SQ = 0, 512, 1024, 1536
A_COLS = 2048
B_MO, B_SK, B_SV, B_G = 0, 512, 640, 768
B_COLS = 896

N_K = 0
N_B = 512
N_COLS = N_B + B_COLS
T_Q, T_V, T_SQ, T_SV, T_G = 0, 512, 1024, 1536, 1664
T_GROWS = 16
T_COLS = T_G + BLK
W_COLS = N_COLS + T_COLS
STATE_ROWS = ML_DV + 16

TM_PROMPT = 1024
TM_DENSE = 1024
DENSE_SUB = 512
FF_SPLITS = (0, 1536, D_FF)
SAMPLE_GB = 16
SLAB = 16
VMEM_LIMIT = 56 * 1024 * 1024


def _softcap(a):
    return GATE_SOFTCAP * jnp.tanh(a / GATE_SOFTCAP)


def _log_sigmoid(x):
    return jnp.minimum(x, 0.0) - jnp.log1p(jnp.exp(-jnp.abs(x)))


def _layer_norm(z, g, b):
    mu = jnp.mean(z, axis=-1, keepdims=True)
    zc = z - mu
    var = jnp.mean(zc * zc, axis=-1, keepdims=True)
    return zc * lax.rsqrt(var + LN_EPS) * g + b


def _dot(a, b):
    return jnp.dot(a, b, preferred_element_type=F32)


def _dot_nt(a, b):
    return lax.dot_general(a, b, (((1,), (1,)), ((), ())), preferred_element_type=F32)


def _dot_tn(a, b):
    return lax.dot_general(a, b, (((0,), (0,)), ((), ())), preferred_element_type=F32)


def _split3(x):
    hi = x.astype(BF16)
    r1 = x - hi.astype(F32)
    mid = r1.astype(BF16)
    lo = (r1 - mid.astype(F32)).astype(BF16)
    return hi, mid, lo


def _t5_bucket(d):
    n = jnp.maximum(d, 0)
    max_exact = REL_BUCKETS // 2
    nlog = REL_BUCKETS - max_exact
    large = jnp.full(n.shape, max_exact, jnp.int32)
    for k in range(1, nlog):
        thr = math.ceil(max_exact * (REL_MAX_DIST / max_exact) ** (k / nlog))
        large = large + jnp.where(n >= thr, 1, 0)
    return jnp.where(n < max_exact, n, large)


def _prompt_mixer_kernel(x_ref, w_ref, gbc_ref, mlg_ref, sinks_ref, rb_ref,
                         mix_ref, c_out, n_out, m_out, ko_ref, vo_ref,
                         wt, kn, pf, kband, q_t, v_t, sq_t, sv_t, g_t, ct, msc, tbl):
    b = pl.program_id(0)
    t = pl.program_id(1)
    nt = pl.num_programs(1)
    tm = x_ref.shape[1]
    nblk = tm // BLK
    pairs = SWA_HEADS // 2

    @pl.when((b == 0) & (t == 0))
    def _transpose_feature_major_weights():
        for i in range(T_COLS // BLK):
            wt[i * BLK:(i + 1) * BLK, :] = w_ref[:, N_COLS + i * BLK:N_COLS + (i + 1) * BLK].T

    @pl.when((b == 0) & (t == 0))
    def _build_bias_tables():
        r = lax.broadcasted_iota(jnp.int32, (BLK, 2 * BLK), 0)
        ln = lax.broadcasted_iota(jnp.int32, (BLK, 2 * BLK), 1)
        second = ln >= BLK
        qi = jnp.where(second, ln - BLK, ln)
        prev = r > qi
        d = jnp.where(prev, WINDOW + qi - r, qi - r)
        bucket = _t5_bucket(d)
        for p in range(pairs):
            acc = jnp.zeros((BLK, 2 * BLK), F32)
            for k in range(REL_BUCKETS):
                acc = jnp.where(bucket == k, jnp.where(second, rb_ref[k, p + pairs], rb_ref[k, p]), acc)
            tbl[0, p] = jnp.where(prev, NEG_INF, acc)
            tbl[1, p] = acc

    @pl.when(t == 0)
    def _reset_state():
        ct[...] = jnp.zeros_like(ct)
        msc[...] = jnp.zeros_like(msc)
        kband[0:BLK, :] = jnp.zeros((BLK, kband.shape[1]), kband.dtype)
        sv_t[0] = jnp.zeros(sv_t.shape[1:], sv_t.dtype)

    xb = x_ref[0].astype(BF16)
    kn[...] = _dot(xb, w_ref[:, N_K:N_K + ML_HEADS * ML_DK]).astype(BF16)
    pf[...] = _dot(xb, w_ref[:, N_B:N_B + B_COLS])
    kband[BLK:BLK + tm, :] = pf[:, B_SK:B_SK + BLK].astype(BF16)

    def put(dst, r0, nrows, off, scale=None):
        res = _dot_nt(wt[r0:r0 + nrows, :], xb)
        if scale is not None:
            res = res * scale
        for jj in range(nblk):
            dst[jj + off] = res[:, jj * BLK:(jj + 1) * BLK].astype(dst.dtype)

    put(q_t, T_Q, ML_HEADS * ML_DK, 0)
    put(v_t, T_V, ML_WIDTH, 0)
    put(sq_t, T_SQ, SWA_WIDTH, 0, SWA_SCALE)
    put(sv_t, T_SV, SWA_KV_HEADS * SWA_HD, 1)
    put(g_t, T_G, T_GROWS, 0)

    row = lax.broadcasted_iota(jnp.int32, (BLK, BLK), 0)
    col = lax.broadcasted_iota(jnp.int32, (BLK, BLK), 1)
    causal_t = row <= col
    triu = causal_t.astype(BF16)
    r16 = lax.broadcasted_iota(jnp.int32, (16, BLK), 0)
    ones_rows = (r16 == 0).astype(BF16)
    zero_rows = jnp.zeros((BLK - T_GROWS, BLK), F32)
    gbc = gbc_ref[...]
    mlg = mlg_ref[...]
    lane2 = lax.broadcasted_iota(jnp.int32, (1, 2 * BLK), 1)
    snk_rows = [jnp.where(lane2 >= BLK, sinks_ref[p + pairs], sinks_ref[p]) for p in range(pairs)]
    zq = jnp.zeros((SWA_HD, BLK), BF16)
    prev2 = jnp.concatenate([row > col, row > col], axis=1)

    def block(j, carry):
        r0 = pl.multiple_of(j * BLK, BLK)
        rows = pl.ds(r0, BLK)
        heads = range(ML_HEADS)
        hsl = [slice(h * ML_DK, (h + 1) * ML_DK) for h in heads]

        pre_t = _softcap(g_t[j] + gbc)
        hi, mid, lo = _split3(_log_sigmoid(pre_t))
        b_t = _dot(hi, triu) + _dot(mid, triu) + _dot(lo, triu)
        u_t = pltpu.roll(pre_t, ML_HEADS, 0) - b_t
        u_c = jnp.concatenate([u_t, zero_rows], axis=0).T

        k_h = [kn[rows, hsl[h]] for h in heads]
        qt_h = [q_t[j, hsl[h], :] for h in heads]
        vte = [jnp.concatenate([v_t[j, hsl[h], :], ones_rows], axis=0) for h in heads]
        kq = [_dot(k_h[h], qt_h[h]) for h in heads]

        sel = jnp.where((t == 0) & (j == 0), 0, 1)
        kb = kband[pl.ds(r0, 2 * BLK), :]
        vtb = jnp.concatenate([sv_t[j], sv_t[j + 1]], axis=1)
        sc2 = []
        for p in range(pairs):
            qa = sq_t[j, p * SWA_HD:(p + 1) * SWA_HD, :]
            qb = sq_t[j, (p + pairs) * SWA_HD:(p + pairs + 1) * SWA_HD, :]
            bd = jnp.concatenate([jnp.concatenate([qa, zq], axis=1),
                                  jnp.concatenate([zq, qb], axis=1)], axis=0)
            sc2.append(_dot(kb, bd))

        s_t, scl, m_row, m_new, a_h, vwte = [], [], [], [], [], []
        for h in heads:
            b_row = b_t[ML_HEADS + h:ML_HEADS + h + 1, :]
            i_row = pre_t[h:h + 1, :]
            u_col = u_c[:, ML_HEADS + h:ML_HEADS + h + 1]
            m_prev = msc[h:h + 1, 0:1]
            log_d = jnp.where(causal_t, b_row + u_col, -jnp.inf)
            inter = b_row + m_prev
            mr = jnp.maximum(inter, jnp.max(log_d, axis=0, keepdims=True))
            m_row.append(mr)
            scl.append(jnp.exp(inter - mr))
            s_t.append((kq[h] * jnp.exp(log_d - (mr - LOG_ML_SCALE))).astype(BF16))
            b_last = b_row[:, BLK - 1:BLK]
            w_row = b_last - b_row + i_row
            mn = jnp.maximum(b_last + m_prev, jnp.max(w_row, axis=1, keepdims=True))
            m_new.append(mn)
            a_h.append(jnp.exp(b_last + m_prev - mn))
            wexp = jnp.exp(w_row - mn) * ML_SCALE
            vwte.append(vte[h] * wexp.astype(BF16))

        pw, den_p = [], []
        for p in range(pairs):
            sc = jnp.where(prev2, sc2[p][:BLK], sc2[p][BLK:]) + tbl[sel, p]
            snk = snk_rows[p]
            mx = jnp.maximum(jnp.max(sc, axis=0, keepdims=True), snk)
            pe = jnp.exp(sc - mx)
            den_p.append(jnp.sum(pe, axis=0, keepdims=True) + jnp.exp(snk - mx))
            pw.append(jnp.concatenate([jnp.where(prev2, pe, 0.0), jnp.where(prev2, 0.0, pe)],
                                      axis=0).astype(BF16))

        c_old = [ct[h] for h in heads]
        cq = [_dot(c_old[h].astype(BF16), qt_h[h]) for h in heads]
        vs = [_dot(vte[h], s_t[h]) for h in heads]
        upd = [_dot(vwte[h], k_h[h]) for h in heads]
        o2 = [_dot(vtb, pw[p]) for p in range(pairs)]

        for h in heads:
            nd = scl[h] * cq[h] + vs[h]
            num = nd[:ML_DV]
            den = nd[ML_DV:ML_DV + 1]
            hh = num / jnp.maximum(jnp.abs(den), jnp.exp(-m_row[h]))
            hn = hh * lax.rsqrt(jnp.mean(hh * hh, axis=0, keepdims=True) + NORM_EPS)
            og = jax.nn.sigmoid(pf[rows, B_MO + h * ML_DV:B_MO + (h + 1) * ML_DV])
            mix_ref[0, rows, hsl[h]] = (hn.T * mlg[:, hsl[h]] * og).astype(mix_ref.dtype)
            ct[h] = a_h[h] * c_old[h] + upd[h]
            msc[h:h + 1, :] = jnp.broadcast_to(m_new[h], (1, msc.shape[1]))

        o_parts = [None] * SWA_HEADS
        for p in range(pairs):
            on = o2[p] / den_p[p]
            o_parts[p] = on[0:SWA_HD, 0:BLK]
            o_parts[p + pairs] = on[SWA_HD:2 * SWA_HD, BLK:2 * BLK]
        o_t = jnp.concatenate(o_parts, axis=0)
        mix_ref[0, rows, ML_WIDTH:ML_WIDTH + SWA_WIDTH] = o_t.T.astype(mix_ref.dtype)
        return carry

    lax.fori_loop(0, nblk, block, 0, unroll=4)

    kband[0:BLK, :] = kband[tm:tm + BLK, :]
    sv_t[0] = sv_t[nblk]

    @pl.when(t == nt - 1)
    def _write_state():
        for h in range(ML_HEADS):
            c_out[0, h] = ct[h, 0:ML_DV, :].T
            n_out[0, h:h + 1, :] = ct[h, ML_DV:ML_DV + 1, :]
        m_out[0] = msc[...]
        ko_ref[0] = pf[tm - BLK:tm, B_SK:B_SK + BLK]
        vo_ref[0] = pf[tm - BLK:tm, B_SV:B_SV + BLK]


def _prompt_mixer(x, wall, gbc, mlg, sinks, rb):
    bsz, seq, _ = x.shape
    tm = TM_PROMPT
    nt = seq // tm
    nblk = tm // BLK
    const2 = lambda b, t: (0, 0)
    smem = pl.BlockSpec(memory_space=pltpu.SMEM)
    return pl.pallas_call(
        _prompt_mixer_kernel,
        grid=(bsz, nt),
        in_specs=[
            pl.BlockSpec((1, tm, D_MODEL), lambda b, t: (b, t, 0)),
            pl.BlockSpec((D_MODEL, W_COLS), const2, pipeline_mode=pl.Buffered(1)),
            pl.BlockSpec((T_GROWS, BLK), const2),
            pl.BlockSpec((1, ML_WIDTH), const2),
            smem, smem,
        ],
        out_specs=[
            pl.BlockSpec((1, tm, D_MODEL), lambda b, t: (b, t, 0)),
            pl.BlockSpec((1, ML_HEADS, ML_DK, ML_DV), lambda b, t: (b, 0, 0, 0)),
            pl.BlockSpec((1, ML_HEADS, ML_DK), lambda b, t: (b, 0, 0)),
            pl.BlockSpec((1, 8, BLK), lambda b, t: (b, 0, 0)),
            pl.BlockSpec((1, BLK, BLK), lambda b, t: (b, 0, 0)),
            pl.BlockSpec((1, BLK, BLK), lambda b, t: (b, 0, 0)),
        ],
        out_shape=[
            jax.ShapeDtypeStruct((bsz, seq, D_MODEL), BF16),
            jax.ShapeDtypeStruct((bsz, ML_HEADS, ML_DK, ML_DV), F32),
            jax.ShapeDtypeStruct((bsz, ML_HEADS, ML_DK), F32),
            jax.ShapeDtypeStruct((bsz, 8, BLK), F32),
            jax.ShapeDtypeStruct((bsz, BLK, BLK), F32),
            jax.ShapeDtypeStruct((bsz, BLK, BLK), F32),
        ],
        scratch_shapes=[
            pltpu.VMEM((T_COLS, D_MODEL), BF16),
            pltpu.VMEM((tm, ML_HEADS * ML_DK), BF16),
            pltpu.VMEM((tm, B_COLS), F32),
            pltpu.VMEM((tm + BLK, BLK), BF16),
            pltpu.VMEM((nblk, ML_HEADS * ML_DK, BLK), BF16),
            pltpu.VMEM((nblk, ML_WIDTH, BLK), BF16),
            pltpu.VMEM((nblk, SWA_WIDTH, BLK), BF16),
            pltpu.VMEM((nblk + 1, SWA_KV_HEADS * SWA_HD, BLK), BF16),
            pltpu.VMEM((nblk, T_GROWS, BLK), F32),
            pltpu.VMEM((ML_HEADS, STATE_ROWS, ML_DK), F32),
            pltpu.VMEM((8, BLK), F32),
            pltpu.VMEM((2, SWA_HEADS // 2, BLK, 2 * BLK), F32),
        ],
        compiler_params=pltpu.CompilerParams(
            dimension_semantics=("arbitrary", "arbitrary"),
            vmem_limit_bytes=VMEM_LIMIT),
        name="prompt_mixer",
    )(x, wall, gbc, mlg, sinks, rb)


def _roll_rows(x, shift):
    return pltpu.roll(x, shift % x.shape[0], 0)


def _sample_mixer_kernel(x_ref, w_ref, gb_ref, mlg_ref, sinks_ref, rb_ref, mrep_ref,
                         c_ref, n_ref, kc_ref, vc_ref,
                         mix_ref, c_out, n_out, mo_ref, ko_ref, vo_ref,
                         pa, pf, tblc, tbln):
    step = pl.program_id(0)
    rows_total = x_ref.shape[0]
    tdec = 4
    nslab = rows_total // SLAB
    qrows = SWA_GROUP * SLAB

    @pl.when(step == 0)
    def _build_bias_tables():
        r = lax.broadcasted_iota(jnp.int32, (qrows, BLK), 0)
        c = lax.broadcasted_iota(jnp.int32, (qrows, BLK), 1)
        hl = r // SLAB
        bl = (r % SLAB) // tdec
        tq = r % tdec
        d_cache = WINDOW + tq - c
        d_new = tq - (c % tdec)
        bk_cache = _t5_bucket(d_cache)
        bk_new = _t5_bucket(d_new)
        ok_cache = c > tq
        ok_new = (c < SLAB) & ((c // tdec) == bl) & (d_new >= 0)
        for g in range(SWA_KV_HEADS):
            acc_c = jnp.zeros((qrows, BLK), F32)
            acc_n = jnp.zeros((qrows, BLK), F32)
            for k in range(REL_BUCKETS):
                for hh in range(SWA_GROUP):
                    val = rb_ref[k, g * SWA_GROUP + hh]
                    acc_c = jnp.where((bk_cache == k) & (hl == hh), val, acc_c)
                    acc_n = jnp.where((bk_new == k) & (hl == hh), val, acc_n)
            tblc[g] = jnp.where(ok_cache, acc_c, NEG_INF)
            tbln[g] = jnp.where(ok_new, acc_n, NEG_INF)

    xb = x_ref[...].astype(BF16)
    tq, tv = N_COLS + T_Q, N_COLS + T_V
    pa[:, A_Q:A_K] = _dot(xb, w_ref[:, tq:tq + ML_HEADS * ML_DK])
    pa[:, A_K:A_V] = _dot(xb, w_ref[:, N_K:N_K + ML_HEADS * ML_DK])
    pa[:, A_V:A_COLS] = _dot(xb, w_ref[:, tv:tv + ML_WIDTH + SWA_WIDTH])
    pf[...] = _dot(xb, w_ref[:, N_B:N_B + B_COLS])

    gbias = gb_ref[...]
    mlg = mlg_ref[...]
    r16 = lax.broadcasted_iota(jnp.int32, (SLAB, BLK), 0)
    rr = r16 % tdec
    bl16 = r16 // tdec
    r64 = lax.broadcasted_iota(jnp.int32, (qrows, BLK), 0)
    bl64 = (r64 % SLAB) // tdec
    zeros_pad = jnp.zeros((BLK - SLAB, SWA_HD), BF16)

    def seg_last(x):
        return jnp.where(rr == 3, x,
                         jnp.where(rr == 2, _roll_rows(x, -1),
                                   jnp.where(rr == 1, _roll_rows(x, -2), _roll_rows(x, -3))))

    def seg_max(x):
        m1 = jnp.maximum(x, jnp.where(rr % 2 == 0, _roll_rows(x, -1), _roll_rows(x, 1)))
        return jnp.maximum(m1, jnp.where(rr < 2, _roll_rows(m1, -2), _roll_rows(m1, 2)))

    def slab(si, carry):
        r0 = pl.multiple_of(si * SLAB, SLAB)
        rows = pl.ds(r0, SLAB)
        seq0 = si * (SLAB // tdec)

        pre = _softcap(pf[rows, B_G:B_G + BLK] + gbias)
        logf = _log_sigmoid(pre)
        y = logf + jnp.where(rr >= 1, _roll_rows(logf, 1), 0.0)
        bcum = y + jnp.where(rr >= 2, _roll_rows(y, 2), 0.0)
        ig = pltpu.roll(pre, ML_HEADS, 1)
        m_prev = mrep_ref[rows, :]
        b_minus_i = bcum - ig
        log_d = [jnp.where(rr >= dl, bcum - _roll_rows(b_minus_i, dl), -jnp.inf) for dl in range(tdec)]
        rowmax = functools.reduce(jnp.maximum, log_d)
        inter = bcum + m_prev
        m_row = jnp.maximum(inter, rowmax)
        scl = jnp.exp(inter - m_row)
        dm = [jnp.exp(ld - m_row) for ld in log_d]
        enm = jnp.exp(-m_row)
        b_last = seg_last(bcum)
        w = b_last - bcum + ig
        m_new = jnp.maximum(b_last + m_prev, seg_max(w))
        a = jnp.exp(b_last + m_prev - m_new)
        kws = jnp.exp(w - m_new) * ML_SCALE
        mo_ref[rows, :] = m_new

        for h in range(ML_HEADS):
            hs = slice(h * ML_DK, (h + 1) * ML_DK)
            lane = ML_HEADS + h
            q = pa[rows, A_Q + h * ML_DK:A_Q + (h + 1) * ML_DK]
            k = pa[rows, A_K + h * ML_DK:A_K + (h + 1) * ML_DK]
            v = pa[rows, A_V + h * ML_DV:A_V + (h + 1) * ML_DV]
            bm = jnp.zeros((SLAB, ML_DV), F32)
            sum_s = jnp.zeros((SLAB, 1), F32)
            for dl in range(tdec):
                s_dl = jnp.sum(q * _roll_rows(k, dl), axis=-1, keepdims=True)
                ws = s_dl * ML_SCALE * dm[dl][:, lane:lane + 1]
                bm = bm + ws * _roll_rows(v, dl)
                sum_s = sum_s + ws
            qc = jnp.zeros((SLAB, ML_DV), F32)
            nsel = jnp.zeros((SLAB, ML_DK), F32)
            for bl in range(SLAB // tdec):
                msk = bl16 == bl
                c_b = c_ref[seq0 + bl, h]
                n_b = n_ref[seq0 + bl][h:h + 1, :]
                qc = qc + _dot(jnp.where(msk, q, 0.0).astype(BF16), c_b.astype(BF16))
                nsel = jnp.where(msk, n_b, nsel)
                a_b = a[bl * tdec:bl * tdec + 1, lane:lane + 1]
                kw = jnp.where(msk, k * kws[:, lane:lane + 1], 0.0)
                c_out[seq0 + bl, h] = a_b * c_b + _dot_tn(kw.astype(BF16), v.astype(BF16))
                n_out[seq0 + bl, h:h + 1, :] = a_b * n_b + jnp.sum(kw, axis=0, keepdims=True)
            sclh = scl[:, lane:lane + 1]
            num = sclh * qc + bm
            den = sclh * jnp.sum(q * nsel, axis=-1, keepdims=True) + sum_s
            hh = num / jnp.maximum(jnp.abs(den), enm[:, lane:lane + 1])
            hn = hh * lax.rsqrt(jnp.mean(hh * hh, axis=-1, keepdims=True) + NORM_EPS)
            og = jax.nn.sigmoid(pf[rows, B_MO + h * ML_DV:B_MO + (h + 1) * ML_DV])
            mix_ref[rows, hs] = (hn * mlg[:, hs] * og).astype(mix_ref.dtype)

        knew = pf[rows, B_SK:B_SK + BLK]
        vnew = pf[rows, B_SV:B_SV + BLK]
        for g in range(SWA_KV_HEADS):
            gs = slice(g * SWA_HD, (g + 1) * SWA_HD)
            qg = jnp.concatenate(
                [pa[rows, A_SQ + (g * SWA_GROUP + hl) * SWA_HD:A_SQ + (g * SWA_GROUP + hl + 1) * SWA_HD]
                 for hl in range(SWA_GROUP)], axis=0).astype(BF16)
            kn = jnp.concatenate([knew[:, gs].astype(BF16), zeros_pad], axis=0)
            vn = jnp.concatenate([vnew[:, gs].astype(BF16), zeros_pad], axis=0)
            s_c = jnp.zeros((qrows, BLK), F32)
            for bl in range(SLAB // tdec):
                kc = kc_ref[seq0 + bl][:, gs].astype(BF16)
                s_c = jnp.where(bl64 == bl, _dot_nt(qg, kc), s_c)
            s_c = s_c * SWA_SCALE + tblc[g]
            s_n = _dot_nt(qg, kn) * SWA_SCALE + tbln[g]
            hl64 = r64[:, 0:1] // SLAB
            snk = jnp.zeros((qrows, 1), F32)
            for hl in range(SWA_GROUP):
                snk = jnp.where(hl64 == hl, sinks_ref[g * SWA_GROUP + hl], snk)
            mx = jnp.maximum(jnp.maximum(jnp.max(s_c, axis=-1, keepdims=True),
                                         jnp.max(s_n, axis=-1, keepdims=True)), snk)
            p_c = jnp.exp(s_c - mx)
            p_n = jnp.exp(s_n - mx)
            den = (jnp.sum(p_c, axis=-1, keepdims=True) + jnp.sum(p_n, axis=-1, keepdims=True)
                   + jnp.exp(snk - mx))
            o = _dot(p_n.astype(BF16), vn)
            for bl in range(SLAB // tdec):
                vc = vc_ref[seq0 + bl][:, gs].astype(BF16)
                o = o + _dot(jnp.where(bl64 == bl, p_c, 0.0).astype(BF16), vc)
            o = o / den
            og = jnp.concatenate([o[hl * SLAB:(hl + 1) * SLAB, :] for hl in range(SWA_GROUP)], axis=1)
            c0 = ML_WIDTH + g * SWA_GROUP * SWA_HD
            mix_ref[rows, c0:c0 + SWA_GROUP * SWA_HD] = og.astype(mix_ref.dtype)

        for bl in range(SLAB // tdec):
            ko_ref[seq0 + bl, 0:WINDOW - tdec, :] = kc_ref[seq0 + bl, tdec:WINDOW, :]
            vo_ref[seq0 + bl, 0:WINDOW - tdec, :] = vc_ref[seq0 + bl, tdec:WINDOW, :]
            ko_ref[seq0 + bl, WINDOW - tdec:WINDOW, :] = knew[bl * tdec:(bl + 1) * tdec, :]
            vo_ref[seq0 + bl, WINDOW - tdec:WINDOW, :] = vnew[bl * tdec:(bl + 1) * tdec, :]
        return carry

    lax.fori_loop(0, nslab, slab, 0)


def _sample_mixer(x2d, wall, gb, mlg, sinks, rb, mrep, c0, n0, kc, vc):
    rows = x2d.shape[0]
    tdec = 4
    nseq = rows // tdec
    gbs = SAMPLE_GB
    steps = nseq // gbs
    rb_rows = gbs * tdec
    const2 = lambda i: (0, 0)
    smem = pl.BlockSpec(memory_space=pltpu.SMEM)
    return pl.pallas_call(
        _sample_mixer_kernel,
        grid=(steps,),
        in_specs=[
            pl.BlockSpec((rb_rows, D_MODEL), lambda i: (i, 0)),
            pl.BlockSpec((D_MODEL, W_COLS), const2, pipeline_mode=pl.Buffered(1)),
            pl.BlockSpec((1, BLK), const2),
            pl.BlockSpec((1, ML_WIDTH), const2),
            smem, smem,
            pl.BlockSpec((rb_rows, BLK), lambda i: (i, 0)),
            pl.BlockSpec((gbs, ML_HEADS, ML_DK, ML_DV), lambda i: (i, 0, 0, 0)),
            pl.BlockSpec((gbs, ML_HEADS, ML_DK), lambda i: (i, 0, 0)),
            pl.BlockSpec((gbs, WINDOW, BLK), lambda i: (i, 0, 0)),
            pl.BlockSpec((gbs, WINDOW, BLK), lambda i: (i, 0, 0)),
        ],
        out_specs=[
            pl.BlockSpec((rb_rows, D_MODEL), lambda i: (i, 0)),
            pl.BlockSpec((gbs, ML_HEADS, ML_DK, ML_DV), lambda i: (i, 0, 0, 0)),
            pl.BlockSpec((gbs, ML_HEADS, ML_DK), lambda i: (i, 0, 0)),
            pl.BlockSpec((rb_rows, BLK), lambda i: (i, 0)),
            pl.BlockSpec((gbs, WINDOW, BLK), lambda i: (i, 0, 0)),
            pl.BlockSpec((gbs, WINDOW, BLK), lambda i: (i, 0, 0)),
        ],
        out_shape=[
            jax.ShapeDtypeStruct((rows, D_MODEL), BF16),
            jax.ShapeDtypeStruct((nseq, ML_HEADS, ML_DK, ML_DV), F32),
            jax.ShapeDtypeStruct((nseq, ML_HEADS, ML_DK), F32),
            jax.ShapeDtypeStruct((rows, BLK), F32),
            jax.ShapeDtypeStruct((nseq, WINDOW, BLK), F32),
            jax.ShapeDtypeStruct((nseq, WINDOW, BLK), F32),
        ],
        scratch_shapes=[
            pltpu.VMEM((rb_rows, A_COLS), F32),
            pltpu.VMEM((rb_rows, B_COLS), F32),
            pltpu.VMEM((SWA_KV_HEADS, SWA_GROUP * SLAB, BLK), F32),
            pltpu.VMEM((SWA_KV_HEADS, SWA_GROUP * SLAB, BLK), F32),
        ],
        compiler_params=pltpu.CompilerParams(
            dimension_semantics=("arbitrary",),
            vmem_limit_bytes=VMEM_LIMIT),
        name="sample_mixer",
    )(x2d, wall, gb, mlg, sinks, rb, mrep, c0, n0, kc, vc)


def _dense_kernel(x_ref, mix_ref, wo_ref, g1_ref, b1_ref, wfi_ref, wfo_ref, g2_ref, b2_ref, y_ref):
    nsub = x_ref.shape[0] // DENSE_SUB
    subs = [slice(i * DENSE_SUB, (i + 1) * DENSE_SUB) for i in range(nsub)]
    chunks = list(zip(FF_SPLITS[:-1], FF_SPLITS[1:]))
    g1, b1, g2, b2 = g1_ref[...], b1_ref[...], g2_ref[...], b2_ref[...]

    proj = [_dot(mix_ref[s, :], wo_ref[...]) for s in subs]
    h1 = [_layer_norm(ALPHA * x_ref[s, :] + proj[i], g1, b1) for i, s in enumerate(subs)]
    h1b = [h.astype(BF16) for h in h1]
    acc = [None] * nsub
    for c0, c1 in chunks:
        gate = [_dot(h1b[i], wfi_ref[:, c0:c1]) for i in range(nsub)]
        up = [_dot(h1b[i], wfi_ref[:, D_FF + c0:D_FF + c1]) for i in range(nsub)]
        for i in range(nsub):
            act = (gate[i] * jax.nn.sigmoid(gate[i]) * up[i]).astype(BF16)
            part = _dot(act, wfo_ref[c0:c1, :])
            acc[i] = part if acc[i] is None else acc[i] + part
    for i, s in enumerate(subs):
        y_ref[s, :] = _layer_norm(ALPHA * h1[i] + acc[i], g2, b2)


def _dense(x2d, mix2d, wo, g1, b1, wfi, wfo, g2, b2):
    rows = x2d.shape[0]
    tm = min(TM_DENSE, rows)
    const2 = lambda i: (0, 0)

    def wspec(shape):
        return pl.BlockSpec(shape, const2, pipeline_mode=pl.Buffered(1))

    return pl.pallas_call(
        _dense_kernel,
        grid=(rows // tm,),
        in_specs=[
            pl.BlockSpec((tm, D_MODEL), lambda i: (i, 0)),
            pl.BlockSpec((tm, D_MODEL), lambda i: (i, 0)),
            wspec((D_MODEL, D_MODEL)),
            wspec((1, D_MODEL)), wspec((1, D_MODEL)),
            wspec((D_MODEL, 2 * D_FF)),
            wspec((D_FF, D_MODEL)),
            wspec((1, D_MODEL)), wspec((1, D_MODEL)),
        ],
        out_specs=pl.BlockSpec((tm, D_MODEL), lambda i: (i, 0)),
        out_shape=jax.ShapeDtypeStruct((rows, D_MODEL), F32),
        compiler_params=pltpu.CompilerParams(
            dimension_semantics=("arbitrary",),
            vmem_limit_bytes=VMEM_LIMIT),
        name="dense",
    )(x2d, mix2d, wo, g1, b1, wfi, wfo, g2, b2)


def kernel(x_prompt, x_sample, state_mlstm_C, state_mlstm_n, state_mlstm_m, cache_swa_k, cache_swa_v,
           w_in, b_igate, b_fgate, ml_norm_g, swa_sinks, rel_bias, w_out, ln1_g, ln1_b,
           w_ffn_in, w_ffn_out, ln2_g, ln2_b):
    bp, seq, _ = x_prompt.shape
    bs, tdec, _ = x_sample.shape
    l = 0

    offs = [0]
    for n in IN_SIZES:
        offs.append(offs[-1] + n)
    w = w_in[l]
    mq, mk, mv, mo, mi, mf, sq, sk, sv = [w[:, offs[i]:offs[i + 1]] for i in range(len(IN_SIZES))]
    gpad = jnp.zeros((D_MODEL, BLK - 2 * ML_HEADS), w.dtype)
    wall = jnp.concatenate([mk, mo, sk, sv, mi, mf, gpad,
                            mq, mv, sq, sv, mi, mf, gpad],
                           axis=1).astype(BF16)
    gvec = jnp.concatenate([b_igate[l], b_fgate[l]]).astype(F32)
    gb = jnp.pad(gvec, (0, BLK - 2 * ML_HEADS))[None, :]
    gbc = jnp.broadcast_to(jnp.pad(gvec, (0, T_GROWS - 2 * ML_HEADS))[:, None], (T_GROWS, BLK))
    mlg = ml_norm_g[l][None, :].astype(F32)
    sinks = swa_sinks[l].astype(F32)
    rb = rel_bias.astype(F32)
    wo = w_out[l].astype(BF16)
    wfi = w_ffn_in[l].astype(BF16)
    wfo = w_ffn_out[l].astype(BF16)
    g1, b1 = ln1_g[l][None, :], ln1_b[l][None, :]
    g2, b2 = ln2_g[l][None, :], ln2_b[l][None, :]

    mix_p, c_p, n_p, m_p, k_p, v_p = _prompt_mixer(x_prompt, wall, gbc, mlg, sinks, rb)
    y_p = _dense(x_prompt.reshape(bp * seq, D_MODEL), mix_p.reshape(bp * seq, D_MODEL),
                 wo, g1, b1, wfi, wfo, g2, b2).reshape(bp, seq, D_MODEL)
    p_m = m_p[:, :ML_HEADS, 0][None]
    p_k = k_p.reshape(1, bp, WINDOW, SWA_KV_HEADS, SWA_HD)
    p_v = v_p.reshape(1, bp, WINDOW, SWA_KV_HEADS, SWA_HD)

    xs = x_sample.reshape(bs * tdec, D_MODEL)
    m0 = state_mlstm_m[l].astype(F32)
    mrep = jnp.pad(jnp.repeat(m0, tdec, axis=0), ((0, 0), (ML_HEADS, BLK - 2 * ML_HEADS)))
    wlen = cache_swa_k.shape[2]
    kc = cache_swa_k[l].reshape(bs, wlen, SWA_KV_HEADS * SWA_HD)
    vc = cache_swa_v[l].reshape(bs, wlen, SWA_KV_HEADS * SWA_HD)
    mix_s, c_s, n_s, mo_s, k_s, v_s = _sample_mixer(
        xs, wall, gb, mlg, sinks, rb, mrep,
        state_mlstm_C[l].astype(F32), state_mlstm_n[l].astype(F32), kc, vc)
    y_s = _dense(xs, mix_s, wo, g1, b1, wfi, wfo, g2, b2).reshape(bs, tdec, D_MODEL)
    s_m = mo_s.reshape(bs, tdec, BLK)[:, 0, ML_HEADS:2 * ML_HEADS][None]
    s_k = k_s.reshape(1, bs, wlen, SWA_KV_HEADS, SWA_HD)
    s_v = v_s.reshape(1, bs, wlen, SWA_KV_HEADS, SWA_HD)

    return (y_p, y_s, c_p[None], n_p[None], p_m, p_k, p_v,
            c_s[None], n_s[None], s_m, s_k, s_v)
```

```python
import functools
import math

import jax
import jax.numpy as jnp
from jax import lax
from jax.experimental import pallas as pl
from jax.experimental.pallas import tpu as pltpu

F32 = jnp.float32
BF16 = jnp.bfloat16

D_MODEL = 1024
ML_HEADS = 4
ML_DK = 128
ML_DV = 128
ML_WIDTH = ML_HEADS * ML_DV
GATE_SOFTCAP = 15.0
SWA_HEADS = 8
SWA_KV_HEADS = 2
SWA_GROUP = SWA_HEADS // SWA_KV_HEADS
SWA_HD = 64
SWA_WIDTH = SWA_HEADS * SWA_HD
WINDOW = 128
REL_BUCKETS = 32
REL_MAX_DIST = 128
D_FF = 2816
DEPTH = 1
ALPHA = (2.0 * DEPTH) ** 0.25
LN_EPS = 1e-5
NORM_EPS = 1e-6
NEG_INF = -1e30
IN_SIZES = (512, 512, 512, 512, 4, 4, 512, 128, 128)
ML_SCALE = ML_DK ** -0.5
LOG_ML_SCALE = math.log(ML_SCALE)
SWA_SCALE = SWA_HD ** -0.5

BLK = 128

A_Q, A_K, A_V, A_SQ = 0, 512, 1024, 1536
A_COLS = 2048
B_MO, B_SK, B_SV, B_G = 0, 512, 640, 768
B_COLS = 896

N_K = 0
N_B = 512
N_COLS = N_B + B_COLS
T_Q, T_V, T_SQ, T_SV, T_G = 0, 512, 1024, 1536, 1664
T_GROWS = 16
T_COLS = T_G + BLK
W_COLS = N_COLS + T_COLS
STATE_ROWS = ML_DV + 16

TM_PROMPT = 1024
TM_DENSE = 1024
DENSE_SUB = 512
FF_SPLITS = (0, 1536, D_FF)
SAMPLE_GB = 16
SLAB = 16
VMEM_LIMIT = 56 * 1024 * 1024


def _softcap(a):
    return GATE_SOFTCAP * jnp.tanh(a / GATE_SOFTCAP)


def _log_sigmoid(x):
    return jnp.minimum(x, 0.0) - jnp.log1p(jnp.exp(-jnp.abs(x)))


def _layer_norm(z, g, b):
    mu = jnp.mean(z, axis=-1, keepdims=True)
    zc = z - mu
    var = jnp.mean(zc * zc, axis=-1, keepdims=True)
    return zc * lax.rsqrt(var + LN_EPS) * g + b


def _dot(a, b):
    return jnp.dot(a, b, preferred_element_type=F32)


def _dot_nt(a, b):
    return lax.dot_general(a, b, (((1,), (1,)), ((), ())), preferred_element_type=F32)


def _dot_tn(a, b):
    return lax.dot_general(a, b, (((0,), (0,)), ((), ())), preferred_element_type=F32)


def _split3(x):
    hi = x.astype(BF16)
    r1 = x - hi.astype(F32)
    mid = r1.astype(BF16)
    lo = (r1 - mid.astype(F32)).astype(BF16)
    return hi, mid, lo


def _t5_bucket(d):
    n = jnp.maximum(d, 0)
    max_exact = REL_BUCKETS // 2
    nlog = REL_BUCKETS - max_exact
    large = jnp.full(n.shape, max_exact, jnp.int32)
    for k in range(1, nlog):
        thr = math.ceil(max_exact * (REL_MAX_DIST / max_exact) ** (k / nlog))
        large = large + jnp.where(n >= thr, 1, 0)
    return jnp.where(n < max_exact, n, large)


def _prompt_mixer_kernel(x_ref, w_ref, gbc_ref, mlg_ref, sinks_ref, rb_ref,
                         mix_ref, c_out, n_out, m_out, ko_ref, vo_ref,
                         wt, kn, pf, kband, q_t, v_t, sq_t, sv_t, g_t, ct, msc, tbl):
    b = pl.program_id(0)
    t = pl.program_id(1)
    nt = pl.num_programs(1)
    tm = x_ref.shape[1]
    nblk = tm // BLK
    pairs = SWA_HEADS // 2

    @pl.when((b == 0) & (t == 0))
    def _transpose_feature_major_weights():
        for i in range(T_COLS // BLK):
            wt[i * BLK:(i + 1) * BLK, :] = w_ref[:, N_COLS + i * BLK:N_COLS + (i + 1) * BLK].T

    @pl.when((b == 0) & (t == 0))
    def _build_bias_tables():
        r = lax.broadcasted_iota(jnp.int32, (BLK, 2 * BLK), 0)
        ln = lax.broadcasted_iota(jnp.int32, (BLK, 2 * BLK), 1)
        second = ln >= BLK
        qi = jnp.where(second, ln - BLK, ln)
        prev = r > qi
        d = jnp.where(prev, WINDOW + qi - r, qi - r)
        bucket = _t5_bucket(d)
        for p in range(pairs):
            acc = jnp.zeros((BLK, 2 * BLK), F32)
            for k in range(REL_BUCKETS):
                acc = jnp.where(bucket == k, jnp.where(second, rb_ref[k, p + pairs], rb_ref[k, p]), acc)
            tbl[0, p] = jnp.where(prev, NEG_INF, acc)
            tbl[1, p] = acc

    @pl.when(t == 0)
    def _reset_state():
        ct[...] = jnp.zeros_like(ct)
        msc[...] = jnp.zeros_like(msc)
        kband[0:BLK, :] = jnp.zeros((BLK, kband.shape[1]), kband.dtype)
        sv_t[0] = jnp.zeros(sv_t.shape[1:], sv_t.dtype)

    xb = x_ref[0].astype(BF16)
    kn[...] = _dot(xb, w_ref[:, N_K:N_K + ML_HEADS * ML_DK]).astype(BF16)
    pf[...] = _dot(xb, w_ref[:, N_B:N_B + B_COLS])
    kband[BLK:BLK + tm, :] = pf[:, B_SK:B_SK + BLK].astype(BF16)

    def put(dst, r0, nrows, off, scale=None):
        res = _dot_nt(wt[r0:r0 + nrows, :], xb)
        if scale is not None:
            res = res * scale
        for jj in range(nblk):
            dst[jj + off] = res[:, jj * BLK:(jj + 1) * BLK].astype(dst.dtype)

    put(q_t, T_Q, ML_HEADS * ML_DK, 0)
    put(v_t, T_V, ML_WIDTH, 0)
    put(sq_t, T_SQ, SWA_WIDTH, 0, SWA_SCALE)
    put(sv_t, T_SV, SWA_KV_HEADS * SWA_HD, 1)
    put(g_t, T_G, T_GROWS, 0)

    row = lax.broadcasted_iota(jnp.int32, (BLK, BLK), 0)
    col = lax.broadcasted_iota(jnp.int32, (BLK, BLK), 1)
    causal_t = row <= col
    triu = causal_t.astype(BF16)
    causal2 = jnp.concatenate([causal_t, causal_t], axis=1)
    r16 = lax.broadcasted_iota(jnp.int32, (16, 2 * BLK), 0)
    ones_rows = (r16 == 0).astype(BF16)
    z128 = jnp.zeros((BLK, BLK), BF16)
    ml_pairs = ML_HEADS // 2
    zero_rows = jnp.zeros((BLK - T_GROWS, BLK), F32)
    gbc = gbc_ref[...]
    mlg = mlg_ref[...]
    lane2 = lax.broadcasted_iota(jnp.int32, (1, 2 * BLK), 1)
    snk_rows = [jnp.where(lane2 >= BLK, sinks_ref[p + pairs], sinks_ref[p]) for p in range(pairs)]
    zq = jnp.zeros((SWA_HD, BLK), BF16)
    prev2 = jnp.concatenate([row > col, row > col], axis=1)

    def pair_row(x0, x1):
        return jnp.concatenate([jnp.broadcast_to(x0, (1, BLK)), jnp.broadcast_to(x1, (1, BLK))], axis=1)

    def block_diag(x0, x1):
        return jnp.concatenate([jnp.concatenate([x0, z128], axis=1),
                                jnp.concatenate([z128, x1], axis=1)], axis=0)

    def block(j, carry):
        r0 = pl.multiple_of(j * BLK, BLK)
        rows = pl.ds(r0, BLK)
        hsl = [slice(h * ML_DK, (h + 1) * ML_DK) for h in range(ML_HEADS)]
        psl = [slice(p * 2 * ML_DK, (p + 1) * 2 * ML_DK) for p in range(ml_pairs)]

        pre_t = _softcap(g_t[j] + gbc)
        hi, mid, lo = _split3(_log_sigmoid(pre_t))
        b_t = _dot(hi, triu) + _dot(mid, triu) + _dot(lo, triu)
        u_t = pltpu.roll(pre_t, ML_HEADS, 0) - b_t
        u_c = jnp.concatenate([u_t, zero_rows], axis=0).T

        k2 = [kn[rows, psl[p]] for p in range(ml_pairs)]
        bdq = [block_diag(q_t[j, hsl[2 * p], :], q_t[j, hsl[2 * p + 1], :]) for p in range(ml_pairs)]
        vte = [jnp.concatenate([jnp.concatenate([v_t[j, hsl[2 * p], :], v_t[j, hsl[2 * p + 1], :]], axis=1),
                                ones_rows], axis=0) for p in range(ml_pairs)]
        kq = [_dot(k2[p], bdq[p]) for p in range(ml_pairs)]

        sel = jnp.where((t == 0) & (j == 0), 0, 1)
        kb = kband[pl.ds(r0, 2 * BLK), :]
        vtb = jnp.concatenate([sv_t[j], sv_t[j + 1]], axis=1)
        sc2 = []
        for p in range(pairs):
            qa = sq_t[j, p * SWA_HD:(p + 1) * SWA_HD, :]
            qb = sq_t[j, (p + pairs) * SWA_HD:(p + pairs + 1) * SWA_HD, :]
            bd = jnp.concatenate([jnp.concatenate([qa, zq], axis=1),
                                  jnp.concatenate([zq, qb], axis=1)], axis=0)
            sc2.append(_dot(kb, bd))

        bds, scl, m_row, m_new, a_p, vwte = [], [], [], [], [], []
        for p in range(ml_pairs):
            h0, h1 = 2 * p, 2 * p + 1
            b_row = pair_row(b_t[ML_HEADS + h0:ML_HEADS + h0 + 1, :], b_t[ML_HEADS + h1:ML_HEADS + h1 + 1, :])
            i_row = pair_row(pre_t[h0:h0 + 1, :], pre_t[h1:h1 + 1, :])
            u_col = jnp.concatenate(
                [jnp.broadcast_to(u_c[:, ML_HEADS + h0:ML_HEADS + h0 + 1], (BLK, BLK)),
                 jnp.broadcast_to(u_c[:, ML_HEADS + h1:ML_HEADS + h1 + 1], (BLK, BLK))], axis=1)
            m_prev = msc[p:p + 1, :]
            log_d = jnp.where(causal2, b_row + u_col, -jnp.inf)
            inter = b_row + m_prev
            mr = jnp.maximum(inter, jnp.max(log_d, axis=0, keepdims=True))
            m_row.append(mr)
            scl.append(jnp.exp(inter - mr))
            s2 = (kq[p] * jnp.exp(log_d - (mr - LOG_ML_SCALE))).astype(BF16)
            bds.append(block_diag(s2[:, :BLK], s2[:, BLK:]))
            b_last = pair_row(b_row[:, BLK - 1:BLK], b_row[:, 2 * BLK - 1:2 * BLK])
            w_row = b_last - b_row + i_row
            w_max = pair_row(jnp.max(w_row[:, :BLK], axis=1, keepdims=True),
                             jnp.max(w_row[:, BLK:], axis=1, keepdims=True))
            mn = jnp.maximum(b_last + m_prev, w_max)
            m_new.append(mn)
            a_p.append(jnp.exp(b_last + m_prev - mn))
            wexp = jnp.exp(w_row - mn) * ML_SCALE
            vwte.append(vte[p] * wexp.astype(BF16))

        pw, den_p = [], []
        for p in range(pairs):
            sc = jnp.where(prev2, sc2[p][:BLK], sc2[p][BLK:]) + tbl[sel, p]
            snk = snk_rows[p]
            mx = jnp.maximum(jnp.max(sc, axis=0, keepdims=True), snk)
            pe = jnp.exp(sc - mx)
            den_p.append(jnp.sum(pe, axis=0, keepdims=True) + jnp.exp(snk - mx))
            pw.append(jnp.concatenate([jnp.where(prev2, pe, 0.0), jnp.where(prev2, 0.0, pe)],
                                      axis=0).astype(BF16))

        c_old = [ct[p] for p in range(ml_pairs)]
        cq = [_dot(c_old[p].astype(BF16), bdq[p]) for p in range(ml_pairs)]
        vs = [_dot(vte[p], bds[p]) for p in range(ml_pairs)]
        upd = [_dot(vwte[p], block_diag(k2[p][:, :BLK], k2[p][:, BLK:])) for p in range(ml_pairs)]
        o2 = [_dot(vtb, pw[p]) for p in range(pairs)]

        for p in range(ml_pairs):
            nd = scl[p] * cq[p] + vs[p]
            num = nd[:ML_DV]
            den = nd[ML_DV:ML_DV + 1]
            hh = num / jnp.maximum(jnp.abs(den), jnp.exp(-m_row[p]))
            hn = hh * lax.rsqrt(jnp.mean(hh * hh, axis=0, keepdims=True) + NORM_EPS)
            for half in range(2):
                h = 2 * p + half
                og = jax.nn.sigmoid(pf[rows, B_MO + h * ML_DV:B_MO + (h + 1) * ML_DV])
                hn_h = hn[:, half * BLK:(half + 1) * BLK].T
                mix_ref[0, rows, hsl[h]] = (hn_h * mlg[:, hsl[h]] * og).astype(mix_ref.dtype)
            ct[p] = a_p[p] * c_old[p] + upd[p]
            msc[p:p + 1, :] = m_new[p]

        o_parts = [None] * SWA_HEADS
        for p in range(pairs):
            on = o2[p] / den_p[p]
            o_parts[p] = on[0:SWA_HD, 0:BLK]
            o_parts[p + pairs] = on[SWA_HD:2 * SWA_HD, BLK:2 * BLK]
        o_t = jnp.concatenate(o_parts, axis=0)
        mix_ref[0, rows, ML_WIDTH:ML_WIDTH + SWA_WIDTH] = o_t.T.astype(mix_ref.dtype)
        return carry

    lax.fori_loop(0, nblk, block, 0, unroll=4)

    kband[0:BLK, :] = kband[tm:tm + BLK, :]
    sv_t[0] = sv_t[nblk]

    @pl.when(t == nt - 1)
    def _write_state():
        for h in range(ML_HEADS):
            p, lanes = h // 2, slice((h % 2) * ML_DK, (h % 2 + 1) * ML_DK)
            c_out[0, h] = ct[p, 0:ML_DV, lanes].T
            n_out[0, h:h + 1, :] = ct[p, ML_DV:ML_DV + 1, lanes]
        m_out[0] = msc[...]
        ko_ref[0] = pf[tm - BLK:tm, B_SK:B_SK + BLK]
        vo_ref[0] = pf[tm - BLK:tm, B_SV:B_SV + BLK]


def _prompt_mixer(x, wall, gbc, mlg, sinks, rb):
    bsz, seq, _ = x.shape
    tm = TM_PROMPT
    nt = seq // tm
    nblk = tm // BLK
    const2 = lambda b, t: (0, 0)
    smem = pl.BlockSpec(memory_space=pltpu.SMEM)
    return pl.pallas_call(
        _prompt_mixer_kernel,
        grid=(bsz, nt),
        in_specs=[
            pl.BlockSpec((1, tm, D_MODEL), lambda b, t: (b, t, 0)),
            pl.BlockSpec((D_MODEL, W_COLS), const2, pipeline_mode=pl.Buffered(1)),
            pl.BlockSpec((T_GROWS, BLK), const2),
            pl.BlockSpec((1, ML_WIDTH), const2),
            smem, smem,
        ],
        out_specs=[
            pl.BlockSpec((1, tm, D_MODEL), lambda b, t: (b, t, 0)),
            pl.BlockSpec((1, ML_HEADS, ML_DK, ML_DV), lambda b, t: (b, 0, 0, 0)),
            pl.BlockSpec((1, ML_HEADS, ML_DK), lambda b, t: (b, 0, 0)),
            pl.BlockSpec((1, 8, 2 * BLK), lambda b, t: (b, 0, 0)),
            pl.BlockSpec((1, BLK, BLK), lambda b, t: (b, 0, 0)),
            pl.BlockSpec((1, BLK, BLK), lambda b, t: (b, 0, 0)),
        ],
        out_shape=[
            jax.ShapeDtypeStruct((bsz, seq, D_MODEL), BF16),
            jax.ShapeDtypeStruct((bsz, ML_HEADS, ML_DK, ML_DV), F32),
            jax.ShapeDtypeStruct((bsz, ML_HEADS, ML_DK), F32),
            jax.ShapeDtypeStruct((bsz, 8, 2 * BLK), F32),
            jax.ShapeDtypeStruct((bsz, BLK, BLK), F32),
            jax.ShapeDtypeStruct((bsz, BLK, BLK), F32),
        ],
        scratch_shapes=[
            pltpu.VMEM((T_COLS, D_MODEL), BF16),
            pltpu.VMEM((tm, ML_HEADS * ML_DK), BF16),
            pltpu.VMEM((tm, B_COLS), F32),
            pltpu.VMEM((tm + BLK, BLK), BF16),
            pltpu.VMEM((nblk, ML_HEADS * ML_DK, BLK), BF16),
            pltpu.VMEM((nblk, ML_WIDTH, BLK), BF16),
            pltpu.VMEM((nblk, SWA_WIDTH, BLK), BF16),
            pltpu.VMEM((nblk + 1, SWA_KV_HEADS * SWA_HD, BLK), BF16),
            pltpu.VMEM((nblk, T_GROWS, BLK), F32),
            pltpu.VMEM((ML_HEADS // 2, STATE_ROWS, 2 * ML_DK), F32),
            pltpu.VMEM((8, 2 * BLK), F32),
            pltpu.VMEM((2, SWA_HEADS // 2, BLK, 2 * BLK), F32),
        ],
        compiler_params=pltpu.CompilerParams(
            dimension_semantics=("arbitrary", "arbitrary"),
            vmem_limit_bytes=VMEM_LIMIT),
        name="prompt_mixer",
    )(x, wall, gbc, mlg, sinks, rb)


def _roll_rows(x, shift):
    return pltpu.roll(x, shift % x.shape[0], 0)


def _sample_mixer_kernel(x_ref, w_ref, gb_ref, mlg_ref, sinks_ref, rb_ref, mrep_ref,
                         c_ref, n_ref, kc_ref, vc_ref,
                         mix_ref, c_out, n_out, mo_ref, ko_ref, vo_ref,
                         pa, pf, tblc, tbln):
    step = pl.program_id(0)
    rows_total = x_ref.shape[0]
    tdec = 4
    nslab = rows_total // SLAB
    qrows = SWA_GROUP * SLAB

    @pl.when(step == 0)
    def _build_bias_tables():
        r = lax.broadcasted_iota(jnp.int32, (qrows, BLK), 0)
        c = lax.broadcasted_iota(jnp.int32, (qrows, BLK), 1)
        hl = r // SLAB
        bl = (r % SLAB) // tdec
        tq = r % tdec
        d_cache = WINDOW + tq - c
        d_new = tq - (c % tdec)
        bk_cache = _t5_bucket(d_cache)
        bk_new = _t5_bucket(d_new)
        ok_cache = c > tq
        ok_new = (c < SLAB) & ((c // tdec) == bl) & (d_new >= 0)
        for g in range(SWA_KV_HEADS):
            acc_c = jnp.zeros((qrows, BLK), F32)
            acc_n = jnp.zeros((qrows, BLK), F32)
            for k in range(REL_BUCKETS):
                for hh in range(SWA_GROUP):
                    val = rb_ref[k, g * SWA_GROUP + hh]
                    acc_c = jnp.where((bk_cache == k) & (hl == hh), val, acc_c)
                    acc_n = jnp.where((bk_new == k) & (hl == hh), val, acc_n)
            tblc[g] = jnp.where(ok_cache, acc_c, NEG_INF)
            tbln[g] = jnp.where(ok_new, acc_n, NEG_INF)

    xb = x_ref[...].astype(BF16)
    tq, tv = N_COLS + T_Q, N_COLS + T_V
    pa[:, A_Q:A_K] = _dot(xb, w_ref[:, tq:tq + ML_HEADS * ML_DK])
    pa[:, A_K:A_V] = _dot(xb, w_ref[:, N_K:N_K + ML_HEADS * ML_DK])
    pa[:, A_V:A_COLS] = _dot(xb, w_ref[:, tv:tv + ML_WIDTH + SWA_WIDTH])
    pf[...] = _dot(xb, w_ref[:, N_B:N_B + B_COLS])

    gbias = gb_ref[...]
    mlg = mlg_ref[...]
    r16 = lax.broadcasted_iota(jnp.int32, (SLAB, BLK), 0)
    rr = r16 % tdec
    bl16 = r16 // tdec
    r64 = lax.broadcasted_iota(jnp.int32, (qrows, BLK), 0)
    bl64 = (r64 % SLAB) // tdec
    zeros_pad = jnp.zeros((BLK - SLAB, SWA_HD), BF16)

    def seg_last(x):
        return jnp.where(rr == 3, x,
                         jnp.where(rr == 2, _roll_rows(x, -1),
                                   jnp.where(rr == 1, _roll_rows(x, -2), _roll_rows(x, -3))))

    def seg_max(x):
        m1 = jnp.maximum(x, jnp.where(rr % 2 == 0, _roll_rows(x, -1), _roll_rows(x, 1)))
        return jnp.maximum(m1, jnp.where(rr < 2, _roll_rows(m1, -2), _roll_rows(m1, 2)))

    def slab(si, carry):
        r0 = pl.multiple_of(si * SLAB, SLAB)
        rows = pl.ds(r0, SLAB)
        seq0 = si * (SLAB // tdec)

        pre = _softcap(pf[rows, B_G:B_G + BLK] + gbias)
        logf = _log_sigmoid(pre)
        y = logf + jnp.where(rr >= 1, _roll_rows(logf, 1), 0.0)
        bcum = y + jnp.where(rr >= 2, _roll_rows(y, 2), 0.0)
        ig = pltpu.roll(pre, ML_HEADS, 1)
        m_prev = mrep_ref[rows, :]
        b_minus_i = bcum - ig
        log_d = [jnp.where(rr >= dl, bcum - _roll_rows(b_minus_i, dl), -jnp.inf) for dl in range(tdec)]
        rowmax = functools.reduce(jnp.maximum, log_d)
        inter = bcum + m_prev
        m_row = jnp.maximum(inter, rowmax)
        scl = jnp.exp(inter - m_row)
        dm = [jnp.exp(ld - m_row) for ld in log_d]
        enm = jnp.exp(-m_row)
        b_last = seg_last(bcum)
        w = b_last - bcum + ig
        m_new = jnp.maximum(b_last + m_prev, seg_max(w))
        a = jnp.exp(b_last + m_prev - m_new)
        kws = jnp.exp(w - m_new) * ML_SCALE
        mo_ref[rows, :] = m_new

        for h in range(ML_HEADS):
            hs = slice(h * ML_DK, (h + 1) * ML_DK)
            lane = ML_HEADS + h
            q = pa[rows, A_Q + h * ML_DK:A_Q + (h + 1) * ML_DK]
            k = pa[rows, A_K + h * ML_DK:A_K + (h + 1) * ML_DK]
            v = pa[rows, A_V + h * ML_DV:A_V + (h + 1) * ML_DV]
            bm = jnp.zeros((SLAB, ML_DV), F32)
            sum_s = jnp.zeros((SLAB, 1), F32)
            for dl in range(tdec):
                s_dl = jnp.sum(q * _roll_rows(k, dl), axis=-1, keepdims=True)
                ws = s_dl * ML_SCALE * dm[dl][:, lane:lane + 1]
                bm = bm + ws * _roll_rows(v, dl)
                sum_s = sum_s + ws
            qc = jnp.zeros((SLAB, ML_DV), F32)
            nsel = jnp.zeros((SLAB, ML_DK), F32)
            for bl in range(SLAB // tdec):
                msk = bl16 == bl
                c_b = c_ref[seq0 + bl, h]
                n_b = n_ref[seq0 + bl][h:h + 1, :]
                qc = qc + _dot(jnp.where(msk, q, 0.0).astype(BF16), c_b.astype(BF16))
                nsel = jnp.where(msk, n_b, nsel)
                a_b = a[bl * tdec:bl * tdec + 1, lane:lane + 1]
                kw = jnp.where(msk, k * kws[:, lane:lane + 1], 0.0)
                c_out[seq0 + bl, h] = a_b * c_b + _dot_tn(kw.astype(BF16), v.astype(BF16))
                n_out[seq0 + bl, h:h + 1, :] = a_b * n_b + jnp.sum(kw, axis=0, keepdims=True)
            sclh = scl[:, lane:lane + 1]
            num = sclh * qc + bm
            den = sclh * jnp.sum(q * nsel, axis=-1, keepdims=True) + sum_s
            hh = num / jnp.maximum(jnp.abs(den), enm[:, lane:lane + 1])
            hn = hh * lax.rsqrt(jnp.mean(hh * hh, axis=-1, keepdims=True) + NORM_EPS)
            og = jax.nn.sigmoid(pf[rows, B_MO + h * ML_DV:B_MO + (h + 1) * ML_DV])
            mix_ref[rows, hs] = (hn * mlg[:, hs] * og).astype(mix_ref.dtype)

        knew = pf[rows, B_SK:B_SK + BLK]
        vnew = pf[rows, B_SV:B_SV + BLK]
        for g in range(SWA_KV_HEADS):
            gs = slice(g * SWA_HD, (g + 1) * SWA_HD)
            qg = jnp.concatenate(
                [pa[rows, A_SQ + (g * SWA_GROUP + hl) * SWA_HD:A_SQ + (g * SWA_GROUP + hl + 1) * SWA_HD]
                 for hl in range(SWA_GROUP)], axis=0).astype(BF16)
            kn = jnp.concatenate([knew[:, gs].astype(BF16), zeros_pad], axis=0)
            vn = jnp.concatenate([vnew[:, gs].astype(BF16), zeros_pad], axis=0)
            s_c = jnp.zeros((qrows, BLK), F32)
            for bl in range(SLAB // tdec):
                kc = kc_ref[seq0 + bl][:, gs].astype(BF16)
                s_c = jnp.where(bl64 == bl, _dot_nt(qg, kc), s_c)
            s_c = s_c * SWA_SCALE + tblc[g]
            s_n = _dot_nt(qg, kn) * SWA_SCALE + tbln[g]
            hl64 = r64[:, 0:1] // SLAB
            snk = jnp.zeros((qrows, 1), F32)
            for hl in range(SWA_GROUP):
                snk = jnp.where(hl64 == hl, sinks_ref[g * SWA_GROUP + hl], snk)
            mx = jnp.maximum(jnp.maximum(jnp.max(s_c, axis=-1, keepdims=True),
                                         jnp.max(s_n, axis=-1, keepdims=True)), snk)
            p_c = jnp.exp(s_c - mx)
            p_n = jnp.exp(s_n - mx)
            den = (jnp.sum(p_c, axis=-1, keepdims=True) + jnp.sum(p_n, axis=-1, keepdims=True)
                   + jnp.exp(snk - mx))
            o = _dot(p_n.astype(BF16), vn)
            for bl in range(SLAB // tdec):
                vc = vc_ref[seq0 + bl][:, gs].astype(BF16)
                o = o + _dot(jnp.where(bl64 == bl, p_c, 0.0).astype(BF16), vc)
            o = o / den
            og = jnp.concatenate([o[hl * SLAB:(hl + 1) * SLAB, :] for hl in range(SWA_GROUP)], axis=1)
            c0 = ML_WIDTH + g * SWA_GROUP * SWA_HD
            mix_ref[rows, c0:c0 + SWA_GROUP * SWA_HD] = og.astype(mix_ref.dtype)

        for bl in range(SLAB // tdec):
            ko_ref[seq0 + bl, 0:WINDOW - tdec, :] = kc_ref[seq0 + bl, tdec:WINDOW, :]
            vo_ref[seq0 + bl, 0:WINDOW - tdec, :] = vc_ref[seq0 + bl, tdec:WINDOW, :]
            ko_ref[seq0 + bl, WINDOW - tdec:WINDOW, :] = knew[bl * tdec:(bl + 1) * tdec, :]
            vo_ref[seq0 + bl, WINDOW - tdec:WINDOW, :] = vnew[bl * tdec:(bl + 1) * tdec, :]
        return carry

    lax.fori_loop(0, nslab, slab, 0)


def _sample_mixer(x2d, wall, gb, mlg, sinks, rb, mrep, c0, n0, kc, vc):
    rows = x2d.shape[0]
    tdec = 4
    nseq = rows // tdec
    gbs = SAMPLE_GB
    steps = nseq // gbs
    rb_rows = gbs * tdec
    const2 = lambda i: (0, 0)
    smem = pl.BlockSpec(memory_space=pltpu.SMEM)
    return pl.pallas_call(
        _sample_mixer_kernel,
        grid=(steps,),
        in_specs=[
            pl.BlockSpec((rb_rows, D_MODEL), lambda i: (i, 0)),
            pl.BlockSpec((D_MODEL, W_COLS), const2, pipeline_mode=pl.Buffered(1)),
            pl.BlockSpec((1, BLK), const2),
            pl.BlockSpec((1, ML_WIDTH), const2),
            smem, smem,
            pl.BlockSpec((rb_rows, BLK), lambda i: (i, 0)),
            pl.BlockSpec((gbs, ML_HEADS, ML_DK, ML_DV), lambda i: (i, 0, 0, 0)),
            pl.BlockSpec((gbs, ML_HEADS, ML_DK), lambda i: (i, 0, 0)),
            pl.BlockSpec((gbs, WINDOW, BLK), lambda i: (i, 0, 0)),
            pl.BlockSpec((gbs, WINDOW, BLK), lambda i: (i, 0, 0)),
        ],
        out_specs=[
            pl.BlockSpec((rb_rows, D_MODEL), lambda i: (i, 0)),
            pl.BlockSpec((gbs, ML_HEADS, ML_DK, ML_DV), lambda i: (i, 0, 0, 0)),
            pl.BlockSpec((gbs, ML_HEADS, ML_DK), lambda i: (i, 0, 0)),
            pl.BlockSpec((rb_rows, BLK), lambda i: (i, 0)),
            pl.BlockSpec((gbs, WINDOW, BLK), lambda i: (i, 0, 0)),
            pl.BlockSpec((gbs, WINDOW, BLK), lambda i: (i, 0, 0)),
        ],
        out_shape=[
            jax.ShapeDtypeStruct((rows, D_MODEL), BF16),
            jax.ShapeDtypeStruct((nseq, ML_HEADS, ML_DK, ML_DV), F32),
            jax.ShapeDtypeStruct((nseq, ML_HEADS, ML_DK), F32),
            jax.ShapeDtypeStruct((rows, BLK), F32),
            jax.ShapeDtypeStruct((nseq, WINDOW, BLK), F32),
            jax.ShapeDtypeStruct((nseq, WINDOW, BLK), F32),
        ],
        scratch_shapes=[
            pltpu.VMEM((rb_rows, A_COLS), F32),
            pltpu.VMEM((rb_rows, B_COLS), F32),
            pltpu.VMEM((SWA_KV_HEADS, SWA_GROUP * SLAB, BLK), F32),
            pltpu.VMEM((SWA_KV_HEADS, SWA_GROUP * SLAB, BLK), F32),
        ],
        compiler_params=pltpu.CompilerParams(
            dimension_semantics=("arbitrary",),
            vmem_limit_bytes=VMEM_LIMIT),
        name="sample_mixer",
    )(x2d, wall, gb, mlg, sinks, rb, mrep, c0, n0, kc, vc)


def _dense_kernel(x_ref, mix_ref, wo_ref, g1_ref, b1_ref, wfi_ref, wfo_ref, g2_ref, b2_ref, y_ref):
    nsub = x_ref.shape[0] // DENSE_SUB
    subs = [slice(i * DENSE_SUB, (i + 1) * DENSE_SUB) for i in range(nsub)]
    chunks = list(zip(FF_SPLITS[:-1], FF_SPLITS[1:]))
    g1, b1, g2, b2 = g1_ref[...], b1_ref[...], g2_ref[...], b2_ref[...]

    proj = [_dot(mix_ref[s, :], wo_ref[...]) for s in subs]
    h1 = [_layer_norm(ALPHA * x_ref[s, :] + proj[i], g1, b1) for i, s in enumerate(subs)]
    h1b = [h.astype(BF16) for h in h1]
    acc = [None] * nsub
    for c0, c1 in chunks:
        gate = [_dot(h1b[i], wfi_ref[:, c0:c1]) for i in range(nsub)]
        up = [_dot(h1b[i], wfi_ref[:, D_FF + c0:D_FF + c1]) for i in range(nsub)]
        for i in range(nsub):
            act = (gate[i] * jax.nn.sigmoid(gate[i]) * up[i]).astype(BF16)
            part = _dot(act, wfo_ref[c0:c1, :])
            acc[i] = part if acc[i] is None else acc[i] + part
    for i, s in enumerate(subs):
        y_ref[s, :] = _layer_norm(ALPHA * h1[i] + acc[i], g2, b2)


def _dense(x2d, mix2d, wo, g1, b1, wfi, wfo, g2, b2):
    rows = x2d.shape[0]
    tm = min(TM_DENSE, rows)
    const2 = lambda i: (0, 0)

    def wspec(shape):
        return pl.BlockSpec(shape, const2, pipeline_mode=pl.Buffered(1))

    return pl.pallas_call(
        _dense_kernel,
        grid=(rows // tm,),
        in_specs=[
            pl.BlockSpec((tm, D_MODEL), lambda i: (i, 0)),
            pl.BlockSpec((tm, D_MODEL), lambda i: (i, 0)),
            wspec((D_MODEL, D_MODEL)),
            wspec((1, D_MODEL)), wspec((1, D_MODEL)),
            wspec((D_MODEL, 2 * D_FF)),
            wspec((D_FF, D_MODEL)),
            wspec((1, D_MODEL)), wspec((1, D_MODEL)),
        ],
        out_specs=pl.BlockSpec((tm, D_MODEL), lambda i: (i, 0)),
        out_shape=jax.ShapeDtypeStruct((rows, D_MODEL), F32),
        compiler_params=pltpu.CompilerParams(
            dimension_semantics=("arbitrary",),
            vmem_limit_bytes=VMEM_LIMIT),
        name="dense",
    )(x2d, mix2d, wo, g1, b1, wfi, wfo, g2, b2)


def kernel(x_prompt, x_sample, state_mlstm_C, state_mlstm_n, state_mlstm_m, cache_swa_k, cache_swa_v,
           w_in, b_igate, b_fgate, ml_norm_g, swa_sinks, rel_bias, w_out, ln1_g, ln1_b,
           w_ffn_in, w_ffn_out, ln2_g, ln2_b):
    bp, seq, _ = x_prompt.shape
    bs, tdec, _ = x_sample.shape
    l = 0

    offs = [0]
    for n in IN_SIZES:
        offs.append(offs[-1] + n)
    w = w_in[l]
    mq, mk, mv, mo, mi, mf, sq, sk, sv = [w[:, offs[i]:offs[i + 1]] for i in range(len(IN_SIZES))]
    gpad = jnp.zeros((D_MODEL, BLK - 2 * ML_HEADS), w.dtype)
    wall = jnp.concatenate([mk, mo, sk, sv, mi, mf, gpad,
                            mq, mv, sq, sv, mi, mf, gpad],
                           axis=1).astype(BF16)
    gvec = jnp.concatenate([b_igate[l], b_fgate[l]]).astype(F32)
    gb = jnp.pad(gvec, (0, BLK - 2 * ML_HEADS))[None, :]
    gbc = jnp.broadcast_to(jnp.pad(gvec, (0, T_GROWS - 2 * ML_HEADS))[:, None], (T_GROWS, BLK))
    mlg = ml_norm_g[l][None, :].astype(F32)
    sinks = swa_sinks[l].astype(F32)
    rb = rel_bias.astype(F32)
    wo = w_out[l].astype(BF16)
    wfi = w_ffn_in[l].astype(BF16)
    wfo = w_ffn_out[l].astype(BF16)
    g1, b1 = ln1_g[l][None, :], ln1_b[l][None, :]
    g2, b2 = ln2_g[l][None, :], ln2_b[l][None, :]

    mix_p, c_p, n_p, m_p, k_p, v_p = _prompt_mixer(x_prompt, wall, gbc, mlg, sinks, rb)
    y_p = _dense(x_prompt.reshape(bp * seq, D_MODEL), mix_p.reshape(bp * seq, D_MODEL),
                 wo, g1, b1, wfi, wfo, g2, b2).reshape(bp, seq, D_MODEL)
    p_m = m_p[:, :ML_HEADS // 2, ::BLK].reshape(bp, ML_HEADS)[None]
    p_k = k_p.reshape(1, bp, WINDOW, SWA_KV_HEADS, SWA_HD)
    p_v = v_p.reshape(1, bp, WINDOW, SWA_KV_HEADS, SWA_HD)

    xs = x_sample.reshape(bs * tdec, D_MODEL)
    m0 = state_mlstm_m[l].astype(F32)
    mrep = jnp.pad(jnp.repeat(m0, tdec, axis=0), ((0, 0), (ML_HEADS, BLK - 2 * ML_HEADS)))
    wlen = cache_swa_k.shape[2]
    kc = cache_swa_k[l].reshape(bs, wlen, SWA_KV_HEADS * SWA_HD)
    vc = cache_swa_v[l].reshape(bs, wlen, SWA_KV_HEADS * SWA_HD)
    mix_s, c_s, n_s, mo_s, k_s, v_s = _sample_mixer(
        xs, wall, gb, mlg, sinks, rb, mrep,
        state_mlstm_C[l].astype(F32), state_mlstm_n[l].astype(F32), kc, vc)
    y_s = _dense(xs, mix_s, wo, g1, b1, wfi, wfo, g2, b2).reshape(bs, tdec, D_MODEL)
    s_m = mo_s.reshape(bs, tdec, BLK)[:, 0, ML_HEADS:2 * ML_HEADS][None]
    s_k = k_s.reshape(1, bs, wlen, SWA_KV_HEADS, SWA_HD)
    s_v = v_s.reshape(1, bs, wlen, SWA_KV_HEADS, SWA_HD)

    return (y_p, y_s, c_p[None], n_p[None], p_m, p_k, p_v,
            c_s[None], n_s[None], s_m, s_k, s_v)
```

```python
import functools
import math

import jax
import jax.numpy as jnp
from jax import lax
from jax.experimental import pallas as pl
from jax.experimental.pallas import tpu as pltpu

F32 = jnp.float32
BF16 = jnp.bfloat16

D_MODEL = 1024
ML_HEADS = 4
ML_DK = 128
ML_DV = 128
ML_WIDTH = ML_HEADS * ML_DV
GATE_SOFTCAP = 15.0
SWA_HEADS = 8
SWA_KV_HEADS = 2
SWA_GROUP = SWA_HEADS // SWA_KV_HEADS
SWA_HD = 64
SWA_WIDTH = SWA_HEADS * SWA_HD
WINDOW = 128
REL_BUCKETS = 32
REL_MAX_DIST = 128
D_FF = 2816
DEPTH = 1
ALPHA = (2.0 * DEPTH) ** 0.25
LN_EPS = 1e-5
NORM_EPS = 1e-6
NEG_INF = -1e30
IN_SIZES = (512, 512, 512, 512, 4, 4, 512, 128, 128)
ML_SCALE = ML_DK ** -0.5
LOG_ML_SCALE = math.log(ML_SCALE)
SWA_SCALE = SWA_HD ** -0.5

BLK = 128

A_Q, A_K, A_V, A_SQ = 0, 512, 1024, 1536
A_COLS = 2048
B_MO, B_SK, B_SV, B_G = 0, 512, 640, 768
B_COLS = 896

N_K = 0
N_B = 512
N_COLS = N_B + B_COLS
T_Q, T_V, T_SQ, T_SV, T_G = 0, 512, 1024, 1536, 1664
T_GROWS = 16
T_COLS = T_G + BLK
W_COLS = N_COLS + T_COLS
STATE_ROWS = ML_DV + 16

TM_PROMPT = 1024
TM_DENSE = 1024
DENSE_SUB = 512
FF_SPLITS = (0, 1536, D_FF)
SAMPLE_GB = 16
SLAB = 16
VMEM_LIMIT = 56 * 1024 * 1024


def _softcap(a):
    return GATE_SOFTCAP * jnp.tanh(a / GATE_SOFTCAP)


def _log_sigmoid(x):
    return jnp.minimum(x, 0.0) - jnp.log1p(jnp.exp(-jnp.abs(x)))


def _layer_norm(z, g, b):
    mu = jnp.mean(z, axis=-1, keepdims=True)
    zc = z - mu
    var = jnp.mean(zc * zc, axis=-1, keepdims=True)
    return zc * lax.rsqrt(var + LN_EPS) * g + b


def _dot(a, b):
    return jnp.dot(a, b, preferred_element_type=F32)


def _dot_nt(a, b):
    return lax.dot_general(a, b, (((1,), (1,)), ((), ())), preferred_element_type=F32)


def _dot_tn(a, b):
    return lax.dot_general(a, b, (((0,), (0,)), ((), ())), preferred_element_type=F32)


def _split3(x):
    hi = x.astype(BF16)
    r1 = x - hi.astype(F32)
    mid = r1.astype(BF16)
    lo = (r1 - mid.astype(F32)).astype(BF16)
    return hi, mid, lo


def _t5_bucket(d):
    n = jnp.maximum(d, 0)
    max_exact = REL_BUCKETS // 2
    nlog = REL_BUCKETS - max_exact
    large = jnp.full(n.shape, max_exact, jnp.int32)
    for k in range(1, nlog):
        thr = math.ceil(max_exact * (REL_MAX_DIST / max_exact) ** (k / nlog))
        large = large + jnp.where(n >= thr, 1, 0)
    return jnp.where(n < max_exact, n, large)


def _prompt_mixer_kernel(x_ref, w_ref, gbc_ref, mlg_ref, sinks_ref, rb_ref,
                         mix_ref, c_out, n_out, m_out, ko_ref, vo_ref,
                         wt, kn, pf, kband, q_t, v_t, sq_t, sv_t, g_t, ct, msc, tbl):
    b = pl.program_id(0)
    t = pl.program_id(1)
    nt = pl.num_programs(1)
    tm = x_ref.shape[1]
    nblk = tm // BLK
    pairs = SWA_HEADS // 2

    @pl.when((b == 0) & (t == 0))
    def _transpose_feature_major_weights():
        for i in range(T_COLS // BLK):
            wt[i * BLK:(i + 1) * BLK, :] = w_ref[:, N_COLS + i * BLK:N_COLS + (i + 1) * BLK].T

    @pl.when((b == 0) & (t == 0))
    def _build_bias_tables():
        r = lax.broadcasted_iota(jnp.int32, (BLK, 2 * BLK), 0)
        ln = lax.broadcasted_iota(jnp.int32, (BLK, 2 * BLK), 1)
        second = ln >= BLK
        qi = jnp.where(second, ln - BLK, ln)
        prev = r > qi
        d = jnp.where(prev, WINDOW + qi - r, qi - r)
        bucket = _t5_bucket(d)
        for p in range(pairs):
            acc = jnp.zeros((BLK, 2 * BLK), F32)
            for k in range(REL_BUCKETS):
                acc = jnp.where(bucket == k, jnp.where(second, rb_ref[k, p + pairs], rb_ref[k, p]), acc)
            tbl[0, p] = jnp.where(prev, NEG_INF, acc)
            tbl[1, p] = acc

    @pl.when(t == 0)
    def _reset_state():
        ct[...] = jnp.zeros_like(ct)
        msc[...] = jnp.zeros_like(msc)
        kband[0:BLK, :] = jnp.zeros((BLK, kband.shape[1]), kband.dtype)
        sv_t[0] = jnp.zeros(sv_t.shape[1:], sv_t.dtype)

    xb = x_ref[0].astype(BF16)
    kn[...] = _dot(xb, w_ref[:, N_K:N_K + ML_HEADS * ML_DK]).astype(BF16)
    pf[...] = _dot(xb, w_ref[:, N_B:N_B + B_COLS])
    kband[BLK:BLK + tm, :] = pf[:, B_SK:B_SK + BLK].astype(BF16)

    res_t = _dot_nt(wt[...], xb)

    def put(dst, r0, nrows, off, scale=None):
        res = res_t[r0:r0 + nrows, :]
        if scale is not None:
            res = res * scale
        for jj in range(nblk):
            dst[jj + off] = res[:, jj * BLK:(jj + 1) * BLK].astype(dst.dtype)

    put(q_t, T_Q, ML_HEADS * ML_DK, 0)
    put(v_t, T_V, ML_WIDTH, 0)
    put(sq_t, T_SQ, SWA_WIDTH, 0, SWA_SCALE)
    put(sv_t, T_SV, SWA_KV_HEADS * SWA_HD, 1)
    put(g_t, T_G, T_GROWS, 0)

    row = lax.broadcasted_iota(jnp.int32, (BLK, BLK), 0)
    col = lax.broadcasted_iota(jnp.int32, (BLK, BLK), 1)
    causal_t = row <= col
    triu = causal_t.astype(BF16)
    causal2 = jnp.concatenate([causal_t, causal_t], axis=1)
    r16 = lax.broadcasted_iota(jnp.int32, (16, 2 * BLK), 0)
    ones_rows = (r16 == 0).astype(BF16)
    z128 = jnp.zeros((BLK, BLK), BF16)
    ml_pairs = ML_HEADS // 2
    zero_rows = jnp.zeros((BLK - T_GROWS, BLK), F32)
    gbc = gbc_ref[...]
    mlg = mlg_ref[...]
    lane2 = lax.broadcasted_iota(jnp.int32, (1, 2 * BLK), 1)
    snk_rows = [jnp.where(lane2 >= BLK, sinks_ref[p + pairs], sinks_ref[p]) for p in range(pairs)]
    zq = jnp.zeros((SWA_HD, BLK), BF16)
    prev2 = jnp.concatenate([row > col, row > col], axis=1)

    def pair_row(x0, x1):
        return jnp.concatenate([jnp.broadcast_to(x0, (1, BLK)), jnp.broadcast_to(x1, (1, BLK))], axis=1)

    def block_diag(x0, x1):
        return jnp.concatenate([jnp.concatenate([x0, z128], axis=1),
                                jnp.concatenate([z128, x1], axis=1)], axis=0)

    def block(j, carry):
        r0 = pl.multiple_of(j * BLK, BLK)
        rows = pl.ds(r0, BLK)
        hsl = [slice(h * ML_DK, (h + 1) * ML_DK) for h in range(ML_HEADS)]
        psl = [slice(p * 2 * ML_DK, (p + 1) * 2 * ML_DK) for p in range(ml_pairs)]

        pre_t = _softcap(g_t[j] + gbc)
        hi, mid, lo = _split3(_log_sigmoid(pre_t))
        b_t = _dot(hi, triu) + _dot(mid, triu) + _dot(lo, triu)
        u_t = pltpu.roll(pre_t, ML_HEADS, 0) - b_t
        u_c = jnp.concatenate([u_t, zero_rows], axis=0).T

        k2 = [kn[rows, psl[p]] for p in range(ml_pairs)]
        bdq = [block_diag(q_t[j, hsl[2 * p], :], q_t[j, hsl[2 * p + 1], :]) for p in range(ml_pairs)]
        vte = [jnp.concatenate([jnp.concatenate([v_t[j, hsl[2 * p], :], v_t[j, hsl[2 * p + 1], :]], axis=1),
                                ones_rows], axis=0) for p in range(ml_pairs)]
        kq = [_dot(k2[p], bdq[p]) for p in range(ml_pairs)]

        sel = jnp.where((t == 0) & (j == 0), 0, 1)
        kb = kband[pl.ds(r0, 2 * BLK), :]
        vtb = jnp.concatenate([sv_t[j], sv_t[j + 1]], axis=1)
        sc2 = []
        for p in range(pairs):
            qa = sq_t[j, p * SWA_HD:(p + 1) * SWA_HD, :]
            qb = sq_t[j, (p + pairs) * SWA_HD:(p + pairs + 1) * SWA_HD, :]
            bd = jnp.concatenate([jnp.concatenate([qa, zq], axis=1),
                                  jnp.concatenate([zq, qb], axis=1)], axis=0)
            sc2.append(_dot(kb, bd))

        bds, scl, m_row, m_new, a_p, vwte = [], [], [], [], [], []
        for p in range(ml_pairs):
            h0, h1 = 2 * p, 2 * p + 1
            b_row = pair_row(b_t[ML_HEADS + h0:ML_HEADS + h0 + 1, :], b_t[ML_HEADS + h1:ML_HEADS + h1 + 1, :])
            i_row = pair_row(pre_t[h0:h0 + 1, :], pre_t[h1:h1 + 1, :])
            u_col = jnp.concatenate(
                [jnp.broadcast_to(u_c[:, ML_HEADS + h0:ML_HEADS + h0 + 1], (BLK, BLK)),
                 jnp.broadcast_to(u_c[:, ML_HEADS + h1:ML_HEADS + h1 + 1], (BLK, BLK))], axis=1)
            m_prev = msc[p:p + 1, :]
            log_d = jnp.where(causal2, b_row + u_col, -jnp.inf)
            inter = b_row + m_prev
            mr = jnp.maximum(inter, jnp.max(log_d, axis=0, keepdims=True))
            m_row.append(mr)
            scl.append(jnp.exp(inter - mr))
            s2 = (kq[p] * jnp.exp(log_d - (mr - LOG_ML_SCALE))).astype(BF16)
            bds.append(block_diag(s2[:, :BLK], s2[:, BLK:]))
            b_last = pair_row(b_row[:, BLK - 1:BLK], b_row[:, 2 * BLK - 1:2 * BLK])
            w_row = b_last - b_row + i_row
            w_max = pair_row(jnp.max(w_row[:, :BLK], axis=1, keepdims=True),
                             jnp.max(w_row[:, BLK:], axis=1, keepdims=True))
            mn = jnp.maximum(b_last + m_prev, w_max)
            m_new.append(mn)
            a_p.append(jnp.exp(b_last + m_prev - mn))
            wexp = jnp.exp(w_row - mn) * ML_SCALE
            vwte.append(vte[p] * wexp.astype(BF16))

        pw, den_p = [], []
        for p in range(pairs):
            sc = jnp.where(prev2, sc2[p][:BLK], sc2[p][BLK:]) + tbl[sel, p]
            snk = snk_rows[p]
            mx = jnp.maximum(jnp.max(sc, axis=0, keepdims=True), snk)
            pe = jnp.exp(sc - mx)
            den_p.append(jnp.sum(pe, axis=0, keepdims=True) + jnp.exp(snk - mx))
            pw.append(jnp.concatenate([jnp.where(prev2, pe, 0.0), jnp.where(prev2, 0.0, pe)],
                                      axis=0).astype(BF16))

        c_old = [ct[p] for p in range(ml_pairs)]
        cq = [_dot(c_old[p].astype(BF16), bdq[p]) for p in range(ml_pairs)]
        vs = [_dot(vte[p], bds[p]) for p in range(ml_pairs)]
        upd = [_dot(vwte[p], block_diag(k2[p][:, :BLK], k2[p][:, BLK:])) for p in range(ml_pairs)]
        o2 = [_dot(vtb, pw[p]) for p in range(pairs)]

        for p in range(ml_pairs):
            nd = scl[p] * cq[p] + vs[p]
            num = nd[:ML_DV]
            den = nd[ML_DV:ML_DV + 1]
            hh = num / jnp.maximum(jnp.abs(den), jnp.exp(-m_row[p]))
            hn = hh * lax.rsqrt(jnp.mean(hh * hh, axis=0, keepdims=True) + NORM_EPS)
            for half in range(2):
                h = 2 * p + half
                og = jax.nn.sigmoid(pf[rows, B_MO + h * ML_DV:B_MO + (h + 1) * ML_DV])
                hn_h = hn[:, half * BLK:(half + 1) * BLK].T
                mix_ref[0, rows, hsl[h]] = (hn_h * mlg[:, hsl[h]] * og).astype(mix_ref.dtype)
            ct[p] = a_p[p] * c_old[p] + upd[p]
            msc[p:p + 1, :] = m_new[p]

        o_parts = [None] * SWA_HEADS
        for p in range(pairs):
            on = o2[p] / den_p[p]
            o_parts[p] = on[0:SWA_HD, 0:BLK]
            o_parts[p + pairs] = on[SWA_HD:2 * SWA_HD, BLK:2 * BLK]
        o_t = jnp.concatenate(o_parts, axis=0)
        mix_ref[0, rows, ML_WIDTH:ML_WIDTH + SWA_WIDTH] = o_t.T.astype(mix_ref.dtype)
        return carry

    lax.fori_loop(0, nblk, block, 0, unroll=4)

    kband[0:BLK, :] = kband[tm:tm + BLK, :]
    sv_t[0] = sv_t[nblk]

    @pl.when(t == nt - 1)
    def _write_state():
        for h in range(ML_HEADS):
            p, lanes = h // 2, slice((h % 2) * ML_DK, (h % 2 + 1) * ML_DK)
            c_out[0, h] = ct[p, 0:ML_DV, lanes].T
            n_out[0, h:h + 1, :] = ct[p, ML_DV:ML_DV + 1, lanes]
        m_out[0] = msc[...]
        ko_ref[0] = pf[tm - BLK:tm, B_SK:B_SK + BLK]
        vo_ref[0] = pf[tm - BLK:tm, B_SV:B_SV + BLK]


def _prompt_mixer(x, wall, gbc, mlg, sinks, rb):
    bsz, seq, _ = x.shape
    tm = TM_PROMPT
    nt = seq // tm
    nblk = tm // BLK
    const2 = lambda b, t: (0, 0)
    smem = pl.BlockSpec(memory_space=pltpu.SMEM)
    return pl.pallas_call(
        _prompt_mixer_kernel,
        grid=(bsz, nt),
        in_specs=[
            pl.BlockSpec((1, tm, D_MODEL), lambda b, t: (b, t, 0)),
            pl.BlockSpec((D_MODEL, W_COLS), const2, pipeline_mode=pl.Buffered(1)),
            pl.BlockSpec((T_GROWS, BLK), const2),
            pl.BlockSpec((1, ML_WIDTH), const2),
            smem, smem,
        ],
        out_specs=[
            pl.BlockSpec((1, tm, D_MODEL), lambda b, t: (b, t, 0)),
            pl.BlockSpec((1, ML_HEADS, ML_DK, ML_DV), lambda b, t: (b, 0, 0, 0)),
            pl.BlockSpec((1, ML_HEADS, ML_DK), lambda b, t: (b, 0, 0)),
            pl.BlockSpec((1, 8, 2 * BLK), lambda b, t: (b, 0, 0)),
            pl.BlockSpec((1, BLK, BLK), lambda b, t: (b, 0, 0)),
            pl.BlockSpec((1, BLK, BLK), lambda b, t: (b, 0, 0)),
        ],
        out_shape=[
            jax.ShapeDtypeStruct((bsz, seq, D_MODEL), BF16),
            jax.ShapeDtypeStruct((bsz, ML_HEADS, ML_DK, ML_DV), F32),
            jax.ShapeDtypeStruct((bsz, ML_HEADS, ML_DK), F32),
            jax.ShapeDtypeStruct((bsz, 8, 2 * BLK), F32),
            jax.ShapeDtypeStruct((bsz, BLK, BLK), F32),
            jax.ShapeDtypeStruct((bsz, BLK, BLK), F32),
        ],
        scratch_shapes=[
            pltpu.VMEM((T_COLS, D_MODEL), BF16),
            pltpu.VMEM((tm, ML_HEADS * ML_DK), BF16),
            pltpu.VMEM((tm, B_COLS), F32),
            pltpu.VMEM((tm + BLK, BLK), BF16),
            pltpu.VMEM((nblk, ML_HEADS * ML_DK, BLK), BF16),
            pltpu.VMEM((nblk, ML_WIDTH, BLK), BF16),
            pltpu.VMEM((nblk, SWA_WIDTH, BLK), BF16),
            pltpu.VMEM((nblk + 1, SWA_KV_HEADS * SWA_HD, BLK), BF16),
            pltpu.VMEM((nblk, T_GROWS, BLK), F32),
            pltpu.VMEM((ML_HEADS // 2, STATE_ROWS, 2 * ML_DK), F32),
            pltpu.VMEM((8, 2 * BLK), F32),
            pltpu.VMEM((2, SWA_HEADS // 2, BLK, 2 * BLK), F32),
        ],
        compiler_params=pltpu.CompilerParams(
            dimension_semantics=("arbitrary", "arbitrary"),
            vmem_limit_bytes=VMEM_LIMIT),
        name="prompt_mixer",
    )(x, wall, gbc, mlg, sinks, rb)


def _roll_rows(x, shift):
    return pltpu.roll(x, shift % x.shape[0], 0)


def _sample_mixer_kernel(x_ref, w_ref, gb_ref, mlg_ref, sinks_ref, rb_ref, mrep_ref,
                         c_ref, n_ref, kc_ref, vc_ref,
                         mix_ref, c_out, n_out, mo_ref, ko_ref, vo_ref,
                         pa, pf, tblc, tbln):
    step = pl.program_id(0)
    rows_total = x_ref.shape[0]
    tdec = 4
    nslab = rows_total // SLAB
    qrows = SWA_GROUP * SLAB

    @pl.when(step == 0)
    def _build_bias_tables():
        r = lax.broadcasted_iota(jnp.int32, (qrows, BLK), 0)
        c = lax.broadcasted_iota(jnp.int32, (qrows, BLK), 1)
        hl = r // SLAB
        bl = (r % SLAB) // tdec
        tq = r % tdec
        d_cache = WINDOW + tq - c
        d_new = tq - (c % tdec)
        bk_cache = _t5_bucket(d_cache)
        bk_new = _t5_bucket(d_new)
        ok_cache = c > tq
        ok_new = (c < SLAB) & ((c // tdec) == bl) & (d_new >= 0)
        for g in range(SWA_KV_HEADS):
            acc_c = jnp.zeros((qrows, BLK), F32)
            acc_n = jnp.zeros((qrows, BLK), F32)
            for k in range(REL_BUCKETS):
                for hh in range(SWA_GROUP):
                    val = rb_ref[k, g * SWA_GROUP + hh]
                    acc_c = jnp.where((bk_cache == k) & (hl == hh), val, acc_c)
                    acc_n = jnp.where((bk_new == k) & (hl == hh), val, acc_n)
            tblc[g] = jnp.where(ok_cache, acc_c, NEG_INF)
            tbln[g] = jnp.where(ok_new, acc_n, NEG_INF)

    xb = x_ref[...].astype(BF16)
    tq, tv = N_COLS + T_Q, N_COLS + T_V
    pa[:, A_Q:A_K] = _dot(xb, w_ref[:, tq:tq + ML_HEADS * ML_DK])
    pa[:, A_K:A_V] = _dot(xb, w_ref[:, N_K:N_K + ML_HEADS * ML_DK])
    pa[:, A_V:A_COLS] = _dot(xb, w_ref[:, tv:tv + ML_WIDTH + SWA_WIDTH])
    pf[...] = _dot(xb, w_ref[:, N_B:N_B + B_COLS])

    gbias = gb_ref[...]
    mlg = mlg_ref[...]
    r16 = lax.broadcasted_iota(jnp.int32, (SLAB, BLK), 0)
    rr = r16 % tdec
    bl16 = r16 // tdec
    r64 = lax.broadcasted_iota(jnp.int32, (qrows, BLK), 0)
    bl64 = (r64 % SLAB) // tdec
    zeros_pad = jnp.zeros((BLK - SLAB, SWA_HD), BF16)

    def seg_last(x):
        return jnp.where(rr == 3, x,
                         jnp.where(rr == 2, _roll_rows(x, -1),
                                   jnp.where(rr == 1, _roll_rows(x, -2), _roll_rows(x, -3))))

    def seg_max(x):
        m1 = jnp.maximum(x, jnp.where(rr % 2 == 0, _roll_rows(x, -1), _roll_rows(x, 1)))
        return jnp.maximum(m1, jnp.where(rr < 2, _roll_rows(m1, -2), _roll_rows(m1, 2)))

    def slab(si, carry):
        r0 = pl.multiple_of(si * SLAB, SLAB)
        rows = pl.ds(r0, SLAB)
        seq0 = si * (SLAB // tdec)

        pre = _softcap(pf[rows, B_G:B_G + BLK] + gbias)
        logf = _log_sigmoid(pre)
        y = logf + jnp.where(rr >= 1, _roll_rows(logf, 1), 0.0)
        bcum = y + jnp.where(rr >= 2, _roll_rows(y, 2), 0.0)
        ig = pltpu.roll(pre, ML_HEADS, 1)
        m_prev = mrep_ref[rows, :]
        b_minus_i = bcum - ig
        log_d = [jnp.where(rr >= dl, bcum - _roll_rows(b_minus_i, dl), -jnp.inf) for dl in range(tdec)]
        rowmax = functools.reduce(jnp.maximum, log_d)
        inter = bcum + m_prev
        m_row = jnp.maximum(inter, rowmax)
        scl = jnp.exp(inter - m_row)
        dm = [jnp.exp(ld - m_row) for ld in log_d]
        enm = jnp.exp(-m_row)
        b_last = seg_last(bcum)
        w = b_last - bcum + ig
        m_new = jnp.maximum(b_last + m_prev, seg_max(w))
        a = jnp.exp(b_last + m_prev - m_new)
        kws = jnp.exp(w - m_new) * ML_SCALE
        mo_ref[rows, :] = m_new

        for h in range(ML_HEADS):
            hs = slice(h * ML_DK, (h + 1) * ML_DK)
            lane = ML_HEADS + h
            q = pa[rows, A_Q + h * ML_DK:A_Q + (h + 1) * ML_DK]
            k = pa[rows, A_K + h * ML_DK:A_K + (h + 1) * ML_DK]
            v = pa[rows, A_V + h * ML_DV:A_V + (h + 1) * ML_DV]
            bm = jnp.zeros((SLAB, ML_DV), F32)
            sum_s = jnp.zeros((SLAB, 1), F32)
            for dl in range(tdec):
                s_dl = jnp.sum(q * _roll_rows(k, dl), axis=-1, keepdims=True)
                ws = s_dl * ML_SCALE * dm[dl][:, lane:lane + 1]
                bm = bm + ws * _roll_rows(v, dl)
                sum_s = sum_s + ws
            qc = jnp.zeros((SLAB, ML_DV), F32)
            nsel = jnp.zeros((SLAB, ML_DK), F32)
            for bl in range(SLAB // tdec):
                msk = bl16 == bl
                c_b = c_ref[seq0 + bl, h]
                n_b = n_ref[seq0 + bl][h:h + 1, :]
                qc = qc + _dot(jnp.where(msk, q, 0.0).astype(BF16), c_b.astype(BF16))
                nsel = jnp.where(msk, n_b, nsel)
                a_b = a[bl * tdec:bl * tdec + 1, lane:lane + 1]
                kw = jnp.where(msk, k * kws[:, lane:lane + 1], 0.0)
                c_out[seq0 + bl, h] = a_b * c_b + _dot_tn(kw.astype(BF16), v.astype(BF16))
                n_out[seq0 + bl, h:h + 1, :] = a_b * n_b + jnp.sum(kw, axis=0, keepdims=True)
            sclh = scl[:, lane:lane + 1]
            num = sclh * qc + bm
            den = sclh * jnp.sum(q * nsel, axis=-1, keepdims=True) + sum_s
            hh = num / jnp.maximum(jnp.abs(den), enm[:, lane:lane + 1])
            hn = hh * lax.rsqrt(jnp.mean(hh * hh, axis=-1, keepdims=True) + NORM_EPS)
            og = jax.nn.sigmoid(pf[rows, B_MO + h * ML_DV:B_MO + (h + 1) * ML_DV])
            mix_ref[rows, hs] = (hn * mlg[:, hs] * og).astype(mix_ref.dtype)

        knew = pf[rows, B_SK:B_SK + BLK]
        vnew = pf[rows, B_SV:B_SV + BLK]
        for g in range(SWA_KV_HEADS):
            gs = slice(g * SWA_HD, (g + 1) * SWA_HD)
            qg = jnp.concatenate(
                [pa[rows, A_SQ + (g * SWA_GROUP + hl) * SWA_HD:A_SQ + (g * SWA_GROUP + hl + 1) * SWA_HD]
                 for hl in range(SWA_GROUP)], axis=0).astype(BF16)
            kn = jnp.concatenate([knew[:, gs].astype(BF16), zeros_pad], axis=0)
            vn = jnp.concatenate([vnew[:, gs].astype(BF16), zeros_pad], axis=0)
            s_c = jnp.zeros((qrows, BLK), F32)
            for bl in range(SLAB // tdec):
                kc = kc_ref[seq0 + bl][:, gs].astype(BF16)
                s_c = jnp.where(bl64 == bl, _dot_nt(qg, kc), s_c)
            s_c = s_c * SWA_SCALE + tblc[g]
            s_n = _dot_nt(qg, kn) * SWA_SCALE + tbln[g]
            hl64 = r64[:, 0:1] // SLAB
            snk = jnp.zeros((qrows, 1), F32)
            for hl in range(SWA_GROUP):
                snk = jnp.where(hl64 == hl, sinks_ref[g * SWA_GROUP + hl], snk)
            mx = jnp.maximum(jnp.maximum(jnp.max(s_c, axis=-1, keepdims=True),
                                         jnp.max(s_n, axis=-1, keepdims=True)), snk)
            p_c = jnp.exp(s_c - mx)
            p_n = jnp.exp(s_n - mx)
            den = (jnp.sum(p_c, axis=-1, keepdims=True) + jnp.sum(p_n, axis=-1, keepdims=True)
                   + jnp.exp(snk - mx))
            o = _dot(p_n.astype(BF16), vn)
            for bl in range(SLAB // tdec):
                vc = vc_ref[seq0 + bl][:, gs].astype(BF16)
                o = o + _dot(jnp.where(bl64 == bl, p_c, 0.0).astype(BF16), vc)
            o = o / den
            og = jnp.concatenate([o[hl * SLAB:(hl + 1) * SLAB, :] for hl in range(SWA_GROUP)], axis=1)
            c0 = ML_WIDTH + g * SWA_GROUP * SWA_HD
            mix_ref[rows, c0:c0 + SWA_GROUP * SWA_HD] = og.astype(mix_ref.dtype)

        for bl in range(SLAB // tdec):
            ko_ref[seq0 + bl, 0:WINDOW - tdec, :] = kc_ref[seq0 + bl, tdec:WINDOW, :]
            vo_ref[seq0 + bl, 0:WINDOW - tdec, :] = vc_ref[seq0 + bl, tdec:WINDOW, :]
            ko_ref[seq0 + bl, WINDOW - tdec:WINDOW, :] = knew[bl * tdec:(bl + 1) * tdec, :]
            vo_ref[seq0 + bl, WINDOW - tdec:WINDOW, :] = vnew[bl * tdec:(bl + 1) * tdec, :]
        return carry

    lax.fori_loop(0, nslab, slab, 0)


def _sample_mixer(x2d, wall, gb, mlg, sinks, rb, mrep, c0, n0, kc, vc):
    rows = x2d.shape[0]
    tdec = 4
    nseq = rows // tdec
    gbs = SAMPLE_GB
    steps = nseq // gbs
    rb_rows = gbs * tdec
    const2 = lambda i: (0, 0)
    smem = pl.BlockSpec(memory_space=pltpu.SMEM)
    return pl.pallas_call(
        _sample_mixer_kernel,
        grid=(steps,),
        in_specs=[
            pl.BlockSpec((rb_rows, D_MODEL), lambda i: (i, 0)),
            pl.BlockSpec((D_MODEL, W_COLS), const2, pipeline_mode=pl.Buffered(1)),
            pl.BlockSpec((1, BLK), const2),
            pl.BlockSpec((1, ML_WIDTH), const2),
            smem, smem,
            pl.BlockSpec((rb_rows, BLK), lambda i: (i, 0)),
            pl.BlockSpec((gbs, ML_HEADS, ML_DK, ML_DV), lambda i: (i, 0, 0, 0)),
            pl.BlockSpec((gbs, ML_HEADS, ML_DK), lambda i: (i, 0, 0)),
            pl.BlockSpec((gbs, WINDOW, BLK), lambda i: (i, 0, 0)),
            pl.BlockSpec((gbs, WINDOW, BLK), lambda i: (i, 0, 0)),
        ],
        out_specs=[
            pl.BlockSpec((rb_rows, D_MODEL), lambda i: (i, 0)),
            pl.BlockSpec((gbs, ML_HEADS, ML_DK, ML_DV), lambda i: (i, 0, 0, 0)),
            pl.BlockSpec((gbs, ML_HEADS, ML_DK), lambda i: (i, 0, 0)),
            pl.BlockSpec((rb_rows, BLK), lambda i: (i, 0)),
            pl.BlockSpec((gbs, WINDOW, BLK), lambda i: (i, 0, 0)),
            pl.BlockSpec((gbs, WINDOW, BLK), lambda i: (i, 0, 0)),
        ],
        out_shape=[
            jax.ShapeDtypeStruct((rows, D_MODEL), BF16),
            jax.ShapeDtypeStruct((nseq, ML_HEADS, ML_DK, ML_DV), F32),
            jax.ShapeDtypeStruct((nseq, ML_HEADS, ML_DK), F32),
            jax.ShapeDtypeStruct((rows, BLK), F32),
            jax.ShapeDtypeStruct((nseq, WINDOW, BLK), F32),
            jax.ShapeDtypeStruct((nseq, WINDOW, BLK), F32),
        ],
        scratch_shapes=[
            pltpu.VMEM((rb_rows, A_COLS), F32),
            pltpu.VMEM((rb_rows, B_COLS), F32),
            pltpu.VMEM((SWA_KV_HEADS, SWA_GROUP * SLAB, BLK), F32),
            pltpu.VMEM((SWA_KV_HEADS, SWA_GROUP * SLAB, BLK), F32),
        ],
        compiler_params=pltpu.CompilerParams(
            dimension_semantics=("arbitrary",),
            vmem_limit_bytes=VMEM_LIMIT),
        name="sample_mixer",
    )(x2d, wall, gb, mlg, sinks, rb, mrep, c0, n0, kc, vc)


def _dense_kernel(x_ref, mix_ref, wo_ref, g1_ref, b1_ref, wfi_ref, wfo_ref, g2_ref, b2_ref, y_ref):
    nsub = x_ref.shape[0] // DENSE_SUB
    subs = [slice(i * DENSE_SUB, (i + 1) * DENSE_SUB) for i in range(nsub)]
    chunks = list(zip(FF_SPLITS[:-1], FF_SPLITS[1:]))
    g1, b1, g2, b2 = g1_ref[...], b1_ref[...], g2_ref[...], b2_ref[...]

    proj = [_dot(mix_ref[s, :], wo_ref[...]) for s in subs]
    h1 = [_layer_norm(ALPHA * x_ref[s, :] + proj[i], g1, b1) for i, s in enumerate(subs)]
    h1b = [h.astype(BF16) for h in h1]
    acc = [None] * nsub
    for c0, c1 in chunks:
        gate = [_dot(h1b[i], wfi_ref[:, c0:c1]) for i in range(nsub)]
        up = [_dot(h1b[i], wfi_ref[:, D_FF + c0:D_FF + c1]) for i in range(nsub)]
        for i in range(nsub):
            act = (gate[i] * jax.nn.sigmoid(gate[i]) * up[i]).astype(BF16)
            part = _dot(act, wfo_ref[c0:c1, :])
            acc[i] = part if acc[i] is None else acc[i] + part
    for i, s in enumerate(subs):
        y_ref[s, :] = _layer_norm(ALPHA * h1[i] + acc[i], g2, b2)


def _dense(x2d, mix2d, wo, g1, b1, wfi, wfo, g2, b2):
    rows = x2d.shape[0]
    tm = min(TM_DENSE, rows)
    const2 = lambda i: (0, 0)

    def wspec(shape):
        return pl.BlockSpec(shape, const2, pipeline_mode=pl.Buffered(1))

    return pl.pallas_call(
        _dense_kernel,
        grid=(rows // tm,),
        in_specs=[
            pl.BlockSpec((tm, D_MODEL), lambda i: (i, 0)),
            pl.BlockSpec((tm, D_MODEL), lambda i: (i, 0)),
            wspec((D_MODEL, D_MODEL)),
            wspec((1, D_MODEL)), wspec((1, D_MODEL)),
            wspec((D_MODEL, 2 * D_FF)),
            wspec((D_FF, D_MODEL)),
            wspec((1, D_MODEL)), wspec((1, D_MODEL)),
        ],
        out_specs=pl.BlockSpec((tm, D_MODEL), lambda i: (i, 0)),
        out_shape=jax.ShapeDtypeStruct((rows, D_MODEL), F32),
        compiler_params=pltpu.CompilerParams(
            dimension_semantics=("arbitrary",),
            vmem_limit_bytes=VMEM_LIMIT),
        name="dense",
    )(x2d, mix2d, wo, g1, b1, wfi, wfo, g2, b2)


def kernel(x_prompt, x_sample, state_mlstm_C, state_mlstm_n, state_mlstm_m, cache_swa_k, cache_swa_v,
           w_in, b_igate, b_fgate, ml_norm_g, swa_sinks, rel_bias, w_out, ln1_g, ln1_b,
           w_ffn_in, w_ffn_out, ln2_g, ln2_b):
    bp, seq, _ = x_prompt.shape
    bs, tdec, _ = x_sample.shape
    l = 0

    offs = [0]
    for n in IN_SIZES:
        offs.append(offs[-1] + n)
    w = w_in[l]
    mq, mk, mv, mo, mi, mf, sq, sk, sv = [w[:, offs[i]:offs[i + 1]] for i in range(len(IN_SIZES))]
    gpad = jnp.zeros((D_MODEL, BLK - 2 * ML_HEADS), w.dtype)
    wall = jnp.concatenate([mk, mo, sk, sv, mi, mf, gpad,
                            mq, mv, sq, sv, mi, mf, gpad],
                           axis=1).astype(BF16)
    gvec = jnp.concatenate([b_igate[l], b_fgate[l]]).astype(F32)
    gb = jnp.pad(gvec, (0, BLK - 2 * ML_HEADS))[None, :]
    gbc = jnp.broadcast_to(jnp.pad(gvec, (0, T_GROWS - 2 * ML_HEADS))[:, None], (T_GROWS, BLK))
    mlg = ml_norm_g[l][None, :].astype(F32)
    sinks = swa_sinks[l].astype(F32)
    rb = rel_bias.astype(F32)
    wo = w_out[l].astype(BF16)
    wfi = w_ffn_in[l].astype(BF16)
    wfo = w_ffn_out[l].astype(BF16)
    g1, b1 = ln1_g[l][None, :], ln1_b[l][None, :]
    g2, b2 = ln2_g[l][None, :], ln2_b[l][None, :]

    mix_p, c_p, n_p, m_p, k_p, v_p = _prompt_mixer(x_prompt, wall, gbc, mlg, sinks, rb)
    y_p = _dense(x_prompt.reshape(bp * seq, D_MODEL), mix_p.reshape(bp * seq, D_MODEL),
                 wo, g1, b1, wfi, wfo, g2, b2).reshape(bp, seq, D_MODEL)
    p_m = m_p[:, :ML_HEADS // 2, ::BLK].reshape(bp, ML_HEADS)[None]
    p_k = k_p.reshape(1, bp, WINDOW, SWA_KV_HEADS, SWA_HD)
    p_v = v_p.reshape(1, bp, WINDOW, SWA_KV_HEADS, SWA_HD)

    xs = x_sample.reshape(bs * tdec, D_MODEL)
    m0 = state_mlstm_m[l].astype(F32)
    mrep = jnp.pad(jnp.repeat(m0, tdec, axis=0), ((0, 0), (ML_HEADS, BLK - 2 * ML_HEADS)))
    wlen = cache_swa_k.shape[2]
    kc = cache_swa_k[l].reshape(bs, wlen, SWA_KV_HEADS * SWA_HD)
    vc = cache_swa_v[l].reshape(bs, wlen, SWA_KV_HEADS * SWA_HD)
    mix_s, c_s, n_s, mo_s, k_s, v_s = _sample_mixer(
        xs, wall, gb, mlg, sinks, rb, mrep,
        state_mlstm_C[l].astype(F32), state_mlstm_n[l].astype(F32), kc, vc)
    y_s = _dense(xs, mix_s, wo, g1, b1, wfi, wfo, g2, b2).reshape(bs, tdec, D_MODEL)
    s_m = mo_s.reshape(bs, tdec, BLK)[:, 0, ML_HEADS:2 * ML_HEADS][None]
    s_k = k_s.reshape(1, bs, wlen, SWA_KV_HEADS, SWA_HD)
    s_v = v_s.reshape(1, bs, wlen, SWA_KV_HEADS, SWA_HD)

    return (y_p, y_s, c_p[None], n_p[None], p_m, p_k, p_v,
            c_s[None], n_s[None], s_m, s_k, s_v)
```

```python
import functools
import math

import jax
import jax.numpy as jnp
from jax import lax
from jax.experimental import pallas as pl
from jax.experimental.pallas import tpu as pltpu

F32 = jnp.float32
BF16 = jnp.bfloat16

D_MODEL = 1024
ML_HEADS = 4
ML_DK = 128
ML_DV = 128
ML_WIDTH = ML_HEADS * ML_DV
GATE_SOFTCAP = 15.0
SWA_HEADS = 8
SWA_KV_HEADS = 2
SWA_GROUP = SWA_HEADS // SWA_KV_HEADS
SWA_HD = 64
SWA_WIDTH = SWA_HEADS * SWA_HD
WINDOW = 128
REL_BUCKETS = 32
REL_MAX_DIST = 128
D_FF = 2816
DEPTH = 1
ALPHA = (2.0 * DEPTH) ** 0.25
LN_EPS = 1e-5
NORM_EPS = 1e-6
NEG_INF = -1e30
IN_SIZES = (512, 512, 512, 512, 4, 4, 512, 128, 128)
ML_SCALE = ML_DK ** -0.5
LOG_ML_SCALE = math.log(ML_SCALE)
SWA_SCALE = SWA_HD ** -0.5

BLK = 128

A_Q, A_K, A_V, A_SQ = 0, 512, 1024, 1536
A_COLS = 2048
B_MO, B_SK, B_SV, B_G = 0, 512, 640, 768
B_COLS = 896

N_K = 0
N_B = 512
N_COLS = N_B + B_COLS
T_Q, T_V, T_SQ, T_SV, T_G = 0, 512, 1024, 1536, 1664
T_GROWS = 16
T_COLS = T_G + BLK
W_COLS = N_COLS + T_COLS
STATE_ROWS = ML_DV + 16

TM_PROMPT = 1024
TM_DENSE = 1024
DENSE_SUB = 512
FF_SPLITS = (0, 1536, D_FF)
SAMPLE_GB = 16
SLAB = 16
VMEM_LIMIT = 56 * 1024 * 1024


def _softcap(a):
    return GATE_SOFTCAP * jnp.tanh(a / GATE_SOFTCAP)


def _log_sigmoid(x):
    return jnp.minimum(x, 0.0) - jnp.log1p(jnp.exp(-jnp.abs(x)))


def _layer_norm(z, g, b):
    mu = jnp.mean(z, axis=-1, keepdims=True)
    zc = z - mu
    var = jnp.mean(zc * zc, axis=-1, keepdims=True)
    return zc * lax.rsqrt(var + LN_EPS) * g + b


def _dot(a, b):
    return jnp.dot(a, b, preferred_element_type=F32)


def _dot_nt(a, b):
    return lax.dot_general(a, b, (((1,), (1,)), ((), ())), preferred_element_type=F32)


def _dot_tn(a, b):
    return lax.dot_general(a, b, (((0,), (0,)), ((), ())), preferred_element_type=F32)


def _split3(x):
    hi = x.astype(BF16)
    r1 = x - hi.astype(F32)
    mid = r1.astype(BF16)
    lo = (r1 - mid.astype(F32)).astype(BF16)
    return hi, mid, lo


def _t5_bucket(d):
    n = jnp.maximum(d, 0)
    max_exact = REL_BUCKETS // 2
    nlog = REL_BUCKETS - max_exact
    large = jnp.full(n.shape, max_exact, jnp.int32)
    for k in range(1, nlog):
        thr = math.ceil(max_exact * (REL_MAX_DIST / max_exact) ** (k / nlog))
        large = large + jnp.where(n >= thr, 1, 0)
    return jnp.where(n < max_exact, n, large)


def _prompt_mixer_kernel(x_ref, w_ref, gbc_ref, mlg_ref, sinks_ref, rb_ref,
                         mix_ref, c_out, n_out, m_out, ko_ref, vo_ref,
                         wt, kn, pf, vlast, kband, q_t, v_t, sq_t, sv_t, g_t, ct, msc, tbl):
    b = pl.program_id(0)
    t = pl.program_id(1)
    nt = pl.num_programs(1)
    tm = x_ref.shape[1]
    nblk = tm // BLK
    pairs = SWA_HEADS // 2

    @pl.when((b == 0) & (t == 0))
    def _transpose_feature_major_weights():
        for i in range(T_COLS // BLK):
            wt[i * BLK:(i + 1) * BLK, :] = w_ref[:, N_COLS + i * BLK:N_COLS + (i + 1) * BLK].T

    @pl.when((b == 0) & (t == 0))
    def _build_bias_tables():
        r = lax.broadcasted_iota(jnp.int32, (BLK, 2 * BLK), 0)
        ln = lax.broadcasted_iota(jnp.int32, (BLK, 2 * BLK), 1)
        second = ln >= BLK
        qi = jnp.where(second, ln - BLK, ln)
        prev = r > qi
        d = jnp.where(prev, WINDOW + qi - r, qi - r)
        bucket = _t5_bucket(d)
        for p in range(pairs):
            acc = jnp.zeros((BLK, 2 * BLK), F32)
            for k in range(REL_BUCKETS):
                acc = jnp.where(bucket == k, jnp.where(second, rb_ref[k, p + pairs], rb_ref[k, p]), acc)
            tbl[0, p] = jnp.where(prev, NEG_INF, acc)
            tbl[1, p] = acc

    @pl.when(t == 0)
    def _reset_state():
        ct[...] = jnp.zeros_like(ct)
        msc[...] = jnp.zeros_like(msc)
        kband[0:BLK, :] = jnp.zeros((BLK, kband.shape[1]), kband.dtype)
        sv_t[0] = jnp.zeros(sv_t.shape[1:], sv_t.dtype)

    xb = x_ref[0].astype(BF16)
    kn[...] = _dot(xb, w_ref[:, N_K:N_K + ML_HEADS * ML_DK]).astype(BF16)
    pf[...] = _dot(xb, w_ref[:, N_B:N_B + B_SV])
    kband[BLK:BLK + tm, :] = pf[:, B_SK:B_SK + BLK].astype(BF16)

    res_t = _dot_nt(wt[0:T_G + T_GROWS, :], xb)
    vlast[...] = res_t[T_SV:T_SV + BLK, tm - BLK:tm]

    def put(dst, r0, nrows, off, scale=None):
        res = res_t[r0:r0 + nrows, :]
        if scale is not None:
            res = res * scale
        for jj in range(nblk):
            dst[jj + off] = res[:, jj * BLK:(jj + 1) * BLK].astype(dst.dtype)

    put(q_t, T_Q, ML_HEADS * ML_DK, 0)
    put(v_t, T_V, ML_WIDTH, 0)
    put(sq_t, T_SQ, SWA_WIDTH, 0, SWA_SCALE)
    put(sv_t, T_SV, SWA_KV_HEADS * SWA_HD, 1)
    put(g_t, T_G, T_GROWS, 0)

    row = lax.broadcasted_iota(jnp.int32, (BLK, BLK), 0)
    col = lax.broadcasted_iota(jnp.int32, (BLK, BLK), 1)
    causal_t = row <= col
    triu = causal_t.astype(BF16)
    causal2 = jnp.concatenate([causal_t, causal_t], axis=1)
    r16 = lax.broadcasted_iota(jnp.int32, (16, 2 * BLK), 0)
    ones_rows = (r16 == 0).astype(BF16)
    z128 = jnp.zeros((BLK, BLK), BF16)
    ml_pairs = ML_HEADS // 2
    zero_rows = jnp.zeros((BLK - T_GROWS, BLK), F32)
    gbc = gbc_ref[...]
    mlg = mlg_ref[...]
    lane2 = lax.broadcasted_iota(jnp.int32, (1, 2 * BLK), 1)
    snk_rows = [jnp.where(lane2 >= BLK, sinks_ref[p + pairs], sinks_ref[p]) for p in range(pairs)]
    zq = jnp.zeros((SWA_HD, BLK), BF16)
    prev2 = jnp.concatenate([row > col, row > col], axis=1)

    def pair_row(x0, x1):
        return jnp.concatenate([jnp.broadcast_to(x0, (1, BLK)), jnp.broadcast_to(x1, (1, BLK))], axis=1)

    def block_diag(x0, x1):
        return jnp.concatenate([jnp.concatenate([x0, z128], axis=1),
                                jnp.concatenate([z128, x1], axis=1)], axis=0)

    def block(j, carry):
        r0 = pl.multiple_of(j * BLK, BLK)
        rows = pl.ds(r0, BLK)
        hsl = [slice(h * ML_DK, (h + 1) * ML_DK) for h in range(ML_HEADS)]
        psl = [slice(p * 2 * ML_DK, (p + 1) * 2 * ML_DK) for p in range(ml_pairs)]

        pre_t = _softcap(g_t[j] + gbc)
        hi, mid, lo = _split3(_log_sigmoid(pre_t))
        b_t = _dot(hi, triu) + _dot(mid, triu) + _dot(lo, triu)
        u_t = pltpu.roll(pre_t, ML_HEADS, 0) - b_t
        u_c = jnp.concatenate([u_t, zero_rows], axis=0).T

        k2 = [kn[rows, psl[p]] for p in range(ml_pairs)]
        bdq = [block_diag(q_t[j, hsl[2 * p], :], q_t[j, hsl[2 * p + 1], :]) for p in range(ml_pairs)]
        vte = [jnp.concatenate([jnp.concatenate([v_t[j, hsl[2 * p], :], v_t[j, hsl[2 * p + 1], :]], axis=1),
                                ones_rows], axis=0) for p in range(ml_pairs)]
        kq = [_dot(k2[p], bdq[p]) for p in range(ml_pairs)]

        sel = jnp.where((t == 0) & (j == 0), 0, 1)
        kb = kband[pl.ds(r0, 2 * BLK), :]
        vtb = jnp.concatenate([sv_t[j], sv_t[j + 1]], axis=1)
        sc2 = []
        for p in range(pairs):
            qa = sq_t[j, p * SWA_HD:(p + 1) * SWA_HD, :]
            qb = sq_t[j, (p + pairs) * SWA_HD:(p + pairs + 1) * SWA_HD, :]
            bd = jnp.concatenate([jnp.concatenate([qa, zq], axis=1),
                                  jnp.concatenate([zq, qb], axis=1)], axis=0)
            sc2.append(_dot(kb, bd))

        bds, scl, m_row, m_new, a_p, vwte = [], [], [], [], [], []
        for p in range(ml_pairs):
            h0, h1 = 2 * p, 2 * p + 1
            b_row = pair_row(b_t[ML_HEADS + h0:ML_HEADS + h0 + 1, :], b_t[ML_HEADS + h1:ML_HEADS + h1 + 1, :])
            i_row = pair_row(pre_t[h0:h0 + 1, :], pre_t[h1:h1 + 1, :])
            u_col = jnp.concatenate(
                [jnp.broadcast_to(u_c[:, ML_HEADS + h0:ML_HEADS + h0 + 1], (BLK, BLK)),
                 jnp.broadcast_to(u_c[:, ML_HEADS + h1:ML_HEADS + h1 + 1], (BLK, BLK))], axis=1)
            m_prev = msc[p:p + 1, :]
            log_d = jnp.where(causal2, b_row + u_col, -jnp.inf)
            inter = b_row + m_prev
            mr = jnp.maximum(inter, jnp.max(log_d, axis=0, keepdims=True))
            m_row.append(mr)
            scl.append(jnp.exp(inter - mr))
            s2 = (kq[p] * jnp.exp(log_d - (mr - LOG_ML_SCALE))).astype(BF16)
            bds.append(block_diag(s2[:, :BLK], s2[:, BLK:]))
            b_last = pair_row(b_row[:, BLK - 1:BLK], b_row[:, 2 * BLK - 1:2 * BLK])
            w_row = b_last - b_row + i_row
            w_max = pair_row(jnp.max(w_row[:, :BLK], axis=1, keepdims=True),
                             jnp.max(w_row[:, BLK:], axis=1, keepdims=True))
            mn = jnp.maximum(b_last + m_prev, w_max)
            m_new.append(mn)
            a_p.append(jnp.exp(b_last + m_prev - mn))
            wexp = jnp.exp(w_row - mn) * ML_SCALE
            vwte.append(vte[p] * wexp.astype(BF16))

        pw, den_p = [], []
        for p in range(pairs):
            sc = jnp.where(prev2, sc2[p][:BLK], sc2[p][BLK:]) + tbl[sel, p]
            snk = snk_rows[p]
            mx = jnp.maximum(jnp.max(sc, axis=0, keepdims=True), snk)
            pe = jnp.exp(sc - mx)
            den_p.append(jnp.sum(pe, axis=0, keepdims=True) + jnp.exp(snk - mx))
            pw.append(jnp.concatenate([jnp.where(prev2, pe, 0.0), jnp.where(prev2, 0.0, pe)],
                                      axis=0).astype(BF16))

        c_old = [ct[p] for p in range(ml_pairs)]
        cq = [_dot(c_old[p].astype(BF16), bdq[p]) for p in range(ml_pairs)]
        vs = [_dot(vte[p], bds[p]) for p in range(ml_pairs)]
        upd = [_dot(vwte[p], block_diag(k2[p][:, :BLK], k2[p][:, BLK:])) for p in range(ml_pairs)]
        o2 = [_dot(vtb, pw[p]) for p in range(pairs)]

        for p in range(ml_pairs):
            nd = scl[p] * cq[p] + vs[p]
            num = nd[:ML_DV]
            den = nd[ML_DV:ML_DV + 1]
            hh = num / jnp.maximum(jnp.abs(den), jnp.exp(-m_row[p]))
            hn = hh * lax.rsqrt(jnp.mean(hh * hh, axis=0, keepdims=True) + NORM_EPS)
            for half in range(2):
                h = 2 * p + half
                og = jax.nn.sigmoid(pf[rows, B_MO + h * ML_DV:B_MO + (h + 1) * ML_DV])
                hn_h = hn[:, half * BLK:(half + 1) * BLK].T
                mix_ref[0, rows, hsl[h]] = (hn_h * mlg[:, hsl[h]] * og).astype(mix_ref.dtype)
            ct[p] = a_p[p] * c_old[p] + upd[p]
            msc[p:p + 1, :] = m_new[p]

        o_parts = [None] * SWA_HEADS
        for p in range(pairs):
            on = o2[p] / den_p[p]
            o_parts[p] = on[0:SWA_HD, 0:BLK]
            o_parts[p + pairs] = on[SWA_HD:2 * SWA_HD, BLK:2 * BLK]
        o_t = jnp.concatenate(o_parts, axis=0)
        mix_ref[0, rows, ML_WIDTH:ML_WIDTH + SWA_WIDTH] = o_t.T.astype(mix_ref.dtype)
        return carry

    lax.fori_loop(0, nblk, block, 0, unroll=4)

    kband[0:BLK, :] = kband[tm:tm + BLK, :]
    sv_t[0] = sv_t[nblk]

    @pl.when(t == nt - 1)
    def _write_state():
        for h in range(ML_HEADS):
            p, lanes = h // 2, slice((h % 2) * ML_DK, (h % 2 + 1) * ML_DK)
            c_out[0, h] = ct[p, 0:ML_DV, lanes].T
            n_out[0, h:h + 1, :] = ct[p, ML_DV:ML_DV + 1, lanes]
        m_out[0] = msc[...]
        ko_ref[0] = pf[tm - BLK:tm, B_SK:B_SK + BLK]
        vo_ref[0] = vlast[...].T


def _prompt_mixer(x, wall, gbc, mlg, sinks, rb):
    bsz, seq, _ = x.shape
    tm = TM_PROMPT
    nt = seq // tm
    nblk = tm // BLK
    const2 = lambda b, t: (0, 0)
    smem = pl.BlockSpec(memory_space=pltpu.SMEM)
    return pl.pallas_call(
        _prompt_mixer_kernel,
        grid=(bsz, nt),
        in_specs=[
            pl.BlockSpec((1, tm, D_MODEL), lambda b, t: (b, t, 0)),
            pl.BlockSpec((D_MODEL, W_COLS), const2, pipeline_mode=pl.Buffered(1)),
            pl.BlockSpec((T_GROWS, BLK), const2),
            pl.BlockSpec((1, ML_WIDTH), const2),
            smem, smem,
        ],
        out_specs=[
            pl.BlockSpec((1, tm, D_MODEL), lambda b, t: (b, t, 0)),
            pl.BlockSpec((1, ML_HEADS, ML_DK, ML_DV), lambda b, t: (b, 0, 0, 0)),
            pl.BlockSpec((1, ML_HEADS, ML_DK), lambda b, t: (b, 0, 0)),
            pl.BlockSpec((1, 8, 2 * BLK), lambda b, t: (b, 0, 0)),
            pl.BlockSpec((1, BLK, BLK), lambda b, t: (b, 0, 0)),
            pl.BlockSpec((1, BLK, BLK), lambda b, t: (b, 0, 0)),
        ],
        out_shape=[
            jax.ShapeDtypeStruct((bsz, seq, D_MODEL), BF16),
            jax.ShapeDtypeStruct((bsz, ML_HEADS, ML_DK, ML_DV), F32),
            jax.ShapeDtypeStruct((bsz, ML_HEADS, ML_DK), F32),
            jax.ShapeDtypeStruct((bsz, 8, 2 * BLK), F32),
            jax.ShapeDtypeStruct((bsz, BLK, BLK), F32),
            jax.ShapeDtypeStruct((bsz, BLK, BLK), F32),
        ],
        scratch_shapes=[
            pltpu.VMEM((T_COLS, D_MODEL), BF16),
            pltpu.VMEM((tm, ML_HEADS * ML_DK), BF16),
            pltpu.VMEM((tm, B_SV), F32),
            pltpu.VMEM((BLK, BLK), F32),
            pltpu.VMEM((tm + BLK, BLK), BF16),
            pltpu.VMEM((nblk, ML_HEADS * ML_DK, BLK), BF16),
            pltpu.VMEM((nblk, ML_WIDTH, BLK), BF16),
            pltpu.VMEM((nblk, SWA_WIDTH, BLK), BF16),
            pltpu.VMEM((nblk + 1, SWA_KV_HEADS * SWA_HD, BLK), BF16),
            pltpu.VMEM((nblk, T_GROWS, BLK), F32),
            pltpu.VMEM((ML_HEADS // 2, STATE_ROWS, 2 * ML_DK), F32),
            pltpu.VMEM((8, 2 * BLK), F32),
            pltpu.VMEM((2, SWA_HEADS // 2, BLK, 2 * BLK), F32),
        ],
        compiler_params=pltpu.CompilerParams(
            dimension_semantics=("arbitrary", "arbitrary"),
            vmem_limit_bytes=VMEM_LIMIT),
        name="prompt_mixer",
    )(x, wall, gbc, mlg, sinks, rb)


def _roll_rows(x, shift):
    return pltpu.roll(x, shift % x.shape[0], 0)


def _sample_mixer_kernel(x_ref, w_ref, gb_ref, mlg_ref, sinks_ref, rb_ref, mrep_ref,
                         c_ref, n_ref, kc_ref, vc_ref,
                         mix_ref, c_out, n_out, mo_ref, ko_ref, vo_ref,
                         pa, pf, tblc, tbln):
    step = pl.program_id(0)
    tdec = 4
    rows_step = mix_ref.shape[0]
    nslab = rows_step // SLAB
    nseq = SLAB // tdec
    qrows = SWA_HEADS * SLAB

    @pl.when(step == 0)
    def _project_all_rows():
        xb = x_ref[...].astype(BF16)
        tq, tv = N_COLS + T_Q, N_COLS + T_V
        pa[:, A_Q:A_K] = _dot(xb, w_ref[:, tq:tq + ML_HEADS * ML_DK])
        pa[:, A_K:A_V] = _dot(xb, w_ref[:, N_K:N_K + ML_HEADS * ML_DK])
        pa[:, A_V:A_COLS] = _dot(xb, w_ref[:, tv:tv + ML_WIDTH + SWA_WIDTH])
        pf[...] = _dot(xb, w_ref[:, N_B:N_B + B_COLS])

    @pl.when(step == 0)
    def _build_bias_tables():
        r = lax.broadcasted_iota(jnp.int32, (qrows, BLK), 0)
        c = lax.broadcasted_iota(jnp.int32, (qrows, BLK), 1)
        hq = r // SLAB
        bl = (r % SLAB) // tdec
        tq = r % tdec
        d_cache = WINDOW + tq - c
        d_new = tq - (c % tdec)
        key_cache = _t5_bucket(d_cache) * SWA_HEADS + hq
        key_new = _t5_bucket(d_new) * SWA_HEADS + hq
        acc_c = jnp.zeros((qrows, BLK), F32)
        acc_n = jnp.zeros((qrows, BLK), F32)
        for k in range(REL_BUCKETS):
            for h in range(SWA_HEADS):
                val = rb_ref[k, h]
                acc_c = jnp.where(key_cache == k * SWA_HEADS + h, val, acc_c)
                acc_n = jnp.where(key_new == k * SWA_HEADS + h, val, acc_n)
        tblc[...] = jnp.where(c > tq, acc_c, NEG_INF)
        tbln[...] = jnp.where((c < SLAB) & ((c // tdec) == bl) & (d_new >= 0), acc_n, NEG_INF)

    gbias = gb_ref[...]
    mlg = mlg_ref[...]
    r16 = lax.broadcasted_iota(jnp.int32, (SLAB, BLK), 0)
    rr = r16 % tdec
    bl16 = r16 // tdec
    rq = lax.broadcasted_iota(jnp.int32, (qrows, BLK), 0)
    blq = (rq % SLAB) // tdec
    hq_col = rq[:, 0:1] // SLAB
    snk = jnp.zeros((qrows, 1), F32)
    for h in range(SWA_HEADS):
        snk = jnp.where(hq_col == h, sinks_ref[h], snk)
    first_group = rq[:, 0:SWA_HD] < SWA_GROUP * SLAB
    zq = jnp.zeros((SLAB, SWA_HD), F32)
    zeros_pad = jnp.zeros((BLK - SLAB, BLK), BF16)

    def seg_last(x):
        return jnp.where(rr == 3, x,
                         jnp.where(rr == 2, _roll_rows(x, -1),
                                   jnp.where(rr == 1, _roll_rows(x, -2), _roll_rows(x, -3))))

    def seg_max(x):
        m1 = jnp.maximum(x, jnp.where(rr % 2 == 0, _roll_rows(x, -1), _roll_rows(x, 1)))
        return jnp.maximum(m1, jnp.where(rr < 2, _roll_rows(m1, -2), _roll_rows(m1, 2)))

    def slab(si):
        rows = pl.ds(si * SLAB, SLAB)
        prow = pl.ds(pl.multiple_of(step * rows_step + si * SLAB, SLAB), SLAB)
        seq0 = si * nseq

        pre = _softcap(pf[prow, B_G:B_G + BLK] + gbias)
        logf = _log_sigmoid(pre)
        y = logf + jnp.where(rr >= 1, _roll_rows(logf, 1), 0.0)
        bcum = y + jnp.where(rr >= 2, _roll_rows(y, 2), 0.0)
        ig = pltpu.roll(pre, ML_HEADS, 1)
        m_prev = mrep_ref[rows, :]
        b_minus_i = bcum - ig
        log_d = [jnp.where(rr >= dl, bcum - _roll_rows(b_minus_i, dl), -jnp.inf) for dl in range(tdec)]
        rowmax = functools.reduce(jnp.maximum, log_d)
        inter = bcum + m_prev
        m_row = jnp.maximum(inter, rowmax)
        scl = jnp.exp(inter - m_row)
        dm = [jnp.exp(ld - m_row) for ld in log_d]
        enm = jnp.exp(-m_row)
        b_last = seg_last(bcum)
        w = b_last - bcum + ig
        m_new = jnp.maximum(b_last + m_prev, seg_max(w))
        a = jnp.exp(b_last + m_prev - m_new)
        kws = jnp.exp(w - m_new) * ML_SCALE
        mo_ref[rows, :] = m_new

        q = [pa[prow, A_Q + h * ML_DK:A_Q + (h + 1) * ML_DK] for h in range(ML_HEADS)]
        k = [pa[prow, A_K + h * ML_DK:A_K + (h + 1) * ML_DK] for h in range(ML_HEADS)]
        v = [pa[prow, A_V + h * ML_DV:A_V + (h + 1) * ML_DV] for h in range(ML_HEADS)]
        c_b = [[c_ref[seq0 + bl, h] for bl in range(nseq)] for h in range(ML_HEADS)]
        qc, upd, kw = [], [], []
        for h in range(ML_HEADS):
            lane = ML_HEADS + h
            qb = q[h].astype(BF16)
            res = [_dot(qb, jnp.concatenate([c_b[h][2 * i].astype(BF16), c_b[h][2 * i + 1].astype(BF16)], axis=1))
                   for i in range(nseq // 2)]
            parts = [res[bl // 2][:, (bl % 2) * ML_DV:(bl % 2 + 1) * ML_DV] for bl in range(nseq)]
            qc_h = parts[nseq - 1]
            for bl in range(nseq - 2, -1, -1):
                qc_h = jnp.where(bl16 == bl, parts[bl], qc_h)
            qc.append(qc_h)
            kw_h = k[h] * kws[:, lane:lane + 1]
            kw.append(kw_h)
            vm = jnp.concatenate([jnp.where(bl16 == bl, v[h], 0.0) for bl in range(nseq)], axis=1)
            upd.append(_dot_tn(kw_h.astype(BF16), vm.astype(BF16)))

        knew = pf[prow, B_SK:B_SK + BLK]
        vnew = pf[prow, B_SV:B_SV + BLK]
        qparts = []
        for hq in range(SWA_HEADS):
            qh = pa[prow, A_SQ + hq * SWA_HD:A_SQ + (hq + 1) * SWA_HD]
            qparts.append(jnp.concatenate([qh, zq] if hq < SWA_GROUP else [zq, qh], axis=1))
        qs = jnp.concatenate(qparts, axis=0).astype(BF16)
        s_c = _dot_nt(qs, kc_ref[seq0 + nseq - 1].astype(BF16))
        for bl in range(nseq - 2, -1, -1):
            s_c = jnp.where(blq == bl, _dot_nt(qs, kc_ref[seq0 + bl].astype(BF16)), s_c)
        kn = jnp.concatenate([knew.astype(BF16), zeros_pad], axis=0)
        s_n = _dot_nt(qs, kn)

        yield
        for h in range(ML_HEADS):
            hs = slice(h * ML_DK, (h + 1) * ML_DK)
            lane = ML_HEADS + h
            bm = jnp.zeros((SLAB, ML_DV), F32)
            sum_s = jnp.zeros((SLAB, 1), F32)
            for dl in range(tdec):
                s_dl = jnp.sum(q[h] * _roll_rows(k[h], dl), axis=-1, keepdims=True)
                ws = s_dl * ML_SCALE * dm[dl][:, lane:lane + 1]
                bm = bm + ws * _roll_rows(v[h], dl)
                sum_s = sum_s + ws
            nsel = jnp.zeros((SLAB, ML_DK), F32)
            for bl in range(nseq):
                msk = bl16 == bl
                n_b = n_ref[seq0 + bl][h:h + 1, :]
                nsel = jnp.where(msk, n_b, nsel)
                a_b = a[bl * tdec:bl * tdec + 1, lane:lane + 1]
                c_out[seq0 + bl, h] = a_b * c_b[h][bl] + upd[h][:, bl * ML_DV:(bl + 1) * ML_DV]
                n_out[seq0 + bl, h:h + 1, :] = a_b * n_b + jnp.sum(jnp.where(msk, kw[h], 0.0), axis=0,
                                                                   keepdims=True)
            sclh = scl[:, lane:lane + 1]
            num = sclh * qc[h] + bm
            den = sclh * jnp.sum(q[h] * nsel, axis=-1, keepdims=True) + sum_s
            hh = num / jnp.maximum(jnp.abs(den), enm[:, lane:lane + 1])
            hn = hh * lax.rsqrt(jnp.mean(hh * hh, axis=-1, keepdims=True) + NORM_EPS)
            og = jax.nn.sigmoid(pf[prow, B_MO + h * ML_DV:B_MO + (h + 1) * ML_DV])
            mix_ref[rows, hs] = (hn * mlg[:, hs] * og).astype(mix_ref.dtype)

        s_c = s_c * SWA_SCALE + tblc[...]
        s_n = s_n * SWA_SCALE + tbln[...]
        mx = jnp.maximum(jnp.maximum(jnp.max(s_c, axis=-1, keepdims=True),
                                     jnp.max(s_n, axis=-1, keepdims=True)), snk)
        p_c = jnp.exp(s_c - mx)
        p_n = jnp.exp(s_n - mx)
        den = (jnp.sum(p_c, axis=-1, keepdims=True) + jnp.sum(p_n, axis=-1, keepdims=True)
               + jnp.exp(snk - mx))

        yield
        vn = jnp.concatenate([vnew.astype(BF16), zeros_pad], axis=0)
        o = _dot(p_n.astype(BF16), vn)
        for bl in range(nseq):
            o = o + _dot(jnp.where(blq == bl, p_c, 0.0).astype(BF16), vc_ref[seq0 + bl].astype(BF16))
        o = o / den
        osel = jnp.where(first_group, o[:, 0:SWA_HD], o[:, SWA_HD:2 * SWA_HD])
        og = jnp.concatenate([osel[hq * SLAB:(hq + 1) * SLAB, :] for hq in range(SWA_HEADS)], axis=1)
        mix_ref[rows, ML_WIDTH:ML_WIDTH + SWA_WIDTH] = og.astype(mix_ref.dtype)

        for bl in range(nseq):
            ko_ref[seq0 + bl, 0:WINDOW - tdec, :] = kc_ref[seq0 + bl, tdec:WINDOW, :]
            vo_ref[seq0 + bl, 0:WINDOW - tdec, :] = vc_ref[seq0 + bl, tdec:WINDOW, :]
            ko_ref[seq0 + bl, WINDOW - tdec:WINDOW, :] = knew[bl * tdec:(bl + 1) * tdec, :]
            vo_ref[seq0 + bl, WINDOW - tdec:WINDOW, :] = vnew[bl * tdec:(bl + 1) * tdec, :]
        yield

    slabs = [slab(si) for si in range(nslab)]
    for _ in range(3):
        for s in slabs:
            next(s)


def _sample_mixer(x2d, wall, gb, mlg, sinks, rb, mrep, c0, n0, kc, vc):
    rows = x2d.shape[0]
    tdec = 4
    nseq = rows // tdec
    gbs = SAMPLE_GB
    steps = nseq // gbs
    rb_rows = gbs * tdec
    const2 = lambda i: (0, 0)
    smem = pl.BlockSpec(memory_space=pltpu.SMEM)
    return pl.pallas_call(
        _sample_mixer_kernel,
        grid=(steps,),
        in_specs=[
            pl.BlockSpec((rows, D_MODEL), const2, pipeline_mode=pl.Buffered(1)),
            pl.BlockSpec((D_MODEL, W_COLS), const2, pipeline_mode=pl.Buffered(1)),
            pl.BlockSpec((1, BLK), const2),
            pl.BlockSpec((1, ML_WIDTH), const2),
            smem, smem,
            pl.BlockSpec((rb_rows, BLK), lambda i: (i, 0)),
            pl.BlockSpec((gbs, ML_HEADS, ML_DK, ML_DV), lambda i: (i, 0, 0, 0)),
            pl.BlockSpec((gbs, ML_HEADS, ML_DK), lambda i: (i, 0, 0)),
            pl.BlockSpec((gbs, WINDOW, BLK), lambda i: (i, 0, 0)),
            pl.BlockSpec((gbs, WINDOW, BLK), lambda i: (i, 0, 0)),
        ],
        out_specs=[
            pl.BlockSpec((rb_rows, D_MODEL), lambda i: (i, 0)),
            pl.BlockSpec((gbs, ML_HEADS, ML_DK, ML_DV), lambda i: (i, 0, 0, 0)),
            pl.BlockSpec((gbs, ML_HEADS, ML_DK), lambda i: (i, 0, 0)),
            pl.BlockSpec((rb_rows, BLK), lambda i: (i, 0)),
            pl.BlockSpec((gbs, WINDOW, BLK), lambda i: (i, 0, 0)),
            pl.BlockSpec((gbs, WINDOW, BLK), lambda i: (i, 0, 0)),
        ],
        out_shape=[
            jax.ShapeDtypeStruct((rows, D_MODEL), BF16),
            jax.ShapeDtypeStruct((nseq, ML_HEADS, ML_DK, ML_DV), F32),
            jax.ShapeDtypeStruct((nseq, ML_HEADS, ML_DK), F32),
            jax.ShapeDtypeStruct((rows, BLK), F32),
            jax.ShapeDtypeStruct((nseq, WINDOW, BLK), F32),
            jax.ShapeDtypeStruct((nseq, WINDOW, BLK), F32),
        ],
        scratch_shapes=[
            pltpu.VMEM((rows, A_COLS), F32),
            pltpu.VMEM((rows, B_COLS), F32),
            pltpu.VMEM((SWA_HEADS * SLAB, BLK), F32),
            pltpu.VMEM((SWA_HEADS * SLAB, BLK), F32),
        ],
        compiler_params=pltpu.CompilerParams(
            dimension_semantics=("arbitrary",),
            vmem_limit_bytes=VMEM_LIMIT),
        name="sample_mixer",
    )(x2d, wall, gb, mlg, sinks, rb, mrep, c0, n0, kc, vc)


def _dense_kernel(x_ref, mix_ref, wo_ref, g1_ref, b1_ref, wfi_ref, wfo_ref, g2_ref, b2_ref, y_ref):
    nsub = x_ref.shape[0] // DENSE_SUB
    subs = [slice(i * DENSE_SUB, (i + 1) * DENSE_SUB) for i in range(nsub)]
    chunks = list(zip(FF_SPLITS[:-1], FF_SPLITS[1:]))
    g1, b1, g2, b2 = g1_ref[...], b1_ref[...], g2_ref[...], b2_ref[...]

    proj = [_dot(mix_ref[s, :], wo_ref[...]) for s in subs]
    h1 = [_layer_norm(ALPHA * x_ref[s, :] + proj[i], g1, b1) for i, s in enumerate(subs)]
    h1b = [h.astype(BF16) for h in h1]
    acc = [None] * nsub
    for c0, c1 in chunks:
        gate = [_dot(h1b[i], wfi_ref[:, c0:c1]) for i in range(nsub)]
        up = [_dot(h1b[i], wfi_ref[:, D_FF + c0:D_FF + c1]) for i in range(nsub)]
        for i in range(nsub):
            act = (gate[i] * jax.nn.sigmoid(gate[i]) * up[i]).astype(BF16)
            part = _dot(act, wfo_ref[c0:c1, :])
            acc[i] = part if acc[i] is None else acc[i] + part
    for i, s in enumerate(subs):
        y_ref[s, :] = _layer_norm(ALPHA * h1[i] + acc[i], g2, b2)


def _dense(x2d, mix2d, wo, g1, b1, wfi, wfo, g2, b2):
    rows = x2d.shape[0]
    tm = min(TM_DENSE, rows)
    const2 = lambda i: (0, 0)

    def wspec(shape):
        return pl.BlockSpec(shape, const2, pipeline_mode=pl.Buffered(1))

    return pl.pallas_call(
        _dense_kernel,
        grid=(rows // tm,),
        in_specs=[
            pl.BlockSpec((tm, D_MODEL), lambda i: (i, 0)),
            pl.BlockSpec((tm, D_MODEL), lambda i: (i, 0)),
            wspec((D_MODEL, D_MODEL)),
            wspec((1, D_MODEL)), wspec((1, D_MODEL)),
            wspec((D_MODEL, 2 * D_FF)),
            wspec((D_FF, D_MODEL)),
            wspec((1, D_MODEL)), wspec((1, D_MODEL)),
        ],
        out_specs=pl.BlockSpec((tm, D_MODEL), lambda i: (i, 0)),
        out_shape=jax.ShapeDtypeStruct((rows, D_MODEL), F32),
        compiler_params=pltpu.CompilerParams(
            dimension_semantics=("arbitrary",),
            vmem_limit_bytes=VMEM_LIMIT),
        name="dense",
    )(x2d, mix2d, wo, g1, b1, wfi, wfo, g2, b2)


def kernel(x_prompt, x_sample, state_mlstm_C, state_mlstm_n, state_mlstm_m, cache_swa_k, cache_swa_v,
           w_in, b_igate, b_fgate, ml_norm_g, swa_sinks, rel_bias, w_out, ln1_g, ln1_b,
           w_ffn_in, w_ffn_out, ln2_g, ln2_b):
    bp, seq, _ = x_prompt.shape
    bs, tdec, _ = x_sample.shape
    l = 0

    offs = [0]
    for n in IN_SIZES:
        offs.append(offs[-1] + n)
    w = w_in[l]
    mq, mk, mv, mo, mi, mf, sq, sk, sv = [w[:, offs[i]:offs[i + 1]] for i in range(len(IN_SIZES))]
    gpad = jnp.zeros((D_MODEL, BLK - 2 * ML_HEADS), w.dtype)
    wall = jnp.concatenate([mk, mo, sk, sv, mi, mf, gpad,
                            mq, mv, sq, sv, mi, mf, gpad],
                           axis=1).astype(BF16)
    gvec = jnp.concatenate([b_igate[l], b_fgate[l]]).astype(F32)
    gb = jnp.pad(gvec, (0, BLK - 2 * ML_HEADS))[None, :]
    gbc = jnp.broadcast_to(jnp.pad(gvec, (0, T_GROWS - 2 * ML_HEADS))[:, None], (T_GROWS, BLK))
    mlg = ml_norm_g[l][None, :].astype(F32)
    sinks = swa_sinks[l].astype(F32)
    rb = rel_bias.astype(F32)
    wo = w_out[l].astype(BF16)
    wfi = w_ffn_in[l].astype(BF16)
    wfo = w_ffn_out[l].astype(BF16)
    g1, b1 = ln1_g[l][None, :], ln1_b[l][None, :]
    g2, b2 = ln2_g[l][None, :], ln2_b[l][None, :]

    mix_p, c_p, n_p, m_p, k_p, v_p = _prompt_mixer(x_prompt, wall, gbc, mlg, sinks, rb)
    y_p = _dense(x_prompt.reshape(bp * seq, D_MODEL), mix_p.reshape(bp * seq, D_MODEL),
                 wo, g1, b1, wfi, wfo, g2, b2).reshape(bp, seq, D_MODEL)
    p_m = m_p[:, :ML_HEADS // 2, ::BLK].reshape(bp, ML_HEADS)[None]
    p_k = k_p.reshape(1, bp, WINDOW, SWA_KV_HEADS, SWA_HD)
    p_v = v_p.reshape(1, bp, WINDOW, SWA_KV_HEADS, SWA_HD)

    xs = x_sample.reshape(bs * tdec, D_MODEL)
    m0 = state_mlstm_m[l].astype(F32)
    mrep = jnp.pad(jnp.repeat(m0, tdec, axis=0), ((0, 0), (ML_HEADS, BLK - 2 * ML_HEADS)))
    wlen = cache_swa_k.shape[2]
    kc = cache_swa_k[l].reshape(bs, wlen, SWA_KV_HEADS * SWA_HD)
    vc = cache_swa_v[l].reshape(bs, wlen, SWA_KV_HEADS * SWA_HD)
    mix_s, c_s, n_s, mo_s, k_s, v_s = _sample_mixer(
        xs, wall, gb, mlg, sinks, rb, mrep,
        state_mlstm_C[l].astype(F32), state_mlstm_n[l].astype(F32), kc, vc)
    y_s = _dense(xs, mix_s, wo, g1, b1, wfi, wfo, g2, b2).reshape(bs, tdec, D_MODEL)
    s_m = mo_s.reshape(bs, tdec, BLK)[:, 0, ML_HEADS:2 * ML_HEADS][None]
    s_k = k_s.reshape(1, bs, wlen, SWA_KV_HEADS, SWA_HD)
    s_v = v_s.reshape(1, bs, wlen, SWA_KV_HEADS, SWA_HD)

    return (y_p, y_s, c_p[None], n_p[None], p_m, p_k, p_v,
            c_s[None], n_s[None], s_m, s_k, s_v)
```

```python
import functools
import math

import jax
import jax.numpy as jnp
from jax import lax
from jax.experimental import pallas as pl
from jax.experimental.pallas import tpu as pltpu

F32 = jnp.float32
BF16 = jnp.bfloat16

D_MODEL = 1024
ML_HEADS = 4
ML_DK = 128
ML_DV = 128
ML_WIDTH = ML_HEADS * ML_DV
GATE_SOFTCAP = 15.0
SWA_HEADS = 8
SWA_KV_HEADS = 2
SWA_GROUP = SWA_HEADS // SWA_KV_HEADS
SWA_HD = 64
SWA_WIDTH = SWA_HEADS * SWA_HD
WINDOW = 128
REL_BUCKETS = 32
REL_MAX_DIST = 128
D_FF = 2816
DEPTH = 1
ALPHA = (2.0 * DEPTH) ** 0.25
LN_EPS = 1e-5
NORM_EPS = 1e-6
NEG_INF = -1e30
IN_SIZES = (512, 512, 512, 512, 4, 4, 512, 128, 128)
ML_SCALE = ML_DK ** -0.5
LOG_ML_SCALE = math.log(ML_SCALE)
SWA_SCALE = SWA_HD ** -0.5

BLK = 128

A_Q, A_K, A_V, A_SQ = 0, 512, 1024, 1536
A_COLS = 2048
B_MO, B_SK, B_SV, B_G = 0, 512, 640, 768
B_COLS = 896

N_K = 0
N_B = 512
N_COLS = N_B + B_COLS
T_Q, T_V, T_SQ, T_SV, T_G = 0, 512, 1024, 1536, 1664
T_GROWS = 16
T_COLS = T_G + BLK
W_COLS = N_COLS + T_COLS
STATE_ROWS = ML_DV + 16

TM_PROMPT = 1024
TM_DENSE = 1024
DENSE_SUB = 512
FF_SPLITS = (0, 1536, D_FF)
SAMPLE_GB = 16
SLAB = 16
VMEM_LIMIT = 56 * 1024 * 1024


def _softcap(a):
    return GATE_SOFTCAP * jnp.tanh(a / GATE_SOFTCAP)


def _log_sigmoid(x):
    return jnp.minimum(x, 0.0) - jnp.log1p(jnp.exp(-jnp.abs(x)))


def _layer_norm(z, g, b):
    mu = jnp.mean(z, axis=-1, keepdims=True)
    zc = z - mu
    var = jnp.mean(zc * zc, axis=-1, keepdims=True)
    return zc * lax.rsqrt(var + LN_EPS) * g + b


def _dot(a, b):
    return jnp.dot(a, b, preferred_element_type=F32)


def _dot_nt(a, b):
    return lax.dot_general(a, b, (((1,), (1,)), ((), ())), preferred_element_type=F32)


def _dot_tn(a, b):
    return lax.dot_general(a, b, (((0,), (0,)), ((), ())), preferred_element_type=F32)


def _split3(x):
    hi = x.astype(BF16)
    r1 = x - hi.astype(F32)
    mid = r1.astype(BF16)
    lo = (r1 - mid.astype(F32)).astype(BF16)
    return hi, mid, lo


def _t5_bucket(d):
    n = jnp.maximum(d, 0)
    max_exact = REL_BUCKETS // 2
    nlog = REL_BUCKETS - max_exact
    large = jnp.full(n.shape, max_exact, jnp.int32)
    for k in range(1, nlog):
        thr = math.ceil(max_exact * (REL_MAX_DIST / max_exact) ** (k / nlog))
        large = large + jnp.where(n >= thr, 1, 0)
    return jnp.where(n < max_exact, n, large)


def _prompt_mixer_kernel(x_ref, w_ref, gbc_ref, mlg_ref, sinks_ref, rb_ref,
                         mix_ref, c_out, n_out, m_out, ko_ref, vo_ref,
                         wt, kn, pf, vlast, kband, q_t, v_t, sq_t, sv_t, g_t, ct, msc, tbl):
    b = pl.program_id(0)
    t = pl.program_id(1)
    nt = pl.num_programs(1)
    tm = x_ref.shape[1]
    nblk = tm // BLK
    pairs = SWA_HEADS // 2

    @pl.when((b == 0) & (t == 0))
    def _transpose_feature_major_weights():
        for i in range(T_COLS // BLK):
            wt[i * BLK:(i + 1) * BLK, :] = w_ref[:, N_COLS + i * BLK:N_COLS + (i + 1) * BLK].T

    @pl.when((b == 0) & (t == 0))
    def _build_bias_tables():
        r = lax.broadcasted_iota(jnp.int32, (BLK, 2 * BLK), 0)
        ln = lax.broadcasted_iota(jnp.int32, (BLK, 2 * BLK), 1)
        second = ln >= BLK
        qi = jnp.where(second, ln - BLK, ln)
        prev = r > qi
        d = jnp.where(prev, WINDOW + qi - r, qi - r)
        bucket = _t5_bucket(d)
        for p in range(pairs):
            acc = jnp.zeros((BLK, 2 * BLK), F32)
            for k in range(REL_BUCKETS):
                acc = jnp.where(bucket == k, jnp.where(second, rb_ref[k, p + pairs], rb_ref[k, p]), acc)
            tbl[0, p] = jnp.where(prev, NEG_INF, acc)
            tbl[1, p] = acc

    @pl.when(t == 0)
    def _reset_state():
        ct[...] = jnp.zeros_like(ct)
        msc[...] = jnp.zeros_like(msc)
        kband[0:BLK, :] = jnp.zeros((BLK, kband.shape[1]), kband.dtype)
        sv_t[0] = jnp.zeros(sv_t.shape[1:], sv_t.dtype)

    xb = x_ref[0].astype(BF16)
    kn[...] = _dot(xb, w_ref[:, N_K:N_K + ML_HEADS * ML_DK]).astype(BF16)
    pf[...] = _dot(xb, w_ref[:, N_B:N_B + B_SV])
    kband[BLK:BLK + tm, :] = pf[:, B_SK:B_SK + BLK].astype(BF16)

    res_t = _dot_nt(wt[0:T_G + T_GROWS, :], xb)
    vlast[...] = res_t[T_SV:T_SV + BLK, tm - BLK:tm]

    def put(dst, r0, nrows, off, scale=None):
        res = res_t[r0:r0 + nrows, :]
        if scale is not None:
            res = res * scale
        for jj in range(nblk):
            dst[jj + off] = res[:, jj * BLK:(jj + 1) * BLK].astype(dst.dtype)

    put(q_t, T_Q, ML_HEADS * ML_DK, 0)
    put(v_t, T_V, ML_WIDTH, 0)
    put(sq_t, T_SQ, SWA_WIDTH, 0, SWA_SCALE)
    put(sv_t, T_SV, SWA_KV_HEADS * SWA_HD, 1)
    put(g_t, T_G, T_GROWS, 0)

    row = lax.broadcasted_iota(jnp.int32, (BLK, BLK), 0)
    col = lax.broadcasted_iota(jnp.int32, (BLK, BLK), 1)
    causal_t = row <= col
    triu = causal_t.astype(BF16)
    causal2 = jnp.concatenate([causal_t, causal_t], axis=1)
    r16 = lax.broadcasted_iota(jnp.int32, (16, 2 * BLK), 0)
    ones_rows = (r16 == 0).astype(BF16)
    z128 = jnp.zeros((BLK, BLK), BF16)
    ml_pairs = ML_HEADS // 2
    zero_rows = jnp.zeros((BLK - T_GROWS, BLK), F32)
    gbc = gbc_ref[...]
    mlg = mlg_ref[...]
    lane2 = lax.broadcasted_iota(jnp.int32, (1, 2 * BLK), 1)
    snk_rows = [jnp.where(lane2 >= BLK, sinks_ref[p + pairs], sinks_ref[p]) for p in range(pairs)]
    zq = jnp.zeros((SWA_HD, BLK), BF16)
    prev2 = jnp.concatenate([row > col, row > col], axis=1)

    def pair_row(x0, x1):
        return jnp.concatenate([jnp.broadcast_to(x0, (1, BLK)), jnp.broadcast_to(x1, (1, BLK))], axis=1)

    def block_diag(x0, x1):
        return jnp.concatenate([jnp.concatenate([x0, z128], axis=1),
                                jnp.concatenate([z128, x1], axis=1)], axis=0)

    def block(j, carry):
        r0 = pl.multiple_of(j * BLK, BLK)
        rows = pl.ds(r0, BLK)
        hsl = [slice(h * ML_DK, (h + 1) * ML_DK) for h in range(ML_HEADS)]
        psl = [slice(p * 2 * ML_DK, (p + 1) * 2 * ML_DK) for p in range(ml_pairs)]

        pre_t = _softcap(g_t[j] + gbc)
        hi, mid, lo = _split3(_log_sigmoid(pre_t))
        b_t = _dot(hi, triu) + _dot(mid, triu) + _dot(lo, triu)
        u_t = pltpu.roll(pre_t, ML_HEADS, 0) - b_t
        u_c = jnp.concatenate([u_t, zero_rows], axis=0).T

        k2 = [kn[rows, psl[p]] for p in range(ml_pairs)]
        bdq = [block_diag(q_t[j, hsl[2 * p], :], q_t[j, hsl[2 * p + 1], :]) for p in range(ml_pairs)]
        vte = [jnp.concatenate([jnp.concatenate([v_t[j, hsl[2 * p], :], v_t[j, hsl[2 * p + 1], :]], axis=1),
                                ones_rows], axis=0) for p in range(ml_pairs)]
        kq = [_dot(k2[p], bdq[p]) for p in range(ml_pairs)]

        sel = jnp.where((t == 0) & (j == 0), 0, 1)
        kb = kband[pl.ds(r0, 2 * BLK), :]
        vtb = jnp.concatenate([sv_t[j], sv_t[j + 1]], axis=1)
        sc2 = []
        for p in range(pairs):
            qa = sq_t[j, p * SWA_HD:(p + 1) * SWA_HD, :]
            qb = sq_t[j, (p + pairs) * SWA_HD:(p + pairs + 1) * SWA_HD, :]
            bd = jnp.concatenate([jnp.concatenate([qa, zq], axis=1),
                                  jnp.concatenate([zq, qb], axis=1)], axis=0)
            sc2.append(_dot(kb, bd))

        bds, scl, m_row, m_new, a_p, vwte = [], [], [], [], [], []
        for p in range(ml_pairs):
            h0, h1 = 2 * p, 2 * p + 1
            b_row = pair_row(b_t[ML_HEADS + h0:ML_HEADS + h0 + 1, :], b_t[ML_HEADS + h1:ML_HEADS + h1 + 1, :])
            i_row = pair_row(pre_t[h0:h0 + 1, :], pre_t[h1:h1 + 1, :])
            u_col = jnp.concatenate(
                [jnp.broadcast_to(u_c[:, ML_HEADS + h0:ML_HEADS + h0 + 1], (BLK, BLK)),
                 jnp.broadcast_to(u_c[:, ML_HEADS + h1:ML_HEADS + h1 + 1], (BLK, BLK))], axis=1)
            m_prev = msc[p:p + 1, :]
            log_d = jnp.where(causal2, b_row + u_col, -jnp.inf)
            inter = b_row + m_prev
            mr = jnp.maximum(inter, jnp.max(log_d, axis=0, keepdims=True))
            m_row.append(mr)
            scl.append(jnp.exp(inter - mr))
            s2 = (kq[p] * jnp.exp(log_d - (mr - LOG_ML_SCALE))).astype(BF16)
            bds.append(block_diag(s2[:, :BLK], s2[:, BLK:]))
            b_last = pair_row(b_row[:, BLK - 1:BLK], b_row[:, 2 * BLK - 1:2 * BLK])
            w_row = b_last - b_row + i_row
            w_max = pair_row(jnp.max(w_row[:, :BLK], axis=1, keepdims=True),
                             jnp.max(w_row[:, BLK:], axis=1, keepdims=True))
            mn = jnp.maximum(b_last + m_prev, w_max)
            m_new.append(mn)
            a_p.append(jnp.exp(b_last + m_prev - mn))
            wexp = jnp.exp(w_row - mn) * ML_SCALE
            vwte.append(vte[p] * wexp.astype(BF16))

        pw, den_p = [], []
        for p in range(pairs):
            sc = jnp.where(prev2, sc2[p][:BLK], sc2[p][BLK:]) + tbl[sel, p]
            snk = snk_rows[p]
            mx = jnp.maximum(jnp.max(sc, axis=0, keepdims=True), snk)
            pe = jnp.exp(sc - mx)
            den_p.append(jnp.sum(pe, axis=0, keepdims=True) + jnp.exp(snk - mx))
            pw.append(jnp.concatenate([jnp.where(prev2, pe, 0.0), jnp.where(prev2, 0.0, pe)],
                                      axis=0).astype(BF16))

        c_old = [ct[p] for p in range(ml_pairs)]
        cq = [_dot(c_old[p].astype(BF16), bdq[p]) for p in range(ml_pairs)]
        vs = [_dot(vte[p], bds[p]) for p in range(ml_pairs)]
        upd = [_dot(vwte[p], block_diag(k2[p][:, :BLK], k2[p][:, BLK:])) for p in range(ml_pairs)]
        o2 = [_dot(vtb, pw[p]) for p in range(pairs)]

        for p in range(ml_pairs):
            nd = scl[p] * cq[p] + vs[p]
            num = nd[:ML_DV]
            den = nd[ML_DV:ML_DV + 1]
            hh = num / jnp.maximum(jnp.abs(den), jnp.exp(-m_row[p]))
            hn = hh * lax.rsqrt(jnp.mean(hh * hh, axis=0, keepdims=True) + NORM_EPS)
            for half in range(2):
                h = 2 * p + half
                og = jax.nn.sigmoid(pf[rows, B_MO + h * ML_DV:B_MO + (h + 1) * ML_DV])
                hn_h = hn[:, half * BLK:(half + 1) * BLK].T
                mix_ref[0, rows, hsl[h]] = (hn_h * mlg[:, hsl[h]] * og).astype(mix_ref.dtype)
            ct[p] = a_p[p] * c_old[p] + upd[p]
            msc[p:p + 1, :] = m_new[p]

        o_parts = [None] * SWA_HEADS
        for p in range(pairs):
            on = o2[p] / den_p[p]
            o_parts[p] = on[0:SWA_HD, 0:BLK]
            o_parts[p + pairs] = on[SWA_HD:2 * SWA_HD, BLK:2 * BLK]
        o_t = jnp.concatenate(o_parts, axis=0)
        mix_ref[0, rows, ML_WIDTH:ML_WIDTH + SWA_WIDTH] = o_t.T.astype(mix_ref.dtype)
        return carry

    lax.fori_loop(0, nblk, block, 0, unroll=8)

    kband[0:BLK, :] = kband[tm:tm + BLK, :]
    sv_t[0] = sv_t[nblk]

    @pl.when(t == nt - 1)
    def _write_state():
        for h in range(ML_HEADS):
            p, lanes = h // 2, slice((h % 2) * ML_DK, (h % 2 + 1) * ML_DK)
            c_out[0, h] = ct[p, 0:ML_DV, lanes].T
            n_out[0, h:h + 1, :] = ct[p, ML_DV:ML_DV + 1, lanes]
        m_out[0] = msc[...]
        ko_ref[0] = pf[tm - BLK:tm, B_SK:B_SK + BLK]
        vo_ref[0] = vlast[...].T


def _prompt_mixer(x, wall, gbc, mlg, sinks, rb):
    bsz, seq, _ = x.shape
    tm = TM_PROMPT
    nt = seq // tm
    nblk = tm // BLK
    const2 = lambda b, t: (0, 0)
    smem = pl.BlockSpec(memory_space=pltpu.SMEM)
    return pl.pallas_call(
        _prompt_mixer_kernel,
        grid=(bsz, nt),
        in_specs=[
            pl.BlockSpec((1, tm, D_MODEL), lambda b, t: (b, t, 0)),
            pl.BlockSpec((D_MODEL, W_COLS), const2, pipeline_mode=pl.Buffered(1)),
            pl.BlockSpec((T_GROWS, BLK), const2),
            pl.BlockSpec((1, ML_WIDTH), const2),
            smem, smem,
        ],
        out_specs=[
            pl.BlockSpec((1, tm, D_MODEL), lambda b, t: (b, t, 0)),
            pl.BlockSpec((1, ML_HEADS, ML_DK, ML_DV), lambda b, t: (b, 0, 0, 0)),
            pl.BlockSpec((1, ML_HEADS, ML_DK), lambda b, t: (b, 0, 0)),
            pl.BlockSpec((1, 8, 2 * BLK), lambda b, t: (b, 0, 0)),
            pl.BlockSpec((1, BLK, BLK), lambda b, t: (b, 0, 0)),
            pl.BlockSpec((1, BLK, BLK), lambda b, t: (b, 0, 0)),
        ],
        out_shape=[
            jax.ShapeDtypeStruct((bsz, seq, D_MODEL), BF16),
            jax.ShapeDtypeStruct((bsz, ML_HEADS, ML_DK, ML_DV), F32),
            jax.ShapeDtypeStruct((bsz, ML_HEADS, ML_DK), F32),
            jax.ShapeDtypeStruct((bsz, 8, 2 * BLK), F32),
            jax.ShapeDtypeStruct((bsz, BLK, BLK), F32),
            jax.ShapeDtypeStruct((bsz, BLK, BLK), F32),
        ],
        scratch_shapes=[
            pltpu.VMEM((T_COLS, D_MODEL), BF16),
            pltpu.VMEM((tm, ML_HEADS * ML_DK), BF16),
            pltpu.VMEM((tm, B_SV), F32),
            pltpu.VMEM((BLK, BLK), F32),
            pltpu.VMEM((tm + BLK, BLK), BF16),
            pltpu.VMEM((nblk, ML_HEADS * ML_DK, BLK), BF16),
            pltpu.VMEM((nblk, ML_WIDTH, BLK), BF16),
            pltpu.VMEM((nblk, SWA_WIDTH, BLK), BF16),
            pltpu.VMEM((nblk + 1, SWA_KV_HEADS * SWA_HD, BLK), BF16),
            pltpu.VMEM((nblk, T_GROWS, BLK), F32),
            pltpu.VMEM((ML_HEADS // 2, STATE_ROWS, 2 * ML_DK), F32),
            pltpu.VMEM((8, 2 * BLK), F32),
            pltpu.VMEM((2, SWA_HEADS // 2, BLK, 2 * BLK), F32),
        ],
        compiler_params=pltpu.CompilerParams(
            dimension_semantics=("arbitrary", "arbitrary"),
            vmem_limit_bytes=VMEM_LIMIT),
        name="prompt_mixer",
    )(x, wall, gbc, mlg, sinks, rb)


def _roll_rows(x, shift):
    return pltpu.roll(x, shift % x.shape[0], 0)


def _sample_mixer_kernel(x_ref, w_ref, gb_ref, mlg_ref, sinks_ref, rb_ref, mrep_ref,
                         c_ref, n_ref, kc_ref, vc_ref,
                         mix_ref, c_out, n_out, mo_ref, ko_ref, vo_ref,
                         pa, pf, tblc, tbln):
    step = pl.program_id(0)
    tdec = 4
    rows_step = mix_ref.shape[0]
    nslab = rows_step // SLAB
    nseq = SLAB // tdec
    qrows = SWA_HEADS * SLAB

    @pl.when(step == 0)
    def _project_all_rows():
        xb = x_ref[...].astype(BF16)
        tq, tv = N_COLS + T_Q, N_COLS + T_V
        pa[:, A_Q:A_K] = _dot(xb, w_ref[:, tq:tq + ML_HEADS * ML_DK])
        pa[:, A_K:A_V] = _dot(xb, w_ref[:, N_K:N_K + ML_HEADS * ML_DK])
        pa[:, A_V:A_COLS] = _dot(xb, w_ref[:, tv:tv + ML_WIDTH + SWA_WIDTH])
        pf[...] = _dot(xb, w_ref[:, N_B:N_B + B_COLS])

    @pl.when(step == 0)
    def _build_bias_tables():
        r = lax.broadcasted_iota(jnp.int32, (qrows, BLK), 0)
        c = lax.broadcasted_iota(jnp.int32, (qrows, BLK), 1)
        hq = r // SLAB
        bl = (r % SLAB) // tdec
        tq = r % tdec
        d_cache = WINDOW + tq - c
        d_new = tq - (c % tdec)
        key_cache = _t5_bucket(d_cache) * SWA_HEADS + hq
        key_new = _t5_bucket(d_new) * SWA_HEADS + hq
        acc_c = jnp.zeros((qrows, BLK), F32)
        acc_n = jnp.zeros((qrows, BLK), F32)
        for k in range(REL_BUCKETS):
            for h in range(SWA_HEADS):
                val = rb_ref[k, h]
                acc_c = jnp.where(key_cache == k * SWA_HEADS + h, val, acc_c)
                acc_n = jnp.where(key_new == k * SWA_HEADS + h, val, acc_n)
        tblc[...] = jnp.where(c > tq, acc_c, NEG_INF)
        tbln[...] = jnp.where((c < SLAB) & ((c // tdec) == bl) & (d_new >= 0), acc_n, NEG_INF)

    gbias = gb_ref[...]
    mlg = mlg_ref[...]
    r16 = lax.broadcasted_iota(jnp.int32, (SLAB, BLK), 0)
    rr = r16 % tdec
    bl16 = r16 // tdec
    rq = lax.broadcasted_iota(jnp.int32, (qrows, BLK), 0)
    blq = (rq % SLAB) // tdec
    hq_col = rq[:, 0:1] // SLAB
    snk = jnp.zeros((qrows, 1), F32)
    for h in range(SWA_HEADS):
        snk = jnp.where(hq_col == h, sinks_ref[h], snk)
    first_group = rq[:, 0:SWA_HD] < SWA_GROUP * SLAB
    zq = jnp.zeros((SLAB, SWA_HD), F32)
    zeros_pad = jnp.zeros((BLK - SLAB, BLK), BF16)

    def seg_last(x):
        return jnp.where(rr == 3, x,
                         jnp.where(rr == 2, _roll_rows(x, -1),
                                   jnp.where(rr == 1, _roll_rows(x, -2), _roll_rows(x, -3))))

    def seg_max(x):
        m1 = jnp.maximum(x, jnp.where(rr % 2 == 0, _roll_rows(x, -1), _roll_rows(x, 1)))
        return jnp.maximum(m1, jnp.where(rr < 2, _roll_rows(m1, -2), _roll_rows(m1, 2)))

    def slab(si):
        rows = pl.ds(si * SLAB, SLAB)
        prow = pl.ds(pl.multiple_of(step * rows_step + si * SLAB, SLAB), SLAB)
        seq0 = si * nseq

        pre = _softcap(pf[prow, B_G:B_G + BLK] + gbias)
        logf = _log_sigmoid(pre)
        y = logf + jnp.where(rr >= 1, _roll_rows(logf, 1), 0.0)
        bcum = y + jnp.where(rr >= 2, _roll_rows(y, 2), 0.0)
        ig = pltpu.roll(pre, ML_HEADS, 1)
        m_prev = mrep_ref[rows, :]
        b_minus_i = bcum - ig
        log_d = [jnp.where(rr >= dl, bcum - _roll_rows(b_minus_i, dl), -jnp.inf) for dl in range(tdec)]
        rowmax = functools.reduce(jnp.maximum, log_d)
        inter = bcum + m_prev
        m_row = jnp.maximum(inter, rowmax)
        scl = jnp.exp(inter - m_row)
        dm = [jnp.exp(ld - m_row) for ld in log_d]
        enm = jnp.exp(-m_row)
        b_last = seg_last(bcum)
        w = b_last - bcum + ig
        m_new = jnp.maximum(b_last + m_prev, seg_max(w))
        a = jnp.exp(b_last + m_prev - m_new)
        kws = jnp.exp(w - m_new) * ML_SCALE
        mo_ref[rows, :] = m_new

        q = [pa[prow, A_Q + h * ML_DK:A_Q + (h + 1) * ML_DK] for h in range(ML_HEADS)]
        k = [pa[prow, A_K + h * ML_DK:A_K + (h + 1) * ML_DK] for h in range(ML_HEADS)]
        v = [pa[prow, A_V + h * ML_DV:A_V + (h + 1) * ML_DV] for h in range(ML_HEADS)]
        c_b = [[c_ref[seq0 + bl, h] for bl in range(nseq)] for h in range(ML_HEADS)]
        qc, upd, kw = [], [], []
        for h in range(ML_HEADS):
            lane = ML_HEADS + h
            qb = q[h].astype(BF16)
            res = [_dot(qb, jnp.concatenate([c_b[h][2 * i].astype(BF16), c_b[h][2 * i + 1].astype(BF16)], axis=1))
                   for i in range(nseq // 2)]
            parts = [res[bl // 2][:, (bl % 2) * ML_DV:(bl % 2 + 1) * ML_DV] for bl in range(nseq)]
            qc_h = parts[nseq - 1]
            for bl in range(nseq - 2, -1, -1):
                qc_h = jnp.where(bl16 == bl, parts[bl], qc_h)
            qc.append(qc_h)
            kw_h = k[h] * kws[:, lane:lane + 1]
            kw.append(kw_h)
            vm = jnp.concatenate([jnp.where(bl16 == bl, v[h], 0.0) for bl in range(nseq)], axis=1)
            upd.append(_dot_tn(kw_h.astype(BF16), vm.astype(BF16)))

        knew = pf[prow, B_SK:B_SK + BLK]
        vnew = pf[prow, B_SV:B_SV + BLK]
        qparts = []
        for hq in range(SWA_HEADS):
            qh = pa[prow, A_SQ + hq * SWA_HD:A_SQ + (hq + 1) * SWA_HD]
            qparts.append(jnp.concatenate([qh, zq] if hq < SWA_GROUP else [zq, qh], axis=1))
        qs = jnp.concatenate(qparts, axis=0).astype(BF16)
        s_c = _dot_nt(qs, kc_ref[seq0 + nseq - 1].astype(BF16))
        for bl in range(nseq - 2, -1, -1):
            s_c = jnp.where(blq == bl, _dot_nt(qs, kc_ref[seq0 + bl].astype(BF16)), s_c)
        kn = jnp.concatenate([knew.astype(BF16), zeros_pad], axis=0)
        s_n = _dot_nt(qs, kn)

        yield
        for h in range(ML_HEADS):
            hs = slice(h * ML_DK, (h + 1) * ML_DK)
            lane = ML_HEADS + h
            bm = jnp.zeros((SLAB, ML_DV), F32)
            sum_s = jnp.zeros((SLAB, 1), F32)
            for dl in range(tdec):
                s_dl = jnp.sum(q[h] * _roll_rows(k[h], dl), axis=-1, keepdims=True)
                ws = s_dl * ML_SCALE * dm[dl][:, lane:lane + 1]
                bm = bm + ws * _roll_rows(v[h], dl)
                sum_s = sum_s + ws
            nsel = jnp.zeros((SLAB, ML_DK), F32)
            for bl in range(nseq):
                msk = bl16 == bl
                n_b = n_ref[seq0 + bl][h:h + 1, :]
                nsel = jnp.where(msk, n_b, nsel)
                a_b = a[bl * tdec:bl * tdec + 1, lane:lane + 1]
                c_out[seq0 + bl, h] = a_b * c_b[h][bl] + upd[h][:, bl * ML_DV:(bl + 1) * ML_DV]
                n_out[seq0 + bl, h:h + 1, :] = a_b * n_b + jnp.sum(jnp.where(msk, kw[h], 0.0), axis=0,
                                                                   keepdims=True)
            sclh = scl[:, lane:lane + 1]
            num = sclh * qc[h] + bm
            den = sclh * jnp.sum(q[h] * nsel, axis=-1, keepdims=True) + sum_s
            hh = num / jnp.maximum(jnp.abs(den), enm[:, lane:lane + 1])
            hn = hh * lax.rsqrt(jnp.mean(hh * hh, axis=-1, keepdims=True) + NORM_EPS)
            og = jax.nn.sigmoid(pf[prow, B_MO + h * ML_DV:B_MO + (h + 1) * ML_DV])
            mix_ref[rows, hs] = (hn * mlg[:, hs] * og).astype(mix_ref.dtype)

        s_c = s_c * SWA_SCALE + tblc[...]
        s_n = s_n * SWA_SCALE + tbln[...]
        mx = jnp.maximum(jnp.maximum(jnp.max(s_c, axis=-1, keepdims=True),
                                     jnp.max(s_n, axis=-1, keepdims=True)), snk)
        p_c = jnp.exp(s_c - mx)
        p_n = jnp.exp(s_n - mx)
        den = (jnp.sum(p_c, axis=-1, keepdims=True) + jnp.sum(p_n, axis=-1, keepdims=True)
               + jnp.exp(snk - mx))

        yield
        vn = jnp.concatenate([vnew.astype(BF16), zeros_pad], axis=0)
        o = _dot(p_n.astype(BF16), vn)
        for bl in range(nseq):
            o = o + _dot(jnp.where(blq == bl, p_c, 0.0).astype(BF16), vc_ref[seq0 + bl].astype(BF16))
        o = o / den
        osel = jnp.where(first_group, o[:, 0:SWA_HD], o[:, SWA_HD:2 * SWA_HD])
        og = jnp.concatenate([osel[hq * SLAB:(hq + 1) * SLAB, :] for hq in range(SWA_HEADS)], axis=1)
        mix_ref[rows, ML_WIDTH:ML_WIDTH + SWA_WIDTH] = og.astype(mix_ref.dtype)

        for bl in range(nseq):
            ko_ref[seq0 + bl, 0:WINDOW - tdec, :] = kc_ref[seq0 + bl, tdec:WINDOW, :]
            vo_ref[seq0 + bl, 0:WINDOW - tdec, :] = vc_ref[seq0 + bl, tdec:WINDOW, :]
            ko_ref[seq0 + bl, WINDOW - tdec:WINDOW, :] = knew[bl * tdec:(bl + 1) * tdec, :]
            vo_ref[seq0 + bl, WINDOW - tdec:WINDOW, :] = vnew[bl * tdec:(bl + 1) * tdec, :]
        yield

    slabs = [slab(si) for si in range(nslab)]
    for _ in range(3):
        for s in slabs:
            next(s)


def _sample_mixer(x2d, wall, gb, mlg, sinks, rb, mrep, c0, n0, kc, vc):
    rows = x2d.shape[0]
    tdec = 4
    nseq = rows // tdec
    gbs = SAMPLE_GB
    steps = nseq // gbs
    rb_rows = gbs * tdec
    const2 = lambda i: (0, 0)
    smem = pl.BlockSpec(memory_space=pltpu.SMEM)
    return pl.pallas_call(
        _sample_mixer_kernel,
        grid=(steps,),
        in_specs=[
            pl.BlockSpec((rows, D_MODEL), const2, pipeline_mode=pl.Buffered(1)),
            pl.BlockSpec((D_MODEL, W_COLS), const2, pipeline_mode=pl.Buffered(1)),
            pl.BlockSpec((1, BLK), const2),
            pl.BlockSpec((1, ML_WIDTH), const2),
            smem, smem,
            pl.BlockSpec((rb_rows, BLK), lambda i: (i, 0)),
            pl.BlockSpec((gbs, ML_HEADS, ML_DK, ML_DV), lambda i: (i, 0, 0, 0)),
            pl.BlockSpec((gbs, ML_HEADS, ML_DK), lambda i: (i, 0, 0)),
            pl.BlockSpec((gbs, WINDOW, BLK), lambda i: (i, 0, 0)),
            pl.BlockSpec((gbs, WINDOW, BLK), lambda i: (i, 0, 0)),
        ],
        out_specs=[
            pl.BlockSpec((rb_rows, D_MODEL), lambda i: (i, 0)),
            pl.BlockSpec((gbs, ML_HEADS, ML_DK, ML_DV), lambda i: (i, 0, 0, 0)),
            pl.BlockSpec((gbs, ML_HEADS, ML_DK), lambda i: (i, 0, 0)),
            pl.BlockSpec((rb_rows, BLK), lambda i: (i, 0)),
            pl.BlockSpec((gbs, WINDOW, BLK), lambda i: (i, 0, 0)),
            pl.BlockSpec((gbs, WINDOW, BLK), lambda i: (i, 0, 0)),
        ],
        out_shape=[
            jax.ShapeDtypeStruct((rows, D_MODEL), BF16),
            jax.ShapeDtypeStruct((nseq, ML_HEADS, ML_DK, ML_DV), F32),
            jax.ShapeDtypeStruct((nseq, ML_HEADS, ML_DK), F32),
            jax.ShapeDtypeStruct((rows, BLK), F32),
            jax.ShapeDtypeStruct((nseq, WINDOW, BLK), F32),
            jax.ShapeDtypeStruct((nseq, WINDOW, BLK), F32),
        ],
        scratch_shapes=[
            pltpu.VMEM((rows, A_COLS), F32),
            pltpu.VMEM((rows, B_COLS), F32),
            pltpu.VMEM((SWA_HEADS * SLAB, BLK), F32),
            pltpu.VMEM((SWA_HEADS * SLAB, BLK), F32),
        ],
        compiler_params=pltpu.CompilerParams(
            dimension_semantics=("arbitrary",),
            vmem_limit_bytes=VMEM_LIMIT),
        name="sample_mixer",
    )(x2d, wall, gb, mlg, sinks, rb, mrep, c0, n0, kc, vc)


def _dense_kernel(x_ref, mix_ref, wo_ref, g1_ref, b1_ref, wfi_ref, wfo_ref, g2_ref, b2_ref, y_ref):
    nsub = x_ref.shape[0] // DENSE_SUB
    subs = [slice(i * DENSE_SUB, (i + 1) * DENSE_SUB) for i in range(nsub)]
    chunks = list(zip(FF_SPLITS[:-1], FF_SPLITS[1:]))
    g1, b1, g2, b2 = g1_ref[...], b1_ref[...], g2_ref[...], b2_ref[...]

    proj = [_dot(mix_ref[s, :], wo_ref[...]) for s in subs]
    h1 = [_layer_norm(ALPHA * x_ref[s, :] + proj[i], g1, b1) for i, s in enumerate(subs)]
    h1b = [h.astype(BF16) for h in h1]
    acc = [None] * nsub
    for c0, c1 in chunks:
        gate = [_dot(h1b[i], wfi_ref[:, c0:c1]) for i in range(nsub)]
        up = [_dot(h1b[i], wfi_ref[:, D_FF + c0:D_FF + c1]) for i in range(nsub)]
        for i in range(nsub):
            act = (gate[i] * jax.nn.sigmoid(gate[i]) * up[i]).astype(BF16)
            part = _dot(act, wfo_ref[c0:c1, :])
            acc[i] = part if acc[i] is None else acc[i] + part
    for i, s in enumerate(subs):
        y_ref[s, :] = _layer_norm(ALPHA * h1[i] + acc[i], g2, b2)


def _dense(x2d, mix2d, wo, g1, b1, wfi, wfo, g2, b2):
    rows = x2d.shape[0]
    tm = min(TM_DENSE, rows)
    const2 = lambda i: (0, 0)

    def wspec(shape):
        return pl.BlockSpec(shape, const2, pipeline_mode=pl.Buffered(1))

    return pl.pallas_call(
        _dense_kernel,
        grid=(rows // tm,),
        in_specs=[
            pl.BlockSpec((tm, D_MODEL), lambda i: (i, 0)),
            pl.BlockSpec((tm, D_MODEL), lambda i: (i, 0)),
            wspec((D_MODEL, D_MODEL)),
            wspec((1, D_MODEL)), wspec((1, D_MODEL)),
            wspec((D_MODEL, 2 * D_FF)),
            wspec((D_FF, D_MODEL)),
            wspec((1, D_MODEL)), wspec((1, D_MODEL)),
        ],
        out_specs=pl.BlockSpec((tm, D_MODEL), lambda i: (i, 0)),
        out_shape=jax.ShapeDtypeStruct((rows, D_MODEL), F32),
        compiler_params=pltpu.CompilerParams(
            dimension_semantics=("arbitrary",),
            vmem_limit_bytes=VMEM_LIMIT),
        name="dense",
    )(x2d, mix2d, wo, g1, b1, wfi, wfo, g2, b2)


def kernel(x_prompt, x_sample, state_mlstm_C, state_mlstm_n, state_mlstm_m, cache_swa_k, cache_swa_v,
           w_in, b_igate, b_fgate, ml_norm_g, swa_sinks, rel_bias, w_out, ln1_g, ln1_b,
           w_ffn_in, w_ffn_out, ln2_g, ln2_b):
    bp, seq, _ = x_prompt.shape
    bs, tdec, _ = x_sample.shape
    l = 0

    offs = [0]
    for n in IN_SIZES:
        offs.append(offs[-1] + n)
    w = w_in[l]
    mq, mk, mv, mo, mi, mf, sq, sk, sv = [w[:, offs[i]:offs[i + 1]] for i in range(len(IN_SIZES))]
    gpad = jnp.zeros((D_MODEL, BLK - 2 * ML_HEADS), w.dtype)
    wall = jnp.concatenate([mk, mo, sk, sv, mi, mf, gpad,
                            mq, mv, sq, sv, mi, mf, gpad],
                           axis=1).astype(BF16)
    gvec = jnp.concatenate([b_igate[l], b_fgate[l]]).astype(F32)
    gb = jnp.pad(gvec, (0, BLK - 2 * ML_HEADS))[None, :]
    gbc = jnp.broadcast_to(jnp.pad(gvec, (0, T_GROWS - 2 * ML_HEADS))[:, None], (T_GROWS, BLK))
    mlg = ml_norm_g[l][None, :].astype(F32)
    sinks = swa_sinks[l].astype(F32)
    rb = rel_bias.astype(F32)
    wo = w_out[l].astype(BF16)
    wfi = w_ffn_in[l].astype(BF16)
    wfo = w_ffn_out[l].astype(BF16)
    g1, b1 = ln1_g[l][None, :], ln1_b[l][None, :]
    g2, b2 = ln2_g[l][None, :], ln2_b[l][None, :]

    mix_p, c_p, n_p, m_p, k_p, v_p = _prompt_mixer(x_prompt, wall, gbc, mlg, sinks, rb)
    y_p = _dense(x_prompt.reshape(bp * seq, D_MODEL), mix_p.reshape(bp * seq, D_MODEL),
                 wo, g1, b1, wfi, wfo, g2, b2).reshape(bp, seq, D_MODEL)
    p_m = m_p[:, :ML_HEADS // 2, ::BLK].reshape(bp, ML_HEADS)[None]
    p_k = k_p.reshape(1, bp, WINDOW, SWA_KV_HEADS, SWA_HD)
    p_v = v_p.reshape(1, bp, WINDOW, SWA_KV_HEADS, SWA_HD)

    xs = x_sample.reshape(bs * tdec, D_MODEL)
    m0 = state_mlstm_m[l].astype(F32)
    mrep = jnp.pad(jnp.repeat(m0, tdec, axis=0), ((0, 0), (ML_HEADS, BLK - 2 * ML_HEADS)))
    wlen = cache_swa_k.shape[2]
    kc = cache_swa_k[l].reshape(bs, wlen, SWA_KV_HEADS * SWA_HD)
    vc = cache_swa_v[l].reshape(bs, wlen, SWA_KV_HEADS * SWA_HD)
    mix_s, c_s, n_s, mo_s, k_s, v_s = _sample_mixer(
        xs, wall, gb, mlg, sinks, rb, mrep,
        state_mlstm_C[l].astype(F32), state_mlstm_n[l].astype(F32), kc, vc)
    y_s = _dense(xs, mix_s, wo, g1, b1, wfi, wfo, g2, b2).reshape(bs, tdec, D_MODEL)
    s_m = mo_s.reshape(bs, tdec, BLK)[:, 0, ML_HEADS:2 * ML_HEADS][None]
    s_k = k_s.reshape(1, bs, wlen, SWA_KV_HEADS, SWA_HD)
    s_v = v_s.reshape(1, bs, wlen, SWA_KV_HEADS, SWA_HD)

    return (y_p, y_s, c_p[None], n_p[None], p_m, p_k, p_v,
            c_s[None], n_s[None], s_m, s_k, s_v)
```

```python
import functools
import math

import jax
import jax.numpy as jnp
from jax import lax
from jax.experimental import pallas as pl
from jax.experimental.pallas import tpu as pltpu

F32 = jnp.float32
BF16 = jnp.bfloat16

D_MODEL = 1024
ML_HEADS = 4
ML_DK = 128
ML_DV = 128
ML_WIDTH = ML_HEADS * ML_DV
GATE_SOFTCAP = 15.0
SWA_HEADS = 8
SWA_KV_HEADS = 2
SWA_GROUP = SWA_HEADS // SWA_KV_HEADS
SWA_HD = 64
SWA_WIDTH = SWA_HEADS * SWA_HD
WINDOW = 128
REL_BUCKETS = 32
REL_MAX_DIST = 128
D_FF = 2816
DEPTH = 1
ALPHA = (2.0 * DEPTH) ** 0.25
LN_EPS = 1e-5
NORM_EPS = 1e-6
NEG_INF = -1e30
IN_SIZES = (512, 512, 512, 512, 4, 4, 512, 128, 128)
ML_SCALE = ML_DK ** -0.5
LOG_ML_SCALE = math.log(ML_SCALE)
SWA_SCALE = SWA_HD ** -0.5

BLK = 128

A_Q, A_K, A_V, A_SQ = 0, 512, 1024, 1536
A_COLS = 2048
B_MO, B_SK, B_SV, B_G = 0, 512, 640, 768
B_COLS = 896

N_K = 0
N_B = 512
T_Q, T_V, T_SQ, T_SV, T_G = 0, 512, 1024, 1536, 1664
T_GROWS = 16
T_COLS = T_G + BLK
STATE_ROWS = ML_DV + 16

TM_PROMPT = 1024
TM_DENSE = 1024
DENSE_SUB = 512
FF_SPLITS = (0, 1536, D_FF)
SAMPLE_GB = 16
SLAB = 16
VMEM_LIMIT = 56 * 1024 * 1024


def _softcap(a):
    return GATE_SOFTCAP * jnp.tanh(a / GATE_SOFTCAP)


def _log_sigmoid(x):
    return jnp.minimum(x, 0.0) - jnp.log1p(jnp.exp(-jnp.abs(x)))


def _layer_norm(z, g, b):
    mu = jnp.mean(z, axis=-1, keepdims=True)
    zc = z - mu
    var = jnp.mean(zc * zc, axis=-1, keepdims=True)
    return zc * lax.rsqrt(var + LN_EPS) * g + b


def _dot(a, b):
    return jnp.dot(a, b, preferred_element_type=F32)


def _dot_nt(a, b):
    return lax.dot_general(a, b, (((1,), (1,)), ((), ())), preferred_element_type=F32)


def _dot_tn(a, b):
    return lax.dot_general(a, b, (((0,), (0,)), ((), ())), preferred_element_type=F32)


def _split3(x):
    hi = x.astype(BF16)
    r1 = x - hi.astype(F32)
    mid = r1.astype(BF16)
    lo = (r1 - mid.astype(F32)).astype(BF16)
    return hi, mid, lo


def _t5_bucket(d):
    n = jnp.maximum(d, 0)
    max_exact = REL_BUCKETS // 2
    nlog = REL_BUCKETS - max_exact
    large = jnp.full(n.shape, max_exact, jnp.int32)
    for k in range(1, nlog):
        thr = math.ceil(max_exact * (REL_MAX_DIST / max_exact) ** (k / nlog))
        large = large + jnp.where(n >= thr, 1, 0)
    return jnp.where(n < max_exact, n, large)


def _prompt_mixer_kernel(x_ref, w_ref, gbc_ref, mlg_ref, sinks_ref, rb_ref,
                         mix_ref, c_out, n_out, m_out, ko_ref, vo_ref,
                         wt, wn, kn, pf, vlast, kband, q_t, v_t, sq_t, sv_t, g_t, ct, msc, tbl):
    b = pl.program_id(0)
    t = pl.program_id(1)
    nt = pl.num_programs(1)
    tm = x_ref.shape[1]
    nblk = tm // BLK
    pairs = SWA_HEADS // 2

    @pl.when((b == 0) & (t == 0))
    def _prepare_weights():
        offs = [0]
        for n in IN_SIZES:
            offs.append(offs[-1] + n)
        o_mq, o_mk, o_mv, o_mo, o_mi, _, o_sq, o_sk, o_sv = offs[:-1]

        def col_blocks(dst_col, src_col, n):
            for i in range(n // BLK):
                wn[:, dst_col + i * BLK:dst_col + (i + 1) * BLK] = (
                    w_ref[:, src_col + i * BLK:src_col + (i + 1) * BLK].astype(BF16))

        def row_blocks(dst_row, src_col, n):
            for i in range(n // BLK):
                wt[dst_row + i * BLK:dst_row + (i + 1) * BLK, :] = (
                    w_ref[:, src_col + i * BLK:src_col + (i + 1) * BLK].T.astype(BF16))

        col_blocks(N_K, o_mk, ML_HEADS * ML_DK)
        col_blocks(N_B + B_MO, o_mo, ML_WIDTH)
        col_blocks(N_B + B_SK, o_sk, SWA_KV_HEADS * SWA_HD)
        row_blocks(T_Q, o_mq, ML_HEADS * ML_DK)
        row_blocks(T_V, o_mv, ML_WIDTH)
        row_blocks(T_SQ, o_sq, SWA_WIDTH)
        row_blocks(T_SV, o_sv, SWA_KV_HEADS * SWA_HD)
        gates_t = w_ref[:, o_mi:o_mi + BLK].T
        grow = lax.broadcasted_iota(jnp.int32, gates_t.shape, 0)
        wt[T_G:T_G + BLK, :] = jnp.where(grow < 2 * ML_HEADS, gates_t, 0.0).astype(BF16)

    @pl.when((b == 0) & (t == 0))
    def _build_bias_tables():
        r = lax.broadcasted_iota(jnp.int32, (BLK, 2 * BLK), 0)
        ln = lax.broadcasted_iota(jnp.int32, (BLK, 2 * BLK), 1)
        second = ln >= BLK
        qi = jnp.where(second, ln - BLK, ln)
        prev = r > qi
        d = jnp.where(prev, WINDOW + qi - r, qi - r)
        bucket = _t5_bucket(d)
        for p in range(pairs):
            acc = jnp.zeros((BLK, 2 * BLK), F32)
            for k in range(REL_BUCKETS):
                acc = jnp.where(bucket == k, jnp.where(second, rb_ref[k, p + pairs], rb_ref[k, p]), acc)
            tbl[0, p] = jnp.where(prev, NEG_INF, acc)
            tbl[1, p] = acc

    @pl.when(t == 0)
    def _reset_state():
        ct[...] = jnp.zeros_like(ct)
        msc[...] = jnp.zeros_like(msc)
        kband[0:BLK, :] = jnp.zeros((BLK, kband.shape[1]), kband.dtype)
        sv_t[0] = jnp.zeros(sv_t.shape[1:], sv_t.dtype)

    xb = x_ref[0].astype(BF16)
    kn[...] = _dot(xb, wn[:, N_K:N_K + ML_HEADS * ML_DK]).astype(BF16)
    pf[...] = _dot(xb, wn[:, N_B:N_B + B_SV])
    kband[BLK:BLK + tm, :] = pf[:, B_SK:B_SK + BLK].astype(BF16)

    res_t = _dot_nt(wt[0:T_G + T_GROWS, :], xb)
    vlast[...] = res_t[T_SV:T_SV + BLK, tm - BLK:tm]

    def put(dst, r0, nrows, off, scale=None):
        res = res_t[r0:r0 + nrows, :]
        if scale is not None:
            res = res * scale
        for jj in range(nblk):
            dst[jj + off] = res[:, jj * BLK:(jj + 1) * BLK].astype(dst.dtype)

    put(q_t, T_Q, ML_HEADS * ML_DK, 0)
    put(v_t, T_V, ML_WIDTH, 0)
    put(sq_t, T_SQ, SWA_WIDTH, 0, SWA_SCALE)
    put(sv_t, T_SV, SWA_KV_HEADS * SWA_HD, 1)
    put(g_t, T_G, T_GROWS, 0)

    row = lax.broadcasted_iota(jnp.int32, (BLK, BLK), 0)
    col = lax.broadcasted_iota(jnp.int32, (BLK, BLK), 1)
    causal_t = row <= col
    triu = causal_t.astype(BF16)
    causal2 = jnp.concatenate([causal_t, causal_t], axis=1)
    r16 = lax.broadcasted_iota(jnp.int32, (16, 2 * BLK), 0)
    ones_rows = (r16 == 0).astype(BF16)
    z128 = jnp.zeros((BLK, BLK), BF16)
    ml_pairs = ML_HEADS // 2
    zero_rows = jnp.zeros((BLK - T_GROWS, BLK), F32)
    gbc = gbc_ref[...]
    mlg = mlg_ref[...]
    lane2 = lax.broadcasted_iota(jnp.int32, (1, 2 * BLK), 1)
    snk_rows = [jnp.where(lane2 >= BLK, sinks_ref[p + pairs], sinks_ref[p]) for p in range(pairs)]
    zq = jnp.zeros((SWA_HD, BLK), BF16)
    prev2 = jnp.concatenate([row > col, row > col], axis=1)

    def pair_row(x0, x1):
        return jnp.concatenate([jnp.broadcast_to(x0, (1, BLK)), jnp.broadcast_to(x1, (1, BLK))], axis=1)

    def block_diag(x0, x1):
        return jnp.concatenate([jnp.concatenate([x0, z128], axis=1),
                                jnp.concatenate([z128, x1], axis=1)], axis=0)

    def block(j, carry):
        r0 = pl.multiple_of(j * BLK, BLK)
        rows = pl.ds(r0, BLK)
        hsl = [slice(h * ML_DK, (h + 1) * ML_DK) for h in range(ML_HEADS)]
        psl = [slice(p * 2 * ML_DK, (p + 1) * 2 * ML_DK) for p in range(ml_pairs)]

        pre_t = _softcap(g_t[j] + gbc)
        hi, mid, lo = _split3(_log_sigmoid(pre_t))
        b_t = _dot(hi, triu) + _dot(mid, triu) + _dot(lo, triu)
        u_t = pltpu.roll(pre_t, ML_HEADS, 0) - b_t
        u_c = jnp.concatenate([u_t, zero_rows], axis=0).T

        k2 = [kn[rows, psl[p]] for p in range(ml_pairs)]
        bdq = [block_diag(q_t[j, hsl[2 * p], :], q_t[j, hsl[2 * p + 1], :]) for p in range(ml_pairs)]
        vte = [jnp.concatenate([jnp.concatenate([v_t[j, hsl[2 * p], :], v_t[j, hsl[2 * p + 1], :]], axis=1),
                                ones_rows], axis=0) for p in range(ml_pairs)]
        kq = [_dot(k2[p], bdq[p]) for p in range(ml_pairs)]

        sel = jnp.where((t == 0) & (j == 0), 0, 1)
        kb = kband[pl.ds(r0, 2 * BLK), :]
        vtb = jnp.concatenate([sv_t[j], sv_t[j + 1]], axis=1)
        sc2 = []
        for p in range(pairs):
            qa = sq_t[j, p * SWA_HD:(p + 1) * SWA_HD, :]
            qb = sq_t[j, (p + pairs) * SWA_HD:(p + pairs + 1) * SWA_HD, :]
            bd = jnp.concatenate([jnp.concatenate([qa, zq], axis=1),
                                  jnp.concatenate([zq, qb], axis=1)], axis=0)
            sc2.append(_dot(kb, bd))

        bds, scl, m_row, m_new, a_p, vwte = [], [], [], [], [], []
        for p in range(ml_pairs):
            h0, h1 = 2 * p, 2 * p + 1
            b_row = pair_row(b_t[ML_HEADS + h0:ML_HEADS + h0 + 1, :], b_t[ML_HEADS + h1:ML_HEADS + h1 + 1, :])
            i_row = pair_row(pre_t[h0:h0 + 1, :], pre_t[h1:h1 + 1, :])
            u_col = jnp.concatenate(
                [jnp.broadcast_to(u_c[:, ML_HEADS + h0:ML_HEADS + h0 + 1], (BLK, BLK)),
                 jnp.broadcast_to(u_c[:, ML_HEADS + h1:ML_HEADS + h1 + 1], (BLK, BLK))], axis=1)
            m_prev = msc[p:p + 1, :]
            log_d = jnp.where(causal2, b_row + u_col, -jnp.inf)
            inter = b_row + m_prev
            mr = jnp.maximum(inter, jnp.max(log_d, axis=0, keepdims=True))
            m_row.append(mr)
            scl.append(jnp.exp(inter - mr))
            s2 = (kq[p] * jnp.exp(log_d - (mr - LOG_ML_SCALE))).astype(BF16)
            bds.append(block_diag(s2[:, :BLK], s2[:, BLK:]))
            b_last = pair_row(b_row[:, BLK - 1:BLK], b_row[:, 2 * BLK - 1:2 * BLK])
            w_row = b_last - b_row + i_row
            w_max = pair_row(jnp.max(w_row[:, :BLK], axis=1, keepdims=True),
                             jnp.max(w_row[:, BLK:], axis=1, keepdims=True))
            mn = jnp.maximum(b_last + m_prev, w_max)
            m_new.append(mn)
            a_p.append(jnp.exp(b_last + m_prev - mn))
            wexp = jnp.exp(w_row - mn) * ML_SCALE
            vwte.append(vte[p] * wexp.astype(BF16))

        pw, den_p = [], []
        for p in range(pairs):
            sc = jnp.where(prev2, sc2[p][:BLK], sc2[p][BLK:]) + tbl[sel, p]
            snk = snk_rows[p]
            mx = jnp.maximum(jnp.max(sc, axis=0, keepdims=True), snk)
            pe = jnp.exp(sc - mx)
            den_p.append(jnp.sum(pe, axis=0, keepdims=True) + jnp.exp(snk - mx))
            pw.append(jnp.concatenate([jnp.where(prev2, pe, 0.0), jnp.where(prev2, 0.0, pe)],
                                      axis=0).astype(BF16))

        c_old = [ct[p] for p in range(ml_pairs)]
        cq = [_dot(c_old[p].astype(BF16), bdq[p]) for p in range(ml_pairs)]
        vs = [_dot(vte[p], bds[p]) for p in range(ml_pairs)]
        upd = [_dot(vwte[p], block_diag(k2[p][:, :BLK], k2[p][:, BLK:])) for p in range(ml_pairs)]
        o2 = [_dot(vtb, pw[p]) for p in range(pairs)]

        for p in range(ml_pairs):
            nd = scl[p] * cq[p] + vs[p]
            num = nd[:ML_DV]
            den = nd[ML_DV:ML_DV + 1]
            hh = num / jnp.maximum(jnp.abs(den), jnp.exp(-m_row[p]))
            hn = hh * lax.rsqrt(jnp.mean(hh * hh, axis=0, keepdims=True) + NORM_EPS)
            for half in range(2):
                h = 2 * p + half
                og = jax.nn.sigmoid(pf[rows, B_MO + h * ML_DV:B_MO + (h + 1) * ML_DV])
                hn_h = hn[:, half * BLK:(half + 1) * BLK].T
                mix_ref[0, rows, hsl[h]] = (hn_h * mlg[:, hsl[h]] * og).astype(mix_ref.dtype)
            ct[p] = a_p[p] * c_old[p] + upd[p]
            msc[p:p + 1, :] = m_new[p]

        o_parts = [None] * SWA_HEADS
        for p in range(pairs):
            on = o2[p] / den_p[p]
            o_parts[p] = on[0:SWA_HD, 0:BLK]
            o_parts[p + pairs] = on[SWA_HD:2 * SWA_HD, BLK:2 * BLK]
        o_t = jnp.concatenate(o_parts, axis=0)
        mix_ref[0, rows, ML_WIDTH:ML_WIDTH + SWA_WIDTH] = o_t.T.astype(mix_ref.dtype)
        return carry

    lax.fori_loop(0, nblk, block, 0, unroll=8)

    kband[0:BLK, :] = kband[tm:tm + BLK, :]
    sv_t[0] = sv_t[nblk]

    @pl.when(t == nt - 1)
    def _write_state():
        for h in range(ML_HEADS):
            p, lanes = h // 2, slice((h % 2) * ML_DK, (h % 2 + 1) * ML_DK)
            c_out[0, h] = ct[p, 0:ML_DV, lanes].T
            n_out[0, h:h + 1, :] = ct[p, ML_DV:ML_DV + 1, lanes]
        m_out[0] = msc[...]
        ko_ref[0] = pf[tm - BLK:tm, B_SK:B_SK + BLK]
        vo_ref[0] = vlast[...].T


def _prompt_mixer(x, w_in_l, gbc, mlg, sinks, rb):
    bsz, seq, _ = x.shape
    tm = TM_PROMPT
    nt = seq // tm
    nblk = tm // BLK
    const2 = lambda b, t: (0, 0)
    smem = pl.BlockSpec(memory_space=pltpu.SMEM)
    return pl.pallas_call(
        _prompt_mixer_kernel,
        grid=(bsz, nt),
        in_specs=[
            pl.BlockSpec((1, tm, D_MODEL), lambda b, t: (b, t, 0)),
            pl.BlockSpec((D_MODEL, sum(IN_SIZES)), const2, pipeline_mode=pl.Buffered(1)),
            pl.BlockSpec((T_GROWS, BLK), const2),
            pl.BlockSpec((1, ML_WIDTH), const2),
            smem, smem,
        ],
        out_specs=[
            pl.BlockSpec((1, tm, D_MODEL), lambda b, t: (b, t, 0)),
            pl.BlockSpec((1, ML_HEADS, ML_DK, ML_DV), lambda b, t: (b, 0, 0, 0)),
            pl.BlockSpec((1, ML_HEADS, ML_DK), lambda b, t: (b, 0, 0)),
            pl.BlockSpec((1, 8, 2 * BLK), lambda b, t: (b, 0, 0)),
            pl.BlockSpec((1, BLK, BLK), lambda b, t: (b, 0, 0)),
            pl.BlockSpec((1, BLK, BLK), lambda b, t: (b, 0, 0)),
            pl.BlockSpec((T_COLS, D_MODEL), const2),
            pl.BlockSpec((D_MODEL, N_B + B_SV), const2),
        ],
        out_shape=[
            jax.ShapeDtypeStruct((bsz, seq, D_MODEL), BF16),
            jax.ShapeDtypeStruct((bsz, ML_HEADS, ML_DK, ML_DV), F32),
            jax.ShapeDtypeStruct((bsz, ML_HEADS, ML_DK), F32),
            jax.ShapeDtypeStruct((bsz, 8, 2 * BLK), F32),
            jax.ShapeDtypeStruct((bsz, BLK, BLK), F32),
            jax.ShapeDtypeStruct((bsz, BLK, BLK), F32),
            jax.ShapeDtypeStruct((T_COLS, D_MODEL), BF16),
            jax.ShapeDtypeStruct((D_MODEL, N_B + B_SV), BF16),
        ],
        scratch_shapes=[
            pltpu.VMEM((tm, ML_HEADS * ML_DK), BF16),
            pltpu.VMEM((tm, B_SV), F32),
            pltpu.VMEM((BLK, BLK), F32),
            pltpu.VMEM((tm + BLK, BLK), BF16),
            pltpu.VMEM((nblk, ML_HEADS * ML_DK, BLK), BF16),
            pltpu.VMEM((nblk, ML_WIDTH, BLK), BF16),
            pltpu.VMEM((nblk, SWA_WIDTH, BLK), BF16),
            pltpu.VMEM((nblk + 1, SWA_KV_HEADS * SWA_HD, BLK), BF16),
            pltpu.VMEM((nblk, T_GROWS, BLK), F32),
            pltpu.VMEM((ML_HEADS // 2, STATE_ROWS, 2 * ML_DK), F32),
            pltpu.VMEM((8, 2 * BLK), F32),
            pltpu.VMEM((2, SWA_HEADS // 2, BLK, 2 * BLK), F32),
        ],
        compiler_params=pltpu.CompilerParams(
            dimension_semantics=("arbitrary", "arbitrary"),
            vmem_limit_bytes=VMEM_LIMIT),
        name="prompt_mixer",
    )(x, w_in_l, gbc, mlg, sinks, rb)


def _roll_rows(x, shift):
    return pltpu.roll(x, shift % x.shape[0], 0)


def _sample_mixer_kernel(x_ref, wt_ref, wn_ref, gb_ref, mlg_ref, sinks_ref, rb_ref, mrep_ref,
                         c_ref, n_ref, kc_ref, vc_ref,
                         mix_ref, c_out, n_out, mo_ref, ko_ref, vo_ref,
                         pa, pf, tblc, tbln):
    step = pl.program_id(0)
    tdec = 4
    rows_step = mix_ref.shape[0]
    nslab = rows_step // SLAB
    nseq = SLAB // tdec
    qrows = SWA_HEADS * SLAB

    @pl.when(step == 0)
    def _project_all_rows():
        xb = x_ref[...].astype(BF16)
        pa[:, A_Q:A_K] = _dot_nt(xb, wt_ref[T_Q:T_Q + ML_HEADS * ML_DK, :])
        pa[:, A_K:A_V] = _dot(xb, wn_ref[:, N_K:N_K + ML_HEADS * ML_DK])
        pa[:, A_V:A_COLS] = _dot_nt(xb, wt_ref[T_V:T_V + ML_WIDTH + SWA_WIDTH, :])
        pf[:, B_MO:B_SV] = _dot(xb, wn_ref[:, N_B:N_B + B_SV])
        pf[:, B_SV:B_COLS] = _dot_nt(xb, wt_ref[T_SV:T_COLS, :])

    @pl.when(step == 0)
    def _build_bias_tables():
        r = lax.broadcasted_iota(jnp.int32, (qrows, BLK), 0)
        c = lax.broadcasted_iota(jnp.int32, (qrows, BLK), 1)
        hq = r // SLAB
        bl = (r % SLAB) // tdec
        tq = r % tdec
        d_cache = WINDOW + tq - c
        d_new = tq - (c % tdec)
        key_cache = _t5_bucket(d_cache) * SWA_HEADS + hq
        key_new = _t5_bucket(d_new) * SWA_HEADS + hq
        acc_c = jnp.zeros((qrows, BLK), F32)
        acc_n = jnp.zeros((qrows, BLK), F32)
        for k in range(REL_BUCKETS):
            for h in range(SWA_HEADS):
                val = rb_ref[k, h]
                acc_c = jnp.where(key_cache == k * SWA_HEADS + h, val, acc_c)
                acc_n = jnp.where(key_new == k * SWA_HEADS + h, val, acc_n)
        tblc[...] = jnp.where(c > tq, acc_c, NEG_INF)
        tbln[...] = jnp.where((c < SLAB) & ((c // tdec) == bl) & (d_new >= 0), acc_n, NEG_INF)

    gbias = gb_ref[...]
    mlg = mlg_ref[...]
    r16 = lax.broadcasted_iota(jnp.int32, (SLAB, BLK), 0)
    rr = r16 % tdec
    bl16 = r16 // tdec
    rq = lax.broadcasted_iota(jnp.int32, (qrows, BLK), 0)
    blq = (rq % SLAB) // tdec
    hq_col = rq[:, 0:1] // SLAB
    snk = jnp.zeros((qrows, 1), F32)
    for h in range(SWA_HEADS):
        snk = jnp.where(hq_col == h, sinks_ref[h], snk)
    first_group = rq[:, 0:SWA_HD] < SWA_GROUP * SLAB
    zq = jnp.zeros((SLAB, SWA_HD), F32)
    zeros_pad = jnp.zeros((BLK - SLAB, BLK), BF16)

    def seg_last(x):
        return jnp.where(rr == 3, x,
                         jnp.where(rr == 2, _roll_rows(x, -1),
                                   jnp.where(rr == 1, _roll_rows(x, -2), _roll_rows(x, -3))))

    def seg_max(x):
        m1 = jnp.maximum(x, jnp.where(rr % 2 == 0, _roll_rows(x, -1), _roll_rows(x, 1)))
        return jnp.maximum(m1, jnp.where(rr < 2, _roll_rows(m1, -2), _roll_rows(m1, 2)))

    def slab(si):
        rows = pl.ds(si * SLAB, SLAB)
        prow = pl.ds(pl.multiple_of(step * rows_step + si * SLAB, SLAB), SLAB)
        seq0 = si * nseq

        pre = _softcap(pf[prow, B_G:B_G + BLK] + gbias)
        logf = _log_sigmoid(pre)
        y = logf + jnp.where(rr >= 1, _roll_rows(logf, 1), 0.0)
        bcum = y + jnp.where(rr >= 2, _roll_rows(y, 2), 0.0)
        ig = pltpu.roll(pre, ML_HEADS, 1)
        m_prev = mrep_ref[rows, :]
        b_minus_i = bcum - ig
        log_d = [jnp.where(rr >= dl, bcum - _roll_rows(b_minus_i, dl), -jnp.inf) for dl in range(tdec)]
        rowmax = functools.reduce(jnp.maximum, log_d)
        inter = bcum + m_prev
        m_row = jnp.maximum(inter, rowmax)
        scl = jnp.exp(inter - m_row)
        dm = [jnp.exp(ld - m_row) for ld in log_d]
        enm = jnp.exp(-m_row)
        b_last = seg_last(bcum)
        w = b_last - bcum + ig
        m_new = jnp.maximum(b_last + m_prev, seg_max(w))
        a = jnp.exp(b_last + m_prev - m_new)
        kws = jnp.exp(w - m_new) * ML_SCALE
        mo_ref[rows, :] = m_new

        q = [pa[prow, A_Q + h * ML_DK:A_Q + (h + 1) * ML_DK] for h in range(ML_HEADS)]
        k = [pa[prow, A_K + h * ML_DK:A_K + (h + 1) * ML_DK] for h in range(ML_HEADS)]
        v = [pa[prow, A_V + h * ML_DV:A_V + (h + 1) * ML_DV] for h in range(ML_HEADS)]
        c_b = [[c_ref[seq0 + bl, h] for bl in range(nseq)] for h in range(ML_HEADS)]
        qc, upd, kw = [], [], []
        for h in range(ML_HEADS):
            lane = ML_HEADS + h
            qb = q[h].astype(BF16)
            res = [_dot(qb, jnp.concatenate([c_b[h][2 * i].astype(BF16), c_b[h][2 * i + 1].astype(BF16)], axis=1))
                   for i in range(nseq // 2)]
            parts = [res[bl // 2][:, (bl % 2) * ML_DV:(bl % 2 + 1) * ML_DV] for bl in range(nseq)]
            qc_h = parts[nseq - 1]
            for bl in range(nseq - 2, -1, -1):
                qc_h = jnp.where(bl16 == bl, parts[bl], qc_h)
            qc.append(qc_h)
            kw_h = k[h] * kws[:, lane:lane + 1]
            kw.append(kw_h)
            vm = jnp.concatenate([jnp.where(bl16 == bl, v[h], 0.0) for bl in range(nseq)], axis=1)
            upd.append(_dot_tn(kw_h.astype(BF16), vm.astype(BF16)))

        knew = pf[prow, B_SK:B_SK + BLK]
        vnew = pf[prow, B_SV:B_SV + BLK]
        qparts = []
        for hq in range(SWA_HEADS):
            qh = pa[prow, A_SQ + hq * SWA_HD:A_SQ + (hq + 1) * SWA_HD]
            qparts.append(jnp.concatenate([qh, zq] if hq < SWA_GROUP else [zq, qh], axis=1))
        qs = jnp.concatenate(qparts, axis=0).astype(BF16)
        s_c = _dot_nt(qs, kc_ref[seq0 + nseq - 1].astype(BF16))
        for bl in range(nseq - 2, -1, -1):
            s_c = jnp.where(blq == bl, _dot_nt(qs, kc_ref[seq0 + bl].astype(BF16)), s_c)
        kn = jnp.concatenate([knew.astype(BF16), zeros_pad], axis=0)
        s_n = _dot_nt(qs, kn)

        yield
        for h in range(ML_HEADS):
            hs = slice(h * ML_DK, (h + 1) * ML_DK)
            lane = ML_HEADS + h
            bm = jnp.zeros((SLAB, ML_DV), F32)
            sum_s = jnp.zeros((SLAB, 1), F32)
            for dl in range(tdec):
                s_dl = jnp.sum(q[h] * _roll_rows(k[h], dl), axis=-1, keepdims=True)
                ws = s_dl * ML_SCALE * dm[dl][:, lane:lane + 1]
                bm = bm + ws * _roll_rows(v[h], dl)
                sum_s = sum_s + ws
            nsel = jnp.zeros((SLAB, ML_DK), F32)
            for bl in range(nseq):
                msk = bl16 == bl
                n_b = n_ref[seq0 + bl][h:h + 1, :]
                nsel = jnp.where(msk, n_b, nsel)
                a_b = a[bl * tdec:bl * tdec + 1, lane:lane + 1]
                c_out[seq0 + bl, h] = a_b * c_b[h][bl] + upd[h][:, bl * ML_DV:(bl + 1) * ML_DV]
                n_out[seq0 + bl, h:h + 1, :] = a_b * n_b + jnp.sum(jnp.where(msk, kw[h], 0.0), axis=0,
                                                                   keepdims=True)
            sclh = scl[:, lane:lane + 1]
            num = sclh * qc[h] + bm
            den = sclh * jnp.sum(q[h] * nsel, axis=-1, keepdims=True) + sum_s
            hh = num / jnp.maximum(jnp.abs(den), enm[:, lane:lane + 1])
            hn = hh * lax.rsqrt(jnp.mean(hh * hh, axis=-1, keepdims=True) + NORM_EPS)
            og = jax.nn.sigmoid(pf[prow, B_MO + h * ML_DV:B_MO + (h + 1) * ML_DV])
            mix_ref[rows, hs] = (hn * mlg[:, hs] * og).astype(mix_ref.dtype)

        s_c = s_c * SWA_SCALE + tblc[...]
        s_n = s_n * SWA_SCALE + tbln[...]
        mx = jnp.maximum(jnp.maximum(jnp.max(s_c, axis=-1, keepdims=True),
                                     jnp.max(s_n, axis=-1, keepdims=True)), snk)
        p_c = jnp.exp(s_c - mx)
        p_n = jnp.exp(s_n - mx)
        den = (jnp.sum(p_c, axis=-1, keepdims=True) + jnp.sum(p_n, axis=-1, keepdims=True)
               + jnp.exp(snk - mx))

        yield
        vn = jnp.concatenate([vnew.astype(BF16), zeros_pad], axis=0)
        o = _dot(p_n.astype(BF16), vn)
        for bl in range(nseq):
            o = o + _dot(jnp.where(blq == bl, p_c, 0.0).astype(BF16), vc_ref[seq0 + bl].astype(BF16))
        o = o / den
        osel = jnp.where(first_group, o[:, 0:SWA_HD], o[:, SWA_HD:2 * SWA_HD])
        og = jnp.concatenate([osel[hq * SLAB:(hq + 1) * SLAB, :] for hq in range(SWA_HEADS)], axis=1)
        mix_ref[rows, ML_WIDTH:ML_WIDTH + SWA_WIDTH] = og.astype(mix_ref.dtype)

        for bl in range(nseq):
            ko_ref[seq0 + bl, 0:WINDOW - tdec, :] = kc_ref[seq0 + bl, tdec:WINDOW, :]
            vo_ref[seq0 + bl, 0:WINDOW - tdec, :] = vc_ref[seq0 + bl, tdec:WINDOW, :]
            ko_ref[seq0 + bl, WINDOW - tdec:WINDOW, :] = knew[bl * tdec:(bl + 1) * tdec, :]
            vo_ref[seq0 + bl, WINDOW - tdec:WINDOW, :] = vnew[bl * tdec:(bl + 1) * tdec, :]
        yield

    slabs = [slab(si) for si in range(nslab)]
    for _ in range(3):
        for s in slabs:
            next(s)


def _sample_mixer(x2d, wt, wn, gb, mlg, sinks, rb, mrep, c0, n0, kc, vc):
    rows = x2d.shape[0]
    tdec = 4
    nseq = rows // tdec
    gbs = SAMPLE_GB
    steps = nseq // gbs
    rb_rows = gbs * tdec
    const2 = lambda i: (0, 0)
    smem = pl.BlockSpec(memory_space=pltpu.SMEM)
    return pl.pallas_call(
        _sample_mixer_kernel,
        grid=(steps,),
        in_specs=[
            pl.BlockSpec((rows, D_MODEL), const2, pipeline_mode=pl.Buffered(1)),
            pl.BlockSpec((T_COLS, D_MODEL), const2, pipeline_mode=pl.Buffered(1)),
            pl.BlockSpec((D_MODEL, N_B + B_SV), const2, pipeline_mode=pl.Buffered(1)),
            pl.BlockSpec((1, BLK), const2),
            pl.BlockSpec((1, ML_WIDTH), const2),
            smem, smem,
            pl.BlockSpec((rb_rows, BLK), lambda i: (i, 0)),
            pl.BlockSpec((gbs, ML_HEADS, ML_DK, ML_DV), lambda i: (i, 0, 0, 0)),
            pl.BlockSpec((gbs, ML_HEADS, ML_DK), lambda i: (i, 0, 0)),
            pl.BlockSpec((gbs, WINDOW, BLK), lambda i: (i, 0, 0)),
            pl.BlockSpec((gbs, WINDOW, BLK), lambda i: (i, 0, 0)),
        ],
        out_specs=[
            pl.BlockSpec((rb_rows, D_MODEL), lambda i: (i, 0)),
            pl.BlockSpec((gbs, ML_HEADS, ML_DK, ML_DV), lambda i: (i, 0, 0, 0)),
            pl.BlockSpec((gbs, ML_HEADS, ML_DK), lambda i: (i, 0, 0)),
            pl.BlockSpec((rb_rows, BLK), lambda i: (i, 0)),
            pl.BlockSpec((gbs, WINDOW, BLK), lambda i: (i, 0, 0)),
            pl.BlockSpec((gbs, WINDOW, BLK), lambda i: (i, 0, 0)),
        ],
        out_shape=[
            jax.ShapeDtypeStruct((rows, D_MODEL), BF16),
            jax.ShapeDtypeStruct((nseq, ML_HEADS, ML_DK, ML_DV), F32),
            jax.ShapeDtypeStruct((nseq, ML_HEADS, ML_DK), F32),
            jax.ShapeDtypeStruct((rows, BLK), F32),
            jax.ShapeDtypeStruct((nseq, WINDOW, BLK), F32),
            jax.ShapeDtypeStruct((nseq, WINDOW, BLK), F32),
        ],
        scratch_shapes=[
            pltpu.VMEM((rows, A_COLS), F32),
            pltpu.VMEM((rows, B_COLS), F32),
            pltpu.VMEM((SWA_HEADS * SLAB, BLK), F32),
            pltpu.VMEM((SWA_HEADS * SLAB, BLK), F32),
        ],
        compiler_params=pltpu.CompilerParams(
            dimension_semantics=("arbitrary",),
            vmem_limit_bytes=VMEM_LIMIT),
        name="sample_mixer",
    )(x2d, wt, wn, gb, mlg, sinks, rb, mrep, c0, n0, kc, vc)


def _dense_kernel(x_ref, mix_ref, wo_ref, g1_ref, b1_ref, wfi_ref, wfo_ref, g2_ref, b2_ref, y_ref):
    nsub = x_ref.shape[0] // DENSE_SUB
    subs = [slice(i * DENSE_SUB, (i + 1) * DENSE_SUB) for i in range(nsub)]
    chunks = list(zip(FF_SPLITS[:-1], FF_SPLITS[1:]))
    g1, b1, g2, b2 = g1_ref[...], b1_ref[...], g2_ref[...], b2_ref[...]

    proj = [_dot(mix_ref[s, :], wo_ref[...]) for s in subs]
    h1 = [_layer_norm(ALPHA * x_ref[s, :] + proj[i], g1, b1) for i, s in enumerate(subs)]
    h1b = [h.astype(BF16) for h in h1]
    acc = [None] * nsub
    for c0, c1 in chunks:
        gate = [_dot(h1b[i], wfi_ref[:, c0:c1]) for i in range(nsub)]
        up = [_dot(h1b[i], wfi_ref[:, D_FF + c0:D_FF + c1]) for i in range(nsub)]
        for i in range(nsub):
            act = (gate[i] * jax.nn.sigmoid(gate[i]) * up[i]).astype(BF16)
            part = _dot(act, wfo_ref[c0:c1, :])
            acc[i] = part if acc[i] is None else acc[i] + part
    for i, s in enumerate(subs):
        y_ref[s, :] = _layer_norm(ALPHA * h1[i] + acc[i], g2, b2)


def _dense(x2d, mix2d, wo, g1, b1, wfi, wfo, g2, b2):
    rows = x2d.shape[0]
    tm = min(TM_DENSE, rows)
    const2 = lambda i: (0, 0)

    def wspec(shape):
        return pl.BlockSpec(shape, const2, pipeline_mode=pl.Buffered(1))

    return pl.pallas_call(
        _dense_kernel,
        grid=(rows // tm,),
        in_specs=[
            pl.BlockSpec((tm, D_MODEL), lambda i: (i, 0)),
            pl.BlockSpec((tm, D_MODEL), lambda i: (i, 0)),
            wspec((D_MODEL, D_MODEL)),
            wspec((1, D_MODEL)), wspec((1, D_MODEL)),
            wspec((D_MODEL, 2 * D_FF)),
            wspec((D_FF, D_MODEL)),
            wspec((1, D_MODEL)), wspec((1, D_MODEL)),
        ],
        out_specs=pl.BlockSpec((tm, D_MODEL), lambda i: (i, 0)),
        out_shape=jax.ShapeDtypeStruct((rows, D_MODEL), F32),
        compiler_params=pltpu.CompilerParams(
            dimension_semantics=("arbitrary",),
            vmem_limit_bytes=VMEM_LIMIT),
        name="dense",
    )(x2d, mix2d, wo, g1, b1, wfi, wfo, g2, b2)


def kernel(x_prompt, x_sample, state_mlstm_C, state_mlstm_n, state_mlstm_m, cache_swa_k, cache_swa_v,
           w_in, b_igate, b_fgate, ml_norm_g, swa_sinks, rel_bias, w_out, ln1_g, ln1_b,
           w_ffn_in, w_ffn_out, ln2_g, ln2_b):
    bp, seq, _ = x_prompt.shape
    bs, tdec, _ = x_sample.shape
    l = 0

    gvec = jnp.concatenate([b_igate[l], b_fgate[l]]).astype(F32)
    gb = jnp.pad(gvec, (0, BLK - 2 * ML_HEADS))[None, :]
    gbc = jnp.broadcast_to(jnp.pad(gvec, (0, T_GROWS - 2 * ML_HEADS))[:, None], (T_GROWS, BLK))
    mlg = ml_norm_g[l][None, :].astype(F32)
    sinks = swa_sinks[l].astype(F32)
    rb = rel_bias.astype(F32)
    wo = w_out[l].astype(BF16)
    wfi = w_ffn_in[l].astype(BF16)
    wfo = w_ffn_out[l].astype(BF16)
    g1, b1 = ln1_g[l][None, :], ln1_b[l][None, :]
    g2, b2 = ln2_g[l][None, :], ln2_b[l][None, :]

    mix_p, c_p, n_p, m_p, k_p, v_p, wt, wn = _prompt_mixer(x_prompt, w_in[l], gbc, mlg, sinks, rb)
    y_p = _dense(x_prompt.reshape(bp * seq, D_MODEL), mix_p.reshape(bp * seq, D_MODEL),
                 wo, g1, b1, wfi, wfo, g2, b2).reshape(bp, seq, D_MODEL)
    p_m = m_p[:, :ML_HEADS // 2, ::BLK].reshape(bp, ML_HEADS)[None]
    p_k = k_p.reshape(1, bp, WINDOW, SWA_KV_HEADS, SWA_HD)
    p_v = v_p.reshape(1, bp, WINDOW, SWA_KV_HEADS, SWA_HD)

    xs = x_sample.reshape(bs * tdec, D_MODEL)
    m0 = state_mlstm_m[l].astype(F32)
    mrep = jnp.pad(jnp.repeat(m0, tdec, axis=0), ((0, 0), (ML_HEADS, BLK - 2 * ML_HEADS)))
    wlen = cache_swa_k.shape[2]
    kc = cache_swa_k[l].reshape(bs, wlen, SWA_KV_HEADS * SWA_HD)
    vc = cache_swa_v[l].reshape(bs, wlen, SWA_KV_HEADS * SWA_HD)
    mix_s, c_s, n_s, mo_s, k_s, v_s = _sample_mixer(
        xs, wt, wn, gb, mlg, sinks, rb, mrep,
        state_mlstm_C[l].astype(F32), state_mlstm_n[l].astype(F32), kc, vc)
    y_s = _dense(xs, mix_s, wo, g1, b1, wfi, wfo, g2, b2).reshape(bs, tdec, D_MODEL)
    s_m = mo_s.reshape(bs, tdec, BLK)[:, 0, ML_HEADS:2 * ML_HEADS][None]
    s_k = k_s.reshape(1, bs, wlen, SWA_KV_HEADS, SWA_HD)
    s_v = v_s.reshape(1, bs, wlen, SWA_KV_HEADS, SWA_HD)

    return (y_p, y_s, c_p[None], n_p[None], p_m, p_k, p_v,
            c_s[None], n_s[None], s_m, s_k, s_v)
```

```python
import functools
import math

import jax
import jax.numpy as jnp
from jax import lax
from jax.experimental import pallas as pl
from jax.experimental.pallas import tpu as pltpu

F32 = jnp.float32
BF16 = jnp.bfloat16

D_MODEL = 1024
ML_HEADS = 4
ML_DK = 128
ML_DV = 128
ML_WIDTH = ML_HEADS * ML_DV
GATE_SOFTCAP = 15.0
SWA_HEADS = 8
SWA_KV_HEADS = 2
SWA_GROUP = SWA_HEADS // SWA_KV_HEADS
SWA_HD = 64
SWA_WIDTH = SWA_HEADS * SWA_HD
WINDOW = 128
REL_BUCKETS = 32
REL_MAX_DIST = 128
D_FF = 2816
DEPTH = 1
ALPHA = (2.0 * DEPTH) ** 0.25
LN_EPS = 1e-5
NORM_EPS = 1e-6
NEG_INF = -1e30
IN_SIZES = (512, 512, 512, 512, 4, 4, 512, 128, 128)
ML_SCALE = ML_DK ** -0.5
LOG_ML_SCALE = math.log(ML_SCALE)
SWA_SCALE = SWA_HD ** -0.5

BLK = 128

A_Q, A_K, A_V, A_SQ = 0, 512, 1024, 1536
A_COLS = 2048
B_MO, B_SK, B_SV, B_G = 0, 512, 640, 768
B_COLS = 896

N_K = 0
N_B = 512
T_Q, T_V, T_SQ, T_SV, T_G = 0, 512, 1024, 1536, 1664
T_GROWS = 16
T_COLS = T_G + BLK
STATE_ROWS = ML_DV + 16

TM_PROMPT = 1024
TM_DENSE = 1024
DENSE_SUB = 512
FF_SPLITS = (0, 1536, D_FF)
SAMPLE_GB = 16
SLAB = 16
VMEM_LIMIT = 56 * 1024 * 1024


def _softcap(a):
    return GATE_SOFTCAP * jnp.tanh(a / GATE_SOFTCAP)


def _log_sigmoid(x):
    return jnp.minimum(x, 0.0) - jnp.log1p(jnp.exp(-jnp.abs(x)))


def _layer_norm(z, g, b):
    mu = jnp.mean(z, axis=-1, keepdims=True)
    zc = z - mu
    var = jnp.mean(zc * zc, axis=-1, keepdims=True)
    return zc * lax.rsqrt(var + LN_EPS) * g + b


def _dot(a, b):
    return jnp.dot(a, b, preferred_element_type=F32)


def _dot_nt(a, b):
    return lax.dot_general(a, b, (((1,), (1,)), ((), ())), preferred_element_type=F32)


def _dot_tn(a, b):
    return lax.dot_general(a, b, (((0,), (0,)), ((), ())), preferred_element_type=F32)


def _split3(x):
    hi = x.astype(BF16)
    r1 = x - hi.astype(F32)
    mid = r1.astype(BF16)
    lo = (r1 - mid.astype(F32)).astype(BF16)
    return hi, mid, lo


def _t5_bucket(d):
    n = jnp.maximum(d, 0)
    max_exact = REL_BUCKETS // 2
    nlog = REL_BUCKETS - max_exact
    large = jnp.full(n.shape, max_exact, jnp.int32)
    for k in range(1, nlog):
        thr = math.ceil(max_exact * (REL_MAX_DIST / max_exact) ** (k / nlog))
        large = large + jnp.where(n >= thr, 1, 0)
    return jnp.where(n < max_exact, n, large)


def _prompt_mixer_kernel(x_ref, w_ref, gbc_ref, mlg_ref, sinks_ref, rb_ref,
                         mix_ref, c_out, n_out, m_out, ko_ref, vo_ref,
                         wt_out, wn_out, wt, wn, kn, pf, vlast, kband, q_t, v_t, sq_t, sv_t, g_t, ct, msc, tbl):
    b = pl.program_id(0)
    t = pl.program_id(1)
    nt = pl.num_programs(1)
    tm = x_ref.shape[1]
    nblk = tm // BLK
    pairs = SWA_HEADS // 2

    @pl.when((b == 0) & (t == 0))
    def _prepare_weights():
        offs = [0]
        for n in IN_SIZES:
            offs.append(offs[-1] + n)
        o_mq, o_mk, o_mv, o_mo, o_mi, _, o_sq, o_sk, o_sv = offs[:-1]

        def col_blocks(dst_col, src_col, n):
            for i in range(n // BLK):
                wn[:, dst_col + i * BLK:dst_col + (i + 1) * BLK] = (
                    w_ref[:, src_col + i * BLK:src_col + (i + 1) * BLK].astype(BF16))

        def row_blocks(dst_row, src_col, n):
            for i in range(n // BLK):
                wt[dst_row + i * BLK:dst_row + (i + 1) * BLK, :] = (
                    w_ref[:, src_col + i * BLK:src_col + (i + 1) * BLK].T.astype(BF16))

        col_blocks(N_K, o_mk, ML_HEADS * ML_DK)
        col_blocks(N_B + B_MO, o_mo, ML_WIDTH)
        col_blocks(N_B + B_SK, o_sk, SWA_KV_HEADS * SWA_HD)
        row_blocks(T_Q, o_mq, ML_HEADS * ML_DK)
        row_blocks(T_V, o_mv, ML_WIDTH)
        row_blocks(T_SQ, o_sq, SWA_WIDTH)
        row_blocks(T_SV, o_sv, SWA_KV_HEADS * SWA_HD)
        gates_t = w_ref[:, o_mi:o_mi + BLK].T
        grow = lax.broadcasted_iota(jnp.int32, gates_t.shape, 0)
        wt[T_G:T_G + BLK, :] = jnp.where(grow < 2 * ML_HEADS, gates_t, 0.0).astype(BF16)
        wt_out[...] = wt[...]
        wn_out[...] = wn[...]

    @pl.when((b == 0) & (t == 0))
    def _build_bias_tables():
        r = lax.broadcasted_iota(jnp.int32, (BLK, 2 * BLK), 0)
        ln = lax.broadcasted_iota(jnp.int32, (BLK, 2 * BLK), 1)
        second = ln >= BLK
        qi = jnp.where(second, ln - BLK, ln)
        prev = r > qi
        d = jnp.where(prev, WINDOW + qi - r, qi - r)
        bucket = _t5_bucket(d)
        for p in range(pairs):
            acc = jnp.zeros((BLK, 2 * BLK), F32)
            for k in range(REL_BUCKETS):
                acc = jnp.where(bucket == k, jnp.where(second, rb_ref[k, p + pairs], rb_ref[k, p]), acc)
            tbl[0, p] = jnp.where(prev, NEG_INF, acc)
            tbl[1, p] = acc

    @pl.when(t == 0)
    def _reset_state():
        ct[...] = jnp.zeros_like(ct)
        msc[...] = jnp.zeros_like(msc)
        kband[0:BLK, :] = jnp.zeros((BLK, kband.shape[1]), kband.dtype)
        sv_t[0] = jnp.zeros(sv_t.shape[1:], sv_t.dtype)

    xb = x_ref[0].astype(BF16)
    kn[...] = _dot(xb, wn[:, N_K:N_K + ML_HEADS * ML_DK]).astype(BF16)
    pf[...] = _dot(xb, wn[:, N_B:N_B + B_SV])
    kband[BLK:BLK + tm, :] = pf[:, B_SK:B_SK + BLK].astype(BF16)

    res_t = _dot_nt(wt[0:T_G + T_GROWS, :], xb)
    vlast[...] = res_t[T_SV:T_SV + BLK, tm - BLK:tm]

    def put(dst, r0, nrows, off, scale=None):
        res = res_t[r0:r0 + nrows, :]
        if scale is not None:
            res = res * scale
        for jj in range(nblk):
            dst[jj + off] = res[:, jj * BLK:(jj + 1) * BLK].astype(dst.dtype)

    put(q_t, T_Q, ML_HEADS * ML_DK, 0)
    put(v_t, T_V, ML_WIDTH, 0)
    put(sq_t, T_SQ, SWA_WIDTH, 0, SWA_SCALE)
    put(sv_t, T_SV, SWA_KV_HEADS * SWA_HD, 1)
    put(g_t, T_G, T_GROWS, 0)

    row = lax.broadcasted_iota(jnp.int32, (BLK, BLK), 0)
    col = lax.broadcasted_iota(jnp.int32, (BLK, BLK), 1)
    causal_t = row <= col
    triu = causal_t.astype(BF16)
    causal2 = jnp.concatenate([causal_t, causal_t], axis=1)
    r16 = lax.broadcasted_iota(jnp.int32, (16, 2 * BLK), 0)
    ones_rows = (r16 == 0).astype(BF16)
    z128 = jnp.zeros((BLK, BLK), BF16)
    ml_pairs = ML_HEADS // 2
    zero_rows = jnp.zeros((BLK - T_GROWS, BLK), F32)
    gbc = gbc_ref[...]
    mlg = mlg_ref[...]
    lane2 = lax.broadcasted_iota(jnp.int32, (1, 2 * BLK), 1)
    snk_rows = [jnp.where(lane2 >= BLK, sinks_ref[p + pairs], sinks_ref[p]) for p in range(pairs)]
    zq = jnp.zeros((SWA_HD, BLK), BF16)
    prev2 = jnp.concatenate([row > col, row > col], axis=1)

    def pair_row(x0, x1):
        return jnp.concatenate([jnp.broadcast_to(x0, (1, BLK)), jnp.broadcast_to(x1, (1, BLK))], axis=1)

    def block_diag(x0, x1):
        return jnp.concatenate([jnp.concatenate([x0, z128], axis=1),
                                jnp.concatenate([z128, x1], axis=1)], axis=0)

    def block(j, carry):
        r0 = pl.multiple_of(j * BLK, BLK)
        rows = pl.ds(r0, BLK)
        hsl = [slice(h * ML_DK, (h + 1) * ML_DK) for h in range(ML_HEADS)]
        psl = [slice(p * 2 * ML_DK, (p + 1) * 2 * ML_DK) for p in range(ml_pairs)]

        pre_t = _softcap(g_t[j] + gbc)
        hi, mid, lo = _split3(_log_sigmoid(pre_t))
        b_t = _dot(hi, triu) + _dot(mid, triu) + _dot(lo, triu)
        u_t = pltpu.roll(pre_t, ML_HEADS, 0) - b_t
        u_c = jnp.concatenate([u_t, zero_rows], axis=0).T

        k2 = [kn[rows, psl[p]] for p in range(ml_pairs)]
        bdq = [block_diag(q_t[j, hsl[2 * p], :], q_t[j, hsl[2 * p + 1], :]) for p in range(ml_pairs)]
        vte = [jnp.concatenate([jnp.concatenate([v_t[j, hsl[2 * p], :], v_t[j, hsl[2 * p + 1], :]], axis=1),
                                ones_rows], axis=0) for p in range(ml_pairs)]
        kq = [_dot(k2[p], bdq[p]) for p in range(ml_pairs)]

        sel = jnp.where((t == 0) & (j == 0), 0, 1)
        kb = kband[pl.ds(r0, 2 * BLK), :]
        vtb = jnp.concatenate([sv_t[j], sv_t[j + 1]], axis=1)
        sc2 = []
        for p in range(pairs):
            qa = sq_t[j, p * SWA_HD:(p + 1) * SWA_HD, :]
            qb = sq_t[j, (p + pairs) * SWA_HD:(p + pairs + 1) * SWA_HD, :]
            bd = jnp.concatenate([jnp.concatenate([qa, zq], axis=1),
                                  jnp.concatenate([zq, qb], axis=1)], axis=0)
            sc2.append(_dot(kb, bd))

        bds, scl, m_row, m_new, a_p, vwte = [], [], [], [], [], []
        for p in range(ml_pairs):
            h0, h1 = 2 * p, 2 * p + 1
            b_row = pair_row(b_t[ML_HEADS + h0:ML_HEADS + h0 + 1, :], b_t[ML_HEADS + h1:ML_HEADS + h1 + 1, :])
            i_row = pair_row(pre_t[h0:h0 + 1, :], pre_t[h1:h1 + 1, :])
            u_col = jnp.concatenate(
                [jnp.broadcast_to(u_c[:, ML_HEADS + h0:ML_HEADS + h0 + 1], (BLK, BLK)),
                 jnp.broadcast_to(u_c[:, ML_HEADS + h1:ML_HEADS + h1 + 1], (BLK, BLK))], axis=1)
            m_prev = msc[p:p + 1, :]
            log_d = jnp.where(causal2, b_row + u_col, -jnp.inf)
            inter = b_row + m_prev
            mr = jnp.maximum(inter, jnp.max(log_d, axis=0, keepdims=True))
            m_row.append(mr)
            scl.append(jnp.exp(inter - mr))
            s2 = (kq[p] * jnp.exp(log_d - (mr - LOG_ML_SCALE))).astype(BF16)
            bds.append(block_diag(s2[:, :BLK], s2[:, BLK:]))
            b_last = pair_row(b_row[:, BLK - 1:BLK], b_row[:, 2 * BLK - 1:2 * BLK])
            w_row = b_last - b_row + i_row
            w_max = pair_row(jnp.max(w_row[:, :BLK], axis=1, keepdims=True),
                             jnp.max(w_row[:, BLK:], axis=1, keepdims=True))
            mn = jnp.maximum(b_last + m_prev, w_max)
            m_new.append(mn)
            a_p.append(jnp.exp(b_last + m_prev - mn))
            wexp = jnp.exp(w_row - mn) * ML_SCALE
            vwte.append(vte[p] * wexp.astype(BF16))

        pw, den_p = [], []
        for p in range(pairs):
            sc = jnp.where(prev2, sc2[p][:BLK], sc2[p][BLK:]) + tbl[sel, p]
            snk = snk_rows[p]
            mx = jnp.maximum(jnp.max(sc, axis=0, keepdims=True), snk)
            pe = jnp.exp(sc - mx)
            den_p.append(jnp.sum(pe, axis=0, keepdims=True) + jnp.exp(snk - mx))
            pw.append(jnp.concatenate([jnp.where(prev2, pe, 0.0), jnp.where(prev2, 0.0, pe)],
                                      axis=0).astype(BF16))

        c_old = [ct[p] for p in range(ml_pairs)]
        cq = [_dot(c_old[p].astype(BF16), bdq[p]) for p in range(ml_pairs)]
        vs = [_dot(vte[p], bds[p]) for p in range(ml_pairs)]
        upd = [_dot(vwte[p], block_diag(k2[p][:, :BLK], k2[p][:, BLK:])) for p in range(ml_pairs)]
        o2 = [_dot(vtb, pw[p]) for p in range(pairs)]

        for p in range(ml_pairs):
            nd = scl[p] * cq[p] + vs[p]
            num = nd[:ML_DV]
            den = nd[ML_DV:ML_DV + 1]
            hh = num / jnp.maximum(jnp.abs(den), jnp.exp(-m_row[p]))
            hn = hh * lax.rsqrt(jnp.mean(hh * hh, axis=0, keepdims=True) + NORM_EPS)
            for half in range(2):
                h = 2 * p + half
                og = jax.nn.sigmoid(pf[rows, B_MO + h * ML_DV:B_MO + (h + 1) * ML_DV])
                hn_h = hn[:, half * BLK:(half + 1) * BLK].T
                mix_ref[0, rows, hsl[h]] = (hn_h * mlg[:, hsl[h]] * og).astype(mix_ref.dtype)
            ct[p] = a_p[p] * c_old[p] + upd[p]
            msc[p:p + 1, :] = m_new[p]

        o_parts = [None] * SWA_HEADS
        for p in range(pairs):
            on = o2[p] / den_p[p]
            o_parts[p] = on[0:SWA_HD, 0:BLK]
            o_parts[p + pairs] = on[SWA_HD:2 * SWA_HD, BLK:2 * BLK]
        o_t = jnp.concatenate(o_parts, axis=0)
        mix_ref[0, rows, ML_WIDTH:ML_WIDTH + SWA_WIDTH] = o_t.T.astype(mix_ref.dtype)
        return carry

    lax.fori_loop(0, nblk, block, 0, unroll=8)

    kband[0:BLK, :] = kband[tm:tm + BLK, :]
    sv_t[0] = sv_t[nblk]

    @pl.when(t == nt - 1)
    def _write_state():
        for h in range(ML_HEADS):
            p, lanes = h // 2, slice((h % 2) * ML_DK, (h % 2 + 1) * ML_DK)
            c_out[0, h] = ct[p, 0:ML_DV, lanes].T
            n_out[0, h:h + 1, :] = ct[p, ML_DV:ML_DV + 1, lanes]
        m_out[0] = msc[...]
        ko_ref[0] = pf[tm - BLK:tm, B_SK:B_SK + BLK]
        vo_ref[0] = vlast[...].T


def _prompt_mixer(x, w_in_l, gbc, mlg, sinks, rb):
    bsz, seq, _ = x.shape
    tm = TM_PROMPT
    nt = seq // tm
    nblk = tm // BLK
    const2 = lambda b, t: (0, 0)
    smem = pl.BlockSpec(memory_space=pltpu.SMEM)
    return pl.pallas_call(
        _prompt_mixer_kernel,
        grid=(bsz, nt),
        in_specs=[
            pl.BlockSpec((1, tm, D_MODEL), lambda b, t: (b, t, 0)),
            pl.BlockSpec((D_MODEL, sum(IN_SIZES)), const2, pipeline_mode=pl.Buffered(1)),
            pl.BlockSpec((T_GROWS, BLK), const2),
            pl.BlockSpec((1, ML_WIDTH), const2),
            smem, smem,
        ],
        out_specs=[
            pl.BlockSpec((1, tm, D_MODEL), lambda b, t: (b, t, 0)),
            pl.BlockSpec((1, ML_HEADS, ML_DK, ML_DV), lambda b, t: (b, 0, 0, 0)),
            pl.BlockSpec((1, ML_HEADS, ML_DK), lambda b, t: (b, 0, 0)),
            pl.BlockSpec((1, 8, 2 * BLK), lambda b, t: (b, 0, 0)),
            pl.BlockSpec((1, BLK, BLK), lambda b, t: (b, 0, 0)),
            pl.BlockSpec((1, BLK, BLK), lambda b, t: (b, 0, 0)),
            pl.BlockSpec((T_COLS, D_MODEL), const2),
            pl.BlockSpec((D_MODEL, N_B + B_SV), const2),
        ],
        out_shape=[
            jax.ShapeDtypeStruct((bsz, seq, D_MODEL), BF16),
            jax.ShapeDtypeStruct((bsz, ML_HEADS, ML_DK, ML_DV), F32),
            jax.ShapeDtypeStruct((bsz, ML_HEADS, ML_DK), F32),
            jax.ShapeDtypeStruct((bsz, 8, 2 * BLK), F32),
            jax.ShapeDtypeStruct((bsz, BLK, BLK), F32),
            jax.ShapeDtypeStruct((bsz, BLK, BLK), F32),
            jax.ShapeDtypeStruct((T_COLS, D_MODEL), BF16),
            jax.ShapeDtypeStruct((D_MODEL, N_B + B_SV), BF16),
        ],
        scratch_shapes=[
            pltpu.VMEM((T_COLS, D_MODEL), BF16),
            pltpu.VMEM((D_MODEL, N_B + B_SV), BF16),
            pltpu.VMEM((tm, ML_HEADS * ML_DK), BF16),
            pltpu.VMEM((tm, B_SV), F32),
            pltpu.VMEM((BLK, BLK), F32),
            pltpu.VMEM((tm + BLK, BLK), BF16),
            pltpu.VMEM((nblk, ML_HEADS * ML_DK, BLK), BF16),
            pltpu.VMEM((nblk, ML_WIDTH, BLK), BF16),
            pltpu.VMEM((nblk, SWA_WIDTH, BLK), BF16),
            pltpu.VMEM((nblk + 1, SWA_KV_HEADS * SWA_HD, BLK), BF16),
            pltpu.VMEM((nblk, T_GROWS, BLK), F32),
            pltpu.VMEM((ML_HEADS // 2, STATE_ROWS, 2 * ML_DK), F32),
            pltpu.VMEM((8, 2 * BLK), F32),
            pltpu.VMEM((2, SWA_HEADS // 2, BLK, 2 * BLK), F32),
        ],
        compiler_params=pltpu.CompilerParams(
            dimension_semantics=("arbitrary", "arbitrary"),
            vmem_limit_bytes=VMEM_LIMIT),
        name="prompt_mixer",
    )(x, w_in_l, gbc, mlg, sinks, rb)


def _roll_rows(x, shift):
    return pltpu.roll(x, shift % x.shape[0], 0)


def _sample_mixer_kernel(x_ref, wt_ref, wn_ref, gb_ref, mlg_ref, sinks_ref, rb_ref, mrep_ref,
                         c_ref, n_ref, kc_ref, vc_ref,
                         mix_ref, c_out, n_out, mo_ref, ko_ref, vo_ref,
                         pa, pf, tblc, tbln):
    step = pl.program_id(0)
    tdec = 4
    rows_step = mix_ref.shape[0]
    nslab = rows_step // SLAB
    nseq = SLAB // tdec
    qrows = SWA_HEADS * SLAB

    @pl.when(step == 0)
    def _project_all_rows():
        xb = x_ref[...].astype(BF16)
        pa[:, A_Q:A_K] = _dot_nt(xb, wt_ref[T_Q:T_Q + ML_HEADS * ML_DK, :])
        pa[:, A_K:A_V] = _dot(xb, wn_ref[:, N_K:N_K + ML_HEADS * ML_DK])
        pa[:, A_V:A_COLS] = _dot_nt(xb, wt_ref[T_V:T_V + ML_WIDTH + SWA_WIDTH, :])
        pf[:, B_MO:B_SV] = _dot(xb, wn_ref[:, N_B:N_B + B_SV])
        pf[:, B_SV:B_COLS] = _dot_nt(xb, wt_ref[T_SV:T_COLS, :])

    @pl.when(step == 0)
    def _build_bias_tables():
        r = lax.broadcasted_iota(jnp.int32, (qrows, BLK), 0)
        c = lax.broadcasted_iota(jnp.int32, (qrows, BLK), 1)
        hq = r // SLAB
        bl = (r % SLAB) // tdec
        tq = r % tdec
        d_cache = WINDOW + tq - c
        d_new = tq - (c % tdec)
        key_cache = _t5_bucket(d_cache) * SWA_HEADS + hq
        key_new = _t5_bucket(d_new) * SWA_HEADS + hq
        acc_c = jnp.zeros((qrows, BLK), F32)
        acc_n = jnp.zeros((qrows, BLK), F32)
        for k in range(REL_BUCKETS):
            for h in range(SWA_HEADS):
                val = rb_ref[k, h]
                acc_c = jnp.where(key_cache == k * SWA_HEADS + h, val, acc_c)
                acc_n = jnp.where(key_new == k * SWA_HEADS + h, val, acc_n)
        tblc[...] = jnp.where(c > tq, acc_c, NEG_INF)
        tbln[...] = jnp.where((c < SLAB) & ((c // tdec) == bl) & (d_new >= 0), acc_n, NEG_INF)

    gbias = gb_ref[...]
    mlg = mlg_ref[...]
    r16 = lax.broadcasted_iota(jnp.int32, (SLAB, BLK), 0)
    rr = r16 % tdec
    bl16 = r16 // tdec
    rq = lax.broadcasted_iota(jnp.int32, (qrows, BLK), 0)
    blq = (rq % SLAB) // tdec
    hq_col = rq[:, 0:1] // SLAB
    snk = jnp.zeros((qrows, 1), F32)
    for h in range(SWA_HEADS):
        snk = jnp.where(hq_col == h, sinks_ref[h], snk)
    first_group = rq[:, 0:SWA_HD] < SWA_GROUP * SLAB
    zq = jnp.zeros((SLAB, SWA_HD), F32)
    zeros_pad = jnp.zeros((BLK - SLAB, BLK), BF16)

    def seg_last(x):
        return jnp.where(rr == 3, x,
                         jnp.where(rr == 2, _roll_rows(x, -1),
                                   jnp.where(rr == 1, _roll_rows(x, -2), _roll_rows(x, -3))))

    def seg_max(x):
        m1 = jnp.maximum(x, jnp.where(rr % 2 == 0, _roll_rows(x, -1), _roll_rows(x, 1)))
        return jnp.maximum(m1, jnp.where(rr < 2, _roll_rows(m1, -2), _roll_rows(m1, 2)))

    def slab(si):
        rows = pl.ds(si * SLAB, SLAB)
        prow = pl.ds(pl.multiple_of(step * rows_step + si * SLAB, SLAB), SLAB)
        seq0 = si * nseq

        pre = _softcap(pf[prow, B_G:B_G + BLK] + gbias)
        logf = _log_sigmoid(pre)
        y = logf + jnp.where(rr >= 1, _roll_rows(logf, 1), 0.0)
        bcum = y + jnp.where(rr >= 2, _roll_rows(y, 2), 0.0)
        ig = pltpu.roll(pre, ML_HEADS, 1)
        m_prev = mrep_ref[rows, :]
        b_minus_i = bcum - ig
        log_d = [jnp.where(rr >= dl, bcum - _roll_rows(b_minus_i, dl), -jnp.inf) for dl in range(tdec)]
        rowmax = functools.reduce(jnp.maximum, log_d)
        inter = bcum + m_prev
        m_row = jnp.maximum(inter, rowmax)
        scl = jnp.exp(inter - m_row)
        dm = [jnp.exp(ld - m_row) for ld in log_d]
        enm = jnp.exp(-m_row)
        b_last = seg_last(bcum)
        w = b_last - bcum + ig
        m_new = jnp.maximum(b_last + m_prev, seg_max(w))
        a = jnp.exp(b_last + m_prev - m_new)
        kws = jnp.exp(w - m_new) * ML_SCALE
        mo_ref[rows, :] = m_new

        q = [pa[prow, A_Q + h * ML_DK:A_Q + (h + 1) * ML_DK] for h in range(ML_HEADS)]
        k = [pa[prow, A_K + h * ML_DK:A_K + (h + 1) * ML_DK] for h in range(ML_HEADS)]
        v = [pa[prow, A_V + h * ML_DV:A_V + (h + 1) * ML_DV] for h in range(ML_HEADS)]
        c_b = [[c_ref[seq0 + bl, h] for bl in range(nseq)] for h in range(ML_HEADS)]
        qc, upd, kw = [], [], []
        for h in range(ML_HEADS):
            lane = ML_HEADS + h
            qb = q[h].astype(BF16)
            res = [_dot(qb, jnp.concatenate([c_b[h][2 * i].astype(BF16), c_b[h][2 * i + 1].astype(BF16)], axis=1))
                   for i in range(nseq // 2)]
            parts = [res[bl // 2][:, (bl % 2) * ML_DV:(bl % 2 + 1) * ML_DV] for bl in range(nseq)]
            qc_h = parts[nseq - 1]
            for bl in range(nseq - 2, -1, -1):
                qc_h = jnp.where(bl16 == bl, parts[bl], qc_h)
            qc.append(qc_h)
            kw_h = k[h] * kws[:, lane:lane + 1]
            kw.append(kw_h)
            vm = jnp.concatenate([jnp.where(bl16 == bl, v[h], 0.0) for bl in range(nseq)], axis=1)
            upd.append(_dot_tn(kw_h.astype(BF16), vm.astype(BF16)))

        knew = pf[prow, B_SK:B_SK + BLK]
        vnew = pf[prow, B_SV:B_SV + BLK]
        qparts = []
        for hq in range(SWA_HEADS):
            qh = pa[prow, A_SQ + hq * SWA_HD:A_SQ + (hq + 1) * SWA_HD]
            qparts.append(jnp.concatenate([qh, zq] if hq < SWA_GROUP else [zq, qh], axis=1))
        qs = jnp.concatenate(qparts, axis=0).astype(BF16)
        s_c = _dot_nt(qs, kc_ref[seq0 + nseq - 1].astype(BF16))
        for bl in range(nseq - 2, -1, -1):
            s_c = jnp.where(blq == bl, _dot_nt(qs, kc_ref[seq0 + bl].astype(BF16)), s_c)
        kn = jnp.concatenate([knew.astype(BF16), zeros_pad], axis=0)
        s_n = _dot_nt(qs, kn)

        yield
        for h in range(ML_HEADS):
            hs = slice(h * ML_DK, (h + 1) * ML_DK)
            lane = ML_HEADS + h
            bm = jnp.zeros((SLAB, ML_DV), F32)
            sum_s = jnp.zeros((SLAB, 1), F32)
            for dl in range(tdec):
                s_dl = jnp.sum(q[h] * _roll_rows(k[h], dl), axis=-1, keepdims=True)
                ws = s_dl * ML_SCALE * dm[dl][:, lane:lane + 1]
                bm = bm + ws * _roll_rows(v[h], dl)
                sum_s = sum_s + ws
            nsel = jnp.zeros((SLAB, ML_DK), F32)
            for bl in range(nseq):
                msk = bl16 == bl
                n_b = n_ref[seq0 + bl][h:h + 1, :]
                nsel = jnp.where(msk, n_b, nsel)
                a_b = a[bl * tdec:bl * tdec + 1, lane:lane + 1]
                c_out[seq0 + bl, h] = a_b * c_b[h][bl] + upd[h][:, bl * ML_DV:(bl + 1) * ML_DV]
                n_out[seq0 + bl, h:h + 1, :] = a_b * n_b + jnp.sum(jnp.where(msk, kw[h], 0.0), axis=0,
                                                                   keepdims=True)
            sclh = scl[:, lane:lane + 1]
            num = sclh * qc[h] + bm
            den = sclh * jnp.sum(q[h] * nsel, axis=-1, keepdims=True) + sum_s
            hh = num / jnp.maximum(jnp.abs(den), enm[:, lane:lane + 1])
            hn = hh * lax.rsqrt(jnp.mean(hh * hh, axis=-1, keepdims=True) + NORM_EPS)
            og = jax.nn.sigmoid(pf[prow, B_MO + h * ML_DV:B_MO + (h + 1) * ML_DV])
            mix_ref[rows, hs] = (hn * mlg[:, hs] * og).astype(mix_ref.dtype)

        s_c = s_c * SWA_SCALE + tblc[...]
        s_n = s_n * SWA_SCALE + tbln[...]
        mx = jnp.maximum(jnp.maximum(jnp.max(s_c, axis=-1, keepdims=True),
                                     jnp.max(s_n, axis=-1, keepdims=True)), snk)
        p_c = jnp.exp(s_c - mx)
        p_n = jnp.exp(s_n - mx)
        den = (jnp.sum(p_c, axis=-1, keepdims=True) + jnp.sum(p_n, axis=-1, keepdims=True)
               + jnp.exp(snk - mx))

        yield
        vn = jnp.concatenate([vnew.astype(BF16), zeros_pad], axis=0)
        o = _dot(p_n.astype(BF16), vn)
        for bl in range(nseq):
            o = o + _dot(jnp.where(blq == bl, p_c, 0.0).astype(BF16), vc_ref[seq0 + bl].astype(BF16))
        o = o / den
        osel = jnp.where(first_group, o[:, 0:SWA_HD], o[:, SWA_HD:2 * SWA_HD])
        og = jnp.concatenate([osel[hq * SLAB:(hq + 1) * SLAB, :] for hq in range(SWA_HEADS)], axis=1)
        mix_ref[rows, ML_WIDTH:ML_WIDTH + SWA_WIDTH] = og.astype(mix_ref.dtype)

        for bl in range(nseq):
            ko_ref[seq0 + bl, 0:WINDOW - tdec, :] = kc_ref[seq0 + bl, tdec:WINDOW, :]
            vo_ref[seq0 + bl, 0:WINDOW - tdec, :] = vc_ref[seq0 + bl, tdec:WINDOW, :]
            ko_ref[seq0 + bl, WINDOW - tdec:WINDOW, :] = knew[bl * tdec:(bl + 1) * tdec, :]
            vo_ref[seq0 + bl, WINDOW - tdec:WINDOW, :] = vnew[bl * tdec:(bl + 1) * tdec, :]
        yield

    slabs = [slab(si) for si in range(nslab)]
    for _ in range(3):
        for s in slabs:
            next(s)


def _sample_mixer(x2d, wt, wn, gb, mlg, sinks, rb, mrep, c0, n0, kc, vc):
    rows = x2d.shape[0]
    tdec = 4
    nseq = rows // tdec
    gbs = SAMPLE_GB
    steps = nseq // gbs
    rb_rows = gbs * tdec
    const2 = lambda i: (0, 0)
    smem = pl.BlockSpec(memory_space=pltpu.SMEM)
    return pl.pallas_call(
        _sample_mixer_kernel,
        grid=(steps,),
        in_specs=[
            pl.BlockSpec((rows, D_MODEL), const2, pipeline_mode=pl.Buffered(1)),
            pl.BlockSpec((T_COLS, D_MODEL), const2, pipeline_mode=pl.Buffered(1)),
            pl.BlockSpec((D_MODEL, N_B + B_SV), const2, pipeline_mode=pl.Buffered(1)),
            pl.BlockSpec((1, BLK), const2),
            pl.BlockSpec((1, ML_WIDTH), const2),
            smem, smem,
            pl.BlockSpec((rb_rows, BLK), lambda i: (i, 0)),
            pl.BlockSpec((gbs, ML_HEADS, ML_DK, ML_DV), lambda i: (i, 0, 0, 0)),
            pl.BlockSpec((gbs, ML_HEADS, ML_DK), lambda i: (i, 0, 0)),
            pl.BlockSpec((gbs, WINDOW, BLK), lambda i: (i, 0, 0)),
            pl.BlockSpec((gbs, WINDOW, BLK), lambda i: (i, 0, 0)),
        ],
        out_specs=[
            pl.BlockSpec((rb_rows, D_MODEL), lambda i: (i, 0)),
            pl.BlockSpec((gbs, ML_HEADS, ML_DK, ML_DV), lambda i: (i, 0, 0, 0)),
            pl.BlockSpec((gbs, ML_HEADS, ML_DK), lambda i: (i, 0, 0)),
            pl.BlockSpec((rb_rows, BLK), lambda i: (i, 0)),
            pl.BlockSpec((gbs, WINDOW, BLK), lambda i: (i, 0, 0)),
            pl.BlockSpec((gbs, WINDOW, BLK), lambda i: (i, 0, 0)),
        ],
        out_shape=[
            jax.ShapeDtypeStruct((rows, D_MODEL), BF16),
            jax.ShapeDtypeStruct((nseq, ML_HEADS, ML_DK, ML_DV), F32),
            jax.ShapeDtypeStruct((nseq, ML_HEADS, ML_DK), F32),
            jax.ShapeDtypeStruct((rows, BLK), F32),
            jax.ShapeDtypeStruct((nseq, WINDOW, BLK), F32),
            jax.ShapeDtypeStruct((nseq, WINDOW, BLK), F32),
        ],
        scratch_shapes=[
            pltpu.VMEM((rows, A_COLS), F32),
            pltpu.VMEM((rows, B_COLS), F32),
            pltpu.VMEM((SWA_HEADS * SLAB, BLK), F32),
            pltpu.VMEM((SWA_HEADS * SLAB, BLK), F32),
        ],
        compiler_params=pltpu.CompilerParams(
            dimension_semantics=("arbitrary",),
            vmem_limit_bytes=VMEM_LIMIT),
        name="sample_mixer",
    )(x2d, wt, wn, gb, mlg, sinks, rb, mrep, c0, n0, kc, vc)


def _dense_kernel(x_ref, mix_ref, wo_ref, g1_ref, b1_ref, wfi_ref, wfo_ref, g2_ref, b2_ref, y_ref):
    nsub = x_ref.shape[0] // DENSE_SUB
    subs = [slice(i * DENSE_SUB, (i + 1) * DENSE_SUB) for i in range(nsub)]
    chunks = list(zip(FF_SPLITS[:-1], FF_SPLITS[1:]))
    g1, b1, g2, b2 = g1_ref[...], b1_ref[...], g2_ref[...], b2_ref[...]

    proj = [_dot(mix_ref[s, :], wo_ref[...]) for s in subs]
    h1 = [_layer_norm(ALPHA * x_ref[s, :] + proj[i], g1, b1) for i, s in enumerate(subs)]
    h1b = [h.astype(BF16) for h in h1]
    acc = [None] * nsub
    for c0, c1 in chunks:
        gate = [_dot(h1b[i], wfi_ref[:, c0:c1]) for i in range(nsub)]
        up = [_dot(h1b[i], wfi_ref[:, D_FF + c0:D_FF + c1]) for i in range(nsub)]
        for i in range(nsub):
            act = (gate[i] * jax.nn.sigmoid(gate[i]) * up[i]).astype(BF16)
            part = _dot(act, wfo_ref[c0:c1, :])
            acc[i] = part if acc[i] is None else acc[i] + part
    for i, s in enumerate(subs):
        y_ref[s, :] = _layer_norm(ALPHA * h1[i] + acc[i], g2, b2)


def _dense(x2d, mix2d, wo, g1, b1, wfi, wfo, g2, b2):
    rows = x2d.shape[0]
    tm = min(TM_DENSE, rows)
    const2 = lambda i: (0, 0)

    def wspec(shape):
        return pl.BlockSpec(shape, const2, pipeline_mode=pl.Buffered(1))

    return pl.pallas_call(
        _dense_kernel,
        grid=(rows // tm,),
        in_specs=[
            pl.BlockSpec((tm, D_MODEL), lambda i: (i, 0)),
            pl.BlockSpec((tm, D_MODEL), lambda i: (i, 0)),
            wspec((D_MODEL, D_MODEL)),
            wspec((1, D_MODEL)), wspec((1, D_MODEL)),
            wspec((D_MODEL, 2 * D_FF)),
            wspec((D_FF, D_MODEL)),
            wspec((1, D_MODEL)), wspec((1, D_MODEL)),
        ],
        out_specs=pl.BlockSpec((tm, D_MODEL), lambda i: (i, 0)),
        out_shape=jax.ShapeDtypeStruct((rows, D_MODEL), F32),
        compiler_params=pltpu.CompilerParams(
            dimension_semantics=("arbitrary",),
            vmem_limit_bytes=VMEM_LIMIT),
        name="dense",
    )(x2d, mix2d, wo, g1, b1, wfi, wfo, g2, b2)


def kernel(x_prompt, x_sample, state_mlstm_C, state_mlstm_n, state_mlstm_m, cache_swa_k, cache_swa_v,
           w_in, b_igate, b_fgate, ml_norm_g, swa_sinks, rel_bias, w_out, ln1_g, ln1_b,
           w_ffn_in, w_ffn_out, ln2_g, ln2_b):
    bp, seq, _ = x_prompt.shape
    bs, tdec, _ = x_sample.shape
    l = 0

    gvec = jnp.concatenate([b_igate[l], b_fgate[l]]).astype(F32)
    gb = jnp.pad(gvec, (0, BLK - 2 * ML_HEADS))[None, :]
    gbc = jnp.broadcast_to(jnp.pad(gvec, (0, T_GROWS - 2 * ML_HEADS))[:, None], (T_GROWS, BLK))
    mlg = ml_norm_g[l][None, :].astype(F32)
    sinks = swa_sinks[l].astype(F32)
    rb = rel_bias.astype(F32)
    wo = w_out[l].astype(BF16)
    wfi = w_ffn_in[l].astype(BF16)
    wfo = w_ffn_out[l].astype(BF16)
    g1, b1 = ln1_g[l][None, :], ln1_b[l][None, :]
    g2, b2 = ln2_g[l][None, :], ln2_b[l][None, :]

    mix_p, c_p, n_p, m_p, k_p, v_p, wt, wn = _prompt_mixer(x_prompt, w_in[l], gbc, mlg, sinks, rb)
    y_p = _dense(x_prompt.reshape(bp * seq, D_MODEL), mix_p.reshape(bp * seq, D_MODEL),
                 wo, g1, b1, wfi, wfo, g2, b2).reshape(bp, seq, D_MODEL)
    p_m = m_p[:, :ML_HEADS // 2, ::BLK].reshape(bp, ML_HEADS)[None]
    p_k = k_p.reshape(1, bp, WINDOW, SWA_KV_HEADS, SWA_HD)
    p_v = v_p.reshape(1, bp, WINDOW, SWA_KV_HEADS, SWA_HD)

    xs = x_sample.reshape(bs * tdec, D_MODEL)
    m0 = state_mlstm_m[l].astype(F32)
    mrep = jnp.pad(jnp.repeat(m0, tdec, axis=0), ((0, 0), (ML_HEADS, BLK - 2 * ML_HEADS)))
    wlen = cache_swa_k.shape[2]
    kc = cache_swa_k[l].reshape(bs, wlen, SWA_KV_HEADS * SWA_HD)
    vc = cache_swa_v[l].reshape(bs, wlen, SWA_KV_HEADS * SWA_HD)
    mix_s, c_s, n_s, mo_s, k_s, v_s = _sample_mixer(
        xs, wt, wn, gb, mlg, sinks, rb, mrep,
        state_mlstm_C[l].astype(F32), state_mlstm_n[l].astype(F32), kc, vc)
    y_s = _dense(xs, mix_s, wo, g1, b1, wfi, wfo, g2, b2).reshape(bs, tdec, D_MODEL)
    s_m = mo_s.reshape(bs, tdec, BLK)[:, 0, ML_HEADS:2 * ML_HEADS][None]
    s_k = k_s.reshape(1, bs, wlen, SWA_KV_HEADS, SWA_HD)
    s_v = v_s.reshape(1, bs, wlen, SWA_KV_HEADS, SWA_HD)

    return (y_p, y_s, c_p[None], n_p[None], p_m, p_k, p_v,
            c_s[None], n_s[None], s_m, s_k, s_v)
```

```python
import functools
import math

import jax
import jax.numpy as jnp
from jax import lax
from jax.experimental import pallas as pl
from jax.experimental.pallas import tpu as pltpu

F32 = jnp.float32
BF16 = jnp.bfloat16

D_MODEL = 1024
ML_HEADS = 4
ML_DK = 128
ML_DV = 128
ML_WIDTH = ML_HEADS * ML_DV
GATE_SOFTCAP = 15.0
SWA_HEADS = 8
SWA_KV_HEADS = 2
SWA_GROUP = SWA_HEADS // SWA_KV_HEADS
SWA_HD = 64
SWA_WIDTH = SWA_HEADS * SWA_HD
WINDOW = 128
REL_BUCKETS = 32
REL_MAX_DIST = 128
D_FF = 2816
DEPTH = 1
ALPHA = (2.0 * DEPTH) ** 0.25
LN_EPS = 1e-5
NORM_EPS = 1e-6
NEG_INF = -1e30
IN_SIZES = (512, 512, 512, 512, 4, 4, 512, 128, 128)
ML_SCALE = ML_DK ** -0.5
LOG_ML_SCALE = math.log(ML_SCALE)
SWA_SCALE = SWA_HD ** -0.5

BLK = 128

A_Q, A_K, A_V, A_SQ = 0, 512, 1024, 1536
A_COLS = 2048
B_MO, B_SK, B_SV, B_G = 0, 512, 640, 768
B_COLS = 896

N_K = 0
N_B = 512
T_Q, T_V, T_SQ, T_SV, T_G = 0, 512, 1024, 1536, 1664
T_GROWS = 16
T_COLS = T_G + BLK
STATE_ROWS = ML_DV + 16

TM_PROMPT = 1024
TM_DENSE = 1024
DENSE_SUB = 512
FF_SPLITS = (0, 1536, D_FF)
SAMPLE_GB = 16
SLAB = 16
VMEM_LIMIT = 56 * 1024 * 1024


def _softcap(a):
    return GATE_SOFTCAP * jnp.tanh(a / GATE_SOFTCAP)


def _log_sigmoid(x):
    return jnp.minimum(x, 0.0) - jnp.log1p(jnp.exp(-jnp.abs(x)))


def _layer_norm(z, g, b):
    mu = jnp.mean(z, axis=-1, keepdims=True)
    zc = z - mu
    var = jnp.mean(zc * zc, axis=-1, keepdims=True)
    return zc * lax.rsqrt(var + LN_EPS) * g + b


def _dot(a, b):
    return jnp.dot(a, b, preferred_element_type=F32)


def _dot_nt(a, b):
    return lax.dot_general(a, b, (((1,), (1,)), ((), ())), preferred_element_type=F32)


def _dot_tn(a, b):
    return lax.dot_general(a, b, (((0,), (0,)), ((), ())), preferred_element_type=F32)


def _split3(x):
    hi = x.astype(BF16)
    r1 = x - hi.astype(F32)
    mid = r1.astype(BF16)
    lo = (r1 - mid.astype(F32)).astype(BF16)
    return hi, mid, lo


def _t5_bucket(d):
    n = jnp.maximum(d, 0)
    max_exact = REL_BUCKETS // 2
    nlog = REL_BUCKETS - max_exact
    large = jnp.full(n.shape, max_exact, jnp.int32)
    for k in range(1, nlog):
        thr = math.ceil(max_exact * (REL_MAX_DIST / max_exact) ** (k / nlog))
        large = large + jnp.where(n >= thr, 1, 0)
    return jnp.where(n < max_exact, n, large)


def _prompt_mixer_kernel(x_ref, w_ref, gbc_ref, mlg_ref, sinks_ref, rb_ref,
                         mix_ref, c_out, n_out, m_out, ko_ref, vo_ref,
                         wt_out, wn_out, wt, kn, pf, vlast, kband, q_t, v_t, sq_t, sv_t, g_t, ct, msc, tbl, wn):
    b = pl.program_id(0)
    t = pl.program_id(1)
    nt = pl.num_programs(1)
    tm = x_ref.shape[1]
    nblk = tm // BLK
    pairs = SWA_HEADS // 2

    @pl.when((b == 0) & (t == 0))
    def _prepare_weights():
        offs = [0]
        for n in IN_SIZES:
            offs.append(offs[-1] + n)
        o_mq, o_mk, o_mv, o_mo, o_mi, _, o_sq, o_sk, o_sv = offs[:-1]

        def col_blocks(dst_col, src_col, n):
            for i in range(n // BLK):
                wn[:, dst_col + i * BLK:dst_col + (i + 1) * BLK] = (
                    w_ref[:, src_col + i * BLK:src_col + (i + 1) * BLK].astype(BF16))

        def row_blocks(dst_row, src_col, n):
            for i in range(n // BLK):
                wt[dst_row + i * BLK:dst_row + (i + 1) * BLK, :] = (
                    w_ref[:, src_col + i * BLK:src_col + (i + 1) * BLK].T.astype(BF16))

        col_blocks(N_K, o_mk, ML_HEADS * ML_DK)
        col_blocks(N_B + B_MO, o_mo, ML_WIDTH)
        col_blocks(N_B + B_SK, o_sk, SWA_KV_HEADS * SWA_HD)
        row_blocks(T_Q, o_mq, ML_HEADS * ML_DK)
        row_blocks(T_V, o_mv, ML_WIDTH)
        row_blocks(T_SQ, o_sq, SWA_WIDTH)
        row_blocks(T_SV, o_sv, SWA_KV_HEADS * SWA_HD)
        gates_t = w_ref[:, o_mi:o_mi + BLK].T
        grow = lax.broadcasted_iota(jnp.int32, gates_t.shape, 0)
        wt[T_G:T_G + BLK, :] = jnp.where(grow < 2 * ML_HEADS, gates_t, 0.0).astype(BF16)
        wt_out[...] = wt[...]
        wn_out[...] = wn[...]

    @pl.when((b == 0) & (t == 0))
    def _build_bias_tables():
        r = lax.broadcasted_iota(jnp.int32, (BLK, 2 * BLK), 0)
        ln = lax.broadcasted_iota(jnp.int32, (BLK, 2 * BLK), 1)
        second = ln >= BLK
        qi = jnp.where(second, ln - BLK, ln)
        prev = r > qi
        d = jnp.where(prev, WINDOW + qi - r, qi - r)
        bucket = _t5_bucket(d)
        for p in range(pairs):
            acc = jnp.zeros((BLK, 2 * BLK), F32)
            for k in range(REL_BUCKETS):
                acc = jnp.where(bucket == k, jnp.where(second, rb_ref[k, p + pairs], rb_ref[k, p]), acc)
            tbl[0, p] = jnp.where(prev, NEG_INF, acc)
            tbl[1, p] = acc

    @pl.when(t == 0)
    def _reset_state():
        ct[...] = jnp.zeros_like(ct)
        msc[...] = jnp.zeros_like(msc)
        kband[0:BLK, :] = jnp.zeros((BLK, kband.shape[1]), kband.dtype)
        sv_t[0] = jnp.zeros(sv_t.shape[1:], sv_t.dtype)

    xb = x_ref[0].astype(BF16)
    res_n = _dot(xb, wn[...])
    kn[...] = res_n[:, N_K:N_K + ML_HEADS * ML_DK].astype(BF16)
    pf[...] = res_n[:, N_B:N_B + B_SV]
    kband[BLK:BLK + tm, :] = pf[:, B_SK:B_SK + BLK].astype(BF16)

    res_t = _dot_nt(wt[0:T_G + T_GROWS, :], xb)
    vlast[...] = res_t[T_SV:T_SV + BLK, tm - BLK:tm]

    def put(dst, r0, nrows, off, scale=None):
        res = res_t[r0:r0 + nrows, :]
        if scale is not None:
            res = res * scale
        for jj in range(nblk):
            dst[jj + off] = res[:, jj * BLK:(jj + 1) * BLK].astype(dst.dtype)

    put(q_t, T_Q, ML_HEADS * ML_DK, 0)
    put(v_t, T_V, ML_WIDTH, 0)
    put(sq_t, T_SQ, SWA_WIDTH, 0, SWA_SCALE)
    put(sv_t, T_SV, SWA_KV_HEADS * SWA_HD, 1)
    put(g_t, T_G, T_GROWS, 0)

    row = lax.broadcasted_iota(jnp.int32, (BLK, BLK), 0)
    col = lax.broadcasted_iota(jnp.int32, (BLK, BLK), 1)
    causal_t = row <= col
    triu = causal_t.astype(BF16)
    causal2 = jnp.concatenate([causal_t, causal_t], axis=1)
    r16 = lax.broadcasted_iota(jnp.int32, (16, 2 * BLK), 0)
    ones_rows = (r16 == 0).astype(BF16)
    z128 = jnp.zeros((BLK, BLK), BF16)
    ml_pairs = ML_HEADS // 2
    zero_rows = jnp.zeros((BLK - T_GROWS, BLK), F32)
    gbc = gbc_ref[...]
    mlg = mlg_ref[...]
    lane2 = lax.broadcasted_iota(jnp.int32, (1, 2 * BLK), 1)
    snk_rows = [jnp.where(lane2 >= BLK, sinks_ref[p + pairs], sinks_ref[p]) for p in range(pairs)]
    zq = jnp.zeros((SWA_HD, BLK), BF16)
    prev2 = jnp.concatenate([row > col, row > col], axis=1)

    def pair_row(x0, x1):
        return jnp.concatenate([jnp.broadcast_to(x0, (1, BLK)), jnp.broadcast_to(x1, (1, BLK))], axis=1)

    def block_diag(x0, x1):
        return jnp.concatenate([jnp.concatenate([x0, z128], axis=1),
                                jnp.concatenate([z128, x1], axis=1)], axis=0)

    def block(j, carry):
        r0 = pl.multiple_of(j * BLK, BLK)
        rows = pl.ds(r0, BLK)
        hsl = [slice(h * ML_DK, (h + 1) * ML_DK) for h in range(ML_HEADS)]
        psl = [slice(p * 2 * ML_DK, (p + 1) * 2 * ML_DK) for p in range(ml_pairs)]

        pre_t = _softcap(g_t[j] + gbc)
        hi, mid, lo = _split3(_log_sigmoid(pre_t))
        b_t = _dot(hi, triu) + _dot(mid, triu) + _dot(lo, triu)
        u_t = pltpu.roll(pre_t, ML_HEADS, 0) - b_t
        u_c = jnp.concatenate([u_t, zero_rows], axis=0).T

        k2 = [kn[rows, psl[p]] for p in range(ml_pairs)]
        bdq = [block_diag(q_t[j, hsl[2 * p], :], q_t[j, hsl[2 * p + 1], :]) for p in range(ml_pairs)]
        vte = [jnp.concatenate([jnp.concatenate([v_t[j, hsl[2 * p], :], v_t[j, hsl[2 * p + 1], :]], axis=1),
                                ones_rows], axis=0) for p in range(ml_pairs)]
        kq = [_dot(k2[p], bdq[p]) for p in range(ml_pairs)]

        sel = jnp.where((t == 0) & (j == 0), 0, 1)
        kb = kband[pl.ds(r0, 2 * BLK), :]
        vtb = jnp.concatenate([sv_t[j], sv_t[j + 1]], axis=1)
        sc2 = []
        for p in range(pairs):
            qa = sq_t[j, p * SWA_HD:(p + 1) * SWA_HD, :]
            qb = sq_t[j, (p + pairs) * SWA_HD:(p + pairs + 1) * SWA_HD, :]
            bd = jnp.concatenate([jnp.concatenate([qa, zq], axis=1),
                                  jnp.concatenate([zq, qb], axis=1)], axis=0)
            sc2.append(_dot(kb, bd))

        bds, scl, m_row, m_new, a_p, vwte = [], [], [], [], [], []
        for p in range(ml_pairs):
            h0, h1 = 2 * p, 2 * p + 1
            b_row = pair_row(b_t[ML_HEADS + h0:ML_HEADS + h0 + 1, :], b_t[ML_HEADS + h1:ML_HEADS + h1 + 1, :])
            i_row = pair_row(pre_t[h0:h0 + 1, :], pre_t[h1:h1 + 1, :])
            u_col = jnp.concatenate(
                [jnp.broadcast_to(u_c[:, ML_HEADS + h0:ML_HEADS + h0 + 1], (BLK, BLK)),
                 jnp.broadcast_to(u_c[:, ML_HEADS + h1:ML_HEADS + h1 + 1], (BLK, BLK))], axis=1)
            m_prev = msc[p:p + 1, :]
            log_d = jnp.where(causal2, b_row + u_col, -jnp.inf)
            inter = b_row + m_prev
            mr = jnp.maximum(inter, jnp.max(log_d, axis=0, keepdims=True))
            m_row.append(mr)
            scl.append(jnp.exp(inter - mr))
            s2 = (kq[p] * jnp.exp(log_d - (mr - LOG_ML_SCALE))).astype(BF16)
            bds.append(block_diag(s2[:, :BLK], s2[:, BLK:]))
            b_last = pair_row(b_row[:, BLK - 1:BLK], b_row[:, 2 * BLK - 1:2 * BLK])
            w_row = b_last - b_row + i_row
            w_max = pair_row(jnp.max(w_row[:, :BLK], axis=1, keepdims=True),
                             jnp.max(w_row[:, BLK:], axis=1, keepdims=True))
            mn = jnp.maximum(b_last + m_prev, w_max)
            m_new.append(mn)
            a_p.append(jnp.exp(b_last + m_prev - mn))
            wexp = jnp.exp(w_row - mn) * ML_SCALE
            vwte.append(vte[p] * wexp.astype(BF16))

        pw, den_p = [], []
        for p in range(pairs):
            sc = jnp.where(prev2, sc2[p][:BLK], sc2[p][BLK:]) + tbl[sel, p]
            snk = snk_rows[p]
            mx = jnp.maximum(jnp.max(sc, axis=0, keepdims=True), snk)
            pe = jnp.exp(sc - mx)
            den_p.append(jnp.sum(pe, axis=0, keepdims=True) + jnp.exp(snk - mx))
            pw.append(jnp.concatenate([jnp.where(prev2, pe, 0.0), jnp.where(prev2, 0.0, pe)],
                                      axis=0).astype(BF16))

        c_old = [ct[p] for p in range(ml_pairs)]
        cq = [_dot(c_old[p].astype(BF16), bdq[p]) for p in range(ml_pairs)]
        vs = [_dot(vte[p], bds[p]) for p in range(ml_pairs)]
        upd = [_dot(vwte[p], block_diag(k2[p][:, :BLK], k2[p][:, BLK:])) for p in range(ml_pairs)]
        o2 = [_dot(vtb, pw[p]) for p in range(pairs)]

        for p in range(ml_pairs):
            nd = scl[p] * cq[p] + vs[p]
            num = nd[:ML_DV]
            den = nd[ML_DV:ML_DV + 1]
            hh = num / jnp.maximum(jnp.abs(den), jnp.exp(-m_row[p]))
            hn = hh * lax.rsqrt(jnp.mean(hh * hh, axis=0, keepdims=True) + NORM_EPS)
            for half in range(2):
                h = 2 * p + half
                og = jax.nn.sigmoid(pf[rows, B_MO + h * ML_DV:B_MO + (h + 1) * ML_DV])
                hn_h = hn[:, half * BLK:(half + 1) * BLK].T
                mix_ref[0, rows, hsl[h]] = (hn_h * mlg[:, hsl[h]] * og).astype(mix_ref.dtype)
            ct[p] = a_p[p] * c_old[p] + upd[p]
            msc[p:p + 1, :] = m_new[p]

        o_parts = [None] * SWA_HEADS
        for p in range(pairs):
            on = o2[p] / den_p[p]
            o_parts[p] = on[0:SWA_HD, 0:BLK]
            o_parts[p + pairs] = on[SWA_HD:2 * SWA_HD, BLK:2 * BLK]
        o_t = jnp.concatenate(o_parts, axis=0)
        mix_ref[0, rows, ML_WIDTH:ML_WIDTH + SWA_WIDTH] = o_t.T.astype(mix_ref.dtype)
        return carry

    lax.fori_loop(0, nblk, block, 0, unroll=8)

    kband[0:BLK, :] = kband[tm:tm + BLK, :]
    sv_t[0] = sv_t[nblk]

    @pl.when(t == nt - 1)
    def _write_state():
        for h in range(ML_HEADS):
            p, lanes = h // 2, slice((h % 2) * ML_DK, (h % 2 + 1) * ML_DK)
            c_out[0, h] = ct[p, 0:ML_DV, lanes].T
            n_out[0, h:h + 1, :] = ct[p, ML_DV:ML_DV + 1, lanes]
        m_out[0] = msc[...]
        ko_ref[0] = pf[tm - BLK:tm, B_SK:B_SK + BLK]
        vo_ref[0] = vlast[...].T


def _prompt_mixer(x, w_in_l, gbc, mlg, sinks, rb):
    bsz, seq, _ = x.shape
    tm = TM_PROMPT
    nt = seq // tm
    nblk = tm // BLK
    const2 = lambda b, t: (0, 0)
    smem = pl.BlockSpec(memory_space=pltpu.SMEM)
    return pl.pallas_call(
        _prompt_mixer_kernel,
        grid=(bsz, nt),
        in_specs=[
            pl.BlockSpec((1, tm, D_MODEL), lambda b, t: (b, t, 0)),
            pl.BlockSpec((None, D_MODEL, sum(IN_SIZES)), lambda b, t: (0, 0, 0),
                         pipeline_mode=pl.Buffered(1)),
            pl.BlockSpec((T_GROWS, BLK), const2),
            pl.BlockSpec((1, ML_WIDTH), const2),
            smem, smem,
        ],
        out_specs=[
            pl.BlockSpec((1, tm, D_MODEL), lambda b, t: (b, t, 0)),
            pl.BlockSpec((1, ML_HEADS, ML_DK, ML_DV), lambda b, t: (b, 0, 0, 0)),
            pl.BlockSpec((1, ML_HEADS, ML_DK), lambda b, t: (b, 0, 0)),
            pl.BlockSpec((1, 8, 2 * BLK), lambda b, t: (b, 0, 0)),
            pl.BlockSpec((1, BLK, BLK), lambda b, t: (b, 0, 0)),
            pl.BlockSpec((1, BLK, BLK), lambda b, t: (b, 0, 0)),
            pl.BlockSpec((T_COLS, D_MODEL), const2),
            pl.BlockSpec((D_MODEL, N_B + B_SV), const2),
        ],
        out_shape=[
            jax.ShapeDtypeStruct((bsz, seq, D_MODEL), BF16),
            jax.ShapeDtypeStruct((bsz, ML_HEADS, ML_DK, ML_DV), F32),
            jax.ShapeDtypeStruct((bsz, ML_HEADS, ML_DK), F32),
            jax.ShapeDtypeStruct((bsz, 8, 2 * BLK), F32),
            jax.ShapeDtypeStruct((bsz, BLK, BLK), F32),
            jax.ShapeDtypeStruct((bsz, BLK, BLK), F32),
            jax.ShapeDtypeStruct((T_COLS, D_MODEL), BF16),
            jax.ShapeDtypeStruct((D_MODEL, N_B + B_SV), BF16),
        ],
        scratch_shapes=[
            pltpu.VMEM((T_COLS, D_MODEL), BF16),
            pltpu.VMEM((tm, ML_HEADS * ML_DK), BF16),
            pltpu.VMEM((tm, B_SV), F32),
            pltpu.VMEM((BLK, BLK), F32),
            pltpu.VMEM((tm + BLK, BLK), BF16),
            pltpu.VMEM((nblk, ML_HEADS * ML_DK, BLK), BF16),
            pltpu.VMEM((nblk, ML_WIDTH, BLK), BF16),
            pltpu.VMEM((nblk, SWA_WIDTH, BLK), BF16),
            pltpu.VMEM((nblk + 1, SWA_KV_HEADS * SWA_HD, BLK), BF16),
            pltpu.VMEM((nblk, T_GROWS, BLK), F32),
            pltpu.VMEM((ML_HEADS // 2, STATE_ROWS, 2 * ML_DK), F32),
            pltpu.VMEM((8, 2 * BLK), F32),
            pltpu.VMEM((2, SWA_HEADS // 2, BLK, 2 * BLK), F32),
            pltpu.VMEM((D_MODEL, N_B + B_SV), BF16),
        ],
        compiler_params=pltpu.CompilerParams(
            dimension_semantics=("arbitrary", "arbitrary"),
            vmem_limit_bytes=VMEM_LIMIT),
        name="prompt_mixer",
    )(x, w_in_l, gbc, mlg, sinks, rb)


def _roll_rows(x, shift):
    return pltpu.roll(x, shift % x.shape[0], 0)


def _sample_mixer_kernel(x_ref, wt_ref, wn_ref, gb_ref, mlg_ref, sinks_ref, rb_ref, mrep_ref,
                         c_ref, n_ref, kc_ref, vc_ref,
                         mix_ref, c_out, n_out, mo_ref, ko_ref, vo_ref,
                         pa, pf, tblc, tbln):
    step = pl.program_id(0)
    tdec = 4
    rows_step = mix_ref.shape[0]
    nslab = rows_step // SLAB
    nseq = SLAB // tdec
    qrows = SWA_HEADS * SLAB

    @pl.when(step == 0)
    def _project_all_rows():
        xb = x_ref[...].astype(BF16)
        pa[:, A_Q:A_K] = _dot_nt(xb, wt_ref[T_Q:T_Q + ML_HEADS * ML_DK, :])
        pa[:, A_K:A_V] = _dot(xb, wn_ref[:, N_K:N_K + ML_HEADS * ML_DK])
        pa[:, A_V:A_COLS] = _dot_nt(xb, wt_ref[T_V:T_V + ML_WIDTH + SWA_WIDTH, :])
        pf[:, B_MO:B_SV] = _dot(xb, wn_ref[:, N_B:N_B + B_SV])
        pf[:, B_SV:B_COLS] = _dot_nt(xb, wt_ref[T_SV:T_COLS, :])

    @pl.when(step == 0)
    def _build_bias_tables():
        r = lax.broadcasted_iota(jnp.int32, (qrows, BLK), 0)
        c = lax.broadcasted_iota(jnp.int32, (qrows, BLK), 1)
        hq = r // SLAB
        bl = (r % SLAB) // tdec
        tq = r % tdec
        d_cache = WINDOW + tq - c
        d_new = tq - (c % tdec)
        key_cache = _t5_bucket(d_cache) * SWA_HEADS + hq
        key_new = _t5_bucket(d_new) * SWA_HEADS + hq
        acc_c = jnp.zeros((qrows, BLK), F32)
        acc_n = jnp.zeros((qrows, BLK), F32)
        for k in range(REL_BUCKETS):
            for h in range(SWA_HEADS):
                val = rb_ref[k, h]
                acc_c = jnp.where(key_cache == k * SWA_HEADS + h, val, acc_c)
                acc_n = jnp.where(key_new == k * SWA_HEADS + h, val, acc_n)
        tblc[...] = jnp.where(c > tq, acc_c, NEG_INF)
        tbln[...] = jnp.where((c < SLAB) & ((c // tdec) == bl) & (d_new >= 0), acc_n, NEG_INF)

    gbias = gb_ref[...]
    mlg = mlg_ref[...]
    r16 = lax.broadcasted_iota(jnp.int32, (SLAB, BLK), 0)
    rr = r16 % tdec
    bl16 = r16 // tdec
    rq = lax.broadcasted_iota(jnp.int32, (qrows, BLK), 0)
    blq = (rq % SLAB) // tdec
    hq_col = rq[:, 0:1] // SLAB
    snk = jnp.zeros((qrows, 1), F32)
    for h in range(SWA_HEADS):
        snk = jnp.where(hq_col == h, sinks_ref[h], snk)
    first_group = rq[:, 0:SWA_HD] < SWA_GROUP * SLAB
    zq = jnp.zeros((SLAB, SWA_HD), F32)
    zeros_pad = jnp.zeros((BLK - SLAB, BLK), BF16)

    def seg_last(x):
        return jnp.where(rr == 3, x,
                         jnp.where(rr == 2, _roll_rows(x, -1),
                                   jnp.where(rr == 1, _roll_rows(x, -2), _roll_rows(x, -3))))

    def seg_max(x):
        m1 = jnp.maximum(x, jnp.where(rr % 2 == 0, _roll_rows(x, -1), _roll_rows(x, 1)))
        return jnp.maximum(m1, jnp.where(rr < 2, _roll_rows(m1, -2), _roll_rows(m1, 2)))

    def slab(si):
        rows = pl.ds(si * SLAB, SLAB)
        prow = pl.ds(pl.multiple_of(step * rows_step + si * SLAB, SLAB), SLAB)
        seq0 = si * nseq

        pre = _softcap(pf[prow, B_G:B_G + BLK] + gbias)
        logf = _log_sigmoid(pre)
        y = logf + jnp.where(rr >= 1, _roll_rows(logf, 1), 0.0)
        bcum = y + jnp.where(rr >= 2, _roll_rows(y, 2), 0.0)
        ig = pltpu.roll(pre, ML_HEADS, 1)
        m_prev = mrep_ref[rows, :]
        b_minus_i = bcum - ig
        log_d = [jnp.where(rr >= dl, bcum - _roll_rows(b_minus_i, dl), -jnp.inf) for dl in range(tdec)]
        rowmax = functools.reduce(jnp.maximum, log_d)
        inter = bcum + m_prev
        m_row = jnp.maximum(inter, rowmax)
        scl = jnp.exp(inter - m_row)
        dm = [jnp.exp(ld - m_row) for ld in log_d]
        enm = jnp.exp(-m_row)
        b_last = seg_last(bcum)
        w = b_last - bcum + ig
        m_new = jnp.maximum(b_last + m_prev, seg_max(w))
        a = jnp.exp(b_last + m_prev - m_new)
        kws = jnp.exp(w - m_new) * ML_SCALE
        mo_ref[rows, :] = m_new

        q = [pa[prow, A_Q + h * ML_DK:A_Q + (h + 1) * ML_DK] for h in range(ML_HEADS)]
        k = [pa[prow, A_K + h * ML_DK:A_K + (h + 1) * ML_DK] for h in range(ML_HEADS)]
        v = [pa[prow, A_V + h * ML_DV:A_V + (h + 1) * ML_DV] for h in range(ML_HEADS)]
        c_b = [[c_ref[seq0 + bl, h] for bl in range(nseq)] for h in range(ML_HEADS)]
        qc, upd, kw = [], [], []
        for h in range(ML_HEADS):
            lane = ML_HEADS + h
            qb = q[h].astype(BF16)
            res = [_dot(qb, jnp.concatenate([c_b[h][2 * i].astype(BF16), c_b[h][2 * i + 1].astype(BF16)], axis=1))
                   for i in range(nseq // 2)]
            parts = [res[bl // 2][:, (bl % 2) * ML_DV:(bl % 2 + 1) * ML_DV] for bl in range(nseq)]
            qc_h = parts[nseq - 1]
            for bl in range(nseq - 2, -1, -1):
                qc_h = jnp.where(bl16 == bl, parts[bl], qc_h)
            qc.append(qc_h)
            kw_h = k[h] * kws[:, lane:lane + 1]
            kw.append(kw_h)
            vm = jnp.concatenate([jnp.where(bl16 == bl, v[h], 0.0) for bl in range(nseq)], axis=1)
            upd.append(_dot_tn(kw_h.astype(BF16), vm.astype(BF16)))

        knew = pf[prow, B_SK:B_SK + BLK]
        vnew = pf[prow, B_SV:B_SV + BLK]
        qparts = []
        for hq in range(SWA_HEADS):
            qh = pa[prow, A_SQ + hq * SWA_HD:A_SQ + (hq + 1) * SWA_HD]
            qparts.append(jnp.concatenate([qh, zq] if hq < SWA_GROUP else [zq, qh], axis=1))
        qs = jnp.concatenate(qparts, axis=0).astype(BF16)
        s_c = _dot_nt(qs, kc_ref[seq0 + nseq - 1].astype(BF16))
        for bl in range(nseq - 2, -1, -1):
            s_c = jnp.where(blq == bl, _dot_nt(qs, kc_ref[seq0 + bl].astype(BF16)), s_c)
        kn = jnp.concatenate([knew.astype(BF16), zeros_pad], axis=0)
        s_n = _dot_nt(qs, kn)

        yield
        for h in range(ML_HEADS):
            hs = slice(h * ML_DK, (h + 1) * ML_DK)
            lane = ML_HEADS + h
            bm = jnp.zeros((SLAB, ML_DV), F32)
            sum_s = jnp.zeros((SLAB, 1), F32)
            for dl in range(tdec):
                s_dl = jnp.sum(q[h] * _roll_rows(k[h], dl), axis=-1, keepdims=True)
                ws = s_dl * ML_SCALE * dm[dl][:, lane:lane + 1]
                bm = bm + ws * _roll_rows(v[h], dl)
                sum_s = sum_s + ws
            nsel = jnp.zeros((SLAB, ML_DK), F32)
            for bl in range(nseq):
                msk = bl16 == bl
                n_b = n_ref[seq0 + bl][h:h + 1, :]
                nsel = jnp.where(msk, n_b, nsel)
                a_b = a[bl * tdec:bl * tdec + 1, lane:lane + 1]
                c_out[seq0 + bl, h] = a_b * c_b[h][bl] + upd[h][:, bl * ML_DV:(bl + 1) * ML_DV]
                n_out[seq0 + bl, h:h + 1, :] = a_b * n_b + jnp.sum(jnp.where(msk, kw[h], 0.0), axis=0,
                                                                   keepdims=True)
            sclh = scl[:, lane:lane + 1]
            num = sclh * qc[h] + bm
            den = sclh * jnp.sum(q[h] * nsel, axis=-1, keepdims=True) + sum_s
            hh = num / jnp.maximum(jnp.abs(den), enm[:, lane:lane + 1])
            hn = hh * lax.rsqrt(jnp.mean(hh * hh, axis=-1, keepdims=True) + NORM_EPS)
            og = jax.nn.sigmoid(pf[prow, B_MO + h * ML_DV:B_MO + (h + 1) * ML_DV])
            mix_ref[rows, hs] = (hn * mlg[:, hs] * og).astype(mix_ref.dtype)

        s_c = s_c * SWA_SCALE + tblc[...]
        s_n = s_n * SWA_SCALE + tbln[...]
        mx = jnp.maximum(jnp.maximum(jnp.max(s_c, axis=-1, keepdims=True),
                                     jnp.max(s_n, axis=-1, keepdims=True)), snk)
        p_c = jnp.exp(s_c - mx)
        p_n = jnp.exp(s_n - mx)
        den = (jnp.sum(p_c, axis=-1, keepdims=True) + jnp.sum(p_n, axis=-1, keepdims=True)
               + jnp.exp(snk - mx))

        yield
        vn = jnp.concatenate([vnew.astype(BF16), zeros_pad], axis=0)
        o = _dot(p_n.astype(BF16), vn)
        for bl in range(nseq):
            o = o + _dot(jnp.where(blq == bl, p_c, 0.0).astype(BF16), vc_ref[seq0 + bl].astype(BF16))
        o = o / den
        osel = jnp.where(first_group, o[:, 0:SWA_HD], o[:, SWA_HD:2 * SWA_HD])
        og = jnp.concatenate([osel[hq * SLAB:(hq + 1) * SLAB, :] for hq in range(SWA_HEADS)], axis=1)
        mix_ref[rows, ML_WIDTH:ML_WIDTH + SWA_WIDTH] = og.astype(mix_ref.dtype)

        for bl in range(nseq):
            ko_ref[seq0 + bl, 0:WINDOW - tdec, :] = kc_ref[seq0 + bl, tdec:WINDOW, :]
            vo_ref[seq0 + bl, 0:WINDOW - tdec, :] = vc_ref[seq0 + bl, tdec:WINDOW, :]
            ko_ref[seq0 + bl, WINDOW - tdec:WINDOW, :] = knew[bl * tdec:(bl + 1) * tdec, :]
            vo_ref[seq0 + bl, WINDOW - tdec:WINDOW, :] = vnew[bl * tdec:(bl + 1) * tdec, :]
        yield

    slabs = [slab(si) for si in range(nslab)]
    for _ in range(3):
        for s in slabs:
            next(s)


def _sample_mixer(x2d, wt, wn, gb, mlg, sinks, rb, mrep, c0, n0, kc, vc):
    rows = x2d.shape[0]
    tdec = 4
    nseq = rows // tdec
    gbs = SAMPLE_GB
    steps = nseq // gbs
    rb_rows = gbs * tdec
    const2 = lambda i: (0, 0)
    smem = pl.BlockSpec(memory_space=pltpu.SMEM)
    return pl.pallas_call(
        _sample_mixer_kernel,
        grid=(steps,),
        in_specs=[
            pl.BlockSpec((rows, D_MODEL), const2, pipeline_mode=pl.Buffered(1)),
            pl.BlockSpec((T_COLS, D_MODEL), const2, pipeline_mode=pl.Buffered(1)),
            pl.BlockSpec((D_MODEL, N_B + B_SV), const2, pipeline_mode=pl.Buffered(1)),
            pl.BlockSpec((1, BLK), const2),
            pl.BlockSpec((1, ML_WIDTH), const2),
            smem, smem,
            pl.BlockSpec((rb_rows, BLK), lambda i: (i, 0)),
            pl.BlockSpec((gbs, ML_HEADS, ML_DK, ML_DV), lambda i: (i, 0, 0, 0)),
            pl.BlockSpec((gbs, ML_HEADS, ML_DK), lambda i: (i, 0, 0)),
            pl.BlockSpec((gbs, WINDOW, BLK), lambda i: (i, 0, 0)),
            pl.BlockSpec((gbs, WINDOW, BLK), lambda i: (i, 0, 0)),
        ],
        out_specs=[
            pl.BlockSpec((rb_rows, D_MODEL), lambda i: (i, 0)),
            pl.BlockSpec((gbs, ML_HEADS, ML_DK, ML_DV), lambda i: (i, 0, 0, 0)),
            pl.BlockSpec((gbs, ML_HEADS, ML_DK), lambda i: (i, 0, 0)),
            pl.BlockSpec((rb_rows, BLK), lambda i: (i, 0)),
            pl.BlockSpec((gbs, WINDOW, BLK), lambda i: (i, 0, 0)),
            pl.BlockSpec((gbs, WINDOW, BLK), lambda i: (i, 0, 0)),
        ],
        out_shape=[
            jax.ShapeDtypeStruct((rows, D_MODEL), BF16),
            jax.ShapeDtypeStruct((nseq, ML_HEADS, ML_DK, ML_DV), F32),
            jax.ShapeDtypeStruct((nseq, ML_HEADS, ML_DK), F32),
            jax.ShapeDtypeStruct((rows, BLK), F32),
            jax.ShapeDtypeStruct((nseq, WINDOW, BLK), F32),
            jax.ShapeDtypeStruct((nseq, WINDOW, BLK), F32),
        ],
        scratch_shapes=[
            pltpu.VMEM((rows, A_COLS), F32),
            pltpu.VMEM((rows, B_COLS), F32),
            pltpu.VMEM((SWA_HEADS * SLAB, BLK), F32),
            pltpu.VMEM((SWA_HEADS * SLAB, BLK), F32),
        ],
        compiler_params=pltpu.CompilerParams(
            dimension_semantics=("arbitrary",),
            vmem_limit_bytes=VMEM_LIMIT),
        name="sample_mixer",
    )(x2d, wt, wn, gb, mlg, sinks, rb, mrep, c0, n0, kc, vc)


def _dense_kernel(x_ref, mix_ref, wo_ref, g1_ref, b1_ref, wfi_ref, wfo_ref, g2_ref, b2_ref, y_ref):
    nsub = x_ref.shape[0] // DENSE_SUB
    subs = [slice(i * DENSE_SUB, (i + 1) * DENSE_SUB) for i in range(nsub)]
    chunks = list(zip(FF_SPLITS[:-1], FF_SPLITS[1:]))
    g1, b1, g2, b2 = g1_ref[...], b1_ref[...], g2_ref[...], b2_ref[...]

    proj = [_dot(mix_ref[s, :], wo_ref[...]) for s in subs]
    h1 = [_layer_norm(ALPHA * x_ref[s, :] + proj[i], g1, b1) for i, s in enumerate(subs)]
    h1b = [h.astype(BF16) for h in h1]
    acc = [None] * nsub
    for c0, c1 in chunks:
        gate = [_dot(h1b[i], wfi_ref[:, c0:c1]) for i in range(nsub)]
        up = [_dot(h1b[i], wfi_ref[:, D_FF + c0:D_FF + c1]) for i in range(nsub)]
        for i in range(nsub):
            act = (gate[i] * jax.nn.sigmoid(gate[i]) * up[i]).astype(BF16)
            part = _dot(act, wfo_ref[c0:c1, :])
            acc[i] = part if acc[i] is None else acc[i] + part
    for i, s in enumerate(subs):
        y_ref[s, :] = _layer_norm(ALPHA * h1[i] + acc[i], g2, b2)


def _dense(x2d, mix2d, wo, g1, b1, wfi, wfo, g2, b2):
    rows = x2d.shape[0]
    tm = min(TM_DENSE, rows)
    const2 = lambda i: (0, 0)

    def wspec(shape):
        return pl.BlockSpec(shape, const2, pipeline_mode=pl.Buffered(1))

    return pl.pallas_call(
        _dense_kernel,
        grid=(rows // tm,),
        in_specs=[
            pl.BlockSpec((tm, D_MODEL), lambda i: (i, 0)),
            pl.BlockSpec((tm, D_MODEL), lambda i: (i, 0)),
            wspec((D_MODEL, D_MODEL)),
            wspec((1, D_MODEL)), wspec((1, D_MODEL)),
            wspec((D_MODEL, 2 * D_FF)),
            wspec((D_FF, D_MODEL)),
            wspec((1, D_MODEL)), wspec((1, D_MODEL)),
        ],
        out_specs=pl.BlockSpec((tm, D_MODEL), lambda i: (i, 0)),
        out_shape=jax.ShapeDtypeStruct((rows, D_MODEL), F32),
        compiler_params=pltpu.CompilerParams(
            dimension_semantics=("arbitrary",),
            vmem_limit_bytes=VMEM_LIMIT),
        name="dense",
    )(x2d, mix2d, wo, g1, b1, wfi, wfo, g2, b2)


def kernel(x_prompt, x_sample, state_mlstm_C, state_mlstm_n, state_mlstm_m, cache_swa_k, cache_swa_v,
           w_in, b_igate, b_fgate, ml_norm_g, swa_sinks, rel_bias, w_out, ln1_g, ln1_b,
           w_ffn_in, w_ffn_out, ln2_g, ln2_b):
    bp, seq, _ = x_prompt.shape
    bs, tdec, _ = x_sample.shape
    l = 0

    gvec = jnp.concatenate([b_igate[l], b_fgate[l]]).astype(F32)
    gb = jnp.pad(gvec, (0, BLK - 2 * ML_HEADS))[None, :]
    gbc = jnp.broadcast_to(jnp.pad(gvec, (0, T_GROWS - 2 * ML_HEADS))[:, None], (T_GROWS, BLK))
    mlg = ml_norm_g[l][None, :].astype(F32)
    sinks = swa_sinks[l].astype(F32)
    rb = rel_bias.astype(F32)
    wo = w_out[l].astype(BF16)
    wfi = w_ffn_in[l].astype(BF16)
    wfo = w_ffn_out[l].astype(BF16)
    g1, b1 = ln1_g[l][None, :], ln1_b[l][None, :]
    g2, b2 = ln2_g[l][None, :], ln2_b[l][None, :]

    mix_p, c_p, n_p, m_p, k_p, v_p, wt, wn = _prompt_mixer(x_prompt, w_in, gbc, mlg, sinks, rb)
    y_p = _dense(x_prompt.reshape(bp * seq, D_MODEL), mix_p.reshape(bp * seq, D_MODEL),
                 wo, g1, b1, wfi, wfo, g2, b2).reshape(bp, seq, D_MODEL)
    p_m = m_p[:, :ML_HEADS // 2, ::BLK].reshape(bp, ML_HEADS)[None]
    p_k = k_p.reshape(1, bp, WINDOW, SWA_KV_HEADS, SWA_HD)
    p_v = v_p.reshape(1, bp, WINDOW, SWA_KV_HEADS, SWA_HD)

    xs = x_sample.reshape(bs * tdec, D_MODEL)
    m0 = state_mlstm_m[l].astype(F32)
    mrep = jnp.pad(jnp.repeat(m0, tdec, axis=0), ((0, 0), (ML_HEADS, BLK - 2 * ML_HEADS)))
    wlen = cache_swa_k.shape[2]
    kc = cache_swa_k[l].reshape(bs, wlen, SWA_KV_HEADS * SWA_HD)
    vc = cache_swa_v[l].reshape(bs, wlen, SWA_KV_HEADS * SWA_HD)
    mix_s, c_s, n_s, mo_s, k_s, v_s = _sample_mixer(
        xs, wt, wn, gb, mlg, sinks, rb, mrep,
        state_mlstm_C[l].astype(F32), state_mlstm_n[l].astype(F32), kc, vc)
    y_s = _dense(xs, mix_s, wo, g1, b1, wfi, wfo, g2, b2).reshape(bs, tdec, D_MODEL)
    s_m = mo_s.reshape(bs, tdec, BLK)[:, 0, ML_HEADS:2 * ML_HEADS][None]
    s_k = k_s.reshape(1, bs, wlen, SWA_KV_HEADS, SWA_HD)
    s_v = v_s.reshape(1, bs, wlen, SWA_KV_HEADS, SWA_HD)

    return (y_p, y_s, c_p[None], n_p[None], p_m, p_k, p_v,
            c_s[None], n_s[None], s_m, s_k, s_v)
```

```python
import functools
import math

import jax
import jax.numpy as jnp
from jax import lax
from jax.experimental import pallas as pl
from jax.experimental.pallas import tpu as pltpu

F32 = jnp.float32
BF16 = jnp.bfloat16

D_MODEL = 1024
ML_HEADS = 4
ML_DK = 128
ML_DV = 128
ML_WIDTH = ML_HEADS * ML_DV
GATE_SOFTCAP = 15.0
SWA_HEADS = 8
SWA_KV_HEADS = 2
SWA_GROUP = SWA_HEADS // SWA_KV_HEADS
SWA_HD = 64
SWA_WIDTH = SWA_HEADS * SWA_HD
WINDOW = 128
REL_BUCKETS = 32
REL_MAX_DIST = 128
D_FF = 2816
DEPTH = 1
ALPHA = (2.0 * DEPTH) ** 0.25
LN_EPS = 1e-5
NORM_EPS = 1e-6
NEG_INF = -1e30
IN_SIZES = (512, 512, 512, 512, 4, 4, 512, 128, 128)
ML_SCALE = ML_DK ** -0.5
LOG_ML_SCALE = math.log(ML_SCALE)
SWA_SCALE = SWA_HD ** -0.5

BLK = 128

A_Q, A_K, A_V, A_SQ = 0, 512, 1024, 1536
A_COLS = 2048
B_MO, B_SK, B_SV, B_G = 0, 512, 640, 768
B_COLS = 896

N_K = 0
N_B = 512
T_Q, T_V, T_SQ, T_SV, T_G = 0, 512, 1024, 1536, 1664
T_GROWS = 16
T_COLS = T_G + BLK
W_IN_PAD = -(-sum(IN_SIZES) // BLK) * BLK
STATE_ROWS = ML_DV + 16

TM_PROMPT = 1024
TM_DENSE = 1024
DENSE_SUB = 512
FF_SPLITS = (0, 1536, D_FF)
SAMPLE_GB = 16
SLAB = 16
VMEM_LIMIT = 56 * 1024 * 1024


def _softcap(a):
    return GATE_SOFTCAP * jnp.tanh(a / GATE_SOFTCAP)


def _log_sigmoid(x):
    return jnp.minimum(x, 0.0) - jnp.log1p(jnp.exp(-jnp.abs(x)))


def _layer_norm(z, g, b):
    mu = jnp.mean(z, axis=-1, keepdims=True)
    zc = z - mu
    var = jnp.mean(zc * zc, axis=-1, keepdims=True)
    return zc * lax.rsqrt(var + LN_EPS) * g + b


def _dot(a, b):
    return jnp.dot(a, b, preferred_element_type=F32)


def _dot_nt(a, b):
    return lax.dot_general(a, b, (((1,), (1,)), ((), ())), preferred_element_type=F32)


def _dot_tn(a, b):
    return lax.dot_general(a, b, (((0,), (0,)), ((), ())), preferred_element_type=F32)


def _split3(x):
    hi = x.astype(BF16)
    r1 = x - hi.astype(F32)
    mid = r1.astype(BF16)
    lo = (r1 - mid.astype(F32)).astype(BF16)
    return hi, mid, lo


def _t5_bucket(d):
    n = jnp.maximum(d, 0)
    max_exact = REL_BUCKETS // 2
    nlog = REL_BUCKETS - max_exact
    large = jnp.full(n.shape, max_exact, jnp.int32)
    for k in range(1, nlog):
        thr = math.ceil(max_exact * (REL_MAX_DIST / max_exact) ** (k / nlog))
        large = large + jnp.where(n >= thr, 1, 0)
    return jnp.where(n < max_exact, n, large)


def _prompt_mixer_kernel(x_ref, w_ref, gbc_ref, mlg_ref, sinks_ref, rb_ref,
                         mix_ref, c_out, n_out, m_out, ko_ref, vo_ref,
                         wt_out, wn_out, wt, kn, pf, vlast, kband, q_t, v_t, sq_t, sv_t, g_t, ct, msc, tbl, wn):
    b = pl.program_id(0)
    t = pl.program_id(1)
    nt = pl.num_programs(1)
    tm = x_ref.shape[1]
    nblk = tm // BLK
    pairs = SWA_HEADS // 2

    @pl.when((b == 0) & (t == 0))
    def _prepare_weights():
        offs = [0]
        for n in IN_SIZES:
            offs.append(offs[-1] + n)
        o_mq, o_mk, o_mv, o_mo, o_mi, _, o_sq, o_sk, o_sv = offs[:-1]

        def col_blocks(dst_col, src_col, n):
            for i in range(n // BLK):
                wn[:, dst_col + i * BLK:dst_col + (i + 1) * BLK] = (
                    w_ref[:, src_col + i * BLK:src_col + (i + 1) * BLK])

        def row_blocks(dst_row, src_col, n):
            for i in range(n // BLK):
                wt[dst_row + i * BLK:dst_row + (i + 1) * BLK, :] = (
                    w_ref[:, src_col + i * BLK:src_col + (i + 1) * BLK].T)

        col_blocks(N_K, o_mk, ML_HEADS * ML_DK)
        col_blocks(N_B + B_MO, o_mo, ML_WIDTH)
        col_blocks(N_B + B_SK, o_sk, SWA_KV_HEADS * SWA_HD)
        row_blocks(T_Q, o_mq, ML_HEADS * ML_DK)
        row_blocks(T_V, o_mv, ML_WIDTH)
        row_blocks(T_SQ, o_sq, SWA_WIDTH)
        row_blocks(T_SV, o_sv, SWA_KV_HEADS * SWA_HD)
        gates_t = w_ref[:, o_mi:o_mi + BLK].T
        grow = lax.broadcasted_iota(jnp.int32, gates_t.shape, 0)
        wt[T_G:T_G + BLK, :] = jnp.where(grow < 2 * ML_HEADS, gates_t, jnp.zeros_like(gates_t))
        wt_out[...] = wt[...]
        wn_out[...] = wn[...]

    @pl.when((b == 0) & (t == 0))
    def _build_bias_tables():
        r = lax.broadcasted_iota(jnp.int32, (BLK, 2 * BLK), 0)
        ln = lax.broadcasted_iota(jnp.int32, (BLK, 2 * BLK), 1)
        second = ln >= BLK
        qi = jnp.where(second, ln - BLK, ln)
        prev = r > qi
        d = jnp.where(prev, WINDOW + qi - r, qi - r)
        bucket = _t5_bucket(d)
        for p in range(pairs):
            acc = jnp.zeros((BLK, 2 * BLK), F32)
            for k in range(REL_BUCKETS):
                acc = jnp.where(bucket == k, jnp.where(second, rb_ref[k, p + pairs], rb_ref[k, p]), acc)
            tbl[0, p] = jnp.where(prev, NEG_INF, acc)
            tbl[1, p] = acc

    @pl.when(t == 0)
    def _reset_state():
        ct[...] = jnp.zeros_like(ct)
        msc[...] = jnp.zeros_like(msc)
        kband[0:BLK, :] = jnp.zeros((BLK, kband.shape[1]), kband.dtype)
        sv_t[0] = jnp.zeros(sv_t.shape[1:], sv_t.dtype)

    xb = x_ref[0].astype(BF16)
    res_n = _dot(xb, wn[...])
    kn[...] = res_n[:, N_K:N_K + ML_HEADS * ML_DK].astype(BF16)
    pf[...] = res_n[:, N_B:N_B + B_SV]
    kband[BLK:BLK + tm, :] = pf[:, B_SK:B_SK + BLK].astype(BF16)

    res_t = _dot_nt(wt[0:T_G + T_GROWS, :], xb)
    vlast[...] = res_t[T_SV:T_SV + BLK, tm - BLK:tm]

    def put(dst, r0, nrows, off, scale=None):
        res = res_t[r0:r0 + nrows, :]
        if scale is not None:
            res = res * scale
        for jj in range(nblk):
            dst[jj + off] = res[:, jj * BLK:(jj + 1) * BLK].astype(dst.dtype)

    put(q_t, T_Q, ML_HEADS * ML_DK, 0)
    put(v_t, T_V, ML_WIDTH, 0)
    put(sq_t, T_SQ, SWA_WIDTH, 0, SWA_SCALE)
    put(sv_t, T_SV, SWA_KV_HEADS * SWA_HD, 1)
    put(g_t, T_G, T_GROWS, 0)

    row = lax.broadcasted_iota(jnp.int32, (BLK, BLK), 0)
    col = lax.broadcasted_iota(jnp.int32, (BLK, BLK), 1)
    causal_t = row <= col
    triu = causal_t.astype(BF16)
    causal2 = jnp.concatenate([causal_t, causal_t], axis=1)
    r16 = lax.broadcasted_iota(jnp.int32, (16, 2 * BLK), 0)
    ones_rows = (r16 == 0).astype(BF16)
    z128 = jnp.zeros((BLK, BLK), BF16)
    ml_pairs = ML_HEADS // 2
    zero_rows = jnp.zeros((BLK - T_GROWS, BLK), F32)
    gbc = gbc_ref[...]
    mlg = mlg_ref[...]
    lane2 = lax.broadcasted_iota(jnp.int32, (1, 2 * BLK), 1)
    snk_rows = [jnp.where(lane2 >= BLK, sinks_ref[p + pairs], sinks_ref[p]) for p in range(pairs)]
    zq = jnp.zeros((SWA_HD, BLK), BF16)
    prev2 = jnp.concatenate([row > col, row > col], axis=1)

    def pair_row(x0, x1):
        return jnp.concatenate([jnp.broadcast_to(x0, (1, BLK)), jnp.broadcast_to(x1, (1, BLK))], axis=1)

    def block_diag(x0, x1):
        return jnp.concatenate([jnp.concatenate([x0, z128], axis=1),
                                jnp.concatenate([z128, x1], axis=1)], axis=0)

    def block(j, carry):
        r0 = pl.multiple_of(j * BLK, BLK)
        rows = pl.ds(r0, BLK)
        hsl = [slice(h * ML_DK, (h + 1) * ML_DK) for h in range(ML_HEADS)]
        psl = [slice(p * 2 * ML_DK, (p + 1) * 2 * ML_DK) for p in range(ml_pairs)]

        pre_t = _softcap(g_t[j] + gbc)
        hi, mid, lo = _split3(_log_sigmoid(pre_t))
        b_t = _dot(hi, triu) + _dot(mid, triu) + _dot(lo, triu)
        u_t = pltpu.roll(pre_t, ML_HEADS, 0) - b_t
        u_c = jnp.concatenate([u_t, zero_rows], axis=0).T

        k2 = [kn[rows, psl[p]] for p in range(ml_pairs)]
        bdq = [block_diag(q_t[j, hsl[2 * p], :], q_t[j, hsl[2 * p + 1], :]) for p in range(ml_pairs)]
        vte = [jnp.concatenate([jnp.concatenate([v_t[j, hsl[2 * p], :], v_t[j, hsl[2 * p + 1], :]], axis=1),
                                ones_rows], axis=0) for p in range(ml_pairs)]
        kq = [_dot(k2[p], bdq[p]) for p in range(ml_pairs)]

        sel = jnp.where((t == 0) & (j == 0), 0, 1)
        kb = kband[pl.ds(r0, 2 * BLK), :]
        vtb = jnp.concatenate([sv_t[j], sv_t[j + 1]], axis=1)
        sc2 = []
        for p in range(pairs):
            qa = sq_t[j, p * SWA_HD:(p + 1) * SWA_HD, :]
            qb = sq_t[j, (p + pairs) * SWA_HD:(p + pairs + 1) * SWA_HD, :]
            bd = jnp.concatenate([jnp.concatenate([qa, zq], axis=1),
                                  jnp.concatenate([zq, qb], axis=1)], axis=0)
            sc2.append(_dot(kb, bd))

        bds, scl, m_row, m_new, a_p, vwte = [], [], [], [], [], []
        for p in range(ml_pairs):
            h0, h1 = 2 * p, 2 * p + 1
            b_row = pair_row(b_t[ML_HEADS + h0:ML_HEADS + h0 + 1, :], b_t[ML_HEADS + h1:ML_HEADS + h1 + 1, :])
            i_row = pair_row(pre_t[h0:h0 + 1, :], pre_t[h1:h1 + 1, :])
            u_col = jnp.concatenate(
                [jnp.broadcast_to(u_c[:, ML_HEADS + h0:ML_HEADS + h0 + 1], (BLK, BLK)),
                 jnp.broadcast_to(u_c[:, ML_HEADS + h1:ML_HEADS + h1 + 1], (BLK, BLK))], axis=1)
            m_prev = msc[p:p + 1, :]
            log_d = jnp.where(causal2, b_row + u_col, -jnp.inf)
            inter = b_row + m_prev
            mr = jnp.maximum(inter, jnp.max(log_d, axis=0, keepdims=True))
            m_row.append(mr)
            scl.append(jnp.exp(inter - mr))
            s2 = (kq[p] * jnp.exp(log_d - (mr - LOG_ML_SCALE))).astype(BF16)
            bds.append(block_diag(s2[:, :BLK], s2[:, BLK:]))
            b_last = pair_row(b_row[:, BLK - 1:BLK], b_row[:, 2 * BLK - 1:2 * BLK])
            w_row = b_last - b_row + i_row
            w_max = pair_row(jnp.max(w_row[:, :BLK], axis=1, keepdims=True),
                             jnp.max(w_row[:, BLK:], axis=1, keepdims=True))
            mn = jnp.maximum(b_last + m_prev, w_max)
            m_new.append(mn)
            a_p.append(jnp.exp(b_last + m_prev - mn))
            wexp = jnp.exp(w_row - mn) * ML_SCALE
            vwte.append(vte[p] * wexp.astype(BF16))

        pw, den_p = [], []
        for p in range(pairs):
            sc = jnp.where(prev2, sc2[p][:BLK], sc2[p][BLK:]) + tbl[sel, p]
            snk = snk_rows[p]
            mx = jnp.maximum(jnp.max(sc, axis=0, keepdims=True), snk)
            pe = jnp.exp(sc - mx)
            den_p.append(jnp.sum(pe, axis=0, keepdims=True) + jnp.exp(snk - mx))
            pw.append(jnp.concatenate([jnp.where(prev2, pe, 0.0), jnp.where(prev2, 0.0, pe)],
                                      axis=0).astype(BF16))

        c_old = [ct[p] for p in range(ml_pairs)]
        cq = [_dot(c_old[p].astype(BF16), bdq[p]) for p in range(ml_pairs)]
        vs = [_dot(vte[p], bds[p]) for p in range(ml_pairs)]
        upd = [_dot(vwte[p], block_diag(k2[p][:, :BLK], k2[p][:, BLK:])) for p in range(ml_pairs)]
        o2 = [_dot(vtb, pw[p]) for p in range(pairs)]

        for p in range(ml_pairs):
            nd = scl[p] * cq[p] + vs[p]
            num = nd[:ML_DV]
            den = nd[ML_DV:ML_DV + 1]
            hh = num / jnp.maximum(jnp.abs(den), jnp.exp(-m_row[p]))
            hn = hh * lax.rsqrt(jnp.mean(hh * hh, axis=0, keepdims=True) + NORM_EPS)
            for half in range(2):
                h = 2 * p + half
                og = jax.nn.sigmoid(pf[rows, B_MO + h * ML_DV:B_MO + (h + 1) * ML_DV])
                hn_h = hn[:, half * BLK:(half + 1) * BLK].T
                mix_ref[0, rows, hsl[h]] = (hn_h * mlg[:, hsl[h]] * og).astype(mix_ref.dtype)
            ct[p] = a_p[p] * c_old[p] + upd[p]
            msc[p:p + 1, :] = m_new[p]

        o_parts = [None] * SWA_HEADS
        for p in range(pairs):
            on = o2[p] / den_p[p]
            o_parts[p] = on[0:SWA_HD, 0:BLK]
            o_parts[p + pairs] = on[SWA_HD:2 * SWA_HD, BLK:2 * BLK]
        o_t = jnp.concatenate(o_parts, axis=0)
        mix_ref[0, rows, ML_WIDTH:ML_WIDTH + SWA_WIDTH] = o_t.T.astype(mix_ref.dtype)
        return carry

    lax.fori_loop(0, nblk, block, 0, unroll=8)

    kband[0:BLK, :] = kband[tm:tm + BLK, :]
    sv_t[0] = sv_t[nblk]

    @pl.when(t == nt - 1)
    def _write_state():
        for h in range(ML_HEADS):
            p, lanes = h // 2, slice((h % 2) * ML_DK, (h % 2 + 1) * ML_DK)
            c_out[0, h] = ct[p, 0:ML_DV, lanes].T
            n_out[0, h:h + 1, :] = ct[p, ML_DV:ML_DV + 1, lanes]
        m_out[0] = msc[...]
        ko_ref[0] = pf[tm - BLK:tm, B_SK:B_SK + BLK]
        vo_ref[0] = vlast[...].T


def _prompt_mixer(x, w_in_l, gbc, mlg, sinks, rb):
    bsz, seq, _ = x.shape
    tm = TM_PROMPT
    nt = seq // tm
    nblk = tm // BLK
    const2 = lambda b, t: (0, 0)
    smem = pl.BlockSpec(memory_space=pltpu.SMEM)
    return pl.pallas_call(
        _prompt_mixer_kernel,
        grid=(bsz, nt),
        in_specs=[
            pl.BlockSpec((1, tm, D_MODEL), lambda b, t: (b, t, 0)),
            pl.BlockSpec((D_MODEL, W_IN_PAD), const2, pipeline_mode=pl.Buffered(1)),
            pl.BlockSpec((T_GROWS, BLK), const2),
            pl.BlockSpec((1, ML_WIDTH), const2),
            smem, smem,
        ],
        out_specs=[
            pl.BlockSpec((1, tm, D_MODEL), lambda b, t: (b, t, 0)),
            pl.BlockSpec((1, ML_HEADS, ML_DK, ML_DV), lambda b, t: (b, 0, 0, 0)),
            pl.BlockSpec((1, ML_HEADS, ML_DK), lambda b, t: (b, 0, 0)),
            pl.BlockSpec((1, 8, 2 * BLK), lambda b, t: (b, 0, 0)),
            pl.BlockSpec((1, BLK, BLK), lambda b, t: (b, 0, 0)),
            pl.BlockSpec((1, BLK, BLK), lambda b, t: (b, 0, 0)),
            pl.BlockSpec((T_COLS, D_MODEL), const2),
            pl.BlockSpec((D_MODEL, N_B + B_SV), const2),
        ],
        out_shape=[
            jax.ShapeDtypeStruct((bsz, seq, D_MODEL), BF16),
            jax.ShapeDtypeStruct((bsz, ML_HEADS, ML_DK, ML_DV), F32),
            jax.ShapeDtypeStruct((bsz, ML_HEADS, ML_DK), F32),
            jax.ShapeDtypeStruct((bsz, 8, 2 * BLK), F32),
            jax.ShapeDtypeStruct((bsz, BLK, BLK), F32),
            jax.ShapeDtypeStruct((bsz, BLK, BLK), F32),
            jax.ShapeDtypeStruct((T_COLS, D_MODEL), BF16),
            jax.ShapeDtypeStruct((D_MODEL, N_B + B_SV), BF16),
        ],
        scratch_shapes=[
            pltpu.VMEM((T_COLS, D_MODEL), BF16),
            pltpu.VMEM((tm, ML_HEADS * ML_DK), BF16),
            pltpu.VMEM((tm, B_SV), F32),
            pltpu.VMEM((BLK, BLK), F32),
            pltpu.VMEM((tm + BLK, BLK), BF16),
            pltpu.VMEM((nblk, ML_HEADS * ML_DK, BLK), BF16),
            pltpu.VMEM((nblk, ML_WIDTH, BLK), BF16),
            pltpu.VMEM((nblk, SWA_WIDTH, BLK), BF16),
            pltpu.VMEM((nblk + 1, SWA_KV_HEADS * SWA_HD, BLK), BF16),
            pltpu.VMEM((nblk, T_GROWS, BLK), F32),
            pltpu.VMEM((ML_HEADS // 2, STATE_ROWS, 2 * ML_DK), F32),
            pltpu.VMEM((8, 2 * BLK), F32),
            pltpu.VMEM((2, SWA_HEADS // 2, BLK, 2 * BLK), F32),
            pltpu.VMEM((D_MODEL, N_B + B_SV), BF16),
        ],
        compiler_params=pltpu.CompilerParams(
            dimension_semantics=("arbitrary", "arbitrary"),
            vmem_limit_bytes=VMEM_LIMIT),
        name="prompt_mixer",
    )(x, w_in_l, gbc, mlg, sinks, rb)


def _roll_rows(x, shift):
    return pltpu.roll(x, shift % x.shape[0], 0)


def _sample_mixer_kernel(x_ref, wt_ref, wn_ref, gb_ref, mlg_ref, sinks_ref, rb_ref, mrep_ref,
                         c_ref, n_ref, kc_ref, vc_ref,
                         mix_ref, c_out, n_out, mo_ref, ko_ref, vo_ref,
                         pa, pf, tblc, tbln):
    step = pl.program_id(0)
    tdec = 4
    rows_step = mix_ref.shape[0]
    nslab = rows_step // SLAB
    nseq = SLAB // tdec
    qrows = SWA_HEADS * SLAB

    @pl.when(step == 0)
    def _project_all_rows():
        xb = x_ref[...].astype(BF16)
        pa[:, A_Q:A_K] = _dot_nt(xb, wt_ref[T_Q:T_Q + ML_HEADS * ML_DK, :])
        pa[:, A_K:A_V] = _dot(xb, wn_ref[:, N_K:N_K + ML_HEADS * ML_DK])
        pa[:, A_V:A_COLS] = _dot_nt(xb, wt_ref[T_V:T_V + ML_WIDTH + SWA_WIDTH, :])
        pf[:, B_MO:B_SV] = _dot(xb, wn_ref[:, N_B:N_B + B_SV])
        pf[:, B_SV:B_COLS] = _dot_nt(xb, wt_ref[T_SV:T_COLS, :])

    @pl.when(step == 0)
    def _build_bias_tables():
        r = lax.broadcasted_iota(jnp.int32, (qrows, BLK), 0)
        c = lax.broadcasted_iota(jnp.int32, (qrows, BLK), 1)
        hq = r // SLAB
        bl = (r % SLAB) // tdec
        tq = r % tdec
        d_cache = WINDOW + tq - c
        d_new = tq - (c % tdec)
        key_cache = _t5_bucket(d_cache) * SWA_HEADS + hq
        key_new = _t5_bucket(d_new) * SWA_HEADS + hq
        acc_c = jnp.zeros((qrows, BLK), F32)
        acc_n = jnp.zeros((qrows, BLK), F32)
        for k in range(REL_BUCKETS):
            for h in range(SWA_HEADS):
                val = rb_ref[k, h]
                acc_c = jnp.where(key_cache == k * SWA_HEADS + h, val, acc_c)
                acc_n = jnp.where(key_new == k * SWA_HEADS + h, val, acc_n)
        tblc[...] = jnp.where(c > tq, acc_c, NEG_INF)
        tbln[...] = jnp.where((c < SLAB) & ((c // tdec) == bl) & (d_new >= 0), acc_n, NEG_INF)

    gbias = gb_ref[...]
    mlg = mlg_ref[...]
    r16 = lax.broadcasted_iota(jnp.int32, (SLAB, BLK), 0)
    rr = r16 % tdec
    bl16 = r16 // tdec
    rq = lax.broadcasted_iota(jnp.int32, (qrows, BLK), 0)
    blq = (rq % SLAB) // tdec
    hq_col = rq[:, 0:1] // SLAB
    snk = jnp.zeros((qrows, 1), F32)
    for h in range(SWA_HEADS):
        snk = jnp.where(hq_col == h, sinks_ref[h], snk)
    first_group = rq[:, 0:SWA_HD] < SWA_GROUP * SLAB
    zq = jnp.zeros((SLAB, SWA_HD), F32)
    zeros_pad = jnp.zeros((BLK - SLAB, BLK), BF16)

    def seg_last(x):
        return jnp.where(rr == 3, x,
                         jnp.where(rr == 2, _roll_rows(x, -1),
                                   jnp.where(rr == 1, _roll_rows(x, -2), _roll_rows(x, -3))))

    def seg_max(x):
        m1 = jnp.maximum(x, jnp.where(rr % 2 == 0, _roll_rows(x, -1), _roll_rows(x, 1)))
        return jnp.maximum(m1, jnp.where(rr < 2, _roll_rows(m1, -2), _roll_rows(m1, 2)))

    def slab(si):
        rows = pl.ds(si * SLAB, SLAB)
        prow = pl.ds(pl.multiple_of(step * rows_step + si * SLAB, SLAB), SLAB)
        seq0 = si * nseq

        pre = _softcap(pf[prow, B_G:B_G + BLK] + gbias)
        logf = _log_sigmoid(pre)
        y = logf + jnp.where(rr >= 1, _roll_rows(logf, 1), 0.0)
        bcum = y + jnp.where(rr >= 2, _roll_rows(y, 2), 0.0)
        ig = pltpu.roll(pre, ML_HEADS, 1)
        m_prev = mrep_ref[rows, :]
        b_minus_i = bcum - ig
        log_d = [jnp.where(rr >= dl, bcum - _roll_rows(b_minus_i, dl), -jnp.inf) for dl in range(tdec)]
        rowmax = functools.reduce(jnp.maximum, log_d)
        inter = bcum + m_prev
        m_row = jnp.maximum(inter, rowmax)
        scl = jnp.exp(inter - m_row)
        dm = [jnp.exp(ld - m_row) for ld in log_d]
        enm = jnp.exp(-m_row)
        b_last = seg_last(bcum)
        w = b_last - bcum + ig
        m_new = jnp.maximum(b_last + m_prev, seg_max(w))
        a = jnp.exp(b_last + m_prev - m_new)
        kws = jnp.exp(w - m_new) * ML_SCALE
        mo_ref[rows, :] = m_new

        q = [pa[prow, A_Q + h * ML_DK:A_Q + (h + 1) * ML_DK] for h in range(ML_HEADS)]
        k = [pa[prow, A_K + h * ML_DK:A_K + (h + 1) * ML_DK] for h in range(ML_HEADS)]
        v = [pa[prow, A_V + h * ML_DV:A_V + (h + 1) * ML_DV] for h in range(ML_HEADS)]
        c_b = [[c_ref[seq0 + bl, h] for bl in range(nseq)] for h in range(ML_HEADS)]
        qc, upd, kw = [], [], []
        for h in range(ML_HEADS):
            lane = ML_HEADS + h
            qb = q[h].astype(BF16)
            res = [_dot(qb, jnp.concatenate([c_b[h][2 * i].astype(BF16), c_b[h][2 * i + 1].astype(BF16)], axis=1))
                   for i in range(nseq // 2)]
            parts = [res[bl // 2][:, (bl % 2) * ML_DV:(bl % 2 + 1) * ML_DV] for bl in range(nseq)]
            qc_h = parts[nseq - 1]
            for bl in range(nseq - 2, -1, -1):
                qc_h = jnp.where(bl16 == bl, parts[bl], qc_h)
            qc.append(qc_h)
            kw_h = k[h] * kws[:, lane:lane + 1]
            kw.append(kw_h)
            vm = jnp.concatenate([jnp.where(bl16 == bl, v[h], 0.0) for bl in range(nseq)], axis=1)
            upd.append(_dot_tn(kw_h.astype(BF16), vm.astype(BF16)))

        knew = pf[prow, B_SK:B_SK + BLK]
        vnew = pf[prow, B_SV:B_SV + BLK]
        qparts = []
        for hq in range(SWA_HEADS):
            qh = pa[prow, A_SQ + hq * SWA_HD:A_SQ + (hq + 1) * SWA_HD]
            qparts.append(jnp.concatenate([qh, zq] if hq < SWA_GROUP else [zq, qh], axis=1))
        qs = jnp.concatenate(qparts, axis=0).astype(BF16)
        s_c = _dot_nt(qs, kc_ref[seq0 + nseq - 1].astype(BF16))
        for bl in range(nseq - 2, -1, -1):
            s_c = jnp.where(blq == bl, _dot_nt(qs, kc_ref[seq0 + bl].astype(BF16)), s_c)
        kn = jnp.concatenate([knew.astype(BF16), zeros_pad], axis=0)
        s_n = _dot_nt(qs, kn)

        yield
        for h in range(ML_HEADS):
            hs = slice(h * ML_DK, (h + 1) * ML_DK)
            lane = ML_HEADS + h
            bm = jnp.zeros((SLAB, ML_DV), F32)
            sum_s = jnp.zeros((SLAB, 1), F32)
            for dl in range(tdec):
                s_dl = jnp.sum(q[h] * _roll_rows(k[h], dl), axis=-1, keepdims=True)
                ws = s_dl * ML_SCALE * dm[dl][:, lane:lane + 1]
                bm = bm + ws * _roll_rows(v[h], dl)
                sum_s = sum_s + ws
            nsel = jnp.zeros((SLAB, ML_DK), F32)
            for bl in range(nseq):
                msk = bl16 == bl
                n_b = n_ref[seq0 + bl][h:h + 1, :]
                nsel = jnp.where(msk, n_b, nsel)
                a_b = a[bl * tdec:bl * tdec + 1, lane:lane + 1]
                c_out[seq0 + bl, h] = a_b * c_b[h][bl] + upd[h][:, bl * ML_DV:(bl + 1) * ML_DV]
                n_out[seq0 + bl, h:h + 1, :] = a_b * n_b + jnp.sum(jnp.where(msk, kw[h], 0.0), axis=0,
                                                                   keepdims=True)
            sclh = scl[:, lane:lane + 1]
            num = sclh * qc[h] + bm
            den = sclh * jnp.sum(q[h] * nsel, axis=-1, keepdims=True) + sum_s
            hh = num / jnp.maximum(jnp.abs(den), enm[:, lane:lane + 1])
            hn = hh * lax.rsqrt(jnp.mean(hh * hh, axis=-1, keepdims=True) + NORM_EPS)
            og = jax.nn.sigmoid(pf[prow, B_MO + h * ML_DV:B_MO + (h + 1) * ML_DV])
            mix_ref[rows, hs] = (hn * mlg[:, hs] * og).astype(mix_ref.dtype)

        s_c = s_c * SWA_SCALE + tblc[...]
        s_n = s_n * SWA_SCALE + tbln[...]
        mx = jnp.maximum(jnp.maximum(jnp.max(s_c, axis=-1, keepdims=True),
                                     jnp.max(s_n, axis=-1, keepdims=True)), snk)
        p_c = jnp.exp(s_c - mx)
        p_n = jnp.exp(s_n - mx)
        den = (jnp.sum(p_c, axis=-1, keepdims=True) + jnp.sum(p_n, axis=-1, keepdims=True)
               + jnp.exp(snk - mx))

        yield
        vn = jnp.concatenate([vnew.astype(BF16), zeros_pad], axis=0)
        o = _dot(p_n.astype(BF16), vn)
        for bl in range(nseq):
            o = o + _dot(jnp.where(blq == bl, p_c, 0.0).astype(BF16), vc_ref[seq0 + bl].astype(BF16))
        o = o / den
        osel = jnp.where(first_group, o[:, 0:SWA_HD], o[:, SWA_HD:2 * SWA_HD])
        og = jnp.concatenate([osel[hq * SLAB:(hq + 1) * SLAB, :] for hq in range(SWA_HEADS)], axis=1)
        mix_ref[rows, ML_WIDTH:ML_WIDTH + SWA_WIDTH] = og.astype(mix_ref.dtype)

        for bl in range(nseq):
            ko_ref[seq0 + bl, 0:WINDOW - tdec, :] = kc_ref[seq0 + bl, tdec:WINDOW, :]
            vo_ref[seq0 + bl, 0:WINDOW - tdec, :] = vc_ref[seq0 + bl, tdec:WINDOW, :]
            ko_ref[seq0 + bl, WINDOW - tdec:WINDOW, :] = knew[bl * tdec:(bl + 1) * tdec, :]
            vo_ref[seq0 + bl, WINDOW - tdec:WINDOW, :] = vnew[bl * tdec:(bl + 1) * tdec, :]
        yield

    slabs = [slab(si) for si in range(nslab)]
    for _ in range(3):
        for s in slabs:
            next(s)


def _sample_mixer(x2d, wt, wn, gb, mlg, sinks, rb, mrep, c0, n0, kc, vc):
    rows = x2d.shape[0]
    tdec = 4
    nseq = rows // tdec
    gbs = SAMPLE_GB
    steps = nseq // gbs
    rb_rows = gbs * tdec
    const2 = lambda i: (0, 0)
    smem = pl.BlockSpec(memory_space=pltpu.SMEM)
    return pl.pallas_call(
        _sample_mixer_kernel,
        grid=(steps,),
        in_specs=[
            pl.BlockSpec((rows, D_MODEL), const2, pipeline_mode=pl.Buffered(1)),
            pl.BlockSpec((T_COLS, D_MODEL), const2, pipeline_mode=pl.Buffered(1)),
            pl.BlockSpec((D_MODEL, N_B + B_SV), const2, pipeline_mode=pl.Buffered(1)),
            pl.BlockSpec((1, BLK), const2),
            pl.BlockSpec((1, ML_WIDTH), const2),
            smem, smem,
            pl.BlockSpec((rb_rows, BLK), lambda i: (i, 0)),
            pl.BlockSpec((gbs, ML_HEADS, ML_DK, ML_DV), lambda i: (i, 0, 0, 0)),
            pl.BlockSpec((gbs, ML_HEADS, ML_DK), lambda i: (i, 0, 0)),
            pl.BlockSpec((gbs, WINDOW, BLK), lambda i: (i, 0, 0)),
            pl.BlockSpec((gbs, WINDOW, BLK), lambda i: (i, 0, 0)),
        ],
        out_specs=[
            pl.BlockSpec((rb_rows, D_MODEL), lambda i: (i, 0)),
            pl.BlockSpec((gbs, ML_HEADS, ML_DK, ML_DV), lambda i: (i, 0, 0, 0)),
            pl.BlockSpec((gbs, ML_HEADS, ML_DK), lambda i: (i, 0, 0)),
            pl.BlockSpec((rb_rows, BLK), lambda i: (i, 0)),
            pl.BlockSpec((gbs, WINDOW, BLK), lambda i: (i, 0, 0)),
            pl.BlockSpec((gbs, WINDOW, BLK), lambda i: (i, 0, 0)),
        ],
        out_shape=[
            jax.ShapeDtypeStruct((rows, D_MODEL), BF16),
            jax.ShapeDtypeStruct((nseq, ML_HEADS, ML_DK, ML_DV), F32),
            jax.ShapeDtypeStruct((nseq, ML_HEADS, ML_DK), F32),
            jax.ShapeDtypeStruct((rows, BLK), F32),
            jax.ShapeDtypeStruct((nseq, WINDOW, BLK), F32),
            jax.ShapeDtypeStruct((nseq, WINDOW, BLK), F32),
        ],
        scratch_shapes=[
            pltpu.VMEM((rows, A_COLS), F32),
            pltpu.VMEM((rows, B_COLS), F32),
            pltpu.VMEM((SWA_HEADS * SLAB, BLK), F32),
            pltpu.VMEM((SWA_HEADS * SLAB, BLK), F32),
        ],
        compiler_params=pltpu.CompilerParams(
            dimension_semantics=("arbitrary",),
            vmem_limit_bytes=VMEM_LIMIT),
        name="sample_mixer",
    )(x2d, wt, wn, gb, mlg, sinks, rb, mrep, c0, n0, kc, vc)


def _dense_kernel(x_ref, mix_ref, wo_ref, g1_ref, b1_ref, wfi_ref, wfo_ref, g2_ref, b2_ref, y_ref):
    nsub = x_ref.shape[0] // DENSE_SUB
    subs = [slice(i * DENSE_SUB, (i + 1) * DENSE_SUB) for i in range(nsub)]
    chunks = list(zip(FF_SPLITS[:-1], FF_SPLITS[1:]))
    g1, b1, g2, b2 = g1_ref[...], b1_ref[...], g2_ref[...], b2_ref[...]

    proj = [_dot(mix_ref[s, :], wo_ref[...]) for s in subs]
    h1 = [_layer_norm(ALPHA * x_ref[s, :] + proj[i], g1, b1) for i, s in enumerate(subs)]
    h1b = [h.astype(BF16) for h in h1]
    acc = [None] * nsub
    for c0, c1 in chunks:
        gate = [_dot(h1b[i], wfi_ref[:, c0:c1]) for i in range(nsub)]
        up = [_dot(h1b[i], wfi_ref[:, D_FF + c0:D_FF + c1]) for i in range(nsub)]
        for i in range(nsub):
            act = (gate[i] * jax.nn.sigmoid(gate[i]) * up[i]).astype(BF16)
            part = _dot(act, wfo_ref[c0:c1, :])
            acc[i] = part if acc[i] is None else acc[i] + part
    for i, s in enumerate(subs):
        y_ref[s, :] = _layer_norm(ALPHA * h1[i] + acc[i], g2, b2)


def _dense(x2d, mix2d, wo, g1, b1, wfi, wfo, g2, b2):
    rows = x2d.shape[0]
    tm = min(TM_DENSE, rows)
    const2 = lambda i: (0, 0)

    def wspec(shape):
        return pl.BlockSpec(shape, const2, pipeline_mode=pl.Buffered(1))

    return pl.pallas_call(
        _dense_kernel,
        grid=(rows // tm,),
        in_specs=[
            pl.BlockSpec((tm, D_MODEL), lambda i: (i, 0)),
            pl.BlockSpec((tm, D_MODEL), lambda i: (i, 0)),
            wspec((D_MODEL, D_MODEL)),
            wspec((1, D_MODEL)), wspec((1, D_MODEL)),
            wspec((D_MODEL, 2 * D_FF)),
            wspec((D_FF, D_MODEL)),
            wspec((1, D_MODEL)), wspec((1, D_MODEL)),
        ],
        out_specs=pl.BlockSpec((tm, D_MODEL), lambda i: (i, 0)),
        out_shape=jax.ShapeDtypeStruct((rows, D_MODEL), F32),
        compiler_params=pltpu.CompilerParams(
            dimension_semantics=("arbitrary",),
            vmem_limit_bytes=VMEM_LIMIT),
        name="dense",
    )(x2d, mix2d, wo, g1, b1, wfi, wfo, g2, b2)


def kernel(x_prompt, x_sample, state_mlstm_C, state_mlstm_n, state_mlstm_m, cache_swa_k, cache_swa_v,
           w_in, b_igate, b_fgate, ml_norm_g, swa_sinks, rel_bias, w_out, ln1_g, ln1_b,
           w_ffn_in, w_ffn_out, ln2_g, ln2_b):
    bp, seq, _ = x_prompt.shape
    bs, tdec, _ = x_sample.shape
    l = 0

    w16 = jnp.pad(w_in[l].astype(BF16), ((0, 0), (0, W_IN_PAD - w_in.shape[-1])))
    gvec = jnp.concatenate([b_igate[l], b_fgate[l]]).astype(F32)
    gb = jnp.pad(gvec, (0, BLK - 2 * ML_HEADS))[None, :]
    gbc = jnp.broadcast_to(jnp.pad(gvec, (0, T_GROWS - 2 * ML_HEADS))[:, None], (T_GROWS, BLK))
    mlg = ml_norm_g[l][None, :].astype(F32)
    sinks = swa_sinks[l].astype(F32)
    rb = rel_bias.astype(F32)
    wo = w_out[l].astype(BF16)
    wfi = w_ffn_in[l].astype(BF16)
    wfo = w_ffn_out[l].astype(BF16)
    g1, b1 = ln1_g[l][None, :], ln1_b[l][None, :]
    g2, b2 = ln2_g[l][None, :], ln2_b[l][None, :]

    mix_p, c_p, n_p, m_p, k_p, v_p, wt, wn = _prompt_mixer(x_prompt, w16, gbc, mlg, sinks, rb)
    y_p = _dense(x_prompt.reshape(bp * seq, D_MODEL), mix_p.reshape(bp * seq, D_MODEL),
                 wo, g1, b1, wfi, wfo, g2, b2).reshape(bp, seq, D_MODEL)
    p_m = m_p[:, :ML_HEADS // 2, ::BLK].reshape(bp, ML_HEADS)[None]
    p_k = k_p.reshape(1, bp, WINDOW, SWA_KV_HEADS, SWA_HD)
    p_v = v_p.reshape(1, bp, WINDOW, SWA_KV_HEADS, SWA_HD)

    xs = x_sample.reshape(bs * tdec, D_MODEL)
    m0 = state_mlstm_m[l].astype(F32)
    mrep = jnp.pad(jnp.repeat(m0, tdec, axis=0), ((0, 0), (ML_HEADS, BLK - 2 * ML_HEADS)))
    wlen = cache_swa_k.shape[2]
    kc = cache_swa_k[l].reshape(bs, wlen, SWA_KV_HEADS * SWA_HD)
    vc = cache_swa_v[l].reshape(bs, wlen, SWA_KV_HEADS * SWA_HD)
    mix_s, c_s, n_s, mo_s, k_s, v_s = _sample_mixer(
        xs, wt, wn, gb, mlg, sinks, rb, mrep,
        state_mlstm_C[l].astype(F32), state_mlstm_n[l].astype(F32), kc, vc)
    y_s = _dense(xs, mix_s, wo, g1, b1, wfi, wfo, g2, b2).reshape(bs, tdec, D_MODEL)
    s_m = mo_s.reshape(bs, tdec, BLK)[:, 0, ML_HEADS:2 * ML_HEADS][None]
    s_k = k_s.reshape(1, bs, wlen, SWA_KV_HEADS, SWA_HD)
    s_v = v_s.reshape(1, bs, wlen, SWA_KV_HEADS, SWA_HD)

    return (y_p, y_s, c_p[None], n_p[None], p_m, p_k, p_v,
            c_s[None], n_s[None], s_m, s_k, s_v)
```

```python
import functools
import math

import jax
import jax.numpy as jnp
from jax import lax
from jax.experimental import pallas as pl
from jax.experimental.pallas import tpu as pltpu

F32 = jnp.float32
BF16 = jnp.bfloat16

D_MODEL = 1024
ML_HEADS = 4
ML_DK = 128
ML_DV = 128
ML_WIDTH = ML_HEADS * ML_DV
GATE_SOFTCAP = 15.0
SWA_HEADS = 8
SWA_KV_HEADS = 2
SWA_GROUP = SWA_HEADS // SWA_KV_HEADS
SWA_HD = 64
SWA_WIDTH = SWA_HEADS * SWA_HD
WINDOW = 128
REL_BUCKETS = 32
REL_MAX_DIST = 128
D_FF = 2816
DEPTH = 1
ALPHA = (2.0 * DEPTH) ** 0.25
LN_EPS = 1e-5
NORM_EPS = 1e-6
NEG_INF = -1e30
IN_SIZES = (512, 512, 512, 512, 4, 4, 512, 128, 128)
ML_SCALE = ML_DK ** -0.5
LOG_ML_SCALE = math.log(ML_SCALE)
SWA_SCALE = SWA_HD ** -0.5

BLK = 128

A_Q, A_K, A_V, A_SQ = 0, 512, 1024, 1536
A_COLS = 2048
B_MO, B_SK, B_SV, B_G = 0, 512, 640, 768
B_COLS = 896

N_K = 0
N_B = 512
T_Q, T_V, T_SQ, T_SV, T_G = 0, 512, 1024, 1536, 1664
T_GROWS = 16
T_COLS = T_G + BLK
W_IN_PAD = -(-sum(IN_SIZES) // BLK) * BLK
STATE_ROWS = ML_DV + 16

TM_PROMPT = 1024
TM_DENSE = 1024
DENSE_SUB = 512
FF_SPLITS = (0, 1536, D_FF)
SAMPLE_GB = 16
SLAB = 16
VMEM_LIMIT = 56 * 1024 * 1024


def _softcap(a):
    return GATE_SOFTCAP * jnp.tanh(a / GATE_SOFTCAP)


def _log_sigmoid(x):
    return jnp.minimum(x, 0.0) - jnp.log1p(jnp.exp(-jnp.abs(x)))


def _layer_norm(z, g, b):
    mu = jnp.mean(z, axis=-1, keepdims=True)
    zc = z - mu
    var = jnp.mean(zc * zc, axis=-1, keepdims=True)
    return zc * lax.rsqrt(var + LN_EPS) * g + b


def _dot(a, b):
    return jnp.dot(a, b, preferred_element_type=F32)


def _dot_nt(a, b):
    return lax.dot_general(a, b, (((1,), (1,)), ((), ())), preferred_element_type=F32)


def _dot_tn(a, b):
    return lax.dot_general(a, b, (((0,), (0,)), ((), ())), preferred_element_type=F32)


def _split3(x):
    hi = x.astype(BF16)
    r1 = x - hi.astype(F32)
    mid = r1.astype(BF16)
    lo = (r1 - mid.astype(F32)).astype(BF16)
    return hi, mid, lo


def _t5_bucket(d):
    n = jnp.maximum(d, 0)
    max_exact = REL_BUCKETS // 2
    nlog = REL_BUCKETS - max_exact
    large = jnp.full(n.shape, max_exact, jnp.int32)
    for k in range(1, nlog):
        thr = math.ceil(max_exact * (REL_MAX_DIST / max_exact) ** (k / nlog))
        large = large + jnp.where(n >= thr, 1, 0)
    return jnp.where(n < max_exact, n, large)


def _prompt_mixer_kernel(x_ref, w_ref, gbc_ref, mlg_ref, sinks_ref, rb_ref,
                         mix_ref, c_out, n_out, m_out, ko_ref, vo_ref,
                         wt_out, wn_out, wt, kn, pf, vlast, kband, q_t, v_t, sq_t, sv_t, g_t, ct, msc, tbl, wn):
    b = pl.program_id(0)
    t = pl.program_id(1)
    nt = pl.num_programs(1)
    tm = x_ref.shape[1]
    nblk = tm // BLK
    pairs = SWA_HEADS // 2

    @pl.when((b == 0) & (t == 0))
    def _prepare_weights():
        offs = [0]
        for n in IN_SIZES:
            offs.append(offs[-1] + n)
        o_mq, o_mk, o_mv, o_mo, o_mi, _, o_sq, o_sk, o_sv = offs[:-1]

        def col_blocks(dst_col, src_col, n):
            for i in range(n // BLK):
                wn[:, dst_col + i * BLK:dst_col + (i + 1) * BLK] = (
                    w_ref[:, src_col + i * BLK:src_col + (i + 1) * BLK])

        def row_blocks(dst_row, src_col, n):
            for i in range(n // BLK):
                wt[dst_row + i * BLK:dst_row + (i + 1) * BLK, :] = (
                    w_ref[:, src_col + i * BLK:src_col + (i + 1) * BLK].T)

        col_blocks(N_K, o_mk, ML_HEADS * ML_DK)
        col_blocks(N_B + B_MO, o_mo, ML_WIDTH)
        col_blocks(N_B + B_SK, o_sk, SWA_KV_HEADS * SWA_HD)
        row_blocks(T_Q, o_mq, ML_HEADS * ML_DK)
        row_blocks(T_V, o_mv, ML_WIDTH)
        row_blocks(T_SQ, o_sq, SWA_WIDTH)
        row_blocks(T_SV, o_sv, SWA_KV_HEADS * SWA_HD)
        gates_t = w_ref[:, o_mi:o_mi + BLK].T
        grow = lax.broadcasted_iota(jnp.int32, gates_t.shape, 0)
        wt[T_G:T_G + BLK, :] = jnp.where(grow < 2 * ML_HEADS, gates_t, jnp.zeros_like(gates_t))
        wt_out[...] = wt[...]
        wn_out[...] = wn[...]

    @pl.when((b == 0) & (t == 0))
    def _build_bias_tables():
        r = lax.broadcasted_iota(jnp.int32, (BLK, 2 * BLK), 0)
        ln = lax.broadcasted_iota(jnp.int32, (BLK, 2 * BLK), 1)
        second = ln >= BLK
        qi = jnp.where(second, ln - BLK, ln)
        prev = r > qi
        d = jnp.where(prev, WINDOW + qi - r, qi - r)
        bucket = _t5_bucket(d)
        for p in range(pairs):
            acc = jnp.zeros((BLK, 2 * BLK), F32)
            for k in range(REL_BUCKETS):
                acc = jnp.where(bucket == k, jnp.where(second, rb_ref[k, p + pairs], rb_ref[k, p]), acc)
            tbl[0, p] = jnp.where(prev, NEG_INF, acc)
            tbl[1, p] = acc

    @pl.when(t == 0)
    def _reset_state():
        ct[...] = jnp.zeros_like(ct)
        msc[...] = jnp.zeros_like(msc)
        kband[0:BLK, :] = jnp.zeros((BLK, kband.shape[1]), kband.dtype)
        sv_t[0] = jnp.zeros(sv_t.shape[1:], sv_t.dtype)

    xb = x_ref[0].astype(BF16)
    res_n = _dot(xb, wn[...])
    kn[...] = res_n[:, N_K:N_K + ML_HEADS * ML_DK].astype(BF16)
    pf[...] = res_n[:, N_B:N_B + B_SV]
    kband[BLK:BLK + tm, :] = pf[:, B_SK:B_SK + BLK].astype(BF16)

    res_t = _dot_nt(wt[0:T_G + T_GROWS, :], xb)
    vlast[...] = res_t[T_SV:T_SV + BLK, tm - BLK:tm]

    def put(dst, r0, nrows, off, scale=None):
        res = res_t[r0:r0 + nrows, :]
        if scale is not None:
            res = res * scale
        for jj in range(nblk):
            dst[jj + off] = res[:, jj * BLK:(jj + 1) * BLK].astype(dst.dtype)

    put(q_t, T_Q, ML_HEADS * ML_DK, 0)
    put(v_t, T_V, ML_WIDTH, 0)
    put(sq_t, T_SQ, SWA_WIDTH, 0, SWA_SCALE)
    put(sv_t, T_SV, SWA_KV_HEADS * SWA_HD, 1)
    put(g_t, T_G, T_GROWS, 0)

    row = lax.broadcasted_iota(jnp.int32, (BLK, BLK), 0)
    col = lax.broadcasted_iota(jnp.int32, (BLK, BLK), 1)
    causal_t = row <= col
    triu = causal_t.astype(BF16)
    causal2 = jnp.concatenate([causal_t, causal_t], axis=1)
    r16 = lax.broadcasted_iota(jnp.int32, (16, 2 * BLK), 0)
    ones_rows = (r16 == 0).astype(BF16)
    z128 = jnp.zeros((BLK, BLK), BF16)
    ml_pairs = ML_HEADS // 2
    zero_rows = jnp.zeros((BLK - T_GROWS, BLK), F32)
    gbc = gbc_ref[...]
    mlg = mlg_ref[...]
    lane2 = lax.broadcasted_iota(jnp.int32, (1, 2 * BLK), 1)
    snk_rows = [jnp.where(lane2 >= BLK, sinks_ref[p + pairs], sinks_ref[p]) for p in range(pairs)]
    zq = jnp.zeros((SWA_HD, BLK), BF16)
    prev2 = jnp.concatenate([row > col, row > col], axis=1)

    def pair_row(x0, x1):
        return jnp.concatenate([jnp.broadcast_to(x0, (1, BLK)), jnp.broadcast_to(x1, (1, BLK))], axis=1)

    def block_diag(x0, x1):
        return jnp.concatenate([jnp.concatenate([x0, z128], axis=1),
                                jnp.concatenate([z128, x1], axis=1)], axis=0)

    def block(j, carry):
        r0 = pl.multiple_of(j * BLK, BLK)
        rows = pl.ds(r0, BLK)
        hsl = [slice(h * ML_DK, (h + 1) * ML_DK) for h in range(ML_HEADS)]
        psl = [slice(p * 2 * ML_DK, (p + 1) * 2 * ML_DK) for p in range(ml_pairs)]

        pre_t = _softcap(g_t[j] + gbc)
        hi, mid, lo = _split3(_log_sigmoid(pre_t))
        b_t = _dot(hi, triu) + _dot(mid, triu) + _dot(lo, triu)
        u_t = pltpu.roll(pre_t, ML_HEADS, 0) - b_t
        u_c = jnp.concatenate([u_t, zero_rows], axis=0).T

        k2 = [kn[rows, psl[p]] for p in range(ml_pairs)]
        bdq = [block_diag(q_t[j, hsl[2 * p], :], q_t[j, hsl[2 * p + 1], :]) for p in range(ml_pairs)]
        vte = [jnp.concatenate([jnp.concatenate([v_t[j, hsl[2 * p], :], v_t[j, hsl[2 * p + 1], :]], axis=1),
                                ones_rows], axis=0) for p in range(ml_pairs)]
        kq = [_dot(k2[p], bdq[p]) for p in range(ml_pairs)]

        sel = jnp.where((t == 0) & (j == 0), 0, 1)
        kb = kband[pl.ds(r0, 2 * BLK), :]
        vtb = jnp.concatenate([sv_t[j], sv_t[j + 1]], axis=1)
        sc2 = []
        for p in range(pairs):
            qa = sq_t[j, p * SWA_HD:(p + 1) * SWA_HD, :]
            qb = sq_t[j, (p + pairs) * SWA_HD:(p + pairs + 1) * SWA_HD, :]
            bd = jnp.concatenate([jnp.concatenate([qa, zq], axis=1),
                                  jnp.concatenate([zq, qb], axis=1)], axis=0)
            sc2.append(_dot(kb, bd))

        bds, scl, m_row, m_new, a_p, vwte = [], [], [], [], [], []
        for p in range(ml_pairs):
            h0, h1 = 2 * p, 2 * p + 1
            b_row = pair_row(b_t[ML_HEADS + h0:ML_HEADS + h0 + 1, :], b_t[ML_HEADS + h1:ML_HEADS + h1 + 1, :])
            i_row = pair_row(pre_t[h0:h0 + 1, :], pre_t[h1:h1 + 1, :])
            u_col = jnp.concatenate(
                [jnp.broadcast_to(u_c[:, ML_HEADS + h0:ML_HEADS + h0 + 1], (BLK, BLK)),
                 jnp.broadcast_to(u_c[:, ML_HEADS + h1:ML_HEADS + h1 + 1], (BLK, BLK))], axis=1)
            m_prev = msc[p:p + 1, :]
            log_d = jnp.where(causal2, b_row + u_col, -jnp.inf)
            inter = b_row + m_prev
            mr = jnp.maximum(inter, jnp.max(log_d, axis=0, keepdims=True))
            m_row.append(mr)
            scl.append(jnp.exp(inter - mr))
            s2 = (kq[p] * jnp.exp(log_d - (mr - LOG_ML_SCALE))).astype(BF16)
            bds.append(block_diag(s2[:, :BLK], s2[:, BLK:]))
            b_last = pair_row(b_row[:, BLK - 1:BLK], b_row[:, 2 * BLK - 1:2 * BLK])
            w_row = b_last - b_row + i_row
            w_max = pair_row(jnp.max(w_row[:, :BLK], axis=1, keepdims=True),
                             jnp.max(w_row[:, BLK:], axis=1, keepdims=True))
            mn = jnp.maximum(b_last + m_prev, w_max)
            m_new.append(mn)
            a_p.append(jnp.exp(b_last + m_prev - mn))
            wexp = jnp.exp(w_row - mn) * ML_SCALE
            vwte.append(vte[p] * wexp.astype(BF16))

        pw, den_p = [], []
        for p in range(pairs):
            sc = jnp.where(prev2, sc2[p][:BLK], sc2[p][BLK:]) + tbl[sel, p]
            snk = snk_rows[p]
            mx = jnp.maximum(jnp.max(sc, axis=0, keepdims=True), snk)
            pe = jnp.exp(sc - mx)
            den_p.append(jnp.sum(pe, axis=0, keepdims=True) + jnp.exp(snk - mx))
            pw.append(jnp.concatenate([jnp.where(prev2, pe, 0.0), jnp.where(prev2, 0.0, pe)],
                                      axis=0).astype(BF16))

        c_old = [ct[p] for p in range(ml_pairs)]
        cq = [_dot(c_old[p].astype(BF16), bdq[p]) for p in range(ml_pairs)]
        vs = [_dot(vte[p], bds[p]) for p in range(ml_pairs)]
        upd = [_dot(vwte[p], block_diag(k2[p][:, :BLK], k2[p][:, BLK:])) for p in range(ml_pairs)]
        o2 = [_dot(vtb, pw[p]) for p in range(pairs)]

        for p in range(ml_pairs):
            nd = scl[p] * cq[p] + vs[p]
            num = nd[:ML_DV]
            den = nd[ML_DV:ML_DV + 1]
            hh = num / jnp.maximum(jnp.abs(den), jnp.exp(-m_row[p]))
            hn = hh * lax.rsqrt(jnp.mean(hh * hh, axis=0, keepdims=True) + NORM_EPS)
            for half in range(2):
                h = 2 * p + half
                og = jax.nn.sigmoid(pf[rows, B_MO + h * ML_DV:B_MO + (h + 1) * ML_DV])
                hn_h = hn[:, half * BLK:(half + 1) * BLK].T
                mix_ref[0, rows, hsl[h]] = (hn_h * mlg[:, hsl[h]] * og).astype(mix_ref.dtype)
            ct[p] = a_p[p] * c_old[p] + upd[p]
            msc[p:p + 1, :] = m_new[p]

        o_parts = [None] * SWA_HEADS
        for p in range(pairs):
            on = o2[p] / den_p[p]
            o_parts[p] = on[0:SWA_HD, 0:BLK]
            o_parts[p + pairs] = on[SWA_HD:2 * SWA_HD, BLK:2 * BLK]
        o_t = jnp.concatenate(o_parts, axis=0)
        mix_ref[0, rows, ML_WIDTH:ML_WIDTH + SWA_WIDTH] = o_t.T.astype(mix_ref.dtype)
        return carry

    lax.fori_loop(0, nblk, block, 0, unroll=8)

    kband[0:BLK, :] = kband[tm:tm + BLK, :]
    sv_t[0] = sv_t[nblk]

    @pl.when(t == nt - 1)
    def _write_state():
        for h in range(ML_HEADS):
            p, lanes = h // 2, slice((h % 2) * ML_DK, (h % 2 + 1) * ML_DK)
            c_out[0, h] = ct[p, 0:ML_DV, lanes].T
            n_out[0, h:h + 1, :] = ct[p, ML_DV:ML_DV + 1, lanes]
        m_out[0] = msc[...]
        ko_ref[0] = pf[tm - BLK:tm, B_SK:B_SK + BLK]
        vo_ref[0] = vlast[...].T


def _prompt_mixer(x, w_in_l, gbc, mlg, sinks, rb):
    bsz, seq, _ = x.shape
    tm = TM_PROMPT
    nt = seq // tm
    nblk = tm // BLK
    const2 = lambda b, t: (0, 0)
    smem = pl.BlockSpec(memory_space=pltpu.SMEM)
    return pl.pallas_call(
        _prompt_mixer_kernel,
        grid=(bsz, nt),
        in_specs=[
            pl.BlockSpec((1, tm, D_MODEL), lambda b, t: (b, t, 0)),
            pl.BlockSpec((D_MODEL, W_IN_PAD), const2, pipeline_mode=pl.Buffered(1)),
            pl.BlockSpec((T_GROWS, BLK), const2),
            pl.BlockSpec((1, ML_WIDTH), const2),
            smem, smem,
        ],
        out_specs=[
            pl.BlockSpec((1, tm, D_MODEL), lambda b, t: (b, t, 0)),
            pl.BlockSpec((1, ML_HEADS, ML_DK, ML_DV), lambda b, t: (b, 0, 0, 0)),
            pl.BlockSpec((1, ML_HEADS, ML_DK), lambda b, t: (b, 0, 0)),
            pl.BlockSpec((1, 8, 2 * BLK), lambda b, t: (b, 0, 0)),
            pl.BlockSpec((1, BLK, BLK), lambda b, t: (b, 0, 0)),
            pl.BlockSpec((1, BLK, BLK), lambda b, t: (b, 0, 0)),
            pl.BlockSpec((T_COLS, D_MODEL), const2),
            pl.BlockSpec((D_MODEL, N_B + B_SV), const2),
        ],
        out_shape=[
            jax.ShapeDtypeStruct((bsz, seq, D_MODEL), BF16),
            jax.ShapeDtypeStruct((bsz, ML_HEADS, ML_DK, ML_DV), F32),
            jax.ShapeDtypeStruct((bsz, ML_HEADS, ML_DK), F32),
            jax.ShapeDtypeStruct((bsz, 8, 2 * BLK), F32),
            jax.ShapeDtypeStruct((bsz, BLK, BLK), F32),
            jax.ShapeDtypeStruct((bsz, BLK, BLK), F32),
            jax.ShapeDtypeStruct((T_COLS, D_MODEL), BF16),
            jax.ShapeDtypeStruct((D_MODEL, N_B + B_SV), BF16),
        ],
        scratch_shapes=[
            pltpu.VMEM((T_COLS, D_MODEL), BF16),
            pltpu.VMEM((tm, ML_HEADS * ML_DK), BF16),
            pltpu.VMEM((tm, B_SV), F32),
            pltpu.VMEM((BLK, BLK), F32),
            pltpu.VMEM((tm + BLK, BLK), BF16),
            pltpu.VMEM((nblk, ML_HEADS * ML_DK, BLK), BF16),
            pltpu.VMEM((nblk, ML_WIDTH, BLK), BF16),
            pltpu.VMEM((nblk, SWA_WIDTH, BLK), BF16),
            pltpu.VMEM((nblk + 1, SWA_KV_HEADS * SWA_HD, BLK), BF16),
            pltpu.VMEM((nblk, T_GROWS, BLK), F32),
            pltpu.VMEM((ML_HEADS // 2, STATE_ROWS, 2 * ML_DK), F32),
            pltpu.VMEM((8, 2 * BLK), F32),
            pltpu.VMEM((2, SWA_HEADS // 2, BLK, 2 * BLK), F32),
            pltpu.VMEM((D_MODEL, N_B + B_SV), BF16),
        ],
        compiler_params=pltpu.CompilerParams(
            dimension_semantics=("arbitrary", "arbitrary"),
            vmem_limit_bytes=VMEM_LIMIT),
        name="prompt_mixer",
    )(x, w_in_l, gbc, mlg, sinks, rb)


def _roll_rows(x, shift):
    return pltpu.roll(x, shift % x.shape[0], 0)


def _sample_mixer_kernel(x_ref, wt_ref, wn_ref, gb_ref, mlg_ref, sinks_ref, rb_ref, mrep_ref,
                         c_ref, n_ref, kc_ref, vc_ref,
                         mix_ref, c_out, n_out, mo_ref, ko_ref, vo_ref,
                         pa, pf, tblc, tbln):
    step = pl.program_id(0)
    tdec = 4
    rows_step = mix_ref.shape[0]
    nslab = rows_step // SLAB
    nseq = SLAB // tdec
    qrows = SWA_HEADS * SLAB

    @pl.when(step == 0)
    def _project_all_rows():
        xb = x_ref[...].astype(BF16)
        pa[:, A_Q:A_K] = _dot_nt(xb, wt_ref[T_Q:T_Q + ML_HEADS * ML_DK, :])
        pa[:, A_K:A_V] = _dot(xb, wn_ref[:, N_K:N_K + ML_HEADS * ML_DK])
        pa[:, A_V:A_COLS] = _dot_nt(xb, wt_ref[T_V:T_V + ML_WIDTH + SWA_WIDTH, :])
        pf[:, B_MO:B_SV] = _dot(xb, wn_ref[:, N_B:N_B + B_SV])
        pf[:, B_SV:B_COLS] = _dot_nt(xb, wt_ref[T_SV:T_COLS, :])

    @pl.when(step == 0)
    def _build_bias_tables():
        r = lax.broadcasted_iota(jnp.int32, (qrows, BLK), 0)
        c = lax.broadcasted_iota(jnp.int32, (qrows, BLK), 1)
        hq = r // SLAB
        bl = (r % SLAB) // tdec
        tq = r % tdec
        d_cache = WINDOW + tq - c
        d_new = tq - (c % tdec)
        key_cache = _t5_bucket(d_cache) * SWA_HEADS + hq
        key_new = _t5_bucket(d_new) * SWA_HEADS + hq
        acc_c = jnp.zeros((qrows, BLK), F32)
        acc_n = jnp.zeros((qrows, BLK), F32)
        for k in range(REL_BUCKETS):
            for h in range(SWA_HEADS):
                val = rb_ref[k, h]
                acc_c = jnp.where(key_cache == k * SWA_HEADS + h, val, acc_c)
                acc_n = jnp.where(key_new == k * SWA_HEADS + h, val, acc_n)
        tblc[...] = jnp.where(c > tq, acc_c, NEG_INF)
        tbln[...] = jnp.where((c < SLAB) & ((c // tdec) == bl) & (d_new >= 0), acc_n, NEG_INF)

    gbias = gb_ref[...]
    mlg = mlg_ref[...]
    r16 = lax.broadcasted_iota(jnp.int32, (SLAB, BLK), 0)
    rr = r16 % tdec
    bl16 = r16 // tdec
    rq = lax.broadcasted_iota(jnp.int32, (qrows, BLK), 0)
    blq = (rq % SLAB) // tdec
    hq_col = rq[:, 0:1] // SLAB
    snk = jnp.zeros((qrows, 1), F32)
    for h in range(SWA_HEADS):
        snk = jnp.where(hq_col == h, sinks_ref[h], snk)
    first_group = rq[:, 0:SWA_HD] < SWA_GROUP * SLAB
    zq = jnp.zeros((SLAB, SWA_HD), F32)
    zeros_pad = jnp.zeros((BLK - SLAB, BLK), BF16)

    def seg_last(x):
        return jnp.where(rr == 3, x,
                         jnp.where(rr == 2, _roll_rows(x, -1),
                                   jnp.where(rr == 1, _roll_rows(x, -2), _roll_rows(x, -3))))

    def seg_max(x):
        m1 = jnp.maximum(x, jnp.where(rr % 2 == 0, _roll_rows(x, -1), _roll_rows(x, 1)))
        return jnp.maximum(m1, jnp.where(rr < 2, _roll_rows(m1, -2), _roll_rows(m1, 2)))

    def slab(si):
        rows = pl.ds(si * SLAB, SLAB)
        prow = pl.ds(pl.multiple_of(step * rows_step + si * SLAB, SLAB), SLAB)
        seq0 = si * nseq

        pre = _softcap(pf[prow, B_G:B_G + BLK] + gbias)
        logf = _log_sigmoid(pre)
        y = logf + jnp.where(rr >= 1, _roll_rows(logf, 1), 0.0)
        bcum = y + jnp.where(rr >= 2, _roll_rows(y, 2), 0.0)
        ig = pltpu.roll(pre, ML_HEADS, 1)
        m_prev = mrep_ref[rows, :]
        b_minus_i = bcum - ig
        log_d = [jnp.where(rr >= dl, bcum - _roll_rows(b_minus_i, dl), -jnp.inf) for dl in range(tdec)]
        rowmax = functools.reduce(jnp.maximum, log_d)
        inter = bcum + m_prev
        m_row = jnp.maximum(inter, rowmax)
        scl = jnp.exp(inter - m_row)
        dm = [jnp.exp(ld - m_row) for ld in log_d]
        enm = jnp.exp(-m_row)
        b_last = seg_last(bcum)
        w = b_last - bcum + ig
        m_new = jnp.maximum(b_last + m_prev, seg_max(w))
        a = jnp.exp(b_last + m_prev - m_new)
        kws = jnp.exp(w - m_new) * ML_SCALE
        mo_ref[rows, :] = m_new

        q = [pa[prow, A_Q + h * ML_DK:A_Q + (h + 1) * ML_DK] for h in range(ML_HEADS)]
        k = [pa[prow, A_K + h * ML_DK:A_K + (h + 1) * ML_DK] for h in range(ML_HEADS)]
        v = [pa[prow, A_V + h * ML_DV:A_V + (h + 1) * ML_DV] for h in range(ML_HEADS)]
        c_b = [[c_ref[seq0 + bl, h] for bl in range(nseq)] for h in range(ML_HEADS)]
        qc, upd, kw = [], [], []
        for h in range(ML_HEADS):
            lane = ML_HEADS + h
            qb = q[h].astype(BF16)
            res = [_dot(qb, jnp.concatenate([c_b[h][2 * i].astype(BF16), c_b[h][2 * i + 1].astype(BF16)], axis=1))
                   for i in range(nseq // 2)]
            parts = [res[bl // 2][:, (bl % 2) * ML_DV:(bl % 2 + 1) * ML_DV] for bl in range(nseq)]
            qc_h = parts[nseq - 1]
            for bl in range(nseq - 2, -1, -1):
                qc_h = jnp.where(bl16 == bl, parts[bl], qc_h)
            qc.append(qc_h)
            kw_h = k[h] * kws[:, lane:lane + 1]
            kw.append(kw_h)
            vm = jnp.concatenate([jnp.where(bl16 == bl, v[h], 0.0) for bl in range(nseq)], axis=1)
            upd.append(_dot_tn(kw_h.astype(BF16), vm.astype(BF16)))

        knew = pf[prow, B_SK:B_SK + BLK]
        vnew = pf[prow, B_SV:B_SV + BLK]
        qparts = []
        for hq in range(SWA_HEADS):
            qh = pa[prow, A_SQ + hq * SWA_HD:A_SQ + (hq + 1) * SWA_HD]
            qparts.append(jnp.concatenate([qh, zq] if hq < SWA_GROUP else [zq, qh], axis=1))
        qs = jnp.concatenate(qparts, axis=0).astype(BF16)
        s_c = _dot_nt(qs, kc_ref[seq0 + nseq - 1].astype(BF16))
        for bl in range(nseq - 2, -1, -1):
            s_c = jnp.where(blq == bl, _dot_nt(qs, kc_ref[seq0 + bl].astype(BF16)), s_c)
        kn = jnp.concatenate([knew.astype(BF16), zeros_pad], axis=0)
        s_n = _dot_nt(qs, kn)

        yield
        for h in range(ML_HEADS):
            hs = slice(h * ML_DK, (h + 1) * ML_DK)
            lane = ML_HEADS + h
            bm = jnp.zeros((SLAB, ML_DV), F32)
            sum_s = jnp.zeros((SLAB, 1), F32)
            for dl in range(tdec):
                s_dl = jnp.sum(q[h] * _roll_rows(k[h], dl), axis=-1, keepdims=True)
                ws = s_dl * ML_SCALE * dm[dl][:, lane:lane + 1]
                bm = bm + ws * _roll_rows(v[h], dl)
                sum_s = sum_s + ws
            nsel = jnp.zeros((SLAB, ML_DK), F32)
            for bl in range(nseq):
                msk = bl16 == bl
                n_b = n_ref[seq0 + bl][h:h + 1, :]
                nsel = jnp.where(msk, n_b, nsel)
                a_b = a[bl * tdec:bl * tdec + 1, lane:lane + 1]
                c_out[seq0 + bl, h] = a_b * c_b[h][bl] + upd[h][:, bl * ML_DV:(bl + 1) * ML_DV]
                n_out[seq0 + bl, h:h + 1, :] = a_b * n_b + jnp.sum(jnp.where(msk, kw[h], 0.0), axis=0,
                                                                   keepdims=True)
            sclh = scl[:, lane:lane + 1]
            num = sclh * qc[h] + bm
            den = sclh * jnp.sum(q[h] * nsel, axis=-1, keepdims=True) + sum_s
            hh = num / jnp.maximum(jnp.abs(den), enm[:, lane:lane + 1])
            hn = hh * lax.rsqrt(jnp.mean(hh * hh, axis=-1, keepdims=True) + NORM_EPS)
            og = jax.nn.sigmoid(pf[prow, B_MO + h * ML_DV:B_MO + (h + 1) * ML_DV])
            mix_ref[rows, hs] = (hn * mlg[:, hs] * og).astype(mix_ref.dtype)

        s_c = s_c * SWA_SCALE + tblc[...]
        s_n = s_n * SWA_SCALE + tbln[...]
        mx = jnp.maximum(jnp.maximum(jnp.max(s_c, axis=-1, keepdims=True),
                                     jnp.max(s_n, axis=-1, keepdims=True)), snk)
        p_c = jnp.exp(s_c - mx)
        p_n = jnp.exp(s_n - mx)
        den = (jnp.sum(p_c, axis=-1, keepdims=True) + jnp.sum(p_n, axis=-1, keepdims=True)
               + jnp.exp(snk - mx))

        yield
        vn = jnp.concatenate([vnew.astype(BF16), zeros_pad], axis=0)
        o = _dot(p_n.astype(BF16), vn)
        for bl in range(nseq):
            o = o + _dot(jnp.where(blq == bl, p_c, 0.0).astype(BF16), vc_ref[seq0 + bl].astype(BF16))
        o = o / den
        osel = jnp.where(first_group, o[:, 0:SWA_HD], o[:, SWA_HD:2 * SWA_HD])
        og = jnp.concatenate([osel[hq * SLAB:(hq + 1) * SLAB, :] for hq in range(SWA_HEADS)], axis=1)
        mix_ref[rows, ML_WIDTH:ML_WIDTH + SWA_WIDTH] = og.astype(mix_ref.dtype)

        for bl in range(nseq):
            ko_ref[seq0 + bl, 0:WINDOW - tdec, :] = kc_ref[seq0 + bl, tdec:WINDOW, :]
            vo_ref[seq0 + bl, 0:WINDOW - tdec, :] = vc_ref[seq0 + bl, tdec:WINDOW, :]
            ko_ref[seq0 + bl, WINDOW - tdec:WINDOW, :] = knew[bl * tdec:(bl + 1) * tdec, :]
            vo_ref[seq0 + bl, WINDOW - tdec:WINDOW, :] = vnew[bl * tdec:(bl + 1) * tdec, :]
        yield

    slabs = [slab(si) for si in range(nslab)]
    for _ in range(3):
        for s in slabs:
            next(s)


def _sample_mixer(x2d, wt, wn, gb, mlg, sinks, rb, mrep, c0, n0, kc, vc):
    rows = x2d.shape[0]
    tdec = 4
    nseq = rows // tdec
    gbs = SAMPLE_GB
    steps = nseq // gbs
    rb_rows = gbs * tdec
    const2 = lambda i: (0, 0)
    smem = pl.BlockSpec(memory_space=pltpu.SMEM)
    return pl.pallas_call(
        _sample_mixer_kernel,
        grid=(steps,),
        in_specs=[
            pl.BlockSpec((rows, D_MODEL), const2, pipeline_mode=pl.Buffered(1)),
            pl.BlockSpec((T_COLS, D_MODEL), const2, pipeline_mode=pl.Buffered(1)),
            pl.BlockSpec((D_MODEL, N_B + B_SV), const2, pipeline_mode=pl.Buffered(1)),
            pl.BlockSpec((1, BLK), const2),
            pl.BlockSpec((1, ML_WIDTH), const2),
            smem, smem,
            pl.BlockSpec((rb_rows, BLK), lambda i: (i, 0)),
            pl.BlockSpec((gbs, ML_HEADS, ML_DK, ML_DV), lambda i: (i, 0, 0, 0)),
            pl.BlockSpec((gbs, ML_HEADS, ML_DK), lambda i: (i, 0, 0)),
            pl.BlockSpec((gbs, WINDOW, BLK), lambda i: (i, 0, 0)),
            pl.BlockSpec((gbs, WINDOW, BLK), lambda i: (i, 0, 0)),
        ],
        out_specs=[
            pl.BlockSpec((rb_rows, D_MODEL), lambda i: (i, 0)),
            pl.BlockSpec((gbs, ML_HEADS, ML_DK, ML_DV), lambda i: (i, 0, 0, 0)),
            pl.BlockSpec((gbs, ML_HEADS, ML_DK), lambda i: (i, 0, 0)),
            pl.BlockSpec((rb_rows, BLK), lambda i: (i, 0)),
            pl.BlockSpec((gbs, WINDOW, BLK), lambda i: (i, 0, 0)),
            pl.BlockSpec((gbs, WINDOW, BLK), lambda i: (i, 0, 0)),
        ],
        out_shape=[
            jax.ShapeDtypeStruct((rows, D_MODEL), BF16),
            jax.ShapeDtypeStruct((nseq, ML_HEADS, ML_DK, ML_DV), F32),
            jax.ShapeDtypeStruct((nseq, ML_HEADS, ML_DK), F32),
            jax.ShapeDtypeStruct((rows, BLK), F32),
            jax.ShapeDtypeStruct((nseq, WINDOW, BLK), F32),
            jax.ShapeDtypeStruct((nseq, WINDOW, BLK), F32),
        ],
        scratch_shapes=[
            pltpu.VMEM((rows, A_COLS), F32),
            pltpu.VMEM((rows, B_COLS), F32),
            pltpu.VMEM((SWA_HEADS * SLAB, BLK), F32),
            pltpu.VMEM((SWA_HEADS * SLAB, BLK), F32),
        ],
        compiler_params=pltpu.CompilerParams(
            dimension_semantics=("arbitrary",),
            vmem_limit_bytes=VMEM_LIMIT),
        name="sample_mixer",
    )(x2d, wt, wn, gb, mlg, sinks, rb, mrep, c0, n0, kc, vc)


def _dense_kernel(x_ref, mix_ref, wo_ref, g1_ref, b1_ref, wfi_ref, wfo_ref, g2_ref, b2_ref, y_ref):
    nsub = x_ref.shape[0] // DENSE_SUB
    subs = [slice(i * DENSE_SUB, (i + 1) * DENSE_SUB) for i in range(nsub)]
    chunks = list(zip(FF_SPLITS[:-1], FF_SPLITS[1:]))
    g1, b1, g2, b2 = g1_ref[...], b1_ref[...], g2_ref[...], b2_ref[...]

    proj = [_dot(mix_ref[s, :], wo_ref[...]) for s in subs]
    h1 = [_layer_norm(ALPHA * x_ref[s, :] + proj[i], g1, b1) for i, s in enumerate(subs)]
    h1b = [h.astype(BF16) for h in h1]
    acc = [None] * nsub
    for c0, c1 in chunks:
        gate = [_dot(h1b[i], wfi_ref[:, c0:c1]) for i in range(nsub)]
        up = [_dot(h1b[i], wfi_ref[:, D_FF + c0:D_FF + c1]) for i in range(nsub)]
        for i in range(nsub):
            act = (gate[i] * jax.nn.sigmoid(gate[i]) * up[i]).astype(BF16)
            part = _dot(act, wfo_ref[c0:c1, :])
            acc[i] = part if acc[i] is None else acc[i] + part
    for i, s in enumerate(subs):
        y_ref[s, :] = _layer_norm(ALPHA * h1[i] + acc[i], g2, b2)


def _dense(x2d, mix2d, wo, g1, b1, wfi, wfo, g2, b2):
    rows = x2d.shape[0]
    tm = min(TM_DENSE, rows)
    const2 = lambda i: (0, 0)

    def wspec(shape):
        return pl.BlockSpec(shape, const2, pipeline_mode=pl.Buffered(1))

    return pl.pallas_call(
        _dense_kernel,
        grid=(rows // tm,),
        in_specs=[
            pl.BlockSpec((tm, D_MODEL), lambda i: (i, 0)),
            pl.BlockSpec((tm, D_MODEL), lambda i: (i, 0)),
            wspec((D_MODEL, D_MODEL)),
            wspec((1, D_MODEL)), wspec((1, D_MODEL)),
            wspec((D_MODEL, 2 * D_FF)),
            wspec((D_FF, D_MODEL)),
            wspec((1, D_MODEL)), wspec((1, D_MODEL)),
        ],
        out_specs=pl.BlockSpec((tm, D_MODEL), lambda i: (i, 0)),
        out_shape=jax.ShapeDtypeStruct((rows, D_MODEL), F32),
        compiler_params=pltpu.CompilerParams(
            dimension_semantics=("arbitrary",),
            vmem_limit_bytes=VMEM_LIMIT),
        name="dense",
    )(x2d, mix2d, wo, g1, b1, wfi, wfo, g2, b2)


def kernel(x_prompt, x_sample, state_mlstm_C, state_mlstm_n, state_mlstm_m, cache_swa_k, cache_swa_v,
           w_in, b_igate, b_fgate, ml_norm_g, swa_sinks, rel_bias, w_out, ln1_g, ln1_b,
           w_ffn_in, w_ffn_out, ln2_g, ln2_b):
    bp, seq, _ = x_prompt.shape
    bs, tdec, _ = x_sample.shape
    l = 0

    w16 = jnp.pad(w_in[l], ((0, 0), (0, W_IN_PAD - w_in.shape[-1]))).astype(BF16)
    gvec = jnp.concatenate([b_igate[l], b_fgate[l]]).astype(F32)
    gb = jnp.pad(gvec, (0, BLK - 2 * ML_HEADS))[None, :]
    gbc = jnp.broadcast_to(jnp.pad(gvec, (0, T_GROWS - 2 * ML_HEADS))[:, None], (T_GROWS, BLK))
    mlg = ml_norm_g[l][None, :].astype(F32)
    sinks = swa_sinks[l].astype(F32)
    rb = rel_bias.astype(F32)
    wo = w_out[l].astype(BF16)
    wfi = w_ffn_in[l].astype(BF16)
    wfo = w_ffn_out[l].astype(BF16)
    g1, b1 = ln1_g[l][None, :], ln1_b[l][None, :]
    g2, b2 = ln2_g[l][None, :], ln2_b[l][None, :]

    mix_p, c_p, n_p, m_p, k_p, v_p, wt, wn = _prompt_mixer(x_prompt, w16, gbc, mlg, sinks, rb)
    y_p = _dense(x_prompt.reshape(bp * seq, D_MODEL), mix_p.reshape(bp * seq, D_MODEL),
                 wo, g1, b1, wfi, wfo, g2, b2).reshape(bp, seq, D_MODEL)
    p_m = m_p[:, :ML_HEADS // 2, ::BLK].reshape(bp, ML_HEADS)[None]
    p_k = k_p.reshape(1, bp, WINDOW, SWA_KV_HEADS, SWA_HD)
    p_v = v_p.reshape(1, bp, WINDOW, SWA_KV_HEADS, SWA_HD)

    xs = x_sample.reshape(bs * tdec, D_MODEL)
    m0 = state_mlstm_m[l].astype(F32)
    mrep = jnp.pad(jnp.repeat(m0, tdec, axis=0), ((0, 0), (ML_HEADS, BLK - 2 * ML_HEADS)))
    wlen = cache_swa_k.shape[2]
    kc = cache_swa_k[l].reshape(bs, wlen, SWA_KV_HEADS * SWA_HD)
    vc = cache_swa_v[l].reshape(bs, wlen, SWA_KV_HEADS * SWA_HD)
    mix_s, c_s, n_s, mo_s, k_s, v_s = _sample_mixer(
        xs, wt, wn, gb, mlg, sinks, rb, mrep,
        state_mlstm_C[l].astype(F32), state_mlstm_n[l].astype(F32), kc, vc)
    y_s = _dense(xs, mix_s, wo, g1, b1, wfi, wfo, g2, b2).reshape(bs, tdec, D_MODEL)
    s_m = mo_s.reshape(bs, tdec, BLK)[:, 0, ML_HEADS:2 * ML_HEADS][None]
    s_k = k_s.reshape(1, bs, wlen, SWA_KV_HEADS, SWA_HD)
    s_v = v_s.reshape(1, bs, wlen, SWA_KV_HEADS, SWA_HD)

    return (y_p, y_s, c_p[None], n_p[None], p_m, p_k, p_v,
            c_s[None], n_s[None], s_m, s_k, s_v)
```

```python
import functools
import math

import jax
import jax.numpy as jnp
from jax import lax
from jax.experimental import pallas as pl
from jax.experimental.pallas import tpu as pltpu

F32 = jnp.float32
BF16 = jnp.bfloat16

D_MODEL = 1024
ML_HEADS = 4
ML_DK = 128
ML_DV = 128
ML_WIDTH = ML_HEADS * ML_DV
GATE_SOFTCAP = 15.0
SWA_HEADS = 8
SWA_KV_HEADS = 2
SWA_GROUP = SWA_HEADS // SWA_KV_HEADS
SWA_HD = 64
SWA_WIDTH = SWA_HEADS * SWA_HD
WINDOW = 128
REL_BUCKETS = 32
REL_MAX_DIST = 128
D_FF = 2816
DEPTH = 1
ALPHA = (2.0 * DEPTH) ** 0.25
LN_EPS = 1e-5
NORM_EPS = 1e-6
NEG_INF = -1e30
IN_SIZES = (512, 512, 512, 512, 4, 4, 512, 128, 128)
ML_SCALE = ML_DK ** -0.5
LOG_ML_SCALE = math.log(ML_SCALE)
SWA_SCALE = SWA_HD ** -0.5

BLK = 128

A_Q, A_K, A_V, A_SQ = 0, 512, 1024, 1536
A_COLS = 2048
B_MO, B_SK, B_SV, B_G = 0, 512, 640, 768
B_COLS = 896

N_K = 0
N_B = 512
T_Q, T_V, T_SQ, T_SV, T_G = 0, 512, 1024, 1536, 1664
T_GROWS = 16
T_COLS = T_G + BLK
W_IN_PAD = -(-sum(IN_SIZES) // BLK) * BLK
STATE_ROWS = ML_DV + 16

TM_PROMPT = 1024
TM_DENSE = 1024
DENSE_SUB = 512
FF_SPLITS = (0, 1536, D_FF)
SAMPLE_GB = 16
SLAB = 16
VMEM_LIMIT = 56 * 1024 * 1024


def _softcap(a):
    return GATE_SOFTCAP * jnp.tanh(a / GATE_SOFTCAP)


def _log_sigmoid(x):
    return jnp.minimum(x, 0.0) - jnp.log1p(jnp.exp(-jnp.abs(x)))


def _layer_norm(z, g, b):
    mu = jnp.mean(z, axis=-1, keepdims=True)
    zc = z - mu
    var = jnp.mean(zc * zc, axis=-1, keepdims=True)
    return zc * lax.rsqrt(var + LN_EPS) * g + b


def _dot(a, b):
    return jnp.dot(a, b, preferred_element_type=F32)


def _dot_nt(a, b):
    return lax.dot_general(a, b, (((1,), (1,)), ((), ())), preferred_element_type=F32)


def _dot_tn(a, b):
    return lax.dot_general(a, b, (((0,), (0,)), ((), ())), preferred_element_type=F32)


def _split3(x):
    hi = x.astype(BF16)
    r1 = x - hi.astype(F32)
    mid = r1.astype(BF16)
    lo = (r1 - mid.astype(F32)).astype(BF16)
    return hi, mid, lo


def _t5_bucket(d):
    n = jnp.maximum(d, 0)
    max_exact = REL_BUCKETS // 2
    nlog = REL_BUCKETS - max_exact
    large = jnp.full(n.shape, max_exact, jnp.int32)
    for k in range(1, nlog):
        thr = math.ceil(max_exact * (REL_MAX_DIST / max_exact) ** (k / nlog))
        large = large + jnp.where(n >= thr, 1, 0)
    return jnp.where(n < max_exact, n, large)


def _prompt_mixer_kernel(x_ref, w_ref, gbc_ref, mlg_ref, sinks_ref, rb_ref, wo32, wfi32, wfo32,
                         mix_ref, c_out, n_out, m_out, ko_ref, vo_ref,
                         wt_out, wn_out, wo16, wfi16, wfo16,
                         wt, kn, pf, vlast, kband, q_t, v_t, sq_t, sv_t, g_t, ct, msc, tbl, wn):
    wo16[...] = wo32[...].astype(BF16)
    wfi16[...] = wfi32[...].astype(BF16)
    wfo16[...] = wfo32[...].astype(BF16)

    b = pl.program_id(0)
    t = pl.program_id(1)
    nt = pl.num_programs(1)
    tm = x_ref.shape[1]
    nblk = tm // BLK
    pairs = SWA_HEADS // 2

    @pl.when((b == 0) & (t == 0))
    def _prepare_weights():
        offs = [0]
        for n in IN_SIZES:
            offs.append(offs[-1] + n)
        o_mq, o_mk, o_mv, o_mo, o_mi, _, o_sq, o_sk, o_sv = offs[:-1]

        def col_blocks(dst_col, src_col, n):
            for i in range(n // BLK):
                wn[:, dst_col + i * BLK:dst_col + (i + 1) * BLK] = (
                    w_ref[:, src_col + i * BLK:src_col + (i + 1) * BLK])

        def row_blocks(dst_row, src_col, n):
            for i in range(n // BLK):
                wt[dst_row + i * BLK:dst_row + (i + 1) * BLK, :] = (
                    w_ref[:, src_col + i * BLK:src_col + (i + 1) * BLK].T)

        col_blocks(N_K, o_mk, ML_HEADS * ML_DK)
        col_blocks(N_B + B_MO, o_mo, ML_WIDTH)
        col_blocks(N_B + B_SK, o_sk, SWA_KV_HEADS * SWA_HD)
        row_blocks(T_Q, o_mq, ML_HEADS * ML_DK)
        row_blocks(T_V, o_mv, ML_WIDTH)
        row_blocks(T_SQ, o_sq, SWA_WIDTH)
        row_blocks(T_SV, o_sv, SWA_KV_HEADS * SWA_HD)
        gates_t = w_ref[:, o_mi:o_mi + BLK].T
        grow = lax.broadcasted_iota(jnp.int32, gates_t.shape, 0)
        wt[T_G:T_G + BLK, :] = jnp.where(grow < 2 * ML_HEADS, gates_t, jnp.zeros_like(gates_t))
        wt_out[...] = wt[...]
        wn_out[...] = wn[...]

    @pl.when((b == 0) & (t == 0))
    def _build_bias_tables():
        r = lax.broadcasted_iota(jnp.int32, (BLK, 2 * BLK), 0)
        ln = lax.broadcasted_iota(jnp.int32, (BLK, 2 * BLK), 1)
        second = ln >= BLK
        qi = jnp.where(second, ln - BLK, ln)
        prev = r > qi
        d = jnp.where(prev, WINDOW + qi - r, qi - r)
        bucket = _t5_bucket(d)
        for p in range(pairs):
            acc = jnp.zeros((BLK, 2 * BLK), F32)
            for k in range(REL_BUCKETS):
                acc = jnp.where(bucket == k, jnp.where(second, rb_ref[k, p + pairs], rb_ref[k, p]), acc)
            tbl[0, p] = jnp.where(prev, NEG_INF, acc)
            tbl[1, p] = acc

    @pl.when(t == 0)
    def _reset_state():
        ct[...] = jnp.zeros_like(ct)
        msc[...] = jnp.zeros_like(msc)
        kband[0:BLK, :] = jnp.zeros((BLK, kband.shape[1]), kband.dtype)
        sv_t[0] = jnp.zeros(sv_t.shape[1:], sv_t.dtype)

    xb = x_ref[0].astype(BF16)
    res_n = _dot(xb, wn[...])
    kn[...] = res_n[:, N_K:N_K + ML_HEADS * ML_DK].astype(BF16)
    pf[...] = res_n[:, N_B:N_B + B_SV]
    kband[BLK:BLK + tm, :] = pf[:, B_SK:B_SK + BLK].astype(BF16)

    res_t = _dot_nt(wt[0:T_G + T_GROWS, :], xb)
    vlast[...] = res_t[T_SV:T_SV + BLK, tm - BLK:tm]

    def put(dst, r0, nrows, off, scale=None):
        res = res_t[r0:r0 + nrows, :]
        if scale is not None:
            res = res * scale
        for jj in range(nblk):
            dst[jj + off] = res[:, jj * BLK:(jj + 1) * BLK].astype(dst.dtype)

    put(q_t, T_Q, ML_HEADS * ML_DK, 0)
    put(v_t, T_V, ML_WIDTH, 0)
    put(sq_t, T_SQ, SWA_WIDTH, 0, SWA_SCALE)
    put(sv_t, T_SV, SWA_KV_HEADS * SWA_HD, 1)
    put(g_t, T_G, T_GROWS, 0)

    row = lax.broadcasted_iota(jnp.int32, (BLK, BLK), 0)
    col = lax.broadcasted_iota(jnp.int32, (BLK, BLK), 1)
    causal_t = row <= col
    triu = causal_t.astype(BF16)
    causal2 = jnp.concatenate([causal_t, causal_t], axis=1)
    r16 = lax.broadcasted_iota(jnp.int32, (16, 2 * BLK), 0)
    ones_rows = (r16 == 0).astype(BF16)
    z128 = jnp.zeros((BLK, BLK), BF16)
    ml_pairs = ML_HEADS // 2
    zero_rows = jnp.zeros((BLK - T_GROWS, BLK), F32)
    gbc = gbc_ref[...]
    mlg = mlg_ref[...]
    lane2 = lax.broadcasted_iota(jnp.int32, (1, 2 * BLK), 1)
    snk_rows = [jnp.where(lane2 >= BLK, sinks_ref[p + pairs], sinks_ref[p]) for p in range(pairs)]
    zq = jnp.zeros((SWA_HD, BLK), BF16)
    prev2 = jnp.concatenate([row > col, row > col], axis=1)

    def pair_row(x0, x1):
        return jnp.concatenate([jnp.broadcast_to(x0, (1, BLK)), jnp.broadcast_to(x1, (1, BLK))], axis=1)

    def block_diag(x0, x1):
        return jnp.concatenate([jnp.concatenate([x0, z128], axis=1),
                                jnp.concatenate([z128, x1], axis=1)], axis=0)

    def block(j, carry):
        r0 = pl.multiple_of(j * BLK, BLK)
        rows = pl.ds(r0, BLK)
        hsl = [slice(h * ML_DK, (h + 1) * ML_DK) for h in range(ML_HEADS)]
        psl = [slice(p * 2 * ML_DK, (p + 1) * 2 * ML_DK) for p in range(ml_pairs)]

        pre_t = _softcap(g_t[j] + gbc)
        hi, mid, lo = _split3(_log_sigmoid(pre_t))
        b_t = _dot(hi, triu) + _dot(mid, triu) + _dot(lo, triu)
        u_t = pltpu.roll(pre_t, ML_HEADS, 0) - b_t
        u_c = jnp.concatenate([u_t, zero_rows], axis=0).T

        k2 = [kn[rows, psl[p]] for p in range(ml_pairs)]
        bdq = [block_diag(q_t[j, hsl[2 * p], :], q_t[j, hsl[2 * p + 1], :]) for p in range(ml_pairs)]
        vte = [jnp.concatenate([jnp.concatenate([v_t[j, hsl[2 * p], :], v_t[j, hsl[2 * p + 1], :]], axis=1),
                                ones_rows], axis=0) for p in range(ml_pairs)]
        kq = [_dot(k2[p], bdq[p]) for p in range(ml_pairs)]

        sel = jnp.where((t == 0) & (j == 0), 0, 1)
        kb = kband[pl.ds(r0, 2 * BLK), :]
        vtb = jnp.concatenate([sv_t[j], sv_t[j + 1]], axis=1)
        sc2 = []
        for p in range(pairs):
            qa = sq_t[j, p * SWA_HD:(p + 1) * SWA_HD, :]
            qb = sq_t[j, (p + pairs) * SWA_HD:(p + pairs + 1) * SWA_HD, :]
            bd = jnp.concatenate([jnp.concatenate([qa, zq], axis=1),
                                  jnp.concatenate([zq, qb], axis=1)], axis=0)
            sc2.append(_dot(kb, bd))

        bds, scl, m_row, m_new, a_p, vwte = [], [], [], [], [], []
        for p in range(ml_pairs):
            h0, h1 = 2 * p, 2 * p + 1
            b_row = pair_row(b_t[ML_HEADS + h0:ML_HEADS + h0 + 1, :], b_t[ML_HEADS + h1:ML_HEADS + h1 + 1, :])
            i_row = pair_row(pre_t[h0:h0 + 1, :], pre_t[h1:h1 + 1, :])
            u_col = jnp.concatenate(
                [jnp.broadcast_to(u_c[:, ML_HEADS + h0:ML_HEADS + h0 + 1], (BLK, BLK)),
                 jnp.broadcast_to(u_c[:, ML_HEADS + h1:ML_HEADS + h1 + 1], (BLK, BLK))], axis=1)
            m_prev = msc[p:p + 1, :]
            log_d = jnp.where(causal2, b_row + u_col, -jnp.inf)
            inter = b_row + m_prev
            mr = jnp.maximum(inter, jnp.max(log_d, axis=0, keepdims=True))
            m_row.append(mr)
            scl.append(jnp.exp(inter - mr))
            s2 = (kq[p] * jnp.exp(log_d - (mr - LOG_ML_SCALE))).astype(BF16)
            bds.append(block_diag(s2[:, :BLK], s2[:, BLK:]))
            b_last = pair_row(b_row[:, BLK - 1:BLK], b_row[:, 2 * BLK - 1:2 * BLK])
            w_row = b_last - b_row + i_row
            w_max = pair_row(jnp.max(w_row[:, :BLK], axis=1, keepdims=True),
                             jnp.max(w_row[:, BLK:], axis=1, keepdims=True))
            mn = jnp.maximum(b_last + m_prev, w_max)
            m_new.append(mn)
            a_p.append(jnp.exp(b_last + m_prev - mn))
            wexp = jnp.exp(w_row - mn) * ML_SCALE
            vwte.append(vte[p] * wexp.astype(BF16))

        pw, den_p = [], []
        for p in range(pairs):
            sc = jnp.where(prev2, sc2[p][:BLK], sc2[p][BLK:]) + tbl[sel, p]
            snk = snk_rows[p]
            mx = jnp.maximum(jnp.max(sc, axis=0, keepdims=True), snk)
            pe = jnp.exp(sc - mx)
            den_p.append(jnp.sum(pe, axis=0, keepdims=True) + jnp.exp(snk - mx))
            pw.append(jnp.concatenate([jnp.where(prev2, pe, 0.0), jnp.where(prev2, 0.0, pe)],
                                      axis=0).astype(BF16))

        c_old = [ct[p] for p in range(ml_pairs)]
        cq = [_dot(c_old[p].astype(BF16), bdq[p]) for p in range(ml_pairs)]
        vs = [_dot(vte[p], bds[p]) for p in range(ml_pairs)]
        upd = [_dot(vwte[p], block_diag(k2[p][:, :BLK], k2[p][:, BLK:])) for p in range(ml_pairs)]
        o2 = [_dot(vtb, pw[p]) for p in range(pairs)]

        for p in range(ml_pairs):
            nd = scl[p] * cq[p] + vs[p]
            num = nd[:ML_DV]
            den = nd[ML_DV:ML_DV + 1]
            hh = num / jnp.maximum(jnp.abs(den), jnp.exp(-m_row[p]))
            hn = hh * lax.rsqrt(jnp.mean(hh * hh, axis=0, keepdims=True) + NORM_EPS)
            for half in range(2):
                h = 2 * p + half
                og = jax.nn.sigmoid(pf[rows, B_MO + h * ML_DV:B_MO + (h + 1) * ML_DV])
                hn_h = hn[:, half * BLK:(half + 1) * BLK].T
                mix_ref[0, rows, hsl[h]] = (hn_h * mlg[:, hsl[h]] * og).astype(mix_ref.dtype)
            ct[p] = a_p[p] * c_old[p] + upd[p]
            msc[p:p + 1, :] = m_new[p]

        o_parts = [None] * SWA_HEADS
        for p in range(pairs):
            on = o2[p] / den_p[p]
            o_parts[p] = on[0:SWA_HD, 0:BLK]
            o_parts[p + pairs] = on[SWA_HD:2 * SWA_HD, BLK:2 * BLK]
        o_t = jnp.concatenate(o_parts, axis=0)
        mix_ref[0, rows, ML_WIDTH:ML_WIDTH + SWA_WIDTH] = o_t.T.astype(mix_ref.dtype)
        return carry

    lax.fori_loop(0, nblk, block, 0, unroll=8)

    kband[0:BLK, :] = kband[tm:tm + BLK, :]
    sv_t[0] = sv_t[nblk]

    @pl.when(t == nt - 1)
    def _write_state():
        for h in range(ML_HEADS):
            p, lanes = h // 2, slice((h % 2) * ML_DK, (h % 2 + 1) * ML_DK)
            c_out[0, h] = ct[p, 0:ML_DV, lanes].T
            n_out[0, h:h + 1, :] = ct[p, ML_DV:ML_DV + 1, lanes]
        m_out[0] = msc[...]
        ko_ref[0] = pf[tm - BLK:tm, B_SK:B_SK + BLK]
        vo_ref[0] = vlast[...].T


def _prompt_mixer(x, w_in_l, gbc, mlg, sinks, rb, wo32, wfi32, wfo32):
    bsz, seq, _ = x.shape
    tm = TM_PROMPT
    nt = seq // tm
    nblk = tm // BLK
    steps = bsz * nt
    const2 = lambda b, t: (0, 0)
    smem = pl.BlockSpec(memory_space=pltpu.SMEM)

    def slab_spec(wmat):
        return pl.BlockSpec((wmat.shape[0] // steps, wmat.shape[1]), lambda b, t: (b * nt + t, 0))

    return pl.pallas_call(
        _prompt_mixer_kernel,
        grid=(bsz, nt),
        in_specs=[
            pl.BlockSpec((1, tm, D_MODEL), lambda b, t: (b, t, 0)),
            pl.BlockSpec((D_MODEL, W_IN_PAD), const2, pipeline_mode=pl.Buffered(1)),
            pl.BlockSpec((T_GROWS, BLK), const2),
            pl.BlockSpec((1, ML_WIDTH), const2),
            smem, smem,
            slab_spec(wo32), slab_spec(wfi32), slab_spec(wfo32),
        ],
        out_specs=[
            pl.BlockSpec((1, tm, D_MODEL), lambda b, t: (b, t, 0)),
            pl.BlockSpec((1, ML_HEADS, ML_DK, ML_DV), lambda b, t: (b, 0, 0, 0)),
            pl.BlockSpec((1, ML_HEADS, ML_DK), lambda b, t: (b, 0, 0)),
            pl.BlockSpec((1, 8, 2 * BLK), lambda b, t: (b, 0, 0)),
            pl.BlockSpec((1, BLK, BLK), lambda b, t: (b, 0, 0)),
            pl.BlockSpec((1, BLK, BLK), lambda b, t: (b, 0, 0)),
            pl.BlockSpec((T_COLS, D_MODEL), const2),
            pl.BlockSpec((D_MODEL, N_B + B_SV), const2),
            slab_spec(wo32), slab_spec(wfi32), slab_spec(wfo32),
        ],
        out_shape=[
            jax.ShapeDtypeStruct((bsz, seq, D_MODEL), BF16),
            jax.ShapeDtypeStruct((bsz, ML_HEADS, ML_DK, ML_DV), F32),
            jax.ShapeDtypeStruct((bsz, ML_HEADS, ML_DK), F32),
            jax.ShapeDtypeStruct((bsz, 8, 2 * BLK), F32),
            jax.ShapeDtypeStruct((bsz, BLK, BLK), F32),
            jax.ShapeDtypeStruct((bsz, BLK, BLK), F32),
            jax.ShapeDtypeStruct((T_COLS, D_MODEL), BF16),
            jax.ShapeDtypeStruct((D_MODEL, N_B + B_SV), BF16),
            jax.ShapeDtypeStruct(wo32.shape, BF16),
            jax.ShapeDtypeStruct(wfi32.shape, BF16),
            jax.ShapeDtypeStruct(wfo32.shape, BF16),
        ],
        scratch_shapes=[
            pltpu.VMEM((T_COLS, D_MODEL), BF16),
            pltpu.VMEM((tm, ML_HEADS * ML_DK), BF16),
            pltpu.VMEM((tm, B_SV), F32),
            pltpu.VMEM((BLK, BLK), F32),
            pltpu.VMEM((tm + BLK, BLK), BF16),
            pltpu.VMEM((nblk, ML_HEADS * ML_DK, BLK), BF16),
            pltpu.VMEM((nblk, ML_WIDTH, BLK), BF16),
            pltpu.VMEM((nblk, SWA_WIDTH, BLK), BF16),
            pltpu.VMEM((nblk + 1, SWA_KV_HEADS * SWA_HD, BLK), BF16),
            pltpu.VMEM((nblk, T_GROWS, BLK), F32),
            pltpu.VMEM((ML_HEADS // 2, STATE_ROWS, 2 * ML_DK), F32),
            pltpu.VMEM((8, 2 * BLK), F32),
            pltpu.VMEM((2, SWA_HEADS // 2, BLK, 2 * BLK), F32),
            pltpu.VMEM((D_MODEL, N_B + B_SV), BF16),
        ],
        compiler_params=pltpu.CompilerParams(
            dimension_semantics=("arbitrary", "arbitrary"),
            vmem_limit_bytes=VMEM_LIMIT),
        name="prompt_mixer",
    )(x, w_in_l, gbc, mlg, sinks, rb, wo32, wfi32, wfo32)


def _roll_rows(x, shift):
    return pltpu.roll(x, shift % x.shape[0], 0)


def _sample_mixer_kernel(x_ref, wt_ref, wn_ref, gb_ref, mlg_ref, sinks_ref, rb_ref, mrep_ref,
                         c_ref, n_ref, kc_ref, vc_ref,
                         mix_ref, c_out, n_out, mo_ref, ko_ref, vo_ref,
                         pa, pf, tblc, tbln):
    step = pl.program_id(0)
    tdec = 4
    rows_step = mix_ref.shape[0]
    nslab = rows_step // SLAB
    nseq = SLAB // tdec
    qrows = SWA_HEADS * SLAB

    @pl.when(step == 0)
    def _project_all_rows():
        xb = x_ref[...].astype(BF16)
        pa[:, A_Q:A_K] = _dot_nt(xb, wt_ref[T_Q:T_Q + ML_HEADS * ML_DK, :])
        pa[:, A_K:A_V] = _dot(xb, wn_ref[:, N_K:N_K + ML_HEADS * ML_DK])
        pa[:, A_V:A_COLS] = _dot_nt(xb, wt_ref[T_V:T_V + ML_WIDTH + SWA_WIDTH, :])
        pf[:, B_MO:B_SV] = _dot(xb, wn_ref[:, N_B:N_B + B_SV])
        pf[:, B_SV:B_COLS] = _dot_nt(xb, wt_ref[T_SV:T_COLS, :])

    @pl.when(step == 0)
    def _build_bias_tables():
        r = lax.broadcasted_iota(jnp.int32, (qrows, BLK), 0)
        c = lax.broadcasted_iota(jnp.int32, (qrows, BLK), 1)
        hq = r // SLAB
        bl = (r % SLAB) // tdec
        tq = r % tdec
        d_cache = WINDOW + tq - c
        d_new = tq - (c % tdec)
        key_cache = _t5_bucket(d_cache) * SWA_HEADS + hq
        key_new = _t5_bucket(d_new) * SWA_HEADS + hq
        acc_c = jnp.zeros((qrows, BLK), F32)
        acc_n = jnp.zeros((qrows, BLK), F32)
        for k in range(REL_BUCKETS):
            for h in range(SWA_HEADS):
                val = rb_ref[k, h]
                acc_c = jnp.where(key_cache == k * SWA_HEADS + h, val, acc_c)
                acc_n = jnp.where(key_new == k * SWA_HEADS + h, val, acc_n)
        tblc[...] = jnp.where(c > tq, acc_c, NEG_INF)
        tbln[...] = jnp.where((c < SLAB) & ((c // tdec) == bl) & (d_new >= 0), acc_n, NEG_INF)

    gbias = gb_ref[...]
    mlg = mlg_ref[...]
    r16 = lax.broadcasted_iota(jnp.int32, (SLAB, BLK), 0)
    rr = r16 % tdec
    bl16 = r16 // tdec
    rq = lax.broadcasted_iota(jnp.int32, (qrows, BLK), 0)
    blq = (rq % SLAB) // tdec
    hq_col = rq[:, 0:1] // SLAB
    snk = jnp.zeros((qrows, 1), F32)
    for h in range(SWA_HEADS):
        snk = jnp.where(hq_col == h, sinks_ref[h], snk)
    first_group = rq[:, 0:SWA_HD] < SWA_GROUP * SLAB
    zq = jnp.zeros((SLAB, SWA_HD), F32)
    zeros_pad = jnp.zeros((BLK - SLAB, BLK), BF16)

    def seg_last(x):
        return jnp.where(rr == 3, x,
                         jnp.where(rr == 2, _roll_rows(x, -1),
                                   jnp.where(rr == 1, _roll_rows(x, -2), _roll_rows(x, -3))))

    def seg_max(x):
        m1 = jnp.maximum(x, jnp.where(rr % 2 == 0, _roll_rows(x, -1), _roll_rows(x, 1)))
        return jnp.maximum(m1, jnp.where(rr < 2, _roll_rows(m1, -2), _roll_rows(m1, 2)))

    def slab(si):
        rows = pl.ds(si * SLAB, SLAB)
        prow = pl.ds(pl.multiple_of(step * rows_step + si * SLAB, SLAB), SLAB)
        seq0 = si * nseq

        pre = _softcap(pf[prow, B_G:B_G + BLK] + gbias)
        logf = _log_sigmoid(pre)
        y = logf + jnp.where(rr >= 1, _roll_rows(logf, 1), 0.0)
        bcum = y + jnp.where(rr >= 2, _roll_rows(y, 2), 0.0)
        ig = pltpu.roll(pre, ML_HEADS, 1)
        m_prev = mrep_ref[rows, :]
        b_minus_i = bcum - ig
        log_d = [jnp.where(rr >= dl, bcum - _roll_rows(b_minus_i, dl), -jnp.inf) for dl in range(tdec)]
        rowmax = functools.reduce(jnp.maximum, log_d)
        inter = bcum + m_prev
        m_row = jnp.maximum(inter, rowmax)
        scl = jnp.exp(inter - m_row)
        dm = [jnp.exp(ld - m_row) for ld in log_d]
        enm = jnp.exp(-m_row)
        b_last = seg_last(bcum)
        w = b_last - bcum + ig
        m_new = jnp.maximum(b_last + m_prev, seg_max(w))
        a = jnp.exp(b_last + m_prev - m_new)
        kws = jnp.exp(w - m_new) * ML_SCALE
        mo_ref[rows, :] = m_new

        q = [pa[prow, A_Q + h * ML_DK:A_Q + (h + 1) * ML_DK] for h in range(ML_HEADS)]
        k = [pa[prow, A_K + h * ML_DK:A_K + (h + 1) * ML_DK] for h in range(ML_HEADS)]
        v = [pa[prow, A_V + h * ML_DV:A_V + (h + 1) * ML_DV] for h in range(ML_HEADS)]
        c_b = [[c_ref[seq0 + bl, h] for bl in range(nseq)] for h in range(ML_HEADS)]
        qc, upd, kw = [], [], []
        for h in range(ML_HEADS):
            lane = ML_HEADS + h
            qb = q[h].astype(BF16)
            res = [_dot(qb, jnp.concatenate([c_b[h][2 * i].astype(BF16), c_b[h][2 * i + 1].astype(BF16)], axis=1))
                   for i in range(nseq // 2)]
            parts = [res[bl // 2][:, (bl % 2) * ML_DV:(bl % 2 + 1) * ML_DV] for bl in range(nseq)]
            qc_h = parts[nseq - 1]
            for bl in range(nseq - 2, -1, -1):
                qc_h = jnp.where(bl16 == bl, parts[bl], qc_h)
            qc.append(qc_h)
            kw_h = k[h] * kws[:, lane:lane + 1]
            kw.append(kw_h)
            vm = jnp.concatenate([jnp.where(bl16 == bl, v[h], 0.0) for bl in range(nseq)], axis=1)
            upd.append(_dot_tn(kw_h.astype(BF16), vm.astype(BF16)))

        knew = pf[prow, B_SK:B_SK + BLK]
        vnew = pf[prow, B_SV:B_SV + BLK]
        qparts = []
        for hq in range(SWA_HEADS):
            qh = pa[prow, A_SQ + hq * SWA_HD:A_SQ + (hq + 1) * SWA_HD]
            qparts.append(jnp.concatenate([qh, zq] if hq < SWA_GROUP else [zq, qh], axis=1))
        qs = jnp.concatenate(qparts, axis=0).astype(BF16)
        s_c = _dot_nt(qs, kc_ref[seq0 + nseq - 1].astype(BF16))
        for bl in range(nseq - 2, -1, -1):
            s_c = jnp.where(blq == bl, _dot_nt(qs, kc_ref[seq0 + bl].astype(BF16)), s_c)
        kn = jnp.concatenate([knew.astype(BF16), zeros_pad], axis=0)
        s_n = _dot_nt(qs, kn)

        yield
        for h in range(ML_HEADS):
            hs = slice(h * ML_DK, (h + 1) * ML_DK)
            lane = ML_HEADS + h
            bm = jnp.zeros((SLAB, ML_DV), F32)
            sum_s = jnp.zeros((SLAB, 1), F32)
            for dl in range(tdec):
                s_dl = jnp.sum(q[h] * _roll_rows(k[h], dl), axis=-1, keepdims=True)
                ws = s_dl * ML_SCALE * dm[dl][:, lane:lane + 1]
                bm = bm + ws * _roll_rows(v[h], dl)
                sum_s = sum_s + ws
            nsel = jnp.zeros((SLAB, ML_DK), F32)
            for bl in range(nseq):
                msk = bl16 == bl
                n_b = n_ref[seq0 + bl][h:h + 1, :]
                nsel = jnp.where(msk, n_b, nsel)
                a_b = a[bl * tdec:bl * tdec + 1, lane:lane + 1]
                c_out[seq0 + bl, h] = a_b * c_b[h][bl] + upd[h][:, bl * ML_DV:(bl + 1) * ML_DV]
                n_out[seq0 + bl, h:h + 1, :] = a_b * n_b + jnp.sum(jnp.where(msk, kw[h], 0.0), axis=0,
                                                                   keepdims=True)
            sclh = scl[:, lane:lane + 1]
            num = sclh * qc[h] + bm
            den = sclh * jnp.sum(q[h] * nsel, axis=-1, keepdims=True) + sum_s
            hh = num / jnp.maximum(jnp.abs(den), enm[:, lane:lane + 1])
            hn = hh * lax.rsqrt(jnp.mean(hh * hh, axis=-1, keepdims=True) + NORM_EPS)
            og = jax.nn.sigmoid(pf[prow, B_MO + h * ML_DV:B_MO + (h + 1) * ML_DV])
            mix_ref[rows, hs] = (hn * mlg[:, hs] * og).astype(mix_ref.dtype)

        s_c = s_c * SWA_SCALE + tblc[...]
        s_n = s_n * SWA_SCALE + tbln[...]
        mx = jnp.maximum(jnp.maximum(jnp.max(s_c, axis=-1, keepdims=True),
                                     jnp.max(s_n, axis=-1, keepdims=True)), snk)
        p_c = jnp.exp(s_c - mx)
        p_n = jnp.exp(s_n - mx)
        den = (jnp.sum(p_c, axis=-1, keepdims=True) + jnp.sum(p_n, axis=-1, keepdims=True)
               + jnp.exp(snk - mx))

        yield
        vn = jnp.concatenate([vnew.astype(BF16), zeros_pad], axis=0)
        o = _dot(p_n.astype(BF16), vn)
        for bl in range(nseq):
            o = o + _dot(jnp.where(blq == bl, p_c, 0.0).astype(BF16), vc_ref[seq0 + bl].astype(BF16))
        o = o / den
        osel = jnp.where(first_group, o[:, 0:SWA_HD], o[:, SWA_HD:2 * SWA_HD])
        og = jnp.concatenate([osel[hq * SLAB:(hq + 1) * SLAB, :] for hq in range(SWA_HEADS)], axis=1)
        mix_ref[rows, ML_WIDTH:ML_WIDTH + SWA_WIDTH] = og.astype(mix_ref.dtype)

        for bl in range(nseq):
            ko_ref[seq0 + bl, 0:WINDOW - tdec, :] = kc_ref[seq0 + bl, tdec:WINDOW, :]
            vo_ref[seq0 + bl, 0:WINDOW - tdec, :] = vc_ref[seq0 + bl, tdec:WINDOW, :]
            ko_ref[seq0 + bl, WINDOW - tdec:WINDOW, :] = knew[bl * tdec:(bl + 1) * tdec, :]
            vo_ref[seq0 + bl, WINDOW - tdec:WINDOW, :] = vnew[bl * tdec:(bl + 1) * tdec, :]
        yield

    slabs = [slab(si) for si in range(nslab)]
    for _ in range(3):
        for s in slabs:
            next(s)


def _sample_mixer(x2d, wt, wn, gb, mlg, sinks, rb, mrep, c0, n0, kc, vc):
    rows = x2d.shape[0]
    tdec = 4
    nseq = rows // tdec
    gbs = SAMPLE_GB
    steps = nseq // gbs
    rb_rows = gbs * tdec
    const2 = lambda i: (0, 0)
    smem = pl.BlockSpec(memory_space=pltpu.SMEM)
    return pl.pallas_call(
        _sample_mixer_kernel,
        grid=(steps,),
        in_specs=[
            pl.BlockSpec((rows, D_MODEL), const2, pipeline_mode=pl.Buffered(1)),
            pl.BlockSpec((T_COLS, D_MODEL), const2, pipeline_mode=pl.Buffered(1)),
            pl.BlockSpec((D_MODEL, N_B + B_SV), const2, pipeline_mode=pl.Buffered(1)),
            pl.BlockSpec((1, BLK), const2),
            pl.BlockSpec((1, ML_WIDTH), const2),
            smem, smem,
            pl.BlockSpec((rb_rows, BLK), lambda i: (i, 0)),
            pl.BlockSpec((gbs, ML_HEADS, ML_DK, ML_DV), lambda i: (i, 0, 0, 0)),
            pl.BlockSpec((gbs, ML_HEADS, ML_DK), lambda i: (i, 0, 0)),
            pl.BlockSpec((gbs, WINDOW, BLK), lambda i: (i, 0, 0)),
            pl.BlockSpec((gbs, WINDOW, BLK), lambda i: (i, 0, 0)),
        ],
        out_specs=[
            pl.BlockSpec((rb_rows, D_MODEL), lambda i: (i, 0)),
            pl.BlockSpec((gbs, ML_HEADS, ML_DK, ML_DV), lambda i: (i, 0, 0, 0)),
            pl.BlockSpec((gbs, ML_HEADS, ML_DK), lambda i: (i, 0, 0)),
            pl.BlockSpec((rb_rows, BLK), lambda i: (i, 0)),
            pl.BlockSpec((gbs, WINDOW, BLK), lambda i: (i, 0, 0)),
            pl.BlockSpec((gbs, WINDOW, BLK), lambda i: (i, 0, 0)),
        ],
        out_shape=[
            jax.ShapeDtypeStruct((rows, D_MODEL), BF16),
            jax.ShapeDtypeStruct((nseq, ML_HEADS, ML_DK, ML_DV), F32),
            jax.ShapeDtypeStruct((nseq, ML_HEADS, ML_DK), F32),
            jax.ShapeDtypeStruct((rows, BLK), F32),
            jax.ShapeDtypeStruct((nseq, WINDOW, BLK), F32),
            jax.ShapeDtypeStruct((nseq, WINDOW, BLK), F32),
        ],
        scratch_shapes=[
            pltpu.VMEM((rows, A_COLS), F32),
            pltpu.VMEM((rows, B_COLS), F32),
            pltpu.VMEM((SWA_HEADS * SLAB, BLK), F32),
            pltpu.VMEM((SWA_HEADS * SLAB, BLK), F32),
        ],
        compiler_params=pltpu.CompilerParams(
            dimension_semantics=("arbitrary",),
            vmem_limit_bytes=VMEM_LIMIT),
        name="sample_mixer",
    )(x2d, wt, wn, gb, mlg, sinks, rb, mrep, c0, n0, kc, vc)


def _dense_kernel(x_ref, mix_ref, wo_ref, g1_ref, b1_ref, wfi_ref, wfo_ref, g2_ref, b2_ref, y_ref):
    nsub = x_ref.shape[0] // DENSE_SUB
    subs = [slice(i * DENSE_SUB, (i + 1) * DENSE_SUB) for i in range(nsub)]
    chunks = list(zip(FF_SPLITS[:-1], FF_SPLITS[1:]))
    g1, b1, g2, b2 = g1_ref[...], b1_ref[...], g2_ref[...], b2_ref[...]

    proj = [_dot(mix_ref[s, :], wo_ref[...]) for s in subs]
    h1 = [_layer_norm(ALPHA * x_ref[s, :] + proj[i], g1, b1) for i, s in enumerate(subs)]
    h1b = [h.astype(BF16) for h in h1]
    acc = [None] * nsub
    for c0, c1 in chunks:
        gate = [_dot(h1b[i], wfi_ref[:, c0:c1]) for i in range(nsub)]
        up = [_dot(h1b[i], wfi_ref[:, D_FF + c0:D_FF + c1]) for i in range(nsub)]
        for i in range(nsub):
            act = (gate[i] * jax.nn.sigmoid(gate[i]) * up[i]).astype(BF16)
            part = _dot(act, wfo_ref[c0:c1, :])
            acc[i] = part if acc[i] is None else acc[i] + part
    for i, s in enumerate(subs):
        y_ref[s, :] = _layer_norm(ALPHA * h1[i] + acc[i], g2, b2)


def _dense(x2d, mix2d, wo, g1, b1, wfi, wfo, g2, b2):
    rows = x2d.shape[0]
    tm = min(TM_DENSE, rows)
    const2 = lambda i: (0, 0)

    def wspec(shape):
        return pl.BlockSpec(shape, const2, pipeline_mode=pl.Buffered(1))

    return pl.pallas_call(
        _dense_kernel,
        grid=(rows // tm,),
        in_specs=[
            pl.BlockSpec((tm, D_MODEL), lambda i: (i, 0)),
            pl.BlockSpec((tm, D_MODEL), lambda i: (i, 0)),
            wspec((D_MODEL, D_MODEL)),
            wspec((1, D_MODEL)), wspec((1, D_MODEL)),
            wspec((D_MODEL, 2 * D_FF)),
            wspec((D_FF, D_MODEL)),
            wspec((1, D_MODEL)), wspec((1, D_MODEL)),
        ],
        out_specs=pl.BlockSpec((tm, D_MODEL), lambda i: (i, 0)),
        out_shape=jax.ShapeDtypeStruct((rows, D_MODEL), F32),
        compiler_params=pltpu.CompilerParams(
            dimension_semantics=("arbitrary",),
            vmem_limit_bytes=VMEM_LIMIT),
        name="dense",
    )(x2d, mix2d, wo, g1, b1, wfi, wfo, g2, b2)


def kernel(x_prompt, x_sample, state_mlstm_C, state_mlstm_n, state_mlstm_m, cache_swa_k, cache_swa_v,
           w_in, b_igate, b_fgate, ml_norm_g, swa_sinks, rel_bias, w_out, ln1_g, ln1_b,
           w_ffn_in, w_ffn_out, ln2_g, ln2_b):
    bp, seq, _ = x_prompt.shape
    bs, tdec, _ = x_sample.shape
    l = 0

    w16 = jnp.pad(w_in[l], ((0, 0), (0, W_IN_PAD - w_in.shape[-1]))).astype(BF16)
    gvec = jnp.concatenate([b_igate[l], b_fgate[l]]).astype(F32)
    gb = jnp.pad(gvec, (0, BLK - 2 * ML_HEADS))[None, :]
    gbc = jnp.broadcast_to(jnp.pad(gvec, (0, T_GROWS - 2 * ML_HEADS))[:, None], (T_GROWS, BLK))
    mlg = ml_norm_g[l][None, :].astype(F32)
    sinks = swa_sinks[l].astype(F32)
    rb = rel_bias.astype(F32)
    g1, b1 = ln1_g[l][None, :], ln1_b[l][None, :]
    g2, b2 = ln2_g[l][None, :], ln2_b[l][None, :]

    mix_p, c_p, n_p, m_p, k_p, v_p, wt, wn, wo, wfi, wfo = _prompt_mixer(
        x_prompt, w16, gbc, mlg, sinks, rb, w_out[l], w_ffn_in[l], w_ffn_out[l])
    y_p = _dense(x_prompt.reshape(bp * seq, D_MODEL), mix_p.reshape(bp * seq, D_MODEL),
                 wo, g1, b1, wfi, wfo, g2, b2).reshape(bp, seq, D_MODEL)
    p_m = m_p[:, :ML_HEADS // 2, ::BLK].reshape(bp, ML_HEADS)[None]
    p_k = k_p.reshape(1, bp, WINDOW, SWA_KV_HEADS, SWA_HD)
    p_v = v_p.reshape(1, bp, WINDOW, SWA_KV_HEADS, SWA_HD)

    xs = x_sample.reshape(bs * tdec, D_MODEL)
    m0 = state_mlstm_m[l].astype(F32)
    mrep = jnp.pad(jnp.repeat(m0, tdec, axis=0), ((0, 0), (ML_HEADS, BLK - 2 * ML_HEADS)))
    wlen = cache_swa_k.shape[2]
    kc = cache_swa_k[l].reshape(bs, wlen, SWA_KV_HEADS * SWA_HD)
    vc = cache_swa_v[l].reshape(bs, wlen, SWA_KV_HEADS * SWA_HD)
    mix_s, c_s, n_s, mo_s, k_s, v_s = _sample_mixer(
        xs, wt, wn, gb, mlg, sinks, rb, mrep,
        state_mlstm_C[l].astype(F32), state_mlstm_n[l].astype(F32), kc, vc)
    y_s = _dense(xs, mix_s, wo, g1, b1, wfi, wfo, g2, b2).reshape(bs, tdec, D_MODEL)
    s_m = mo_s.reshape(bs, tdec, BLK)[:, 0, ML_HEADS:2 * ML_HEADS][None]
    s_k = k_s.reshape(1, bs, wlen, SWA_KV_HEADS, SWA_HD)
    s_v = v_s.reshape(1, bs, wlen, SWA_KV_HEADS, SWA_HD)

    return (y_p, y_s, c_p[None], n_p[None], p_m, p_k, p_v,
            c_s[None], n_s[None], s_m, s_k, s_v)
```

```python
import functools
import math

import jax
import jax.numpy as jnp
from jax import lax
from jax.experimental import pallas as pl
from jax.experimental.pallas import tpu as pltpu

F32 = jnp.float32
BF16 = jnp.bfloat16

D_MODEL = 1024
ML_HEADS = 4
ML_DK = 128
ML_DV = 128
ML_WIDTH = ML_HEADS * ML_DV
GATE_SOFTCAP = 15.0
SWA_HEADS = 8
SWA_KV_HEADS = 2
SWA_GROUP = SWA_HEADS // SWA_KV_HEADS
SWA_HD = 64
SWA_WIDTH = SWA_HEADS * SWA_HD
WINDOW = 128
REL_BUCKETS = 32
REL_MAX_DIST = 128
D_FF = 2816
DEPTH = 1
ALPHA = (2.0 * DEPTH) ** 0.25
LN_EPS = 1e-5
NORM_EPS = 1e-6
NEG_INF = -1e30
IN_SIZES = (512, 512, 512, 512, 4, 4, 512, 128, 128)
ML_SCALE = ML_DK ** -0.5
LOG_ML_SCALE = math.log(ML_SCALE)
SWA_SCALE = SWA_HD ** -0.5

BLK = 128

A_Q, A_K, A_V, A_SQ = 0, 512, 1024, 1536
A_COLS = 2048
B_MO, B_SK, B_SV, B_G = 0, 512, 640, 768
B_COLS = 896

N_K = 0
N_B = 512
T_Q, T_V, T_SQ, T_SV, T_G = 0, 512, 1024, 1536, 1664
T_GROWS = 16
T_COLS = T_G + BLK
W_IN_PAD = -(-sum(IN_SIZES) // BLK) * BLK
STATE_ROWS = ML_DV + 16

TM_PROMPT = 1024
TM_DENSE = 1024
DENSE_SUB = 512
FF_SPLITS = (0, 1536, D_FF)
SAMPLE_GB = 16
SLAB = 16
VMEM_LIMIT = 56 * 1024 * 1024


def _softcap(a):
    return GATE_SOFTCAP * jnp.tanh(a / GATE_SOFTCAP)


def _log_sigmoid(x):
    return jnp.minimum(x, 0.0) - jnp.log1p(jnp.exp(-jnp.abs(x)))


def _layer_norm(z, g, b):
    mu = jnp.mean(z, axis=-1, keepdims=True)
    zc = z - mu
    var = jnp.mean(zc * zc, axis=-1, keepdims=True)
    return zc * lax.rsqrt(var + LN_EPS) * g + b


def _dot(a, b):
    return jnp.dot(a, b, preferred_element_type=F32)


def _dot_nt(a, b):
    return lax.dot_general(a, b, (((1,), (1,)), ((), ())), preferred_element_type=F32)


def _dot_tn(a, b):
    return lax.dot_general(a, b, (((0,), (0,)), ((), ())), preferred_element_type=F32)


def _split3(x):
    hi = x.astype(BF16)
    r1 = x - hi.astype(F32)
    mid = r1.astype(BF16)
    lo = (r1 - mid.astype(F32)).astype(BF16)
    return hi, mid, lo


def _t5_bucket(d):
    n = jnp.maximum(d, 0)
    max_exact = REL_BUCKETS // 2
    nlog = REL_BUCKETS - max_exact
    large = jnp.full(n.shape, max_exact, jnp.int32)
    for k in range(1, nlog):
        thr = math.ceil(max_exact * (REL_MAX_DIST / max_exact) ** (k / nlog))
        large = large + jnp.where(n >= thr, 1, 0)
    return jnp.where(n < max_exact, n, large)


def _prompt_mixer_kernel(x_ref, w_ref, gbc_ref, mlg_ref, sinks_ref, rb_ref, wo32, wfi32, wfo32,
                         mix_ref, c_out, n_out, m_out, ko_ref, vo_ref,
                         wt_out, wn_out, wo16, wfi16, wfo16,
                         wt, kn, pf, vlast, kband, q_t, v_t, sq_t, sv_t, g_t, ct, msc, tbl, wn):
    wo16[...] = wo32[...].astype(BF16)
    wfi16[...] = wfi32[...].astype(BF16)
    wfo16[...] = wfo32[...].astype(BF16)

    b = pl.program_id(0)
    t = pl.program_id(1)
    nt = pl.num_programs(1)
    tm = x_ref.shape[1]
    nblk = tm // BLK
    pairs = SWA_HEADS // 2

    @pl.when((b == 0) & (t == 0))
    def _prepare_weights():
        offs = [0]
        for n in IN_SIZES:
            offs.append(offs[-1] + n)
        o_mq, o_mk, o_mv, o_mo, o_mi, _, o_sq, o_sk, o_sv = offs[:-1]

        def col_blocks(dst_col, src_col, n):
            for i in range(n // BLK):
                wn[:, dst_col + i * BLK:dst_col + (i + 1) * BLK] = (
                    w_ref[:, src_col + i * BLK:src_col + (i + 1) * BLK])

        def row_blocks(dst_row, src_col, n):
            for i in range(n // BLK):
                wt[dst_row + i * BLK:dst_row + (i + 1) * BLK, :] = (
                    w_ref[:, src_col + i * BLK:src_col + (i + 1) * BLK].T)

        col_blocks(N_K, o_mk, ML_HEADS * ML_DK)
        col_blocks(N_B + B_MO, o_mo, ML_WIDTH)
        col_blocks(N_B + B_SK, o_sk, SWA_KV_HEADS * SWA_HD)
        row_blocks(T_Q, o_mq, ML_HEADS * ML_DK)
        row_blocks(T_V, o_mv, ML_WIDTH)
        row_blocks(T_SQ, o_sq, SWA_WIDTH)
        row_blocks(T_SV, o_sv, SWA_KV_HEADS * SWA_HD)
        gates_t = w_ref[:, o_mi:o_mi + BLK].T
        grow = lax.broadcasted_iota(jnp.int32, gates_t.shape, 0)
        wt[T_G:T_G + BLK, :] = jnp.where(grow < 2 * ML_HEADS, gates_t, jnp.zeros_like(gates_t))
        wt_out[...] = wt[...]
        wn_out[...] = wn[...]

    @pl.when((b == 0) & (t == 0))
    def _build_bias_tables():
        r = lax.broadcasted_iota(jnp.int32, (BLK, 2 * BLK), 0)
        ln = lax.broadcasted_iota(jnp.int32, (BLK, 2 * BLK), 1)
        second = ln >= BLK
        qi = jnp.where(second, ln - BLK, ln)
        prev = r > qi
        d = jnp.where(prev, WINDOW + qi - r, qi - r)
        bucket = _t5_bucket(d)
        for p in range(pairs):
            acc = jnp.zeros((BLK, 2 * BLK), F32)
            for k in range(REL_BUCKETS):
                acc = jnp.where(bucket == k, jnp.where(second, rb_ref[k, p + pairs], rb_ref[k, p]), acc)
            tbl[0, p] = jnp.where(prev, NEG_INF, acc)
            tbl[1, p] = acc

    @pl.when(t == 0)
    def _reset_state():
        ct[...] = jnp.zeros_like(ct)
        msc[...] = jnp.zeros_like(msc)
        kband[0:BLK, :] = jnp.zeros((BLK, kband.shape[1]), kband.dtype)
        sv_t[0] = jnp.zeros(sv_t.shape[1:], sv_t.dtype)

    xb = x_ref[0].astype(BF16)
    res_n = _dot(xb, wn[...])
    kn[...] = res_n[:, N_K:N_K + ML_HEADS * ML_DK].astype(BF16)
    pf[...] = res_n[:, N_B:N_B + B_SV]
    kband[BLK:BLK + tm, :] = pf[:, B_SK:B_SK + BLK].astype(BF16)

    res_t = _dot_nt(wt[0:T_G + T_GROWS, :], xb)
    vlast[...] = res_t[T_SV:T_SV + BLK, tm - BLK:tm]

    def put(dst, r0, nrows, off, scale=None):
        res = res_t[r0:r0 + nrows, :]
        if scale is not None:
            res = res * scale
        for jj in range(nblk):
            dst[jj + off] = res[:, jj * BLK:(jj + 1) * BLK].astype(dst.dtype)

    put(q_t, T_Q, ML_HEADS * ML_DK, 0)
    put(v_t, T_V, ML_WIDTH, 0)
    put(sq_t, T_SQ, SWA_WIDTH, 0, SWA_SCALE)
    put(sv_t, T_SV, SWA_KV_HEADS * SWA_HD, 1)
    put(g_t, T_G, T_GROWS, 0)

    row = lax.broadcasted_iota(jnp.int32, (BLK, BLK), 0)
    col = lax.broadcasted_iota(jnp.int32, (BLK, BLK), 1)
    causal_t = row <= col
    triu = causal_t.astype(BF16)
    causal2 = jnp.concatenate([causal_t, causal_t], axis=1)
    r16 = lax.broadcasted_iota(jnp.int32, (16, 2 * BLK), 0)
    ones_rows = (r16 == 0).astype(BF16)
    z128 = jnp.zeros((BLK, BLK), BF16)
    ml_pairs = ML_HEADS // 2
    zero_rows = jnp.zeros((BLK - T_GROWS, BLK), F32)
    gbc = gbc_ref[...]
    mlg = mlg_ref[...]
    lane2 = lax.broadcasted_iota(jnp.int32, (1, 2 * BLK), 1)
    snk_rows = [jnp.where(lane2 >= BLK, sinks_ref[p + pairs], sinks_ref[p]) for p in range(pairs)]
    zq = jnp.zeros((SWA_HD, BLK), BF16)
    prev2 = jnp.concatenate([row > col, row > col], axis=1)

    def pair_row(x0, x1):
        return jnp.concatenate([jnp.broadcast_to(x0, (1, BLK)), jnp.broadcast_to(x1, (1, BLK))], axis=1)

    def block_diag(x0, x1):
        return jnp.concatenate([jnp.concatenate([x0, z128], axis=1),
                                jnp.concatenate([z128, x1], axis=1)], axis=0)

    def block(j, carry):
        r0 = pl.multiple_of(j * BLK, BLK)
        rows = pl.ds(r0, BLK)
        hsl = [slice(h * ML_DK, (h + 1) * ML_DK) for h in range(ML_HEADS)]
        psl = [slice(p * 2 * ML_DK, (p + 1) * 2 * ML_DK) for p in range(ml_pairs)]

        pre_t = _softcap(g_t[j] + gbc)
        hi, mid, lo = _split3(_log_sigmoid(pre_t))
        b_t = _dot(hi, triu) + _dot(mid, triu) + _dot(lo, triu)
        u_t = pltpu.roll(pre_t, ML_HEADS, 0) - b_t
        u_c = jnp.concatenate([u_t, zero_rows], axis=0).T

        k2 = [kn[rows, psl[p]] for p in range(ml_pairs)]
        bdq = [block_diag(q_t[j, hsl[2 * p], :], q_t[j, hsl[2 * p + 1], :]) for p in range(ml_pairs)]
        vte = [jnp.concatenate([jnp.concatenate([v_t[j, hsl[2 * p], :], v_t[j, hsl[2 * p + 1], :]], axis=1),
                                ones_rows], axis=0) for p in range(ml_pairs)]
        kq = [_dot(k2[p], bdq[p]) for p in range(ml_pairs)]

        sel = jnp.where((t == 0) & (j == 0), 0, 1)
        kb = kband[pl.ds(r0, 2 * BLK), :]
        vtb = jnp.concatenate([sv_t[j], sv_t[j + 1]], axis=1)
        sc2 = []
        for p in range(pairs):
            qa = sq_t[j, p * SWA_HD:(p + 1) * SWA_HD, :]
            qb = sq_t[j, (p + pairs) * SWA_HD:(p + pairs + 1) * SWA_HD, :]
            bd = jnp.concatenate([jnp.concatenate([qa, zq], axis=1),
                                  jnp.concatenate([zq, qb], axis=1)], axis=0)
            sc2.append(_dot(kb, bd))

        bds, scl, m_row, m_new, a_p, vwte = [], [], [], [], [], []
        for p in range(ml_pairs):
            h0, h1 = 2 * p, 2 * p + 1
            b_row = pair_row(b_t[ML_HEADS + h0:ML_HEADS + h0 + 1, :], b_t[ML_HEADS + h1:ML_HEADS + h1 + 1, :])
            i_row = pair_row(pre_t[h0:h0 + 1, :], pre_t[h1:h1 + 1, :])
            u_col = jnp.concatenate(
                [jnp.broadcast_to(u_c[:, ML_HEADS + h0:ML_HEADS + h0 + 1], (BLK, BLK)),
                 jnp.broadcast_to(u_c[:, ML_HEADS + h1:ML_HEADS + h1 + 1], (BLK, BLK))], axis=1)
            m_prev = msc[p:p + 1, :]
            log_d = jnp.where(causal2, b_row + u_col, -jnp.inf)
            inter = b_row + m_prev
            mr = jnp.maximum(inter, jnp.max(log_d, axis=0, keepdims=True))
            m_row.append(mr)
            scl.append(jnp.exp(inter - mr))
            s2 = (kq[p] * jnp.exp(log_d - (mr - LOG_ML_SCALE))).astype(BF16)
            bds.append(block_diag(s2[:, :BLK], s2[:, BLK:]))
            b_last = pair_row(b_row[:, BLK - 1:BLK], b_row[:, 2 * BLK - 1:2 * BLK])
            w_row = b_last - b_row + i_row
            w_max = pair_row(jnp.max(w_row[:, :BLK], axis=1, keepdims=True),
                             jnp.max(w_row[:, BLK:], axis=1, keepdims=True))
            mn = jnp.maximum(b_last + m_prev, w_max)
            m_new.append(mn)
            a_p.append(jnp.exp(b_last + m_prev - mn))
            wexp = jnp.exp(w_row - mn) * ML_SCALE
            vwte.append(vte[p] * wexp.astype(BF16))

        pw, den_p = [], []
        for p in range(pairs):
            sc = jnp.where(prev2, sc2[p][:BLK], sc2[p][BLK:]) + tbl[sel, p]
            snk = snk_rows[p]
            mx = jnp.maximum(jnp.max(sc, axis=0, keepdims=True), snk)
            pe = jnp.exp(sc - mx)
            den_p.append(jnp.sum(pe, axis=0, keepdims=True) + jnp.exp(snk - mx))
            pw.append(jnp.concatenate([jnp.where(prev2, pe, 0.0), jnp.where(prev2, 0.0, pe)],
                                      axis=0).astype(BF16))

        c_old = [ct[p] for p in range(ml_pairs)]
        cq = [_dot(c_old[p].astype(BF16), bdq[p]) for p in range(ml_pairs)]
        vs = [_dot(vte[p], bds[p]) for p in range(ml_pairs)]
        upd = [_dot(vwte[p], block_diag(k2[p][:, :BLK], k2[p][:, BLK:])) for p in range(ml_pairs)]
        o2 = [_dot(vtb, pw[p]) for p in range(pairs)]

        for p in range(ml_pairs):
            nd = scl[p] * cq[p] + vs[p]
            num = nd[:ML_DV]
            den = nd[ML_DV:ML_DV + 1]
            hh = num / jnp.maximum(jnp.abs(den), jnp.exp(-m_row[p]))
            hn = hh * lax.rsqrt(jnp.mean(hh * hh, axis=0, keepdims=True) + NORM_EPS)
            for half in range(2):
                h = 2 * p + half
                og = jax.nn.sigmoid(pf[rows, B_MO + h * ML_DV:B_MO + (h + 1) * ML_DV])
                hn_h = hn[:, half * BLK:(half + 1) * BLK].T
                mix_ref[0, rows, hsl[h]] = (hn_h * mlg[:, hsl[h]] * og).astype(mix_ref.dtype)
            ct[p] = a_p[p] * c_old[p] + upd[p]
            msc[p:p + 1, :] = m_new[p]

        o_parts = [None] * SWA_HEADS
        for p in range(pairs):
            on = o2[p] / den_p[p]
            o_parts[p] = on[0:SWA_HD, 0:BLK]
            o_parts[p + pairs] = on[SWA_HD:2 * SWA_HD, BLK:2 * BLK]
        o_t = jnp.concatenate(o_parts, axis=0)
        mix_ref[0, rows, ML_WIDTH:ML_WIDTH + SWA_WIDTH] = o_t.T.astype(mix_ref.dtype)
        return carry

    lax.fori_loop(0, nblk, block, 0, unroll=8)

    kband[0:BLK, :] = kband[tm:tm + BLK, :]
    sv_t[0] = sv_t[nblk]

    @pl.when(t == nt - 1)
    def _write_state():
        for h in range(ML_HEADS):
            p, lanes = h // 2, slice((h % 2) * ML_DK, (h % 2 + 1) * ML_DK)
            c_out[0, h] = ct[p, 0:ML_DV, lanes].T
            n_out[0, h:h + 1, :] = ct[p, ML_DV:ML_DV + 1, lanes]
        m_out[0] = msc[...]
        ko_ref[0] = pf[tm - BLK:tm, B_SK:B_SK + BLK]
        vo_ref[0] = vlast[...].T


def _prompt_mixer(x, w_in_l, gbc, mlg, sinks, rb, wo32, wfi32, wfo32):
    bsz, seq, _ = x.shape
    tm = TM_PROMPT
    nt = seq // tm
    nblk = tm // BLK
    steps = bsz * nt
    const2 = lambda b, t: (0, 0)
    smem = pl.BlockSpec(memory_space=pltpu.SMEM)

    def slab_spec(wmat):
        return pl.BlockSpec((wmat.shape[0] // steps, wmat.shape[1]), lambda b, t: (b * nt + t, 0))

    return pl.pallas_call(
        _prompt_mixer_kernel,
        grid=(bsz, nt),
        in_specs=[
            pl.BlockSpec((1, tm, D_MODEL), lambda b, t: (b, t, 0)),
            pl.BlockSpec((D_MODEL, W_IN_PAD), const2, pipeline_mode=pl.Buffered(1)),
            pl.BlockSpec((T_GROWS, BLK), const2),
            pl.BlockSpec((1, ML_WIDTH), const2),
            smem, smem,
            slab_spec(wo32), slab_spec(wfi32), slab_spec(wfo32),
        ],
        out_specs=[
            pl.BlockSpec((1, tm, D_MODEL), lambda b, t: (b, t, 0)),
            pl.BlockSpec((1, ML_HEADS, ML_DK, ML_DV), lambda b, t: (b, 0, 0, 0)),
            pl.BlockSpec((1, ML_HEADS, ML_DK), lambda b, t: (b, 0, 0)),
            pl.BlockSpec((1, 8, 2 * BLK), lambda b, t: (b, 0, 0)),
            pl.BlockSpec((1, BLK, BLK), lambda b, t: (b, 0, 0)),
            pl.BlockSpec((1, BLK, BLK), lambda b, t: (b, 0, 0)),
            pl.BlockSpec((T_COLS, D_MODEL), const2),
            pl.BlockSpec((D_MODEL, N_B + B_SV), const2),
            slab_spec(wo32), slab_spec(wfi32), slab_spec(wfo32),
        ],
        out_shape=[
            jax.ShapeDtypeStruct((bsz, seq, D_MODEL), BF16),
            jax.ShapeDtypeStruct((bsz, ML_HEADS, ML_DK, ML_DV), F32),
            jax.ShapeDtypeStruct((bsz, ML_HEADS, ML_DK), F32),
            jax.ShapeDtypeStruct((bsz, 8, 2 * BLK), F32),
            jax.ShapeDtypeStruct((bsz, BLK, BLK), F32),
            jax.ShapeDtypeStruct((bsz, BLK, BLK), F32),
            jax.ShapeDtypeStruct((T_COLS, D_MODEL), BF16),
            jax.ShapeDtypeStruct((D_MODEL, N_B + B_SV), BF16),
            jax.ShapeDtypeStruct(wo32.shape, BF16),
            jax.ShapeDtypeStruct(wfi32.shape, BF16),
            jax.ShapeDtypeStruct(wfo32.shape, BF16),
        ],
        scratch_shapes=[
            pltpu.VMEM((T_COLS, D_MODEL), BF16),
            pltpu.VMEM((tm, ML_HEADS * ML_DK), BF16),
            pltpu.VMEM((tm, B_SV), F32),
            pltpu.VMEM((BLK, BLK), F32),
            pltpu.VMEM((tm + BLK, BLK), BF16),
            pltpu.VMEM((nblk, ML_HEADS * ML_DK, BLK), BF16),
            pltpu.VMEM((nblk, ML_WIDTH, BLK), BF16),
            pltpu.VMEM((nblk, SWA_WIDTH, BLK), BF16),
            pltpu.VMEM((nblk + 1, SWA_KV_HEADS * SWA_HD, BLK), BF16),
            pltpu.VMEM((nblk, T_GROWS, BLK), F32),
            pltpu.VMEM((ML_HEADS // 2, STATE_ROWS, 2 * ML_DK), F32),
            pltpu.VMEM((8, 2 * BLK), F32),
            pltpu.VMEM((2, SWA_HEADS // 2, BLK, 2 * BLK), F32),
            pltpu.VMEM((D_MODEL, N_B + B_SV), BF16),
        ],
        compiler_params=pltpu.CompilerParams(
            dimension_semantics=("arbitrary", "arbitrary"),
            vmem_limit_bytes=VMEM_LIMIT),
        name="prompt_mixer",
    )(x, w_in_l, gbc, mlg, sinks, rb, wo32, wfi32, wfo32)


def _roll_rows(x, shift):
    return pltpu.roll(x, shift % x.shape[0], 0)


def _sample_mixer_kernel(x_ref, wt_ref, wn_ref, gb_ref, mlg_ref, sinks_ref, rb_ref, mrep_ref,
                         c_ref, n_ref, kc_ref, vc_ref,
                         mix_ref, c_out, n_out, mo_ref, ko_ref, vo_ref,
                         pa, pf, tblc, tbln):
    step = pl.program_id(0)
    tdec = 4
    rows_step = mix_ref.shape[0]
    nslab = rows_step // SLAB
    nseq = SLAB // tdec
    qrows = SWA_HEADS * SLAB

    @pl.when(step == 0)
    def _project_all_rows():
        xb = x_ref[...].astype(BF16)
        pa[:, A_Q:A_K] = _dot_nt(xb, wt_ref[T_Q:T_Q + ML_HEADS * ML_DK, :])
        pa[:, A_K:A_V] = _dot(xb, wn_ref[:, N_K:N_K + ML_HEADS * ML_DK])
        pa[:, A_V:A_COLS] = _dot_nt(xb, wt_ref[T_V:T_V + ML_WIDTH + SWA_WIDTH, :])
        pf[:, B_MO:B_SV] = _dot(xb, wn_ref[:, N_B:N_B + B_SV])
        pf[:, B_SV:B_COLS] = _dot_nt(xb, wt_ref[T_SV:T_COLS, :])

    @pl.when(step == 0)
    def _build_bias_tables():
        r = lax.broadcasted_iota(jnp.int32, (qrows, BLK), 0)
        c = lax.broadcasted_iota(jnp.int32, (qrows, BLK), 1)
        hq = r // SLAB
        bl = (r % SLAB) // tdec
        tq = r % tdec
        d_cache = WINDOW + tq - c
        d_new = tq - (c % tdec)
        key_cache = _t5_bucket(d_cache) * SWA_HEADS + hq
        key_new = _t5_bucket(d_new) * SWA_HEADS + hq
        acc_c = jnp.zeros((qrows, BLK), F32)
        acc_n = jnp.zeros((qrows, BLK), F32)
        for k in range(REL_BUCKETS):
            for h in range(SWA_HEADS):
                val = rb_ref[k, h]
                acc_c = jnp.where(key_cache == k * SWA_HEADS + h, val, acc_c)
                acc_n = jnp.where(key_new == k * SWA_HEADS + h, val, acc_n)
        tblc[...] = jnp.where(c > tq, acc_c, NEG_INF)
        tbln[...] = jnp.where((c < SLAB) & ((c // tdec) == bl) & (d_new >= 0), acc_n, NEG_INF)

    gbias = gb_ref[...]
    mlg = mlg_ref[...]
    r16 = lax.broadcasted_iota(jnp.int32, (SLAB, BLK), 0)
    rr = r16 % tdec
    bl16 = r16 // tdec
    rq = lax.broadcasted_iota(jnp.int32, (qrows, BLK), 0)
    blq = (rq % SLAB) // tdec
    hq_col = rq[:, 0:1] // SLAB
    snk = jnp.zeros((qrows, 1), F32)
    for h in range(SWA_HEADS):
        snk = jnp.where(hq_col == h, sinks_ref[h], snk)
    first_group = rq[:, 0:SWA_HD] < SWA_GROUP * SLAB
    zq = jnp.zeros((SLAB, SWA_HD), F32)
    zeros_pad = jnp.zeros((BLK - SLAB, BLK), BF16)

    def seg_last(x):
        return jnp.where(rr == 3, x,
                         jnp.where(rr == 2, _roll_rows(x, -1),
                                   jnp.where(rr == 1, _roll_rows(x, -2), _roll_rows(x, -3))))

    def seg_max(x):
        m1 = jnp.maximum(x, jnp.where(rr % 2 == 0, _roll_rows(x, -1), _roll_rows(x, 1)))
        return jnp.maximum(m1, jnp.where(rr < 2, _roll_rows(m1, -2), _roll_rows(m1, 2)))

    def slab(si):
        rows = pl.ds(si * SLAB, SLAB)
        prow = pl.ds(pl.multiple_of(step * rows_step + si * SLAB, SLAB), SLAB)
        seq0 = si * nseq

        pre = _softcap(pf[prow, B_G:B_G + BLK] + gbias)
        logf = _log_sigmoid(pre)
        y = logf + jnp.where(rr >= 1, _roll_rows(logf, 1), 0.0)
        bcum = y + jnp.where(rr >= 2, _roll_rows(y, 2), 0.0)
        ig = pltpu.roll(pre, ML_HEADS, 1)
        m_prev = mrep_ref[rows, :]
        b_minus_i = bcum - ig
        log_d = [jnp.where(rr >= dl, bcum - _roll_rows(b_minus_i, dl), -jnp.inf) for dl in range(tdec)]
        rowmax = functools.reduce(jnp.maximum, log_d)
        inter = bcum + m_prev
        m_row = jnp.maximum(inter, rowmax)
        scl = jnp.exp(inter - m_row)
        dm = [jnp.exp(ld - m_row) for ld in log_d]
        enm = jnp.exp(-m_row)
        b_last = seg_last(bcum)
        w = b_last - bcum + ig
        m_new = jnp.maximum(b_last + m_prev, seg_max(w))
        a = jnp.exp(b_last + m_prev - m_new)
        kws = jnp.exp(w - m_new) * ML_SCALE
        mo_ref[rows, :] = m_new

        q = [pa[prow, A_Q + h * ML_DK:A_Q + (h + 1) * ML_DK] for h in range(ML_HEADS)]
        k = [pa[prow, A_K + h * ML_DK:A_K + (h + 1) * ML_DK] for h in range(ML_HEADS)]
        v = [pa[prow, A_V + h * ML_DV:A_V + (h + 1) * ML_DV] for h in range(ML_HEADS)]
        c_b = [[c_ref[seq0 + bl, h] for bl in range(nseq)] for h in range(ML_HEADS)]
        qc, upd, kw = [], [], []
        for h in range(ML_HEADS):
            lane = ML_HEADS + h
            qb = q[h].astype(BF16)
            res = [_dot(qb, jnp.concatenate([c_b[h][2 * i].astype(BF16), c_b[h][2 * i + 1].astype(BF16)], axis=1))
                   for i in range(nseq // 2)]
            parts = [res[bl // 2][:, (bl % 2) * ML_DV:(bl % 2 + 1) * ML_DV] for bl in range(nseq)]
            qc_h = parts[nseq - 1]
            for bl in range(nseq - 2, -1, -1):
                qc_h = jnp.where(bl16 == bl, parts[bl], qc_h)
            qc.append(qc_h)
            kw_h = k[h] * kws[:, lane:lane + 1]
            kw.append(kw_h)
            vm = jnp.concatenate([jnp.where(bl16 == bl, v[h], 0.0) for bl in range(nseq)], axis=1)
            upd.append(_dot_tn(kw_h.astype(BF16), vm.astype(BF16)))

        knew = pf[prow, B_SK:B_SK + BLK]
        vnew = pf[prow, B_SV:B_SV + BLK]
        qparts = []
        for hq in range(SWA_HEADS):
            qh = pa[prow, A_SQ + hq * SWA_HD:A_SQ + (hq + 1) * SWA_HD]
            qparts.append(jnp.concatenate([qh, zq] if hq < SWA_GROUP else [zq, qh], axis=1))
        qs = jnp.concatenate(qparts, axis=0).astype(BF16)
        s_c = _dot_nt(qs, kc_ref[seq0 + nseq - 1].astype(BF16))
        for bl in range(nseq - 2, -1, -1):
            s_c = jnp.where(blq == bl, _dot_nt(qs, kc_ref[seq0 + bl].astype(BF16)), s_c)
        kn = jnp.concatenate([knew.astype(BF16), zeros_pad], axis=0)
        s_n = _dot_nt(qs, kn)

        yield
        for h in range(ML_HEADS):
            hs = slice(h * ML_DK, (h + 1) * ML_DK)
            lane = ML_HEADS + h
            bm = jnp.zeros((SLAB, ML_DV), F32)
            sum_s = jnp.zeros((SLAB, 1), F32)
            for dl in range(tdec):
                s_dl = jnp.sum(q[h] * _roll_rows(k[h], dl), axis=-1, keepdims=True)
                ws = s_dl * ML_SCALE * dm[dl][:, lane:lane + 1]
                bm = bm + ws * _roll_rows(v[h], dl)
                sum_s = sum_s + ws
            nsel = jnp.zeros((SLAB, ML_DK), F32)
            for bl in range(nseq):
                msk = bl16 == bl
                n_b = n_ref[seq0 + bl][h:h + 1, :]
                nsel = jnp.where(msk, n_b, nsel)
                a_b = a[bl * tdec:bl * tdec + 1, lane:lane + 1]
                c_out[seq0 + bl, h] = a_b * c_b[h][bl] + upd[h][:, bl * ML_DV:(bl + 1) * ML_DV]
                n_out[seq0 + bl, h:h + 1, :] = a_b * n_b + jnp.sum(jnp.where(msk, kw[h], 0.0), axis=0,
                                                                   keepdims=True)
            sclh = scl[:, lane:lane + 1]
            num = sclh * qc[h] + bm
            den = sclh * jnp.sum(q[h] * nsel, axis=-1, keepdims=True) + sum_s
            hh = num / jnp.maximum(jnp.abs(den), enm[:, lane:lane + 1])
            hn = hh * lax.rsqrt(jnp.mean(hh * hh, axis=-1, keepdims=True) + NORM_EPS)
            og = jax.nn.sigmoid(pf[prow, B_MO + h * ML_DV:B_MO + (h + 1) * ML_DV])
            mix_ref[rows, hs] = (hn * mlg[:, hs] * og).astype(mix_ref.dtype)

        s_c = s_c * SWA_SCALE + tblc[...]
        s_n = s_n * SWA_SCALE + tbln[...]
        mx = jnp.maximum(jnp.maximum(jnp.max(s_c, axis=-1, keepdims=True),
                                     jnp.max(s_n, axis=-1, keepdims=True)), snk)
        p_c = jnp.exp(s_c - mx)
        p_n = jnp.exp(s_n - mx)
        den = (jnp.sum(p_c, axis=-1, keepdims=True) + jnp.sum(p_n, axis=-1, keepdims=True)
               + jnp.exp(snk - mx))

        yield
        vn = jnp.concatenate([vnew.astype(BF16), zeros_pad], axis=0)
        o = _dot(p_n.astype(BF16), vn)
        for bl in range(nseq):
            o = o + _dot(jnp.where(blq == bl, p_c, 0.0).astype(BF16), vc_ref[seq0 + bl].astype(BF16))
        o = o / den
        osel = jnp.where(first_group, o[:, 0:SWA_HD], o[:, SWA_HD:2 * SWA_HD])
        og = jnp.concatenate([osel[hq * SLAB:(hq + 1) * SLAB, :] for hq in range(SWA_HEADS)], axis=1)
        mix_ref[rows, ML_WIDTH:ML_WIDTH + SWA_WIDTH] = og.astype(mix_ref.dtype)

        for bl in range(nseq):
            ko_ref[seq0 + bl, 0:WINDOW - tdec, :] = kc_ref[seq0 + bl, tdec:WINDOW, :]
            vo_ref[seq0 + bl, 0:WINDOW - tdec, :] = vc_ref[seq0 + bl, tdec:WINDOW, :]
            ko_ref[seq0 + bl, WINDOW - tdec:WINDOW, :] = knew[bl * tdec:(bl + 1) * tdec, :]
            vo_ref[seq0 + bl, WINDOW - tdec:WINDOW, :] = vnew[bl * tdec:(bl + 1) * tdec, :]
        yield

    slabs = [slab(si) for si in range(nslab)]
    for _ in range(3):
        for s in slabs:
            next(s)


def _sample_mixer(x2d, wt, wn, gb, mlg, sinks, rb, mrep, c0, n0, kc, vc):
    rows = x2d.shape[0]
    tdec = 4
    nseq = rows // tdec
    gbs = SAMPLE_GB
    steps = nseq // gbs
    rb_rows = gbs * tdec
    const2 = lambda i: (0, 0)
    smem = pl.BlockSpec(memory_space=pltpu.SMEM)
    return pl.pallas_call(
        _sample_mixer_kernel,
        grid=(steps,),
        in_specs=[
            pl.BlockSpec((rows, D_MODEL), const2, pipeline_mode=pl.Buffered(1)),
            pl.BlockSpec((T_COLS, D_MODEL), const2, pipeline_mode=pl.Buffered(1)),
            pl.BlockSpec((D_MODEL, N_B + B_SV), const2, pipeline_mode=pl.Buffered(1)),
            pl.BlockSpec((1, BLK), const2),
            pl.BlockSpec((1, ML_WIDTH), const2),
            smem, smem,
            pl.BlockSpec((rb_rows, BLK), lambda i: (i, 0)),
            pl.BlockSpec((gbs, ML_HEADS, ML_DK, ML_DV), lambda i: (i, 0, 0, 0)),
            pl.BlockSpec((gbs, ML_HEADS, ML_DK), lambda i: (i, 0, 0)),
            pl.BlockSpec((gbs, WINDOW, BLK), lambda i: (i, 0, 0)),
            pl.BlockSpec((gbs, WINDOW, BLK), lambda i: (i, 0, 0)),
        ],
        out_specs=[
            pl.BlockSpec((rb_rows, D_MODEL), lambda i: (i, 0)),
            pl.BlockSpec((gbs, ML_HEADS, ML_DK, ML_DV), lambda i: (i, 0, 0, 0)),
            pl.BlockSpec((gbs, ML_HEADS, ML_DK), lambda i: (i, 0, 0)),
            pl.BlockSpec((rb_rows, BLK), lambda i: (i, 0)),
            pl.BlockSpec((gbs, WINDOW, BLK), lambda i: (i, 0, 0)),
            pl.BlockSpec((gbs, WINDOW, BLK), lambda i: (i, 0, 0)),
        ],
        out_shape=[
            jax.ShapeDtypeStruct((rows, D_MODEL), BF16),
            jax.ShapeDtypeStruct((nseq, ML_HEADS, ML_DK, ML_DV), F32),
            jax.ShapeDtypeStruct((nseq, ML_HEADS, ML_DK), F32),
            jax.ShapeDtypeStruct((rows, BLK), F32),
            jax.ShapeDtypeStruct((nseq, WINDOW, BLK), F32),
            jax.ShapeDtypeStruct((nseq, WINDOW, BLK), F32),
        ],
        scratch_shapes=[
            pltpu.VMEM((rows, A_COLS), F32),
            pltpu.VMEM((rows, B_COLS), F32),
            pltpu.VMEM((SWA_HEADS * SLAB, BLK), F32),
            pltpu.VMEM((SWA_HEADS * SLAB, BLK), F32),
        ],
        compiler_params=pltpu.CompilerParams(
            dimension_semantics=("arbitrary",),
            vmem_limit_bytes=VMEM_LIMIT),
        name="sample_mixer",
    )(x2d, wt, wn, gb, mlg, sinks, rb, mrep, c0, n0, kc, vc)


def _dense_kernel(x_ref, mix_ref, xs_ref, mixs_ref, wo_ref, g1_ref, b1_ref, wfi_ref, wfo_ref, g2_ref, b2_ref,
                  y_ref, ys_ref):
    nsub = x_ref.shape[0] // DENSE_SUB
    subs = [slice(i * DENSE_SUB, (i + 1) * DENSE_SUB) for i in range(nsub)]
    chunks = list(zip(FF_SPLITS[:-1], FF_SPLITS[1:]))
    g1, b1, g2, b2 = g1_ref[...], b1_ref[...], g2_ref[...], b2_ref[...]

    xin = [x_ref[s, :] for s in subs]
    mixin = [mix_ref[s, :] for s in subs]
    xin[-1] = jnp.concatenate([xin[-1], xs_ref[...]], axis=0)
    mixin[-1] = jnp.concatenate([mixin[-1], mixs_ref[...]], axis=0)

    proj = [_dot(mixin[i], wo_ref[...]) for i in range(nsub)]
    h1 = [_layer_norm(ALPHA * xin[i] + proj[i], g1, b1) for i in range(nsub)]
    h1b = [h.astype(BF16) for h in h1]
    acc = [None] * nsub
    for c0, c1 in chunks:
        gate = [_dot(h1b[i], wfi_ref[:, c0:c1]) for i in range(nsub)]
        up = [_dot(h1b[i], wfi_ref[:, D_FF + c0:D_FF + c1]) for i in range(nsub)]
        for i in range(nsub):
            act = (gate[i] * jax.nn.sigmoid(gate[i]) * up[i]).astype(BF16)
            part = _dot(act, wfo_ref[c0:c1, :])
            acc[i] = part if acc[i] is None else acc[i] + part
    for i, s in enumerate(subs):
        out = _layer_norm(ALPHA * h1[i] + acc[i], g2, b2)
        y_ref[s, :] = out[:DENSE_SUB]
        if i == nsub - 1:
            ys_ref[...] = out[DENSE_SUB:]


def _dense(x2d, mix2d, xs2d, mixs2d, wo, g1, b1, wfi, wfo, g2, b2):
    rows = x2d.shape[0]
    tm = TM_DENSE
    steps = rows // tm
    srows = xs2d.shape[0] // steps
    const2 = lambda i: (0, 0)
    step2 = lambda i: (i, 0)

    def wspec(shape):
        return pl.BlockSpec(shape, const2, pipeline_mode=pl.Buffered(1))

    return pl.pallas_call(
        _dense_kernel,
        grid=(steps,),
        in_specs=[
            pl.BlockSpec((tm, D_MODEL), step2),
            pl.BlockSpec((tm, D_MODEL), step2),
            pl.BlockSpec((srows, D_MODEL), step2),
            pl.BlockSpec((srows, D_MODEL), step2),
            wspec((D_MODEL, D_MODEL)),
            wspec((1, D_MODEL)), wspec((1, D_MODEL)),
            wspec((D_MODEL, 2 * D_FF)),
            wspec((D_FF, D_MODEL)),
            wspec((1, D_MODEL)), wspec((1, D_MODEL)),
        ],
        out_specs=[pl.BlockSpec((tm, D_MODEL), step2), pl.BlockSpec((srows, D_MODEL), step2)],
        out_shape=[jax.ShapeDtypeStruct((rows, D_MODEL), F32),
                   jax.ShapeDtypeStruct((xs2d.shape[0], D_MODEL), F32)],
        compiler_params=pltpu.CompilerParams(
            dimension_semantics=("arbitrary",),
            vmem_limit_bytes=VMEM_LIMIT),
        name="dense",
    )(x2d, mix2d, xs2d, mixs2d, wo, g1, b1, wfi, wfo, g2, b2)


def kernel(x_prompt, x_sample, state_mlstm_C, state_mlstm_n, state_mlstm_m, cache_swa_k, cache_swa_v,
           w_in, b_igate, b_fgate, ml_norm_g, swa_sinks, rel_bias, w_out, ln1_g, ln1_b,
           w_ffn_in, w_ffn_out, ln2_g, ln2_b):
    bp, seq, _ = x_prompt.shape
    bs, tdec, _ = x_sample.shape
    l = 0

    w16 = jnp.pad(w_in[l], ((0, 0), (0, W_IN_PAD - w_in.shape[-1]))).astype(BF16)
    gvec = jnp.concatenate([b_igate[l], b_fgate[l]]).astype(F32)
    gb = jnp.pad(gvec, (0, BLK - 2 * ML_HEADS))[None, :]
    gbc = jnp.broadcast_to(jnp.pad(gvec, (0, T_GROWS - 2 * ML_HEADS))[:, None], (T_GROWS, BLK))
    mlg = ml_norm_g[l][None, :].astype(F32)
    sinks = swa_sinks[l].astype(F32)
    rb = rel_bias.astype(F32)
    g1, b1 = ln1_g[l][None, :], ln1_b[l][None, :]
    g2, b2 = ln2_g[l][None, :], ln2_b[l][None, :]

    mix_p, c_p, n_p, m_p, k_p, v_p, wt, wn, wo, wfi, wfo = _prompt_mixer(
        x_prompt, w16, gbc, mlg, sinks, rb, w_out[l], w_ffn_in[l], w_ffn_out[l])
    p_m =m_p[:, :ML_HEADS // 2, ::BLK].reshape(bp, ML_HEADS)[None]
    p_k = k_p.reshape(1, bp, WINDOW, SWA_KV_HEADS, SWA_HD)
    p_v = v_p.reshape(1, bp, WINDOW, SWA_KV_HEADS, SWA_HD)

    xs = x_sample.reshape(bs * tdec, D_MODEL)
    m0 = state_mlstm_m[l].astype(F32)
    mrep = jnp.pad(jnp.repeat(m0, tdec, axis=0), ((0, 0), (ML_HEADS, BLK - 2 * ML_HEADS)))
    wlen = cache_swa_k.shape[2]
    kc = cache_swa_k[l].reshape(bs, wlen, SWA_KV_HEADS * SWA_HD)
    vc = cache_swa_v[l].reshape(bs, wlen, SWA_KV_HEADS * SWA_HD)
    mix_s, c_s, n_s, mo_s, k_s, v_s = _sample_mixer(
        xs, wt, wn, gb, mlg, sinks, rb, mrep,
        state_mlstm_C[l].astype(F32), state_mlstm_n[l].astype(F32), kc, vc)
    s_m =mo_s.reshape(bs, tdec, BLK)[:, 0, ML_HEADS:2 * ML_HEADS][None]
    s_k = k_s.reshape(1, bs, wlen, SWA_KV_HEADS, SWA_HD)
    s_v = v_s.reshape(1, bs, wlen, SWA_KV_HEADS, SWA_HD)

    y_p, y_s = _dense(x_prompt.reshape(bp * seq, D_MODEL), mix_p.reshape(bp * seq, D_MODEL), xs, mix_s,
                      wo, g1, b1, wfi, wfo, g2, b2)
    y_p = y_p.reshape(bp, seq, D_MODEL)
    y_s = y_s.reshape(bs, tdec, D_MODEL)

    return (y_p, y_s, c_p[None], n_p[None], p_m, p_k, p_v,
            c_s[None], n_s[None], s_m, s_k, s_v)
```

```python
import functools
import math

import jax
import jax.numpy as jnp
from jax import lax
from jax.experimental import pallas as pl
from jax.experimental.pallas import tpu as pltpu

F32 = jnp.float32
BF16 = jnp.bfloat16

D_MODEL = 1024
ML_HEADS = 4
ML_DK = 128
ML_DV = 128
ML_WIDTH = ML_HEADS * ML_DV
GATE_SOFTCAP = 15.0
SWA_HEADS = 8
SWA_KV_HEADS = 2
SWA_GROUP = SWA_HEADS // SWA_KV_HEADS
SWA_HD = 64
SWA_WIDTH = SWA_HEADS * SWA_HD
WINDOW = 128
REL_BUCKETS = 32
REL_MAX_DIST = 128
D_FF = 2816
DEPTH = 1
ALPHA = (2.0 * DEPTH) ** 0.25
LN_EPS = 1e-5
NORM_EPS = 1e-6
NEG_INF = -1e30
IN_SIZES = (512, 512, 512, 512, 4, 4, 512, 128, 128)
ML_SCALE = ML_DK ** -0.5
LOG_ML_SCALE = math.log(ML_SCALE)
SWA_SCALE = SWA_HD ** -0.5

BLK = 128
SUBLANES_F32 = 8
SUBLANES_BF16 = 16

A_Q, A_K, A_V, A_SQ = 0, 512, 1024, 1536
A_COLS = 2048
B_MO, B_SK, B_SV, B_G = 0, 512, 640, 768
B_COLS = 896

N_K = 0
N_B = 512
T_Q, T_V, T_SQ, T_SV, T_G = 0, 512, 1024, 1536, 1664
T_GROWS = SUBLANES_BF16
T_COLS = T_G + BLK
W_IN_PAD = -(-sum(IN_SIZES) // BLK) * BLK
STATE_ROWS = ML_DV + SUBLANES_BF16
M_ROWS = SUBLANES_F32

TM_PROMPT = 1024
TM_DENSE = 1024
DENSE_SUB = 512
FF_SPLITS = (0, 1536, D_FF)
SAMPLE_GB = 16
SLAB = 16
VMEM_LIMIT = 56 * 1024 * 1024


def _softcap(a):
    return GATE_SOFTCAP * jnp.tanh(a / GATE_SOFTCAP)


def _log_sigmoid(x):
    return jnp.minimum(x, 0.0) - jnp.log1p(jnp.exp(-jnp.abs(x)))


def _layer_norm(z, g, b):
    mu = jnp.mean(z, axis=-1, keepdims=True)
    zc = z - mu
    var = jnp.mean(zc * zc, axis=-1, keepdims=True)
    return zc * lax.rsqrt(var + LN_EPS) * g + b


def _dot(a, b):
    return jnp.dot(a, b, preferred_element_type=F32)


def _dot_nt(a, b):
    return lax.dot_general(a, b, (((1,), (1,)), ((), ())), preferred_element_type=F32)


def _dot_tn(a, b):
    return lax.dot_general(a, b, (((0,), (0,)), ((), ())), preferred_element_type=F32)


def _split3(x):
    hi = x.astype(BF16)
    r1 = x - hi.astype(F32)
    mid = r1.astype(BF16)
    lo = (r1 - mid.astype(F32)).astype(BF16)
    return hi, mid, lo


def _t5_bucket(d):
    n = jnp.maximum(d, 0)
    max_exact = REL_BUCKETS // 2
    nlog = REL_BUCKETS - max_exact
    large = jnp.full(n.shape, max_exact, jnp.int32)
    for k in range(1, nlog):
        thr = math.ceil(max_exact * (REL_MAX_DIST / max_exact) ** (k / nlog))
        large = large + jnp.where(n >= thr, 1, 0)
    return jnp.where(n < max_exact, n, large)


def _prompt_mixer_kernel(x_ref, w_ref, gbc_ref, mlg_ref, sinks_ref, rb_ref, wo32, wfi32, wfo32,
                         mix_ref, c_out, n_out, m_out, ko_ref, vo_ref,
                         wt, wn, wo16, wfi16, wfo16,
                         kn, pf, vlast, kband, q_t, v_t, sq_t, sv_t, g_t, ct, msc, tbl):
    wo16[...] = wo32[...].astype(BF16)
    wfi16[...] = wfi32[...].astype(BF16)
    wfo16[...] = wfo32[...].astype(BF16)

    b = pl.program_id(0)
    t = pl.program_id(1)
    nt = pl.num_programs(1)
    tm = x_ref.shape[1]
    nblk = tm // BLK
    pairs = SWA_HEADS // 2

    @pl.when((b == 0) & (t == 0))
    def _prepare_weights():
        offs = [0]
        for n in IN_SIZES:
            offs.append(offs[-1] + n)
        o_mq, o_mk, o_mv, o_mo, o_mi, _, o_sq, o_sk, o_sv = offs[:-1]

        def col_blocks(dst_col, src_col, n):
            for i in range(n // BLK):
                wn[:, dst_col + i * BLK:dst_col + (i + 1) * BLK] = (
                    w_ref[:, src_col + i * BLK:src_col + (i + 1) * BLK])

        def row_blocks(dst_row, src_col, n):
            for i in range(n // BLK):
                wt[dst_row + i * BLK:dst_row + (i + 1) * BLK, :] = (
                    w_ref[:, src_col + i * BLK:src_col + (i + 1) * BLK].T)

        col_blocks(N_K, o_mk, ML_HEADS * ML_DK)
        col_blocks(N_B + B_MO, o_mo, ML_WIDTH)
        col_blocks(N_B + B_SK, o_sk, SWA_KV_HEADS * SWA_HD)
        row_blocks(T_Q, o_mq, ML_HEADS * ML_DK)
        row_blocks(T_V, o_mv, ML_WIDTH)
        row_blocks(T_SQ, o_sq, SWA_WIDTH)
        row_blocks(T_SV, o_sv, SWA_KV_HEADS * SWA_HD)
        gates_t = w_ref[:, o_mi:o_mi + BLK].T
        grow = lax.broadcasted_iota(jnp.int32, gates_t.shape, 0)
        wt[T_G:T_G + BLK, :] = jnp.where(grow < 2 * ML_HEADS, gates_t, jnp.zeros_like(gates_t))

    @pl.when((b == 0) & (t == 0))
    def _build_bias_tables():
        r = lax.broadcasted_iota(jnp.int32, (BLK, 2 * BLK), 0)
        ln = lax.broadcasted_iota(jnp.int32, (BLK, 2 * BLK), 1)
        second = ln >= BLK
        qi = jnp.where(second, ln - BLK, ln)
        prev = r > qi
        d = jnp.where(prev, WINDOW + qi - r, qi - r)
        bucket = _t5_bucket(d)
        for p in range(pairs):
            acc = jnp.zeros((BLK, 2 * BLK), F32)
            for k in range(REL_BUCKETS):
                acc = jnp.where(bucket == k, jnp.where(second, rb_ref[k, p + pairs], rb_ref[k, p]), acc)
            tbl[0, p] = jnp.where(prev, NEG_INF, acc)
            tbl[1, p] = acc

    @pl.when(t == 0)
    def _reset_state():
        ct[...] = jnp.zeros_like(ct)
        msc[...] = jnp.zeros_like(msc)
        kband[0:BLK, :] = jnp.zeros((BLK, kband.shape[1]), kband.dtype)
        sv_t[0] = jnp.zeros(sv_t.shape[1:], sv_t.dtype)

    xb = x_ref[0].astype(BF16)
    res_n = _dot(xb, wn[...])
    kn[...] = res_n[:, N_K:N_K + ML_HEADS * ML_DK].astype(BF16)
    pf[...] = res_n[:, N_B:N_B + B_SV]
    kband[BLK:BLK + tm, :] = pf[:, B_SK:B_SK + BLK].astype(BF16)

    res_t = _dot_nt(wt[0:T_G + T_GROWS, :], xb)
    vlast[...] = res_t[T_SV:T_SV + BLK, tm - BLK:tm]

    def put(dst, r0, nrows, off, scale=None):
        res = res_t[r0:r0 + nrows, :]
        if scale is not None:
            res = res * scale
        for jj in range(nblk):
            dst[jj + off] = res[:, jj * BLK:(jj + 1) * BLK].astype(dst.dtype)

    put(q_t, T_Q, ML_HEADS * ML_DK, 0)
    put(v_t, T_V, ML_WIDTH, 0)
    put(sq_t, T_SQ, SWA_WIDTH, 0, SWA_SCALE)
    put(sv_t, T_SV, SWA_KV_HEADS * SWA_HD, 1)
    put(g_t, T_G, T_GROWS, 0)

    row = lax.broadcasted_iota(jnp.int32, (BLK, BLK), 0)
    col = lax.broadcasted_iota(jnp.int32, (BLK, BLK), 1)
    causal_t = row <= col
    triu = causal_t.astype(BF16)
    causal2 = jnp.concatenate([causal_t, causal_t], axis=1)
    r16 = lax.broadcasted_iota(jnp.int32, (STATE_ROWS - ML_DV, 2 * BLK), 0)
    ones_rows = (r16 == 0).astype(BF16)
    z128 = jnp.zeros((BLK, BLK), BF16)
    ml_pairs = ML_HEADS // 2
    zero_rows = jnp.zeros((BLK - T_GROWS, BLK), F32)
    gbc = gbc_ref[...]
    mlg = mlg_ref[...]
    lane2 = lax.broadcasted_iota(jnp.int32, (1, 2 * BLK), 1)
    snk_rows = [jnp.where(lane2 >= BLK, sinks_ref[p + pairs], sinks_ref[p]) for p in range(pairs)]
    zq = jnp.zeros((SWA_HD, BLK), BF16)
    prev2 = jnp.concatenate([row > col, row > col], axis=1)

    def pair_row(x0, x1):
        return jnp.concatenate([jnp.broadcast_to(x0, (1, BLK)), jnp.broadcast_to(x1, (1, BLK))], axis=1)

    def block_diag(x0, x1):
        return jnp.concatenate([jnp.concatenate([x0, z128], axis=1),
                                jnp.concatenate([z128, x1], axis=1)], axis=0)

    def block(j):
        r0 = j * BLK
        rows = pl.ds(r0, BLK)
        hsl = [slice(h * ML_DK, (h + 1) * ML_DK) for h in range(ML_HEADS)]
        psl = [slice(p * 2 * ML_DK, (p + 1) * 2 * ML_DK) for p in range(ml_pairs)]

        pre_t = _softcap(g_t[j] + gbc)
        hi, mid, lo = _split3(_log_sigmoid(pre_t))
        b_t = _dot(hi, triu) + _dot(mid, triu) + _dot(lo, triu)
        u_t = pltpu.roll(pre_t, ML_HEADS, 0) - b_t
        u_c = jnp.concatenate([u_t, zero_rows], axis=0).T

        k2 = [kn[rows, psl[p]] for p in range(ml_pairs)]
        bdq = [block_diag(q_t[j, hsl[2 * p], :], q_t[j, hsl[2 * p + 1], :]) for p in range(ml_pairs)]
        vte = [jnp.concatenate([jnp.concatenate([v_t[j, hsl[2 * p], :], v_t[j, hsl[2 * p + 1], :]], axis=1),
                                ones_rows], axis=0) for p in range(ml_pairs)]
        kq = [_dot(k2[p], bdq[p]) for p in range(ml_pairs)]

        sel = jnp.where(t == 0, 0, 1) if j == 0 else 1
        kb = kband[pl.ds(r0, 2 * BLK), :]
        vtb = jnp.concatenate([sv_t[j], sv_t[j + 1]], axis=1)
        sc2 = []
        for p in range(pairs):
            qa = sq_t[j, p * SWA_HD:(p + 1) * SWA_HD, :]
            qb = sq_t[j, (p + pairs) * SWA_HD:(p + pairs + 1) * SWA_HD, :]
            bd = jnp.concatenate([jnp.concatenate([qa, zq], axis=1),
                                  jnp.concatenate([zq, qb], axis=1)], axis=0)
            sc2.append(_dot(kb, bd))

        yield
        bds, scl, m_row, m_new, a_p, vwte = [], [], [], [], [], []
        for p in range(ml_pairs):
            h0, h1 = 2 * p, 2 * p + 1
            b_row = pair_row(b_t[ML_HEADS + h0:ML_HEADS + h0 + 1, :], b_t[ML_HEADS + h1:ML_HEADS + h1 + 1, :])
            i_row = pair_row(pre_t[h0:h0 + 1, :], pre_t[h1:h1 + 1, :])
            u_col = jnp.concatenate(
                [jnp.broadcast_to(u_c[:, ML_HEADS + h0:ML_HEADS + h0 + 1], (BLK, BLK)),
                 jnp.broadcast_to(u_c[:, ML_HEADS + h1:ML_HEADS + h1 + 1], (BLK, BLK))], axis=1)
            m_prev = msc[p:p + 1, :]
            log_d = jnp.where(causal2, b_row + u_col, -jnp.inf)
            inter = b_row + m_prev
            mr = jnp.maximum(inter, jnp.max(log_d, axis=0, keepdims=True))
            m_row.append(mr)
            scl.append(jnp.exp(inter - mr))
            s2 = (kq[p] * jnp.exp(log_d - (mr - LOG_ML_SCALE))).astype(BF16)
            bds.append(block_diag(s2[:, :BLK], s2[:, BLK:]))
            b_last = pair_row(b_row[:, BLK - 1:BLK], b_row[:, 2 * BLK - 1:2 * BLK])
            w_row = b_last - b_row + i_row
            w_max = pair_row(jnp.max(w_row[:, :BLK], axis=1, keepdims=True),
                             jnp.max(w_row[:, BLK:], axis=1, keepdims=True))
            mn = jnp.maximum(b_last + m_prev, w_max)
            m_new.append(mn)
            a_p.append(jnp.exp(b_last + m_prev - mn))
            wexp = jnp.exp(w_row - mn) * ML_SCALE
            vwte.append(vte[p] * wexp.astype(BF16))

        pw, den_p = [], []
        for p in range(pairs):
            sc = jnp.where(prev2, sc2[p][:BLK], sc2[p][BLK:]) + tbl[sel, p]
            snk = snk_rows[p]
            mx = jnp.maximum(jnp.max(sc, axis=0, keepdims=True), snk)
            pe = jnp.exp(sc - mx)
            den_p.append(jnp.sum(pe, axis=0, keepdims=True) + jnp.exp(snk - mx))
            pw.append(jnp.concatenate([jnp.where(prev2, pe, 0.0), jnp.where(prev2, 0.0, pe)],
                                      axis=0).astype(BF16))

        c_old = [ct[p] for p in range(ml_pairs)]
        cq = [_dot(c_old[p].astype(BF16), bdq[p]) for p in range(ml_pairs)]
        vs = [_dot(vte[p], bds[p]) for p in range(ml_pairs)]
        upd = [_dot(vwte[p], block_diag(k2[p][:, :BLK], k2[p][:, BLK:])) for p in range(ml_pairs)]
        o2 = [_dot(vtb, pw[p]) for p in range(pairs)]

        for p in range(ml_pairs):
            nd = scl[p] * cq[p] + vs[p]
            num = nd[:ML_DV]
            den = nd[ML_DV:ML_DV + 1]
            hh = num / jnp.maximum(jnp.abs(den), jnp.exp(-m_row[p]))
            hn = hh * lax.rsqrt(jnp.mean(hh * hh, axis=0, keepdims=True) + NORM_EPS)
            for half in range(2):
                h = 2 * p + half
                og = jax.nn.sigmoid(pf[rows, B_MO + h * ML_DV:B_MO + (h + 1) * ML_DV])
                hn_h = hn[:, half * BLK:(half + 1) * BLK].T
                mix_ref[0, rows, hsl[h]] = (hn_h * mlg[:, hsl[h]] * og).astype(mix_ref.dtype)
            ct[p] = a_p[p] * c_old[p] + upd[p]
            msc[p:p + 1, :] = m_new[p]

        o_parts = [None] * SWA_HEADS
        for p in range(pairs):
            on = o2[p] / den_p[p]
            o_parts[p] = on[0:SWA_HD, 0:BLK]
            o_parts[p + pairs] = on[SWA_HD:2 * SWA_HD, BLK:2 * BLK]
        o_t = jnp.concatenate(o_parts, axis=0)
        mix_ref[0, rows, ML_WIDTH:ML_WIDTH + SWA_WIDTH] = o_t.T.astype(mix_ref.dtype)
        yield

    blocks = [block(j) for j in range(nblk)]
    for g in blocks:
        next(g)
    for g in blocks:
        next(g)

    kband[0:BLK, :] = kband[tm:tm + BLK, :]
    sv_t[0] = sv_t[nblk]

    @pl.when(t == nt - 1)
    def _write_state():
        for h in range(ML_HEADS):
            p, lanes = h // 2, slice((h % 2) * ML_DK, (h % 2 + 1) * ML_DK)
            c_out[0, h] = ct[p, 0:ML_DV, lanes].T
            n_out[0, h:h + 1, :] = ct[p, ML_DV:ML_DV + 1, lanes]
        m_out[0] = msc[...]
        ko_ref[0] = pf[tm - BLK:tm, B_SK:B_SK + BLK]
        vo_ref[0] = vlast[...].T


def _prompt_mixer(x, w_in_l, gbc, mlg, sinks, rb, wo32, wfi32, wfo32):
    bsz, seq, _ = x.shape
    tm = TM_PROMPT
    nt = seq // tm
    nblk = tm // BLK
    steps = bsz * nt
    const2 = lambda b, t: (0, 0)
    smem = pl.BlockSpec(memory_space=pltpu.SMEM)

    def slab_spec(wmat):
        return pl.BlockSpec((wmat.shape[0] // steps, wmat.shape[1]), lambda b, t: (b * nt + t, 0))

    return pl.pallas_call(
        _prompt_mixer_kernel,
        grid=(bsz, nt),
        in_specs=[
            pl.BlockSpec((1, tm, D_MODEL), lambda b, t: (b, t, 0)),
            pl.BlockSpec((D_MODEL, W_IN_PAD), const2, pipeline_mode=pl.Buffered(1)),
            pl.BlockSpec((T_GROWS, BLK), const2),
            pl.BlockSpec((1, ML_WIDTH), const2),
            smem, smem,
            slab_spec(wo32), slab_spec(wfi32), slab_spec(wfo32),
        ],
        out_specs=[
            pl.BlockSpec((1, tm, D_MODEL), lambda b, t: (b, t, 0)),
            pl.BlockSpec((1, ML_HEADS, ML_DK, ML_DV), lambda b, t: (b, 0, 0, 0)),
            pl.BlockSpec((1, ML_HEADS, ML_DK), lambda b, t: (b, 0, 0)),
            pl.BlockSpec((1, M_ROWS, 2 * BLK), lambda b, t: (b, 0, 0)),
            pl.BlockSpec((1, BLK, BLK), lambda b, t: (b, 0, 0)),
            pl.BlockSpec((1, BLK, BLK), lambda b, t: (b, 0, 0)),
            pl.BlockSpec((T_COLS, D_MODEL), const2),
            pl.BlockSpec((D_MODEL, N_B + B_SV), const2),
            slab_spec(wo32), slab_spec(wfi32), slab_spec(wfo32),
        ],
        out_shape=[
            jax.ShapeDtypeStruct((bsz, seq, D_MODEL), BF16),
            jax.ShapeDtypeStruct((bsz, ML_HEADS, ML_DK, ML_DV), F32),
            jax.ShapeDtypeStruct((bsz, ML_HEADS, ML_DK), F32),
            jax.ShapeDtypeStruct((bsz, M_ROWS, 2 * BLK), F32),
            jax.ShapeDtypeStruct((bsz, BLK, BLK), F32),
            jax.ShapeDtypeStruct((bsz, BLK, BLK), F32),
            jax.ShapeDtypeStruct((T_COLS, D_MODEL), BF16),
            jax.ShapeDtypeStruct((D_MODEL, N_B + B_SV), BF16),
            jax.ShapeDtypeStruct(wo32.shape, BF16),
            jax.ShapeDtypeStruct(wfi32.shape, BF16),
            jax.ShapeDtypeStruct(wfo32.shape, BF16),
        ],
        scratch_shapes=[
            pltpu.VMEM((tm, ML_HEADS * ML_DK), BF16),
            pltpu.VMEM((tm, B_SV), F32),
            pltpu.VMEM((BLK, BLK), F32),
            pltpu.VMEM((tm + BLK, BLK), BF16),
            pltpu.VMEM((nblk, ML_HEADS * ML_DK, BLK), BF16),
            pltpu.VMEM((nblk, ML_WIDTH, BLK), BF16),
            pltpu.VMEM((nblk, SWA_WIDTH, BLK), BF16),
            pltpu.VMEM((nblk + 1, SWA_KV_HEADS * SWA_HD, BLK), BF16),
            pltpu.VMEM((nblk, T_GROWS, BLK), F32),
            pltpu.VMEM((ML_HEADS // 2, STATE_ROWS, 2 * ML_DK), F32),
            pltpu.VMEM((M_ROWS, 2 * BLK), F32),
            pltpu.VMEM((2, SWA_HEADS // 2, BLK, 2 * BLK), F32),
        ],
        compiler_params=pltpu.CompilerParams(
            dimension_semantics=("arbitrary", "arbitrary"),
            vmem_limit_bytes=VMEM_LIMIT),
        name="prompt_mixer",
    )(x, w_in_l, gbc, mlg, sinks, rb, wo32, wfi32, wfo32)


def _roll_rows(x, shift):
    return pltpu.roll(x, shift % x.shape[0], 0)


def _sample_mixer_kernel(x_ref, wt_ref, wn_ref, gb_ref, mlg_ref, sinks_ref, rb_ref, mrep_ref,
                         c_ref, n_ref, kc_ref, vc_ref,
                         mix_ref, c_out, n_out, mo_ref, ko_ref, vo_ref,
                         pa, pf, tblc, tbln):
    step = pl.program_id(0)
    tdec = 4
    rows_step = mix_ref.shape[0]
    nslab = rows_step // SLAB
    nseq = SLAB // tdec
    qrows = SWA_HEADS * SLAB

    @pl.when(step == 0)
    def _project_all_rows():
        xb = x_ref[...].astype(BF16)
        pa[:, A_Q:A_K] = _dot_nt(xb, wt_ref[T_Q:T_Q + ML_HEADS * ML_DK, :])
        pa[:, A_K:A_V] = _dot(xb, wn_ref[:, N_K:N_K + ML_HEADS * ML_DK])
        pa[:, A_V:A_COLS] = _dot_nt(xb, wt_ref[T_V:T_V + ML_WIDTH + SWA_WIDTH, :])
        pf[:, B_MO:B_SV] = _dot(xb, wn_ref[:, N_B:N_B + B_SV])
        pf[:, B_SV:B_COLS] = _dot_nt(xb, wt_ref[T_SV:T_COLS, :])

    @pl.when(step == 0)
    def _build_bias_tables():
        r = lax.broadcasted_iota(jnp.int32, (qrows, BLK), 0)
        c = lax.broadcasted_iota(jnp.int32, (qrows, BLK), 1)
        hq = r // SLAB
        bl = (r % SLAB) // tdec
        tq = r % tdec
        d_cache = WINDOW + tq - c
        d_new = tq - (c % tdec)
        key_cache = _t5_bucket(d_cache) * SWA_HEADS + hq
        key_new = _t5_bucket(d_new) * SWA_HEADS + hq
        acc_c = jnp.zeros((qrows, BLK), F32)
        acc_n = jnp.zeros((qrows, BLK), F32)
        for k in range(REL_BUCKETS):
            for h in range(SWA_HEADS):
                val = rb_ref[k, h]
                acc_c = jnp.where(key_cache == k * SWA_HEADS + h, val, acc_c)
                acc_n = jnp.where(key_new == k * SWA_HEADS + h, val, acc_n)
        tblc[...] = jnp.where(c > tq, acc_c, NEG_INF)
        tbln[...] = jnp.where((c < SLAB) & ((c // tdec) == bl) & (d_new >= 0), acc_n, NEG_INF)

    gbias = gb_ref[...]
    mlg = mlg_ref[...]
    r16 = lax.broadcasted_iota(jnp.int32, (SLAB, BLK), 0)
    rr = r16 % tdec
    bl16 = r16 // tdec
    rq = lax.broadcasted_iota(jnp.int32, (qrows, BLK), 0)
    blq = (rq % SLAB) // tdec
    hq_col = rq[:, 0:1] // SLAB
    snk = jnp.zeros((qrows, 1), F32)
    for h in range(SWA_HEADS):
        snk = jnp.where(hq_col == h, sinks_ref[h], snk)
    first_group = rq[:, 0:SWA_HD] < SWA_GROUP * SLAB
    zq = jnp.zeros((SLAB, SWA_HD), F32)
    zeros_pad = jnp.zeros((BLK - SLAB, BLK), BF16)

    def seg_last(x):
        return jnp.where(rr == 3, x,
                         jnp.where(rr == 2, _roll_rows(x, -1),
                                   jnp.where(rr == 1, _roll_rows(x, -2), _roll_rows(x, -3))))

    def seg_max(x):
        m1 = jnp.maximum(x, jnp.where(rr % 2 == 0, _roll_rows(x, -1), _roll_rows(x, 1)))
        return jnp.maximum(m1, jnp.where(rr < 2, _roll_rows(m1, -2), _roll_rows(m1, 2)))

    def slab(si):
        rows = pl.ds(si * SLAB, SLAB)
        prow = pl.ds(pl.multiple_of(step * rows_step + si * SLAB, SLAB), SLAB)
        seq0 = si * nseq

        pre = _softcap(pf[prow, B_G:B_G + BLK] + gbias)
        logf = _log_sigmoid(pre)
        y = logf + jnp.where(rr >= 1, _roll_rows(logf, 1), 0.0)
        bcum = y + jnp.where(rr >= 2, _roll_rows(y, 2), 0.0)
        ig = pltpu.roll(pre, ML_HEADS, 1)
        m_prev = mrep_ref[rows, :]
        b_minus_i = bcum - ig
        log_d = [jnp.where(rr >= dl, bcum - _roll_rows(b_minus_i, dl), -jnp.inf) for dl in range(tdec)]
        rowmax = functools.reduce(jnp.maximum, log_d)
        inter = bcum + m_prev
        m_row = jnp.maximum(inter, rowmax)
        scl = jnp.exp(inter - m_row)
        dm = [jnp.exp(ld - m_row) for ld in log_d]
        enm = jnp.exp(-m_row)
        b_last = seg_last(bcum)
        w = b_last - bcum + ig
        m_new = jnp.maximum(b_last + m_prev, seg_max(w))
        a = jnp.exp(b_last + m_prev - m_new)
        kws = jnp.exp(w - m_new) * ML_SCALE
        mo_ref[rows, :] = m_new

        q = [pa[prow, A_Q + h * ML_DK:A_Q + (h + 1) * ML_DK] for h in range(ML_HEADS)]
        k = [pa[prow, A_K + h * ML_DK:A_K + (h + 1) * ML_DK] for h in range(ML_HEADS)]
        v = [pa[prow, A_V + h * ML_DV:A_V + (h + 1) * ML_DV] for h in range(ML_HEADS)]
        c_b = [[c_ref[seq0 + bl, h] for bl in range(nseq)] for h in range(ML_HEADS)]
        qc, upd, kw = [], [], []
        for h in range(ML_HEADS):
            lane = ML_HEADS + h
            qb = q[h].astype(BF16)
            res = [_dot(qb, jnp.concatenate([c_b[h][2 * i].astype(BF16), c_b[h][2 * i + 1].astype(BF16)], axis=1))
                   for i in range(nseq // 2)]
            parts = [res[bl // 2][:, (bl % 2) * ML_DV:(bl % 2 + 1) * ML_DV] for bl in range(nseq)]
            qc_h = parts[nseq - 1]
            for bl in range(nseq - 2, -1, -1):
                qc_h = jnp.where(bl16 == bl, parts[bl], qc_h)
            qc.append(qc_h)
            kw_h = k[h] * kws[:, lane:lane + 1]
            kw.append(kw_h)
            vm = jnp.concatenate([jnp.where(bl16 == bl, v[h], 0.0) for bl in range(nseq)], axis=1)
            upd.append(_dot_tn(kw_h.astype(BF16), vm.astype(BF16)))

        knew = pf[prow, B_SK:B_SK + BLK]
        vnew = pf[prow, B_SV:B_SV + BLK]
        qparts = []
        for hq in range(SWA_HEADS):
            qh = pa[prow, A_SQ + hq * SWA_HD:A_SQ + (hq + 1) * SWA_HD]
            qparts.append(jnp.concatenate([qh, zq] if hq < SWA_GROUP else [zq, qh], axis=1))
        qs = jnp.concatenate(qparts, axis=0).astype(BF16)
        s_c = _dot_nt(qs, kc_ref[seq0 + nseq - 1].astype(BF16))
        for bl in range(nseq - 2, -1, -1):
            s_c = jnp.where(blq == bl, _dot_nt(qs, kc_ref[seq0 + bl].astype(BF16)), s_c)
        kn = jnp.concatenate([knew.astype(BF16), zeros_pad], axis=0)
        s_n = _dot_nt(qs, kn)

        yield
        for h in range(ML_HEADS):
            hs = slice(h * ML_DK, (h + 1) * ML_DK)
            lane = ML_HEADS + h
            bm = jnp.zeros((SLAB, ML_DV), F32)
            sum_s = jnp.zeros((SLAB, 1), F32)
            for dl in range(tdec):
                s_dl = jnp.sum(q[h] * _roll_rows(k[h], dl), axis=-1, keepdims=True)
                ws = s_dl * ML_SCALE * dm[dl][:, lane:lane + 1]
                bm = bm + ws * _roll_rows(v[h], dl)
                sum_s = sum_s + ws
            nsel = jnp.zeros((SLAB, ML_DK), F32)
            for bl in range(nseq):
                msk = bl16 == bl
                n_b = n_ref[seq0 + bl][h:h + 1, :]
                nsel = jnp.where(msk, n_b, nsel)
                a_b = a[bl * tdec:bl * tdec + 1, lane:lane + 1]
                c_out[seq0 + bl, h] = a_b * c_b[h][bl] + upd[h][:, bl * ML_DV:(bl + 1) * ML_DV]
                n_out[seq0 + bl, h:h + 1, :] = a_b * n_b + jnp.sum(jnp.where(msk, kw[h], 0.0), axis=0,
                                                                   keepdims=True)
            sclh = scl[:, lane:lane + 1]
            num = sclh * qc[h] + bm
            den = sclh * jnp.sum(q[h] * nsel, axis=-1, keepdims=True) + sum_s
            hh = num / jnp.maximum(jnp.abs(den), enm[:, lane:lane + 1])
            hn = hh * lax.rsqrt(jnp.mean(hh * hh, axis=-1, keepdims=True) + NORM_EPS)
            og = jax.nn.sigmoid(pf[prow, B_MO + h * ML_DV:B_MO + (h + 1) * ML_DV])
            mix_ref[rows, hs] = (hn * mlg[:, hs] * og).astype(mix_ref.dtype)

        s_c = s_c * SWA_SCALE + tblc[...]
        s_n = s_n * SWA_SCALE + tbln[...]
        mx = jnp.maximum(jnp.maximum(jnp.max(s_c, axis=-1, keepdims=True),
                                     jnp.max(s_n, axis=-1, keepdims=True)), snk)
        p_c = jnp.exp(s_c - mx)
        p_n = jnp.exp(s_n - mx)
        den = (jnp.sum(p_c, axis=-1, keepdims=True) + jnp.sum(p_n, axis=-1, keepdims=True)
               + jnp.exp(snk - mx))

        yield
        vn = jnp.concatenate([vnew.astype(BF16), zeros_pad], axis=0)
        o = _dot(p_n.astype(BF16), vn)
        for bl in range(nseq):
            o = o + _dot(jnp.where(blq == bl, p_c, 0.0).astype(BF16), vc_ref[seq0 + bl].astype(BF16))
        o = o / den
        osel = jnp.where(first_group, o[:, 0:SWA_HD], o[:, SWA_HD:2 * SWA_HD])
        og = jnp.concatenate([osel[hq * SLAB:(hq + 1) * SLAB, :] for hq in range(SWA_HEADS)], axis=1)
        mix_ref[rows, ML_WIDTH:ML_WIDTH + SWA_WIDTH] = og.astype(mix_ref.dtype)

        for bl in range(nseq):
            ko_ref[seq0 + bl, 0:WINDOW - tdec, :] = kc_ref[seq0 + bl, tdec:WINDOW, :]
            vo_ref[seq0 + bl, 0:WINDOW - tdec, :] = vc_ref[seq0 + bl, tdec:WINDOW, :]
            ko_ref[seq0 + bl, WINDOW - tdec:WINDOW, :] = knew[bl * tdec:(bl + 1) * tdec, :]
            vo_ref[seq0 + bl, WINDOW - tdec:WINDOW, :] = vnew[bl * tdec:(bl + 1) * tdec, :]
        yield

    slabs = [slab(si) for si in range(nslab)]
    for _ in range(3):
        for s in slabs:
            next(s)


def _sample_mixer(x2d, wt, wn, gb, mlg, sinks, rb, mrep, c0, n0, kc, vc):
    rows = x2d.shape[0]
    tdec = 4
    nseq = rows // tdec
    gbs = SAMPLE_GB
    steps = nseq // gbs
    rb_rows = gbs * tdec
    const2 = lambda i: (0, 0)
    smem = pl.BlockSpec(memory_space=pltpu.SMEM)
    return pl.pallas_call(
        _sample_mixer_kernel,
        grid=(steps,),
        in_specs=[
            pl.BlockSpec((rows, D_MODEL), const2, pipeline_mode=pl.Buffered(1)),
            pl.BlockSpec((T_COLS, D_MODEL), const2, pipeline_mode=pl.Buffered(1)),
            pl.BlockSpec((D_MODEL, N_B + B_SV), const2, pipeline_mode=pl.Buffered(1)),
            pl.BlockSpec((1, BLK), const2),
            pl.BlockSpec((1, ML_WIDTH), const2),
            smem, smem,
            pl.BlockSpec((rb_rows, BLK), lambda i: (i, 0)),
            pl.BlockSpec((gbs, ML_HEADS, ML_DK, ML_DV), lambda i: (i, 0, 0, 0)),
            pl.BlockSpec((gbs, ML_HEADS, ML_DK), lambda i: (i, 0, 0)),
            pl.BlockSpec((gbs, WINDOW, BLK), lambda i: (i, 0, 0)),
            pl.BlockSpec((gbs, WINDOW, BLK), lambda i: (i, 0, 0)),
        ],
        out_specs=[
            pl.BlockSpec((rb_rows, D_MODEL), lambda i: (i, 0)),
            pl.BlockSpec((gbs, ML_HEADS, ML_DK, ML_DV), lambda i: (i, 0, 0, 0)),
            pl.BlockSpec((gbs, ML_HEADS, ML_DK), lambda i: (i, 0, 0)),
            pl.BlockSpec((rb_rows, BLK), lambda i: (i, 0)),
            pl.BlockSpec((gbs, WINDOW, BLK), lambda i: (i, 0, 0)),
            pl.BlockSpec((gbs, WINDOW, BLK), lambda i: (i, 0, 0)),
        ],
        out_shape=[
            jax.ShapeDtypeStruct((rows, D_MODEL), BF16),
            jax.ShapeDtypeStruct((nseq, ML_HEADS, ML_DK, ML_DV), F32),
            jax.ShapeDtypeStruct((nseq, ML_HEADS, ML_DK), F32),
            jax.ShapeDtypeStruct((rows, BLK), F32),
            jax.ShapeDtypeStruct((nseq, WINDOW, BLK), F32),
            jax.ShapeDtypeStruct((nseq, WINDOW, BLK), F32),
        ],
        scratch_shapes=[
            pltpu.VMEM((rows, A_COLS), F32),
            pltpu.VMEM((rows, B_COLS), F32),
            pltpu.VMEM((SWA_HEADS * SLAB, BLK), F32),
            pltpu.VMEM((SWA_HEADS * SLAB, BLK), F32),
        ],
        compiler_params=pltpu.CompilerParams(
            dimension_semantics=("arbitrary",),
            vmem_limit_bytes=VMEM_LIMIT),
        name="sample_mixer",
    )(x2d, wt, wn, gb, mlg, sinks, rb, mrep, c0, n0, kc, vc)


def _dense_kernel(x_ref, mix_ref, xs_ref, mixs_ref, wo_ref, g1_ref, b1_ref, wfi_ref, wfo_ref, g2_ref, b2_ref,
                  y_ref, ys_ref):
    nsub = x_ref.shape[0] // DENSE_SUB
    subs = [slice(i * DENSE_SUB, (i + 1) * DENSE_SUB) for i in range(nsub)]
    chunks = list(zip(FF_SPLITS[:-1], FF_SPLITS[1:]))
    g1, b1, g2, b2 = g1_ref[...], b1_ref[...], g2_ref[...], b2_ref[...]

    xin = [x_ref[s, :] for s in subs]
    mixin = [mix_ref[s, :] for s in subs]
    xin[-1] = jnp.concatenate([xin[-1], xs_ref[...]], axis=0)
    mixin[-1] = jnp.concatenate([mixin[-1], mixs_ref[...]], axis=0)

    proj = [_dot(mixin[i], wo_ref[...]) for i in range(nsub)]
    h1 = [_layer_norm(ALPHA * xin[i] + proj[i], g1, b1) for i in range(nsub)]
    h1b = [h.astype(BF16) for h in h1]
    acc = [None] * nsub
    for c0, c1 in chunks:
        gate = [_dot(h1b[i], wfi_ref[:, c0:c1]) for i in range(nsub)]
        up = [_dot(h1b[i], wfi_ref[:, D_FF + c0:D_FF + c1]) for i in range(nsub)]
        for i in range(nsub):
            act = (gate[i] * jax.nn.sigmoid(gate[i]) * up[i]).astype(BF16)
            part = _dot(act, wfo_ref[c0:c1, :])
            acc[i] = part if acc[i] is None else acc[i] + part
    for i, s in enumerate(subs):
        out = _layer_norm(ALPHA * h1[i] + acc[i], g2, b2)
        y_ref[s, :] = out[:DENSE_SUB]
        if i == nsub - 1:
            ys_ref[...] = out[DENSE_SUB:]


def _dense(x2d, mix2d, xs2d, mixs2d, wo, g1, b1, wfi, wfo, g2, b2):
    rows = x2d.shape[0]
    tm = TM_DENSE
    steps = rows // tm
    srows = xs2d.shape[0] // steps
    const2 = lambda i: (0, 0)
    step2 = lambda i: (i, 0)

    def wspec(shape):
        return pl.BlockSpec(shape, const2, pipeline_mode=pl.Buffered(1))

    return pl.pallas_call(
        _dense_kernel,
        grid=(steps,),
        in_specs=[
            pl.BlockSpec((tm, D_MODEL), step2),
            pl.BlockSpec((tm, D_MODEL), step2),
            pl.BlockSpec((srows, D_MODEL), step2),
            pl.BlockSpec((srows, D_MODEL), step2),
            wspec((D_MODEL, D_MODEL)),
            wspec((1, D_MODEL)), wspec((1, D_MODEL)),
            wspec((D_MODEL, 2 * D_FF)),
            wspec((D_FF, D_MODEL)),
            wspec((1, D_MODEL)), wspec((1, D_MODEL)),
        ],
        out_specs=[pl.BlockSpec((tm, D_MODEL), step2), pl.BlockSpec((srows, D_MODEL), step2)],
        out_shape=[jax.ShapeDtypeStruct((rows, D_MODEL), F32),
                   jax.ShapeDtypeStruct((xs2d.shape[0], D_MODEL), F32)],
        compiler_params=pltpu.CompilerParams(
            dimension_semantics=("arbitrary",),
            vmem_limit_bytes=VMEM_LIMIT),
        name="dense",
    )(x2d, mix2d, xs2d, mixs2d, wo, g1, b1, wfi, wfo, g2, b2)


def kernel(x_prompt, x_sample, state_mlstm_C, state_mlstm_n, state_mlstm_m, cache_swa_k, cache_swa_v,
           w_in, b_igate, b_fgate, ml_norm_g, swa_sinks, rel_bias, w_out, ln1_g, ln1_b,
           w_ffn_in, w_ffn_out, ln2_g, ln2_b):
    bp, seq, _ = x_prompt.shape
    bs, tdec, _ = x_sample.shape
    l = 0

    w16 = jnp.pad(w_in[l], ((0, 0), (0, W_IN_PAD - w_in.shape[-1]))).astype(BF16)
    gvec = jnp.concatenate([b_igate[l], b_fgate[l]]).astype(F32)
    gb = jnp.pad(gvec, (0, BLK - 2 * ML_HEADS))[None, :]
    gbc = jnp.broadcast_to(jnp.pad(gvec, (0, T_GROWS - 2 * ML_HEADS))[:, None], (T_GROWS, BLK))
    mlg = ml_norm_g[l][None, :].astype(F32)
    sinks = swa_sinks[l].astype(F32)
    rb = rel_bias.astype(F32)
    g1, b1 = ln1_g[l][None, :], ln1_b[l][None, :]
    g2, b2 = ln2_g[l][None, :], ln2_b[l][None, :]

    mix_p, c_p, n_p, m_p, k_p, v_p, wt, wn, wo, wfi, wfo = _prompt_mixer(
        x_prompt, w16, gbc, mlg, sinks, rb, w_out[l], w_ffn_in[l], w_ffn_out[l])
    p_m =m_p[:, :ML_HEADS // 2, ::BLK].reshape(bp, ML_HEADS)[None]
    p_k = k_p.reshape(1, bp, WINDOW, SWA_KV_HEADS, SWA_HD)
    p_v = v_p.reshape(1, bp, WINDOW, SWA_KV_HEADS, SWA_HD)

    xs = x_sample.reshape(bs * tdec, D_MODEL)
    m0 = state_mlstm_m[l].astype(F32)
    mrep = jnp.pad(jnp.repeat(m0, tdec, axis=0), ((0, 0), (ML_HEADS, BLK - 2 * ML_HEADS)))
    wlen = cache_swa_k.shape[2]
    kc = cache_swa_k[l].reshape(bs, wlen, SWA_KV_HEADS * SWA_HD)
    vc = cache_swa_v[l].reshape(bs, wlen, SWA_KV_HEADS * SWA_HD)
    mix_s, c_s, n_s, mo_s, k_s, v_s = _sample_mixer(
        xs, wt, wn, gb, mlg, sinks, rb, mrep,
        state_mlstm_C[l].astype(F32), state_mlstm_n[l].astype(F32), kc, vc)
    s_m =mo_s.reshape(bs, tdec, BLK)[:, 0, ML_HEADS:2 * ML_HEADS][None]
    s_k = k_s.reshape(1, bs, wlen, SWA_KV_HEADS, SWA_HD)
    s_v = v_s.reshape(1, bs, wlen, SWA_KV_HEADS, SWA_HD)

    y_p, y_s = _dense(x_prompt.reshape(bp * seq, D_MODEL), mix_p.reshape(bp * seq, D_MODEL), xs, mix_s,
                      wo, g1, b1, wfi, wfo, g2, b2)
    y_p = y_p.reshape(bp, seq, D_MODEL)
    y_s = y_s.reshape(bs, tdec, D_MODEL)

    return (y_p, y_s, c_p[None], n_p[None], p_m, p_k, p_v,
            c_s[None], n_s[None], s_m, s_k, s_v)
```

```python
import functools
import math

import jax
import jax.numpy as jnp
from jax import lax
from jax.experimental import pallas as pl
from jax.experimental.pallas import tpu as pltpu

F32 = jnp.float32
BF16 = jnp.bfloat16

D_MODEL = 1024
ML_HEADS = 4
ML_DK = 128
ML_DV = 128
ML_WIDTH = ML_HEADS * ML_DV
GATE_SOFTCAP = 15.0
SWA_HEADS = 8
SWA_KV_HEADS = 2
SWA_GROUP = SWA_HEADS // SWA_KV_HEADS
SWA_HD = 64
SWA_WIDTH = SWA_HEADS * SWA_HD
WINDOW = 128
REL_BUCKETS = 32
REL_MAX_DIST = 128
D_FF = 2816
DEPTH = 1
ALPHA = (2.0 * DEPTH) ** 0.25
LN_EPS = 1e-5
NORM_EPS = 1e-6
NEG_INF = -1e30
IN_SIZES = (512, 512, 512, 512, 4, 4, 512, 128, 128)
ML_SCALE = ML_DK ** -0.5
LOG_ML_SCALE = math.log(ML_SCALE)
SWA_SCALE = SWA_HD ** -0.5

BLK = 128
SUBLANES_F32 = 8
SUBLANES_BF16 = 16

A_Q, A_K, A_V, A_SQ = 0, 512, 1024, 1536
A_COLS = 2048
B_MO, B_SK, B_SV, B_G = 0, 512, 640, 768
B_COLS = 896

N_K = 0
N_B = 512
T_Q, T_V, T_SQ, T_SV, T_G = 0, 512, 1024, 1536, 1664
T_GROWS = SUBLANES_BF16
T_COLS = T_G + BLK
W_IN_PAD = -(-sum(IN_SIZES) // BLK) * BLK
STATE_ROWS = ML_DV + SUBLANES_BF16
M_ROWS = SUBLANES_F32

TM_PROMPT = 1024
TM_DENSE = 1024
DENSE_SUB = 512
FF_SPLITS = (0, 1536, D_FF)
SAMPLE_GB = 16
SLAB = 16
VMEM_LIMIT = 56 * 1024 * 1024


def _softcap(a):
    return GATE_SOFTCAP * jnp.tanh(a / GATE_SOFTCAP)


def _log_sigmoid(x):
    return jnp.minimum(x, 0.0) - jnp.log1p(jnp.exp(-jnp.abs(x)))


def _layer_norm(z, g, b):
    mu = jnp.mean(z, axis=-1, keepdims=True)
    zc = z - mu
    var = jnp.mean(zc * zc, axis=-1, keepdims=True)
    return zc * lax.rsqrt(var + LN_EPS) * g + b


def _dot(a, b):
    return jnp.dot(a, b, preferred_element_type=F32)


def _dot_nt(a, b):
    return lax.dot_general(a, b, (((1,), (1,)), ((), ())), preferred_element_type=F32)


def _dot_tn(a, b):
    return lax.dot_general(a, b, (((0,), (0,)), ((), ())), preferred_element_type=F32)


def _split3(x):
    hi = x.astype(BF16)
    r1 = x - hi.astype(F32)
    mid = r1.astype(BF16)
    lo = (r1 - mid.astype(F32)).astype(BF16)
    return hi, mid, lo


def _t5_bucket(d):
    n = jnp.maximum(d, 0)
    max_exact = REL_BUCKETS // 2
    nlog = REL_BUCKETS - max_exact
    large = jnp.full(n.shape, max_exact, jnp.int32)
    for k in range(1, nlog):
        thr = math.ceil(max_exact * (REL_MAX_DIST / max_exact) ** (k / nlog))
        large = large + jnp.where(n >= thr, 1, 0)
    return jnp.where(n < max_exact, n, large)


def _prompt_mixer_kernel(x_ref, w_ref, gbc_ref, mlg_ref, sinks_ref, rb_ref, wo32, wfi32, wfo32,
                         mix_ref, c_out, n_out, m_out, ko_ref, vo_ref,
                         wt, wn, wo16, wfi16, wfo16,
                         kn, pf, vlast, kband, q_t, v_t, sq_t, sv_t, g_t, ct, msc, tbl):
    wo16[...] = wo32[...].astype(BF16)
    wfi16[...] = wfi32[...].astype(BF16)
    wfo16[...] = wfo32[...].astype(BF16)

    b = pl.program_id(0)
    t = pl.program_id(1)
    nt = pl.num_programs(1)
    tm = x_ref.shape[1]
    nblk = tm // BLK
    pairs = SWA_HEADS // 2

    @pl.when((b == 0) & (t == 0))
    def _prepare_weights():
        offs = [0]
        for n in IN_SIZES:
            offs.append(offs[-1] + n)
        o_mq, o_mk, o_mv, o_mo, o_mi, _, o_sq, o_sk, o_sv = offs[:-1]

        def col_blocks(dst_col, src_col, n):
            for i in range(n // BLK):
                wn[:, dst_col + i * BLK:dst_col + (i + 1) * BLK] = (
                    w_ref[:, src_col + i * BLK:src_col + (i + 1) * BLK])

        def row_blocks(dst_row, src_col, n):
            for i in range(n // BLK):
                wt[dst_row + i * BLK:dst_row + (i + 1) * BLK, :] = (
                    w_ref[:, src_col + i * BLK:src_col + (i + 1) * BLK].T)

        col_blocks(N_K, o_mk, ML_HEADS * ML_DK)
        col_blocks(N_B + B_MO, o_mo, ML_WIDTH)
        col_blocks(N_B + B_SK, o_sk, SWA_KV_HEADS * SWA_HD)
        row_blocks(T_Q, o_mq, ML_HEADS * ML_DK)
        row_blocks(T_V, o_mv, ML_WIDTH)
        row_blocks(T_SQ, o_sq, SWA_WIDTH)
        row_blocks(T_SV, o_sv, SWA_KV_HEADS * SWA_HD)
        gates_t = w_ref[:, o_mi:o_mi + BLK].T
        grow = lax.broadcasted_iota(jnp.int32, gates_t.shape, 0)
        wt[T_G:T_G + BLK, :] = jnp.where(grow < 2 * ML_HEADS, gates_t, jnp.zeros_like(gates_t))

    @pl.when((b == 0) & (t == 0))
    def _build_bias_tables():
        r = lax.broadcasted_iota(jnp.int32, (BLK, 2 * BLK), 0)
        ln = lax.broadcasted_iota(jnp.int32, (BLK, 2 * BLK), 1)
        second = ln >= BLK
        qi = jnp.where(second, ln - BLK, ln)
        prev = r > qi
        d = jnp.where(prev, WINDOW + qi - r, qi - r)
        bucket = _t5_bucket(d)
        for p in range(pairs):
            acc = jnp.zeros((BLK, 2 * BLK), F32)
            for k in range(REL_BUCKETS):
                acc = jnp.where(bucket == k, jnp.where(second, rb_ref[k, p + pairs], rb_ref[k, p]), acc)
            tbl[0, p] = jnp.where(prev, NEG_INF, acc)
            tbl[1, p] = acc

    @pl.when(t == 0)
    def _reset_state():
        ct[...] = jnp.zeros_like(ct)
        msc[...] = jnp.zeros_like(msc)
        kband[0:BLK, :] = jnp.zeros((BLK, kband.shape[1]), kband.dtype)
        sv_t[0] = jnp.zeros(sv_t.shape[1:], sv_t.dtype)

    xb = x_ref[0].astype(BF16)
    res_n = _dot(xb, wn[...])
    kn[...] = res_n[:, N_K:N_K + ML_HEADS * ML_DK].astype(BF16)
    pf[...] = res_n[:, N_B:N_B + B_SV]
    kband[BLK:BLK + tm, :] = pf[:, B_SK:B_SK + BLK].astype(BF16)

    res_t = _dot_nt(wt[0:T_G + T_GROWS, :], xb)
    vlast[...] = res_t[T_SV:T_SV + BLK, tm - BLK:tm]

    def put(dst, r0, nrows, off, scale=None):
        res = res_t[r0:r0 + nrows, :]
        if scale is not None:
            res = res * scale
        for jj in range(nblk):
            dst[jj + off] = res[:, jj * BLK:(jj + 1) * BLK].astype(dst.dtype)

    put(q_t, T_Q, ML_HEADS * ML_DK, 0)
    put(v_t, T_V, ML_WIDTH, 0)
    put(sq_t, T_SQ, SWA_WIDTH, 0, SWA_SCALE)
    put(sv_t, T_SV, SWA_KV_HEADS * SWA_HD, 1)
    put(g_t, T_G, T_GROWS, 0)

    row = lax.broadcasted_iota(jnp.int32, (BLK, BLK), 0)
    col = lax.broadcasted_iota(jnp.int32, (BLK, BLK), 1)
    causal_t = row <= col
    triu = causal_t.astype(BF16)
    causal2 = jnp.concatenate([causal_t, causal_t], axis=1)
    r16 = lax.broadcasted_iota(jnp.int32, (STATE_ROWS - ML_DV, 2 * BLK), 0)
    ones_rows = (r16 == 0).astype(BF16)
    z128 = jnp.zeros((BLK, BLK), BF16)
    ml_pairs = ML_HEADS // 2
    zero_rows = jnp.zeros((BLK - T_GROWS, BLK), F32)
    gbc = gbc_ref[...]
    mlg = mlg_ref[...]
    lane2 = lax.broadcasted_iota(jnp.int32, (1, 2 * BLK), 1)
    snk_rows = [jnp.where(lane2 >= BLK, sinks_ref[p + pairs], sinks_ref[p]) for p in range(pairs)]
    zq = jnp.zeros((SWA_HD, BLK), BF16)
    prev2 = jnp.concatenate([row > col, row > col], axis=1)

    def pair_row(x0, x1):
        return jnp.concatenate([jnp.broadcast_to(x0, (1, BLK)), jnp.broadcast_to(x1, (1, BLK))], axis=1)

    def block_diag(x0, x1):
        return jnp.concatenate([jnp.concatenate([x0, z128], axis=1),
                                jnp.concatenate([z128, x1], axis=1)], axis=0)

    def block(j):
        r0 = j * BLK
        rows = pl.ds(r0, BLK)
        hsl = [slice(h * ML_DK, (h + 1) * ML_DK) for h in range(ML_HEADS)]
        psl = [slice(p * 2 * ML_DK, (p + 1) * 2 * ML_DK) for p in range(ml_pairs)]

        pre_t = _softcap(g_t[j] + gbc)
        hi, mid, lo = _split3(_log_sigmoid(pre_t))
        b_t = _dot(hi, triu) + _dot(mid, triu) + _dot(lo, triu)
        u_t = pltpu.roll(pre_t, ML_HEADS, 0) - b_t
        u_c = jnp.concatenate([u_t, zero_rows], axis=0).T

        k2 = [kn[rows, psl[p]] for p in range(ml_pairs)]
        bdq = [block_diag(q_t[j, hsl[2 * p], :], q_t[j, hsl[2 * p + 1], :]) for p in range(ml_pairs)]
        vte = [jnp.concatenate([jnp.concatenate([v_t[j, hsl[2 * p], :], v_t[j, hsl[2 * p + 1], :]], axis=1),
                                ones_rows], axis=0) for p in range(ml_pairs)]
        kq = [_dot(k2[p], bdq[p]) for p in range(ml_pairs)]

        sel = jnp.where(t == 0, 0, 1) if j == 0 else 1
        kb = kband[pl.ds(r0, 2 * BLK), :]
        vtb = jnp.concatenate([sv_t[j], sv_t[j + 1]], axis=1)
        sc2 = []
        for p in range(pairs):
            qa = sq_t[j, p * SWA_HD:(p + 1) * SWA_HD, :]
            qb = sq_t[j, (p + pairs) * SWA_HD:(p + pairs + 1) * SWA_HD, :]
            bd = jnp.concatenate([jnp.concatenate([qa, zq], axis=1),
                                  jnp.concatenate([zq, qb], axis=1)], axis=0)
            sc2.append(_dot(kb, bd))

        yield
        bds, scl, m_row, a_p, vwte = [], [], [], [], []
        for p in range(ml_pairs):
            h0, h1 = 2 * p, 2 * p + 1
            b_row = pair_row(b_t[ML_HEADS + h0:ML_HEADS + h0 + 1, :], b_t[ML_HEADS + h1:ML_HEADS + h1 + 1, :])
            i_row = pair_row(pre_t[h0:h0 + 1, :], pre_t[h1:h1 + 1, :])
            u_col = jnp.concatenate(
                [jnp.broadcast_to(u_c[:, ML_HEADS + h0:ML_HEADS + h0 + 1], (BLK, BLK)),
                 jnp.broadcast_to(u_c[:, ML_HEADS + h1:ML_HEADS + h1 + 1], (BLK, BLK))], axis=1)
            m_prev = m_state[p]
            log_d = jnp.where(causal2, b_row + u_col, -jnp.inf)
            inter = b_row + m_prev
            mr = jnp.maximum(inter, jnp.max(log_d, axis=0, keepdims=True))
            m_row.append(mr)
            scl.append(jnp.exp(inter - mr))
            s2 = (kq[p] * jnp.exp(log_d - (mr - LOG_ML_SCALE))).astype(BF16)
            bds.append(block_diag(s2[:, :BLK], s2[:, BLK:]))
            b_last = pair_row(b_row[:, BLK - 1:BLK], b_row[:, 2 * BLK - 1:2 * BLK])
            w_row = b_last - b_row + i_row
            w_max = pair_row(jnp.max(w_row[:, :BLK], axis=1, keepdims=True),
                             jnp.max(w_row[:, BLK:], axis=1, keepdims=True))
            mn = jnp.maximum(b_last + m_prev, w_max)
            m_state[p] = mn
            a_p.append(jnp.exp(b_last + m_prev - mn))
            wexp = jnp.exp(w_row - mn) * ML_SCALE
            vwte.append(vte[p] * wexp.astype(BF16))

        pw, den_p = [], []
        for p in range(pairs):
            sc = jnp.where(prev2, sc2[p][:BLK], sc2[p][BLK:]) + tbl[sel, p]
            snk = snk_rows[p]
            mx = jnp.maximum(jnp.max(sc, axis=0, keepdims=True), snk)
            pe = jnp.exp(sc - mx)
            den_p.append(jnp.sum(pe, axis=0, keepdims=True) + jnp.exp(snk - mx))
            pw.append(jnp.concatenate([jnp.where(prev2, pe, 0.0), jnp.where(prev2, 0.0, pe)],
                                      axis=0).astype(BF16))

        yield
        c_old = [ct[p] for p in range(ml_pairs)]
        cq = [_dot(c_old[p].astype(BF16), bdq[p]) for p in range(ml_pairs)]
        vs = [_dot(vte[p], bds[p]) for p in range(ml_pairs)]
        upd = [_dot(vwte[p], block_diag(k2[p][:, :BLK], k2[p][:, BLK:])) for p in range(ml_pairs)]
        o2 = [_dot(vtb, pw[p]) for p in range(pairs)]

        for p in range(ml_pairs):
            nd = scl[p] * cq[p] + vs[p]
            num = nd[:ML_DV]
            den = nd[ML_DV:ML_DV + 1]
            hh = num / jnp.maximum(jnp.abs(den), jnp.exp(-m_row[p]))
            hn = hh * lax.rsqrt(jnp.mean(hh * hh, axis=0, keepdims=True) + NORM_EPS)
            for half in range(2):
                h = 2 * p + half
                og = jax.nn.sigmoid(pf[rows, B_MO + h * ML_DV:B_MO + (h + 1) * ML_DV])
                hn_h = hn[:, half * BLK:(half + 1) * BLK].T
                mix_ref[0, rows, hsl[h]] = (hn_h * mlg[:, hsl[h]] * og).astype(mix_ref.dtype)
            ct[p] = a_p[p] * c_old[p] + upd[p]

        o_parts = [None] * SWA_HEADS
        for p in range(pairs):
            on = o2[p] / den_p[p]
            o_parts[p] = on[0:SWA_HD, 0:BLK]
            o_parts[p + pairs] = on[SWA_HD:2 * SWA_HD, BLK:2 * BLK]
        o_t = jnp.concatenate(o_parts, axis=0)
        mix_ref[0, rows, ML_WIDTH:ML_WIDTH + SWA_WIDTH] = o_t.T.astype(mix_ref.dtype)
        yield

    m_state = [msc[p:p + 1, :] for p in range(ml_pairs)]
    blocks = [block(j) for j in range(nblk)]
    for g in blocks:
        next(g)
    next(blocks[0])
    for j in range(nblk):
        if j + 1 < nblk:
            next(blocks[j + 1])
        next(blocks[j])
    for p in range(ml_pairs):
        msc[p:p + 1, :] = m_state[p]

    kband[0:BLK, :] = kband[tm:tm + BLK, :]
    sv_t[0] = sv_t[nblk]

    @pl.when(t == nt - 1)
    def _write_state():
        for h in range(ML_HEADS):
            p, lanes = h // 2, slice((h % 2) * ML_DK, (h % 2 + 1) * ML_DK)
            c_out[0, h] = ct[p, 0:ML_DV, lanes].T
            n_out[0, h:h + 1, :] = ct[p, ML_DV:ML_DV + 1, lanes]
        m_out[0] = msc[...]
        ko_ref[0] = pf[tm - BLK:tm, B_SK:B_SK + BLK]
        vo_ref[0] = vlast[...].T


def _prompt_mixer(x, w_in_l, gbc, mlg, sinks, rb, wo32, wfi32, wfo32):
    bsz, seq, _ = x.shape
    tm = TM_PROMPT
    nt = seq // tm
    nblk = tm // BLK
    steps = bsz * nt
    const2 = lambda b, t: (0, 0)
    smem = pl.BlockSpec(memory_space=pltpu.SMEM)

    def slab_spec(wmat):
        return pl.BlockSpec((wmat.shape[0] // steps, wmat.shape[1]), lambda b, t: (b * nt + t, 0))

    return pl.pallas_call(
        _prompt_mixer_kernel,
        grid=(bsz, nt),
        in_specs=[
            pl.BlockSpec((1, tm, D_MODEL), lambda b, t: (b, t, 0)),
            pl.BlockSpec((D_MODEL, W_IN_PAD), const2, pipeline_mode=pl.Buffered(1)),
            pl.BlockSpec((T_GROWS, BLK), const2),
            pl.BlockSpec((1, ML_WIDTH), const2),
            smem, smem,
            slab_spec(wo32), slab_spec(wfi32), slab_spec(wfo32),
        ],
        out_specs=[
            pl.BlockSpec((1, tm, D_MODEL), lambda b, t: (b, t, 0)),
            pl.BlockSpec((1, ML_HEADS, ML_DK, ML_DV), lambda b, t: (b, 0, 0, 0)),
            pl.BlockSpec((1, ML_HEADS, ML_DK), lambda b, t: (b, 0, 0)),
            pl.BlockSpec((1, M_ROWS, 2 * BLK), lambda b, t: (b, 0, 0)),
            pl.BlockSpec((1, BLK, BLK), lambda b, t: (b, 0, 0)),
            pl.BlockSpec((1, BLK, BLK), lambda b, t: (b, 0, 0)),
            pl.BlockSpec((T_COLS, D_MODEL), const2),
            pl.BlockSpec((D_MODEL, N_B + B_SV), const2),
            slab_spec(wo32), slab_spec(wfi32), slab_spec(wfo32),
        ],
        out_shape=[
            jax.ShapeDtypeStruct((bsz, seq, D_MODEL), BF16),
            jax.ShapeDtypeStruct((bsz, ML_HEADS, ML_DK, ML_DV), F32),
            jax.ShapeDtypeStruct((bsz, ML_HEADS, ML_DK), F32),
            jax.ShapeDtypeStruct((bsz, M_ROWS, 2 * BLK), F32),
            jax.ShapeDtypeStruct((bsz, BLK, BLK), F32),
            jax.ShapeDtypeStruct((bsz, BLK, BLK), F32),
            jax.ShapeDtypeStruct((T_COLS, D_MODEL), BF16),
            jax.ShapeDtypeStruct((D_MODEL, N_B + B_SV), BF16),
            jax.ShapeDtypeStruct(wo32.shape, BF16),
            jax.ShapeDtypeStruct(wfi32.shape, BF16),
            jax.ShapeDtypeStruct(wfo32.shape, BF16),
        ],
        scratch_shapes=[
            pltpu.VMEM((tm, ML_HEADS * ML_DK), BF16),
            pltpu.VMEM((tm, B_SV), F32),
            pltpu.VMEM((BLK, BLK), F32),
            pltpu.VMEM((tm + BLK, BLK), BF16),
            pltpu.VMEM((nblk, ML_HEADS * ML_DK, BLK), BF16),
            pltpu.VMEM((nblk, ML_WIDTH, BLK), BF16),
            pltpu.VMEM((nblk, SWA_WIDTH, BLK), BF16),
            pltpu.VMEM((nblk + 1, SWA_KV_HEADS * SWA_HD, BLK), BF16),
            pltpu.VMEM((nblk, T_GROWS, BLK), F32),
            pltpu.VMEM((ML_HEADS // 2, STATE_ROWS, 2 * ML_DK), F32),
            pltpu.VMEM((M_ROWS, 2 * BLK), F32),
            pltpu.VMEM((2, SWA_HEADS // 2, BLK, 2 * BLK), F32),
        ],
        compiler_params=pltpu.CompilerParams(
            dimension_semantics=("arbitrary", "arbitrary"),
            vmem_limit_bytes=VMEM_LIMIT),
        name="prompt_mixer",
    )(x, w_in_l, gbc, mlg, sinks, rb, wo32, wfi32, wfo32)


def _roll_rows(x, shift):
    return pltpu.roll(x, shift % x.shape[0], 0)


def _sample_mixer_kernel(x_ref, wt_ref, wn_ref, gb_ref, mlg_ref, sinks_ref, rb_ref, mrep_ref,
                         c_ref, n_ref, kc_ref, vc_ref,
                         mix_ref, c_out, n_out, mo_ref, ko_ref, vo_ref,
                         pa, pf, tblc, tbln):
    step = pl.program_id(0)
    tdec = 4
    rows_step = mix_ref.shape[0]
    nslab = rows_step // SLAB
    nseq = SLAB // tdec
    qrows = SWA_HEADS * SLAB

    @pl.when(step == 0)
    def _project_all_rows():
        xb = x_ref[...].astype(BF16)
        pa[:, A_Q:A_K] = _dot_nt(xb, wt_ref[T_Q:T_Q + ML_HEADS * ML_DK, :])
        pa[:, A_K:A_V] = _dot(xb, wn_ref[:, N_K:N_K + ML_HEADS * ML_DK])
        pa[:, A_V:A_COLS] = _dot_nt(xb, wt_ref[T_V:T_V + ML_WIDTH + SWA_WIDTH, :])
        pf[:, B_MO:B_SV] = _dot(xb, wn_ref[:, N_B:N_B + B_SV])
        pf[:, B_SV:B_COLS] = _dot_nt(xb, wt_ref[T_SV:T_COLS, :])

    @pl.when(step == 0)
    def _build_bias_tables():
        r = lax.broadcasted_iota(jnp.int32, (qrows, BLK), 0)
        c = lax.broadcasted_iota(jnp.int32, (qrows, BLK), 1)
        hq = r // SLAB
        bl = (r % SLAB) // tdec
        tq = r % tdec
        d_cache = WINDOW + tq - c
        d_new = tq - (c % tdec)
        key_cache = _t5_bucket(d_cache) * SWA_HEADS + hq
        key_new = _t5_bucket(d_new) * SWA_HEADS + hq
        acc_c = jnp.zeros((qrows, BLK), F32)
        acc_n = jnp.zeros((qrows, BLK), F32)
        for k in range(REL_BUCKETS):
            for h in range(SWA_HEADS):
                val = rb_ref[k, h]
                acc_c = jnp.where(key_cache == k * SWA_HEADS + h, val, acc_c)
                acc_n = jnp.where(key_new == k * SWA_HEADS + h, val, acc_n)
        tblc[...] = jnp.where(c > tq, acc_c, NEG_INF)
        tbln[...] = jnp.where((c < SLAB) & ((c // tdec) == bl) & (d_new >= 0), acc_n, NEG_INF)

    gbias = gb_ref[...]
    mlg = mlg_ref[...]
    r16 = lax.broadcasted_iota(jnp.int32, (SLAB, BLK), 0)
    rr = r16 % tdec
    bl16 = r16 // tdec
    rq = lax.broadcasted_iota(jnp.int32, (qrows, BLK), 0)
    blq = (rq % SLAB) // tdec
    hq_col = rq[:, 0:1] // SLAB
    snk = jnp.zeros((qrows, 1), F32)
    for h in range(SWA_HEADS):
        snk = jnp.where(hq_col == h, sinks_ref[h], snk)
    first_group = rq[:, 0:SWA_HD] < SWA_GROUP * SLAB
    zq = jnp.zeros((SLAB, SWA_HD), F32)
    zeros_pad = jnp.zeros((BLK - SLAB, BLK), BF16)

    def seg_last(x):
        return jnp.where(rr == 3, x,
                         jnp.where(rr == 2, _roll_rows(x, -1),
                                   jnp.where(rr == 1, _roll_rows(x, -2), _roll_rows(x, -3))))

    def seg_max(x):
        m1 = jnp.maximum(x, jnp.where(rr % 2 == 0, _roll_rows(x, -1), _roll_rows(x, 1)))
        return jnp.maximum(m1, jnp.where(rr < 2, _roll_rows(m1, -2), _roll_rows(m1, 2)))

    def slab(si):
        rows = pl.ds(si * SLAB, SLAB)
        prow = pl.ds(pl.multiple_of(step * rows_step + si * SLAB, SLAB), SLAB)
        seq0 = si * nseq

        pre = _softcap(pf[prow, B_G:B_G + BLK] + gbias)
        logf = _log_sigmoid(pre)
        y = logf + jnp.where(rr >= 1, _roll_rows(logf, 1), 0.0)
        bcum = y + jnp.where(rr >= 2, _roll_rows(y, 2), 0.0)
        ig = pltpu.roll(pre, ML_HEADS, 1)
        m_prev = mrep_ref[rows, :]
        b_minus_i = bcum - ig
        log_d = [jnp.where(rr >= dl, bcum - _roll_rows(b_minus_i, dl), -jnp.inf) for dl in range(tdec)]
        rowmax = functools.reduce(jnp.maximum, log_d)
        inter = bcum + m_prev
        m_row = jnp.maximum(inter, rowmax)
        scl = jnp.exp(inter - m_row)
        dm = [jnp.exp(ld - m_row) for ld in log_d]
        enm = jnp.exp(-m_row)
        b_last = seg_last(bcum)
        w = b_last - bcum + ig
        m_new = jnp.maximum(b_last + m_prev, seg_max(w))
        a = jnp.exp(b_last + m_prev - m_new)
        kws = jnp.exp(w - m_new) * ML_SCALE
        mo_ref[rows, :] = m_new

        q = [pa[prow, A_Q + h * ML_DK:A_Q + (h + 1) * ML_DK] for h in range(ML_HEADS)]
        k = [pa[prow, A_K + h * ML_DK:A_K + (h + 1) * ML_DK] for h in range(ML_HEADS)]
        v = [pa[prow, A_V + h * ML_DV:A_V + (h + 1) * ML_DV] for h in range(ML_HEADS)]
        c_b = [[c_ref[seq0 + bl, h] for bl in range(nseq)] for h in range(ML_HEADS)]
        qc, upd, kw = [], [], []
        for h in range(ML_HEADS):
            lane = ML_HEADS + h
            qb = q[h].astype(BF16)
            res = [_dot(qb, jnp.concatenate([c_b[h][2 * i].astype(BF16), c_b[h][2 * i + 1].astype(BF16)], axis=1))
                   for i in range(nseq // 2)]
            parts = [res[bl // 2][:, (bl % 2) * ML_DV:(bl % 2 + 1) * ML_DV] for bl in range(nseq)]
            qc_h = parts[nseq - 1]
            for bl in range(nseq - 2, -1, -1):
                qc_h = jnp.where(bl16 == bl, parts[bl], qc_h)
            qc.append(qc_h)
            kw_h = k[h] * kws[:, lane:lane + 1]
            kw.append(kw_h)
            vm = jnp.concatenate([jnp.where(bl16 == bl, v[h], 0.0) for bl in range(nseq)], axis=1)
            upd.append(_dot_tn(kw_h.astype(BF16), vm.astype(BF16)))

        knew = pf[prow, B_SK:B_SK + BLK]
        vnew = pf[prow, B_SV:B_SV + BLK]
        qparts = []
        for hq in range(SWA_HEADS):
            qh = pa[prow, A_SQ + hq * SWA_HD:A_SQ + (hq + 1) * SWA_HD]
            qparts.append(jnp.concatenate([qh, zq] if hq < SWA_GROUP else [zq, qh], axis=1))
        qs = jnp.concatenate(qparts, axis=0).astype(BF16)
        s_c = _dot_nt(qs, kc_ref[seq0 + nseq - 1].astype(BF16))
        for bl in range(nseq - 2, -1, -1):
            s_c = jnp.where(blq == bl, _dot_nt(qs, kc_ref[seq0 + bl].astype(BF16)), s_c)
        kn = jnp.concatenate([knew.astype(BF16), zeros_pad], axis=0)
        s_n = _dot_nt(qs, kn)

        yield
        for h in range(ML_HEADS):
            hs = slice(h * ML_DK, (h + 1) * ML_DK)
            lane = ML_HEADS + h
            bm = jnp.zeros((SLAB, ML_DV), F32)
            sum_s = jnp.zeros((SLAB, 1), F32)
            for dl in range(tdec):
                s_dl = jnp.sum(q[h] * _roll_rows(k[h], dl), axis=-1, keepdims=True)
                ws = s_dl * ML_SCALE * dm[dl][:, lane:lane + 1]
                bm = bm + ws * _roll_rows(v[h], dl)
                sum_s = sum_s + ws
            nsel = jnp.zeros((SLAB, ML_DK), F32)
            for bl in range(nseq):
                msk = bl16 == bl
                n_b = n_ref[seq0 + bl][h:h + 1, :]
                nsel = jnp.where(msk, n_b, nsel)
                a_b = a[bl * tdec:bl * tdec + 1, lane:lane + 1]
                c_out[seq0 + bl, h] = a_b * c_b[h][bl] + upd[h][:, bl * ML_DV:(bl + 1) * ML_DV]
                n_out[seq0 + bl, h:h + 1, :] = a_b * n_b + jnp.sum(jnp.where(msk, kw[h], 0.0), axis=0,
                                                                   keepdims=True)
            sclh = scl[:, lane:lane + 1]
            num = sclh * qc[h] + bm
            den = sclh * jnp.sum(q[h] * nsel, axis=-1, keepdims=True) + sum_s
            hh = num / jnp.maximum(jnp.abs(den), enm[:, lane:lane + 1])
            hn = hh * lax.rsqrt(jnp.mean(hh * hh, axis=-1, keepdims=True) + NORM_EPS)
            og = jax.nn.sigmoid(pf[prow, B_MO + h * ML_DV:B_MO + (h + 1) * ML_DV])
            mix_ref[rows, hs] = (hn * mlg[:, hs] * og).astype(mix_ref.dtype)

        s_c = s_c * SWA_SCALE + tblc[...]
        s_n = s_n * SWA_SCALE + tbln[...]
        mx = jnp.maximum(jnp.maximum(jnp.max(s_c, axis=-1, keepdims=True),
                                     jnp.max(s_n, axis=-1, keepdims=True)), snk)
        p_c = jnp.exp(s_c - mx)
        p_n = jnp.exp(s_n - mx)
        den = (jnp.sum(p_c, axis=-1, keepdims=True) + jnp.sum(p_n, axis=-1, keepdims=True)
               + jnp.exp(snk - mx))

        yield
        vn = jnp.concatenate([vnew.astype(BF16), zeros_pad], axis=0)
        o = _dot(p_n.astype(BF16), vn)
        for bl in range(nseq):
            o = o + _dot(jnp.where(blq == bl, p_c, 0.0).astype(BF16), vc_ref[seq0 + bl].astype(BF16))
        o = o / den
        osel = jnp.where(first_group, o[:, 0:SWA_HD], o[:, SWA_HD:2 * SWA_HD])
        og = jnp.concatenate([osel[hq * SLAB:(hq + 1) * SLAB, :] for hq in range(SWA_HEADS)], axis=1)
        mix_ref[rows, ML_WIDTH:ML_WIDTH + SWA_WIDTH] = og.astype(mix_ref.dtype)

        for bl in range(nseq):
            ko_ref[seq0 + bl, 0:WINDOW - tdec, :] = kc_ref[seq0 + bl, tdec:WINDOW, :]
            vo_ref[seq0 + bl, 0:WINDOW - tdec, :] = vc_ref[seq0 + bl, tdec:WINDOW, :]
            ko_ref[seq0 + bl, WINDOW - tdec:WINDOW, :] = knew[bl * tdec:(bl + 1) * tdec, :]
            vo_ref[seq0 + bl, WINDOW - tdec:WINDOW, :] = vnew[bl * tdec:(bl + 1) * tdec, :]
        yield

    slabs = [slab(si) for si in range(nslab)]
    for _ in range(3):
        for s in slabs:
            next(s)


def _sample_mixer(x2d, wt, wn, gb, mlg, sinks, rb, mrep, c0, n0, kc, vc):
    rows = x2d.shape[0]
    tdec = 4
    nseq = rows // tdec
    gbs = SAMPLE_GB
    steps = nseq // gbs
    rb_rows = gbs * tdec
    const2 = lambda i: (0, 0)
    smem = pl.BlockSpec(memory_space=pltpu.SMEM)
    return pl.pallas_call(
        _sample_mixer_kernel,
        grid=(steps,),
        in_specs=[
            pl.BlockSpec((rows, D_MODEL), const2, pipeline_mode=pl.Buffered(1)),
            pl.BlockSpec((T_COLS, D_MODEL), const2, pipeline_mode=pl.Buffered(1)),
            pl.BlockSpec((D_MODEL, N_B + B_SV), const2, pipeline_mode=pl.Buffered(1)),
            pl.BlockSpec((1, BLK), const2),
            pl.BlockSpec((1, ML_WIDTH), const2),
            smem, smem,
            pl.BlockSpec((rb_rows, BLK), lambda i: (i, 0)),
            pl.BlockSpec((gbs, ML_HEADS, ML_DK, ML_DV), lambda i: (i, 0, 0, 0)),
            pl.BlockSpec((gbs, ML_HEADS, ML_DK), lambda i: (i, 0, 0)),
            pl.BlockSpec((gbs, WINDOW, BLK), lambda i: (i, 0, 0)),
            pl.BlockSpec((gbs, WINDOW, BLK), lambda i: (i, 0, 0)),
        ],
        out_specs=[
            pl.BlockSpec((rb_rows, D_MODEL), lambda i: (i, 0)),
            pl.BlockSpec((gbs, ML_HEADS, ML_DK, ML_DV), lambda i: (i, 0, 0, 0)),
            pl.BlockSpec((gbs, ML_HEADS, ML_DK), lambda i: (i, 0, 0)),
            pl.BlockSpec((rb_rows, BLK), lambda i: (i, 0)),
            pl.BlockSpec((gbs, WINDOW, BLK), lambda i: (i, 0, 0)),
            pl.BlockSpec((gbs, WINDOW, BLK), lambda i: (i, 0, 0)),
        ],
        out_shape=[
            jax.ShapeDtypeStruct((rows, D_MODEL), BF16),
            jax.ShapeDtypeStruct((nseq, ML_HEADS, ML_DK, ML_DV), F32),
            jax.ShapeDtypeStruct((nseq, ML_HEADS, ML_DK), F32),
            jax.ShapeDtypeStruct((rows, BLK), F32),
            jax.ShapeDtypeStruct((nseq, WINDOW, BLK), F32),
            jax.ShapeDtypeStruct((nseq, WINDOW, BLK), F32),
        ],
        scratch_shapes=[
            pltpu.VMEM((rows, A_COLS), F32),
            pltpu.VMEM((rows, B_COLS), F32),
            pltpu.VMEM((SWA_HEADS * SLAB, BLK), F32),
            pltpu.VMEM((SWA_HEADS * SLAB, BLK), F32),
        ],
        compiler_params=pltpu.CompilerParams(
            dimension_semantics=("arbitrary",),
            vmem_limit_bytes=VMEM_LIMIT),
        name="sample_mixer",
    )(x2d, wt, wn, gb, mlg, sinks, rb, mrep, c0, n0, kc, vc)


def _dense_kernel(x_ref, mix_ref, xs_ref, mixs_ref, wo_ref, g1_ref, b1_ref, wfi_ref, wfo_ref, g2_ref, b2_ref,
                  y_ref, ys_ref):
    nsub = x_ref.shape[0] // DENSE_SUB
    subs = [slice(i * DENSE_SUB, (i + 1) * DENSE_SUB) for i in range(nsub)]
    chunks = list(zip(FF_SPLITS[:-1], FF_SPLITS[1:]))
    g1, b1, g2, b2 = g1_ref[...], b1_ref[...], g2_ref[...], b2_ref[...]

    xin = [x_ref[s, :] for s in subs]
    mixin = [mix_ref[s, :] for s in subs]
    xin[-1] = jnp.concatenate([xin[-1], xs_ref[...]], axis=0)
    mixin[-1] = jnp.concatenate([mixin[-1], mixs_ref[...]], axis=0)

    proj = [_dot(mixin[i], wo_ref[...]) for i in range(nsub)]
    h1 = [_layer_norm(ALPHA * xin[i] + proj[i], g1, b1) for i in range(nsub)]
    h1b = [h.astype(BF16) for h in h1]
    acc = [None] * nsub
    for c0, c1 in chunks:
        gate = [_dot(h1b[i], wfi_ref[:, c0:c1]) for i in range(nsub)]
        up = [_dot(h1b[i], wfi_ref[:, D_FF + c0:D_FF + c1]) for i in range(nsub)]
        for i in range(nsub):
            act = (gate[i] * jax.nn.sigmoid(gate[i]) * up[i]).astype(BF16)
            part = _dot(act, wfo_ref[c0:c1, :])
            acc[i] = part if acc[i] is None else acc[i] + part
    for i, s in enumerate(subs):
        out = _layer_norm(ALPHA * h1[i] + acc[i], g2, b2)
        y_ref[s, :] = out[:DENSE_SUB]
        if i == nsub - 1:
            ys_ref[...] = out[DENSE_SUB:]


def _dense(x2d, mix2d, xs2d, mixs2d, wo, g1, b1, wfi, wfo, g2, b2):
    rows = x2d.shape[0]
    tm = TM_DENSE
    steps = rows // tm
    srows = xs2d.shape[0] // steps
    const2 = lambda i: (0, 0)
    step2 = lambda i: (i, 0)

    def wspec(shape):
        return pl.BlockSpec(shape, const2, pipeline_mode=pl.Buffered(1))

    return pl.pallas_call(
        _dense_kernel,
        grid=(steps,),
        in_specs=[
            pl.BlockSpec((tm, D_MODEL), step2),
            pl.BlockSpec((tm, D_MODEL), step2),
            pl.BlockSpec((srows, D_MODEL), step2),
            pl.BlockSpec((srows, D_MODEL), step2),
            wspec((D_MODEL, D_MODEL)),
            wspec((1, D_MODEL)), wspec((1, D_MODEL)),
            wspec((D_MODEL, 2 * D_FF)),
            wspec((D_FF, D_MODEL)),
            wspec((1, D_MODEL)), wspec((1, D_MODEL)),
        ],
        out_specs=[pl.BlockSpec((tm, D_MODEL), step2), pl.BlockSpec((srows, D_MODEL), step2)],
        out_shape=[jax.ShapeDtypeStruct((rows, D_MODEL), F32),
                   jax.ShapeDtypeStruct((xs2d.shape[0], D_MODEL), F32)],
        compiler_params=pltpu.CompilerParams(
            dimension_semantics=("arbitrary",),
            vmem_limit_bytes=VMEM_LIMIT),
        name="dense",
    )(x2d, mix2d, xs2d, mixs2d, wo, g1, b1, wfi, wfo, g2, b2)


def kernel(x_prompt, x_sample, state_mlstm_C, state_mlstm_n, state_mlstm_m, cache_swa_k, cache_swa_v,
           w_in, b_igate, b_fgate, ml_norm_g, swa_sinks, rel_bias, w_out, ln1_g, ln1_b,
           w_ffn_in, w_ffn_out, ln2_g, ln2_b):
    bp, seq, _ = x_prompt.shape
    bs, tdec, _ = x_sample.shape
    l = 0

    w16 = jnp.pad(w_in[l], ((0, 0), (0, W_IN_PAD - w_in.shape[-1]))).astype(BF16)
    gvec = jnp.concatenate([b_igate[l], b_fgate[l]]).astype(F32)
    gb = jnp.pad(gvec, (0, BLK - 2 * ML_HEADS))[None, :]
    gbc = jnp.broadcast_to(jnp.pad(gvec, (0, T_GROWS - 2 * ML_HEADS))[:, None], (T_GROWS, BLK))
    mlg = ml_norm_g[l][None, :].astype(F32)
    sinks = swa_sinks[l].astype(F32)
    rb = rel_bias.astype(F32)
    g1, b1 = ln1_g[l][None, :], ln1_b[l][None, :]
    g2, b2 = ln2_g[l][None, :], ln2_b[l][None, :]

    mix_p, c_p, n_p, m_p, k_p, v_p, wt, wn, wo, wfi, wfo = _prompt_mixer(
        x_prompt, w16, gbc, mlg, sinks, rb, w_out[l], w_ffn_in[l], w_ffn_out[l])
    p_m =m_p[:, :ML_HEADS // 2, ::BLK].reshape(bp, ML_HEADS)[None]
    p_k = k_p.reshape(1, bp, WINDOW, SWA_KV_HEADS, SWA_HD)
    p_v = v_p.reshape(1, bp, WINDOW, SWA_KV_HEADS, SWA_HD)

    xs = x_sample.reshape(bs * tdec, D_MODEL)
    m0 = state_mlstm_m[l].astype(F32)
    mrep = jnp.pad(jnp.repeat(m0, tdec, axis=0), ((0, 0), (ML_HEADS, BLK - 2 * ML_HEADS)))
    wlen = cache_swa_k.shape[2]
    kc = cache_swa_k[l].reshape(bs, wlen, SWA_KV_HEADS * SWA_HD)
    vc = cache_swa_v[l].reshape(bs, wlen, SWA_KV_HEADS * SWA_HD)
    mix_s, c_s, n_s, mo_s, k_s, v_s = _sample_mixer(
        xs, wt, wn, gb, mlg, sinks, rb, mrep,
        state_mlstm_C[l].astype(F32), state_mlstm_n[l].astype(F32), kc, vc)
    s_m =mo_s.reshape(bs, tdec, BLK)[:, 0, ML_HEADS:2 * ML_HEADS][None]
    s_k = k_s.reshape(1, bs, wlen, SWA_KV_HEADS, SWA_HD)
    s_v = v_s.reshape(1, bs, wlen, SWA_KV_HEADS, SWA_HD)

    y_p, y_s = _dense(x_prompt.reshape(bp * seq, D_MODEL), mix_p.reshape(bp * seq, D_MODEL), xs, mix_s,
                      wo, g1, b1, wfi, wfo, g2, b2)
    y_p = y_p.reshape(bp, seq, D_MODEL)
    y_s = y_s.reshape(bs, tdec, D_MODEL)

    return (y_p, y_s, c_p[None], n_p[None], p_m, p_k, p_v,
            c_s[None], n_s[None], s_m, s_k, s_v)
```

```python
import functools
import math

import jax
import jax.numpy as jnp
from jax import lax
from jax.experimental import pallas as pl
from jax.experimental.pallas import tpu as pltpu

F32 = jnp.float32
BF16 = jnp.bfloat16

D_MODEL = 1024
ML_HEADS = 4
ML_DK = 128
ML_DV = 128
ML_WIDTH = ML_HEADS * ML_DV
GATE_SOFTCAP = 15.0
SWA_HEADS = 8
SWA_KV_HEADS = 2
SWA_GROUP = SWA_HEADS // SWA_KV_HEADS
SWA_HD = 64
SWA_WIDTH = SWA_HEADS * SWA_HD
WINDOW = 128
REL_BUCKETS = 32
REL_MAX_DIST = 128
D_FF = 2816
DEPTH = 1
ALPHA = (2.0 * DEPTH) ** 0.25
LN_EPS = 1e-5
NORM_EPS = 1e-6
NEG_INF = -1e30
IN_SIZES = (512, 512, 512, 512, 4, 4, 512, 128, 128)
ML_SCALE = ML_DK ** -0.5
LOG_ML_SCALE = math.log(ML_SCALE)
SWA_SCALE = SWA_HD ** -0.5

BLK = 128
SUBLANES_F32 = 8
SUBLANES_BF16 = 16

A_Q, A_K, A_V, A_SQ = 0, 512, 1024, 1536
A_COLS = 2048
B_MO, B_SK, B_SV, B_G = 0, 512, 640, 768
B_COLS = 896

N_K = 0
N_B = 512
T_Q, T_V, T_SQ, T_SV, T_G = 0, 512, 1024, 1536, 1664
T_GROWS = SUBLANES_BF16
T_COLS = T_G + BLK
W_IN_PAD = -(-sum(IN_SIZES) // BLK) * BLK
STATE_ROWS = ML_DV + SUBLANES_BF16
M_ROWS = SUBLANES_F32

TM_PROMPT = 1024
TM_DENSE = 1024
DENSE_SPLITS = (0, 512, 768, 1024)
FF_SPLITS = (0, 1536, D_FF)
SAMPLE_GB = 16
SLAB = 16
VMEM_LIMIT = 56 * 1024 * 1024


def _softcap(a):
    return GATE_SOFTCAP * jnp.tanh(a / GATE_SOFTCAP)


def _log_sigmoid(x):
    return jnp.minimum(x, 0.0) - jnp.log1p(jnp.exp(-jnp.abs(x)))


def _layer_norm(z, g, b):
    mu = jnp.mean(z, axis=-1, keepdims=True)
    zc = z - mu
    var = jnp.mean(zc * zc, axis=-1, keepdims=True)
    return zc * lax.rsqrt(var + LN_EPS) * g + b


def _dot(a, b):
    return jnp.dot(a, b, preferred_element_type=F32)


def _dot_nt(a, b):
    return lax.dot_general(a, b, (((1,), (1,)), ((), ())), preferred_element_type=F32)


def _dot_tn(a, b):
    return lax.dot_general(a, b, (((0,), (0,)), ((), ())), preferred_element_type=F32)


def _split3(x):
    hi = x.astype(BF16)
    r1 = x - hi.astype(F32)
    mid = r1.astype(BF16)
    lo = (r1 - mid.astype(F32)).astype(BF16)
    return hi, mid, lo


def _t5_bucket(d):
    n = jnp.maximum(d, 0)
    max_exact = REL_BUCKETS // 2
    nlog = REL_BUCKETS - max_exact
    large = jnp.full(n.shape, max_exact, jnp.int32)
    for k in range(1, nlog):
        thr = math.ceil(max_exact * (REL_MAX_DIST / max_exact) ** (k / nlog))
        large = large + jnp.where(n >= thr, 1, 0)
    return jnp.where(n < max_exact, n, large)


def _prompt_mixer_kernel(x_ref, w_ref, gbc_ref, mlg_ref, sinks_ref, rb_ref, wo32, wfi32, wfo32,
                         mix_ref, c_out, n_out, m_out, ko_ref, vo_ref,
                         wt, wn, wo16, wfi16, wfo16,
                         kn, pf, vlast, kband, q_t, v_t, sq_t, sv_t, g_t, ct, msc, tbl):
    wo16[...] = wo32[...].astype(BF16)
    wfi16[...] = wfi32[...].astype(BF16)
    wfo16[...] = wfo32[...].astype(BF16)

    b = pl.program_id(0)
    t = pl.program_id(1)
    nt = pl.num_programs(1)
    tm = x_ref.shape[1]
    nblk = tm // BLK
    pairs = SWA_HEADS // 2

    @pl.when((b == 0) & (t == 0))
    def _prepare_weights():
        offs = [0]
        for n in IN_SIZES:
            offs.append(offs[-1] + n)
        o_mq, o_mk, o_mv, o_mo, o_mi, _, o_sq, o_sk, o_sv = offs[:-1]

        def col_blocks(dst_col, src_col, n):
            for i in range(n // BLK):
                wn[:, dst_col + i * BLK:dst_col + (i + 1) * BLK] = (
                    w_ref[:, src_col + i * BLK:src_col + (i + 1) * BLK])

        def row_blocks(dst_row, src_col, n):
            for i in range(n // BLK):
                wt[dst_row + i * BLK:dst_row + (i + 1) * BLK, :] = (
                    w_ref[:, src_col + i * BLK:src_col + (i + 1) * BLK].T)

        col_blocks(N_K, o_mk, ML_HEADS * ML_DK)
        col_blocks(N_B + B_MO, o_mo, ML_WIDTH)
        col_blocks(N_B + B_SK, o_sk, SWA_KV_HEADS * SWA_HD)
        row_blocks(T_Q, o_mq, ML_HEADS * ML_DK)
        row_blocks(T_V, o_mv, ML_WIDTH)
        row_blocks(T_SQ, o_sq, SWA_WIDTH)
        row_blocks(T_SV, o_sv, SWA_KV_HEADS * SWA_HD)
        gates_t = w_ref[:, o_mi:o_mi + BLK].T
        grow = lax.broadcasted_iota(jnp.int32, gates_t.shape, 0)
        wt[T_G:T_G + BLK, :] = jnp.where(grow < 2 * ML_HEADS, gates_t, jnp.zeros_like(gates_t))

    @pl.when((b == 0) & (t == 0))
    def _build_bias_tables():
        r = lax.broadcasted_iota(jnp.int32, (BLK, 2 * BLK), 0)
        ln = lax.broadcasted_iota(jnp.int32, (BLK, 2 * BLK), 1)
        second = ln >= BLK
        qi = jnp.where(second, ln - BLK, ln)
        prev = r > qi
        d = jnp.where(prev, WINDOW + qi - r, qi - r)
        bucket = _t5_bucket(d)
        for p in range(pairs):
            acc = jnp.zeros((BLK, 2 * BLK), F32)
            for k in range(REL_BUCKETS):
                acc = jnp.where(bucket == k, jnp.where(second, rb_ref[k, p + pairs], rb_ref[k, p]), acc)
            tbl[0, p] = jnp.where(prev, NEG_INF, acc)
            tbl[1, p] = acc

    @pl.when(t == 0)
    def _reset_state():
        ct[...] = jnp.zeros_like(ct)
        msc[...] = jnp.zeros_like(msc)
        kband[0:BLK, :] = jnp.zeros((BLK, kband.shape[1]), kband.dtype)
        sv_t[0] = jnp.zeros(sv_t.shape[1:], sv_t.dtype)

    xb = x_ref[0].astype(BF16)
    res_n = _dot(xb, wn[...])
    kn[...] = res_n[:, N_K:N_K + ML_HEADS * ML_DK].astype(BF16)
    pf[...] = res_n[:, N_B:N_B + B_SV]
    kband[BLK:BLK + tm, :] = pf[:, B_SK:B_SK + BLK].astype(BF16)

    res_t = _dot_nt(wt[0:T_G + T_GROWS, :], xb)
    vlast[...] = res_t[T_SV:T_SV + BLK, tm - BLK:tm]

    def put(dst, r0, nrows, off, scale=None):
        res = res_t[r0:r0 + nrows, :]
        if scale is not None:
            res = res * scale
        for jj in range(nblk):
            dst[jj + off] = res[:, jj * BLK:(jj + 1) * BLK].astype(dst.dtype)

    put(q_t, T_Q, ML_HEADS * ML_DK, 0)
    put(v_t, T_V, ML_WIDTH, 0)
    put(sq_t, T_SQ, SWA_WIDTH, 0, SWA_SCALE)
    put(sv_t, T_SV, SWA_KV_HEADS * SWA_HD, 1)
    put(g_t, T_G, T_GROWS, 0)

    row = lax.broadcasted_iota(jnp.int32, (BLK, BLK), 0)
    col = lax.broadcasted_iota(jnp.int32, (BLK, BLK), 1)
    causal_t = row <= col
    triu = causal_t.astype(BF16)
    causal2 = jnp.concatenate([causal_t, causal_t], axis=1)
    r16 = lax.broadcasted_iota(jnp.int32, (STATE_ROWS - ML_DV, 2 * BLK), 0)
    ones_rows = (r16 == 0).astype(BF16)
    z128 = jnp.zeros((BLK, BLK), BF16)
    ml_pairs = ML_HEADS // 2
    zero_rows = jnp.zeros((BLK - T_GROWS, BLK), F32)
    gbc = gbc_ref[...]
    mlg = mlg_ref[...]
    lane2 = lax.broadcasted_iota(jnp.int32, (1, 2 * BLK), 1)
    snk_rows = [jnp.where(lane2 >= BLK, sinks_ref[p + pairs], sinks_ref[p]) for p in range(pairs)]
    zq = jnp.zeros((SWA_HD, BLK), BF16)
    prev2 = jnp.concatenate([row > col, row > col], axis=1)

    def pair_row(x0, x1):
        return jnp.concatenate([jnp.broadcast_to(x0, (1, BLK)), jnp.broadcast_to(x1, (1, BLK))], axis=1)

    def block_diag(x0, x1):
        return jnp.concatenate([jnp.concatenate([x0, z128], axis=1),
                                jnp.concatenate([z128, x1], axis=1)], axis=0)

    def block(j):
        r0 = j * BLK
        rows = pl.ds(r0, BLK)
        hsl = [slice(h * ML_DK, (h + 1) * ML_DK) for h in range(ML_HEADS)]
        psl = [slice(p * 2 * ML_DK, (p + 1) * 2 * ML_DK) for p in range(ml_pairs)]

        pre_t = _softcap(g_t[j] + gbc)
        hi, mid, lo = _split3(_log_sigmoid(pre_t))
        b_t = _dot(hi, triu) + _dot(mid, triu) + _dot(lo, triu)
        u_t = pltpu.roll(pre_t, ML_HEADS, 0) - b_t
        u_c = jnp.concatenate([u_t, zero_rows], axis=0).T

        k2 = [kn[rows, psl[p]] for p in range(ml_pairs)]
        bdq = [block_diag(q_t[j, hsl[2 * p], :], q_t[j, hsl[2 * p + 1], :]) for p in range(ml_pairs)]
        vte = [jnp.concatenate([jnp.concatenate([v_t[j, hsl[2 * p], :], v_t[j, hsl[2 * p + 1], :]], axis=1),
                                ones_rows], axis=0) for p in range(ml_pairs)]
        kq = [_dot(k2[p], bdq[p]) for p in range(ml_pairs)]

        sel = jnp.where(t == 0, 0, 1) if j == 0 else 1
        kb = kband[pl.ds(r0, 2 * BLK), :]
        vtb = jnp.concatenate([sv_t[j], sv_t[j + 1]], axis=1)
        sc2 = []
        for p in range(pairs):
            qa = sq_t[j, p * SWA_HD:(p + 1) * SWA_HD, :]
            qb = sq_t[j, (p + pairs) * SWA_HD:(p + pairs + 1) * SWA_HD, :]
            bd = jnp.concatenate([jnp.concatenate([qa, zq], axis=1),
                                  jnp.concatenate([zq, qb], axis=1)], axis=0)
            sc2.append(_dot(kb, bd))

        yield
        bds, scl, m_row, m_new, a_p, vwte = [], [], [], [], [], []
        for p in range(ml_pairs):
            h0, h1 = 2 * p, 2 * p + 1
            b_row = pair_row(b_t[ML_HEADS + h0:ML_HEADS + h0 + 1, :], b_t[ML_HEADS + h1:ML_HEADS + h1 + 1, :])
            i_row = pair_row(pre_t[h0:h0 + 1, :], pre_t[h1:h1 + 1, :])
            u_col = jnp.concatenate(
                [jnp.broadcast_to(u_c[:, ML_HEADS + h0:ML_HEADS + h0 + 1], (BLK, BLK)),
                 jnp.broadcast_to(u_c[:, ML_HEADS + h1:ML_HEADS + h1 + 1], (BLK, BLK))], axis=1)
            m_prev = msc[p:p + 1, :]
            log_d = jnp.where(causal2, b_row + u_col, -jnp.inf)
            inter = b_row + m_prev
            mr = jnp.maximum(inter, jnp.max(log_d, axis=0, keepdims=True))
            m_row.append(mr)
            scl.append(jnp.exp(inter - mr))
            s2 = (kq[p] * jnp.exp(log_d - (mr - LOG_ML_SCALE))).astype(BF16)
            bds.append(block_diag(s2[:, :BLK], s2[:, BLK:]))
            b_last = pair_row(b_row[:, BLK - 1:BLK], b_row[:, 2 * BLK - 1:2 * BLK])
            w_row = b_last - b_row + i_row
            w_max = pair_row(jnp.max(w_row[:, :BLK], axis=1, keepdims=True),
                             jnp.max(w_row[:, BLK:], axis=1, keepdims=True))
            mn = jnp.maximum(b_last + m_prev, w_max)
            m_new.append(mn)
            a_p.append(jnp.exp(b_last + m_prev - mn))
            wexp = jnp.exp(w_row - mn) * ML_SCALE
            vwte.append(vte[p] * wexp.astype(BF16))

        pw, den_p = [], []
        for p in range(pairs):
            sc = jnp.where(prev2, sc2[p][:BLK], sc2[p][BLK:]) + tbl[sel, p]
            snk = snk_rows[p]
            mx = jnp.maximum(jnp.max(sc, axis=0, keepdims=True), snk)
            pe = jnp.exp(sc - mx)
            den_p.append(jnp.sum(pe, axis=0, keepdims=True) + jnp.exp(snk - mx))
            pw.append(jnp.concatenate([jnp.where(prev2, pe, 0.0), jnp.where(prev2, 0.0, pe)],
                                      axis=0).astype(BF16))

        c_old = [ct[p] for p in range(ml_pairs)]
        cq = [_dot(c_old[p].astype(BF16), bdq[p]) for p in range(ml_pairs)]
        vs = [_dot(vte[p], bds[p]) for p in range(ml_pairs)]
        upd = [_dot(vwte[p], block_diag(k2[p][:, :BLK], k2[p][:, BLK:])) for p in range(ml_pairs)]
        o2 = [_dot(vtb, pw[p]) for p in range(pairs)]

        for p in range(ml_pairs):
            nd = scl[p] * cq[p] + vs[p]
            num = nd[:ML_DV]
            den = nd[ML_DV:ML_DV + 1]
            hh = num / jnp.maximum(jnp.abs(den), jnp.exp(-m_row[p]))
            hn = hh * lax.rsqrt(jnp.mean(hh * hh, axis=0, keepdims=True) + NORM_EPS)
            for half in range(2):
                h = 2 * p + half
                og = jax.nn.sigmoid(pf[rows, B_MO + h * ML_DV:B_MO + (h + 1) * ML_DV])
                hn_h = hn[:, half * BLK:(half + 1) * BLK].T
                mix_ref[0, rows, hsl[h]] = (hn_h * mlg[:, hsl[h]] * og).astype(mix_ref.dtype)
            ct[p] = a_p[p] * c_old[p] + upd[p]
            msc[p:p + 1, :] = m_new[p]

        o_parts = [None] * SWA_HEADS
        for p in range(pairs):
            on = o2[p] / den_p[p]
            o_parts[p] = on[0:SWA_HD, 0:BLK]
            o_parts[p + pairs] = on[SWA_HD:2 * SWA_HD, BLK:2 * BLK]
        o_t = jnp.concatenate(o_parts, axis=0)
        mix_ref[0, rows, ML_WIDTH:ML_WIDTH + SWA_WIDTH] = o_t.T.astype(mix_ref.dtype)
        yield

    blocks = [block(j) for j in range(nblk)]
    for g in blocks:
        next(g)
    for g in blocks:
        next(g)

    kband[0:BLK, :] = kband[tm:tm + BLK, :]
    sv_t[0] = sv_t[nblk]

    @pl.when(t == nt - 1)
    def _write_state():
        for h in range(ML_HEADS):
            p, lanes = h // 2, slice((h % 2) * ML_DK, (h % 2 + 1) * ML_DK)
            c_out[0, h] = ct[p, 0:ML_DV, lanes].T
            n_out[0, h:h + 1, :] = ct[p, ML_DV:ML_DV + 1, lanes]
        m_out[0] = msc[...]
        ko_ref[0] = pf[tm - BLK:tm, B_SK:B_SK + BLK]
        vo_ref[0] = vlast[...].T


def _prompt_mixer(x, w_in_l, gbc, mlg, sinks, rb, wo32, wfi32, wfo32):
    bsz, seq, _ = x.shape
    tm = TM_PROMPT
    nt = seq // tm
    nblk = tm // BLK
    steps = bsz * nt
    const2 = lambda b, t: (0, 0)
    smem = pl.BlockSpec(memory_space=pltpu.SMEM)

    def slab_spec(wmat):
        return pl.BlockSpec((wmat.shape[0] // steps, wmat.shape[1]), lambda b, t: (b * nt + t, 0))

    return pl.pallas_call(
        _prompt_mixer_kernel,
        grid=(bsz, nt),
        in_specs=[
            pl.BlockSpec((1, tm, D_MODEL), lambda b, t: (b, t, 0)),
            pl.BlockSpec((D_MODEL, W_IN_PAD), const2, pipeline_mode=pl.Buffered(1)),
            pl.BlockSpec((T_GROWS, BLK), const2),
            pl.BlockSpec((1, ML_WIDTH), const2),
            smem, smem,
            slab_spec(wo32), slab_spec(wfi32), slab_spec(wfo32),
        ],
        out_specs=[
            pl.BlockSpec((1, tm, D_MODEL), lambda b, t: (b, t, 0)),
            pl.BlockSpec((1, ML_HEADS, ML_DK, ML_DV), lambda b, t: (b, 0, 0, 0)),
            pl.BlockSpec((1, ML_HEADS, ML_DK), lambda b, t: (b, 0, 0)),
            pl.BlockSpec((1, M_ROWS, 2 * BLK), lambda b, t: (b, 0, 0)),
            pl.BlockSpec((1, BLK, BLK), lambda b, t: (b, 0, 0)),
            pl.BlockSpec((1, BLK, BLK), lambda b, t: (b, 0, 0)),
            pl.BlockSpec((T_COLS, D_MODEL), const2),
            pl.BlockSpec((D_MODEL, N_B + B_SV), const2),
            slab_spec(wo32), slab_spec(wfi32), slab_spec(wfo32),
        ],
        out_shape=[
            jax.ShapeDtypeStruct((bsz, seq, D_MODEL), BF16),
            jax.ShapeDtypeStruct((bsz, ML_HEADS, ML_DK, ML_DV), F32),
            jax.ShapeDtypeStruct((bsz, ML_HEADS, ML_DK), F32),
            jax.ShapeDtypeStruct((bsz, M_ROWS, 2 * BLK), F32),
            jax.ShapeDtypeStruct((bsz, BLK, BLK), F32),
            jax.ShapeDtypeStruct((bsz, BLK, BLK), F32),
            jax.ShapeDtypeStruct((T_COLS, D_MODEL), BF16),
            jax.ShapeDtypeStruct((D_MODEL, N_B + B_SV), BF16),
            jax.ShapeDtypeStruct(wo32.shape, BF16),
            jax.ShapeDtypeStruct(wfi32.shape, BF16),
            jax.ShapeDtypeStruct(wfo32.shape, BF16),
        ],
        scratch_shapes=[
            pltpu.VMEM((tm, ML_HEADS * ML_DK), BF16),
            pltpu.VMEM((tm, B_SV), F32),
            pltpu.VMEM((BLK, BLK), F32),
            pltpu.VMEM((tm + BLK, BLK), BF16),
            pltpu.VMEM((nblk, ML_HEADS * ML_DK, BLK), BF16),
            pltpu.VMEM((nblk, ML_WIDTH, BLK), BF16),
            pltpu.VMEM((nblk, SWA_WIDTH, BLK), BF16),
            pltpu.VMEM((nblk + 1, SWA_KV_HEADS * SWA_HD, BLK), BF16),
            pltpu.VMEM((nblk, T_GROWS, BLK), F32),
            pltpu.VMEM((ML_HEADS // 2, STATE_ROWS, 2 * ML_DK), F32),
            pltpu.VMEM((M_ROWS, 2 * BLK), F32),
            pltpu.VMEM((2, SWA_HEADS // 2, BLK, 2 * BLK), F32),
        ],
        compiler_params=pltpu.CompilerParams(
            dimension_semantics=("arbitrary", "arbitrary"),
            vmem_limit_bytes=VMEM_LIMIT),
        name="prompt_mixer",
    )(x, w_in_l, gbc, mlg, sinks, rb, wo32, wfi32, wfo32)


def _roll_rows(x, shift):
    return pltpu.roll(x, shift % x.shape[0], 0)


def _sample_mixer_kernel(x_ref, wt_ref, wn_ref, gb_ref, mlg_ref, sinks_ref, rb_ref, mrep_ref,
                         c_ref, n_ref, kc_ref, vc_ref,
                         mix_ref, c_out, n_out, mo_ref, ko_ref, vo_ref,
                         pa, pf, tblc, tbln):
    step = pl.program_id(0)
    tdec = 4
    rows_step = mix_ref.shape[0]
    nslab = rows_step // SLAB
    nseq = SLAB // tdec
    qrows = SWA_HEADS * SLAB

    @pl.when(step == 0)
    def _project_all_rows():
        xb = x_ref[...].astype(BF16)
        pa[:, A_Q:A_K] = _dot_nt(xb, wt_ref[T_Q:T_Q + ML_HEADS * ML_DK, :])
        pa[:, A_K:A_V] = _dot(xb, wn_ref[:, N_K:N_K + ML_HEADS * ML_DK])
        pa[:, A_V:A_COLS] = _dot_nt(xb, wt_ref[T_V:T_V + ML_WIDTH + SWA_WIDTH, :])
        pf[:, B_MO:B_SV] = _dot(xb, wn_ref[:, N_B:N_B + B_SV])
        pf[:, B_SV:B_COLS] = _dot_nt(xb, wt_ref[T_SV:T_COLS, :])

    @pl.when(step == 0)
    def _build_bias_tables():
        r = lax.broadcasted_iota(jnp.int32, (qrows, BLK), 0)
        c = lax.broadcasted_iota(jnp.int32, (qrows, BLK), 1)
        hq = r // SLAB
        bl = (r % SLAB) // tdec
        tq = r % tdec
        d_cache = WINDOW + tq - c
        d_new = tq - (c % tdec)
        key_cache = _t5_bucket(d_cache) * SWA_HEADS + hq
        key_new = _t5_bucket(d_new) * SWA_HEADS + hq
        acc_c = jnp.zeros((qrows, BLK), F32)
        acc_n = jnp.zeros((qrows, BLK), F32)
        for k in range(REL_BUCKETS):
            for h in range(SWA_HEADS):
                val = rb_ref[k, h]
                acc_c = jnp.where(key_cache == k * SWA_HEADS + h, val, acc_c)
                acc_n = jnp.where(key_new == k * SWA_HEADS + h, val, acc_n)
        tblc[...] = jnp.where(c > tq, acc_c, NEG_INF)
        tbln[...] = jnp.where((c < SLAB) & ((c // tdec) == bl) & (d_new >= 0), acc_n, NEG_INF)

    gbias = gb_ref[...]
    mlg = mlg_ref[...]
    r16 = lax.broadcasted_iota(jnp.int32, (SLAB, BLK), 0)
    rr = r16 % tdec
    bl16 = r16 // tdec
    rq = lax.broadcasted_iota(jnp.int32, (qrows, BLK), 0)
    blq = (rq % SLAB) // tdec
    hq_col = rq[:, 0:1] // SLAB
    snk = jnp.zeros((qrows, 1), F32)
    for h in range(SWA_HEADS):
        snk = jnp.where(hq_col == h, sinks_ref[h], snk)
    first_group = rq[:, 0:SWA_HD] < SWA_GROUP * SLAB
    zq = jnp.zeros((SLAB, SWA_HD), F32)
    zeros_pad = jnp.zeros((BLK - SLAB, BLK), BF16)

    def seg_last(x):
        return jnp.where(rr == 3, x,
                         jnp.where(rr == 2, _roll_rows(x, -1),
                                   jnp.where(rr == 1, _roll_rows(x, -2), _roll_rows(x, -3))))

    def seg_max(x):
        m1 = jnp.maximum(x, jnp.where(rr % 2 == 0, _roll_rows(x, -1), _roll_rows(x, 1)))
        return jnp.maximum(m1, jnp.where(rr < 2, _roll_rows(m1, -2), _roll_rows(m1, 2)))

    def slab(si):
        rows = pl.ds(si * SLAB, SLAB)
        prow = pl.ds(pl.multiple_of(step * rows_step + si * SLAB, SLAB), SLAB)
        seq0 = si * nseq

        pre = _softcap(pf[prow, B_G:B_G + BLK] + gbias)
        logf = _log_sigmoid(pre)
        y = logf + jnp.where(rr >= 1, _roll_rows(logf, 1), 0.0)
        bcum = y + jnp.where(rr >= 2, _roll_rows(y, 2), 0.0)
        ig = pltpu.roll(pre, ML_HEADS, 1)
        m_prev = mrep_ref[rows, :]
        b_minus_i = bcum - ig
        log_d = [jnp.where(rr >= dl, bcum - _roll_rows(b_minus_i, dl), -jnp.inf) for dl in range(tdec)]
        rowmax = functools.reduce(jnp.maximum, log_d)
        inter = bcum + m_prev
        m_row = jnp.maximum(inter, rowmax)
        scl = jnp.exp(inter - m_row)
        dm = [jnp.exp(ld - m_row) for ld in log_d]
        enm = jnp.exp(-m_row)
        b_last = seg_last(bcum)
        w = b_last - bcum + ig
        m_new = jnp.maximum(b_last + m_prev, seg_max(w))
        a = jnp.exp(b_last + m_prev - m_new)
        kws = jnp.exp(w - m_new) * ML_SCALE
        mo_ref[rows, :] = m_new

        q = [pa[prow, A_Q + h * ML_DK:A_Q + (h + 1) * ML_DK] for h in range(ML_HEADS)]
        k = [pa[prow, A_K + h * ML_DK:A_K + (h + 1) * ML_DK] for h in range(ML_HEADS)]
        v = [pa[prow, A_V + h * ML_DV:A_V + (h + 1) * ML_DV] for h in range(ML_HEADS)]
        c_b = [[c_ref[seq0 + bl, h] for bl in range(nseq)] for h in range(ML_HEADS)]
        qc, upd, kw = [], [], []
        for h in range(ML_HEADS):
            lane = ML_HEADS + h
            qb = q[h].astype(BF16)
            res = [_dot(qb, jnp.concatenate([c_b[h][2 * i].astype(BF16), c_b[h][2 * i + 1].astype(BF16)], axis=1))
                   for i in range(nseq // 2)]
            parts = [res[bl // 2][:, (bl % 2) * ML_DV:(bl % 2 + 1) * ML_DV] for bl in range(nseq)]
            qc_h = parts[nseq - 1]
            for bl in range(nseq - 2, -1, -1):
                qc_h = jnp.where(bl16 == bl, parts[bl], qc_h)
            qc.append(qc_h)
            kw_h = k[h] * kws[:, lane:lane + 1]
            kw.append(kw_h)
            vm = jnp.concatenate([jnp.where(bl16 == bl, v[h], 0.0) for bl in range(nseq)], axis=1)
            upd.append(_dot_tn(kw_h.astype(BF16), vm.astype(BF16)))

        knew = pf[prow, B_SK:B_SK + BLK]
        vnew = pf[prow, B_SV:B_SV + BLK]
        qparts = []
        for hq in range(SWA_HEADS):
            qh = pa[prow, A_SQ + hq * SWA_HD:A_SQ + (hq + 1) * SWA_HD]
            qparts.append(jnp.concatenate([qh, zq] if hq < SWA_GROUP else [zq, qh], axis=1))
        qs = jnp.concatenate(qparts, axis=0).astype(BF16)
        s_c = _dot_nt(qs, kc_ref[seq0 + nseq - 1].astype(BF16))
        for bl in range(nseq - 2, -1, -1):
            s_c = jnp.where(blq == bl, _dot_nt(qs, kc_ref[seq0 + bl].astype(BF16)), s_c)
        kn = jnp.concatenate([knew.astype(BF16), zeros_pad], axis=0)
        s_n = _dot_nt(qs, kn)

        yield
        for h in range(ML_HEADS):
            hs = slice(h * ML_DK, (h + 1) * ML_DK)
            lane = ML_HEADS + h
            bm = jnp.zeros((SLAB, ML_DV), F32)
            sum_s = jnp.zeros((SLAB, 1), F32)
            for dl in range(tdec):
                s_dl = jnp.sum(q[h] * _roll_rows(k[h], dl), axis=-1, keepdims=True)
                ws = s_dl * ML_SCALE * dm[dl][:, lane:lane + 1]
                bm = bm + ws * _roll_rows(v[h], dl)
                sum_s = sum_s + ws
            nsel = jnp.zeros((SLAB, ML_DK), F32)
            for bl in range(nseq):
                msk = bl16 == bl
                n_b = n_ref[seq0 + bl][h:h + 1, :]
                nsel = jnp.where(msk, n_b, nsel)
                a_b = a[bl * tdec:bl * tdec + 1, lane:lane + 1]
                c_out[seq0 + bl, h] = a_b * c_b[h][bl] + upd[h][:, bl * ML_DV:(bl + 1) * ML_DV]
                n_out[seq0 + bl, h:h + 1, :] = a_b * n_b + jnp.sum(jnp.where(msk, kw[h], 0.0), axis=0,
                                                                   keepdims=True)
            sclh = scl[:, lane:lane + 1]
            num = sclh * qc[h] + bm
            den = sclh * jnp.sum(q[h] * nsel, axis=-1, keepdims=True) + sum_s
            hh = num / jnp.maximum(jnp.abs(den), enm[:, lane:lane + 1])
            hn = hh * lax.rsqrt(jnp.mean(hh * hh, axis=-1, keepdims=True) + NORM_EPS)
            og = jax.nn.sigmoid(pf[prow, B_MO + h * ML_DV:B_MO + (h + 1) * ML_DV])
            mix_ref[rows, hs] = (hn * mlg[:, hs] * og).astype(mix_ref.dtype)

        s_c = s_c * SWA_SCALE + tblc[...]
        s_n = s_n * SWA_SCALE + tbln[...]
        mx = jnp.maximum(jnp.maximum(jnp.max(s_c, axis=-1, keepdims=True),
                                     jnp.max(s_n, axis=-1, keepdims=True)), snk)
        p_c = jnp.exp(s_c - mx)
        p_n = jnp.exp(s_n - mx)
        den = (jnp.sum(p_c, axis=-1, keepdims=True) + jnp.sum(p_n, axis=-1, keepdims=True)
               + jnp.exp(snk - mx))

        yield
        vn = jnp.concatenate([vnew.astype(BF16), zeros_pad], axis=0)
        o = _dot(p_n.astype(BF16), vn)
        for bl in range(nseq):
            o = o + _dot(jnp.where(blq == bl, p_c, 0.0).astype(BF16), vc_ref[seq0 + bl].astype(BF16))
        o = o / den
        osel = jnp.where(first_group, o[:, 0:SWA_HD], o[:, SWA_HD:2 * SWA_HD])
        og = jnp.concatenate([osel[hq * SLAB:(hq + 1) * SLAB, :] for hq in range(SWA_HEADS)], axis=1)
        mix_ref[rows, ML_WIDTH:ML_WIDTH + SWA_WIDTH] = og.astype(mix_ref.dtype)

        for bl in range(nseq):
            ko_ref[seq0 + bl, 0:WINDOW - tdec, :] = kc_ref[seq0 + bl, tdec:WINDOW, :]
            vo_ref[seq0 + bl, 0:WINDOW - tdec, :] = vc_ref[seq0 + bl, tdec:WINDOW, :]
            ko_ref[seq0 + bl, WINDOW - tdec:WINDOW, :] = knew[bl * tdec:(bl + 1) * tdec, :]
            vo_ref[seq0 + bl, WINDOW - tdec:WINDOW, :] = vnew[bl * tdec:(bl + 1) * tdec, :]
        yield

    slabs = [slab(si) for si in range(nslab)]
    for _ in range(3):
        for s in slabs:
            next(s)


def _sample_mixer(x2d, wt, wn, gb, mlg, sinks, rb, mrep, c0, n0, kc, vc):
    rows = x2d.shape[0]
    tdec = 4
    nseq = rows // tdec
    gbs = SAMPLE_GB
    steps = nseq // gbs
    rb_rows = gbs * tdec
    const2 = lambda i: (0, 0)
    smem = pl.BlockSpec(memory_space=pltpu.SMEM)
    return pl.pallas_call(
        _sample_mixer_kernel,
        grid=(steps,),
        in_specs=[
            pl.BlockSpec((rows, D_MODEL), const2, pipeline_mode=pl.Buffered(1)),
            pl.BlockSpec((T_COLS, D_MODEL), const2, pipeline_mode=pl.Buffered(1)),
            pl.BlockSpec((D_MODEL, N_B + B_SV), const2, pipeline_mode=pl.Buffered(1)),
            pl.BlockSpec((1, BLK), const2),
            pl.BlockSpec((1, ML_WIDTH), const2),
            smem, smem,
            pl.BlockSpec((rb_rows, BLK), lambda i: (i, 0)),
            pl.BlockSpec((gbs, ML_HEADS, ML_DK, ML_DV), lambda i: (i, 0, 0, 0)),
            pl.BlockSpec((gbs, ML_HEADS, ML_DK), lambda i: (i, 0, 0)),
            pl.BlockSpec((gbs, WINDOW, BLK), lambda i: (i, 0, 0)),
            pl.BlockSpec((gbs, WINDOW, BLK), lambda i: (i, 0, 0)),
        ],
        out_specs=[
            pl.BlockSpec((rb_rows, D_MODEL), lambda i: (i, 0)),
            pl.BlockSpec((gbs, ML_HEADS, ML_DK, ML_DV), lambda i: (i, 0, 0, 0)),
            pl.BlockSpec((gbs, ML_HEADS, ML_DK), lambda i: (i, 0, 0)),
            pl.BlockSpec((rb_rows, BLK), lambda i: (i, 0)),
            pl.BlockSpec((gbs, WINDOW, BLK), lambda i: (i, 0, 0)),
            pl.BlockSpec((gbs, WINDOW, BLK), lambda i: (i, 0, 0)),
        ],
        out_shape=[
            jax.ShapeDtypeStruct((rows, D_MODEL), BF16),
            jax.ShapeDtypeStruct((nseq, ML_HEADS, ML_DK, ML_DV), F32),
            jax.ShapeDtypeStruct((nseq, ML_HEADS, ML_DK), F32),
            jax.ShapeDtypeStruct((rows, BLK), F32),
            jax.ShapeDtypeStruct((nseq, WINDOW, BLK), F32),
            jax.ShapeDtypeStruct((nseq, WINDOW, BLK), F32),
        ],
        scratch_shapes=[
            pltpu.VMEM((rows, A_COLS), F32),
            pltpu.VMEM((rows, B_COLS), F32),
            pltpu.VMEM((SWA_HEADS * SLAB, BLK), F32),
            pltpu.VMEM((SWA_HEADS * SLAB, BLK), F32),
        ],
        compiler_params=pltpu.CompilerParams(
            dimension_semantics=("arbitrary",),
            vmem_limit_bytes=VMEM_LIMIT),
        name="sample_mixer",
    )(x2d, wt, wn, gb, mlg, sinks, rb, mrep, c0, n0, kc, vc)


def _dense_kernel(x_ref, mix_ref, xs_ref, mixs_ref, wo_ref, g1_ref, b1_ref, wfi_ref, wfo_ref, g2_ref, b2_ref,
                  y_ref, ys_ref):
    bounds = [b for b in DENSE_SPLITS if b <= x_ref.shape[0]]
    nsub = len(bounds) - 1
    subs = [slice(bounds[i], bounds[i + 1]) for i in range(nsub)]
    chunks = list(zip(FF_SPLITS[:-1], FF_SPLITS[1:]))
    g1, b1, g2, b2 = g1_ref[...], b1_ref[...], g2_ref[...], b2_ref[...]

    xin = [x_ref[s, :] for s in subs]
    mixin = [mix_ref[s, :] for s in subs]
    xin[-1] = jnp.concatenate([xin[-1], xs_ref[...]], axis=0)
    mixin[-1] = jnp.concatenate([mixin[-1], mixs_ref[...]], axis=0)

    proj = [_dot(mixin[i], wo_ref[...]) for i in range(nsub)]
    h1 = [_layer_norm(ALPHA * xin[i] + proj[i], g1, b1) for i in range(nsub)]
    h1b = [h.astype(BF16) for h in h1]
    acc = [None] * nsub
    for c0, c1 in chunks:
        gate, up = [], []
        for i in range(nsub):
            gate.append(_dot(h1b[i], wfi_ref[:, c0:c1]))
            up.append(_dot(h1b[i], wfi_ref[:, D_FF + c0:D_FF + c1]))
        for i in range(nsub):
            act = (gate[i] * jax.nn.sigmoid(gate[i]) * up[i]).astype(BF16)
            part = _dot(act, wfo_ref[c0:c1, :])
            acc[i] = part if acc[i] is None else acc[i] + part
    for i, s in enumerate(subs):
        out = _layer_norm(ALPHA * h1[i] + acc[i], g2, b2)
        nrows = s.stop - s.start
        y_ref[s, :] = out[:nrows]
        if i == nsub - 1:
            ys_ref[...] = out[nrows:]


def _dense(x2d, mix2d, xs2d, mixs2d, wo, g1, b1, wfi, wfo, g2, b2):
    rows = x2d.shape[0]
    tm = TM_DENSE
    steps = rows // tm
    srows = xs2d.shape[0] // steps
    const2 = lambda i: (0, 0)
    step2 = lambda i: (i, 0)

    def wspec(shape):
        return pl.BlockSpec(shape, const2, pipeline_mode=pl.Buffered(1))

    return pl.pallas_call(
        _dense_kernel,
        grid=(steps,),
        in_specs=[
            pl.BlockSpec((tm, D_MODEL), step2),
            pl.BlockSpec((tm, D_MODEL), step2),
            pl.BlockSpec((srows, D_MODEL), step2),
            pl.BlockSpec((srows, D_MODEL), step2),
            wspec((D_MODEL, D_MODEL)),
            wspec((1, D_MODEL)), wspec((1, D_MODEL)),
            wspec((D_MODEL, 2 * D_FF)),
            wspec((D_FF, D_MODEL)),
            wspec((1, D_MODEL)), wspec((1, D_MODEL)),
        ],
        out_specs=[pl.BlockSpec((tm, D_MODEL), step2), pl.BlockSpec((srows, D_MODEL), step2)],
        out_shape=[jax.ShapeDtypeStruct((rows, D_MODEL), F32),
                   jax.ShapeDtypeStruct((xs2d.shape[0], D_MODEL), F32)],
        compiler_params=pltpu.CompilerParams(
            dimension_semantics=("arbitrary",),
            vmem_limit_bytes=VMEM_LIMIT),
        name="dense",
    )(x2d, mix2d, xs2d, mixs2d, wo, g1, b1, wfi, wfo, g2, b2)


def kernel(x_prompt, x_sample, state_mlstm_C, state_mlstm_n, state_mlstm_m, cache_swa_k, cache_swa_v,
           w_in, b_igate, b_fgate, ml_norm_g, swa_sinks, rel_bias, w_out, ln1_g, ln1_b,
           w_ffn_in, w_ffn_out, ln2_g, ln2_b):
    bp, seq, _ = x_prompt.shape
    bs, tdec, _ = x_sample.shape
    l = 0

    w16 = jnp.pad(w_in[l], ((0, 0), (0, W_IN_PAD - w_in.shape[-1]))).astype(BF16)
    gvec = jnp.concatenate([b_igate[l], b_fgate[l]]).astype(F32)
    gb = jnp.pad(gvec, (0, BLK - 2 * ML_HEADS))[None, :]
    gbc = jnp.broadcast_to(jnp.pad(gvec, (0, T_GROWS - 2 * ML_HEADS))[:, None], (T_GROWS, BLK))
    mlg = ml_norm_g[l][None, :].astype(F32)
    sinks = swa_sinks[l].astype(F32)
    rb = rel_bias.astype(F32)
    g1, b1 = ln1_g[l][None, :], ln1_b[l][None, :]
    g2, b2 = ln2_g[l][None, :], ln2_b[l][None, :]

    mix_p, c_p, n_p, m_p, k_p, v_p, wt, wn, wo, wfi, wfo = _prompt_mixer(
        x_prompt, w16, gbc, mlg, sinks, rb, w_out[l], w_ffn_in[l], w_ffn_out[l])
    p_m =m_p[:, :ML_HEADS // 2, ::BLK].reshape(bp, ML_HEADS)[None]
    p_k = k_p.reshape(1, bp, WINDOW, SWA_KV_HEADS, SWA_HD)
    p_v = v_p.reshape(1, bp, WINDOW, SWA_KV_HEADS, SWA_HD)

    xs = x_sample.reshape(bs * tdec, D_MODEL)
    m0 = state_mlstm_m[l].astype(F32)
    mrep = jnp.pad(jnp.repeat(m0, tdec, axis=0), ((0, 0), (ML_HEADS, BLK - 2 * ML_HEADS)))
    wlen = cache_swa_k.shape[2]
    kc = cache_swa_k[l].reshape(bs, wlen, SWA_KV_HEADS * SWA_HD)
    vc = cache_swa_v[l].reshape(bs, wlen, SWA_KV_HEADS * SWA_HD)
    mix_s, c_s, n_s, mo_s, k_s, v_s = _sample_mixer(
        xs, wt, wn, gb, mlg, sinks, rb, mrep,
        state_mlstm_C[l].astype(F32), state_mlstm_n[l].astype(F32), kc, vc)
    s_m =mo_s.reshape(bs, tdec, BLK)[:, 0, ML_HEADS:2 * ML_HEADS][None]
    s_k = k_s.reshape(1, bs, wlen, SWA_KV_HEADS, SWA_HD)
    s_v = v_s.reshape(1, bs, wlen, SWA_KV_HEADS, SWA_HD)

    y_p, y_s = _dense(x_prompt.reshape(bp * seq, D_MODEL), mix_p.reshape(bp * seq, D_MODEL), xs, mix_s,
                      wo, g1, b1, wfi, wfo, g2, b2)
    y_p = y_p.reshape(bp, seq, D_MODEL)
    y_s = y_s.reshape(bs, tdec, D_MODEL)

    return (y_p, y_s, c_p[None], n_p[None], p_m, p_k, p_v,
            c_s[None], n_s[None], s_m, s_k, s_v)
```

```python
import functools
import math

import jax
import jax.numpy as jnp
from jax import lax
from jax.experimental import pallas as pl
from jax.experimental.pallas import tpu as pltpu

F32 = jnp.float32
BF16 = jnp.bfloat16

D_MODEL = 1024
ML_HEADS = 4
ML_DK = 128
ML_DV = 128
ML_WIDTH = ML_HEADS * ML_DV
GATE_SOFTCAP = 15.0
SWA_HEADS = 8
SWA_KV_HEADS = 2
SWA_GROUP = SWA_HEADS // SWA_KV_HEADS
SWA_HD = 64
SWA_WIDTH = SWA_HEADS * SWA_HD
WINDOW = 128
REL_BUCKETS = 32
REL_MAX_DIST = 128
D_FF = 2816
DEPTH = 1
ALPHA = (2.0 * DEPTH) ** 0.25
LN_EPS = 1e-5
NORM_EPS = 1e-6
NEG_INF = -1e30
IN_SIZES = (512, 512, 512, 512, 4, 4, 512, 128, 128)
ML_SCALE = ML_DK ** -0.5
LOG_ML_SCALE = math.log(ML_SCALE)
SWA_SCALE = SWA_HD ** -0.5

BLK = 128
SUBLANES_F32 = 8
SUBLANES_BF16 = 16

A_Q, A_K, A_V, A_SQ = 0, 512, 1024, 1536
A_COLS = 2048
B_MO, B_SK, B_SV, B_G = 0, 512, 640, 768
B_COLS = 896

N_K = 0
N_B = 512
T_Q, T_V, T_SQ, T_SV, T_G = 0, 512, 1024, 1536, 1664
T_GROWS = SUBLANES_BF16
T_COLS = T_G + BLK
W_IN_PAD = -(-sum(IN_SIZES) // BLK) * BLK
STATE_ROWS = ML_DV + SUBLANES_BF16
M_ROWS = SUBLANES_F32

TM_PROMPT = 1024
TM_DENSE = 1024
DENSE_SPLITS = (0, 512, 768, 1024)
FF_SPLITS = (0, 1536, D_FF)
SAMPLE_GB = 16
SLAB = 16
VMEM_LIMIT = 56 * 1024 * 1024


def _softcap(a):
    return GATE_SOFTCAP * jnp.tanh(a / GATE_SOFTCAP)


def _log_sigmoid(x):
    return jnp.minimum(x, 0.0) - jnp.log1p(jnp.exp(-jnp.abs(x)))


def _layer_norm(z, g, b):
    mu = jnp.mean(z, axis=-1, keepdims=True)
    zc = z - mu
    var = jnp.mean(zc * zc, axis=-1, keepdims=True)
    return zc * lax.rsqrt(var + LN_EPS) * g + b


def _dot(a, b):
    return jnp.dot(a, b, preferred_element_type=F32)


def _dot_nt(a, b):
    return lax.dot_general(a, b, (((1,), (1,)), ((), ())), preferred_element_type=F32)


def _dot_tn(a, b):
    return lax.dot_general(a, b, (((0,), (0,)), ((), ())), preferred_element_type=F32)


def _split3(x):
    hi = x.astype(BF16)
    r1 = x - hi.astype(F32)
    mid = r1.astype(BF16)
    lo = (r1 - mid.astype(F32)).astype(BF16)
    return hi, mid, lo


def _t5_bucket(d):
    n = jnp.maximum(d, 0)
    max_exact = REL_BUCKETS // 2
    nlog = REL_BUCKETS - max_exact
    large = jnp.full(n.shape, max_exact, jnp.int32)
    for k in range(1, nlog):
        thr = math.ceil(max_exact * (REL_MAX_DIST / max_exact) ** (k / nlog))
        large = large + jnp.where(n >= thr, 1, 0)
    return jnp.where(n < max_exact, n, large)


def _prompt_mixer_kernel(x_ref, w_ref, gbc_ref, mlg_ref, sinks_ref, rb_ref, wo32, wfi32, wfo32,
                         mix_ref, c_out, n_out, m_out, ko_ref, vo_ref,
                         wt, wn, wo16, wfi16, wfo16,
                         kn, pf, vlast, kband, q_t, v_t, sq_t, sv_t, g_t, ct, msc, tbl):
    wo16[...] = wo32[...].astype(BF16)
    wfi16[...] = wfi32[...].astype(BF16)
    wfo16[...] = wfo32[...].astype(BF16)

    b = pl.program_id(0)
    t = pl.program_id(1)
    nt = pl.num_programs(1)
    tm = x_ref.shape[1]
    nblk = tm // BLK
    pairs = SWA_HEADS // 2

    @pl.when((b == 0) & (t == 0))
    def _prepare_weights():
        offs = [0]
        for n in IN_SIZES:
            offs.append(offs[-1] + n)
        o_mq, o_mk, o_mv, o_mo, o_mi, _, o_sq, o_sk, o_sv = offs[:-1]

        def col_blocks(dst_col, src_col, n):
            for i in range(n // BLK):
                wn[:, dst_col + i * BLK:dst_col + (i + 1) * BLK] = (
                    w_ref[:, src_col + i * BLK:src_col + (i + 1) * BLK])

        def row_blocks(dst_row, src_col, n):
            for i in range(n // BLK):
                wt[dst_row + i * BLK:dst_row + (i + 1) * BLK, :] = (
                    w_ref[:, src_col + i * BLK:src_col + (i + 1) * BLK].T)

        col_blocks(N_K, o_mk, ML_HEADS * ML_DK)
        col_blocks(N_B + B_MO, o_mo, ML_WIDTH)
        col_blocks(N_B + B_SK, o_sk, SWA_KV_HEADS * SWA_HD)
        row_blocks(T_Q, o_mq, ML_HEADS * ML_DK)
        row_blocks(T_V, o_mv, ML_WIDTH)
        row_blocks(T_SQ, o_sq, SWA_WIDTH)
        row_blocks(T_SV, o_sv, SWA_KV_HEADS * SWA_HD)
        gates_t = w_ref[:, o_mi:o_mi + BLK].T
        grow = lax.broadcasted_iota(jnp.int32, gates_t.shape, 0)
        wt[T_G:T_G + BLK, :] = jnp.where(grow < 2 * ML_HEADS, gates_t, jnp.zeros_like(gates_t))

    @pl.when((b == 0) & (t == 0))
    def _build_bias_tables():
        r = lax.broadcasted_iota(jnp.int32, (BLK, 2 * BLK), 0)
        ln = lax.broadcasted_iota(jnp.int32, (BLK, 2 * BLK), 1)
        second = ln >= BLK
        qi = jnp.where(second, ln - BLK, ln)
        prev = r > qi
        d = jnp.where(prev, WINDOW + qi - r, qi - r)
        bucket = _t5_bucket(d)
        for p in range(pairs):
            acc = jnp.zeros((BLK, 2 * BLK), F32)
            for k in range(REL_BUCKETS):
                acc = jnp.where(bucket == k, jnp.where(second, rb_ref[k, p + pairs], rb_ref[k, p]), acc)
            tbl[0, p] = jnp.where(prev, NEG_INF, acc)
            tbl[1, p] = acc

    @pl.when(t == 0)
    def _reset_state():
        ct[...] = jnp.zeros_like(ct)
        msc[...] = jnp.zeros_like(msc)
        kband[0:BLK, :] = jnp.zeros((BLK, kband.shape[1]), kband.dtype)
        sv_t[0] = jnp.zeros(sv_t.shape[1:], sv_t.dtype)

    xb = x_ref[0].astype(BF16)
    res_n = _dot(xb, wn[...])
    kn[...] = res_n[:, N_K:N_K + ML_HEADS * ML_DK].astype(BF16)
    pf[...] = res_n[:, N_B:N_B + B_SV]
    kband[BLK:BLK + tm, :] = pf[:, B_SK:B_SK + BLK].astype(BF16)

    res_t = _dot_nt(wt[0:T_G + T_GROWS, :], xb)
    vlast[...] = res_t[T_SV:T_SV + BLK, tm - BLK:tm]

    def put(dst, r0, nrows, off, scale=None):
        res = res_t[r0:r0 + nrows, :]
        if scale is not None:
            res = res * scale
        for jj in range(nblk):
            dst[jj + off] = res[:, jj * BLK:(jj + 1) * BLK].astype(dst.dtype)

    put(q_t, T_Q, ML_HEADS * ML_DK, 0)
    put(v_t, T_V, ML_WIDTH, 0)
    put(sq_t, T_SQ, SWA_WIDTH, 0, SWA_SCALE)
    put(sv_t, T_SV, SWA_KV_HEADS * SWA_HD, 1)
    put(g_t, T_G, T_GROWS, 0)

    row = lax.broadcasted_iota(jnp.int32, (BLK, BLK), 0)
    col = lax.broadcasted_iota(jnp.int32, (BLK, BLK), 1)
    causal_t = row <= col
    triu = causal_t.astype(BF16)
    causal2 = jnp.concatenate([causal_t, causal_t], axis=1)
    r16 = lax.broadcasted_iota(jnp.int32, (STATE_ROWS - ML_DV, 2 * BLK), 0)
    ones_rows = (r16 == 0).astype(BF16)
    z128 = jnp.zeros((BLK, BLK), BF16)
    ml_pairs = ML_HEADS // 2
    zero_rows = jnp.zeros((BLK - T_GROWS, BLK), F32)
    gbc = gbc_ref[...]
    mlg = mlg_ref[...]
    lane2 = lax.broadcasted_iota(jnp.int32, (1, 2 * BLK), 1)
    snk_rows = [jnp.where(lane2 >= BLK, sinks_ref[p + pairs], sinks_ref[p]) for p in range(pairs)]
    zq = jnp.zeros((SWA_HD, BLK), BF16)
    prev2 = jnp.concatenate([row > col, row > col], axis=1)

    def pair_row(x0, x1):
        return jnp.concatenate([jnp.broadcast_to(x0, (1, BLK)), jnp.broadcast_to(x1, (1, BLK))], axis=1)

    def block_diag(x0, x1):
        return jnp.concatenate([jnp.concatenate([x0, z128], axis=1),
                                jnp.concatenate([z128, x1], axis=1)], axis=0)

    def block(j):
        r0 = j * BLK
        rows = pl.ds(r0, BLK)
        hsl = [slice(h * ML_DK, (h + 1) * ML_DK) for h in range(ML_HEADS)]
        psl = [slice(p * 2 * ML_DK, (p + 1) * 2 * ML_DK) for p in range(ml_pairs)]

        pre_t = _softcap(g_t[j] + gbc)
        hi, mid, lo = _split3(_log_sigmoid(pre_t))
        b_t = _dot(hi, triu) + _dot(mid, triu) + _dot(lo, triu)
        u_t = pltpu.roll(pre_t, ML_HEADS, 0) - b_t
        u_c = jnp.concatenate([u_t, zero_rows], axis=0).T

        k2 = [kn[rows, psl[p]] for p in range(ml_pairs)]
        bdq = [block_diag(q_t[j, hsl[2 * p], :], q_t[j, hsl[2 * p + 1], :]) for p in range(ml_pairs)]
        vte = [jnp.concatenate([jnp.concatenate([v_t[j, hsl[2 * p], :], v_t[j, hsl[2 * p + 1], :]], axis=1),
                                ones_rows], axis=0) for p in range(ml_pairs)]
        kq = [_dot(k2[p], bdq[p]) for p in range(ml_pairs)]

        sel = jnp.where(t == 0, 0, 1) if j == 0 else 1
        kb = kband[pl.ds(r0, 2 * BLK), :]
        vtb = jnp.concatenate([sv_t[j], sv_t[j + 1]], axis=1)
        sc2 = []
        for p in range(pairs):
            qa = sq_t[j, p * SWA_HD:(p + 1) * SWA_HD, :]
            qb = sq_t[j, (p + pairs) * SWA_HD:(p + pairs + 1) * SWA_HD, :]
            bd = jnp.concatenate([jnp.concatenate([qa, zq], axis=1),
                                  jnp.concatenate([zq, qb], axis=1)], axis=0)
            sc2.append(_dot(kb, bd))

        yield
        bds, scl, m_row, m_new, a_p, vwte = [], [], [], [], [], []
        for p in range(ml_pairs):
            h0, h1 = 2 * p, 2 * p + 1
            b_row = pair_row(b_t[ML_HEADS + h0:ML_HEADS + h0 + 1, :], b_t[ML_HEADS + h1:ML_HEADS + h1 + 1, :])
            i_row = pair_row(pre_t[h0:h0 + 1, :], pre_t[h1:h1 + 1, :])
            u_col = jnp.concatenate(
                [jnp.broadcast_to(u_c[:, ML_HEADS + h0:ML_HEADS + h0 + 1], (BLK, BLK)),
                 jnp.broadcast_to(u_c[:, ML_HEADS + h1:ML_HEADS + h1 + 1], (BLK, BLK))], axis=1)
            m_prev = msc[p:p + 1, :]
            log_d = jnp.where(causal2, b_row + u_col, -jnp.inf)
            inter = b_row + m_prev
            mr = jnp.maximum(inter, jnp.max(log_d, axis=0, keepdims=True))
            m_row.append(mr)
            scl.append(jnp.exp(inter - mr))
            s2 = (kq[p] * jnp.exp(log_d - (mr - LOG_ML_SCALE))).astype(BF16)
            bds.append(block_diag(s2[:, :BLK], s2[:, BLK:]))
            b_last = pair_row(b_row[:, BLK - 1:BLK], b_row[:, 2 * BLK - 1:2 * BLK])
            w_row = b_last - b_row + i_row
            w_max = pair_row(jnp.max(w_row[:, :BLK], axis=1, keepdims=True),
                             jnp.max(w_row[:, BLK:], axis=1, keepdims=True))
            mn = jnp.maximum(b_last + m_prev, w_max)
            m_new.append(mn)
            a_p.append(jnp.exp(b_last + m_prev - mn))
            wexp = jnp.exp(w_row - mn) * ML_SCALE
            vwte.append(vte[p] * wexp.astype(BF16))

        pw, den_p = [], []
        for p in range(pairs):
            sc = jnp.where(prev2, sc2[p][:BLK], sc2[p][BLK:]) + tbl[sel, p]
            snk = snk_rows[p]
            mx = jnp.maximum(jnp.max(sc, axis=0, keepdims=True), snk)
            pe = jnp.exp(sc - mx)
            den_p.append(jnp.sum(pe, axis=0, keepdims=True) + jnp.exp(snk - mx))
            pw.append(jnp.concatenate([jnp.where(prev2, pe, 0.0), jnp.where(prev2, 0.0, pe)],
                                      axis=0).astype(BF16))

        c_old = [ct[p] for p in range(ml_pairs)]
        cq = [_dot(c_old[p].astype(BF16), bdq[p]) for p in range(ml_pairs)]
        vs = [_dot(vte[p], bds[p]) for p in range(ml_pairs)]
        upd = [_dot(vwte[p], block_diag(k2[p][:, :BLK], k2[p][:, BLK:])) for p in range(ml_pairs)]
        o2 = [_dot(vtb, pw[p]) for p in range(pairs)]

        for p in range(ml_pairs):
            nd = scl[p] * cq[p] + vs[p]
            num = nd[:ML_DV]
            den = nd[ML_DV:ML_DV + 1]
            hh = num / jnp.maximum(jnp.abs(den), jnp.exp(-m_row[p]))
            hn = hh * lax.rsqrt(jnp.mean(hh * hh, axis=0, keepdims=True) + NORM_EPS)
            for half in range(2):
                h = 2 * p + half
                og = jax.nn.sigmoid(pf[rows, B_MO + h * ML_DV:B_MO + (h + 1) * ML_DV])
                hn_h = hn[:, half * BLK:(half + 1) * BLK].T
                mix_ref[0, rows, hsl[h]] = (hn_h * mlg[:, hsl[h]] * og).astype(mix_ref.dtype)
            ct[p] = a_p[p] * c_old[p] + upd[p]
            msc[p:p + 1, :] = m_new[p]

        o_parts = [None] * SWA_HEADS
        for p in range(pairs):
            on = o2[p] / den_p[p]
            o_parts[p] = on[0:SWA_HD, 0:BLK]
            o_parts[p + pairs] = on[SWA_HD:2 * SWA_HD, BLK:2 * BLK]
        o_t = jnp.concatenate(o_parts, axis=0)
        mix_ref[0, rows, ML_WIDTH:ML_WIDTH + SWA_WIDTH] = o_t.T.astype(mix_ref.dtype)
        yield

    blocks = [block(j) for j in range(nblk)]
    for g in blocks:
        next(g)
    for g in blocks:
        next(g)

    kband[0:BLK, :] = kband[tm:tm + BLK, :]
    sv_t[0] = sv_t[nblk]

    @pl.when(t == nt - 1)
    def _write_state():
        for h in range(ML_HEADS):
            p, lanes = h // 2, slice((h % 2) * ML_DK, (h % 2 + 1) * ML_DK)
            c_out[0, h] = ct[p, 0:ML_DV, lanes].T
            n_out[0, h:h + 1, :] = ct[p, ML_DV:ML_DV + 1, lanes]
        m_out[0] = msc[...]
        ko_ref[0] = pf[tm - BLK:tm, B_SK:B_SK + BLK]
        vo_ref[0] = vlast[...].T


def _prompt_mixer(x, w_in_l, gbc, mlg, sinks, rb, wo32, wfi32, wfo32):
    bsz, seq, _ = x.shape
    tm = TM_PROMPT
    nt = seq // tm
    nblk = tm // BLK
    steps = bsz * nt
    const2 = lambda b, t: (0, 0)
    smem = pl.BlockSpec(memory_space=pltpu.SMEM)

    def slab_spec(wmat):
        return pl.BlockSpec((wmat.shape[0] // steps, wmat.shape[1]), lambda b, t: (b * nt + t, 0))

    return pl.pallas_call(
        _prompt_mixer_kernel,
        grid=(bsz, nt),
        in_specs=[
            pl.BlockSpec((1, tm, D_MODEL), lambda b, t: (b, t, 0)),
            pl.BlockSpec((D_MODEL, W_IN_PAD), const2, pipeline_mode=pl.Buffered(1)),
            pl.BlockSpec((T_GROWS, BLK), const2),
            pl.BlockSpec((1, ML_WIDTH), const2),
            smem, smem,
            slab_spec(wo32), slab_spec(wfi32), slab_spec(wfo32),
        ],
        out_specs=[
            pl.BlockSpec((1, tm, D_MODEL), lambda b, t: (b, t, 0)),
            pl.BlockSpec((1, ML_HEADS, ML_DK, ML_DV), lambda b, t: (b, 0, 0, 0)),
            pl.BlockSpec((1, ML_HEADS, ML_DK), lambda b, t: (b, 0, 0)),
            pl.BlockSpec((1, M_ROWS, 2 * BLK), lambda b, t: (b, 0, 0)),
            pl.BlockSpec((1, BLK, BLK), lambda b, t: (b, 0, 0)),
            pl.BlockSpec((1, BLK, BLK), lambda b, t: (b, 0, 0)),
            pl.BlockSpec((T_COLS, D_MODEL), const2),
            pl.BlockSpec((D_MODEL, N_B + B_SV), const2),
            slab_spec(wo32), slab_spec(wfi32), slab_spec(wfo32),
        ],
        out_shape=[
            jax.ShapeDtypeStruct((bsz, seq, D_MODEL), BF16),
            jax.ShapeDtypeStruct((bsz, ML_HEADS, ML_DK, ML_DV), F32),
            jax.ShapeDtypeStruct((bsz, ML_HEADS, ML_DK), F32),
            jax.ShapeDtypeStruct((bsz, M_ROWS, 2 * BLK), F32),
            jax.ShapeDtypeStruct((bsz, BLK, BLK), F32),
            jax.ShapeDtypeStruct((bsz, BLK, BLK), F32),
            jax.ShapeDtypeStruct((T_COLS, D_MODEL), BF16),
            jax.ShapeDtypeStruct((D_MODEL, N_B + B_SV), BF16),
            jax.ShapeDtypeStruct(wo32.shape, BF16),
            jax.ShapeDtypeStruct(wfi32.shape, BF16),
            jax.ShapeDtypeStruct(wfo32.shape, BF16),
        ],
        scratch_shapes=[
            pltpu.VMEM((tm, ML_HEADS * ML_DK), BF16),
            pltpu.VMEM((tm, B_SV), F32),
            pltpu.VMEM((BLK, BLK), F32),
            pltpu.VMEM((tm + BLK, BLK), BF16),
            pltpu.VMEM((nblk, ML_HEADS * ML_DK, BLK), BF16),
            pltpu.VMEM((nblk, ML_WIDTH, BLK), BF16),
            pltpu.VMEM((nblk, SWA_WIDTH, BLK), BF16),
            pltpu.VMEM((nblk + 1, SWA_KV_HEADS * SWA_HD, BLK), BF16),
            pltpu.VMEM((nblk, T_GROWS, BLK), F32),
            pltpu.VMEM((ML_HEADS // 2, STATE_ROWS, 2 * ML_DK), F32),
            pltpu.VMEM((M_ROWS, 2 * BLK), F32),
            pltpu.VMEM((2, SWA_HEADS // 2, BLK, 2 * BLK), F32),
        ],
        compiler_params=pltpu.CompilerParams(
            dimension_semantics=("arbitrary", "arbitrary"),
            vmem_limit_bytes=VMEM_LIMIT),
        name="prompt_mixer",
    )(x, w_in_l, gbc, mlg, sinks, rb, wo32, wfi32, wfo32)


def _roll_rows(x, shift):
    return pltpu.roll(x, shift % x.shape[0], 0)


def _sample_mixer_kernel(x_ref, wt_ref, wn_ref, gb_ref, mlg_ref, sinks_ref, rb_ref, mrep_ref,
                         c_ref, n_ref, kc_ref, vc_ref,
                         mix_ref, c_out, n_out, mo_ref, ko_ref, vo_ref,
                         pa, pf, tblc, tbln):
    step = pl.program_id(0)
    tdec = 4
    rows_step = mix_ref.shape[0]
    nslab = rows_step // SLAB
    nseq = SLAB // tdec
    qrows = SWA_HEADS * SLAB

    @pl.when(step == 0)
    def _project_all_rows():
        xb = x_ref[...].astype(BF16)
        pa[:, A_Q:A_K] = _dot_nt(xb, wt_ref[T_Q:T_Q + ML_HEADS * ML_DK, :])
        pa[:, A_K:A_V] = _dot(xb, wn_ref[:, N_K:N_K + ML_HEADS * ML_DK])
        pa[:, A_V:A_COLS] = _dot_nt(xb, wt_ref[T_V:T_V + ML_WIDTH + SWA_WIDTH, :])
        pf[:, B_MO:B_SV] = _dot(xb, wn_ref[:, N_B:N_B + B_SV])
        pf[:, B_SV:B_COLS] = _dot_nt(xb, wt_ref[T_SV:T_COLS, :])

    @pl.when(step == 0)
    def _build_bias_tables():
        r = lax.broadcasted_iota(jnp.int32, (qrows, BLK), 0)
        c = lax.broadcasted_iota(jnp.int32, (qrows, BLK), 1)
        hq = r // SLAB
        bl = (r % SLAB) // tdec
        tq = r % tdec
        d_cache = WINDOW + tq - c
        d_new = tq - (c % tdec)
        key_cache = _t5_bucket(d_cache) * SWA_HEADS + hq
        key_new = _t5_bucket(d_new) * SWA_HEADS + hq
        acc_c = jnp.zeros((qrows, BLK), F32)
        acc_n = jnp.zeros((qrows, BLK), F32)
        for k in range(REL_BUCKETS):
            for h in range(SWA_HEADS):
                val = rb_ref[k, h]
                acc_c = jnp.where(key_cache == k * SWA_HEADS + h, val, acc_c)
                acc_n = jnp.where(key_new == k * SWA_HEADS + h, val, acc_n)
        tblc[...] = jnp.where(c > tq, acc_c, NEG_INF)
        tbln[...] = jnp.where((c < SLAB) & ((c // tdec) == bl) & (d_new >= 0), acc_n, NEG_INF)

    gbias = gb_ref[...]
    mlg = mlg_ref[...]
    r16 = lax.broadcasted_iota(jnp.int32, (SLAB, BLK), 0)
    rr = r16 % tdec
    bl16 = r16 // tdec
    rq = lax.broadcasted_iota(jnp.int32, (qrows, BLK), 0)
    blq = (rq % SLAB) // tdec
    hq_col = rq[:, 0:1] // SLAB
    snk = jnp.zeros((qrows, 1), F32)
    for h in range(SWA_HEADS):
        snk = jnp.where(hq_col == h, sinks_ref[h], snk)
    first_group = rq[:, 0:SWA_HD] < SWA_GROUP * SLAB
    zq = jnp.zeros((SLAB, SWA_HD), F32)
    zeros_pad = jnp.zeros((BLK - SLAB, BLK), BF16)

    def seg_last(x):
        return jnp.where(rr == 3, x,
                         jnp.where(rr == 2, _roll_rows(x, -1),
                                   jnp.where(rr == 1, _roll_rows(x, -2), _roll_rows(x, -3))))

    def seg_max(x):
        m1 = jnp.maximum(x, jnp.where(rr % 2 == 0, _roll_rows(x, -1), _roll_rows(x, 1)))
        return jnp.maximum(m1, jnp.where(rr < 2, _roll_rows(m1, -2), _roll_rows(m1, 2)))

    def slab(si):
        rows = pl.ds(si * SLAB, SLAB)
        prow = pl.ds(pl.multiple_of(step * rows_step + si * SLAB, SLAB), SLAB)
        seq0 = si * nseq

        pre = _softcap(pf[prow, B_G:B_G + BLK] + gbias)
        logf = _log_sigmoid(pre)
        y = logf + jnp.where(rr >= 1, _roll_rows(logf, 1), 0.0)
        bcum = y + jnp.where(rr >= 2, _roll_rows(y, 2), 0.0)
        ig = pltpu.roll(pre, ML_HEADS, 1)
        m_prev = mrep_ref[rows, :]
        b_minus_i = bcum - ig
        log_d = [jnp.where(rr >= dl, bcum - _roll_rows(b_minus_i, dl), -jnp.inf) for dl in range(tdec)]
        rowmax = functools.reduce(jnp.maximum, log_d)
        inter = bcum + m_prev
        m_row = jnp.maximum(inter, rowmax)
        scl = jnp.exp(inter - m_row)
        dm = [jnp.exp(ld - m_row) for ld in log_d]
        enm = jnp.exp(-m_row)
        b_last = seg_last(bcum)
        w = b_last - bcum + ig
        m_new = jnp.maximum(b_last + m_prev, seg_max(w))
        a = jnp.exp(b_last + m_prev - m_new)
        kws = jnp.exp(w - m_new) * ML_SCALE
        mo_ref[rows, :] = m_new

        q = [pa[prow, A_Q + h * ML_DK:A_Q + (h + 1) * ML_DK] for h in range(ML_HEADS)]
        k = [pa[prow, A_K + h * ML_DK:A_K + (h + 1) * ML_DK] for h in range(ML_HEADS)]
        v = [pa[prow, A_V + h * ML_DV:A_V + (h + 1) * ML_DV] for h in range(ML_HEADS)]
        c_b = [[c_ref[seq0 + bl, h] for bl in range(nseq)] for h in range(ML_HEADS)]
        qc, upd, kw = [], [], []
        for h in range(ML_HEADS):
            qb = q[h].astype(BF16)
            res = [_dot(qb, jnp.concatenate([c_b[h][2 * i].astype(BF16), c_b[h][2 * i + 1].astype(BF16)], axis=1))
                   for i in range(nseq // 2)]
            parts = [res[bl // 2][:, (bl % 2) * ML_DV:(bl % 2 + 1) * ML_DV] for bl in range(nseq)]
            qc_h = parts[nseq - 1]
            for bl in range(nseq - 2, -1, -1):
                qc_h = jnp.where(bl16 == bl, parts[bl], qc_h)
            qc.append(qc_h)

        knew = pf[prow, B_SK:B_SK + BLK]
        vnew = pf[prow, B_SV:B_SV + BLK]
        qparts = []
        for hq in range(SWA_HEADS):
            qh = pa[prow, A_SQ + hq * SWA_HD:A_SQ + (hq + 1) * SWA_HD]
            qparts.append(jnp.concatenate([qh, zq] if hq < SWA_GROUP else [zq, qh], axis=1))
        qs = jnp.concatenate(qparts, axis=0).astype(BF16)
        s_c = _dot_nt(qs, kc_ref[seq0 + nseq - 1].astype(BF16))
        for bl in range(nseq - 2, -1, -1):
            s_c = jnp.where(blq == bl, _dot_nt(qs, kc_ref[seq0 + bl].astype(BF16)), s_c)
        kn = jnp.concatenate([knew.astype(BF16), zeros_pad], axis=0)
        s_n = _dot_nt(qs, kn)

        yield
        for h in range(ML_HEADS):
            lane = ML_HEADS + h
            kw_h = k[h] * kws[:, lane:lane + 1]
            kw.append(kw_h)
            vm = jnp.concatenate([jnp.where(bl16 == bl, v[h], 0.0) for bl in range(nseq)], axis=1)
            upd.append(_dot_tn(kw_h.astype(BF16), vm.astype(BF16)))

        yield
        for h in range(ML_HEADS):
            hs = slice(h * ML_DK, (h + 1) * ML_DK)
            lane = ML_HEADS + h
            bm = jnp.zeros((SLAB, ML_DV), F32)
            sum_s = jnp.zeros((SLAB, 1), F32)
            for dl in range(tdec):
                s_dl = jnp.sum(q[h] * _roll_rows(k[h], dl), axis=-1, keepdims=True)
                ws = s_dl * ML_SCALE * dm[dl][:, lane:lane + 1]
                bm = bm + ws * _roll_rows(v[h], dl)
                sum_s = sum_s + ws
            nsel = jnp.zeros((SLAB, ML_DK), F32)
            for bl in range(nseq):
                msk = bl16 == bl
                n_b = n_ref[seq0 + bl][h:h + 1, :]
                nsel = jnp.where(msk, n_b, nsel)
                a_b = a[bl * tdec:bl * tdec + 1, lane:lane + 1]
                c_out[seq0 + bl, h] = a_b * c_b[h][bl] + upd[h][:, bl * ML_DV:(bl + 1) * ML_DV]
                n_out[seq0 + bl, h:h + 1, :] = a_b * n_b + jnp.sum(jnp.where(msk, kw[h], 0.0), axis=0,
                                                                   keepdims=True)
            sclh = scl[:, lane:lane + 1]
            num = sclh * qc[h] + bm
            den = sclh * jnp.sum(q[h] * nsel, axis=-1, keepdims=True) + sum_s
            hh = num / jnp.maximum(jnp.abs(den), enm[:, lane:lane + 1])
            hn = hh * lax.rsqrt(jnp.mean(hh * hh, axis=-1, keepdims=True) + NORM_EPS)
            og = jax.nn.sigmoid(pf[prow, B_MO + h * ML_DV:B_MO + (h + 1) * ML_DV])
            mix_ref[rows, hs] = (hn * mlg[:, hs] * og).astype(mix_ref.dtype)

        s_c = s_c * SWA_SCALE + tblc[...]
        s_n = s_n * SWA_SCALE + tbln[...]
        mx = jnp.maximum(jnp.maximum(jnp.max(s_c, axis=-1, keepdims=True),
                                     jnp.max(s_n, axis=-1, keepdims=True)), snk)
        p_c = jnp.exp(s_c - mx)
        p_n = jnp.exp(s_n - mx)
        den = (jnp.sum(p_c, axis=-1, keepdims=True) + jnp.sum(p_n, axis=-1, keepdims=True)
               + jnp.exp(snk - mx))

        yield
        vn = jnp.concatenate([vnew.astype(BF16), zeros_pad], axis=0)
        o = _dot(p_n.astype(BF16), vn)
        for bl in range(nseq):
            o = o + _dot(jnp.where(blq == bl, p_c, 0.0).astype(BF16), vc_ref[seq0 + bl].astype(BF16))
        o = o / den
        osel = jnp.where(first_group, o[:, 0:SWA_HD], o[:, SWA_HD:2 * SWA_HD])
        og = jnp.concatenate([osel[hq * SLAB:(hq + 1) * SLAB, :] for hq in range(SWA_HEADS)], axis=1)
        mix_ref[rows, ML_WIDTH:ML_WIDTH + SWA_WIDTH] = og.astype(mix_ref.dtype)

        for bl in range(nseq):
            ko_ref[seq0 + bl, 0:WINDOW - tdec, :] = kc_ref[seq0 + bl, tdec:WINDOW, :]
            vo_ref[seq0 + bl, 0:WINDOW - tdec, :] = vc_ref[seq0 + bl, tdec:WINDOW, :]
            ko_ref[seq0 + bl, WINDOW - tdec:WINDOW, :] = knew[bl * tdec:(bl + 1) * tdec, :]
            vo_ref[seq0 + bl, WINDOW - tdec:WINDOW, :] = vnew[bl * tdec:(bl + 1) * tdec, :]
        yield

    slabs = [slab(si) for si in range(nslab)]
    for _ in range(4):
        for s in slabs:
            next(s)


def _sample_mixer(x2d, wt, wn, gb, mlg, sinks, rb, mrep, c0, n0, kc, vc):
    rows = x2d.shape[0]
    tdec = 4
    nseq = rows // tdec
    gbs = SAMPLE_GB
    steps = nseq // gbs
    rb_rows = gbs * tdec
    const2 = lambda i: (0, 0)
    smem = pl.BlockSpec(memory_space=pltpu.SMEM)
    return pl.pallas_call(
        _sample_mixer_kernel,
        grid=(steps,),
        in_specs=[
            pl.BlockSpec((rows, D_MODEL), const2, pipeline_mode=pl.Buffered(1)),
            pl.BlockSpec((T_COLS, D_MODEL), const2, pipeline_mode=pl.Buffered(1)),
            pl.BlockSpec((D_MODEL, N_B + B_SV), const2, pipeline_mode=pl.Buffered(1)),
            pl.BlockSpec((1, BLK), const2),
            pl.BlockSpec((1, ML_WIDTH), const2),
            smem, smem,
            pl.BlockSpec((rb_rows, BLK), lambda i: (i, 0)),
            pl.BlockSpec((gbs, ML_HEADS, ML_DK, ML_DV), lambda i: (i, 0, 0, 0)),
            pl.BlockSpec((gbs, ML_HEADS, ML_DK), lambda i: (i, 0, 0)),
            pl.BlockSpec((gbs, WINDOW, BLK), lambda i: (i, 0, 0)),
            pl.BlockSpec((gbs, WINDOW, BLK), lambda i: (i, 0, 0)),
        ],
        out_specs=[
            pl.BlockSpec((rb_rows, D_MODEL), lambda i: (i, 0)),
            pl.BlockSpec((gbs, ML_HEADS, ML_DK, ML_DV), lambda i: (i, 0, 0, 0)),
            pl.BlockSpec((gbs, ML_HEADS, ML_DK), lambda i: (i, 0, 0)),
            pl.BlockSpec((rb_rows, BLK), lambda i: (i, 0)),
            pl.BlockSpec((gbs, WINDOW, BLK), lambda i: (i, 0, 0)),
            pl.BlockSpec((gbs, WINDOW, BLK), lambda i: (i, 0, 0)),
        ],
        out_shape=[
            jax.ShapeDtypeStruct((rows, D_MODEL), BF16),
            jax.ShapeDtypeStruct((nseq, ML_HEADS, ML_DK, ML_DV), F32),
            jax.ShapeDtypeStruct((nseq, ML_HEADS, ML_DK), F32),
            jax.ShapeDtypeStruct((rows, BLK), F32),
            jax.ShapeDtypeStruct((nseq, WINDOW, BLK), F32),
            jax.ShapeDtypeStruct((nseq, WINDOW, BLK), F32),
        ],
        scratch_shapes=[
            pltpu.VMEM((rows, A_COLS), F32),
            pltpu.VMEM((rows, B_COLS), F32),
            pltpu.VMEM((SWA_HEADS * SLAB, BLK), F32),
            pltpu.VMEM((SWA_HEADS * SLAB, BLK), F32),
        ],
        compiler_params=pltpu.CompilerParams(
            dimension_semantics=("arbitrary",),
            vmem_limit_bytes=VMEM_LIMIT),
        name="sample_mixer",
    )(x2d, wt, wn, gb, mlg, sinks, rb, mrep, c0, n0, kc, vc)


def _dense_kernel(x_ref, mix_ref, xs_ref, mixs_ref, wo_ref, g1_ref, b1_ref, wfi_ref, wfo_ref, g2_ref, b2_ref,
                  y_ref, ys_ref):
    bounds = [b for b in DENSE_SPLITS if b <= x_ref.shape[0]]
    nsub = len(bounds) - 1
    subs = [slice(bounds[i], bounds[i + 1]) for i in range(nsub)]
    chunks = list(zip(FF_SPLITS[:-1], FF_SPLITS[1:]))
    g1, b1, g2, b2 = g1_ref[...], b1_ref[...], g2_ref[...], b2_ref[...]

    xin = [x_ref[s, :] for s in subs]
    mixin = [mix_ref[s, :] for s in subs]
    xin[-1] = jnp.concatenate([xin[-1], xs_ref[...]], axis=0)
    mixin[-1] = jnp.concatenate([mixin[-1], mixs_ref[...]], axis=0)

    proj = [_dot(mixin[i], wo_ref[...]) for i in range(nsub)]
    h1 = [_layer_norm(ALPHA * xin[i] + proj[i], g1, b1) for i in range(nsub)]
    h1b = [h.astype(BF16) for h in h1]
    acc = [None] * nsub
    for c0, c1 in chunks:
        gate, up = [], []
        for i in range(nsub):
            gate.append(_dot(h1b[i], wfi_ref[:, c0:c1]))
            up.append(_dot(h1b[i], wfi_ref[:, D_FF + c0:D_FF + c1]))
        for i in range(nsub):
            act = (gate[i] * jax.nn.sigmoid(gate[i]) * up[i]).astype(BF16)
            part = _dot(act, wfo_ref[c0:c1, :])
            acc[i] = part if acc[i] is None else acc[i] + part
    for i, s in enumerate(subs):
        out = _layer_norm(ALPHA * h1[i] + acc[i], g2, b2)
        nrows = s.stop - s.start
        y_ref[s, :] = out[:nrows]
        if i == nsub - 1:
            ys_ref[...] = out[nrows:]


def _dense(x2d, mix2d, xs2d, mixs2d, wo, g1, b1, wfi, wfo, g2, b2):
    rows = x2d.shape[0]
    tm = TM_DENSE
    steps = rows // tm
    srows = xs2d.shape[0] // steps
    const2 = lambda i: (0, 0)
    step2 = lambda i: (i, 0)

    def wspec(shape):
        return pl.BlockSpec(shape, const2, pipeline_mode=pl.Buffered(1))

    return pl.pallas_call(
        _dense_kernel,
        grid=(steps,),
        in_specs=[
            pl.BlockSpec((tm, D_MODEL), step2),
            pl.BlockSpec((tm, D_MODEL), step2),
            pl.BlockSpec((srows, D_MODEL), step2),
            pl.BlockSpec((srows, D_MODEL), step2),
            wspec((D_MODEL, D_MODEL)),
            wspec((1, D_MODEL)), wspec((1, D_MODEL)),
            wspec((D_MODEL, 2 * D_FF)),
            wspec((D_FF, D_MODEL)),
            wspec((1, D_MODEL)), wspec((1, D_MODEL)),
        ],
        out_specs=[pl.BlockSpec((tm, D_MODEL), step2), pl.BlockSpec((srows, D_MODEL), step2)],
        out_shape=[jax.ShapeDtypeStruct((rows, D_MODEL), F32),
                   jax.ShapeDtypeStruct((xs2d.shape[0], D_MODEL), F32)],
        compiler_params=pltpu.CompilerParams(
            dimension_semantics=("arbitrary",),
            vmem_limit_bytes=VMEM_LIMIT),
        name="dense",
    )(x2d, mix2d, xs2d, mixs2d, wo, g1, b1, wfi, wfo, g2, b2)


def kernel(x_prompt, x_sample, state_mlstm_C, state_mlstm_n, state_mlstm_m, cache_swa_k, cache_swa_v,
           w_in, b_igate, b_fgate, ml_norm_g, swa_sinks, rel_bias, w_out, ln1_g, ln1_b,
           w_ffn_in, w_ffn_out, ln2_g, ln2_b):
    bp, seq, _ = x_prompt.shape
    bs, tdec, _ = x_sample.shape
    l = 0

    w16 = jnp.pad(w_in[l], ((0, 0), (0, W_IN_PAD - w_in.shape[-1]))).astype(BF16)
    gvec = jnp.concatenate([b_igate[l], b_fgate[l]]).astype(F32)
    gb = jnp.pad(gvec, (0, BLK - 2 * ML_HEADS))[None, :]
    gbc = jnp.broadcast_to(jnp.pad(gvec, (0, T_GROWS - 2 * ML_HEADS))[:, None], (T_GROWS, BLK))
    mlg = ml_norm_g[l][None, :].astype(F32)
    sinks = swa_sinks[l].astype(F32)
    rb = rel_bias.astype(F32)
    g1, b1 = ln1_g[l][None, :], ln1_b[l][None, :]
    g2, b2 = ln2_g[l][None, :], ln2_b[l][None, :]

    mix_p, c_p, n_p, m_p, k_p, v_p, wt, wn, wo, wfi, wfo = _prompt_mixer(
        x_prompt, w16, gbc, mlg, sinks, rb, w_out[l], w_ffn_in[l], w_ffn_out[l])
    p_m =m_p[:, :ML_HEADS // 2, ::BLK].reshape(bp, ML_HEADS)[None]
    p_k = k_p.reshape(1, bp, WINDOW, SWA_KV_HEADS, SWA_HD)
    p_v = v_p.reshape(1, bp, WINDOW, SWA_KV_HEADS, SWA_HD)

    xs = x_sample.reshape(bs * tdec, D_MODEL)
    m0 = state_mlstm_m[l].astype(F32)
    mrep = jnp.pad(jnp.repeat(m0, tdec, axis=0), ((0, 0), (ML_HEADS, BLK - 2 * ML_HEADS)))
    wlen = cache_swa_k.shape[2]
    kc = cache_swa_k[l].reshape(bs, wlen, SWA_KV_HEADS * SWA_HD)
    vc = cache_swa_v[l].reshape(bs, wlen, SWA_KV_HEADS * SWA_HD)
    mix_s, c_s, n_s, mo_s, k_s, v_s = _sample_mixer(
        xs, wt, wn, gb, mlg, sinks, rb, mrep,
        state_mlstm_C[l].astype(F32), state_mlstm_n[l].astype(F32), kc, vc)
    s_m =mo_s.reshape(bs, tdec, BLK)[:, 0, ML_HEADS:2 * ML_HEADS][None]
    s_k = k_s.reshape(1, bs, wlen, SWA_KV_HEADS, SWA_HD)
    s_v = v_s.reshape(1, bs, wlen, SWA_KV_HEADS, SWA_HD)

    y_p, y_s = _dense(x_prompt.reshape(bp * seq, D_MODEL), mix_p.reshape(bp * seq, D_MODEL), xs, mix_s,
                      wo, g1, b1, wfi, wfo, g2, b2)
    y_p = y_p.reshape(bp, seq, D_MODEL)
    y_s = y_s.reshape(bs, tdec, D_MODEL)

    return (y_p, y_s, c_p[None], n_p[None], p_m, p_k, p_v,
            c_s[None], n_s[None], s_m, s_k, s_v)
```

```python
import functools
import math

import jax
import jax.numpy as jnp
from jax import lax
from jax.experimental import pallas as pl
from jax.experimental.pallas import tpu as pltpu

F32 = jnp.float32
BF16 = jnp.bfloat16

D_MODEL = 1024
ML_HEADS = 4
ML_DK = 128
ML_DV = 128
ML_WIDTH = ML_HEADS * ML_DV
GATE_SOFTCAP = 15.0
SWA_HEADS = 8
SWA_KV_HEADS = 2
SWA_GROUP = SWA_HEADS // SWA_KV_HEADS
SWA_HD = 64
SWA_WIDTH = SWA_HEADS * SWA_HD
WINDOW = 128
REL_BUCKETS = 32
REL_MAX_DIST = 128
D_FF = 2816
DEPTH = 1
ALPHA = (2.0 * DEPTH) ** 0.25
LN_EPS = 1e-5
NORM_EPS = 1e-6
NEG_INF = -1e30
IN_SIZES = (512, 512, 512, 512, 4, 4, 512, 128, 128)
ML_SCALE = ML_DK ** -0.5
LOG_ML_SCALE = math.log(ML_SCALE)
SWA_SCALE = SWA_HD ** -0.5

BLK = 128
SUBLANES_F32 = 8
SUBLANES_BF16 = 16

A_Q, A_K, A_V, A_SQ = 0, 512, 1024, 1536
A_COLS = 2048
B_MO, B_SK, B_SV, B_G = 0, 512, 640, 768
B_COLS = 896

N_K = 0
N_B = 512
T_Q, T_V, T_SQ, T_SV, T_G = 0, 512, 1024, 1536, 1664
T_GROWS = SUBLANES_BF16
T_COLS = T_G + BLK
W_IN_PAD = -(-sum(IN_SIZES) // BLK) * BLK
STATE_ROWS = ML_DV + SUBLANES_BF16
M_ROWS = SUBLANES_F32

TM_PROMPT = 1024
TM_DENSE = 1024
DENSE_SPLITS = (0, 384, 768, 1024)
FF_SPLITS = (0, 1536, D_FF)
SAMPLE_GB = 16
SLAB = 16
VMEM_LIMIT = 56 * 1024 * 1024


def _softcap(a):
    return GATE_SOFTCAP * jnp.tanh(a / GATE_SOFTCAP)


def _log_sigmoid(x):
    return jnp.minimum(x, 0.0) - jnp.log1p(jnp.exp(-jnp.abs(x)))


def _layer_norm(z, g, b):
    mu = jnp.mean(z, axis=-1, keepdims=True)
    zc = z - mu
    var = jnp.mean(zc * zc, axis=-1, keepdims=True)
    return zc * lax.rsqrt(var + LN_EPS) * g + b


def _dot(a, b):
    return jnp.dot(a, b, preferred_element_type=F32)


def _dot_nt(a, b):
    return lax.dot_general(a, b, (((1,), (1,)), ((), ())), preferred_element_type=F32)


def _dot_tn(a, b):
    return lax.dot_general(a, b, (((0,), (0,)), ((), ())), preferred_element_type=F32)


def _split3(x):
    hi = x.astype(BF16)
    r1 = x - hi.astype(F32)
    mid = r1.astype(BF16)
    lo = (r1 - mid.astype(F32)).astype(BF16)
    return hi, mid, lo


def _t5_bucket(d):
    n = jnp.maximum(d, 0)
    max_exact = REL_BUCKETS // 2
    nlog = REL_BUCKETS - max_exact
    large = jnp.full(n.shape, max_exact, jnp.int32)
    for k in range(1, nlog):
        thr = math.ceil(max_exact * (REL_MAX_DIST / max_exact) ** (k / nlog))
        large = large + jnp.where(n >= thr, 1, 0)
    return jnp.where(n < max_exact, n, large)


def _prompt_mixer_kernel(x_ref, w_ref, gbc_ref, mlg_ref, sinks_ref, rb_ref, wo32, wfi32, wfo32,
                         mix_ref, c_out, n_out, m_out, ko_ref, vo_ref,
                         wt, wn, wo16, wfi16, wfo16,
                         kn, pf, vlast, kband, q_t, v_t, sq_t, sv_t, g_t, ct, msc, tbl):
    wo16[...] = wo32[...].astype(BF16)
    wfi16[...] = wfi32[...].astype(BF16)
    wfo16[...] = wfo32[...].astype(BF16)

    b = pl.program_id(0)
    t = pl.program_id(1)
    nt = pl.num_programs(1)
    tm = x_ref.shape[1]
    nblk = tm // BLK
    pairs = SWA_HEADS // 2

    @pl.when((b == 0) & (t == 0))
    def _prepare_weights():
        offs = [0]
        for n in IN_SIZES:
            offs.append(offs[-1] + n)
        o_mq, o_mk, o_mv, o_mo, o_mi, _, o_sq, o_sk, o_sv = offs[:-1]

        def col_blocks(dst_col, src_col, n):
            for i in range(n // BLK):
                wn[:, dst_col + i * BLK:dst_col + (i + 1) * BLK] = (
                    w_ref[:, src_col + i * BLK:src_col + (i + 1) * BLK])

        def row_blocks(dst_row, src_col, n):
            for i in range(n // BLK):
                wt[dst_row + i * BLK:dst_row + (i + 1) * BLK, :] = (
                    w_ref[:, src_col + i * BLK:src_col + (i + 1) * BLK].T)

        col_blocks(N_K, o_mk, ML_HEADS * ML_DK)
        col_blocks(N_B + B_MO, o_mo, ML_WIDTH)
        col_blocks(N_B + B_SK, o_sk, SWA_KV_HEADS * SWA_HD)
        row_blocks(T_Q, o_mq, ML_HEADS * ML_DK)
        row_blocks(T_V, o_mv, ML_WIDTH)
        row_blocks(T_SQ, o_sq, SWA_WIDTH)
        row_blocks(T_SV, o_sv, SWA_KV_HEADS * SWA_HD)
        gates_t = w_ref[:, o_mi:o_mi + BLK].T
        grow = lax.broadcasted_iota(jnp.int32, gates_t.shape, 0)
        wt[T_G:T_G + BLK, :] = jnp.where(grow < 2 * ML_HEADS, gates_t, jnp.zeros_like(gates_t))

    @pl.when((b == 0) & (t == 0))
    def _build_bias_tables():
        r = lax.broadcasted_iota(jnp.int32, (BLK, 2 * BLK), 0)
        ln = lax.broadcasted_iota(jnp.int32, (BLK, 2 * BLK), 1)
        second = ln >= BLK
        qi = jnp.where(second, ln - BLK, ln)
        prev = r > qi
        d = jnp.where(prev, WINDOW + qi - r, qi - r)
        bucket = _t5_bucket(d)
        for p in range(pairs):
            acc = jnp.zeros((BLK, 2 * BLK), F32)
            for k in range(REL_BUCKETS):
                acc = jnp.where(bucket == k, jnp.where(second, rb_ref[k, p + pairs], rb_ref[k, p]), acc)
            tbl[0, p] = jnp.where(prev, NEG_INF, acc)
            tbl[1, p] = acc

    @pl.when(t == 0)
    def _reset_state():
        ct[...] = jnp.zeros_like(ct)
        msc[...] = jnp.zeros_like(msc)
        kband[0:BLK, :] = jnp.zeros((BLK, kband.shape[1]), kband.dtype)
        sv_t[0] = jnp.zeros(sv_t.shape[1:], sv_t.dtype)

    xb = x_ref[0].astype(BF16)
    res_n = _dot(xb, wn[...])
    kn[...] = res_n[:, N_K:N_K + ML_HEADS * ML_DK].astype(BF16)
    pf[...] = res_n[:, N_B:N_B + B_SV]
    kband[BLK:BLK + tm, :] = pf[:, B_SK:B_SK + BLK].astype(BF16)

    res_t = _dot_nt(wt[0:T_G + T_GROWS, :], xb)
    vlast[...] = res_t[T_SV:T_SV + BLK, tm - BLK:tm]

    def put(dst, r0, nrows, off, scale=None):
        res = res_t[r0:r0 + nrows, :]
        if scale is not None:
            res = res * scale
        for jj in range(nblk):
            dst[jj + off] = res[:, jj * BLK:(jj + 1) * BLK].astype(dst.dtype)

    put(q_t, T_Q, ML_HEADS * ML_DK, 0)
    put(v_t, T_V, ML_WIDTH, 0)
    put(sq_t, T_SQ, SWA_WIDTH, 0, SWA_SCALE)
    put(sv_t, T_SV, SWA_KV_HEADS * SWA_HD, 1)
    put(g_t, T_G, T_GROWS, 0)

    row = lax.broadcasted_iota(jnp.int32, (BLK, BLK), 0)
    col = lax.broadcasted_iota(jnp.int32, (BLK, BLK), 1)
    causal_t = row <= col
    triu = causal_t.astype(BF16)
    causal2 = jnp.concatenate([causal_t, causal_t], axis=1)
    r16 = lax.broadcasted_iota(jnp.int32, (STATE_ROWS - ML_DV, 2 * BLK), 0)
    ones_rows = (r16 == 0).astype(BF16)
    z128 = jnp.zeros((BLK, BLK), BF16)
    ml_pairs = ML_HEADS // 2
    zero_rows = jnp.zeros((BLK - T_GROWS, BLK), F32)
    gbc = gbc_ref[...]
    mlg = mlg_ref[...]
    lane2 = lax.broadcasted_iota(jnp.int32, (1, 2 * BLK), 1)
    snk_rows = [jnp.where(lane2 >= BLK, sinks_ref[p + pairs], sinks_ref[p]) for p in range(pairs)]
    zq = jnp.zeros((SWA_HD, BLK), BF16)
    prev2 = jnp.concatenate([row > col, row > col], axis=1)

    def pair_row(x0, x1):
        return jnp.concatenate([jnp.broadcast_to(x0, (1, BLK)), jnp.broadcast_to(x1, (1, BLK))], axis=1)

    def block_diag(x0, x1):
        return jnp.concatenate([jnp.concatenate([x0, z128], axis=1),
                                jnp.concatenate([z128, x1], axis=1)], axis=0)

    def block(j):
        r0 = j * BLK
        rows = pl.ds(r0, BLK)
        hsl = [slice(h * ML_DK, (h + 1) * ML_DK) for h in range(ML_HEADS)]
        psl = [slice(p * 2 * ML_DK, (p + 1) * 2 * ML_DK) for p in range(ml_pairs)]

        pre_t = _softcap(g_t[j] + gbc)
        hi, mid, lo = _split3(_log_sigmoid(pre_t))
        b_t = _dot(hi, triu) + _dot(mid, triu) + _dot(lo, triu)
        u_t = pltpu.roll(pre_t, ML_HEADS, 0) - b_t
        u_c = jnp.concatenate([u_t, zero_rows], axis=0).T

        k2 = [kn[rows, psl[p]] for p in range(ml_pairs)]
        bdq = [block_diag(q_t[j, hsl[2 * p], :], q_t[j, hsl[2 * p + 1], :]) for p in range(ml_pairs)]
        vte = [jnp.concatenate([jnp.concatenate([v_t[j, hsl[2 * p], :], v_t[j, hsl[2 * p + 1], :]], axis=1),
                                ones_rows], axis=0) for p in range(ml_pairs)]
        kq = [_dot(k2[p], bdq[p]) for p in range(ml_pairs)]

        sel = jnp.where(t == 0, 0, 1) if j == 0 else 1
        kb = kband[pl.ds(r0, 2 * BLK), :]
        vtb = jnp.concatenate([sv_t[j], sv_t[j + 1]], axis=1)
        sc2 = []
        for p in range(pairs):
            qa = sq_t[j, p * SWA_HD:(p + 1) * SWA_HD, :]
            qb = sq_t[j, (p + pairs) * SWA_HD:(p + pairs + 1) * SWA_HD, :]
            bd = jnp.concatenate([jnp.concatenate([qa, zq], axis=1),
                                  jnp.concatenate([zq, qb], axis=1)], axis=0)
            sc2.append(_dot(kb, bd))

        yield
        bds, scl, m_row, m_new, a_p, vwte = [], [], [], [], [], []
        for p in range(ml_pairs):
            h0, h1 = 2 * p, 2 * p + 1
            b_row = pair_row(b_t[ML_HEADS + h0:ML_HEADS + h0 + 1, :], b_t[ML_HEADS + h1:ML_HEADS + h1 + 1, :])
            i_row = pair_row(pre_t[h0:h0 + 1, :], pre_t[h1:h1 + 1, :])
            u_col = jnp.concatenate(
                [jnp.broadcast_to(u_c[:, ML_HEADS + h0:ML_HEADS + h0 + 1], (BLK, BLK)),
                 jnp.broadcast_to(u_c[:, ML_HEADS + h1:ML_HEADS + h1 + 1], (BLK, BLK))], axis=1)
            m_prev = msc[p:p + 1, :]
            log_d = jnp.where(causal2, b_row + u_col, -jnp.inf)
            inter = b_row + m_prev
            mr = jnp.maximum(inter, jnp.max(log_d, axis=0, keepdims=True))
            m_row.append(mr)
            scl.append(jnp.exp(inter - mr))
            s2 = (kq[p] * jnp.exp(log_d - (mr - LOG_ML_SCALE))).astype(BF16)
            bds.append(block_diag(s2[:, :BLK], s2[:, BLK:]))
            b_last = pair_row(b_row[:, BLK - 1:BLK], b_row[:, 2 * BLK - 1:2 * BLK])
            w_row = b_last - b_row + i_row
            w_max = pair_row(jnp.max(w_row[:, :BLK], axis=1, keepdims=True),
                             jnp.max(w_row[:, BLK:], axis=1, keepdims=True))
            mn = jnp.maximum(b_last + m_prev, w_max)
            m_new.append(mn)
            a_p.append(jnp.exp(b_last + m_prev - mn))
            wexp = jnp.exp(w_row - mn) * ML_SCALE
            vwte.append(vte[p] * wexp.astype(BF16))

        pw, den_p = [], []
        for p in range(pairs):
            sc = jnp.where(prev2, sc2[p][:BLK], sc2[p][BLK:]) + tbl[sel, p]
            snk = snk_rows[p]
            mx = jnp.maximum(jnp.max(sc, axis=0, keepdims=True), snk)
            pe = jnp.exp(sc - mx)
            den_p.append(jnp.sum(pe, axis=0, keepdims=True) + jnp.exp(snk - mx))
            pw.append(jnp.concatenate([jnp.where(prev2, pe, 0.0), jnp.where(prev2, 0.0, pe)],
                                      axis=0).astype(BF16))

        c_old = [ct[p] for p in range(ml_pairs)]
        cq = [_dot(c_old[p].astype(BF16), bdq[p]) for p in range(ml_pairs)]
        vs = [_dot(vte[p], bds[p]) for p in range(ml_pairs)]
        upd = [_dot(vwte[p], block_diag(k2[p][:, :BLK], k2[p][:, BLK:])) for p in range(ml_pairs)]
        o2 = [_dot(vtb, pw[p]) for p in range(pairs)]

        for p in range(ml_pairs):
            nd = scl[p] * cq[p] + vs[p]
            num = nd[:ML_DV]
            den = nd[ML_DV:ML_DV + 1]
            hh = num / jnp.maximum(jnp.abs(den), jnp.exp(-m_row[p]))
            hn = hh * lax.rsqrt(jnp.mean(hh * hh, axis=0, keepdims=True) + NORM_EPS)
            for half in range(2):
                h = 2 * p + half
                og = jax.nn.sigmoid(pf[rows, B_MO + h * ML_DV:B_MO + (h + 1) * ML_DV])
                hn_h = hn[:, half * BLK:(half + 1) * BLK].T
                mix_ref[0, rows, hsl[h]] = (hn_h * mlg[:, hsl[h]] * og).astype(mix_ref.dtype)
            ct[p] = a_p[p] * c_old[p] + upd[p]
            msc[p:p + 1, :] = m_new[p]

        o_parts = [None] * SWA_HEADS
        for p in range(pairs):
            on = o2[p] / den_p[p]
            o_parts[p] = on[0:SWA_HD, 0:BLK]
            o_parts[p + pairs] = on[SWA_HD:2 * SWA_HD, BLK:2 * BLK]
        o_t = jnp.concatenate(o_parts, axis=0)
        mix_ref[0, rows, ML_WIDTH:ML_WIDTH + SWA_WIDTH] = o_t.T.astype(mix_ref.dtype)
        yield

    blocks = [block(j) for j in range(nblk)]
    for g in blocks:
        next(g)
    for g in blocks:
        next(g)

    kband[0:BLK, :] = kband[tm:tm + BLK, :]
    sv_t[0] = sv_t[nblk]

    @pl.when(t == nt - 1)
    def _write_state():
        for h in range(ML_HEADS):
            p, lanes = h // 2, slice((h % 2) * ML_DK, (h % 2 + 1) * ML_DK)
            c_out[0, h] = ct[p, 0:ML_DV, lanes].T
            n_out[0, h:h + 1, :] = ct[p, ML_DV:ML_DV + 1, lanes]
        m_out[0] = msc[...]
        ko_ref[0] = pf[tm - BLK:tm, B_SK:B_SK + BLK]
        vo_ref[0] = vlast[...].T


def _prompt_mixer(x, w_in_l, gbc, mlg, sinks, rb, wo32, wfi32, wfo32):
    bsz, seq, _ = x.shape
    tm = TM_PROMPT
    nt = seq // tm
    nblk = tm // BLK
    steps = bsz * nt
    const2 = lambda b, t: (0, 0)
    smem = pl.BlockSpec(memory_space=pltpu.SMEM)

    def slab_spec(wmat):
        return pl.BlockSpec((wmat.shape[0] // steps, wmat.shape[1]), lambda b, t: (b * nt + t, 0))

    return pl.pallas_call(
        _prompt_mixer_kernel,
        grid=(bsz, nt),
        in_specs=[
            pl.BlockSpec((1, tm, D_MODEL), lambda b, t: (b, t, 0)),
            pl.BlockSpec((D_MODEL, W_IN_PAD), const2, pipeline_mode=pl.Buffered(1)),
            pl.BlockSpec((T_GROWS, BLK), const2),
            pl.BlockSpec((1, ML_WIDTH), const2),
            smem, smem,
            slab_spec(wo32), slab_spec(wfi32), slab_spec(wfo32),
        ],
        out_specs=[
            pl.BlockSpec((1, tm, D_MODEL), lambda b, t: (b, t, 0)),
            pl.BlockSpec((1, ML_HEADS, ML_DK, ML_DV), lambda b, t: (b, 0, 0, 0)),
            pl.BlockSpec((1, ML_HEADS, ML_DK), lambda b, t: (b, 0, 0)),
            pl.BlockSpec((1, M_ROWS, 2 * BLK), lambda b, t: (b, 0, 0)),
            pl.BlockSpec((1, BLK, BLK), lambda b, t: (b, 0, 0)),
            pl.BlockSpec((1, BLK, BLK), lambda b, t: (b, 0, 0)),
            pl.BlockSpec((T_COLS, D_MODEL), const2),
            pl.BlockSpec((D_MODEL, N_B + B_SV), const2),
            slab_spec(wo32), slab_spec(wfi32), slab_spec(wfo32),
        ],
        out_shape=[
            jax.ShapeDtypeStruct((bsz, seq, D_MODEL), BF16),
            jax.ShapeDtypeStruct((bsz, ML_HEADS, ML_DK, ML_DV), F32),
            jax.ShapeDtypeStruct((bsz, ML_HEADS, ML_DK), F32),
            jax.ShapeDtypeStruct((bsz, M_ROWS, 2 * BLK), F32),
            jax.ShapeDtypeStruct((bsz, BLK, BLK), F32),
            jax.ShapeDtypeStruct((bsz, BLK, BLK), F32),
            jax.ShapeDtypeStruct((T_COLS, D_MODEL), BF16),
            jax.ShapeDtypeStruct((D_MODEL, N_B + B_SV), BF16),
            jax.ShapeDtypeStruct(wo32.shape, BF16),
            jax.ShapeDtypeStruct(wfi32.shape, BF16),
            jax.ShapeDtypeStruct(wfo32.shape, BF16),
        ],
        scratch_shapes=[
            pltpu.VMEM((tm, ML_HEADS * ML_DK), BF16),
            pltpu.VMEM((tm, B_SV), F32),
            pltpu.VMEM((BLK, BLK), F32),
            pltpu.VMEM((tm + BLK, BLK), BF16),
            pltpu.VMEM((nblk, ML_HEADS * ML_DK, BLK), BF16),
            pltpu.VMEM((nblk, ML_WIDTH, BLK), BF16),
            pltpu.VMEM((nblk, SWA_WIDTH, BLK), BF16),
            pltpu.VMEM((nblk + 1, SWA_KV_HEADS * SWA_HD, BLK), BF16),
            pltpu.VMEM((nblk, T_GROWS, BLK), F32),
            pltpu.VMEM((ML_HEADS // 2, STATE_ROWS, 2 * ML_DK), F32),
            pltpu.VMEM((M_ROWS, 2 * BLK), F32),
            pltpu.VMEM((2, SWA_HEADS // 2, BLK, 2 * BLK), F32),
        ],
        compiler_params=pltpu.CompilerParams(
            dimension_semantics=("arbitrary", "arbitrary"),
            vmem_limit_bytes=VMEM_LIMIT),
        name="prompt_mixer",
    )(x, w_in_l, gbc, mlg, sinks, rb, wo32, wfi32, wfo32)


def _roll_rows(x, shift):
    return pltpu.roll(x, shift % x.shape[0], 0)


def _sample_mixer_kernel(x_ref, wt_ref, wn_ref, gb_ref, mlg_ref, sinks_ref, rb_ref, mrep_ref,
                         c_ref, n_ref, kc_ref, vc_ref,
                         mix_ref, c_out, n_out, mo_ref, ko_ref, vo_ref,
                         pa, pf, tblc, tbln):
    step = pl.program_id(0)
    tdec = 4
    rows_step = mix_ref.shape[0]
    nslab = rows_step // SLAB
    nseq = SLAB // tdec
    qrows = SWA_HEADS * SLAB

    @pl.when(step == 0)
    def _project_all_rows():
        xb = x_ref[...].astype(BF16)
        pa[:, A_Q:A_K] = _dot_nt(xb, wt_ref[T_Q:T_Q + ML_HEADS * ML_DK, :])
        pa[:, A_K:A_V] = _dot(xb, wn_ref[:, N_K:N_K + ML_HEADS * ML_DK])
        pa[:, A_V:A_COLS] = _dot_nt(xb, wt_ref[T_V:T_V + ML_WIDTH + SWA_WIDTH, :])
        pf[:, B_MO:B_SV] = _dot(xb, wn_ref[:, N_B:N_B + B_SV])
        pf[:, B_SV:B_COLS] = _dot_nt(xb, wt_ref[T_SV:T_COLS, :])

    @pl.when(step == 0)
    def _build_bias_tables():
        r = lax.broadcasted_iota(jnp.int32, (qrows, BLK), 0)
        c = lax.broadcasted_iota(jnp.int32, (qrows, BLK), 1)
        hq = r // SLAB
        bl = (r % SLAB) // tdec
        tq = r % tdec
        d_cache = WINDOW + tq - c
        d_new = tq - (c % tdec)
        key_cache = _t5_bucket(d_cache) * SWA_HEADS + hq
        key_new = _t5_bucket(d_new) * SWA_HEADS + hq
        acc_c = jnp.zeros((qrows, BLK), F32)
        acc_n = jnp.zeros((qrows, BLK), F32)
        for k in range(REL_BUCKETS):
            for h in range(SWA_HEADS):
                val = rb_ref[k, h]
                acc_c = jnp.where(key_cache == k * SWA_HEADS + h, val, acc_c)
                acc_n = jnp.where(key_new == k * SWA_HEADS + h, val, acc_n)
        tblc[...] = jnp.where(c > tq, acc_c, NEG_INF)
        tbln[...] = jnp.where((c < SLAB) & ((c // tdec) == bl) & (d_new >= 0), acc_n, NEG_INF)

    gbias = gb_ref[...]
    mlg = mlg_ref[...]
    r16 = lax.broadcasted_iota(jnp.int32, (SLAB, BLK), 0)
    rr = r16 % tdec
    bl16 = r16 // tdec
    rq = lax.broadcasted_iota(jnp.int32, (qrows, BLK), 0)
    blq = (rq % SLAB) // tdec
    hq_col = rq[:, 0:1] // SLAB
    snk = jnp.zeros((qrows, 1), F32)
    for h in range(SWA_HEADS):
        snk = jnp.where(hq_col == h, sinks_ref[h], snk)
    first_group = rq[:, 0:SWA_HD] < SWA_GROUP * SLAB
    zq = jnp.zeros((SLAB, SWA_HD), F32)
    zeros_pad = jnp.zeros((BLK - SLAB, BLK), BF16)

    def seg_last(x):
        return jnp.where(rr == 3, x,
                         jnp.where(rr == 2, _roll_rows(x, -1),
                                   jnp.where(rr == 1, _roll_rows(x, -2), _roll_rows(x, -3))))

    def seg_max(x):
        m1 = jnp.maximum(x, jnp.where(rr % 2 == 0, _roll_rows(x, -1), _roll_rows(x, 1)))
        return jnp.maximum(m1, jnp.where(rr < 2, _roll_rows(m1, -2), _roll_rows(m1, 2)))

    def slab(si):
        rows = pl.ds(si * SLAB, SLAB)
        prow = pl.ds(pl.multiple_of(step * rows_step + si * SLAB, SLAB), SLAB)
        seq0 = si * nseq

        pre = _softcap(pf[prow, B_G:B_G + BLK] + gbias)
        logf = _log_sigmoid(pre)
        y = logf + jnp.where(rr >= 1, _roll_rows(logf, 1), 0.0)
        bcum = y + jnp.where(rr >= 2, _roll_rows(y, 2), 0.0)
        ig = pltpu.roll(pre, ML_HEADS, 1)
        m_prev = mrep_ref[rows, :]
        b_minus_i = bcum - ig
        log_d = [jnp.where(rr >= dl, bcum - _roll_rows(b_minus_i, dl), -jnp.inf) for dl in range(tdec)]
        rowmax = functools.reduce(jnp.maximum, log_d)
        inter = bcum + m_prev
        m_row = jnp.maximum(inter, rowmax)
        scl = jnp.exp(inter - m_row)
        dm = [jnp.exp(ld - m_row) for ld in log_d]
        enm = jnp.exp(-m_row)
        b_last = seg_last(bcum)
        w = b_last - bcum + ig
        m_new = jnp.maximum(b_last + m_prev, seg_max(w))
        a = jnp.exp(b_last + m_prev - m_new)
        kws = jnp.exp(w - m_new) * ML_SCALE
        mo_ref[rows, :] = m_new

        q = [pa[prow, A_Q + h * ML_DK:A_Q + (h + 1) * ML_DK] for h in range(ML_HEADS)]
        k = [pa[prow, A_K + h * ML_DK:A_K + (h + 1) * ML_DK] for h in range(ML_HEADS)]
        v = [pa[prow, A_V + h * ML_DV:A_V + (h + 1) * ML_DV] for h in range(ML_HEADS)]
        c_b = [[c_ref[seq0 + bl, h] for bl in range(nseq)] for h in range(ML_HEADS)]
        qc, upd, kw = [], [], []
        for h in range(ML_HEADS):
            qb = q[h].astype(BF16)
            res = [_dot(qb, jnp.concatenate([c_b[h][2 * i].astype(BF16), c_b[h][2 * i + 1].astype(BF16)], axis=1))
                   for i in range(nseq // 2)]
            parts = [res[bl // 2][:, (bl % 2) * ML_DV:(bl % 2 + 1) * ML_DV] for bl in range(nseq)]
            qc_h = parts[nseq - 1]
            for bl in range(nseq - 2, -1, -1):
                qc_h = jnp.where(bl16 == bl, parts[bl], qc_h)
            qc.append(qc_h)

        knew = pf[prow, B_SK:B_SK + BLK]
        vnew = pf[prow, B_SV:B_SV + BLK]
        qparts = []
        for hq in range(SWA_HEADS):
            qh = pa[prow, A_SQ + hq * SWA_HD:A_SQ + (hq + 1) * SWA_HD]
            qparts.append(jnp.concatenate([qh, zq] if hq < SWA_GROUP else [zq, qh], axis=1))
        qs = jnp.concatenate(qparts, axis=0).astype(BF16)
        s_c = _dot_nt(qs, kc_ref[seq0 + nseq - 1].astype(BF16))
        for bl in range(nseq - 2, -1, -1):
            s_c = jnp.where(blq == bl, _dot_nt(qs, kc_ref[seq0 + bl].astype(BF16)), s_c)
        kn = jnp.concatenate([knew.astype(BF16), zeros_pad], axis=0)
        s_n = _dot_nt(qs, kn)

        yield
        for h in range(ML_HEADS):
            lane = ML_HEADS + h
            kw_h = k[h] * kws[:, lane:lane + 1]
            kw.append(kw_h)
            vm = jnp.concatenate([jnp.where(bl16 == bl, v[h], 0.0) for bl in range(nseq)], axis=1)
            upd.append(_dot_tn(kw_h.astype(BF16), vm.astype(BF16)))

        yield
        for h in range(ML_HEADS):
            hs = slice(h * ML_DK, (h + 1) * ML_DK)
            lane = ML_HEADS + h
            bm = jnp.zeros((SLAB, ML_DV), F32)
            sum_s = jnp.zeros((SLAB, 1), F32)
            for dl in range(tdec):
                s_dl = jnp.sum(q[h] * _roll_rows(k[h], dl), axis=-1, keepdims=True)
                ws = s_dl * ML_SCALE * dm[dl][:, lane:lane + 1]
                bm = bm + ws * _roll_rows(v[h], dl)
                sum_s = sum_s + ws
            nsel = jnp.zeros((SLAB, ML_DK), F32)
            for bl in range(nseq):
                msk = bl16 == bl
                n_b = n_ref[seq0 + bl][h:h + 1, :]
                nsel = jnp.where(msk, n_b, nsel)
                a_b = a[bl * tdec:bl * tdec + 1, lane:lane + 1]
                c_out[seq0 + bl, h] = a_b * c_b[h][bl] + upd[h][:, bl * ML_DV:(bl + 1) * ML_DV]
                n_out[seq0 + bl, h:h + 1, :] = a_b * n_b + jnp.sum(jnp.where(msk, kw[h], 0.0), axis=0,
                                                                   keepdims=True)
            sclh = scl[:, lane:lane + 1]
            num = sclh * qc[h] + bm
            den = sclh * jnp.sum(q[h] * nsel, axis=-1, keepdims=True) + sum_s
            hh = num / jnp.maximum(jnp.abs(den), enm[:, lane:lane + 1])
            hn = hh * lax.rsqrt(jnp.mean(hh * hh, axis=-1, keepdims=True) + NORM_EPS)
            og = jax.nn.sigmoid(pf[prow, B_MO + h * ML_DV:B_MO + (h + 1) * ML_DV])
            mix_ref[rows, hs] = (hn * mlg[:, hs] * og).astype(mix_ref.dtype)

        s_c = s_c * SWA_SCALE + tblc[...]
        s_n = s_n * SWA_SCALE + tbln[...]
        mx = jnp.maximum(jnp.maximum(jnp.max(s_c, axis=-1, keepdims=True),
                                     jnp.max(s_n, axis=-1, keepdims=True)), snk)
        p_c = jnp.exp(s_c - mx)
        p_n = jnp.exp(s_n - mx)
        den = (jnp.sum(p_c, axis=-1, keepdims=True) + jnp.sum(p_n, axis=-1, keepdims=True)
               + jnp.exp(snk - mx))

        yield
        vn = jnp.concatenate([vnew.astype(BF16), zeros_pad], axis=0)
        o = _dot(p_n.astype(BF16), vn)
        for bl in range(nseq):
            o = o + _dot(jnp.where(blq == bl, p_c, 0.0).astype(BF16), vc_ref[seq0 + bl].astype(BF16))
        o = o / den
        osel = jnp.where(first_group, o[:, 0:SWA_HD], o[:, SWA_HD:2 * SWA_HD])
        og = jnp.concatenate([osel[hq * SLAB:(hq + 1) * SLAB, :] for hq in range(SWA_HEADS)], axis=1)
        mix_ref[rows, ML_WIDTH:ML_WIDTH + SWA_WIDTH] = og.astype(mix_ref.dtype)

        for bl in range(nseq):
            ko_ref[seq0 + bl, 0:WINDOW - tdec, :] = kc_ref[seq0 + bl, tdec:WINDOW, :]
            vo_ref[seq0 + bl, 0:WINDOW - tdec, :] = vc_ref[seq0 + bl, tdec:WINDOW, :]
            ko_ref[seq0 + bl, WINDOW - tdec:WINDOW, :] = knew[bl * tdec:(bl + 1) * tdec, :]
            vo_ref[seq0 + bl, WINDOW - tdec:WINDOW, :] = vnew[bl * tdec:(bl + 1) * tdec, :]
        yield

    slabs = [slab(si) for si in range(nslab)]
    for _ in range(4):
        for s in slabs:
            next(s)


def _sample_mixer(x2d, wt, wn, gb, mlg, sinks, rb, mrep, c0, n0, kc, vc):
    rows = x2d.shape[0]
    tdec = 4
    nseq = rows // tdec
    gbs = SAMPLE_GB
    steps = nseq // gbs
    rb_rows = gbs * tdec
    const2 = lambda i: (0, 0)
    smem = pl.BlockSpec(memory_space=pltpu.SMEM)
    return pl.pallas_call(
        _sample_mixer_kernel,
        grid=(steps,),
        in_specs=[
            pl.BlockSpec((rows, D_MODEL), const2, pipeline_mode=pl.Buffered(1)),
            pl.BlockSpec((T_COLS, D_MODEL), const2, pipeline_mode=pl.Buffered(1)),
            pl.BlockSpec((D_MODEL, N_B + B_SV), const2, pipeline_mode=pl.Buffered(1)),
            pl.BlockSpec((1, BLK), const2),
            pl.BlockSpec((1, ML_WIDTH), const2),
            smem, smem,
            pl.BlockSpec((rb_rows, BLK), lambda i: (i, 0)),
            pl.BlockSpec((gbs, ML_HEADS, ML_DK, ML_DV), lambda i: (i, 0, 0, 0)),
            pl.BlockSpec((gbs, ML_HEADS, ML_DK), lambda i: (i, 0, 0)),
            pl.BlockSpec((gbs, WINDOW, BLK), lambda i: (i, 0, 0)),
            pl.BlockSpec((gbs, WINDOW, BLK), lambda i: (i, 0, 0)),
        ],
        out_specs=[
            pl.BlockSpec((rb_rows, D_MODEL), lambda i: (i, 0)),
            pl.BlockSpec((gbs, ML_HEADS, ML_DK, ML_DV), lambda i: (i, 0, 0, 0)),
            pl.BlockSpec((gbs, ML_HEADS, ML_DK), lambda i: (i, 0, 0)),
            pl.BlockSpec((rb_rows, BLK), lambda i: (i, 0)),
            pl.BlockSpec((gbs, WINDOW, BLK), lambda i: (i, 0, 0)),
            pl.BlockSpec((gbs, WINDOW, BLK), lambda i: (i, 0, 0)),
        ],
        out_shape=[
            jax.ShapeDtypeStruct((rows, D_MODEL), BF16),
            jax.ShapeDtypeStruct((nseq, ML_HEADS, ML_DK, ML_DV), F32),
            jax.ShapeDtypeStruct((nseq, ML_HEADS, ML_DK), F32),
            jax.ShapeDtypeStruct((rows, BLK), F32),
            jax.ShapeDtypeStruct((nseq, WINDOW, BLK), F32),
            jax.ShapeDtypeStruct((nseq, WINDOW, BLK), F32),
        ],
        scratch_shapes=[
            pltpu.VMEM((rows, A_COLS), F32),
            pltpu.VMEM((rows, B_COLS), F32),
            pltpu.VMEM((SWA_HEADS * SLAB, BLK), F32),
            pltpu.VMEM((SWA_HEADS * SLAB, BLK), F32),
        ],
        compiler_params=pltpu.CompilerParams(
            dimension_semantics=("arbitrary",),
            vmem_limit_bytes=VMEM_LIMIT),
        name="sample_mixer",
    )(x2d, wt, wn, gb, mlg, sinks, rb, mrep, c0, n0, kc, vc)


def _dense_kernel(x_ref, mix_ref, xs_ref, mixs_ref, wo_ref, g1_ref, b1_ref, wfi_ref, wfo_ref, g2_ref, b2_ref,
                  y_ref, ys_ref):
    bounds = [b for b in DENSE_SPLITS if b <= x_ref.shape[0]]
    nsub = len(bounds) - 1
    subs = [slice(bounds[i], bounds[i + 1]) for i in range(nsub)]
    chunks = list(zip(FF_SPLITS[:-1], FF_SPLITS[1:]))
    g1, b1, g2, b2 = g1_ref[...], b1_ref[...], g2_ref[...], b2_ref[...]

    xin = [x_ref[s, :] for s in subs]
    mixin = [mix_ref[s, :] for s in subs]
    xin[-1] = jnp.concatenate([xin[-1], xs_ref[...]], axis=0)
    mixin[-1] = jnp.concatenate([mixin[-1], mixs_ref[...]], axis=0)

    proj = [_dot(mixin[i], wo_ref[...]) for i in range(nsub)]
    h1 = [_layer_norm(ALPHA * xin[i] + proj[i], g1, b1) for i in range(nsub)]
    h1b = [h.astype(BF16) for h in h1]
    acc = [None] * nsub
    for c0, c1 in chunks:
        gate, up = [], []
        for i in range(nsub):
            gate.append(_dot(h1b[i], wfi_ref[:, c0:c1]))
            up.append(_dot(h1b[i], wfi_ref[:, D_FF + c0:D_FF + c1]))
        for i in range(nsub):
            act = (gate[i] * jax.nn.sigmoid(gate[i]) * up[i]).astype(BF16)
            part = _dot(act, wfo_ref[c0:c1, :])
            acc[i] = part if acc[i] is None else acc[i] + part
    for i, s in enumerate(subs):
        out = _layer_norm(ALPHA * h1[i] + acc[i], g2, b2)
        nrows = s.stop - s.start
        y_ref[s, :] = out[:nrows]
        if i == nsub - 1:
            ys_ref[...] = out[nrows:]


def _dense(x2d, mix2d, xs2d, mixs2d, wo, g1, b1, wfi, wfo, g2, b2):
    rows = x2d.shape[0]
    tm = TM_DENSE
    steps = rows // tm
    srows = xs2d.shape[0] // steps
    const2 = lambda i: (0, 0)
    step2 = lambda i: (i, 0)

    def wspec(shape):
        return pl.BlockSpec(shape, const2, pipeline_mode=pl.Buffered(1))

    return pl.pallas_call(
        _dense_kernel,
        grid=(steps,),
        in_specs=[
            pl.BlockSpec((tm, D_MODEL), step2),
            pl.BlockSpec((tm, D_MODEL), step2),
            pl.BlockSpec((srows, D_MODEL), step2),
            pl.BlockSpec((srows, D_MODEL), step2),
            wspec((D_MODEL, D_MODEL)),
            wspec((1, D_MODEL)), wspec((1, D_MODEL)),
            wspec((D_MODEL, 2 * D_FF)),
            wspec((D_FF, D_MODEL)),
            wspec((1, D_MODEL)), wspec((1, D_MODEL)),
        ],
        out_specs=[pl.BlockSpec((tm, D_MODEL), step2), pl.BlockSpec((srows, D_MODEL), step2)],
        out_shape=[jax.ShapeDtypeStruct((rows, D_MODEL), F32),
                   jax.ShapeDtypeStruct((xs2d.shape[0], D_MODEL), F32)],
        compiler_params=pltpu.CompilerParams(
            dimension_semantics=("arbitrary",),
            vmem_limit_bytes=VMEM_LIMIT),
        name="dense",
    )(x2d, mix2d, xs2d, mixs2d, wo, g1, b1, wfi, wfo, g2, b2)


def kernel(x_prompt, x_sample, state_mlstm_C, state_mlstm_n, state_mlstm_m, cache_swa_k, cache_swa_v,
           w_in, b_igate, b_fgate, ml_norm_g, swa_sinks, rel_bias, w_out, ln1_g, ln1_b,
           w_ffn_in, w_ffn_out, ln2_g, ln2_b):
    bp, seq, _ = x_prompt.shape
    bs, tdec, _ = x_sample.shape
    l = 0

    w16 = jnp.pad(w_in[l], ((0, 0), (0, W_IN_PAD - w_in.shape[-1]))).astype(BF16)
    gvec = jnp.concatenate([b_igate[l], b_fgate[l]]).astype(F32)
    gb = jnp.pad(gvec, (0, BLK - 2 * ML_HEADS))[None, :]
    gbc = jnp.broadcast_to(jnp.pad(gvec, (0, T_GROWS - 2 * ML_HEADS))[:, None], (T_GROWS, BLK))
    mlg = ml_norm_g[l][None, :].astype(F32)
    sinks = swa_sinks[l].astype(F32)
    rb = rel_bias.astype(F32)
    g1, b1 = ln1_g[l][None, :], ln1_b[l][None, :]
    g2, b2 = ln2_g[l][None, :], ln2_b[l][None, :]

    mix_p, c_p, n_p, m_p, k_p, v_p, wt, wn, wo, wfi, wfo = _prompt_mixer(
        x_prompt, w16, gbc, mlg, sinks, rb, w_out[l], w_ffn_in[l], w_ffn_out[l])
    p_m =m_p[:, :ML_HEADS // 2, ::BLK].reshape(bp, ML_HEADS)[None]
    p_k = k_p.reshape(1, bp, WINDOW, SWA_KV_HEADS, SWA_HD)
    p_v = v_p.reshape(1, bp, WINDOW, SWA_KV_HEADS, SWA_HD)

    xs = x_sample.reshape(bs * tdec, D_MODEL)
    m0 = state_mlstm_m[l].astype(F32)
    mrep = jnp.pad(jnp.repeat(m0, tdec, axis=0), ((0, 0), (ML_HEADS, BLK - 2 * ML_HEADS)))
    wlen = cache_swa_k.shape[2]
    kc = cache_swa_k[l].reshape(bs, wlen, SWA_KV_HEADS * SWA_HD)
    vc = cache_swa_v[l].reshape(bs, wlen, SWA_KV_HEADS * SWA_HD)
    mix_s, c_s, n_s, mo_s, k_s, v_s = _sample_mixer(
        xs, wt, wn, gb, mlg, sinks, rb, mrep,
        state_mlstm_C[l].astype(F32), state_mlstm_n[l].astype(F32), kc, vc)
    s_m =mo_s.reshape(bs, tdec, BLK)[:, 0, ML_HEADS:2 * ML_HEADS][None]
    s_k = k_s.reshape(1, bs, wlen, SWA_KV_HEADS, SWA_HD)
    s_v = v_s.reshape(1, bs, wlen, SWA_KV_HEADS, SWA_HD)

    y_p, y_s = _dense(x_prompt.reshape(bp * seq, D_MODEL), mix_p.reshape(bp * seq, D_MODEL), xs, mix_s,
                      wo, g1, b1, wfi, wfo, g2, b2)
    y_p = y_p.reshape(bp, seq, D_MODEL)
    y_s = y_s.reshape(bs, tdec, D_MODEL)

    return (y_p, y_s, c_p[None], n_p[None], p_m, p_k, p_v,
            c_s[None], n_s[None], s_m, s_k, s_v)
```

```python
import functools
import math

import jax
import jax.numpy as jnp
from jax import lax
from jax.experimental import pallas as pl
from jax.experimental.pallas import tpu as pltpu

F32 = jnp.float32
BF16 = jnp.bfloat16

D_MODEL = 1024
ML_HEADS = 4
ML_DK = 128
ML_DV = 128
ML_WIDTH = ML_HEADS * ML_DV
GATE_SOFTCAP = 15.0
SWA_HEADS = 8
SWA_KV_HEADS = 2
SWA_GROUP = SWA_HEADS // SWA_KV_HEADS
SWA_HD = 64
SWA_WIDTH = SWA_HEADS * SWA_HD
WINDOW = 128
REL_BUCKETS = 32
REL_MAX_DIST = 128
D_FF = 2816
DEPTH = 1
ALPHA = (2.0 * DEPTH) ** 0.25
LN_EPS = 1e-5
NORM_EPS = 1e-6
NEG_INF = -1e30
IN_SIZES = (512, 512, 512, 512, 4, 4, 512, 128, 128)
ML_SCALE = ML_DK ** -0.5
LOG_ML_SCALE = math.log(ML_SCALE)
SWA_SCALE = SWA_HD ** -0.5

BLK = 128
SUBLANES_F32 = 8
SUBLANES_BF16 = 16

A_Q, A_K, A_V, A_SQ = 0, 512, 1024, 1536
A_COLS = 2048
B_MO, B_SK, B_SV, B_G = 0, 512, 640, 768
B_COLS = 896

N_K = 0
N_B = 512
T_Q, T_V, T_SQ, T_SV, T_G = 0, 512, 1024, 1536, 1664
T_GROWS = SUBLANES_BF16
T_COLS = T_G + BLK
W_IN_PAD = -(-sum(IN_SIZES) // BLK) * BLK
STATE_ROWS = ML_DV + SUBLANES_BF16
M_ROWS = SUBLANES_F32

TM_PROMPT = 1024
TM_DENSE = 1024
DENSE_SPLITS = (0, 384, 768, 1024)
FF_SPLITS = (0, 1536, D_FF)
SAMPLE_GB = 16
SLAB = 16
VMEM_LIMIT = 56 * 1024 * 1024


def _softcap(a):
    return GATE_SOFTCAP * jnp.tanh(a / GATE_SOFTCAP)


def _log_sigmoid(x):
    return jnp.minimum(x, 0.0) - jnp.log1p(jnp.exp(-jnp.abs(x)))


def _layer_norm(z, g, b):
    mu = jnp.mean(z, axis=-1, keepdims=True)
    zc = z - mu
    var = jnp.mean(zc * zc, axis=-1, keepdims=True)
    return zc * lax.rsqrt(var + LN_EPS) * g + b


def _dot(a, b):
    return jnp.dot(a, b, preferred_element_type=F32)


def _dot_nt(a, b):
    return lax.dot_general(a, b, (((1,), (1,)), ((), ())), preferred_element_type=F32)


def _dot_tn(a, b):
    return lax.dot_general(a, b, (((0,), (0,)), ((), ())), preferred_element_type=F32)


def _split3(x):
    hi = x.astype(BF16)
    r1 = x - hi.astype(F32)
    mid = r1.astype(BF16)
    lo = (r1 - mid.astype(F32)).astype(BF16)
    return hi, mid, lo


def _t5_bucket(d):
    n = jnp.maximum(d, 0)
    max_exact = REL_BUCKETS // 2
    nlog = REL_BUCKETS - max_exact
    large = jnp.full(n.shape, max_exact, jnp.int32)
    for k in range(1, nlog):
        thr = math.ceil(max_exact * (REL_MAX_DIST / max_exact) ** (k / nlog))
        large = large + jnp.where(n >= thr, 1, 0)
    return jnp.where(n < max_exact, n, large)


def _prompt_mixer_kernel(x_ref, w_ref, gbc_ref, mlg_ref, sinks_ref, rb_ref, wo32, wfi32, wfo32,
                         mix_ref, c_out, n_out, m_out, ko_ref, vo_ref,
                         wt, wn, wo16, wfi16, wfo16,
                         kn, pf, vlast, kband, q_t, v_t, sq_t, sv_t, g_t, ct, msc, tbl):
    wo16[...] = wo32[...].astype(BF16)
    wfi16[...] = wfi32[...].astype(BF16)
    wfo16[...] = wfo32[...].astype(BF16)

    b = pl.program_id(0)
    t = pl.program_id(1)
    nt = pl.num_programs(1)
    tm = x_ref.shape[1]
    nblk = tm // BLK
    pairs = SWA_HEADS // 2

    @pl.when((b == 0) & (t == 0))
    def _prepare_weights():
        offs = [0]
        for n in IN_SIZES:
            offs.append(offs[-1] + n)
        o_mq, o_mk, o_mv, o_mo, o_mi, _, o_sq, o_sk, o_sv = offs[:-1]

        def col_blocks(dst_col, src_col, n):
            for i in range(n // BLK):
                wn[:, dst_col + i * BLK:dst_col + (i + 1) * BLK] = (
                    w_ref[:, src_col + i * BLK:src_col + (i + 1) * BLK])

        def row_blocks(dst_row, src_col, n):
            for i in range(n // BLK):
                wt[dst_row + i * BLK:dst_row + (i + 1) * BLK, :] = (
                    w_ref[:, src_col + i * BLK:src_col + (i + 1) * BLK].T)

        col_blocks(N_K, o_mk, ML_HEADS * ML_DK)
        col_blocks(N_B + B_MO, o_mo, ML_WIDTH)
        col_blocks(N_B + B_SK, o_sk, SWA_KV_HEADS * SWA_HD)
        row_blocks(T_Q, o_mq, ML_HEADS * ML_DK)
        row_blocks(T_V, o_mv, ML_WIDTH)
        row_blocks(T_SQ, o_sq, SWA_WIDTH)
        row_blocks(T_SV, o_sv, SWA_KV_HEADS * SWA_HD)
        gates_t = w_ref[:, o_mi:o_mi + BLK].T
        grow = lax.broadcasted_iota(jnp.int32, gates_t.shape, 0)
        wt[T_G:T_G + BLK, :] = jnp.where(grow < 2 * ML_HEADS, gates_t, jnp.zeros_like(gates_t))

    @pl.when((b == 0) & (t == 0))
    def _build_bias_tables():
        r = lax.broadcasted_iota(jnp.int32, (BLK, 2 * BLK), 0)
        ln = lax.broadcasted_iota(jnp.int32, (BLK, 2 * BLK), 1)
        second = ln >= BLK
        qi = jnp.where(second, ln - BLK, ln)
        prev = r > qi
        d = jnp.where(prev, WINDOW + qi - r, qi - r)
        bucket = _t5_bucket(d)
        for p in range(pairs):
            acc = jnp.zeros((BLK, 2 * BLK), F32)
            for k in range(REL_BUCKETS):
                acc = jnp.where(bucket == k, jnp.where(second, rb_ref[k, p + pairs], rb_ref[k, p]), acc)
            tbl[0, p] = jnp.where(prev, NEG_INF, acc)
            tbl[1, p] = acc

    @pl.when(t == 0)
    def _reset_state():
        ct[...] = jnp.zeros_like(ct)
        msc[...] = jnp.zeros_like(msc)
        kband[0:BLK, :] = jnp.zeros((BLK, kband.shape[1]), kband.dtype)
        sv_t[0] = jnp.zeros(sv_t.shape[1:], sv_t.dtype)

    xb = x_ref[0].astype(BF16)
    res_n = _dot(xb, wn[...])
    kn[...] = res_n[:, N_K:N_K + ML_HEADS * ML_DK].astype(BF16)
    pf[...] = res_n[:, N_B:N_B + B_SV]
    kband[BLK:BLK + tm, :] = pf[:, B_SK:B_SK + BLK].astype(BF16)

    res_t = _dot_nt(wt[0:T_G + T_GROWS, :], xb)
    vlast[...] = res_t[T_SV:T_SV + BLK, tm - BLK:tm]

    def put(dst, r0, nrows, off, scale=None):
        res = res_t[r0:r0 + nrows, :]
        if scale is not None:
            res = res * scale
        for jj in range(nblk):
            dst[jj + off] = res[:, jj * BLK:(jj + 1) * BLK].astype(dst.dtype)

    put(q_t, T_Q, ML_HEADS * ML_DK, 0)
    put(v_t, T_V, ML_WIDTH, 0)
    put(sq_t, T_SQ, SWA_WIDTH, 0, SWA_SCALE)
    put(sv_t, T_SV, SWA_KV_HEADS * SWA_HD, 1)
    put(g_t, T_G, T_GROWS, 0)

    row = lax.broadcasted_iota(jnp.int32, (BLK, BLK), 0)
    col = lax.broadcasted_iota(jnp.int32, (BLK, BLK), 1)
    causal_t = row <= col
    triu = causal_t.astype(BF16)
    causal2 = jnp.concatenate([causal_t, causal_t], axis=1)
    r16 = lax.broadcasted_iota(jnp.int32, (STATE_ROWS - ML_DV, 2 * BLK), 0)
    ones_rows = (r16 == 0).astype(BF16)
    z128 = jnp.zeros((BLK, BLK), BF16)
    ml_pairs = ML_HEADS // 2
    zero_rows = jnp.zeros((BLK - T_GROWS, BLK), F32)
    gbc = gbc_ref[...]
    mlg = mlg_ref[...]
    lane2 = lax.broadcasted_iota(jnp.int32, (1, 2 * BLK), 1)
    snk_rows = [jnp.where(lane2 >= BLK, sinks_ref[p + pairs], sinks_ref[p]) for p in range(pairs)]
    zq = jnp.zeros((SWA_HD, BLK), BF16)
    prev2 = jnp.concatenate([row > col, row > col], axis=1)

    def pair_row(x0, x1):
        return jnp.concatenate([jnp.broadcast_to(x0, (1, BLK)), jnp.broadcast_to(x1, (1, BLK))], axis=1)

    def block_diag(x0, x1):
        return jnp.concatenate([jnp.concatenate([x0, z128], axis=1),
                                jnp.concatenate([z128, x1], axis=1)], axis=0)

    def block(j):
        r0 = j * BLK
        rows = pl.ds(r0, BLK)
        hsl = [slice(h * ML_DK, (h + 1) * ML_DK) for h in range(ML_HEADS)]
        psl = [slice(p * 2 * ML_DK, (p + 1) * 2 * ML_DK) for p in range(ml_pairs)]

        pre_t = _softcap(g_t[j] + gbc)
        hi, mid, lo = _split3(_log_sigmoid(pre_t))
        b_t = _dot(hi, triu) + _dot(mid, triu) + _dot(lo, triu)
        u_t = pltpu.roll(pre_t, ML_HEADS, 0) - b_t
        u_c = jnp.concatenate([u_t, zero_rows], axis=0).T

        k2 = [kn[rows, psl[p]] for p in range(ml_pairs)]
        bdq = [block_diag(q_t[j, hsl[2 * p], :], q_t[j, hsl[2 * p + 1], :]) for p in range(ml_pairs)]
        vte = [jnp.concatenate([jnp.concatenate([v_t[j, hsl[2 * p], :], v_t[j, hsl[2 * p + 1], :]], axis=1),
                                ones_rows], axis=0) for p in range(ml_pairs)]
        kq = [_dot(k2[p], bdq[p]) for p in range(ml_pairs)]

        sel = jnp.where(t == 0, 0, 1) if j == 0 else 1
        kb = kband[pl.ds(r0, 2 * BLK), :]
        vtb = jnp.concatenate([sv_t[j], sv_t[j + 1]], axis=1)
        sc2 = []
        for p in range(pairs):
            qa = sq_t[j, p * SWA_HD:(p + 1) * SWA_HD, :]
            qb = sq_t[j, (p + pairs) * SWA_HD:(p + pairs + 1) * SWA_HD, :]
            bd = jnp.concatenate([jnp.concatenate([qa, zq], axis=1),
                                  jnp.concatenate([zq, qb], axis=1)], axis=0)
            sc2.append(_dot(kb, bd))

        yield
        bds, scl, m_row, m_new, a_p, vwte = [], [], [], [], [], []
        for p in range(ml_pairs):
            h0, h1 = 2 * p, 2 * p + 1
            b_row = pair_row(b_t[ML_HEADS + h0:ML_HEADS + h0 + 1, :], b_t[ML_HEADS + h1:ML_HEADS + h1 + 1, :])
            i_row = pair_row(pre_t[h0:h0 + 1, :], pre_t[h1:h1 + 1, :])
            u_col = jnp.concatenate(
                [jnp.broadcast_to(u_c[:, ML_HEADS + h0:ML_HEADS + h0 + 1], (BLK, BLK)),
                 jnp.broadcast_to(u_c[:, ML_HEADS + h1:ML_HEADS + h1 + 1], (BLK, BLK))], axis=1)
            m_prev = msc[p:p + 1, :]
            log_d = jnp.where(causal2, b_row + u_col, -jnp.inf)
            inter = b_row + m_prev
            mr = jnp.maximum(inter, jnp.max(log_d, axis=0, keepdims=True))
            m_row.append(mr)
            scl.append(jnp.exp(inter - mr))
            s2 = (kq[p] * jnp.exp(log_d - (mr - LOG_ML_SCALE))).astype(BF16)
            bds.append(block_diag(s2[:, :BLK], s2[:, BLK:]))
            b_last = pair_row(b_row[:, BLK - 1:BLK], b_row[:, 2 * BLK - 1:2 * BLK])
            w_row = b_last - b_row + i_row
            w_max = pair_row(jnp.max(w_row[:, :BLK], axis=1, keepdims=True),
                             jnp.max(w_row[:, BLK:], axis=1, keepdims=True))
            mn = jnp.maximum(b_last + m_prev, w_max)
            m_new.append(mn)
            a_p.append(jnp.exp(b_last + m_prev - mn))
            wexp = jnp.exp(w_row - mn) * ML_SCALE
            vwte.append(vte[p] * wexp.astype(BF16))

        pw, den_p = [], []
        for p in range(pairs):
            sc = jnp.where(prev2, sc2[p][:BLK], sc2[p][BLK:]) + tbl[sel, p]
            snk = snk_rows[p]
            mx = jnp.maximum(jnp.max(sc, axis=0, keepdims=True), snk)
            pe = jnp.exp(sc - mx)
            den_p.append(jnp.sum(pe, axis=0, keepdims=True) + jnp.exp(snk - mx))
            pw.append(jnp.concatenate([jnp.where(prev2, pe, 0.0), jnp.where(prev2, 0.0, pe)],
                                      axis=0).astype(BF16))

        c_old = [ct[p] for p in range(ml_pairs)]
        cq = [_dot(c_old[p].astype(BF16), bdq[p]) for p in range(ml_pairs)]
        vs = [_dot(vte[p], bds[p]) for p in range(ml_pairs)]
        upd = [_dot(vwte[p], block_diag(k2[p][:, :BLK], k2[p][:, BLK:])) for p in range(ml_pairs)]
        o2 = [_dot(vtb, pw[p]) for p in range(pairs)]

        for p in range(ml_pairs):
            nd = scl[p] * cq[p] + vs[p]
            num = nd[:ML_DV]
            den = nd[ML_DV:ML_DV + 1]
            hh = num / jnp.maximum(jnp.abs(den), jnp.exp(-m_row[p]))
            hn = hh * lax.rsqrt(jnp.mean(hh * hh, axis=0, keepdims=True) + NORM_EPS)
            for half in range(2):
                h = 2 * p + half
                og = jax.nn.sigmoid(pf[rows, B_MO + h * ML_DV:B_MO + (h + 1) * ML_DV])
                hn_h = hn[:, half * BLK:(half + 1) * BLK].T
                mix_ref[0, rows, hsl[h]] = (hn_h * mlg[:, hsl[h]] * og).astype(mix_ref.dtype)
            ct[p] = a_p[p] * c_old[p] + upd[p]
            msc[p:p + 1, :] = m_new[p]

        o_parts = [None] * SWA_HEADS
        for p in range(pairs):
            on = o2[p] / den_p[p]
            o_parts[p] = on[0:SWA_HD, 0:BLK]
            o_parts[p + pairs] = on[SWA_HD:2 * SWA_HD, BLK:2 * BLK]
        o_t = jnp.concatenate(o_parts, axis=0)
        mix_ref[0, rows, ML_WIDTH:ML_WIDTH + SWA_WIDTH] = o_t.T.astype(mix_ref.dtype)
        yield

    blocks = [block(j) for j in range(nblk)]
    for g in blocks:
        next(g)
    for g in blocks:
        next(g)

    kband[0:BLK, :] = kband[tm:tm + BLK, :]
    sv_t[0] = sv_t[nblk]

    @pl.when(t == nt - 1)
    def _write_state():
        for h in range(ML_HEADS):
            p, lanes = h // 2, slice((h % 2) * ML_DK, (h % 2 + 1) * ML_DK)
            c_out[0, h] = ct[p, 0:ML_DV, lanes].T
            n_out[0, h:h + 1, :] = ct[p, ML_DV:ML_DV + 1, lanes]
        m_out[0] = msc[...]
        ko_ref[0] = pf[tm - BLK:tm, B_SK:B_SK + BLK]
        vo_ref[0] = vlast[...].T


def _prompt_mixer(x, w_in_l, gbc, mlg, sinks, rb, wo32, wfi32, wfo32):
    bsz, seq, _ = x.shape
    tm = TM_PROMPT
    nt = seq // tm
    nblk = tm // BLK
    steps = bsz * nt
    const2 = lambda b, t: (0, 0)
    smem = pl.BlockSpec(memory_space=pltpu.SMEM)

    def slab_spec(wmat):
        return pl.BlockSpec((wmat.shape[0] // steps, wmat.shape[1]), lambda b, t: (b * nt + t, 0))

    return pl.pallas_call(
        _prompt_mixer_kernel,
        grid=(bsz, nt),
        in_specs=[
            pl.BlockSpec((1, tm, D_MODEL), lambda b, t: (b, t, 0)),
            pl.BlockSpec((D_MODEL, W_IN_PAD), const2, pipeline_mode=pl.Buffered(1)),
            pl.BlockSpec((T_GROWS, BLK), const2),
            pl.BlockSpec((1, ML_WIDTH), const2),
            smem, smem,
            slab_spec(wo32), slab_spec(wfi32), slab_spec(wfo32),
        ],
        out_specs=[
            pl.BlockSpec((1, tm, D_MODEL), lambda b, t: (b, t, 0)),
            pl.BlockSpec((1, ML_HEADS, ML_DK, ML_DV), lambda b, t: (b, 0, 0, 0)),
            pl.BlockSpec((1, ML_HEADS, ML_DK), lambda b, t: (b, 0, 0)),
            pl.BlockSpec((1, M_ROWS, 2 * BLK), lambda b, t: (b, 0, 0)),
            pl.BlockSpec((1, BLK, BLK), lambda b, t: (b, 0, 0)),
            pl.BlockSpec((1, BLK, BLK), lambda b, t: (b, 0, 0)),
            pl.BlockSpec((T_COLS, D_MODEL), const2),
            pl.BlockSpec((D_MODEL, N_B + B_SV), const2),
            slab_spec(wo32), slab_spec(wfi32), slab_spec(wfo32),
        ],
        out_shape=[
            jax.ShapeDtypeStruct((bsz, seq, D_MODEL), BF16),
            jax.ShapeDtypeStruct((bsz, ML_HEADS, ML_DK, ML_DV), F32),
            jax.ShapeDtypeStruct((bsz, ML_HEADS, ML_DK), F32),
            jax.ShapeDtypeStruct((bsz, M_ROWS, 2 * BLK), F32),
            jax.ShapeDtypeStruct((bsz, BLK, BLK), F32),
            jax.ShapeDtypeStruct((bsz, BLK, BLK), F32),
            jax.ShapeDtypeStruct((T_COLS, D_MODEL), BF16),
            jax.ShapeDtypeStruct((D_MODEL, N_B + B_SV), BF16),
            jax.ShapeDtypeStruct(wo32.shape, BF16),
            jax.ShapeDtypeStruct(wfi32.shape, BF16),
            jax.ShapeDtypeStruct(wfo32.shape, BF16),
        ],
        scratch_shapes=[
            pltpu.VMEM((tm, ML_HEADS * ML_DK), BF16),
            pltpu.VMEM((tm, B_SV), F32),
            pltpu.VMEM((BLK, BLK), F32),
            pltpu.VMEM((tm + BLK, BLK), BF16),
            pltpu.VMEM((nblk, ML_HEADS * ML_DK, BLK), BF16),
            pltpu.VMEM((nblk, ML_WIDTH, BLK), BF16),
            pltpu.VMEM((nblk, SWA_WIDTH, BLK), BF16),
            pltpu.VMEM((nblk + 1, SWA_KV_HEADS * SWA_HD, BLK), BF16),
            pltpu.VMEM((nblk, T_GROWS, BLK), F32),
            pltpu.VMEM((ML_HEADS // 2, STATE_ROWS, 2 * ML_DK), F32),
            pltpu.VMEM((M_ROWS, 2 * BLK), F32),
            pltpu.VMEM((2, SWA_HEADS // 2, BLK, 2 * BLK), F32),
        ],
        compiler_params=pltpu.CompilerParams(
            dimension_semantics=("arbitrary", "arbitrary"),
            vmem_limit_bytes=VMEM_LIMIT),
        name="prompt_mixer",
    )(x, w_in_l, gbc, mlg, sinks, rb, wo32, wfi32, wfo32)


def _roll_rows(x, shift):
    return pltpu.roll(x, shift % x.shape[0], 0)


def _sample_mixer_kernel(x_ref, wt_ref, wn_ref, gb_ref, mlg_ref, sinks_ref, rb_ref, mrep_ref,
                         c_ref, n_ref, kc_ref, vc_ref,
                         mix_ref, c_out, n_out, mo_ref, ko_ref, vo_ref,
                         pa, pf, tblc, tbln):
    step = pl.program_id(0)
    tdec = 4
    rows_step = mix_ref.shape[0]
    nslab = rows_step // SLAB
    nseq = SLAB // tdec
    qrows = SWA_HEADS * SLAB

    @pl.when(step == 0)
    def _project_all_rows():
        xb = x_ref[...].astype(BF16)
        pa[:, A_Q:A_K] = _dot_nt(xb, wt_ref[T_Q:T_Q + ML_HEADS * ML_DK, :])
        pa[:, A_K:A_V] = _dot(xb, wn_ref[:, N_K:N_K + ML_HEADS * ML_DK])
        pa[:, A_V:A_COLS] = _dot_nt(xb, wt_ref[T_V:T_V + ML_WIDTH + SWA_WIDTH, :])
        pf[:, B_MO:B_SV] = _dot(xb, wn_ref[:, N_B:N_B + B_SV])
        pf[:, B_SV:B_COLS] = _dot_nt(xb, wt_ref[T_SV:T_COLS, :])

    @pl.when(step == 0)
    def _build_bias_tables():
        r = lax.broadcasted_iota(jnp.int32, (qrows, BLK), 0)
        c = lax.broadcasted_iota(jnp.int32, (qrows, BLK), 1)
        hq = r // SLAB
        bl = (r % SLAB) // tdec
        tq = r % tdec
        d_cache = WINDOW + tq - c
        d_new = tq - (c % tdec)
        key_cache = _t5_bucket(d_cache) * SWA_HEADS + hq
        key_new = _t5_bucket(d_new) * SWA_HEADS + hq
        acc_c = jnp.zeros((qrows, BLK), F32)
        acc_n = jnp.zeros((qrows, BLK), F32)
        for k in range(REL_BUCKETS):
            for h in range(SWA_HEADS):
                val = rb_ref[k, h]
                acc_c = jnp.where(key_cache == k * SWA_HEADS + h, val, acc_c)
                acc_n = jnp.where(key_new == k * SWA_HEADS + h, val, acc_n)
        tblc[...] = jnp.where(c > tq, acc_c, NEG_INF)
        tbln[...] = jnp.where((c < SLAB) & ((c // tdec) == bl) & (d_new >= 0), acc_n, NEG_INF)

    gbias = gb_ref[...]
    mlg = mlg_ref[...]
    r16 = lax.broadcasted_iota(jnp.int32, (SLAB, BLK), 0)
    rr = r16 % tdec
    bl16 = r16 // tdec
    rq = lax.broadcasted_iota(jnp.int32, (qrows, BLK), 0)
    blq = (rq % SLAB) // tdec
    hq_col = rq[:, 0:1] // SLAB
    snk = jnp.zeros((qrows, 1), F32)
    for h in range(SWA_HEADS):
        snk = jnp.where(hq_col == h, sinks_ref[h], snk)
    first_group = rq[:, 0:SWA_HD] < SWA_GROUP * SLAB
    zq = jnp.zeros((SLAB, SWA_HD), F32)
    zeros_pad = jnp.zeros((BLK - SLAB, BLK), BF16)

    def seg_last(x):
        return jnp.where(rr == 3, x,
                         jnp.where(rr == 2, _roll_rows(x, -1),
                                   jnp.where(rr == 1, _roll_rows(x, -2), _roll_rows(x, -3))))

    def seg_max(x):
        m1 = jnp.maximum(x, jnp.where(rr % 2 == 0, _roll_rows(x, -1), _roll_rows(x, 1)))
        return jnp.maximum(m1, jnp.where(rr < 2, _roll_rows(m1, -2), _roll_rows(m1, 2)))

    def slab(si):
        rows = pl.ds(si * SLAB, SLAB)
        prow = pl.ds(pl.multiple_of(step * rows_step + si * SLAB, SLAB), SLAB)
        seq0 = si * nseq

        pre = _softcap(pf[prow, B_G:B_G + BLK] + gbias)
        logf = _log_sigmoid(pre)
        y = logf + jnp.where(rr >= 1, _roll_rows(logf, 1), 0.0)
        bcum = y + jnp.where(rr >= 2, _roll_rows(y, 2), 0.0)
        ig = pltpu.roll(pre, ML_HEADS, 1)
        m_prev = mrep_ref[rows, :]
        b_minus_i = bcum - ig
        log_d = [jnp.where(rr >= dl, bcum - _roll_rows(b_minus_i, dl), -jnp.inf) for dl in range(tdec)]
        rowmax = functools.reduce(jnp.maximum, log_d)
        inter = bcum + m_prev
        m_row = jnp.maximum(inter, rowmax)
        scl = jnp.exp(inter - m_row)
        dm = [jnp.exp(ld - m_row) for ld in log_d]
        enm = jnp.exp(-m_row)
        b_last = seg_last(bcum)
        w = b_last - bcum + ig
        m_new = jnp.maximum(b_last + m_prev, seg_max(w))
        a = jnp.exp(b_last + m_prev - m_new)
        kws = jnp.exp(w - m_new) * ML_SCALE
        mo_ref[rows, :] = m_new

        q = [pa[prow, A_Q + h * ML_DK:A_Q + (h + 1) * ML_DK] for h in range(ML_HEADS)]
        k = [pa[prow, A_K + h * ML_DK:A_K + (h + 1) * ML_DK] for h in range(ML_HEADS)]
        v = [pa[prow, A_V + h * ML_DV:A_V + (h + 1) * ML_DV] for h in range(ML_HEADS)]
        c_b = [[c_ref[seq0 + bl, h] for bl in range(nseq)] for h in range(ML_HEADS)]
        qc, upd, kw = [], [], []
        for h in range(ML_HEADS):
            qb = q[h].astype(BF16)
            res = [_dot(qb, jnp.concatenate([c_b[h][2 * i].astype(BF16), c_b[h][2 * i + 1].astype(BF16)], axis=1))
                   for i in range(nseq // 2)]
            parts = [res[bl // 2][:, (bl % 2) * ML_DV:(bl % 2 + 1) * ML_DV] for bl in range(nseq)]
            qc_h = parts[nseq - 1]
            for bl in range(nseq - 2, -1, -1):
                qc_h = jnp.where(bl16 == bl, parts[bl], qc_h)
            qc.append(qc_h)

        knew = pf[prow, B_SK:B_SK + BLK]
        vnew = pf[prow, B_SV:B_SV + BLK]
        qparts = []
        for hq in range(SWA_HEADS):
            qh = pa[prow, A_SQ + hq * SWA_HD:A_SQ + (hq + 1) * SWA_HD]
            qparts.append(jnp.concatenate([qh, zq] if hq < SWA_GROUP else [zq, qh], axis=1))
        qs = jnp.concatenate(qparts, axis=0).astype(BF16)
        s_c = _dot_nt(qs, kc_ref[seq0 + nseq - 1].astype(BF16))
        for bl in range(nseq - 2, -1, -1):
            s_c = jnp.where(blq == bl, _dot_nt(qs, kc_ref[seq0 + bl].astype(BF16)), s_c)
        kn = jnp.concatenate([knew.astype(BF16), zeros_pad], axis=0)
        s_n = _dot_nt(qs, kn)

        yield
        for h in range(ML_HEADS):
            lane = ML_HEADS + h
            kw_h = k[h] * kws[:, lane:lane + 1]
            kw.append(kw_h)
            vm = jnp.concatenate([jnp.where(bl16 == bl, v[h], 0.0) for bl in range(nseq)], axis=1)
            upd.append(_dot_tn(kw_h.astype(BF16), vm.astype(BF16)))

        yield
        for h in range(ML_HEADS):
            hs = slice(h * ML_DK, (h + 1) * ML_DK)
            lane = ML_HEADS + h
            bm = jnp.zeros((SLAB, ML_DV), F32)
            sum_s = jnp.zeros((SLAB, 1), F32)
            for dl in range(tdec):
                s_dl = jnp.sum(q[h] * _roll_rows(k[h], dl), axis=-1, keepdims=True)
                ws = s_dl * ML_SCALE * dm[dl][:, lane:lane + 1]
                bm = bm + ws * _roll_rows(v[h], dl)
                sum_s = sum_s + ws
            nsel = jnp.zeros((SLAB, ML_DK), F32)
            for bl in range(nseq):
                msk = bl16 == bl
                n_b = n_ref[seq0 + bl][h:h + 1, :]
                nsel = jnp.where(msk, n_b, nsel)
                a_b = a[bl * tdec:bl * tdec + 1, lane:lane + 1]
                c_out[seq0 + bl, h] = a_b * c_b[h][bl] + upd[h][:, bl * ML_DV:(bl + 1) * ML_DV]
                n_out[seq0 + bl, h:h + 1, :] = a_b * n_b + jnp.sum(jnp.where(msk, kw[h], 0.0), axis=0,
                                                                   keepdims=True)
            sclh = scl[:, lane:lane + 1]
            num = sclh * qc[h] + bm
            den = sclh * jnp.sum(q[h] * nsel, axis=-1, keepdims=True) + sum_s
            hh = num / jnp.maximum(jnp.abs(den), enm[:, lane:lane + 1])
            hn = hh * lax.rsqrt(jnp.mean(hh * hh, axis=-1, keepdims=True) + NORM_EPS)
            og = jax.nn.sigmoid(pf[prow, B_MO + h * ML_DV:B_MO + (h + 1) * ML_DV])
            mix_ref[rows, hs] = (hn * mlg[:, hs] * og).astype(mix_ref.dtype)

        s_c = s_c * SWA_SCALE + tblc[...]
        s_n = s_n * SWA_SCALE + tbln[...]
        mx = jnp.maximum(jnp.maximum(jnp.max(s_c, axis=-1, keepdims=True),
                                     jnp.max(s_n, axis=-1, keepdims=True)), snk)
        p_c = jnp.exp(s_c - mx)
        p_n = jnp.exp(s_n - mx)
        den = (jnp.sum(p_c, axis=-1, keepdims=True) + jnp.sum(p_n, axis=-1, keepdims=True)
               + jnp.exp(snk - mx))

        yield
        vn = jnp.concatenate([vnew.astype(BF16), zeros_pad], axis=0)
        o = _dot(p_n.astype(BF16), vn)
        for bl in range(nseq):
            o = o + _dot(jnp.where(blq == bl, p_c, 0.0).astype(BF16), vc_ref[seq0 + bl].astype(BF16))
        o = o / den
        osel = jnp.where(first_group, o[:, 0:SWA_HD], o[:, SWA_HD:2 * SWA_HD])
        og = jnp.concatenate([osel[hq * SLAB:(hq + 1) * SLAB, :] for hq in range(SWA_HEADS)], axis=1)
        mix_ref[rows, ML_WIDTH:ML_WIDTH + SWA_WIDTH] = og.astype(mix_ref.dtype)

        for bl in range(nseq):
            ko_ref[seq0 + bl, 0:WINDOW - tdec, :] = kc_ref[seq0 + bl, tdec:WINDOW, :]
            vo_ref[seq0 + bl, 0:WINDOW - tdec, :] = vc_ref[seq0 + bl, tdec:WINDOW, :]
            ko_ref[seq0 + bl, WINDOW - tdec:WINDOW, :] = knew[bl * tdec:(bl + 1) * tdec, :]
            vo_ref[seq0 + bl, WINDOW - tdec:WINDOW, :] = vnew[bl * tdec:(bl + 1) * tdec, :]
        yield

    slabs = [slab(si) for si in range(nslab)]
    for _ in range(4):
        for s in slabs:
            next(s)


def _sample_mixer(x2d, wt, wn, gb, mlg, sinks, rb, mrep, c0, n0, kc, vc):
    rows = x2d.shape[0]
    tdec = 4
    nseq = rows // tdec
    gbs = SAMPLE_GB
    steps = nseq // gbs
    rb_rows = gbs * tdec
    const2 = lambda i: (0, 0)
    smem = pl.BlockSpec(memory_space=pltpu.SMEM)
    return pl.pallas_call(
        _sample_mixer_kernel,
        grid=(steps,),
        in_specs=[
            pl.BlockSpec((rows, D_MODEL), const2, pipeline_mode=pl.Buffered(1)),
            pl.BlockSpec((T_COLS, D_MODEL), const2, pipeline_mode=pl.Buffered(1)),
            pl.BlockSpec((D_MODEL, N_B + B_SV), const2, pipeline_mode=pl.Buffered(1)),
            pl.BlockSpec((1, BLK), const2),
            pl.BlockSpec((1, ML_WIDTH), const2),
            smem, smem,
            pl.BlockSpec((rb_rows, BLK), lambda i: (i, 0)),
            pl.BlockSpec((gbs, ML_HEADS, ML_DK, ML_DV), lambda i: (i, 0, 0, 0)),
            pl.BlockSpec((gbs, ML_HEADS, ML_DK), lambda i: (i, 0, 0)),
            pl.BlockSpec((gbs, WINDOW, BLK), lambda i: (i, 0, 0)),
            pl.BlockSpec((gbs, WINDOW, BLK), lambda i: (i, 0, 0)),
        ],
        out_specs=[
            pl.BlockSpec((rb_rows, D_MODEL), lambda i: (i, 0)),
            pl.BlockSpec((gbs, ML_HEADS, ML_DK, ML_DV), lambda i: (i, 0, 0, 0)),
            pl.BlockSpec((gbs, ML_HEADS, ML_DK), lambda i: (i, 0, 0)),
            pl.BlockSpec((rb_rows, BLK), lambda i: (i, 0)),
            pl.BlockSpec((gbs, WINDOW, BLK), lambda i: (i, 0, 0)),
            pl.BlockSpec((gbs, WINDOW, BLK), lambda i: (i, 0, 0)),
        ],
        out_shape=[
            jax.ShapeDtypeStruct((rows, D_MODEL), BF16),
            jax.ShapeDtypeStruct((nseq, ML_HEADS, ML_DK, ML_DV), F32),
            jax.ShapeDtypeStruct((nseq, ML_HEADS, ML_DK), F32),
            jax.ShapeDtypeStruct((rows, BLK), F32),
            jax.ShapeDtypeStruct((nseq, WINDOW, BLK), F32),
            jax.ShapeDtypeStruct((nseq, WINDOW, BLK), F32),
        ],
        scratch_shapes=[
            pltpu.VMEM((rows, A_COLS), F32),
            pltpu.VMEM((rows, B_COLS), F32),
            pltpu.VMEM((SWA_HEADS * SLAB, BLK), F32),
            pltpu.VMEM((SWA_HEADS * SLAB, BLK), F32),
        ],
        compiler_params=pltpu.CompilerParams(
            dimension_semantics=("arbitrary",),
            vmem_limit_bytes=VMEM_LIMIT),
        name="sample_mixer",
    )(x2d, wt, wn, gb, mlg, sinks, rb, mrep, c0, n0, kc, vc)


def _dense_kernel(x_ref, mix_ref, xs_ref, mixs_ref, wo_ref, g1_ref, b1_ref, wfi_ref, wfo_ref, g2_ref, b2_ref,
                  y_ref, ys_ref):
    bounds = [b for b in DENSE_SPLITS if b <= x_ref.shape[0]]
    nsub = len(bounds) - 1
    subs = [slice(bounds[i], bounds[i + 1]) for i in range(nsub)]
    chunks = list(zip(FF_SPLITS[:-1], FF_SPLITS[1:]))
    g1, b1, g2, b2 = g1_ref[...], b1_ref[...], g2_ref[...], b2_ref[...]

    xin = [x_ref[s, :] for s in subs]
    mixin = [mix_ref[s, :] for s in subs]
    xin[-1] = jnp.concatenate([xin[-1], xs_ref[...]], axis=0)
    mixin[-1] = jnp.concatenate([mixin[-1], mixs_ref[...]], axis=0)

    proj = [_dot(mixin[i], wo_ref[...]) for i in range(nsub)]
    h1 = [_layer_norm(ALPHA * xin[i] + proj[i], g1, b1) for i in range(nsub)]
    h1b = [h.astype(BF16) for h in h1]
    acc = [None] * nsub
    for c0, c1 in chunks:
        gate, up = [], []
        for i in range(nsub):
            gate.append(_dot(h1b[i], wfi_ref[:, c0:c1]))
            up.append(_dot(h1b[i], wfi_ref[:, D_FF + c0:D_FF + c1]))
        for i in range(nsub):
            act = (gate[i] * jax.nn.sigmoid(gate[i]) * up[i]).astype(BF16)
            part = _dot(act, wfo_ref[c0:c1, :])
            acc[i] = part if acc[i] is None else acc[i] + part
    for i, s in enumerate(subs):
        out = _layer_norm(ALPHA * h1[i] + acc[i], g2, b2)
        nrows = s.stop - s.start
        y_ref[s, :] = out[:nrows]
        if i == nsub - 1:
            ys_ref[...] = out[nrows:].reshape(ys_ref.shape)


def _dense(x2d, mix2d, xs2d, mixs2d, tdec, wo, g1, b1, wfi, wfo, g2, b2):
    rows = x2d.shape[0]
    tm = TM_DENSE
    steps = rows // tm
    srows = xs2d.shape[0] // steps
    const2 = lambda i: (0, 0)
    step2 = lambda i: (i, 0)

    def wspec(shape):
        return pl.BlockSpec(shape, const2, pipeline_mode=pl.Buffered(1))

    return pl.pallas_call(
        _dense_kernel,
        grid=(steps,),
        in_specs=[
            pl.BlockSpec((tm, D_MODEL), step2),
            pl.BlockSpec((tm, D_MODEL), step2),
            pl.BlockSpec((srows, D_MODEL), step2),
            pl.BlockSpec((srows, D_MODEL), step2),
            wspec((D_MODEL, D_MODEL)),
            wspec((1, D_MODEL)), wspec((1, D_MODEL)),
            wspec((D_MODEL, 2 * D_FF)),
            wspec((D_FF, D_MODEL)),
            wspec((1, D_MODEL)), wspec((1, D_MODEL)),
        ],
        out_specs=[pl.BlockSpec((tm, D_MODEL), step2),
                   pl.BlockSpec((srows // tdec, tdec, D_MODEL), lambda i: (i, 0, 0))],
        out_shape=[jax.ShapeDtypeStruct((rows, D_MODEL), F32),
                   jax.ShapeDtypeStruct((xs2d.shape[0] // tdec, tdec, D_MODEL), F32)],
        compiler_params=pltpu.CompilerParams(
            dimension_semantics=("arbitrary",),
            vmem_limit_bytes=VMEM_LIMIT),
        name="dense",
    )(x2d, mix2d, xs2d, mixs2d, wo, g1, b1, wfi, wfo, g2, b2)


def kernel(x_prompt, x_sample, state_mlstm_C, state_mlstm_n, state_mlstm_m, cache_swa_k, cache_swa_v,
           w_in, b_igate, b_fgate, ml_norm_g, swa_sinks, rel_bias, w_out, ln1_g, ln1_b,
           w_ffn_in, w_ffn_out, ln2_g, ln2_b):
    bp, seq, _ = x_prompt.shape
    bs, tdec, _ = x_sample.shape
    l = 0

    w16 = jnp.pad(w_in[l], ((0, 0), (0, W_IN_PAD - w_in.shape[-1]))).astype(BF16)
    gvec = jnp.concatenate([b_igate[l], b_fgate[l]]).astype(F32)
    gb = jnp.pad(gvec, (0, BLK - 2 * ML_HEADS))[None, :]
    gbc = jnp.broadcast_to(jnp.pad(gvec, (0, T_GROWS - 2 * ML_HEADS))[:, None], (T_GROWS, BLK))
    mlg = ml_norm_g[l][None, :].astype(F32)
    sinks = swa_sinks[l].astype(F32)
    rb = rel_bias.astype(F32)
    g1, b1 = ln1_g[l][None, :], ln1_b[l][None, :]
    g2, b2 = ln2_g[l][None, :], ln2_b[l][None, :]

    mix_p, c_p, n_p, m_p, k_p, v_p, wt, wn, wo, wfi, wfo = _prompt_mixer(
        x_prompt, w16, gbc, mlg, sinks, rb, w_out[l], w_ffn_in[l], w_ffn_out[l])
    p_m =m_p[:, :ML_HEADS // 2, ::BLK].reshape(bp, ML_HEADS)[None]
    p_k = k_p.reshape(1, bp, WINDOW, SWA_KV_HEADS, SWA_HD)
    p_v = v_p.reshape(1, bp, WINDOW, SWA_KV_HEADS, SWA_HD)

    xs = x_sample.reshape(bs * tdec, D_MODEL)
    m0 = state_mlstm_m[l].astype(F32)
    mrep = jnp.pad(jnp.repeat(m0, tdec, axis=0), ((0, 0), (ML_HEADS, BLK - 2 * ML_HEADS)))
    wlen = cache_swa_k.shape[2]
    kc = cache_swa_k[l].reshape(bs, wlen, SWA_KV_HEADS * SWA_HD)
    vc = cache_swa_v[l].reshape(bs, wlen, SWA_KV_HEADS * SWA_HD)
    mix_s, c_s, n_s, mo_s, k_s, v_s = _sample_mixer(
        xs, wt, wn, gb, mlg, sinks, rb, mrep,
        state_mlstm_C[l].astype(F32), state_mlstm_n[l].astype(F32), kc, vc)
    s_m =mo_s.reshape(bs, tdec, BLK)[:, 0, ML_HEADS:2 * ML_HEADS][None]
    s_k = k_s.reshape(1, bs, wlen, SWA_KV_HEADS, SWA_HD)
    s_v = v_s.reshape(1, bs, wlen, SWA_KV_HEADS, SWA_HD)

    y_p, y_s = _dense(x_prompt.reshape(bp * seq, D_MODEL), mix_p.reshape(bp * seq, D_MODEL), xs, mix_s, tdec,
                      wo, g1, b1, wfi, wfo, g2, b2)
    y_p = y_p.reshape(bp, seq, D_MODEL)

    return (y_p, y_s, c_p[None], n_p[None], p_m, p_k, p_v,
            c_s[None], n_s[None], s_m, s_k, s_v)
```

```python
import functools
import math

import jax
import jax.numpy as jnp
from jax import lax
from jax.experimental import pallas as pl
from jax.experimental.pallas import tpu as pltpu

F32 = jnp.float32
BF16 = jnp.bfloat16

D_MODEL = 1024
ML_HEADS = 4
ML_DK = 128
ML_DV = 128
ML_WIDTH = ML_HEADS * ML_DV
GATE_SOFTCAP = 15.0
SWA_HEADS = 8
SWA_KV_HEADS = 2
SWA_GROUP = SWA_HEADS // SWA_KV_HEADS
SWA_HD = 64
SWA_WIDTH = SWA_HEADS * SWA_HD
WINDOW = 128
REL_BUCKETS = 32
REL_MAX_DIST = 128
D_FF = 2816
DEPTH = 1
ALPHA = (2.0 * DEPTH) ** 0.25
LN_EPS = 1e-5
NORM_EPS = 1e-6
NEG_INF = -1e30
IN_SIZES = (512, 512, 512, 512, 4, 4, 512, 128, 128)
ML_SCALE = ML_DK ** -0.5
LOG_ML_SCALE = math.log(ML_SCALE)
SWA_SCALE = SWA_HD ** -0.5

BLK = 128
SUBLANES_F32 = 8
SUBLANES_BF16 = 16

A_Q, A_K, A_V, A_SQ = 0, 512, 1024, 1536
A_COLS = 2048
B_MO, B_SK, B_SV, B_G = 0, 512, 640, 768
B_COLS = 896

N_K = 0
N_B = 512
T_Q, T_V, T_SQ, T_SV, T_G = 0, 512, 1024, 1536, 1664
T_GROWS = SUBLANES_BF16
T_COLS = T_G + BLK
W_IN_PAD = -(-sum(IN_SIZES) // BLK) * BLK
STATE_ROWS = ML_DV + SUBLANES_BF16
M_ROWS = SUBLANES_F32

TM_PROMPT = 1024
TM_DENSE = 1024
DENSE_SPLITS = (0, 384, 768, 1024)
FF_SPLITS = (0, 1536, D_FF)
SAMPLE_GB = 16
SLAB = 16
VMEM_LIMIT = 56 * 1024 * 1024


def _softcap(a):
    return GATE_SOFTCAP * jnp.tanh(a / GATE_SOFTCAP)


def _log_sigmoid(x):
    return jnp.minimum(x, 0.0) - jnp.log1p(jnp.exp(-jnp.abs(x)))


def _layer_norm(z, g, b):
    mu = jnp.mean(z, axis=-1, keepdims=True)
    zc = z - mu
    var = jnp.mean(zc * zc, axis=-1, keepdims=True)
    return zc * lax.rsqrt(var + LN_EPS) * g + b


def _dot(a, b):
    return jnp.dot(a, b, preferred_element_type=F32)


def _dot_nt(a, b):
    return lax.dot_general(a, b, (((1,), (1,)), ((), ())), preferred_element_type=F32)


def _dot_tn(a, b):
    return lax.dot_general(a, b, (((0,), (0,)), ((), ())), preferred_element_type=F32)


def _split3(x):
    hi = x.astype(BF16)
    r1 = x - hi.astype(F32)
    mid = r1.astype(BF16)
    lo = (r1 - mid.astype(F32)).astype(BF16)
    return hi, mid, lo


def _t5_bucket(d):
    n = jnp.maximum(d, 0)
    max_exact = REL_BUCKETS // 2
    nlog = REL_BUCKETS - max_exact
    large = jnp.full(n.shape, max_exact, jnp.int32)
    for k in range(1, nlog):
        thr = math.ceil(max_exact * (REL_MAX_DIST / max_exact) ** (k / nlog))
        large = large + jnp.where(n >= thr, 1, 0)
    return jnp.where(n < max_exact, n, large)


def _prompt_mixer_kernel(x_ref, w_ref, gbc_ref, mlg_ref, sinks_ref, rb_ref, wo32, wfi32, wfo32,
                         mix_ref, c_out, n_out, m_out, ko_ref, vo_ref,
                         wt, wn, wo16, wfi16, wfo16,
                         kn, pf, vlast, kband, q_t, v_t, sq_t, sv_t, g_t, ct, msc, tbl):
    wo16[...] = wo32[...].astype(BF16)
    wfi16[...] = wfi32[...].astype(BF16)
    wfo16[...] = wfo32[...].astype(BF16)

    b = pl.program_id(0)
    t = pl.program_id(1)
    nt = pl.num_programs(1)
    tm = x_ref.shape[1]
    nblk = tm // BLK
    pairs = SWA_HEADS // 2

    @pl.when((b == 0) & (t == 0))
    def _prepare_weights():
        offs = [0]
        for n in IN_SIZES:
            offs.append(offs[-1] + n)
        o_mq, o_mk, o_mv, o_mo, o_mi, _, o_sq, o_sk, o_sv = offs[:-1]

        def col_blocks(dst_col, src_col, n):
            for i in range(n // BLK):
                wn[:, dst_col + i * BLK:dst_col + (i + 1) * BLK] = (
                    w_ref[:, src_col + i * BLK:src_col + (i + 1) * BLK])

        def row_blocks(dst_row, src_col, n):
            for i in range(n // BLK):
                wt[dst_row + i * BLK:dst_row + (i + 1) * BLK, :] = (
                    w_ref[:, src_col + i * BLK:src_col + (i + 1) * BLK].T)

        col_blocks(N_K, o_mk, ML_HEADS * ML_DK)
        col_blocks(N_B + B_MO, o_mo, ML_WIDTH)
        col_blocks(N_B + B_SK, o_sk, SWA_KV_HEADS * SWA_HD)
        row_blocks(T_Q, o_mq, ML_HEADS * ML_DK)
        row_blocks(T_V, o_mv, ML_WIDTH)
        row_blocks(T_SQ, o_sq, SWA_WIDTH)
        row_blocks(T_SV, o_sv, SWA_KV_HEADS * SWA_HD)
        gates_t = w_ref[:, o_mi:o_mi + BLK].T
        grow = lax.broadcasted_iota(jnp.int32, gates_t.shape, 0)
        wt[T_G:T_G + BLK, :] = jnp.where(grow < 2 * ML_HEADS, gates_t, jnp.zeros_like(gates_t))

    @pl.when((b == 0) & (t == 0))
    def _build_bias_tables():
        r = lax.broadcasted_iota(jnp.int32, (BLK, 2 * BLK), 0)
        ln = lax.broadcasted_iota(jnp.int32, (BLK, 2 * BLK), 1)
        second = ln >= BLK
        qi = jnp.where(second, ln - BLK, ln)
        prev = r > qi
        d = jnp.where(prev, WINDOW + qi - r, qi - r)
        bucket = _t5_bucket(d)
        for p in range(pairs):
            acc = jnp.zeros((BLK, 2 * BLK), F32)
            for k in range(REL_BUCKETS):
                acc = jnp.where(bucket == k, jnp.where(second, rb_ref[k, p + pairs], rb_ref[k, p]), acc)
            tbl[0, p] = jnp.where(prev, NEG_INF, acc)
            tbl[1, p] = acc

    @pl.when(t == 0)
    def _reset_state():
        ct[...] = jnp.zeros_like(ct)
        msc[...] = jnp.zeros_like(msc)
        kband[0:BLK, :] = jnp.zeros((BLK, kband.shape[1]), kband.dtype)
        sv_t[0] = jnp.zeros(sv_t.shape[1:], sv_t.dtype)

    xb = x_ref[0].astype(BF16)
    res_n = _dot(xb, wn[...])
    kn[...] = res_n[:, N_K:N_K + ML_HEADS * ML_DK].astype(BF16)
    pf[...] = res_n[:, N_B:N_B + B_SV]
    kband[BLK:BLK + tm, :] = pf[:, B_SK:B_SK + BLK].astype(BF16)

    res_t = _dot_nt(wt[0:T_G + T_GROWS, :], xb)
    vlast[...] = res_t[T_SV:T_SV + BLK, tm - BLK:tm]

    def put(dst, r0, nrows, off, scale=None):
        res = res_t[r0:r0 + nrows, :]
        if scale is not None:
            res = res * scale
        for jj in range(nblk):
            dst[jj + off] = res[:, jj * BLK:(jj + 1) * BLK].astype(dst.dtype)

    put(q_t, T_Q, ML_HEADS * ML_DK, 0)
    put(v_t, T_V, ML_WIDTH, 0)
    put(sq_t, T_SQ, SWA_WIDTH, 0, SWA_SCALE)
    put(sv_t, T_SV, SWA_KV_HEADS * SWA_HD, 1)
    put(g_t, T_G, T_GROWS, 0)

    row = lax.broadcasted_iota(jnp.int32, (BLK, BLK), 0)
    col = lax.broadcasted_iota(jnp.int32, (BLK, BLK), 1)
    causal_t = row <= col
    triu = causal_t.astype(BF16)
    causal2 = jnp.concatenate([causal_t, causal_t], axis=1)
    r16 = lax.broadcasted_iota(jnp.int32, (STATE_ROWS - ML_DV, 2 * BLK), 0)
    ones_rows = (r16 == 0).astype(BF16)
    z128 = jnp.zeros((BLK, BLK), BF16)
    ml_pairs = ML_HEADS // 2
    zero_rows = jnp.zeros((BLK - T_GROWS, BLK), F32)
    gbc = gbc_ref[...]
    mlg = mlg_ref[...]
    lane2 = lax.broadcasted_iota(jnp.int32, (1, 2 * BLK), 1)
    snk_rows = [jnp.where(lane2 >= BLK, sinks_ref[p + pairs], sinks_ref[p]) for p in range(pairs)]
    zq = jnp.zeros((SWA_HD, BLK), BF16)
    prev2 = jnp.concatenate([row > col, row > col], axis=1)

    def pair_row(x0, x1):
        return jnp.concatenate([jnp.broadcast_to(x0, (1, BLK)), jnp.broadcast_to(x1, (1, BLK))], axis=1)

    def block_diag(x0, x1):
        return jnp.concatenate([jnp.concatenate([x0, z128], axis=1),
                                jnp.concatenate([z128, x1], axis=1)], axis=0)

    def block(j):
        r0 = j * BLK
        rows = pl.ds(r0, BLK)
        hsl = [slice(h * ML_DK, (h + 1) * ML_DK) for h in range(ML_HEADS)]
        psl = [slice(p * 2 * ML_DK, (p + 1) * 2 * ML_DK) for p in range(ml_pairs)]

        pre_t = _softcap(g_t[j] + gbc)
        hi, mid, lo = _split3(_log_sigmoid(pre_t))
        b_t = _dot(hi, triu) + _dot(mid, triu) + _dot(lo, triu)
        u_t = pltpu.roll(pre_t, ML_HEADS, 0) - b_t
        u_c = jnp.concatenate([u_t, zero_rows], axis=0).T

        k2 = [kn[rows, psl[p]] for p in range(ml_pairs)]
        bdq = [block_diag(q_t[j, hsl[2 * p], :], q_t[j, hsl[2 * p + 1], :]) for p in range(ml_pairs)]
        vte = [jnp.concatenate([jnp.concatenate([v_t[j, hsl[2 * p], :], v_t[j, hsl[2 * p + 1], :]], axis=1),
                                ones_rows], axis=0) for p in range(ml_pairs)]
        kq = [_dot(k2[p], bdq[p]) for p in range(ml_pairs)]

        sel = jnp.where(t == 0, 0, 1) if j == 0 else 1
        kb = kband[pl.ds(r0, 2 * BLK), :]
        vtb = jnp.concatenate([sv_t[j], sv_t[j + 1]], axis=1)
        sc2 = []
        for p in range(pairs):
            qa = sq_t[j, p * SWA_HD:(p + 1) * SWA_HD, :]
            qb = sq_t[j, (p + pairs) * SWA_HD:(p + pairs + 1) * SWA_HD, :]
            bd = jnp.concatenate([jnp.concatenate([qa, zq], axis=1),
                                  jnp.concatenate([zq, qb], axis=1)], axis=0)
            sc2.append(_dot(kb, bd))

        yield
        bds, scl, m_row, m_new, a_p, vwte = [], [], [], [], [], []
        for p in range(ml_pairs):
            h0, h1 = 2 * p, 2 * p + 1
            b_row = pair_row(b_t[ML_HEADS + h0:ML_HEADS + h0 + 1, :], b_t[ML_HEADS + h1:ML_HEADS + h1 + 1, :])
            i_row = pair_row(pre_t[h0:h0 + 1, :], pre_t[h1:h1 + 1, :])
            u_col = jnp.concatenate(
                [jnp.broadcast_to(u_c[:, ML_HEADS + h0:ML_HEADS + h0 + 1], (BLK, BLK)),
                 jnp.broadcast_to(u_c[:, ML_HEADS + h1:ML_HEADS + h1 + 1], (BLK, BLK))], axis=1)
            m_prev = msc[p:p + 1, :]
            log_d = jnp.where(causal2, b_row + u_col, -jnp.inf)
            inter = b_row + m_prev
            mr = jnp.maximum(inter, jnp.max(log_d, axis=0, keepdims=True))
            m_row.append(mr)
            scl.append(jnp.exp(inter - mr))
            s2 = (kq[p] * jnp.exp(log_d - (mr - LOG_ML_SCALE))).astype(BF16)
            bds.append(block_diag(s2[:, :BLK], s2[:, BLK:]))
            b_last = pair_row(b_row[:, BLK - 1:BLK], b_row[:, 2 * BLK - 1:2 * BLK])
            w_row = b_last - b_row + i_row
            w_max = pair_row(jnp.max(w_row[:, :BLK], axis=1, keepdims=True),
                             jnp.max(w_row[:, BLK:], axis=1, keepdims=True))
            mn = jnp.maximum(b_last + m_prev, w_max)
            m_new.append(mn)
            a_p.append(jnp.exp(b_last + m_prev - mn))
            wexp = jnp.exp(w_row - mn) * ML_SCALE
            vwte.append(vte[p] * wexp.astype(BF16))

        pw, den_p = [], []
        for p in range(pairs):
            sc = jnp.where(prev2, sc2[p][:BLK], sc2[p][BLK:]) + tbl[sel, p]
            snk = snk_rows[p]
            mx = jnp.maximum(jnp.max(sc, axis=0, keepdims=True), snk)
            pe = jnp.exp(sc - mx)
            den_p.append(jnp.sum(pe, axis=0, keepdims=True) + jnp.exp(snk - mx))
            pw.append(jnp.concatenate([jnp.where(prev2, pe, 0.0), jnp.where(prev2, 0.0, pe)],
                                      axis=0).astype(BF16))

        c_old = [ct[p] for p in range(ml_pairs)]
        cq = [_dot(c_old[p].astype(BF16), bdq[p]) for p in range(ml_pairs)]
        vs = [_dot(vte[p], bds[p]) for p in range(ml_pairs)]
        upd = [_dot(vwte[p], block_diag(k2[p][:, :BLK], k2[p][:, BLK:])) for p in range(ml_pairs)]
        o2 = [_dot(vtb, pw[p]) for p in range(pairs)]

        for p in range(ml_pairs):
            nd = scl[p] * cq[p] + vs[p]
            num = nd[:ML_DV]
            den = nd[ML_DV:ML_DV + 1]
            hh = num / jnp.maximum(jnp.abs(den), jnp.exp(-m_row[p]))
            hn = hh * lax.rsqrt(jnp.mean(hh * hh, axis=0, keepdims=True) + NORM_EPS)
            for half in range(2):
                h = 2 * p + half
                og = jax.nn.sigmoid(pf[rows, B_MO + h * ML_DV:B_MO + (h + 1) * ML_DV])
                hn_h = hn[:, half * BLK:(half + 1) * BLK].T
                mix_ref[0, rows, hsl[h]] = (hn_h * mlg[:, hsl[h]] * og).astype(mix_ref.dtype)
            ct[p] = a_p[p] * c_old[p] + upd[p]
            msc[p:p + 1, :] = m_new[p]

        o_parts = [None] * SWA_HEADS
        for p in range(pairs):
            on = o2[p] / den_p[p]
            o_parts[p] = on[0:SWA_HD, 0:BLK]
            o_parts[p + pairs] = on[SWA_HD:2 * SWA_HD, BLK:2 * BLK]
        o_t = jnp.concatenate(o_parts, axis=0)
        mix_ref[0, rows, ML_WIDTH:ML_WIDTH + SWA_WIDTH] = o_t.T.astype(mix_ref.dtype)
        yield

    blocks = [block(j) for j in range(nblk)]
    for g in blocks:
        next(g)
    for g in blocks:
        next(g)

    kband[0:BLK, :] = kband[tm:tm + BLK, :]
    sv_t[0] = sv_t[nblk]

    @pl.when(t == nt - 1)
    def _write_state():
        for h in range(ML_HEADS):
            p, lanes = h // 2, slice((h % 2) * ML_DK, (h % 2 + 1) * ML_DK)
            c_out[0, h] = ct[p, 0:ML_DV, lanes].T
            n_out[0, h:h + 1, :] = ct[p, ML_DV:ML_DV + 1, lanes]
        m_out[0] = msc[...]
        ko_ref[0] = pf[tm - BLK:tm, B_SK:B_SK + BLK]
        vo_ref[0] = vlast[...].T


def _prompt_mixer(x, w_in_l, gbc, mlg, sinks, rb, wo32, wfi32, wfo32):
    bsz, seq, _ = x.shape
    tm = TM_PROMPT
    nt = seq // tm
    nblk = tm // BLK
    steps = bsz * nt
    const2 = lambda b, t: (0, 0)
    smem = pl.BlockSpec(memory_space=pltpu.SMEM)

    def slab_spec(wmat):
        return pl.BlockSpec((wmat.shape[0] // steps, wmat.shape[1]), lambda b, t: (b * nt + t, 0))

    return pl.pallas_call(
        _prompt_mixer_kernel,
        grid=(bsz, nt),
        in_specs=[
            pl.BlockSpec((1, tm, D_MODEL), lambda b, t: (b, t, 0)),
            pl.BlockSpec((D_MODEL, W_IN_PAD), const2, pipeline_mode=pl.Buffered(1)),
            pl.BlockSpec((T_GROWS, BLK), const2),
            pl.BlockSpec((1, ML_WIDTH), const2),
            smem, smem,
            slab_spec(wo32), slab_spec(wfi32), slab_spec(wfo32),
        ],
        out_specs=[
            pl.BlockSpec((1, tm, D_MODEL), lambda b, t: (b, t, 0)),
            pl.BlockSpec((1, ML_HEADS, ML_DK, ML_DV), lambda b, t: (b, 0, 0, 0)),
            pl.BlockSpec((1, ML_HEADS, ML_DK), lambda b, t: (b, 0, 0)),
            pl.BlockSpec((1, M_ROWS, 2 * BLK), lambda b, t: (b, 0, 0)),
            pl.BlockSpec((1, BLK, BLK), lambda b, t: (b, 0, 0)),
            pl.BlockSpec((1, BLK, BLK), lambda b, t: (b, 0, 0)),
            pl.BlockSpec((T_COLS, D_MODEL), const2),
            pl.BlockSpec((D_MODEL, N_B + B_SV), const2),
            slab_spec(wo32), slab_spec(wfi32), slab_spec(wfo32),
        ],
        out_shape=[
            jax.ShapeDtypeStruct((bsz, seq, D_MODEL), BF16),
            jax.ShapeDtypeStruct((bsz, ML_HEADS, ML_DK, ML_DV), F32),
            jax.ShapeDtypeStruct((bsz, ML_HEADS, ML_DK), F32),
            jax.ShapeDtypeStruct((bsz, M_ROWS, 2 * BLK), F32),
            jax.ShapeDtypeStruct((bsz, BLK, BLK), F32),
            jax.ShapeDtypeStruct((bsz, BLK, BLK), F32),
            jax.ShapeDtypeStruct((T_COLS, D_MODEL), BF16),
            jax.ShapeDtypeStruct((D_MODEL, N_B + B_SV), BF16),
            jax.ShapeDtypeStruct(wo32.shape, BF16),
            jax.ShapeDtypeStruct(wfi32.shape, BF16),
            jax.ShapeDtypeStruct(wfo32.shape, BF16),
        ],
        scratch_shapes=[
            pltpu.VMEM((tm, ML_HEADS * ML_DK), BF16),
            pltpu.VMEM((tm, B_SV), F32),
            pltpu.VMEM((BLK, BLK), F32),
            pltpu.VMEM((tm + BLK, BLK), BF16),
            pltpu.VMEM((nblk, ML_HEADS * ML_DK, BLK), BF16),
            pltpu.VMEM((nblk, ML_WIDTH, BLK), BF16),
            pltpu.VMEM((nblk, SWA_WIDTH, BLK), BF16),
            pltpu.VMEM((nblk + 1, SWA_KV_HEADS * SWA_HD, BLK), BF16),
            pltpu.VMEM((nblk, T_GROWS, BLK), F32),
            pltpu.VMEM((ML_HEADS // 2, STATE_ROWS, 2 * ML_DK), F32),
            pltpu.VMEM((M_ROWS, 2 * BLK), F32),
            pltpu.VMEM((2, SWA_HEADS // 2, BLK, 2 * BLK), F32),
        ],
        compiler_params=pltpu.CompilerParams(
            dimension_semantics=("arbitrary", "arbitrary"),
            vmem_limit_bytes=VMEM_LIMIT),
        name="prompt_mixer",
    )(x, w_in_l, gbc, mlg, sinks, rb, wo32, wfi32, wfo32)


def _roll_rows(x, shift):
    return pltpu.roll(x, shift % x.shape[0], 0)


def _sample_mixer_kernel(x_ref, wt_ref, wn_ref, gb_ref, mlg_ref, sinks_ref, rb_ref, mrep_ref,
                         c_ref, n_ref, kc_ref, vc_ref,
                         mix_ref, c_out, n_out, mo_ref, ko_ref, vo_ref,
                         pa, pf, tblc, tbln):
    step = pl.program_id(0)
    tdec = 4
    rows_step = mix_ref.shape[0]
    nslab = rows_step // SLAB
    nseq = SLAB // tdec
    qrows = SWA_HEADS * SLAB

    @pl.when(step == 0)
    def _project_all_rows():
        xb = x_ref[...].reshape(pa.shape[0], D_MODEL).astype(BF16)
        pa[:, A_Q:A_K] = _dot_nt(xb, wt_ref[T_Q:T_Q + ML_HEADS * ML_DK, :])
        pa[:, A_K:A_V] = _dot(xb, wn_ref[:, N_K:N_K + ML_HEADS * ML_DK])
        pa[:, A_V:A_COLS] = _dot_nt(xb, wt_ref[T_V:T_V + ML_WIDTH + SWA_WIDTH, :])
        pf[:, B_MO:B_SV] = _dot(xb, wn_ref[:, N_B:N_B + B_SV])
        pf[:, B_SV:B_COLS] = _dot_nt(xb, wt_ref[T_SV:T_COLS, :])

    @pl.when(step == 0)
    def _build_bias_tables():
        r = lax.broadcasted_iota(jnp.int32, (qrows, BLK), 0)
        c = lax.broadcasted_iota(jnp.int32, (qrows, BLK), 1)
        hq = r // SLAB
        bl = (r % SLAB) // tdec
        tq = r % tdec
        d_cache = WINDOW + tq - c
        d_new = tq - (c % tdec)
        key_cache = _t5_bucket(d_cache) * SWA_HEADS + hq
        key_new = _t5_bucket(d_new) * SWA_HEADS + hq
        acc_c = jnp.zeros((qrows, BLK), F32)
        acc_n = jnp.zeros((qrows, BLK), F32)
        for k in range(REL_BUCKETS):
            for h in range(SWA_HEADS):
                val = rb_ref[k, h]
                acc_c = jnp.where(key_cache == k * SWA_HEADS + h, val, acc_c)
                acc_n = jnp.where(key_new == k * SWA_HEADS + h, val, acc_n)
        tblc[...] = jnp.where(c > tq, acc_c, NEG_INF)
        tbln[...] = jnp.where((c < SLAB) & ((c // tdec) == bl) & (d_new >= 0), acc_n, NEG_INF)

    gbias = gb_ref[...]
    mlg = mlg_ref[...]
    r16 = lax.broadcasted_iota(jnp.int32, (SLAB, BLK), 0)
    rr = r16 % tdec
    bl16 = r16 // tdec
    rq = lax.broadcasted_iota(jnp.int32, (qrows, BLK), 0)
    blq = (rq % SLAB) // tdec
    hq_col = rq[:, 0:1] // SLAB
    snk = jnp.zeros((qrows, 1), F32)
    for h in range(SWA_HEADS):
        snk = jnp.where(hq_col == h, sinks_ref[h], snk)
    first_group = rq[:, 0:SWA_HD] < SWA_GROUP * SLAB
    zq = jnp.zeros((SLAB, SWA_HD), F32)
    zeros_pad = jnp.zeros((BLK - SLAB, BLK), BF16)

    def seg_last(x):
        return jnp.where(rr == 3, x,
                         jnp.where(rr == 2, _roll_rows(x, -1),
                                   jnp.where(rr == 1, _roll_rows(x, -2), _roll_rows(x, -3))))

    def seg_max(x):
        m1 = jnp.maximum(x, jnp.where(rr % 2 == 0, _roll_rows(x, -1), _roll_rows(x, 1)))
        return jnp.maximum(m1, jnp.where(rr < 2, _roll_rows(m1, -2), _roll_rows(m1, 2)))

    def slab(si):
        rows = pl.ds(si * SLAB, SLAB)
        prow = pl.ds(pl.multiple_of(step * rows_step + si * SLAB, SLAB), SLAB)
        seq0 = si * nseq

        pre = _softcap(pf[prow, B_G:B_G + BLK] + gbias)
        logf = _log_sigmoid(pre)
        y = logf + jnp.where(rr >= 1, _roll_rows(logf, 1), 0.0)
        bcum = y + jnp.where(rr >= 2, _roll_rows(y, 2), 0.0)
        ig = pltpu.roll(pre, ML_HEADS, 1)
        m_prev = mrep_ref[rows, :]
        b_minus_i = bcum - ig
        log_d = [jnp.where(rr >= dl, bcum - _roll_rows(b_minus_i, dl), -jnp.inf) for dl in range(tdec)]
        rowmax = functools.reduce(jnp.maximum, log_d)
        inter = bcum + m_prev
        m_row = jnp.maximum(inter, rowmax)
        scl = jnp.exp(inter - m_row)
        dm = [jnp.exp(ld - m_row) for ld in log_d]
        enm = jnp.exp(-m_row)
        b_last = seg_last(bcum)
        w = b_last - bcum + ig
        m_new = jnp.maximum(b_last + m_prev, seg_max(w))
        a = jnp.exp(b_last + m_prev - m_new)
        kws = jnp.exp(w - m_new) * ML_SCALE
        mo_ref[rows, :] = m_new

        q = [pa[prow, A_Q + h * ML_DK:A_Q + (h + 1) * ML_DK] for h in range(ML_HEADS)]
        k = [pa[prow, A_K + h * ML_DK:A_K + (h + 1) * ML_DK] for h in range(ML_HEADS)]
        v = [pa[prow, A_V + h * ML_DV:A_V + (h + 1) * ML_DV] for h in range(ML_HEADS)]
        c_b = [[c_ref[seq0 + bl, h] for bl in range(nseq)] for h in range(ML_HEADS)]
        qc, upd, kw = [], [], []
        for h in range(ML_HEADS):
            qb = q[h].astype(BF16)
            res = [_dot(qb, jnp.concatenate([c_b[h][2 * i].astype(BF16), c_b[h][2 * i + 1].astype(BF16)], axis=1))
                   for i in range(nseq // 2)]
            parts = [res[bl // 2][:, (bl % 2) * ML_DV:(bl % 2 + 1) * ML_DV] for bl in range(nseq)]
            qc_h = parts[nseq - 1]
            for bl in range(nseq - 2, -1, -1):
                qc_h = jnp.where(bl16 == bl, parts[bl], qc_h)
            qc.append(qc_h)

        knew = pf[prow, B_SK:B_SK + BLK]
        vnew = pf[prow, B_SV:B_SV + BLK]
        qparts = []
        for hq in range(SWA_HEADS):
            qh = pa[prow, A_SQ + hq * SWA_HD:A_SQ + (hq + 1) * SWA_HD]
            qparts.append(jnp.concatenate([qh, zq] if hq < SWA_GROUP else [zq, qh], axis=1))
        qs = jnp.concatenate(qparts, axis=0).astype(BF16)
        s_c = _dot_nt(qs, kc_ref[seq0 + nseq - 1].astype(BF16))
        for bl in range(nseq - 2, -1, -1):
            s_c = jnp.where(blq == bl, _dot_nt(qs, kc_ref[seq0 + bl].astype(BF16)), s_c)
        kn = jnp.concatenate([knew.astype(BF16), zeros_pad], axis=0)
        s_n = _dot_nt(qs, kn)

        yield
        for h in range(ML_HEADS):
            lane = ML_HEADS + h
            kw_h = k[h] * kws[:, lane:lane + 1]
            kw.append(kw_h)
            vm = jnp.concatenate([jnp.where(bl16 == bl, v[h], 0.0) for bl in range(nseq)], axis=1)
            upd.append(_dot_tn(kw_h.astype(BF16), vm.astype(BF16)))

        yield
        for h in range(ML_HEADS):
            hs = slice(h * ML_DK, (h + 1) * ML_DK)
            lane = ML_HEADS + h
            bm = jnp.zeros((SLAB, ML_DV), F32)
            sum_s = jnp.zeros((SLAB, 1), F32)
            for dl in range(tdec):
                s_dl = jnp.sum(q[h] * _roll_rows(k[h], dl), axis=-1, keepdims=True)
                ws = s_dl * ML_SCALE * dm[dl][:, lane:lane + 1]
                bm = bm + ws * _roll_rows(v[h], dl)
                sum_s = sum_s + ws
            nsel = jnp.zeros((SLAB, ML_DK), F32)
            for bl in range(nseq):
                msk = bl16 == bl
                n_b = n_ref[seq0 + bl][h:h + 1, :]
                nsel = jnp.where(msk, n_b, nsel)
                a_b = a[bl * tdec:bl * tdec + 1, lane:lane + 1]
                c_out[seq0 + bl, h] = a_b * c_b[h][bl] + upd[h][:, bl * ML_DV:(bl + 1) * ML_DV]
                n_out[seq0 + bl, h:h + 1, :] = a_b * n_b + jnp.sum(jnp.where(msk, kw[h], 0.0), axis=0,
                                                                   keepdims=True)
            sclh = scl[:, lane:lane + 1]
            num = sclh * qc[h] + bm
            den = sclh * jnp.sum(q[h] * nsel, axis=-1, keepdims=True) + sum_s
            hh = num / jnp.maximum(jnp.abs(den), enm[:, lane:lane + 1])
            hn = hh * lax.rsqrt(jnp.mean(hh * hh, axis=-1, keepdims=True) + NORM_EPS)
            og = jax.nn.sigmoid(pf[prow, B_MO + h * ML_DV:B_MO + (h + 1) * ML_DV])
            mix_ref[rows, hs] = (hn * mlg[:, hs] * og).astype(mix_ref.dtype)

        s_c = s_c * SWA_SCALE + tblc[...]
        s_n = s_n * SWA_SCALE + tbln[...]
        mx = jnp.maximum(jnp.maximum(jnp.max(s_c, axis=-1, keepdims=True),
                                     jnp.max(s_n, axis=-1, keepdims=True)), snk)
        p_c = jnp.exp(s_c - mx)
        p_n = jnp.exp(s_n - mx)
        den = (jnp.sum(p_c, axis=-1, keepdims=True) + jnp.sum(p_n, axis=-1, keepdims=True)
               + jnp.exp(snk - mx))

        yield
        vn = jnp.concatenate([vnew.astype(BF16), zeros_pad], axis=0)
        o = _dot(p_n.astype(BF16), vn)
        for bl in range(nseq):
            o = o + _dot(jnp.where(blq == bl, p_c, 0.0).astype(BF16), vc_ref[seq0 + bl].astype(BF16))
        o = o / den
        osel = jnp.where(first_group, o[:, 0:SWA_HD], o[:, SWA_HD:2 * SWA_HD])
        og = jnp.concatenate([osel[hq * SLAB:(hq + 1) * SLAB, :] for hq in range(SWA_HEADS)], axis=1)
        mix_ref[rows, ML_WIDTH:ML_WIDTH + SWA_WIDTH] = og.astype(mix_ref.dtype)

        for bl in range(nseq):
            ko_ref[seq0 + bl, 0:WINDOW - tdec, :] = kc_ref[seq0 + bl, tdec:WINDOW, :]
            vo_ref[seq0 + bl, 0:WINDOW - tdec, :] = vc_ref[seq0 + bl, tdec:WINDOW, :]
            ko_ref[seq0 + bl, WINDOW - tdec:WINDOW, :] = knew[bl * tdec:(bl + 1) * tdec, :]
            vo_ref[seq0 + bl, WINDOW - tdec:WINDOW, :] = vnew[bl * tdec:(bl + 1) * tdec, :]
        yield

    slabs = [slab(si) for si in range(nslab)]
    for _ in range(4):
        for s in slabs:
            next(s)


def _sample_mixer(x3d, wt, wn, gb, mlg, sinks, rb, mrep, c0, n0, kc, vc):
    nseq, tdec, _ = x3d.shape
    rows = nseq * tdec
    gbs = SAMPLE_GB
    steps = nseq // gbs
    rb_rows = gbs * tdec
    const2 = lambda i: (0, 0)
    smem = pl.BlockSpec(memory_space=pltpu.SMEM)
    return pl.pallas_call(
        _sample_mixer_kernel,
        grid=(steps,),
        in_specs=[
            pl.BlockSpec((nseq, tdec, D_MODEL), lambda i: (0, 0, 0), pipeline_mode=pl.Buffered(1)),
            pl.BlockSpec((T_COLS, D_MODEL), const2, pipeline_mode=pl.Buffered(1)),
            pl.BlockSpec((D_MODEL, N_B + B_SV), const2, pipeline_mode=pl.Buffered(1)),
            pl.BlockSpec((1, BLK), const2),
            pl.BlockSpec((1, ML_WIDTH), const2),
            smem, smem,
            pl.BlockSpec((rb_rows, BLK), lambda i: (i, 0)),
            pl.BlockSpec((gbs, ML_HEADS, ML_DK, ML_DV), lambda i: (i, 0, 0, 0)),
            pl.BlockSpec((gbs, ML_HEADS, ML_DK), lambda i: (i, 0, 0)),
            pl.BlockSpec((gbs, WINDOW, BLK), lambda i: (i, 0, 0)),
            pl.BlockSpec((gbs, WINDOW, BLK), lambda i: (i, 0, 0)),
        ],
        out_specs=[
            pl.BlockSpec((rb_rows, D_MODEL), lambda i: (i, 0)),
            pl.BlockSpec((gbs, ML_HEADS, ML_DK, ML_DV), lambda i: (i, 0, 0, 0)),
            pl.BlockSpec((gbs, ML_HEADS, ML_DK), lambda i: (i, 0, 0)),
            pl.BlockSpec((rb_rows, BLK), lambda i: (i, 0)),
            pl.BlockSpec((gbs, WINDOW, BLK), lambda i: (i, 0, 0)),
            pl.BlockSpec((gbs, WINDOW, BLK), lambda i: (i, 0, 0)),
        ],
        out_shape=[
            jax.ShapeDtypeStruct((rows, D_MODEL), BF16),
            jax.ShapeDtypeStruct((nseq, ML_HEADS, ML_DK, ML_DV), F32),
            jax.ShapeDtypeStruct((nseq, ML_HEADS, ML_DK), F32),
            jax.ShapeDtypeStruct((rows, BLK), F32),
            jax.ShapeDtypeStruct((nseq, WINDOW, BLK), F32),
            jax.ShapeDtypeStruct((nseq, WINDOW, BLK), F32),
        ],
        scratch_shapes=[
            pltpu.VMEM((rows, A_COLS), F32),
            pltpu.VMEM((rows, B_COLS), F32),
            pltpu.VMEM((SWA_HEADS * SLAB, BLK), F32),
            pltpu.VMEM((SWA_HEADS * SLAB, BLK), F32),
        ],
        compiler_params=pltpu.CompilerParams(
            dimension_semantics=("arbitrary",),
            vmem_limit_bytes=VMEM_LIMIT),
        name="sample_mixer",
    )(x3d, wt, wn, gb, mlg, sinks, rb, mrep, c0, n0, kc, vc)


def _dense_kernel(x_ref, mix_ref, xs_ref, mixs_ref, wo_ref, g1_ref, b1_ref, wfi_ref, wfo_ref, g2_ref, b2_ref,
                  y_ref, ys_ref):
    bounds = [b for b in DENSE_SPLITS if b <= x_ref.shape[0]]
    nsub = len(bounds) - 1
    subs = [slice(bounds[i], bounds[i + 1]) for i in range(nsub)]
    chunks = list(zip(FF_SPLITS[:-1], FF_SPLITS[1:]))
    g1, b1, g2, b2 = g1_ref[...], b1_ref[...], g2_ref[...], b2_ref[...]

    xin = [x_ref[s, :] for s in subs]
    mixin = [mix_ref[s, :] for s in subs]
    xs = xs_ref[...].reshape(mixs_ref.shape)
    xin[-1] = jnp.concatenate([xin[-1], xs], axis=0)
    mixin[-1] = jnp.concatenate([mixin[-1], mixs_ref[...]], axis=0)

    proj = [_dot(mixin[i], wo_ref[...]) for i in range(nsub)]
    h1 = [_layer_norm(ALPHA * xin[i] + proj[i], g1, b1) for i in range(nsub)]
    h1b = [h.astype(BF16) for h in h1]
    acc = [None] * nsub
    for c0, c1 in chunks:
        gate, up = [], []
        for i in range(nsub):
            gate.append(_dot(h1b[i], wfi_ref[:, c0:c1]))
            up.append(_dot(h1b[i], wfi_ref[:, D_FF + c0:D_FF + c1]))
        for i in range(nsub):
            act = (gate[i] * jax.nn.sigmoid(gate[i]) * up[i]).astype(BF16)
            part = _dot(act, wfo_ref[c0:c1, :])
            acc[i] = part if acc[i] is None else acc[i] + part
    for i, s in enumerate(subs):
        out = _layer_norm(ALPHA * h1[i] + acc[i], g2, b2)
        nrows = s.stop - s.start
        y_ref[s, :] = out[:nrows]
        if i == nsub - 1:
            ys_ref[...] = out[nrows:].reshape(ys_ref.shape)


def _dense(x2d, mix2d, xs3d, mixs2d, wo, g1, b1, wfi, wfo, g2, b2):
    tdec = xs3d.shape[1]
    rows = x2d.shape[0]
    tm = TM_DENSE
    steps = rows // tm
    srows = mixs2d.shape[0] // steps
    seq3 = pl.BlockSpec((srows // tdec, tdec, D_MODEL), lambda i: (i, 0, 0))
    const2 = lambda i: (0, 0)
    step2 = lambda i: (i, 0)

    def wspec(shape):
        return pl.BlockSpec(shape, const2, pipeline_mode=pl.Buffered(1))

    return pl.pallas_call(
        _dense_kernel,
        grid=(steps,),
        in_specs=[
            pl.BlockSpec((tm, D_MODEL), step2),
            pl.BlockSpec((tm, D_MODEL), step2),
            seq3,
            pl.BlockSpec((srows, D_MODEL), step2),
            wspec((D_MODEL, D_MODEL)),
            wspec((1, D_MODEL)), wspec((1, D_MODEL)),
            wspec((D_MODEL, 2 * D_FF)),
            wspec((D_FF, D_MODEL)),
            wspec((1, D_MODEL)), wspec((1, D_MODEL)),
        ],
        out_specs=[pl.BlockSpec((tm, D_MODEL), step2),
                   seq3],
        out_shape=[jax.ShapeDtypeStruct((rows, D_MODEL), F32),
                   jax.ShapeDtypeStruct(xs3d.shape, F32)],
        compiler_params=pltpu.CompilerParams(
            dimension_semantics=("arbitrary",),
            vmem_limit_bytes=VMEM_LIMIT),
        name="dense",
    )(x2d, mix2d, xs3d, mixs2d, wo, g1, b1, wfi, wfo, g2, b2)


def kernel(x_prompt, x_sample, state_mlstm_C, state_mlstm_n, state_mlstm_m, cache_swa_k, cache_swa_v,
           w_in, b_igate, b_fgate, ml_norm_g, swa_sinks, rel_bias, w_out, ln1_g, ln1_b,
           w_ffn_in, w_ffn_out, ln2_g, ln2_b):
    bp, seq, _ = x_prompt.shape
    bs, tdec, _ = x_sample.shape
    l = 0

    w16 = jnp.pad(w_in[l], ((0, 0), (0, W_IN_PAD - w_in.shape[-1]))).astype(BF16)
    gvec = jnp.concatenate([b_igate[l], b_fgate[l]]).astype(F32)
    gb = jnp.pad(gvec, (0, BLK - 2 * ML_HEADS))[None, :]
    gbc = jnp.broadcast_to(jnp.pad(gvec, (0, T_GROWS - 2 * ML_HEADS))[:, None], (T_GROWS, BLK))
    mlg = ml_norm_g[l][None, :].astype(F32)
    sinks = swa_sinks[l].astype(F32)
    rb = rel_bias.astype(F32)
    g1, b1 = ln1_g[l][None, :], ln1_b[l][None, :]
    g2, b2 = ln2_g[l][None, :], ln2_b[l][None, :]

    mix_p, c_p, n_p, m_p, k_p, v_p, wt, wn, wo, wfi, wfo = _prompt_mixer(
        x_prompt, w16, gbc, mlg, sinks, rb, w_out[l], w_ffn_in[l], w_ffn_out[l])
    p_m =m_p[:, :ML_HEADS // 2, ::BLK].reshape(bp, ML_HEADS)[None]
    p_k = k_p.reshape(1, bp, WINDOW, SWA_KV_HEADS, SWA_HD)
    p_v = v_p.reshape(1, bp, WINDOW, SWA_KV_HEADS, SWA_HD)

    m0 =state_mlstm_m[l].astype(F32)
    mrep = jnp.pad(jnp.repeat(m0, tdec, axis=0), ((0, 0), (ML_HEADS, BLK - 2 * ML_HEADS)))
    wlen = cache_swa_k.shape[2]
    kc = cache_swa_k[l].reshape(bs, wlen, SWA_KV_HEADS * SWA_HD)
    vc = cache_swa_v[l].reshape(bs, wlen, SWA_KV_HEADS * SWA_HD)
    mix_s, c_s, n_s, mo_s, k_s, v_s = _sample_mixer(
        x_sample, wt, wn, gb, mlg, sinks, rb, mrep,
        state_mlstm_C[l].astype(F32), state_mlstm_n[l].astype(F32), kc, vc)
    s_m =mo_s.reshape(bs, tdec, BLK)[:, 0, ML_HEADS:2 * ML_HEADS][None]
    s_k = k_s.reshape(1, bs, wlen, SWA_KV_HEADS, SWA_HD)
    s_v = v_s.reshape(1, bs, wlen, SWA_KV_HEADS, SWA_HD)

    y_p, y_s = _dense(x_prompt.reshape(bp * seq, D_MODEL), mix_p.reshape(bp * seq, D_MODEL), x_sample, mix_s,
                      wo, g1, b1, wfi, wfo, g2, b2)
    y_p = y_p.reshape(bp, seq, D_MODEL)

    return (y_p, y_s, c_p[None], n_p[None], p_m, p_k, p_v,
            c_s[None], n_s[None], s_m, s_k, s_v)
```

```python
import functools
import math

import jax
import jax.numpy as jnp
from jax import lax
from jax.experimental import pallas as pl
from jax.experimental.pallas import tpu as pltpu

F32 = jnp.float32
BF16 = jnp.bfloat16

D_MODEL = 1024
ML_HEADS = 4
ML_DK = 128
ML_DV = 128
ML_WIDTH = ML_HEADS * ML_DV
GATE_SOFTCAP = 15.0
SWA_HEADS = 8
SWA_KV_HEADS = 2
SWA_GROUP = SWA_HEADS // SWA_KV_HEADS
SWA_HD = 64
SWA_WIDTH = SWA_HEADS * SWA_HD
WINDOW = 128
REL_BUCKETS = 32
REL_MAX_DIST = 128
D_FF = 2816
DEPTH = 1
ALPHA = (2.0 * DEPTH) ** 0.25
LN_EPS = 1e-5
NORM_EPS = 1e-6
NEG_INF = -1e30
IN_SIZES = (512, 512, 512, 512, 4, 4, 512, 128, 128)
ML_SCALE = ML_DK ** -0.5
LOG_ML_SCALE = math.log(ML_SCALE)
SWA_SCALE = SWA_HD ** -0.5

BLK = 128
SUBLANES_F32 = 8
SUBLANES_BF16 = 16

A_Q, A_K, A_V, A_SQ = 0, 512, 1024, 1536
A_COLS = 2048
B_MO, B_SK, B_SV, B_G = 0, 512, 640, 768
B_COLS = 896

N_K = 0
N_B = 512
T_Q, T_V, T_SQ, T_SV, T_G = 0, 512, 1024, 1536, 1664
T_GROWS = SUBLANES_BF16
T_COLS = T_G + BLK
W_IN_PAD = -(-sum(IN_SIZES) // BLK) * BLK
STATE_ROWS = ML_DV + SUBLANES_BF16
M_ROWS = SUBLANES_F32

TM_PROMPT = 1024
TM_DENSE = 1024
DENSE_SPLITS = (0, 384, 768, 1024)
FF_SPLITS = (0, 1280, D_FF)
SAMPLE_GB = 16
SLAB = 16
VMEM_LIMIT = 56 * 1024 * 1024


def _softcap(a):
    return GATE_SOFTCAP * jnp.tanh(a / GATE_SOFTCAP)


def _log_sigmoid(x):
    return jnp.minimum(x, 0.0) - jnp.log1p(jnp.exp(-jnp.abs(x)))


def _layer_norm(z, g, b):
    mu = jnp.mean(z, axis=-1, keepdims=True)
    zc = z - mu
    var = jnp.mean(zc * zc, axis=-1, keepdims=True)
    return zc * lax.rsqrt(var + LN_EPS) * g + b


def _dot(a, b):
    return jnp.dot(a, b, preferred_element_type=F32)


def _dot_nt(a, b):
    return lax.dot_general(a, b, (((1,), (1,)), ((), ())), preferred_element_type=F32)


def _dot_tn(a, b):
    return lax.dot_general(a, b, (((0,), (0,)), ((), ())), preferred_element_type=F32)


def _split3(x):
    hi = x.astype(BF16)
    r1 = x - hi.astype(F32)
    mid = r1.astype(BF16)
    lo = (r1 - mid.astype(F32)).astype(BF16)
    return hi, mid, lo


def _t5_bucket(d):
    n = jnp.maximum(d, 0)
    max_exact = REL_BUCKETS // 2
    nlog = REL_BUCKETS - max_exact
    large = jnp.full(n.shape, max_exact, jnp.int32)
    for k in range(1, nlog):
        thr = math.ceil(max_exact * (REL_MAX_DIST / max_exact) ** (k / nlog))
        large = large + jnp.where(n >= thr, 1, 0)
    return jnp.where(n < max_exact, n, large)


def _prompt_mixer_kernel(x_ref, w_ref, gbc_ref, mlg_ref, sinks_ref, rb_ref, wo32, wfi32, wfo32,
                         mix_ref, c_out, n_out, m_out, ko_ref, vo_ref,
                         wt, wn, wo16, wfi16, wfo16,
                         kn, pf, vlast, kband, q_t, v_t, sq_t, sv_t, g_t, ct, msc, tbl):
    wo16[...] = wo32[...].astype(BF16)
    wfi16[...] = wfi32[...].astype(BF16)
    wfo16[...] = wfo32[...].astype(BF16)

    b = pl.program_id(0)
    t = pl.program_id(1)
    nt = pl.num_programs(1)
    tm = x_ref.shape[1]
    nblk = tm // BLK
    pairs = SWA_HEADS // 2

    @pl.when((b == 0) & (t == 0))
    def _prepare_weights():
        offs = [0]
        for n in IN_SIZES:
            offs.append(offs[-1] + n)
        o_mq, o_mk, o_mv, o_mo, o_mi, _, o_sq, o_sk, o_sv = offs[:-1]

        def col_blocks(dst_col, src_col, n):
            for i in range(n // BLK):
                wn[:, dst_col + i * BLK:dst_col + (i + 1) * BLK] = (
                    w_ref[:, src_col + i * BLK:src_col + (i + 1) * BLK])

        def row_blocks(dst_row, src_col, n):
            for i in range(n // BLK):
                wt[dst_row + i * BLK:dst_row + (i + 1) * BLK, :] = (
                    w_ref[:, src_col + i * BLK:src_col + (i + 1) * BLK].T)

        col_blocks(N_K, o_mk, ML_HEADS * ML_DK)
        col_blocks(N_B + B_MO, o_mo, ML_WIDTH)
        col_blocks(N_B + B_SK, o_sk, SWA_KV_HEADS * SWA_HD)
        row_blocks(T_Q, o_mq, ML_HEADS * ML_DK)
        row_blocks(T_V, o_mv, ML_WIDTH)
        row_blocks(T_SQ, o_sq, SWA_WIDTH)
        row_blocks(T_SV, o_sv, SWA_KV_HEADS * SWA_HD)
        gates_t = w_ref[:, o_mi:o_mi + BLK].T
        grow = lax.broadcasted_iota(jnp.int32, gates_t.shape, 0)
        wt[T_G:T_G + BLK, :] = jnp.where(grow < 2 * ML_HEADS, gates_t, jnp.zeros_like(gates_t))

    @pl.when((b == 0) & (t == 0))
    def _build_bias_tables():
        r = lax.broadcasted_iota(jnp.int32, (BLK, 2 * BLK), 0)
        ln = lax.broadcasted_iota(jnp.int32, (BLK, 2 * BLK), 1)
        second = ln >= BLK
        qi = jnp.where(second, ln - BLK, ln)
        prev = r > qi
        d = jnp.where(prev, WINDOW + qi - r, qi - r)
        bucket = _t5_bucket(d)
        for p in range(pairs):
            acc = jnp.zeros((BLK, 2 * BLK), F32)
            for k in range(REL_BUCKETS):
                acc = jnp.where(bucket == k, jnp.where(second, rb_ref[k, p + pairs], rb_ref[k, p]), acc)
            tbl[0, p] = jnp.where(prev, NEG_INF, acc)
            tbl[1, p] = acc

    @pl.when(t == 0)
    def _reset_state():
        ct[...] = jnp.zeros_like(ct)
        msc[...] = jnp.zeros_like(msc)
        kband[0:BLK, :] = jnp.zeros((BLK, kband.shape[1]), kband.dtype)
        sv_t[0] = jnp.zeros(sv_t.shape[1:], sv_t.dtype)

    xb = x_ref[0].astype(BF16)
    res_n = _dot(xb, wn[...])
    kn[...] = res_n[:, N_K:N_K + ML_HEADS * ML_DK].astype(BF16)
    pf[...] = res_n[:, N_B:N_B + B_SV]
    kband[BLK:BLK + tm, :] = pf[:, B_SK:B_SK + BLK].astype(BF16)

    res_t = _dot_nt(wt[0:T_G + T_GROWS, :], xb)
    vlast[...] = res_t[T_SV:T_SV + BLK, tm - BLK:tm]

    def put(dst, r0, nrows, off, scale=None):
        res = res_t[r0:r0 + nrows, :]
        if scale is not None:
            res = res * scale
        for jj in range(nblk):
            dst[jj + off] = res[:, jj * BLK:(jj + 1) * BLK].astype(dst.dtype)

    put(q_t, T_Q, ML_HEADS * ML_DK, 0)
    put(v_t, T_V, ML_WIDTH, 0)
    put(sq_t, T_SQ, SWA_WIDTH, 0, SWA_SCALE)
    put(sv_t, T_SV, SWA_KV_HEADS * SWA_HD, 1)
    put(g_t, T_G, T_GROWS, 0)

    row = lax.broadcasted_iota(jnp.int32, (BLK, BLK), 0)
    col = lax.broadcasted_iota(jnp.int32, (BLK, BLK), 1)
    causal_t = row <= col
    triu = causal_t.astype(BF16)
    causal2 = jnp.concatenate([causal_t, causal_t], axis=1)
    r16 = lax.broadcasted_iota(jnp.int32, (STATE_ROWS - ML_DV, 2 * BLK), 0)
    ones_rows = (r16 == 0).astype(BF16)
    z128 = jnp.zeros((BLK, BLK), BF16)
    ml_pairs = ML_HEADS // 2
    zero_rows = jnp.zeros((BLK - T_GROWS, BLK), F32)
    gbc = gbc_ref[...]
    mlg = mlg_ref[...]
    lane2 = lax.broadcasted_iota(jnp.int32, (1, 2 * BLK), 1)
    snk_rows = [jnp.where(lane2 >= BLK, sinks_ref[p + pairs], sinks_ref[p]) for p in range(pairs)]
    zq = jnp.zeros((SWA_HD, BLK), BF16)
    prev2 = jnp.concatenate([row > col, row > col], axis=1)

    def pair_row(x0, x1):
        return jnp.concatenate([jnp.broadcast_to(x0, (1, BLK)), jnp.broadcast_to(x1, (1, BLK))], axis=1)

    def block_diag(x0, x1):
        return jnp.concatenate([jnp.concatenate([x0, z128], axis=1),
                                jnp.concatenate([z128, x1], axis=1)], axis=0)

    def block(j):
        r0 = j * BLK
        rows = pl.ds(r0, BLK)
        hsl = [slice(h * ML_DK, (h + 1) * ML_DK) for h in range(ML_HEADS)]
        psl = [slice(p * 2 * ML_DK, (p + 1) * 2 * ML_DK) for p in range(ml_pairs)]

        pre_t = _softcap(g_t[j] + gbc)
        hi, mid, lo = _split3(_log_sigmoid(pre_t))
        b_t = _dot(hi, triu) + _dot(mid, triu) + _dot(lo, triu)
        u_t = pltpu.roll(pre_t, ML_HEADS, 0) - b_t
        u_c = jnp.concatenate([u_t, zero_rows], axis=0).T

        k2 = [kn[rows, psl[p]] for p in range(ml_pairs)]
        bdq = [block_diag(q_t[j, hsl[2 * p], :], q_t[j, hsl[2 * p + 1], :]) for p in range(ml_pairs)]
        vte = [jnp.concatenate([jnp.concatenate([v_t[j, hsl[2 * p], :], v_t[j, hsl[2 * p + 1], :]], axis=1),
                                ones_rows], axis=0) for p in range(ml_pairs)]
        kq = [_dot(k2[p], bdq[p]) for p in range(ml_pairs)]

        sel = jnp.where(t == 0, 0, 1) if j == 0 else 1
        kb = kband[pl.ds(r0, 2 * BLK), :]
        vtb = jnp.concatenate([sv_t[j], sv_t[j + 1]], axis=1)
        sc2 = []
        for p in range(pairs):
            qa = sq_t[j, p * SWA_HD:(p + 1) * SWA_HD, :]
            qb = sq_t[j, (p + pairs) * SWA_HD:(p + pairs + 1) * SWA_HD, :]
            bd = jnp.concatenate([jnp.concatenate([qa, zq], axis=1),
                                  jnp.concatenate([zq, qb], axis=1)], axis=0)
            sc2.append(_dot(kb, bd))

        yield
        bds, scl, m_row, m_new, a_p, vwte = [], [], [], [], [], []
        for p in range(ml_pairs):
            h0, h1 = 2 * p, 2 * p + 1
            b_row = pair_row(b_t[ML_HEADS + h0:ML_HEADS + h0 + 1, :], b_t[ML_HEADS + h1:ML_HEADS + h1 + 1, :])
            i_row = pair_row(pre_t[h0:h0 + 1, :], pre_t[h1:h1 + 1, :])
            u_col = jnp.concatenate(
                [jnp.broadcast_to(u_c[:, ML_HEADS + h0:ML_HEADS + h0 + 1], (BLK, BLK)),
                 jnp.broadcast_to(u_c[:, ML_HEADS + h1:ML_HEADS + h1 + 1], (BLK, BLK))], axis=1)
            m_prev = msc[p:p + 1, :]
            log_d = jnp.where(causal2, b_row + u_col, -jnp.inf)
            inter = b_row + m_prev
            mr = jnp.maximum(inter, jnp.max(log_d, axis=0, keepdims=True))
            m_row.append(mr)
            scl.append(jnp.exp(inter - mr))
            s2 = (kq[p] * jnp.exp(log_d - (mr - LOG_ML_SCALE))).astype(BF16)
            bds.append(block_diag(s2[:, :BLK], s2[:, BLK:]))
            b_last = pair_row(b_row[:, BLK - 1:BLK], b_row[:, 2 * BLK - 1:2 * BLK])
            w_row = b_last - b_row + i_row
            w_max = pair_row(jnp.max(w_row[:, :BLK], axis=1, keepdims=True),
                             jnp.max(w_row[:, BLK:], axis=1, keepdims=True))
            mn = jnp.maximum(b_last + m_prev, w_max)
            m_new.append(mn)
            a_p.append(jnp.exp(b_last + m_prev - mn))
            wexp = jnp.exp(w_row - mn) * ML_SCALE
            vwte.append(vte[p] * wexp.astype(BF16))

        pw, den_p = [], []
        for p in range(pairs):
            sc = jnp.where(prev2, sc2[p][:BLK], sc2[p][BLK:]) + tbl[sel, p]
            snk = snk_rows[p]
            mx = jnp.maximum(jnp.max(sc, axis=0, keepdims=True), snk)
            pe = jnp.exp(sc - mx)
            den_p.append(jnp.sum(pe, axis=0, keepdims=True) + jnp.exp(snk - mx))
            pw.append(jnp.concatenate([jnp.where(prev2, pe, 0.0), jnp.where(prev2, 0.0, pe)],
                                      axis=0).astype(BF16))

        c_old = [ct[p] for p in range(ml_pairs)]
        cq = [_dot(c_old[p].astype(BF16), bdq[p]) for p in range(ml_pairs)]
        vs = [_dot(vte[p], bds[p]) for p in range(ml_pairs)]
        upd = [_dot(vwte[p], block_diag(k2[p][:, :BLK], k2[p][:, BLK:])) for p in range(ml_pairs)]
        o2 = [_dot(vtb, pw[p]) for p in range(pairs)]

        for p in range(ml_pairs):
            nd = scl[p] * cq[p] + vs[p]
            num = nd[:ML_DV]
            den = nd[ML_DV:ML_DV + 1]
            hh = num / jnp.maximum(jnp.abs(den), jnp.exp(-m_row[p]))
            hn = hh * lax.rsqrt(jnp.mean(hh * hh, axis=0, keepdims=True) + NORM_EPS)
            for half in range(2):
                h = 2 * p + half
                og = jax.nn.sigmoid(pf[rows, B_MO + h * ML_DV:B_MO + (h + 1) * ML_DV])
                hn_h = hn[:, half * BLK:(half + 1) * BLK].T
                mix_ref[0, rows, hsl[h]] = (hn_h * mlg[:, hsl[h]] * og).astype(mix_ref.dtype)
            ct[p] = a_p[p] * c_old[p] + upd[p]
            msc[p:p + 1, :] = m_new[p]

        o_parts = [None] * SWA_HEADS
        for p in range(pairs):
            on = o2[p] / den_p[p]
            o_parts[p] = on[0:SWA_HD, 0:BLK]
            o_parts[p + pairs] = on[SWA_HD:2 * SWA_HD, BLK:2 * BLK]
        o_t = jnp.concatenate(o_parts, axis=0)
        mix_ref[0, rows, ML_WIDTH:ML_WIDTH + SWA_WIDTH] = o_t.T.astype(mix_ref.dtype)
        yield

    blocks = [block(j) for j in range(nblk)]
    for g in blocks:
        next(g)
    for g in blocks:
        next(g)

    kband[0:BLK, :] = kband[tm:tm + BLK, :]
    sv_t[0] = sv_t[nblk]

    @pl.when(t == nt - 1)
    def _write_state():
        for h in range(ML_HEADS):
            p, lanes = h // 2, slice((h % 2) * ML_DK, (h % 2 + 1) * ML_DK)
            c_out[0, h] = ct[p, 0:ML_DV, lanes].T
            n_out[0, h:h + 1, :] = ct[p, ML_DV:ML_DV + 1, lanes]
        m_out[0] = msc[...]
        ko_ref[0] = pf[tm - BLK:tm, B_SK:B_SK + BLK]
        vo_ref[0] = vlast[...].T


def _prompt_mixer(x, w_in_l, gbc, mlg, sinks, rb, wo32, wfi32, wfo32):
    bsz, seq, _ = x.shape
    tm = TM_PROMPT
    nt = seq // tm
    nblk = tm // BLK
    steps = bsz * nt
    const2 = lambda b, t: (0, 0)
    smem = pl.BlockSpec(memory_space=pltpu.SMEM)

    def slab_spec(wmat):
        return pl.BlockSpec((wmat.shape[0] // steps, wmat.shape[1]), lambda b, t: (b * nt + t, 0))

    return pl.pallas_call(
        _prompt_mixer_kernel,
        grid=(bsz, nt),
        in_specs=[
            pl.BlockSpec((1, tm, D_MODEL), lambda b, t: (b, t, 0)),
            pl.BlockSpec((D_MODEL, W_IN_PAD), const2, pipeline_mode=pl.Buffered(1)),
            pl.BlockSpec((T_GROWS, BLK), const2),
            pl.BlockSpec((1, ML_WIDTH), const2),
            smem, smem,
            slab_spec(wo32), slab_spec(wfi32), slab_spec(wfo32),
        ],
        out_specs=[
            pl.BlockSpec((1, tm, D_MODEL), lambda b, t: (b, t, 0)),
            pl.BlockSpec((1, ML_HEADS, ML_DK, ML_DV), lambda b, t: (b, 0, 0, 0)),
            pl.BlockSpec((1, ML_HEADS, ML_DK), lambda b, t: (b, 0, 0)),
            pl.BlockSpec((1, M_ROWS, 2 * BLK), lambda b, t: (b, 0, 0)),
            pl.BlockSpec((1, BLK, BLK), lambda b, t: (b, 0, 0)),
            pl.BlockSpec((1, BLK, BLK), lambda b, t: (b, 0, 0)),
            pl.BlockSpec((T_COLS, D_MODEL), const2),
            pl.BlockSpec((D_MODEL, N_B + B_SV), const2),
            slab_spec(wo32), slab_spec(wfi32), slab_spec(wfo32),
        ],
        out_shape=[
            jax.ShapeDtypeStruct((bsz, seq, D_MODEL), BF16),
            jax.ShapeDtypeStruct((bsz, ML_HEADS, ML_DK, ML_DV), F32),
            jax.ShapeDtypeStruct((bsz, ML_HEADS, ML_DK), F32),
            jax.ShapeDtypeStruct((bsz, M_ROWS, 2 * BLK), F32),
            jax.ShapeDtypeStruct((bsz, BLK, BLK), F32),
            jax.ShapeDtypeStruct((bsz, BLK, BLK), F32),
            jax.ShapeDtypeStruct((T_COLS, D_MODEL), BF16),
            jax.ShapeDtypeStruct((D_MODEL, N_B + B_SV), BF16),
            jax.ShapeDtypeStruct(wo32.shape, BF16),
            jax.ShapeDtypeStruct(wfi32.shape, BF16),
            jax.ShapeDtypeStruct(wfo32.shape, BF16),
        ],
        scratch_shapes=[
            pltpu.VMEM((tm, ML_HEADS * ML_DK), BF16),
            pltpu.VMEM((tm, B_SV), F32),
            pltpu.VMEM((BLK, BLK), F32),
            pltpu.VMEM((tm + BLK, BLK), BF16),
            pltpu.VMEM((nblk, ML_HEADS * ML_DK, BLK), BF16),
            pltpu.VMEM((nblk, ML_WIDTH, BLK), BF16),
            pltpu.VMEM((nblk, SWA_WIDTH, BLK), BF16),
            pltpu.VMEM((nblk + 1, SWA_KV_HEADS * SWA_HD, BLK), BF16),
            pltpu.VMEM((nblk, T_GROWS, BLK), F32),
            pltpu.VMEM((ML_HEADS // 2, STATE_ROWS, 2 * ML_DK), F32),
            pltpu.VMEM((M_ROWS, 2 * BLK), F32),
            pltpu.VMEM((2, SWA_HEADS // 2, BLK, 2 * BLK), F32),
        ],
        compiler_params=pltpu.CompilerParams(
            dimension_semantics=("arbitrary", "arbitrary"),
            vmem_limit_bytes=VMEM_LIMIT),
        name="prompt_mixer",
    )(x, w_in_l, gbc, mlg, sinks, rb, wo32, wfi32, wfo32)


def _roll_rows(x, shift):
    return pltpu.roll(x, shift % x.shape[0], 0)


def _sample_mixer_kernel(x_ref, wt_ref, wn_ref, gb_ref, mlg_ref, sinks_ref, rb_ref, mrep_ref,
                         c_ref, n_ref, kc_ref, vc_ref,
                         mix_ref, c_out, n_out, mo_ref, ko_ref, vo_ref,
                         pa, pf, tblc, tbln):
    step = pl.program_id(0)
    tdec = 4
    rows_step = mix_ref.shape[0]
    nslab = rows_step // SLAB
    nseq = SLAB // tdec
    qrows = SWA_HEADS * SLAB

    @pl.when(step == 0)
    def _project_all_rows():
        xb = x_ref[...].reshape(pa.shape[0], D_MODEL).astype(BF16)
        pa[:, A_Q:A_K] = _dot_nt(xb, wt_ref[T_Q:T_Q + ML_HEADS * ML_DK, :])
        pa[:, A_K:A_V] = _dot(xb, wn_ref[:, N_K:N_K + ML_HEADS * ML_DK])
        pa[:, A_V:A_COLS] = _dot_nt(xb, wt_ref[T_V:T_V + ML_WIDTH + SWA_WIDTH, :])
        pf[:, B_MO:B_SV] = _dot(xb, wn_ref[:, N_B:N_B + B_SV])
        pf[:, B_SV:B_COLS] = _dot_nt(xb, wt_ref[T_SV:T_COLS, :])

    @pl.when(step == 0)
    def _build_bias_tables():
        r = lax.broadcasted_iota(jnp.int32, (qrows, BLK), 0)
        c = lax.broadcasted_iota(jnp.int32, (qrows, BLK), 1)
        hq = r // SLAB
        bl = (r % SLAB) // tdec
        tq = r % tdec
        d_cache = WINDOW + tq - c
        d_new = tq - (c % tdec)
        key_cache = _t5_bucket(d_cache) * SWA_HEADS + hq
        key_new = _t5_bucket(d_new) * SWA_HEADS + hq
        acc_c = jnp.zeros((qrows, BLK), F32)
        acc_n = jnp.zeros((qrows, BLK), F32)
        for k in range(REL_BUCKETS):
            for h in range(SWA_HEADS):
                val = rb_ref[k, h]
                acc_c = jnp.where(key_cache == k * SWA_HEADS + h, val, acc_c)
                acc_n = jnp.where(key_new == k * SWA_HEADS + h, val, acc_n)
        tblc[...] = jnp.where(c > tq, acc_c, NEG_INF)
        tbln[...] = jnp.where((c < SLAB) & ((c // tdec) == bl) & (d_new >= 0), acc_n, NEG_INF)

    gbias = gb_ref[...]
    mlg = mlg_ref[...]
    r16 = lax.broadcasted_iota(jnp.int32, (SLAB, BLK), 0)
    rr = r16 % tdec
    bl16 = r16 // tdec
    rq = lax.broadcasted_iota(jnp.int32, (qrows, BLK), 0)
    blq = (rq % SLAB) // tdec
    hq_col = rq[:, 0:1] // SLAB
    snk = jnp.zeros((qrows, 1), F32)
    for h in range(SWA_HEADS):
        snk = jnp.where(hq_col == h, sinks_ref[h], snk)
    first_group = rq[:, 0:SWA_HD] < SWA_GROUP * SLAB
    zq = jnp.zeros((SLAB, SWA_HD), F32)
    zeros_pad = jnp.zeros((BLK - SLAB, BLK), BF16)

    def seg_last(x):
        return jnp.where(rr == 3, x,
                         jnp.where(rr == 2, _roll_rows(x, -1),
                                   jnp.where(rr == 1, _roll_rows(x, -2), _roll_rows(x, -3))))

    def seg_max(x):
        m1 = jnp.maximum(x, jnp.where(rr % 2 == 0, _roll_rows(x, -1), _roll_rows(x, 1)))
        return jnp.maximum(m1, jnp.where(rr < 2, _roll_rows(m1, -2), _roll_rows(m1, 2)))

    def slab(si):
        rows = pl.ds(si * SLAB, SLAB)
        prow = pl.ds(pl.multiple_of(step * rows_step + si * SLAB, SLAB), SLAB)
        seq0 = si * nseq

        pre = _softcap(pf[prow, B_G:B_G + BLK] + gbias)
        logf = _log_sigmoid(pre)
        y = logf + jnp.where(rr >= 1, _roll_rows(logf, 1), 0.0)
        bcum = y + jnp.where(rr >= 2, _roll_rows(y, 2), 0.0)
        ig = pltpu.roll(pre, ML_HEADS, 1)
        m_prev = mrep_ref[rows, :]
        b_minus_i = bcum - ig
        log_d = [jnp.where(rr >= dl, bcum - _roll_rows(b_minus_i, dl), -jnp.inf) for dl in range(tdec)]
        rowmax = functools.reduce(jnp.maximum, log_d)
        inter = bcum + m_prev
        m_row = jnp.maximum(inter, rowmax)
        scl = jnp.exp(inter - m_row)
        dm = [jnp.exp(ld - m_row) for ld in log_d]
        enm = jnp.exp(-m_row)
        b_last = seg_last(bcum)
        w = b_last - bcum + ig
        m_new = jnp.maximum(b_last + m_prev, seg_max(w))
        a = jnp.exp(b_last + m_prev - m_new)
        kws = jnp.exp(w - m_new) * ML_SCALE
        mo_ref[rows, :] = m_new

        q = [pa[prow, A_Q + h * ML_DK:A_Q + (h + 1) * ML_DK] for h in range(ML_HEADS)]
        k = [pa[prow, A_K + h * ML_DK:A_K + (h + 1) * ML_DK] for h in range(ML_HEADS)]
        v = [pa[prow, A_V + h * ML_DV:A_V + (h + 1) * ML_DV] for h in range(ML_HEADS)]
        c_b = [[c_ref[seq0 + bl, h] for bl in range(nseq)] for h in range(ML_HEADS)]
        qc, upd, kw = [], [], []
        for h in range(ML_HEADS):
            qb = q[h].astype(BF16)
            res = [_dot(qb, jnp.concatenate([c_b[h][2 * i].astype(BF16), c_b[h][2 * i + 1].astype(BF16)], axis=1))
                   for i in range(nseq // 2)]
            parts = [res[bl // 2][:, (bl % 2) * ML_DV:(bl % 2 + 1) * ML_DV] for bl in range(nseq)]
            qc_h = parts[nseq - 1]
            for bl in range(nseq - 2, -1, -1):
                qc_h = jnp.where(bl16 == bl, parts[bl], qc_h)
            qc.append(qc_h)

        knew = pf[prow, B_SK:B_SK + BLK]
        vnew = pf[prow, B_SV:B_SV + BLK]
        qparts = []
        for hq in range(SWA_HEADS):
            qh = pa[prow, A_SQ + hq * SWA_HD:A_SQ + (hq + 1) * SWA_HD]
            qparts.append(jnp.concatenate([qh, zq] if hq < SWA_GROUP else [zq, qh], axis=1))
        qs = jnp.concatenate(qparts, axis=0).astype(BF16)
        s_c = _dot_nt(qs, kc_ref[seq0 + nseq - 1].astype(BF16))
        for bl in range(nseq - 2, -1, -1):
            s_c = jnp.where(blq == bl, _dot_nt(qs, kc_ref[seq0 + bl].astype(BF16)), s_c)
        kn = jnp.concatenate([knew.astype(BF16), zeros_pad], axis=0)
        s_n = _dot_nt(qs, kn)

        yield
        for h in range(ML_HEADS):
            lane = ML_HEADS + h
            kw_h = k[h] * kws[:, lane:lane + 1]
            kw.append(kw_h)
            vm = jnp.concatenate([jnp.where(bl16 == bl, v[h], 0.0) for bl in range(nseq)], axis=1)
            upd.append(_dot_tn(kw_h.astype(BF16), vm.astype(BF16)))

        yield
        for h in range(ML_HEADS):
            hs = slice(h * ML_DK, (h + 1) * ML_DK)
            lane = ML_HEADS + h
            bm = jnp.zeros((SLAB, ML_DV), F32)
            sum_s = jnp.zeros((SLAB, 1), F32)
            for dl in range(tdec):
                s_dl = jnp.sum(q[h] * _roll_rows(k[h], dl), axis=-1, keepdims=True)
                ws = s_dl * ML_SCALE * dm[dl][:, lane:lane + 1]
                bm = bm + ws * _roll_rows(v[h], dl)
                sum_s = sum_s + ws
            nsel = jnp.zeros((SLAB, ML_DK), F32)
            for bl in range(nseq):
                msk = bl16 == bl
                n_b = n_ref[seq0 + bl][h:h + 1, :]
                nsel = jnp.where(msk, n_b, nsel)
                a_b = a[bl * tdec:bl * tdec + 1, lane:lane + 1]
                c_out[seq0 + bl, h] = a_b * c_b[h][bl] + upd[h][:, bl * ML_DV:(bl + 1) * ML_DV]
                n_out[seq0 + bl, h:h + 1, :] = a_b * n_b + jnp.sum(jnp.where(msk, kw[h], 0.0), axis=0,
                                                                   keepdims=True)
            sclh = scl[:, lane:lane + 1]
            num = sclh * qc[h] + bm
            den = sclh * jnp.sum(q[h] * nsel, axis=-1, keepdims=True) + sum_s
            hh = num / jnp.maximum(jnp.abs(den), enm[:, lane:lane + 1])
            hn = hh * lax.rsqrt(jnp.mean(hh * hh, axis=-1, keepdims=True) + NORM_EPS)
            og = jax.nn.sigmoid(pf[prow, B_MO + h * ML_DV:B_MO + (h + 1) * ML_DV])
            mix_ref[rows, hs] = (hn * mlg[:, hs] * og).astype(mix_ref.dtype)

        s_c = s_c * SWA_SCALE + tblc[...]
        s_n = s_n * SWA_SCALE + tbln[...]
        mx = jnp.maximum(jnp.maximum(jnp.max(s_c, axis=-1, keepdims=True),
                                     jnp.max(s_n, axis=-1, keepdims=True)), snk)
        p_c = jnp.exp(s_c - mx)
        p_n = jnp.exp(s_n - mx)
        den = (jnp.sum(p_c, axis=-1, keepdims=True) + jnp.sum(p_n, axis=-1, keepdims=True)
               + jnp.exp(snk - mx))

        yield
        vn = jnp.concatenate([vnew.astype(BF16), zeros_pad], axis=0)
        o = _dot(p_n.astype(BF16), vn)
        for bl in range(nseq):
            o = o + _dot(jnp.where(blq == bl, p_c, 0.0).astype(BF16), vc_ref[seq0 + bl].astype(BF16))
        o = o / den
        osel = jnp.where(first_group, o[:, 0:SWA_HD], o[:, SWA_HD:2 * SWA_HD])
        og = jnp.concatenate([osel[hq * SLAB:(hq + 1) * SLAB, :] for hq in range(SWA_HEADS)], axis=1)
        mix_ref[rows, ML_WIDTH:ML_WIDTH + SWA_WIDTH] = og.astype(mix_ref.dtype)

        for bl in range(nseq):
            ko_ref[seq0 + bl, 0:WINDOW - tdec, :] = kc_ref[seq0 + bl, tdec:WINDOW, :]
            vo_ref[seq0 + bl, 0:WINDOW - tdec, :] = vc_ref[seq0 + bl, tdec:WINDOW, :]
            ko_ref[seq0 + bl, WINDOW - tdec:WINDOW, :] = knew[bl * tdec:(bl + 1) * tdec, :]
            vo_ref[seq0 + bl, WINDOW - tdec:WINDOW, :] = vnew[bl * tdec:(bl + 1) * tdec, :]
        yield

    slabs = [slab(si) for si in range(nslab)]
    for _ in range(4):
        for s in slabs:
            next(s)


def _sample_mixer(x3d, wt, wn, gb, mlg, sinks, rb, mrep, c0, n0, kc, vc):
    nseq, tdec, _ = x3d.shape
    rows = nseq * tdec
    gbs = SAMPLE_GB
    steps = nseq // gbs
    rb_rows = gbs * tdec
    const2 = lambda i: (0, 0)
    smem = pl.BlockSpec(memory_space=pltpu.SMEM)
    return pl.pallas_call(
        _sample_mixer_kernel,
        grid=(steps,),
        in_specs=[
            pl.BlockSpec((nseq, tdec, D_MODEL), lambda i: (0, 0, 0), pipeline_mode=pl.Buffered(1)),
            pl.BlockSpec((T_COLS, D_MODEL), const2, pipeline_mode=pl.Buffered(1)),
            pl.BlockSpec((D_MODEL, N_B + B_SV), const2, pipeline_mode=pl.Buffered(1)),
            pl.BlockSpec((1, BLK), const2),
            pl.BlockSpec((1, ML_WIDTH), const2),
            smem, smem,
            pl.BlockSpec((rb_rows, BLK), lambda i: (i, 0)),
            pl.BlockSpec((gbs, ML_HEADS, ML_DK, ML_DV), lambda i: (i, 0, 0, 0)),
            pl.BlockSpec((gbs, ML_HEADS, ML_DK), lambda i: (i, 0, 0)),
            pl.BlockSpec((gbs, WINDOW, BLK), lambda i: (i, 0, 0)),
            pl.BlockSpec((gbs, WINDOW, BLK), lambda i: (i, 0, 0)),
        ],
        out_specs=[
            pl.BlockSpec((rb_rows, D_MODEL), lambda i: (i, 0)),
            pl.BlockSpec((gbs, ML_HEADS, ML_DK, ML_DV), lambda i: (i, 0, 0, 0)),
            pl.BlockSpec((gbs, ML_HEADS, ML_DK), lambda i: (i, 0, 0)),
            pl.BlockSpec((rb_rows, BLK), lambda i: (i, 0)),
            pl.BlockSpec((gbs, WINDOW, BLK), lambda i: (i, 0, 0)),
            pl.BlockSpec((gbs, WINDOW, BLK), lambda i: (i, 0, 0)),
        ],
        out_shape=[
            jax.ShapeDtypeStruct((rows, D_MODEL), BF16),
            jax.ShapeDtypeStruct((nseq, ML_HEADS, ML_DK, ML_DV), F32),
            jax.ShapeDtypeStruct((nseq, ML_HEADS, ML_DK), F32),
            jax.ShapeDtypeStruct((rows, BLK), F32),
            jax.ShapeDtypeStruct((nseq, WINDOW, BLK), F32),
            jax.ShapeDtypeStruct((nseq, WINDOW, BLK), F32),
        ],
        scratch_shapes=[
            pltpu.VMEM((rows, A_COLS), F32),
            pltpu.VMEM((rows, B_COLS), F32),
            pltpu.VMEM((SWA_HEADS * SLAB, BLK), F32),
            pltpu.VMEM((SWA_HEADS * SLAB, BLK), F32),
        ],
        compiler_params=pltpu.CompilerParams(
            dimension_semantics=("arbitrary",),
            vmem_limit_bytes=VMEM_LIMIT),
        name="sample_mixer",
    )(x3d, wt, wn, gb, mlg, sinks, rb, mrep, c0, n0, kc, vc)


def _dense_kernel(x_ref, mix_ref, xs_ref, mixs_ref, wo_ref, g1_ref, b1_ref, wfi_ref, wfo_ref, g2_ref, b2_ref,
                  y_ref, ys_ref):
    bounds = [b for b in DENSE_SPLITS if b <= x_ref.shape[0]]
    nsub = len(bounds) - 1
    subs = [slice(bounds[i], bounds[i + 1]) for i in range(nsub)]
    chunks = list(zip(FF_SPLITS[:-1], FF_SPLITS[1:]))
    g1, b1, g2, b2 = g1_ref[...], b1_ref[...], g2_ref[...], b2_ref[...]

    xin = [x_ref[s, :] for s in subs]
    mixin = [mix_ref[s, :] for s in subs]
    xs = xs_ref[...].reshape(mixs_ref.shape)
    xin[-1] = jnp.concatenate([xin[-1], xs], axis=0)
    mixin[-1] = jnp.concatenate([mixin[-1], mixs_ref[...]], axis=0)

    proj = [_dot(mixin[i], wo_ref[...]) for i in range(nsub)]
    h1 = [_layer_norm(ALPHA * xin[i] + proj[i], g1, b1) for i in range(nsub)]
    h1b = [h.astype(BF16) for h in h1]
    acc = [None] * nsub
    for c0, c1 in chunks:
        gate, up = [], []
        for i in range(nsub):
            gate.append(_dot(h1b[i], wfi_ref[:, c0:c1]))
            up.append(_dot(h1b[i], wfi_ref[:, D_FF + c0:D_FF + c1]))
        for i in range(nsub):
            act = (gate[i] * jax.nn.sigmoid(gate[i]) * up[i]).astype(BF16)
            part = _dot(act, wfo_ref[c0:c1, :])
            acc[i] = part if acc[i] is None else acc[i] + part
    for i, s in enumerate(subs):
        out = _layer_norm(ALPHA * h1[i] + acc[i], g2, b2)
        nrows = s.stop - s.start
        y_ref[s, :] = out[:nrows]
        if i == nsub - 1:
            ys_ref[...] = out[nrows:].reshape(ys_ref.shape)


def _dense(x2d, mix2d, xs3d, mixs2d, wo, g1, b1, wfi, wfo, g2, b2):
    tdec = xs3d.shape[1]
    rows = x2d.shape[0]
    tm = TM_DENSE
    steps = rows // tm
    srows = mixs2d.shape[0] // steps
    seq3 = pl.BlockSpec((srows // tdec, tdec, D_MODEL), lambda i: (i, 0, 0))
    const2 = lambda i: (0, 0)
    step2 = lambda i: (i, 0)

    def wspec(shape):
        return pl.BlockSpec(shape, const2, pipeline_mode=pl.Buffered(1))

    return pl.pallas_call(
        _dense_kernel,
        grid=(steps,),
        in_specs=[
            pl.BlockSpec((tm, D_MODEL), step2),
            pl.BlockSpec((tm, D_MODEL), step2),
            seq3,
            pl.BlockSpec((srows, D_MODEL), step2),
            wspec((D_MODEL, D_MODEL)),
            wspec((1, D_MODEL)), wspec((1, D_MODEL)),
            wspec((D_MODEL, 2 * D_FF)),
            wspec((D_FF, D_MODEL)),
            wspec((1, D_MODEL)), wspec((1, D_MODEL)),
        ],
        out_specs=[pl.BlockSpec((tm, D_MODEL), step2),
                   seq3],
        out_shape=[jax.ShapeDtypeStruct((rows, D_MODEL), F32),
                   jax.ShapeDtypeStruct(xs3d.shape, F32)],
        compiler_params=pltpu.CompilerParams(
            dimension_semantics=("arbitrary",),
            vmem_limit_bytes=VMEM_LIMIT),
        name="dense",
    )(x2d, mix2d, xs3d, mixs2d, wo, g1, b1, wfi, wfo, g2, b2)


def kernel(x_prompt, x_sample, state_mlstm_C, state_mlstm_n, state_mlstm_m, cache_swa_k, cache_swa_v,
           w_in, b_igate, b_fgate, ml_norm_g, swa_sinks, rel_bias, w_out, ln1_g, ln1_b,
           w_ffn_in, w_ffn_out, ln2_g, ln2_b):
    bp, seq, _ = x_prompt.shape
    bs, tdec, _ = x_sample.shape
    l = 0

    w16 = jnp.pad(w_in[l], ((0, 0), (0, W_IN_PAD - w_in.shape[-1]))).astype(BF16)
    gvec = jnp.concatenate([b_igate[l], b_fgate[l]]).astype(F32)
    gb = jnp.pad(gvec, (0, BLK - 2 * ML_HEADS))[None, :]
    gbc = jnp.broadcast_to(jnp.pad(gvec, (0, T_GROWS - 2 * ML_HEADS))[:, None], (T_GROWS, BLK))
    mlg = ml_norm_g[l][None, :].astype(F32)
    sinks = swa_sinks[l].astype(F32)
    rb = rel_bias.astype(F32)
    g1, b1 = ln1_g[l][None, :], ln1_b[l][None, :]
    g2, b2 = ln2_g[l][None, :], ln2_b[l][None, :]

    mix_p, c_p, n_p, m_p, k_p, v_p, wt, wn, wo, wfi, wfo = _prompt_mixer(
        x_prompt, w16, gbc, mlg, sinks, rb, w_out[l], w_ffn_in[l], w_ffn_out[l])
    p_m =m_p[:, :ML_HEADS // 2, ::BLK].reshape(bp, ML_HEADS)[None]
    p_k = k_p.reshape(1, bp, WINDOW, SWA_KV_HEADS, SWA_HD)
    p_v = v_p.reshape(1, bp, WINDOW, SWA_KV_HEADS, SWA_HD)

    m0 =state_mlstm_m[l].astype(F32)
    mrep = jnp.pad(jnp.repeat(m0, tdec, axis=0), ((0, 0), (ML_HEADS, BLK - 2 * ML_HEADS)))
    wlen = cache_swa_k.shape[2]
    kc = cache_swa_k[l].reshape(bs, wlen, SWA_KV_HEADS * SWA_HD)
    vc = cache_swa_v[l].reshape(bs, wlen, SWA_KV_HEADS * SWA_HD)
    mix_s, c_s, n_s, mo_s, k_s, v_s = _sample_mixer(
        x_sample, wt, wn, gb, mlg, sinks, rb, mrep,
        state_mlstm_C[l].astype(F32), state_mlstm_n[l].astype(F32), kc, vc)
    s_m =mo_s.reshape(bs, tdec, BLK)[:, 0, ML_HEADS:2 * ML_HEADS][None]
    s_k = k_s.reshape(1, bs, wlen, SWA_KV_HEADS, SWA_HD)
    s_v = v_s.reshape(1, bs, wlen, SWA_KV_HEADS, SWA_HD)

    y_p, y_s = _dense(x_prompt.reshape(bp * seq, D_MODEL), mix_p.reshape(bp * seq, D_MODEL), x_sample, mix_s,
                      wo, g1, b1, wfi, wfo, g2, b2)
    y_p = y_p.reshape(bp, seq, D_MODEL)

    return (y_p, y_s, c_p[None], n_p[None], p_m, p_k, p_v,
            c_s[None], n_s[None], s_m, s_k, s_v)
```

```python
import functools
import math

import jax
import jax.numpy as jnp
from jax import lax
from jax.experimental import pallas as pl
from jax.experimental.pallas import tpu as pltpu

F32 = jnp.float32
BF16 = jnp.bfloat16

D_MODEL = 1024
ML_HEADS = 4
ML_DK = 128
ML_DV = 128
ML_WIDTH = ML_HEADS * ML_DV
GATE_SOFTCAP = 15.0
SWA_HEADS = 8
SWA_KV_HEADS = 2
SWA_GROUP = SWA_HEADS // SWA_KV_HEADS
SWA_HD = 64
SWA_WIDTH = SWA_HEADS * SWA_HD
WINDOW = 128
REL_BUCKETS = 32
REL_MAX_DIST = 128
D_FF = 2816
DEPTH = 1
ALPHA = (2.0 * DEPTH) ** 0.25
LN_EPS = 1e-5
NORM_EPS = 1e-6
NEG_INF = -1e30
IN_SIZES = (512, 512, 512, 512, 4, 4, 512, 128, 128)
ML_SCALE = ML_DK ** -0.5
LOG_ML_SCALE = math.log(ML_SCALE)
SWA_SCALE = SWA_HD ** -0.5

BLK = 128
SUBLANES_F32 = 8
SUBLANES_BF16 = 16

A_Q, A_K, A_V, A_SQ = 0, 512, 1024, 1536
A_COLS = 2048
B_MO, B_SK, B_SV, B_G = 0, 512, 640, 768
B_COLS = 896

N_K = 0
N_B = 512
T_Q, T_V, T_SQ, T_SV, T_G = 0, 512, 1024, 1536, 1664
T_GROWS = SUBLANES_BF16
T_COLS = T_G + BLK
W_IN_PAD = -(-sum(IN_SIZES) // BLK) * BLK
STATE_ROWS = ML_DV + SUBLANES_BF16
M_ROWS = SUBLANES_F32

TM_PROMPT = 1024
TM_DENSE = 1024
DENSE_SPLITS = (0, 384, 768, 1024)
FF_SPLITS = (0, 1536, D_FF)
SAMPLE_GB = 8
SLAB = 16
VMEM_LIMIT = 56 * 1024 * 1024


def _softcap(a):
    return GATE_SOFTCAP * jnp.tanh(a / GATE_SOFTCAP)


def _log_sigmoid(x):
    return jnp.minimum(x, 0.0) - jnp.log1p(jnp.exp(-jnp.abs(x)))


def _layer_norm(z, g, b):
    mu = jnp.mean(z, axis=-1, keepdims=True)
    zc = z - mu
    var = jnp.mean(zc * zc, axis=-1, keepdims=True)
    return zc * lax.rsqrt(var + LN_EPS) * g + b


def _dot(a, b):
    return jnp.dot(a, b, preferred_element_type=F32)


def _dot_nt(a, b):
    return lax.dot_general(a, b, (((1,), (1,)), ((), ())), preferred_element_type=F32)


def _dot_tn(a, b):
    return lax.dot_general(a, b, (((0,), (0,)), ((), ())), preferred_element_type=F32)


def _split3(x):
    hi = x.astype(BF16)
    r1 = x - hi.astype(F32)
    mid = r1.astype(BF16)
    lo = (r1 - mid.astype(F32)).astype(BF16)
    return hi, mid, lo


def _t5_bucket(d):
    n = jnp.maximum(d, 0)
    max_exact = REL_BUCKETS // 2
    nlog = REL_BUCKETS - max_exact
    large = jnp.full(n.shape, max_exact, jnp.int32)
    for k in range(1, nlog):
        thr = math.ceil(max_exact * (REL_MAX_DIST / max_exact) ** (k / nlog))
        large = large + jnp.where(n >= thr, 1, 0)
    return jnp.where(n < max_exact, n, large)


def _prompt_mixer_kernel(x_ref, w_ref, gbc_ref, mlg_ref, sinks_ref, rb_ref, wo32, wfi32, wfo32,
                         mix_ref, c_out, n_out, m_out, ko_ref, vo_ref,
                         wt, wn, wo16, wfi16, wfo16,
                         kn, pf, vlast, kband, q_t, v_t, sq_t, sv_t, g_t, ct, msc, tbl):
    wo16[...] = wo32[...].astype(BF16)
    wfi16[...] = wfi32[...].astype(BF16)
    wfo16[...] = wfo32[...].astype(BF16)

    b = pl.program_id(0)
    t = pl.program_id(1)
    nt = pl.num_programs(1)
    tm = x_ref.shape[1]
    nblk = tm // BLK
    pairs = SWA_HEADS // 2

    @pl.when((b == 0) & (t == 0))
    def _prepare_weights():
        offs = [0]
        for n in IN_SIZES:
            offs.append(offs[-1] + n)
        o_mq, o_mk, o_mv, o_mo, o_mi, _, o_sq, o_sk, o_sv = offs[:-1]

        def col_blocks(dst_col, src_col, n):
            for i in range(n // BLK):
                wn[:, dst_col + i * BLK:dst_col + (i + 1) * BLK] = (
                    w_ref[:, src_col + i * BLK:src_col + (i + 1) * BLK])

        def row_blocks(dst_row, src_col, n):
            for i in range(n // BLK):
                wt[dst_row + i * BLK:dst_row + (i + 1) * BLK, :] = (
                    w_ref[:, src_col + i * BLK:src_col + (i + 1) * BLK].T)

        col_blocks(N_K, o_mk, ML_HEADS * ML_DK)
        col_blocks(N_B + B_MO, o_mo, ML_WIDTH)
        col_blocks(N_B + B_SK, o_sk, SWA_KV_HEADS * SWA_HD)
        row_blocks(T_Q, o_mq, ML_HEADS * ML_DK)
        row_blocks(T_V, o_mv, ML_WIDTH)
        row_blocks(T_SQ, o_sq, SWA_WIDTH)
        row_blocks(T_SV, o_sv, SWA_KV_HEADS * SWA_HD)
        gates_t = w_ref[:, o_mi:o_mi + BLK].T
        grow = lax.broadcasted_iota(jnp.int32, gates_t.shape, 0)
        wt[T_G:T_G + BLK, :] = jnp.where(grow < 2 * ML_HEADS, gates_t, jnp.zeros_like(gates_t))

    @pl.when((b == 0) & (t == 0))
    def _build_bias_tables():
        r = lax.broadcasted_iota(jnp.int32, (BLK, 2 * BLK), 0)
        ln = lax.broadcasted_iota(jnp.int32, (BLK, 2 * BLK), 1)
        second = ln >= BLK
        qi = jnp.where(second, ln - BLK, ln)
        prev = r > qi
        d = jnp.where(prev, WINDOW + qi - r, qi - r)
        bucket = _t5_bucket(d)
        for p in range(pairs):
            acc = jnp.zeros((BLK, 2 * BLK), F32)
            for k in range(REL_BUCKETS):
                acc = jnp.where(bucket == k, jnp.where(second, rb_ref[k, p + pairs], rb_ref[k, p]), acc)
            tbl[0, p] = jnp.where(prev, NEG_INF, acc)
            tbl[1, p] = acc

    @pl.when(t == 0)
    def _reset_state():
        ct[...] = jnp.zeros_like(ct)
        msc[...] = jnp.zeros_like(msc)
        kband[0:BLK, :] = jnp.zeros((BLK, kband.shape[1]), kband.dtype)
        sv_t[0] = jnp.zeros(sv_t.shape[1:], sv_t.dtype)

    xb = x_ref[0].astype(BF16)
    res_n = _dot(xb, wn[...])
    kn[...] = res_n[:, N_K:N_K + ML_HEADS * ML_DK].astype(BF16)
    pf[...] = res_n[:, N_B:N_B + B_SV]
    kband[BLK:BLK + tm, :] = pf[:, B_SK:B_SK + BLK].astype(BF16)

    res_t = _dot_nt(wt[0:T_G + T_GROWS, :], xb)
    vlast[...] = res_t[T_SV:T_SV + BLK, tm - BLK:tm]

    def put(dst, r0, nrows, off, scale=None):
        res = res_t[r0:r0 + nrows, :]
        if scale is not None:
            res = res * scale
        for jj in range(nblk):
            dst[jj + off] = res[:, jj * BLK:(jj + 1) * BLK].astype(dst.dtype)

    put(q_t, T_Q, ML_HEADS * ML_DK, 0)
    put(v_t, T_V, ML_WIDTH, 0)
    put(sq_t, T_SQ, SWA_WIDTH, 0, SWA_SCALE)
    put(sv_t, T_SV, SWA_KV_HEADS * SWA_HD, 1)
    put(g_t, T_G, T_GROWS, 0)

    row = lax.broadcasted_iota(jnp.int32, (BLK, BLK), 0)
    col = lax.broadcasted_iota(jnp.int32, (BLK, BLK), 1)
    causal_t = row <= col
    triu = causal_t.astype(BF16)
    causal2 = jnp.concatenate([causal_t, causal_t], axis=1)
    r16 = lax.broadcasted_iota(jnp.int32, (STATE_ROWS - ML_DV, 2 * BLK), 0)
    ones_rows = (r16 == 0).astype(BF16)
    z128 = jnp.zeros((BLK, BLK), BF16)
    ml_pairs = ML_HEADS // 2
    zero_rows = jnp.zeros((BLK - T_GROWS, BLK), F32)
    gbc = gbc_ref[...]
    mlg = mlg_ref[...]
    lane2 = lax.broadcasted_iota(jnp.int32, (1, 2 * BLK), 1)
    snk_rows = [jnp.where(lane2 >= BLK, sinks_ref[p + pairs], sinks_ref[p]) for p in range(pairs)]
    zq = jnp.zeros((SWA_HD, BLK), BF16)
    prev2 = jnp.concatenate([row > col, row > col], axis=1)

    def pair_row(x0, x1):
        return jnp.concatenate([jnp.broadcast_to(x0, (1, BLK)), jnp.broadcast_to(x1, (1, BLK))], axis=1)

    def block_diag(x0, x1):
        return jnp.concatenate([jnp.concatenate([x0, z128], axis=1),
                                jnp.concatenate([z128, x1], axis=1)], axis=0)

    def block(j):
        r0 = j * BLK
        rows = pl.ds(r0, BLK)
        hsl = [slice(h * ML_DK, (h + 1) * ML_DK) for h in range(ML_HEADS)]
        psl = [slice(p * 2 * ML_DK, (p + 1) * 2 * ML_DK) for p in range(ml_pairs)]

        pre_t = _softcap(g_t[j] + gbc)
        hi, mid, lo = _split3(_log_sigmoid(pre_t))
        b_t = _dot(hi, triu) + _dot(mid, triu) + _dot(lo, triu)
        u_t = pltpu.roll(pre_t, ML_HEADS, 0) - b_t
        u_c = jnp.concatenate([u_t, zero_rows], axis=0).T

        k2 = [kn[rows, psl[p]] for p in range(ml_pairs)]
        bdq = [block_diag(q_t[j, hsl[2 * p], :], q_t[j, hsl[2 * p + 1], :]) for p in range(ml_pairs)]
        vte = [jnp.concatenate([jnp.concatenate([v_t[j, hsl[2 * p], :], v_t[j, hsl[2 * p + 1], :]], axis=1),
                                ones_rows], axis=0) for p in range(ml_pairs)]
        kq = [_dot(k2[p], bdq[p]) for p in range(ml_pairs)]

        sel = jnp.where(t == 0, 0, 1) if j == 0 else 1
        kb = kband[pl.ds(r0, 2 * BLK), :]
        vtb = jnp.concatenate([sv_t[j], sv_t[j + 1]], axis=1)
        sc2 = []
        for p in range(pairs):
            qa = sq_t[j, p * SWA_HD:(p + 1) * SWA_HD, :]
            qb = sq_t[j, (p + pairs) * SWA_HD:(p + pairs + 1) * SWA_HD, :]
            bd = jnp.concatenate([jnp.concatenate([qa, zq], axis=1),
                                  jnp.concatenate([zq, qb], axis=1)], axis=0)
            sc2.append(_dot(kb, bd))

        yield
        bds, scl, m_row, m_new, a_p, vwte = [], [], [], [], [], []
        for p in range(ml_pairs):
            h0, h1 = 2 * p, 2 * p + 1
            b_row = pair_row(b_t[ML_HEADS + h0:ML_HEADS + h0 + 1, :], b_t[ML_HEADS + h1:ML_HEADS + h1 + 1, :])
            i_row = pair_row(pre_t[h0:h0 + 1, :], pre_t[h1:h1 + 1, :])
            u_col = jnp.concatenate(
                [jnp.broadcast_to(u_c[:, ML_HEADS + h0:ML_HEADS + h0 + 1], (BLK, BLK)),
                 jnp.broadcast_to(u_c[:, ML_HEADS + h1:ML_HEADS + h1 + 1], (BLK, BLK))], axis=1)
            m_prev = msc[p:p + 1, :]
            log_d = jnp.where(causal2, b_row + u_col, -jnp.inf)
            inter = b_row + m_prev
            mr = jnp.maximum(inter, jnp.max(log_d, axis=0, keepdims=True))
            m_row.append(mr)
            scl.append(jnp.exp(inter - mr))
            s2 = (kq[p] * jnp.exp(log_d - (mr - LOG_ML_SCALE))).astype(BF16)
            bds.append(block_diag(s2[:, :BLK], s2[:, BLK:]))
            b_last = pair_row(b_row[:, BLK - 1:BLK], b_row[:, 2 * BLK - 1:2 * BLK])
            w_row = b_last - b_row + i_row
            w_max = pair_row(jnp.max(w_row[:, :BLK], axis=1, keepdims=True),
                             jnp.max(w_row[:, BLK:], axis=1, keepdims=True))
            mn = jnp.maximum(b_last + m_prev, w_max)
            m_new.append(mn)
            a_p.append(jnp.exp(b_last + m_prev - mn))
            wexp = jnp.exp(w_row - mn) * ML_SCALE
            vwte.append(vte[p] * wexp.astype(BF16))

        pw, den_p = [], []
        for p in range(pairs):
            sc = jnp.where(prev2, sc2[p][:BLK], sc2[p][BLK:]) + tbl[sel, p]
            snk = snk_rows[p]
            mx = jnp.maximum(jnp.max(sc, axis=0, keepdims=True), snk)
            pe = jnp.exp(sc - mx)
            den_p.append(jnp.sum(pe, axis=0, keepdims=True) + jnp.exp(snk - mx))
            pw.append(jnp.concatenate([jnp.where(prev2, pe, 0.0), jnp.where(prev2, 0.0, pe)],
                                      axis=0).astype(BF16))

        c_old = [ct[p] for p in range(ml_pairs)]
        cq = [_dot(c_old[p].astype(BF16), bdq[p]) for p in range(ml_pairs)]
        vs = [_dot(vte[p], bds[p]) for p in range(ml_pairs)]
        upd = [_dot(vwte[p], block_diag(k2[p][:, :BLK], k2[p][:, BLK:])) for p in range(ml_pairs)]
        o2 = [_dot(vtb, pw[p]) for p in range(pairs)]

        for p in range(ml_pairs):
            nd = scl[p] * cq[p] + vs[p]
            num = nd[:ML_DV]
            den = nd[ML_DV:ML_DV + 1]
            hh = num / jnp.maximum(jnp.abs(den), jnp.exp(-m_row[p]))
            hn = hh * lax.rsqrt(jnp.mean(hh * hh, axis=0, keepdims=True) + NORM_EPS)
            for half in range(2):
                h = 2 * p + half
                og = jax.nn.sigmoid(pf[rows, B_MO + h * ML_DV:B_MO + (h + 1) * ML_DV])
                hn_h = hn[:, half * BLK:(half + 1) * BLK].T
                mix_ref[0, rows, hsl[h]] = (hn_h * mlg[:, hsl[h]] * og).astype(mix_ref.dtype)
            ct[p] = a_p[p] * c_old[p] + upd[p]
            msc[p:p + 1, :] = m_new[p]

        o_parts = [None] * SWA_HEADS
        for p in range(pairs):
            on = o2[p] / den_p[p]
            o_parts[p] = on[0:SWA_HD, 0:BLK]
            o_parts[p + pairs] = on[SWA_HD:2 * SWA_HD, BLK:2 * BLK]
        o_t = jnp.concatenate(o_parts, axis=0)
        mix_ref[0, rows, ML_WIDTH:ML_WIDTH + SWA_WIDTH] = o_t.T.astype(mix_ref.dtype)
        yield

    blocks = [block(j) for j in range(nblk)]
    for g in blocks:
        next(g)
    for g in blocks:
        next(g)

    kband[0:BLK, :] = kband[tm:tm + BLK, :]
    sv_t[0] = sv_t[nblk]

    @pl.when(t == nt - 1)
    def _write_state():
        for h in range(ML_HEADS):
            p, lanes = h // 2, slice((h % 2) * ML_DK, (h % 2 + 1) * ML_DK)
            c_out[0, h] = ct[p, 0:ML_DV, lanes].T
            n_out[0, h:h + 1, :] = ct[p, ML_DV:ML_DV + 1, lanes]
        m_out[0] = msc[...]
        ko_ref[0] = pf[tm - BLK:tm, B_SK:B_SK + BLK]
        vo_ref[0] = vlast[...].T


def _prompt_mixer(x, w_in_l, gbc, mlg, sinks, rb, wo32, wfi32, wfo32):
    bsz, seq, _ = x.shape
    tm = TM_PROMPT
    nt = seq // tm
    nblk = tm // BLK
    steps = bsz * nt
    const2 = lambda b, t: (0, 0)
    smem = pl.BlockSpec(memory_space=pltpu.SMEM)

    def slab_spec(wmat):
        return pl.BlockSpec((wmat.shape[0] // steps, wmat.shape[1]), lambda b, t: (b * nt + t, 0))

    return pl.pallas_call(
        _prompt_mixer_kernel,
        grid=(bsz, nt),
        in_specs=[
            pl.BlockSpec((1, tm, D_MODEL), lambda b, t: (b, t, 0)),
            pl.BlockSpec((D_MODEL, W_IN_PAD), const2, pipeline_mode=pl.Buffered(1)),
            pl.BlockSpec((T_GROWS, BLK), const2),
            pl.BlockSpec((1, ML_WIDTH), const2),
            smem, smem,
            slab_spec(wo32), slab_spec(wfi32), slab_spec(wfo32),
        ],
        out_specs=[
            pl.BlockSpec((1, tm, D_MODEL), lambda b, t: (b, t, 0)),
            pl.BlockSpec((1, ML_HEADS, ML_DK, ML_DV), lambda b, t: (b, 0, 0, 0)),
            pl.BlockSpec((1, ML_HEADS, ML_DK), lambda b, t: (b, 0, 0)),
            pl.BlockSpec((1, M_ROWS, 2 * BLK), lambda b, t: (b, 0, 0)),
            pl.BlockSpec((1, BLK, BLK), lambda b, t: (b, 0, 0)),
            pl.BlockSpec((1, BLK, BLK), lambda b, t: (b, 0, 0)),
            pl.BlockSpec((T_COLS, D_MODEL), const2),
            pl.BlockSpec((D_MODEL, N_B + B_SV), const2),
            slab_spec(wo32), slab_spec(wfi32), slab_spec(wfo32),
        ],
        out_shape=[
            jax.ShapeDtypeStruct((bsz, seq, D_MODEL), BF16),
            jax.ShapeDtypeStruct((bsz, ML_HEADS, ML_DK, ML_DV), F32),
            jax.ShapeDtypeStruct((bsz, ML_HEADS, ML_DK), F32),
            jax.ShapeDtypeStruct((bsz, M_ROWS, 2 * BLK), F32),
            jax.ShapeDtypeStruct((bsz, BLK, BLK), F32),
            jax.ShapeDtypeStruct((bsz, BLK, BLK), F32),
            jax.ShapeDtypeStruct((T_COLS, D_MODEL), BF16),
            jax.ShapeDtypeStruct((D_MODEL, N_B + B_SV), BF16),
            jax.ShapeDtypeStruct(wo32.shape, BF16),
            jax.ShapeDtypeStruct(wfi32.shape, BF16),
            jax.ShapeDtypeStruct(wfo32.shape, BF16),
        ],
        scratch_shapes=[
            pltpu.VMEM((tm, ML_HEADS * ML_DK), BF16),
            pltpu.VMEM((tm, B_SV), F32),
            pltpu.VMEM((BLK, BLK), F32),
            pltpu.VMEM((tm + BLK, BLK), BF16),
            pltpu.VMEM((nblk, ML_HEADS * ML_DK, BLK), BF16),
            pltpu.VMEM((nblk, ML_WIDTH, BLK), BF16),
            pltpu.VMEM((nblk, SWA_WIDTH, BLK), BF16),
            pltpu.VMEM((nblk + 1, SWA_KV_HEADS * SWA_HD, BLK), BF16),
            pltpu.VMEM((nblk, T_GROWS, BLK), F32),
            pltpu.VMEM((ML_HEADS // 2, STATE_ROWS, 2 * ML_DK), F32),
            pltpu.VMEM((M_ROWS, 2 * BLK), F32),
            pltpu.VMEM((2, SWA_HEADS // 2, BLK, 2 * BLK), F32),
        ],
        compiler_params=pltpu.CompilerParams(
            dimension_semantics=("arbitrary", "arbitrary"),
            vmem_limit_bytes=VMEM_LIMIT),
        name="prompt_mixer",
    )(x, w_in_l, gbc, mlg, sinks, rb, wo32, wfi32, wfo32)


def _roll_rows(x, shift):
    return pltpu.roll(x, shift % x.shape[0], 0)


def _sample_mixer_kernel(x_ref, wt_ref, wn_ref, gb_ref, mlg_ref, sinks_ref, rb_ref, mrep_ref,
                         c_ref, n_ref, kc_ref, vc_ref,
                         mix_ref, c_out, n_out, mo_ref, ko_ref, vo_ref,
                         pa, pf, tblc, tbln):
    step = pl.program_id(0)
    tdec = 4
    rows_step = mix_ref.shape[0]
    nslab = rows_step // SLAB
    nseq = SLAB // tdec
    qrows = SWA_HEADS * SLAB

    @pl.when(step == 0)
    def _project_all_rows():
        xb = x_ref[...].reshape(pa.shape[0], D_MODEL).astype(BF16)
        pa[:, A_Q:A_K] = _dot_nt(xb, wt_ref[T_Q:T_Q + ML_HEADS * ML_DK, :])
        pa[:, A_K:A_V] = _dot(xb, wn_ref[:, N_K:N_K + ML_HEADS * ML_DK])
        pa[:, A_V:A_COLS] = _dot_nt(xb, wt_ref[T_V:T_V + ML_WIDTH + SWA_WIDTH, :])
        pf[:, B_MO:B_SV] = _dot(xb, wn_ref[:, N_B:N_B + B_SV])
        pf[:, B_SV:B_COLS] = _dot_nt(xb, wt_ref[T_SV:T_COLS, :])

    @pl.when(step == 0)
    def _build_bias_tables():
        r = lax.broadcasted_iota(jnp.int32, (qrows, BLK), 0)
        c = lax.broadcasted_iota(jnp.int32, (qrows, BLK), 1)
        hq = r // SLAB
        bl = (r % SLAB) // tdec
        tq = r % tdec
        d_cache = WINDOW + tq - c
        d_new = tq - (c % tdec)
        key_cache = _t5_bucket(d_cache) * SWA_HEADS + hq
        key_new = _t5_bucket(d_new) * SWA_HEADS + hq
        acc_c = jnp.zeros((qrows, BLK), F32)
        acc_n = jnp.zeros((qrows, BLK), F32)
        for k in range(REL_BUCKETS):
            for h in range(SWA_HEADS):
                val = rb_ref[k, h]
                acc_c = jnp.where(key_cache == k * SWA_HEADS + h, val, acc_c)
                acc_n = jnp.where(key_new == k * SWA_HEADS + h, val, acc_n)
        tblc[...] = jnp.where(c > tq, acc_c, NEG_INF)
        tbln[...] = jnp.where((c < SLAB) & ((c // tdec) == bl) & (d_new >= 0), acc_n, NEG_INF)

    gbias = gb_ref[...]
    mlg = mlg_ref[...]
    r16 = lax.broadcasted_iota(jnp.int32, (SLAB, BLK), 0)
    rr = r16 % tdec
    bl16 = r16 // tdec
    rq = lax.broadcasted_iota(jnp.int32, (qrows, BLK), 0)
    blq = (rq % SLAB) // tdec
    hq_col = rq[:, 0:1] // SLAB
    snk = jnp.zeros((qrows, 1), F32)
    for h in range(SWA_HEADS):
        snk = jnp.where(hq_col == h, sinks_ref[h], snk)
    first_group = rq[:, 0:SWA_HD] < SWA_GROUP * SLAB
    zq = jnp.zeros((SLAB, SWA_HD), F32)
    zeros_pad = jnp.zeros((BLK - SLAB, BLK), BF16)

    def seg_last(x):
        return jnp.where(rr == 3, x,
                         jnp.where(rr == 2, _roll_rows(x, -1),
                                   jnp.where(rr == 1, _roll_rows(x, -2), _roll_rows(x, -3))))

    def seg_max(x):
        m1 = jnp.maximum(x, jnp.where(rr % 2 == 0, _roll_rows(x, -1), _roll_rows(x, 1)))
        return jnp.maximum(m1, jnp.where(rr < 2, _roll_rows(m1, -2), _roll_rows(m1, 2)))

    def slab(si):
        rows = pl.ds(si * SLAB, SLAB)
        prow = pl.ds(pl.multiple_of(step * rows_step + si * SLAB, SLAB), SLAB)
        seq0 = si * nseq

        pre = _softcap(pf[prow, B_G:B_G + BLK] + gbias)
        logf = _log_sigmoid(pre)
        y = logf + jnp.where(rr >= 1, _roll_rows(logf, 1), 0.0)
        bcum = y + jnp.where(rr >= 2, _roll_rows(y, 2), 0.0)
        ig = pltpu.roll(pre, ML_HEADS, 1)
        m_prev = mrep_ref[rows, :]
        b_minus_i = bcum - ig
        log_d = [jnp.where(rr >= dl, bcum - _roll_rows(b_minus_i, dl), -jnp.inf) for dl in range(tdec)]
        rowmax = functools.reduce(jnp.maximum, log_d)
        inter = bcum + m_prev
        m_row = jnp.maximum(inter, rowmax)
        scl = jnp.exp(inter - m_row)
        dm = [jnp.exp(ld - m_row) for ld in log_d]
        enm = jnp.exp(-m_row)
        b_last = seg_last(bcum)
        w = b_last - bcum + ig
        m_new = jnp.maximum(b_last + m_prev, seg_max(w))
        a = jnp.exp(b_last + m_prev - m_new)
        kws = jnp.exp(w - m_new) * ML_SCALE
        mo_ref[rows, :] = m_new

        q = [pa[prow, A_Q + h * ML_DK:A_Q + (h + 1) * ML_DK] for h in range(ML_HEADS)]
        k = [pa[prow, A_K + h * ML_DK:A_K + (h + 1) * ML_DK] for h in range(ML_HEADS)]
        v = [pa[prow, A_V + h * ML_DV:A_V + (h + 1) * ML_DV] for h in range(ML_HEADS)]
        c_b = [[c_ref[seq0 + bl, h] for bl in range(nseq)] for h in range(ML_HEADS)]
        qc, upd, kw = [], [], []
        for h in range(ML_HEADS):
            qb = q[h].astype(BF16)
            res = [_dot(qb, jnp.concatenate([c_b[h][2 * i].astype(BF16), c_b[h][2 * i + 1].astype(BF16)], axis=1))
                   for i in range(nseq // 2)]
            parts = [res[bl // 2][:, (bl % 2) * ML_DV:(bl % 2 + 1) * ML_DV] for bl in range(nseq)]
            qc_h = parts[nseq - 1]
            for bl in range(nseq - 2, -1, -1):
                qc_h = jnp.where(bl16 == bl, parts[bl], qc_h)
            qc.append(qc_h)

        knew = pf[prow, B_SK:B_SK + BLK]
        vnew = pf[prow, B_SV:B_SV + BLK]
        qparts = []
        for hq in range(SWA_HEADS):
            qh = pa[prow, A_SQ + hq * SWA_HD:A_SQ + (hq + 1) * SWA_HD]
            qparts.append(jnp.concatenate([qh, zq] if hq < SWA_GROUP else [zq, qh], axis=1))
        qs = jnp.concatenate(qparts, axis=0).astype(BF16)
        s_c = _dot_nt(qs, kc_ref[seq0 + nseq - 1].astype(BF16))
        for bl in range(nseq - 2, -1, -1):
            s_c = jnp.where(blq == bl, _dot_nt(qs, kc_ref[seq0 + bl].astype(BF16)), s_c)
        kn = jnp.concatenate([knew.astype(BF16), zeros_pad], axis=0)
        s_n = _dot_nt(qs, kn)

        yield
        for h in range(ML_HEADS):
            lane = ML_HEADS + h
            kw_h = k[h] * kws[:, lane:lane + 1]
            kw.append(kw_h)
            vm = jnp.concatenate([jnp.where(bl16 == bl, v[h], 0.0) for bl in range(nseq)], axis=1)
            upd.append(_dot_tn(kw_h.astype(BF16), vm.astype(BF16)))

        yield
        for h in range(ML_HEADS):
            hs = slice(h * ML_DK, (h + 1) * ML_DK)
            lane = ML_HEADS + h
            bm = jnp.zeros((SLAB, ML_DV), F32)
            sum_s = jnp.zeros((SLAB, 1), F32)
            for dl in range(tdec):
                s_dl = jnp.sum(q[h] * _roll_rows(k[h], dl), axis=-1, keepdims=True)
                ws = s_dl * ML_SCALE * dm[dl][:, lane:lane + 1]
                bm = bm + ws * _roll_rows(v[h], dl)
                sum_s = sum_s + ws
            nsel = jnp.zeros((SLAB, ML_DK), F32)
            for bl in range(nseq):
                msk = bl16 == bl
                n_b = n_ref[seq0 + bl][h:h + 1, :]
                nsel = jnp.where(msk, n_b, nsel)
                a_b = a[bl * tdec:bl * tdec + 1, lane:lane + 1]
                c_out[seq0 + bl, h] = a_b * c_b[h][bl] + upd[h][:, bl * ML_DV:(bl + 1) * ML_DV]
                n_out[seq0 + bl, h:h + 1, :] = a_b * n_b + jnp.sum(jnp.where(msk, kw[h], 0.0), axis=0,
                                                                   keepdims=True)
            sclh = scl[:, lane:lane + 1]
            num = sclh * qc[h] + bm
            den = sclh * jnp.sum(q[h] * nsel, axis=-1, keepdims=True) + sum_s
            hh = num / jnp.maximum(jnp.abs(den), enm[:, lane:lane + 1])
            hn = hh * lax.rsqrt(jnp.mean(hh * hh, axis=-1, keepdims=True) + NORM_EPS)
            og = jax.nn.sigmoid(pf[prow, B_MO + h * ML_DV:B_MO + (h + 1) * ML_DV])
            mix_ref[rows, hs] = (hn * mlg[:, hs] * og).astype(mix_ref.dtype)

        s_c = s_c * SWA_SCALE + tblc[...]
        s_n = s_n * SWA_SCALE + tbln[...]
        mx = jnp.maximum(jnp.maximum(jnp.max(s_c, axis=-1, keepdims=True),
                                     jnp.max(s_n, axis=-1, keepdims=True)), snk)
        p_c = jnp.exp(s_c - mx)
        p_n = jnp.exp(s_n - mx)
        den = (jnp.sum(p_c, axis=-1, keepdims=True) + jnp.sum(p_n, axis=-1, keepdims=True)
               + jnp.exp(snk - mx))

        yield
        vn = jnp.concatenate([vnew.astype(BF16), zeros_pad], axis=0)
        o = _dot(p_n.astype(BF16), vn)
        for bl in range(nseq):
            o = o + _dot(jnp.where(blq == bl, p_c, 0.0).astype(BF16), vc_ref[seq0 + bl].astype(BF16))
        o = o / den
        osel = jnp.where(first_group, o[:, 0:SWA_HD], o[:, SWA_HD:2 * SWA_HD])
        og = jnp.concatenate([osel[hq * SLAB:(hq + 1) * SLAB, :] for hq in range(SWA_HEADS)], axis=1)
        mix_ref[rows, ML_WIDTH:ML_WIDTH + SWA_WIDTH] = og.astype(mix_ref.dtype)

        for bl in range(nseq):
            ko_ref[seq0 + bl, 0:WINDOW - tdec, :] = kc_ref[seq0 + bl, tdec:WINDOW, :]
            vo_ref[seq0 + bl, 0:WINDOW - tdec, :] = vc_ref[seq0 + bl, tdec:WINDOW, :]
            ko_ref[seq0 + bl, WINDOW - tdec:WINDOW, :] = knew[bl * tdec:(bl + 1) * tdec, :]
            vo_ref[seq0 + bl, WINDOW - tdec:WINDOW, :] = vnew[bl * tdec:(bl + 1) * tdec, :]
        yield

    slabs = [slab(si) for si in range(nslab)]
    for _ in range(4):
        for s in slabs:
            next(s)


def _sample_mixer(x3d, wt, wn, gb, mlg, sinks, rb, mrep, c0, n0, kc, vc):
    nseq, tdec, _ = x3d.shape
    rows = nseq * tdec
    gbs = SAMPLE_GB
    steps = nseq // gbs
    rb_rows = gbs * tdec
    const2 = lambda i: (0, 0)
    smem = pl.BlockSpec(memory_space=pltpu.SMEM)
    return pl.pallas_call(
        _sample_mixer_kernel,
        grid=(steps,),
        in_specs=[
            pl.BlockSpec((nseq, tdec, D_MODEL), lambda i: (0, 0, 0), pipeline_mode=pl.Buffered(1)),
            pl.BlockSpec((T_COLS, D_MODEL), const2, pipeline_mode=pl.Buffered(1)),
            pl.BlockSpec((D_MODEL, N_B + B_SV), const2, pipeline_mode=pl.Buffered(1)),
            pl.BlockSpec((1, BLK), const2),
            pl.BlockSpec((1, ML_WIDTH), const2),
            smem, smem,
            pl.BlockSpec((rb_rows, BLK), lambda i: (i, 0)),
            pl.BlockSpec((gbs, ML_HEADS, ML_DK, ML_DV), lambda i: (i, 0, 0, 0)),
            pl.BlockSpec((gbs, ML_HEADS, ML_DK), lambda i: (i, 0, 0)),
            pl.BlockSpec((gbs, WINDOW, BLK), lambda i: (i, 0, 0)),
            pl.BlockSpec((gbs, WINDOW, BLK), lambda i: (i, 0, 0)),
        ],
        out_specs=[
            pl.BlockSpec((rb_rows, D_MODEL), lambda i: (i, 0)),
            pl.BlockSpec((gbs, ML_HEADS, ML_DK, ML_DV), lambda i: (i, 0, 0, 0)),
            pl.BlockSpec((gbs, ML_HEADS, ML_DK), lambda i: (i, 0, 0)),
            pl.BlockSpec((rb_rows, BLK), lambda i: (i, 0)),
            pl.BlockSpec((gbs, WINDOW, BLK), lambda i: (i, 0, 0)),
            pl.BlockSpec((gbs, WINDOW, BLK), lambda i: (i, 0, 0)),
        ],
        out_shape=[
            jax.ShapeDtypeStruct((rows, D_MODEL), BF16),
            jax.ShapeDtypeStruct((nseq, ML_HEADS, ML_DK, ML_DV), F32),
            jax.ShapeDtypeStruct((nseq, ML_HEADS, ML_DK), F32),
            jax.ShapeDtypeStruct((rows, BLK), F32),
            jax.ShapeDtypeStruct((nseq, WINDOW, BLK), F32),
            jax.ShapeDtypeStruct((nseq, WINDOW, BLK), F32),
        ],
        scratch_shapes=[
            pltpu.VMEM((rows, A_COLS), F32),
            pltpu.VMEM((rows, B_COLS), F32),
            pltpu.VMEM((SWA_HEADS * SLAB, BLK), F32),
            pltpu.VMEM((SWA_HEADS * SLAB, BLK), F32),
        ],
        compiler_params=pltpu.CompilerParams(
            dimension_semantics=("arbitrary",),
            vmem_limit_bytes=VMEM_LIMIT),
        name="sample_mixer",
    )(x3d, wt, wn, gb, mlg, sinks, rb, mrep, c0, n0, kc, vc)


def _dense_kernel(x_ref, mix_ref, xs_ref, mixs_ref, wo_ref, g1_ref, b1_ref, wfi_ref, wfo_ref, g2_ref, b2_ref,
                  y_ref, ys_ref):
    bounds = [b for b in DENSE_SPLITS if b <= x_ref.shape[0]]
    nsub = len(bounds) - 1
    subs = [slice(bounds[i], bounds[i + 1]) for i in range(nsub)]
    chunks = list(zip(FF_SPLITS[:-1], FF_SPLITS[1:]))
    g1, b1, g2, b2 = g1_ref[...], b1_ref[...], g2_ref[...], b2_ref[...]

    xin = [x_ref[s, :] for s in subs]
    mixin = [mix_ref[s, :] for s in subs]
    xs = xs_ref[...].reshape(mixs_ref.shape)
    xin[-1] = jnp.concatenate([xin[-1], xs], axis=0)
    mixin[-1] = jnp.concatenate([mixin[-1], mixs_ref[...]], axis=0)

    proj = [_dot(mixin[i], wo_ref[...]) for i in range(nsub)]
    h1 = [_layer_norm(ALPHA * xin[i] + proj[i], g1, b1) for i in range(nsub)]
    h1b = [h.astype(BF16) for h in h1]
    acc = [None] * nsub
    for c0, c1 in chunks:
        gate, up = [], []
        for i in range(nsub):
            gate.append(_dot(h1b[i], wfi_ref[:, c0:c1]))
            up.append(_dot(h1b[i], wfi_ref[:, D_FF + c0:D_FF + c1]))
        for i in range(nsub):
            act = (gate[i] * jax.nn.sigmoid(gate[i]) * up[i]).astype(BF16)
            part = _dot(act, wfo_ref[c0:c1, :])
            acc[i] = part if acc[i] is None else acc[i] + part
    for i, s in enumerate(subs):
        out = _layer_norm(ALPHA * h1[i] + acc[i], g2, b2)
        nrows = s.stop - s.start
        y_ref[s, :] = out[:nrows]
        if i == nsub - 1:
            ys_ref[...] = out[nrows:].reshape(ys_ref.shape)


def _dense(x2d, mix2d, xs3d, mixs2d, wo, g1, b1, wfi, wfo, g2, b2):
    tdec = xs3d.shape[1]
    rows = x2d.shape[0]
    tm = TM_DENSE
    steps = rows // tm
    srows = mixs2d.shape[0] // steps
    seq3 = pl.BlockSpec((srows // tdec, tdec, D_MODEL), lambda i: (i, 0, 0))
    const2 = lambda i: (0, 0)
    step2 = lambda i: (i, 0)

    def wspec(shape):
        return pl.BlockSpec(shape, const2, pipeline_mode=pl.Buffered(1))

    return pl.pallas_call(
        _dense_kernel,
        grid=(steps,),
        in_specs=[
            pl.BlockSpec((tm, D_MODEL), step2),
            pl.BlockSpec((tm, D_MODEL), step2),
            seq3,
            pl.BlockSpec((srows, D_MODEL), step2),
            wspec((D_MODEL, D_MODEL)),
            wspec((1, D_MODEL)), wspec((1, D_MODEL)),
            wspec((D_MODEL, 2 * D_FF)),
            wspec((D_FF, D_MODEL)),
            wspec((1, D_MODEL)), wspec((1, D_MODEL)),
        ],
        out_specs=[pl.BlockSpec((tm, D_MODEL), step2),
                   seq3],
        out_shape=[jax.ShapeDtypeStruct((rows, D_MODEL), F32),
                   jax.ShapeDtypeStruct(xs3d.shape, F32)],
        compiler_params=pltpu.CompilerParams(
            dimension_semantics=("arbitrary",),
            vmem_limit_bytes=VMEM_LIMIT),
        name="dense",
    )(x2d, mix2d, xs3d, mixs2d, wo, g1, b1, wfi, wfo, g2, b2)


def kernel(x_prompt, x_sample, state_mlstm_C, state_mlstm_n, state_mlstm_m, cache_swa_k, cache_swa_v,
           w_in, b_igate, b_fgate, ml_norm_g, swa_sinks, rel_bias, w_out, ln1_g, ln1_b,
           w_ffn_in, w_ffn_out, ln2_g, ln2_b):
    bp, seq, _ = x_prompt.shape
    bs, tdec, _ = x_sample.shape
    l = 0

    w16 = jnp.pad(w_in[l], ((0, 0), (0, W_IN_PAD - w_in.shape[-1]))).astype(BF16)
    gvec = jnp.concatenate([b_igate[l], b_fgate[l]]).astype(F32)
    gb = jnp.pad(gvec, (0, BLK - 2 * ML_HEADS))[None, :]
    gbc = jnp.broadcast_to(jnp.pad(gvec, (0, T_GROWS - 2 * ML_HEADS))[:, None], (T_GROWS, BLK))
    mlg = ml_norm_g[l][None, :].astype(F32)
    sinks = swa_sinks[l].astype(F32)
    rb = rel_bias.astype(F32)
    g1, b1 = ln1_g[l][None, :], ln1_b[l][None, :]
    g2, b2 = ln2_g[l][None, :], ln2_b[l][None, :]

    mix_p, c_p, n_p, m_p, k_p, v_p, wt, wn, wo, wfi, wfo = _prompt_mixer(
        x_prompt, w16, gbc, mlg, sinks, rb, w_out[l], w_ffn_in[l], w_ffn_out[l])
    p_m =m_p[:, :ML_HEADS // 2, ::BLK].reshape(bp, ML_HEADS)[None]
    p_k = k_p.reshape(1, bp, WINDOW, SWA_KV_HEADS, SWA_HD)
    p_v = v_p.reshape(1, bp, WINDOW, SWA_KV_HEADS, SWA_HD)

    m0 =state_mlstm_m[l].astype(F32)
    mrep = jnp.pad(jnp.repeat(m0, tdec, axis=0), ((0, 0), (ML_HEADS, BLK - 2 * ML_HEADS)))
    wlen = cache_swa_k.shape[2]
    kc = cache_swa_k[l].reshape(bs, wlen, SWA_KV_HEADS * SWA_HD)
    vc = cache_swa_v[l].reshape(bs, wlen, SWA_KV_HEADS * SWA_HD)
    mix_s, c_s, n_s, mo_s, k_s, v_s = _sample_mixer(
        x_sample, wt, wn, gb, mlg, sinks, rb, mrep,
        state_mlstm_C[l].astype(F32), state_mlstm_n[l].astype(F32), kc, vc)
    s_m =mo_s.reshape(bs, tdec, BLK)[:, 0, ML_HEADS:2 * ML_HEADS][None]
    s_k = k_s.reshape(1, bs, wlen, SWA_KV_HEADS, SWA_HD)
    s_v = v_s.reshape(1, bs, wlen, SWA_KV_HEADS, SWA_HD)

    y_p, y_s = _dense(x_prompt.reshape(bp * seq, D_MODEL), mix_p.reshape(bp * seq, D_MODEL), x_sample, mix_s,
                      wo, g1, b1, wfi, wfo, g2, b2)
    y_p = y_p.reshape(bp, seq, D_MODEL)

    return (y_p, y_s, c_p[None], n_p[None], p_m, p_k, p_v,
            c_s[None], n_s[None], s_m, s_k, s_v)
```

```python
import functools
import math

import jax
import jax.numpy as jnp
from jax import lax
from jax.experimental import pallas as pl
from jax.experimental.pallas import tpu as pltpu

F32 = jnp.float32
BF16 = jnp.bfloat16

D_MODEL = 1024
ML_HEADS = 4
ML_DK = 128
ML_DV = 128
ML_WIDTH = ML_HEADS * ML_DV
GATE_SOFTCAP = 15.0
SWA_HEADS = 8
SWA_KV_HEADS = 2
SWA_GROUP = SWA_HEADS // SWA_KV_HEADS
SWA_HD = 64
SWA_WIDTH = SWA_HEADS * SWA_HD
WINDOW = 128
REL_BUCKETS = 32
REL_MAX_DIST = 128
D_FF = 2816
DEPTH = 1
ALPHA = (2.0 * DEPTH) ** 0.25
LN_EPS = 1e-5
NORM_EPS = 1e-6
NEG_INF = -1e30
IN_SIZES = (512, 512, 512, 512, 4, 4, 512, 128, 128)
ML_SCALE = ML_DK ** -0.5
LOG_ML_SCALE = math.log(ML_SCALE)
SWA_SCALE = SWA_HD ** -0.5

BLK = 128
SUBLANES_F32 = 8
SUBLANES_BF16 = 16

A_Q, A_K, A_V, A_SQ = 0, 512, 1024, 1536
A_COLS = 2048
B_MO, B_SK, B_SV, B_G = 0, 512, 640, 768
B_COLS = 896

N_K = 0
N_B = 512
T_Q, T_V, T_SQ, T_SV, T_G = 0, 512, 1024, 1536, 1664
T_GROWS = SUBLANES_BF16
T_COLS = T_G + BLK
W_IN_PAD = -(-sum(IN_SIZES) // BLK) * BLK
STATE_ROWS = ML_DV + SUBLANES_BF16
M_ROWS = SUBLANES_F32

TM_PROMPT = 1024
TM_DENSE = 1024
DENSE_SPLITS = (0, 384, 768, 1024)
FF_SPLITS = (0, 1536, D_FF)
SAMPLE_GB = 16
SLAB = 16
VMEM_LIMIT = 56 * 1024 * 1024


def _softcap(a):
    return GATE_SOFTCAP * jnp.tanh(a / GATE_SOFTCAP)


def _log_sigmoid(x):
    return jnp.minimum(x, 0.0) - jnp.log1p(jnp.exp(-jnp.abs(x)))


def _layer_norm(z, g, b):
    mu = jnp.mean(z, axis=-1, keepdims=True)
    zc = z - mu
    var = jnp.mean(zc * zc, axis=-1, keepdims=True)
    return zc * lax.rsqrt(var + LN_EPS) * g + b


def _dot(a, b):
    return jnp.dot(a, b, preferred_element_type=F32)


def _dot_nt(a, b):
    return lax.dot_general(a, b, (((1,), (1,)), ((), ())), preferred_element_type=F32)


def _dot_tn(a, b):
    return lax.dot_general(a, b, (((0,), (0,)), ((), ())), preferred_element_type=F32)


def _split3(x):
    hi = x.astype(BF16)
    r1 = x - hi.astype(F32)
    mid = r1.astype(BF16)
    lo = (r1 - mid.astype(F32)).astype(BF16)
    return hi, mid, lo


def _t5_bucket(d):
    n = jnp.maximum(d, 0)
    max_exact = REL_BUCKETS // 2
    nlog = REL_BUCKETS - max_exact
    large = jnp.full(n.shape, max_exact, jnp.int32)
    for k in range(1, nlog):
        thr = math.ceil(max_exact * (REL_MAX_DIST / max_exact) ** (k / nlog))
        large = large + jnp.where(n >= thr, 1, 0)
    return jnp.where(n < max_exact, n, large)


def _prompt_mixer_kernel(x_ref, w_ref, gbc_ref, mlg_ref, sinks_ref, rb_ref, wo32, wfi32, wfo32,
                         mix_ref, c_out, n_out, m_out, ko_ref, vo_ref,
                         wt, wn, wo16, wfi16, wfo16,
                         kn, pf, vlast, kband, q_t, v_t, sq_t, sv_t, g_t, ct, msc, tbl):
    wo16[...] = wo32[...].astype(BF16)
    wfi16[...] = wfi32[...].astype(BF16)
    wfo16[...] = wfo32[...].astype(BF16)

    b = pl.program_id(0)
    t = pl.program_id(1)
    nt = pl.num_programs(1)
    tm = x_ref.shape[1]
    nblk = tm // BLK
    pairs = SWA_HEADS // 2

    @pl.when((b == 0) & (t == 0))
    def _prepare_weights():
        offs = [0]
        for n in IN_SIZES:
            offs.append(offs[-1] + n)
        o_mq, o_mk, o_mv, o_mo, o_mi, _, o_sq, o_sk, o_sv = offs[:-1]

        def col_blocks(dst_col, src_col, n):
            for i in range(n // BLK):
                wn[:, dst_col + i * BLK:dst_col + (i + 1) * BLK] = (
                    w_ref[:, src_col + i * BLK:src_col + (i + 1) * BLK])

        def row_blocks(dst_row, src_col, n):
            for i in range(n // BLK):
                wt[dst_row + i * BLK:dst_row + (i + 1) * BLK, :] = (
                    w_ref[:, src_col + i * BLK:src_col + (i + 1) * BLK].T)

        col_blocks(N_K, o_mk, ML_HEADS * ML_DK)
        col_blocks(N_B + B_MO, o_mo, ML_WIDTH)
        col_blocks(N_B + B_SK, o_sk, SWA_KV_HEADS * SWA_HD)
        row_blocks(T_Q, o_mq, ML_HEADS * ML_DK)
        row_blocks(T_V, o_mv, ML_WIDTH)
        row_blocks(T_SQ, o_sq, SWA_WIDTH)
        row_blocks(T_SV, o_sv, SWA_KV_HEADS * SWA_HD)
        gates_t = w_ref[:, o_mi:o_mi + BLK].T
        grow = lax.broadcasted_iota(jnp.int32, gates_t.shape, 0)
        wt[T_G:T_G + BLK, :] = jnp.where(grow < 2 * ML_HEADS, gates_t, jnp.zeros_like(gates_t))

    @pl.when((b == 0) & (t == 0))
    def _build_bias_tables():
        r = lax.broadcasted_iota(jnp.int32, (BLK, 2 * BLK), 0)
        ln = lax.broadcasted_iota(jnp.int32, (BLK, 2 * BLK), 1)
        second = ln >= BLK
        qi = jnp.where(second, ln - BLK, ln)
        prev = r > qi
        d = jnp.where(prev, WINDOW + qi - r, qi - r)
        bucket = _t5_bucket(d)
        for p in range(pairs):
            acc = jnp.zeros((BLK, 2 * BLK), F32)
            for k in range(REL_BUCKETS):
                acc = jnp.where(bucket == k, jnp.where(second, rb_ref[k, p + pairs], rb_ref[k, p]), acc)
            tbl[0, p] = jnp.where(prev, NEG_INF, acc)
            tbl[1, p] = acc

    @pl.when(t == 0)
    def _reset_state():
        ct[...] = jnp.zeros_like(ct)
        msc[...] = jnp.zeros_like(msc)
        kband[0:BLK, :] = jnp.zeros((BLK, kband.shape[1]), kband.dtype)
        sv_t[0] = jnp.zeros(sv_t.shape[1:], sv_t.dtype)

    xb = x_ref[0].astype(BF16)
    res_n = _dot(xb, wn[...])
    kn[...] = res_n[:, N_K:N_K + ML_HEADS * ML_DK].astype(BF16)
    pf[...] = res_n[:, N_B:N_B + B_SV]
    kband[BLK:BLK + tm, :] = pf[:, B_SK:B_SK + BLK].astype(BF16)

    res_t = _dot_nt(wt[0:T_G + T_GROWS, :], xb)
    vlast[...] = res_t[T_SV:T_SV + BLK, tm - BLK:tm]

    def put(dst, r0, nrows, off, scale=None):
        res = res_t[r0:r0 + nrows, :]
        if scale is not None:
            res = res * scale
        for jj in range(nblk):
            dst[jj + off] = res[:, jj * BLK:(jj + 1) * BLK].astype(dst.dtype)

    put(q_t, T_Q, ML_HEADS * ML_DK, 0)
    put(v_t, T_V, ML_WIDTH, 0)
    put(sq_t, T_SQ, SWA_WIDTH, 0, SWA_SCALE)
    put(sv_t, T_SV, SWA_KV_HEADS * SWA_HD, 1)
    put(g_t, T_G, T_GROWS, 0)

    row = lax.broadcasted_iota(jnp.int32, (BLK, BLK), 0)
    col = lax.broadcasted_iota(jnp.int32, (BLK, BLK), 1)
    causal_t = row <= col
    triu = causal_t.astype(BF16)
    causal2 = jnp.concatenate([causal_t, causal_t], axis=1)
    r16 = lax.broadcasted_iota(jnp.int32, (STATE_ROWS - ML_DV, 2 * BLK), 0)
    ones_rows = (r16 == 0).astype(BF16)
    z128 = jnp.zeros((BLK, BLK), BF16)
    ml_pairs = ML_HEADS // 2
    zero_rows = jnp.zeros((BLK - T_GROWS, BLK), F32)
    gbc = gbc_ref[...]
    mlg = mlg_ref[...]
    lane2 = lax.broadcasted_iota(jnp.int32, (1, 2 * BLK), 1)
    snk_rows = [jnp.where(lane2 >= BLK, sinks_ref[p + pairs], sinks_ref[p]) for p in range(pairs)]
    zq = jnp.zeros((SWA_HD, BLK), BF16)
    prev2 = jnp.concatenate([row > col, row > col], axis=1)

    def pair_row(x0, x1):
        return jnp.concatenate([jnp.broadcast_to(x0, (1, BLK)), jnp.broadcast_to(x1, (1, BLK))], axis=1)

    def block_diag(x0, x1):
        return jnp.concatenate([jnp.concatenate([x0, z128], axis=1),
                                jnp.concatenate([z128, x1], axis=1)], axis=0)

    def block(j):
        r0 = j * BLK
        rows = pl.ds(r0, BLK)
        hsl = [slice(h * ML_DK, (h + 1) * ML_DK) for h in range(ML_HEADS)]
        psl = [slice(p * 2 * ML_DK, (p + 1) * 2 * ML_DK) for p in range(ml_pairs)]

        pre_t = _softcap(g_t[j] + gbc)
        hi, mid, lo = _split3(_log_sigmoid(pre_t))
        b_t = _dot(hi, triu) + _dot(mid, triu) + _dot(lo, triu)
        u_t = pltpu.roll(pre_t, ML_HEADS, 0) - b_t
        u_c = jnp.concatenate([u_t, zero_rows], axis=0).T

        k2 = [kn[rows, psl[p]] for p in range(ml_pairs)]
        bdq = [block_diag(q_t[j, hsl[2 * p], :], q_t[j, hsl[2 * p + 1], :]) for p in range(ml_pairs)]
        vte = [jnp.concatenate([jnp.concatenate([v_t[j, hsl[2 * p], :], v_t[j, hsl[2 * p + 1], :]], axis=1),
                                ones_rows], axis=0) for p in range(ml_pairs)]
        kq = [_dot(k2[p], bdq[p]) for p in range(ml_pairs)]

        sel = jnp.where(t == 0, 0, 1) if j == 0 else 1
        kb = kband[pl.ds(r0, 2 * BLK), :]
        vtb = jnp.concatenate([sv_t[j], sv_t[j + 1]], axis=1)
        sc2 = []
        for p in range(pairs):
            qa = sq_t[j, p * SWA_HD:(p + 1) * SWA_HD, :]
            qb = sq_t[j, (p + pairs) * SWA_HD:(p + pairs + 1) * SWA_HD, :]
            bd = jnp.concatenate([jnp.concatenate([qa, zq], axis=1),
                                  jnp.concatenate([zq, qb], axis=1)], axis=0)
            sc2.append(_dot(kb, bd))

        yield
        bds, scl, m_row, m_new, a_p, vwte = [], [], [], [], [], []
        for p in range(ml_pairs):
            h0, h1 = 2 * p, 2 * p + 1
            b_row = pair_row(b_t[ML_HEADS + h0:ML_HEADS + h0 + 1, :], b_t[ML_HEADS + h1:ML_HEADS + h1 + 1, :])
            i_row = pair_row(pre_t[h0:h0 + 1, :], pre_t[h1:h1 + 1, :])
            u_col = jnp.concatenate(
                [jnp.broadcast_to(u_c[:, ML_HEADS + h0:ML_HEADS + h0 + 1], (BLK, BLK)),
                 jnp.broadcast_to(u_c[:, ML_HEADS + h1:ML_HEADS + h1 + 1], (BLK, BLK))], axis=1)
            m_prev = msc[p:p + 1, :]
            log_d = jnp.where(causal2, b_row + u_col, -jnp.inf)
            inter = b_row + m_prev
            mr = jnp.maximum(inter, jnp.max(log_d, axis=0, keepdims=True))
            m_row.append(mr)
            scl.append(jnp.exp(inter - mr))
            s2 = (kq[p] * jnp.exp(log_d - (mr - LOG_ML_SCALE))).astype(BF16)
            bds.append(block_diag(s2[:, :BLK], s2[:, BLK:]))
            b_last = pair_row(b_row[:, BLK - 1:BLK], b_row[:, 2 * BLK - 1:2 * BLK])
            w_row = b_last - b_row + i_row
            w_max = pair_row(jnp.max(w_row[:, :BLK], axis=1, keepdims=True),
                             jnp.max(w_row[:, BLK:], axis=1, keepdims=True))
            mn = jnp.maximum(b_last + m_prev, w_max)
            m_new.append(mn)
            a_p.append(jnp.exp(b_last + m_prev - mn))
            wexp = jnp.exp(w_row - mn) * ML_SCALE
            vwte.append(vte[p] * wexp.astype(BF16))

        pw, den_p = [], []
        for p in range(pairs):
            sc = jnp.where(prev2, sc2[p][:BLK], sc2[p][BLK:]) + tbl[sel, p]
            snk = snk_rows[p]
            mx = jnp.maximum(jnp.max(sc, axis=0, keepdims=True), snk)
            pe = jnp.exp(sc - mx)
            den_p.append(jnp.sum(pe, axis=0, keepdims=True) + jnp.exp(snk - mx))
            pw.append(jnp.concatenate([jnp.where(prev2, pe, 0.0), jnp.where(prev2, 0.0, pe)],
                                      axis=0).astype(BF16))

        c_old = [ct[p] for p in range(ml_pairs)]
        cq = [_dot(c_old[p].astype(BF16), bdq[p]) for p in range(ml_pairs)]
        vs = [_dot(vte[p], bds[p]) for p in range(ml_pairs)]
        upd = [_dot(vwte[p], block_diag(k2[p][:, :BLK], k2[p][:, BLK:])) for p in range(ml_pairs)]
        o2 = [_dot(vtb, pw[p]) for p in range(pairs)]

        for p in range(ml_pairs):
            nd = scl[p] * cq[p] + vs[p]
            num = nd[:ML_DV]
            den = nd[ML_DV:ML_DV + 1]
            hh = num / jnp.maximum(jnp.abs(den), jnp.exp(-m_row[p]))
            hn = hh * lax.rsqrt(jnp.mean(hh * hh, axis=0, keepdims=True) + NORM_EPS)
            for half in range(2):
                h = 2 * p + half
                og = jax.nn.sigmoid(pf[rows, B_MO + h * ML_DV:B_MO + (h + 1) * ML_DV])
                hn_h = hn[:, half * BLK:(half + 1) * BLK].T
                mix_ref[0, rows, hsl[h]] = (hn_h * mlg[:, hsl[h]] * og).astype(mix_ref.dtype)
            ct[p] = a_p[p] * c_old[p] + upd[p]
            msc[p:p + 1, :] = m_new[p]

        o_parts = [None] * SWA_HEADS
        for p in range(pairs):
            o_parts[p] = o2[p][0:SWA_HD, 0:BLK] / den_p[p][:, 0:BLK]
            o_parts[p + pairs] = o2[p][SWA_HD:2 * SWA_HD, BLK:2 * BLK] / den_p[p][:, BLK:2 * BLK]
        o_t = jnp.concatenate(o_parts, axis=0)
        mix_ref[0, rows, ML_WIDTH:ML_WIDTH + SWA_WIDTH] = o_t.T.astype(mix_ref.dtype)
        yield

    blocks = [block(j) for j in range(nblk)]
    for g in blocks:
        next(g)
    for g in blocks:
        next(g)

    kband[0:BLK, :] = kband[tm:tm + BLK, :]
    sv_t[0] = sv_t[nblk]

    @pl.when(t == nt - 1)
    def _write_state():
        for h in range(ML_HEADS):
            p, lanes = h // 2, slice((h % 2) * ML_DK, (h % 2 + 1) * ML_DK)
            c_out[0, h] = ct[p, 0:ML_DV, lanes].T
            n_out[0, h:h + 1, :] = ct[p, ML_DV:ML_DV + 1, lanes]
        m_out[0] = msc[...]
        ko_ref[0] = pf[tm - BLK:tm, B_SK:B_SK + BLK]
        vo_ref[0] = vlast[...].T


def _prompt_mixer(x, w_in_l, gbc, mlg, sinks, rb, wo32, wfi32, wfo32):
    bsz, seq, _ = x.shape
    tm = TM_PROMPT
    nt = seq // tm
    nblk = tm // BLK
    steps = bsz * nt
    const2 = lambda b, t: (0, 0)
    smem = pl.BlockSpec(memory_space=pltpu.SMEM)

    def slab_spec(wmat):
        return pl.BlockSpec((wmat.shape[0] // steps, wmat.shape[1]), lambda b, t: (b * nt + t, 0))

    return pl.pallas_call(
        _prompt_mixer_kernel,
        grid=(bsz, nt),
        in_specs=[
            pl.BlockSpec((1, tm, D_MODEL), lambda b, t: (b, t, 0)),
            pl.BlockSpec((D_MODEL, W_IN_PAD), const2, pipeline_mode=pl.Buffered(1)),
            pl.BlockSpec((T_GROWS, BLK), const2),
            pl.BlockSpec((1, ML_WIDTH), const2),
            smem, smem,
            slab_spec(wo32), slab_spec(wfi32), slab_spec(wfo32),
        ],
        out_specs=[
            pl.BlockSpec((1, tm, D_MODEL), lambda b, t: (b, t, 0)),
            pl.BlockSpec((1, ML_HEADS, ML_DK, ML_DV), lambda b, t: (b, 0, 0, 0)),
            pl.BlockSpec((1, ML_HEADS, ML_DK), lambda b, t: (b, 0, 0)),
            pl.BlockSpec((1, M_ROWS, 2 * BLK), lambda b, t: (b, 0, 0)),
            pl.BlockSpec((1, BLK, BLK), lambda b, t: (b, 0, 0)),
            pl.BlockSpec((1, BLK, BLK), lambda b, t: (b, 0, 0)),
            pl.BlockSpec((T_COLS, D_MODEL), const2),
            pl.BlockSpec((D_MODEL, N_B + B_SV), const2),
            slab_spec(wo32), slab_spec(wfi32), slab_spec(wfo32),
        ],
        out_shape=[
            jax.ShapeDtypeStruct((bsz, seq, D_MODEL), BF16),
            jax.ShapeDtypeStruct((bsz, ML_HEADS, ML_DK, ML_DV), F32),
            jax.ShapeDtypeStruct((bsz, ML_HEADS, ML_DK), F32),
            jax.ShapeDtypeStruct((bsz, M_ROWS, 2 * BLK), F32),
            jax.ShapeDtypeStruct((bsz, BLK, BLK), F32),
            jax.ShapeDtypeStruct((bsz, BLK, BLK), F32),
            jax.ShapeDtypeStruct((T_COLS, D_MODEL), BF16),
            jax.ShapeDtypeStruct((D_MODEL, N_B + B_SV), BF16),
            jax.ShapeDtypeStruct(wo32.shape, BF16),
            jax.ShapeDtypeStruct(wfi32.shape, BF16),
            jax.ShapeDtypeStruct(wfo32.shape, BF16),
        ],
        scratch_shapes=[
            pltpu.VMEM((tm, ML_HEADS * ML_DK), BF16),
            pltpu.VMEM((tm, B_SV), F32),
            pltpu.VMEM((BLK, BLK), F32),
            pltpu.VMEM((tm + BLK, BLK), BF16),
            pltpu.VMEM((nblk, ML_HEADS * ML_DK, BLK), BF16),
            pltpu.VMEM((nblk, ML_WIDTH, BLK), BF16),
            pltpu.VMEM((nblk, SWA_WIDTH, BLK), BF16),
            pltpu.VMEM((nblk + 1, SWA_KV_HEADS * SWA_HD, BLK), BF16),
            pltpu.VMEM((nblk, T_GROWS, BLK), F32),
            pltpu.VMEM((ML_HEADS // 2, STATE_ROWS, 2 * ML_DK), F32),
            pltpu.VMEM((M_ROWS, 2 * BLK), F32),
            pltpu.VMEM((2, SWA_HEADS // 2, BLK, 2 * BLK), F32),
        ],
        compiler_params=pltpu.CompilerParams(
            dimension_semantics=("arbitrary", "arbitrary"),
            vmem_limit_bytes=VMEM_LIMIT),
        name="prompt_mixer",
    )(x, w_in_l, gbc, mlg, sinks, rb, wo32, wfi32, wfo32)


def _roll_rows(x, shift):
    return pltpu.roll(x, shift % x.shape[0], 0)


def _sample_mixer_kernel(x_ref, wt_ref, wn_ref, gb_ref, mlg_ref, sinks_ref, rb_ref, mrep_ref,
                         c_ref, n_ref, kc_ref, vc_ref,
                         mix_ref, c_out, n_out, mo_ref, ko_ref, vo_ref,
                         pa, pf, tblc, tbln):
    step = pl.program_id(0)
    tdec = 4
    rows_step = mix_ref.shape[0]
    nslab = rows_step // SLAB
    nseq = SLAB // tdec
    qrows = SWA_HEADS * SLAB

    @pl.when(step == 0)
    def _project_all_rows():
        xb = x_ref[...].reshape(pa.shape[0], D_MODEL).astype(BF16)
        pa[:, A_Q:A_K] = _dot_nt(xb, wt_ref[T_Q:T_Q + ML_HEADS * ML_DK, :])
        pa[:, A_K:A_V] = _dot(xb, wn_ref[:, N_K:N_K + ML_HEADS * ML_DK])
        pa[:, A_V:A_COLS] = _dot_nt(xb, wt_ref[T_V:T_V + ML_WIDTH + SWA_WIDTH, :])
        pf[:, B_MO:B_SV] = _dot(xb, wn_ref[:, N_B:N_B + B_SV])
        pf[:, B_SV:B_COLS] = _dot_nt(xb, wt_ref[T_SV:T_COLS, :])

    @pl.when(step == 0)
    def _build_bias_tables():
        r = lax.broadcasted_iota(jnp.int32, (qrows, BLK), 0)
        c = lax.broadcasted_iota(jnp.int32, (qrows, BLK), 1)
        hq = r // SLAB
        bl = (r % SLAB) // tdec
        tq = r % tdec
        d_cache = WINDOW + tq - c
        d_new = tq - (c % tdec)
        key_cache = _t5_bucket(d_cache) * SWA_HEADS + hq
        key_new = _t5_bucket(d_new) * SWA_HEADS + hq
        acc_c = jnp.zeros((qrows, BLK), F32)
        acc_n = jnp.zeros((qrows, BLK), F32)
        for k in range(REL_BUCKETS):
            for h in range(SWA_HEADS):
                val = rb_ref[k, h]
                acc_c = jnp.where(key_cache == k * SWA_HEADS + h, val, acc_c)
                acc_n = jnp.where(key_new == k * SWA_HEADS + h, val, acc_n)
        tblc[...] = jnp.where(c > tq, acc_c, NEG_INF)
        tbln[...] = jnp.where((c < SLAB) & ((c // tdec) == bl) & (d_new >= 0), acc_n, NEG_INF)

    gbias = gb_ref[...]
    mlg = mlg_ref[...]
    r16 = lax.broadcasted_iota(jnp.int32, (SLAB, BLK), 0)
    rr = r16 % tdec
    bl16 = r16 // tdec
    rq = lax.broadcasted_iota(jnp.int32, (qrows, BLK), 0)
    blq = (rq % SLAB) // tdec
    hq_col = rq[:, 0:1] // SLAB
    snk = jnp.zeros((qrows, 1), F32)
    for h in range(SWA_HEADS):
        snk = jnp.where(hq_col == h, sinks_ref[h], snk)
    first_group = rq[:, 0:SWA_HD] < SWA_GROUP * SLAB
    zq = jnp.zeros((SLAB, SWA_HD), F32)
    zeros_pad = jnp.zeros((BLK - SLAB, BLK), BF16)

    def seg_last(x):
        return jnp.where(rr == 3, x,
                         jnp.where(rr == 2, _roll_rows(x, -1),
                                   jnp.where(rr == 1, _roll_rows(x, -2), _roll_rows(x, -3))))

    def seg_max(x):
        m1 = jnp.maximum(x, jnp.where(rr % 2 == 0, _roll_rows(x, -1), _roll_rows(x, 1)))
        return jnp.maximum(m1, jnp.where(rr < 2, _roll_rows(m1, -2), _roll_rows(m1, 2)))

    def slab(si):
        rows = pl.ds(si * SLAB, SLAB)
        prow = pl.ds(pl.multiple_of(step * rows_step + si * SLAB, SLAB), SLAB)
        seq0 = si * nseq

        pre = _softcap(pf[prow, B_G:B_G + BLK] + gbias)
        logf = _log_sigmoid(pre)
        y = logf + jnp.where(rr >= 1, _roll_rows(logf, 1), 0.0)
        bcum = y + jnp.where(rr >= 2, _roll_rows(y, 2), 0.0)
        ig = pltpu.roll(pre, ML_HEADS, 1)
        m_prev = mrep_ref[rows, :]
        b_minus_i = bcum - ig
        log_d = [jnp.where(rr >= dl, bcum - _roll_rows(b_minus_i, dl), -jnp.inf) for dl in range(tdec)]
        rowmax = functools.reduce(jnp.maximum, log_d)
        inter = bcum + m_prev
        m_row = jnp.maximum(inter, rowmax)
        scl = jnp.exp(inter - m_row)
        dm = [jnp.exp(ld - m_row) for ld in log_d]
        enm = jnp.exp(-m_row)
        b_last = seg_last(bcum)
        w = b_last - bcum + ig
        m_new = jnp.maximum(b_last + m_prev, seg_max(w))
        a = jnp.exp(b_last + m_prev - m_new)
        kws = jnp.exp(w - m_new) * ML_SCALE
        mo_ref[rows, :] = m_new

        q = [pa[prow, A_Q + h * ML_DK:A_Q + (h + 1) * ML_DK] for h in range(ML_HEADS)]
        k = [pa[prow, A_K + h * ML_DK:A_K + (h + 1) * ML_DK] for h in range(ML_HEADS)]
        v = [pa[prow, A_V + h * ML_DV:A_V + (h + 1) * ML_DV] for h in range(ML_HEADS)]
        c_b = [[c_ref[seq0 + bl, h] for bl in range(nseq)] for h in range(ML_HEADS)]
        qc, upd, kw = [], [], []
        for h in range(ML_HEADS):
            qb = q[h].astype(BF16)
            res = [_dot(qb, jnp.concatenate([c_b[h][2 * i].astype(BF16), c_b[h][2 * i + 1].astype(BF16)], axis=1))
                   for i in range(nseq // 2)]
            parts = [res[bl // 2][:, (bl % 2) * ML_DV:(bl % 2 + 1) * ML_DV] for bl in range(nseq)]
            qc_h = parts[nseq - 1]
            for bl in range(nseq - 2, -1, -1):
                qc_h = jnp.where(bl16 == bl, parts[bl], qc_h)
            qc.append(qc_h)

        knew = pf[prow, B_SK:B_SK + BLK]
        vnew = pf[prow, B_SV:B_SV + BLK]
        qparts = []
        for hq in range(SWA_HEADS):
            qh = pa[prow, A_SQ + hq * SWA_HD:A_SQ + (hq + 1) * SWA_HD]
            qparts.append(jnp.concatenate([qh, zq] if hq < SWA_GROUP else [zq, qh], axis=1))
        qs = jnp.concatenate(qparts, axis=0).astype(BF16)
        s_c = _dot_nt(qs, kc_ref[seq0 + nseq - 1].astype(BF16))
        for bl in range(nseq - 2, -1, -1):
            s_c = jnp.where(blq == bl, _dot_nt(qs, kc_ref[seq0 + bl].astype(BF16)), s_c)
        kn = jnp.concatenate([knew.astype(BF16), zeros_pad], axis=0)
        s_n = _dot_nt(qs, kn)

        yield
        for h in range(ML_HEADS):
            lane = ML_HEADS + h
            kw_h = k[h] * kws[:, lane:lane + 1]
            kw.append(kw_h)
            vm = jnp.concatenate([jnp.where(bl16 == bl, v[h], 0.0) for bl in range(nseq)], axis=1)
            upd.append(_dot_tn(kw_h.astype(BF16), vm.astype(BF16)))

        yield
        for h in range(ML_HEADS):
            hs = slice(h * ML_DK, (h + 1) * ML_DK)
            lane = ML_HEADS + h
            bm = jnp.zeros((SLAB, ML_DV), F32)
            sum_s = jnp.zeros((SLAB, 1), F32)
            for dl in range(tdec):
                s_dl = jnp.sum(q[h] * _roll_rows(k[h], dl), axis=-1, keepdims=True)
                ws = s_dl * ML_SCALE * dm[dl][:, lane:lane + 1]
                bm = bm + ws * _roll_rows(v[h], dl)
                sum_s = sum_s + ws
            nsel = jnp.zeros((SLAB, ML_DK), F32)
            for bl in range(nseq):
                msk = bl16 == bl
                n_b = n_ref[seq0 + bl][h:h + 1, :]
                nsel = jnp.where(msk, n_b, nsel)
                a_b = a[bl * tdec:bl * tdec + 1, lane:lane + 1]
                c_out[seq0 + bl, h] = a_b * c_b[h][bl] + upd[h][:, bl * ML_DV:(bl + 1) * ML_DV]
                n_out[seq0 + bl, h:h + 1, :] = a_b * n_b + jnp.sum(jnp.where(msk, kw[h], 0.0), axis=0,
                                                                   keepdims=True)
            sclh = scl[:, lane:lane + 1]
            num = sclh * qc[h] + bm
            den = sclh * jnp.sum(q[h] * nsel, axis=-1, keepdims=True) + sum_s
            hh = num / jnp.maximum(jnp.abs(den), enm[:, lane:lane + 1])
            hn = hh * lax.rsqrt(jnp.mean(hh * hh, axis=-1, keepdims=True) + NORM_EPS)
            og = jax.nn.sigmoid(pf[prow, B_MO + h * ML_DV:B_MO + (h + 1) * ML_DV])
            mix_ref[rows, hs] = (hn * mlg[:, hs] * og).astype(mix_ref.dtype)

        s_c = s_c * SWA_SCALE + tblc[...]
        s_n = s_n * SWA_SCALE + tbln[...]
        mx = jnp.maximum(jnp.maximum(jnp.max(s_c, axis=-1, keepdims=True),
                                     jnp.max(s_n, axis=-1, keepdims=True)), snk)
        p_c = jnp.exp(s_c - mx)
        p_n = jnp.exp(s_n - mx)
        den = (jnp.sum(p_c, axis=-1, keepdims=True) + jnp.sum(p_n, axis=-1, keepdims=True)
               + jnp.exp(snk - mx))

        yield
        vn = jnp.concatenate([vnew.astype(BF16), zeros_pad], axis=0)
        o = _dot(p_n.astype(BF16), vn)
        for bl in range(nseq):
            o = o + _dot(jnp.where(blq == bl, p_c, 0.0).astype(BF16), vc_ref[seq0 + bl].astype(BF16))
        o = o / den
        osel = jnp.where(first_group, o[:, 0:SWA_HD], o[:, SWA_HD:2 * SWA_HD])
        og = jnp.concatenate([osel[hq * SLAB:(hq + 1) * SLAB, :] for hq in range(SWA_HEADS)], axis=1)
        mix_ref[rows, ML_WIDTH:ML_WIDTH + SWA_WIDTH] = og.astype(mix_ref.dtype)

        for bl in range(nseq):
            ko_ref[seq0 + bl, 0:WINDOW - tdec, :] = kc_ref[seq0 + bl, tdec:WINDOW, :]
            vo_ref[seq0 + bl, 0:WINDOW - tdec, :] = vc_ref[seq0 + bl, tdec:WINDOW, :]
            ko_ref[seq0 + bl, WINDOW - tdec:WINDOW, :] = knew[bl * tdec:(bl + 1) * tdec, :]
            vo_ref[seq0 + bl, WINDOW - tdec:WINDOW, :] = vnew[bl * tdec:(bl + 1) * tdec, :]
        yield

    slabs = [slab(si) for si in range(nslab)]
    for _ in range(4):
        for s in slabs:
            next(s)


def _sample_mixer(x3d, wt, wn, gb, mlg, sinks, rb, mrep, c0, n0, kc, vc):
    nseq, tdec, _ = x3d.shape
    rows = nseq * tdec
    gbs = SAMPLE_GB
    steps = nseq // gbs
    rb_rows = gbs * tdec
    const2 = lambda i: (0, 0)
    smem = pl.BlockSpec(memory_space=pltpu.SMEM)
    return pl.pallas_call(
        _sample_mixer_kernel,
        grid=(steps,),
        in_specs=[
            pl.BlockSpec((nseq, tdec, D_MODEL), lambda i: (0, 0, 0), pipeline_mode=pl.Buffered(1)),
            pl.BlockSpec((T_COLS, D_MODEL), const2, pipeline_mode=pl.Buffered(1)),
            pl.BlockSpec((D_MODEL, N_B + B_SV), const2, pipeline_mode=pl.Buffered(1)),
            pl.BlockSpec((1, BLK), const2),
            pl.BlockSpec((1, ML_WIDTH), const2),
            smem, smem,
            pl.BlockSpec((rb_rows, BLK), lambda i: (i, 0)),
            pl.BlockSpec((gbs, ML_HEADS, ML_DK, ML_DV), lambda i: (i, 0, 0, 0)),
            pl.BlockSpec((gbs, ML_HEADS, ML_DK), lambda i: (i, 0, 0)),
            pl.BlockSpec((gbs, WINDOW, BLK), lambda i: (i, 0, 0)),
            pl.BlockSpec((gbs, WINDOW, BLK), lambda i: (i, 0, 0)),
        ],
        out_specs=[
            pl.BlockSpec((rb_rows, D_MODEL), lambda i: (i, 0)),
            pl.BlockSpec((gbs, ML_HEADS, ML_DK, ML_DV), lambda i: (i, 0, 0, 0)),
            pl.BlockSpec((gbs, ML_HEADS, ML_DK), lambda i: (i, 0, 0)),
            pl.BlockSpec((rb_rows, BLK), lambda i: (i, 0)),
            pl.BlockSpec((gbs, WINDOW, BLK), lambda i: (i, 0, 0)),
            pl.BlockSpec((gbs, WINDOW, BLK), lambda i: (i, 0, 0)),
        ],
        out_shape=[
            jax.ShapeDtypeStruct((rows, D_MODEL), BF16),
            jax.ShapeDtypeStruct((nseq, ML_HEADS, ML_DK, ML_DV), F32),
            jax.ShapeDtypeStruct((nseq, ML_HEADS, ML_DK), F32),
            jax.ShapeDtypeStruct((rows, BLK), F32),
            jax.ShapeDtypeStruct((nseq, WINDOW, BLK), F32),
            jax.ShapeDtypeStruct((nseq, WINDOW, BLK), F32),
        ],
        scratch_shapes=[
            pltpu.VMEM((rows, A_COLS), F32),
            pltpu.VMEM((rows, B_COLS), F32),
            pltpu.VMEM((SWA_HEADS * SLAB, BLK), F32),
            pltpu.VMEM((SWA_HEADS * SLAB, BLK), F32),
        ],
        compiler_params=pltpu.CompilerParams(
            dimension_semantics=("arbitrary",),
            vmem_limit_bytes=VMEM_LIMIT),
        name="sample_mixer",
    )(x3d, wt, wn, gb, mlg, sinks, rb, mrep, c0, n0, kc, vc)


def _dense_kernel(x_ref, mix_ref, xs_ref, mixs_ref, wo_ref, g1_ref, b1_ref, wfi_ref, wfo_ref, g2_ref, b2_ref,
                  y_ref, ys_ref):
    bounds = [b for b in DENSE_SPLITS if b <= x_ref.shape[0]]
    nsub = len(bounds) - 1
    subs = [slice(bounds[i], bounds[i + 1]) for i in range(nsub)]
    chunks = list(zip(FF_SPLITS[:-1], FF_SPLITS[1:]))
    g1, b1, g2, b2 = g1_ref[...], b1_ref[...], g2_ref[...], b2_ref[...]

    xin = [x_ref[s, :] for s in subs]
    mixin = [mix_ref[s, :] for s in subs]
    xs = xs_ref[...].reshape(mixs_ref.shape)
    xin[-1] = jnp.concatenate([xin[-1], xs], axis=0)
    mixin[-1] = jnp.concatenate([mixin[-1], mixs_ref[...]], axis=0)

    proj = [_dot(mixin[i], wo_ref[...]) for i in range(nsub)]
    h1 = [_layer_norm(ALPHA * xin[i] + proj[i], g1, b1) for i in range(nsub)]
    h1b = [h.astype(BF16) for h in h1]
    acc = [None] * nsub
    for c0, c1 in chunks:
        gate, up = [], []
        for i in range(nsub):
            gate.append(_dot(h1b[i], wfi_ref[:, c0:c1]))
            up.append(_dot(h1b[i], wfi_ref[:, D_FF + c0:D_FF + c1]))
        for i in range(nsub):
            act = (gate[i] * jax.nn.sigmoid(gate[i]) * up[i]).astype(BF16)
            part = _dot(act, wfo_ref[c0:c1, :])
            acc[i] = part if acc[i] is None else acc[i] + part
    for i, s in enumerate(subs):
        out = _layer_norm(ALPHA * h1[i] + acc[i], g2, b2)
        nrows = s.stop - s.start
        y_ref[s, :] = out[:nrows]
        if i == nsub - 1:
            ys_ref[...] = out[nrows:].reshape(ys_ref.shape)


def _dense(x2d, mix2d, xs3d, mixs2d, wo, g1, b1, wfi, wfo, g2, b2):
    tdec = xs3d.shape[1]
    rows = x2d.shape[0]
    tm = TM_DENSE
    steps = rows // tm
    srows = mixs2d.shape[0] // steps
    seq3 = pl.BlockSpec((srows // tdec, tdec, D_MODEL), lambda i: (i, 0, 0))
    const2 = lambda i: (0, 0)
    step2 = lambda i: (i, 0)

    def wspec(shape):
        return pl.BlockSpec(shape, const2, pipeline_mode=pl.Buffered(1))

    return pl.pallas_call(
        _dense_kernel,
        grid=(steps,),
        in_specs=[
            pl.BlockSpec((tm, D_MODEL), step2),
            pl.BlockSpec((tm, D_MODEL), step2),
            seq3,
            pl.BlockSpec((srows, D_MODEL), step2),
            wspec((D_MODEL, D_MODEL)),
            wspec((1, D_MODEL)), wspec((1, D_MODEL)),
            wspec((D_MODEL, 2 * D_FF)),
            wspec((D_FF, D_MODEL)),
            wspec((1, D_MODEL)), wspec((1, D_MODEL)),
        ],
        out_specs=[pl.BlockSpec((tm, D_MODEL), step2),
                   seq3],
        out_shape=[jax.ShapeDtypeStruct((rows, D_MODEL), F32),
                   jax.ShapeDtypeStruct(xs3d.shape, F32)],
        compiler_params=pltpu.CompilerParams(
            dimension_semantics=("arbitrary",),
            vmem_limit_bytes=VMEM_LIMIT),
        name="dense",
    )(x2d, mix2d, xs3d, mixs2d, wo, g1, b1, wfi, wfo, g2, b2)


def kernel(x_prompt, x_sample, state_mlstm_C, state_mlstm_n, state_mlstm_m, cache_swa_k, cache_swa_v,
           w_in, b_igate, b_fgate, ml_norm_g, swa_sinks, rel_bias, w_out, ln1_g, ln1_b,
           w_ffn_in, w_ffn_out, ln2_g, ln2_b):
    bp, seq, _ = x_prompt.shape
    bs, tdec, _ = x_sample.shape
    l = 0

    w16 = jnp.pad(w_in[l], ((0, 0), (0, W_IN_PAD - w_in.shape[-1]))).astype(BF16)
    gvec = jnp.concatenate([b_igate[l], b_fgate[l]]).astype(F32)
    gb = jnp.pad(gvec, (0, BLK - 2 * ML_HEADS))[None, :]
    gbc = jnp.broadcast_to(jnp.pad(gvec, (0, T_GROWS - 2 * ML_HEADS))[:, None], (T_GROWS, BLK))
    mlg = ml_norm_g[l][None, :].astype(F32)
    sinks = swa_sinks[l].astype(F32)
    rb = rel_bias.astype(F32)
    g1, b1 = ln1_g[l][None, :], ln1_b[l][None, :]
    g2, b2 = ln2_g[l][None, :], ln2_b[l][None, :]

    mix_p, c_p, n_p, m_p, k_p, v_p, wt, wn, wo, wfi, wfo = _prompt_mixer(
        x_prompt, w16, gbc, mlg, sinks, rb, w_out[l], w_ffn_in[l], w_ffn_out[l])
    p_m =m_p[:, :ML_HEADS // 2, ::BLK].reshape(bp, ML_HEADS)[None]
    p_k = k_p.reshape(1, bp, WINDOW, SWA_KV_HEADS, SWA_HD)
    p_v = v_p.reshape(1, bp, WINDOW, SWA_KV_HEADS, SWA_HD)

    m0 =state_mlstm_m[l].astype(F32)
    mrep = jnp.pad(jnp.repeat(m0, tdec, axis=0), ((0, 0), (ML_HEADS, BLK - 2 * ML_HEADS)))
    wlen = cache_swa_k.shape[2]
    kc = cache_swa_k[l].reshape(bs, wlen, SWA_KV_HEADS * SWA_HD)
    vc = cache_swa_v[l].reshape(bs, wlen, SWA_KV_HEADS * SWA_HD)
    mix_s, c_s, n_s, mo_s, k_s, v_s = _sample_mixer(
        x_sample, wt, wn, gb, mlg, sinks, rb, mrep,
        state_mlstm_C[l].astype(F32), state_mlstm_n[l].astype(F32), kc, vc)
    s_m =mo_s.reshape(bs, tdec, BLK)[:, 0, ML_HEADS:2 * ML_HEADS][None]
    s_k = k_s.reshape(1, bs, wlen, SWA_KV_HEADS, SWA_HD)
    s_v = v_s.reshape(1, bs, wlen, SWA_KV_HEADS, SWA_HD)

    y_p, y_s = _dense(x_prompt.reshape(bp * seq, D_MODEL), mix_p.reshape(bp * seq, D_MODEL), x_sample, mix_s,
                      wo, g1, b1, wfi, wfo, g2, b2)
    y_p = y_p.reshape(bp, seq, D_MODEL)

    return (y_p, y_s, c_p[None], n_p[None], p_m, p_k, p_v,
            c_s[None], n_s[None], s_m, s_k, s_v)
```
